```python
import jax, jax.numpy as jnp
from jax import lax
import numpy as np

D_MODEL = 2048
BATCH = 4
SEQ = 2048
DEPTH = 2

D_GROUP = D_MODEL // 2
HEAD_DIM = 128
N_MLSTM = D_GROUP // HEAD_DIM
RWKV_HEAD = 64
N_RWKV = D_GROUP // RWKV_HEAD
N_RET = D_GROUP // HEAD_DIM
N_GDN = D_GROUP // HEAD_DIM
RWKV_DECAY_RANK = max(32, int(round(1.8 * D_GROUP ** 0.5 / 32)) * 32)
RWKV_AAA_RANK = RWKV_DECAY_RANK
RWKV_GATE_RANK = max(32, int(round(0.6 * D_GROUP ** 0.8 / 32)) * 32)
CONV_WIDTH = 4
CHUNK = 64
D_FF = ((8 * D_MODEL + 767) // 768) * 256
ALPHA = (2 * DEPTH) ** 0.25
BETA = (8 * DEPTH) ** -0.25
LN_EPS = 1e-5
RWKV_LN_EPS = 64e-5
ROPE_BASE = 10000.0
RET_GAMMA_BASE = 5.0

A_SIZES = (D_GROUP, D_GROUP, D_GROUP, D_GROUP, N_MLSTM, N_MLSTM)
B_SIZES = (D_GROUP, D_GROUP, D_GROUP, RWKV_DECAY_RANK, RWKV_AAA_RANK, RWKV_GATE_RANK)
C_SIZES = (D_GROUP, D_GROUP, D_GROUP, D_GROUP)
D_SIZES = (D_GROUP, D_GROUP, D_GROUP, D_GROUP, N_GDN, N_GDN)
A_COLS = sum(A_SIZES)
B_COLS = sum(B_SIZES)
C_COLS = sum(C_SIZES)
D_COLS = sum(D_SIZES)

kernel_name = 'hybrid_mlstm_rwkv7_retnet_gdn_deepnorm_adaln'


def _split(z, sizes):
    return jnp.split(z, np.cumsum(sizes)[:-1].tolist(), axis=-1)


def _layer_norm(x, g, b):
    xf = x.astype(jnp.float32)
    mu = xf.mean(-1, keepdims=True)
    var = jnp.square(xf - mu).mean(-1, keepdims=True)
    return ((xf - mu) * lax.rsqrt(var + LN_EPS) * g + b).astype(x.dtype)


def _head_norm(h, g, b=None, eps=LN_EPS, center=True):
    hf = h.astype(jnp.float32)
    if center:
        hf = hf - hf.mean(-1, keepdims=True)
    hf = hf * lax.rsqrt(jnp.square(hf).mean(-1, keepdims=True) + eps)
    out = hf.reshape(h.shape[0], h.shape[1], -1) * g
    return out if b is None else out + b


def _l2norm(z):
    return z * lax.rsqrt(jnp.sum(jnp.square(z), -1, keepdims=True) + 1e-6)


def _token_shift(z):
    return jnp.pad(z[:, :-1], ((0, 0), (1, 0), (0, 0)))


def _causal_conv(z, w):
    return lax.conv_general_dilated(z, w[:, None, :].astype(z.dtype), window_strides=(1,),
                                    padding=[(w.shape[0] - 1, 0)],
                                    dimension_numbers=('NWC', 'WIO', 'NWC'),
                                    feature_group_count=z.shape[-1])


def _heads(z, n_heads):
    B, T, _ = z.shape
    return z.reshape(B, T, n_heads, -1).transpose(0, 2, 1, 3).astype(jnp.float32)


def _chunk(z):
    B, H, T = z.shape[:3]
    return jnp.moveaxis(z.reshape(B, H, T // CHUNK, CHUNK, *z.shape[3:]), 2, 0)


def _unchunk(z):
    nc, B, H, L, d = z.shape
    return jnp.moveaxis(z, 0, 2).reshape(B, H, nc * L, d).transpose(0, 2, 1, 3)


def _rotary(z, positions):
    half = z.shape[-1] // 2
    inv_freq = ROPE_BASE ** (-jnp.arange(half, dtype=jnp.float32) / half)
    ang = positions.astype(jnp.float32)[:, :, None, None] * inv_freq
    cos, sin = jnp.cos(ang), jnp.sin(ang)
    z1 = z[..., :half].astype(jnp.float32)
    z2 = z[..., half:].astype(jnp.float32)
    return jnp.concatenate([z1 * cos - z2 * sin, z1 * sin + z2 * cos], -1)


def _mlstm(q, k, v, o_pre, i_pre, f_pre, conv_w, gate_b, norm_g):
    B, T, _ = q.shape
    H, dh = N_MLSTM, HEAD_DIM
    qk = jax.nn.silu(_causal_conv(jnp.concatenate([q, k], -1), conv_w))
    q, k = jnp.split(qk, 2, axis=-1)
    qh, kh, vh = _heads(q, H), _heads(k, H) * dh ** -0.5, _heads(v, H)
    ig = (i_pre + gate_b[:H]).astype(jnp.float32).transpose(0, 2, 1)
    lf = jax.nn.log_sigmoid((f_pre + gate_b[H:]).astype(jnp.float32)).transpose(0, 2, 1)
    causal = jnp.tril(jnp.ones((CHUNK, CHUNK), bool))

    def step(carry, inp):
        C, n, m = carry
        qc, kc, vc, ic, fc = inp
        b = jnp.cumsum(fc, -1)
        dlog = jnp.where(causal, b[..., :, None] - b[..., None, :] + ic[..., None, :], -jnp.inf)
        inter = b + m[..., None]
        mt = jnp.maximum(dlog.max(-1), inter)
        s = jnp.einsum('bhtd,bhsd->bhts', qc, kc) * jnp.exp(dlog - mt[..., None])
        sc = jnp.exp(inter - mt)
        num = jnp.einsum('bhts,bhse->bhte', s, vc) + sc[..., None] * jnp.einsum('bhtd,bhde->bhte', qc, C)
        den = s.sum(-1) + sc * jnp.einsum('bhtd,bhd->bht', qc, n)
        h = num / jnp.maximum(jnp.abs(den), jnp.exp(-mt))[..., None]
        bl = b[..., -1]
        lw = bl[..., None] - b + ic
        m_new = jnp.maximum(bl + m, lw.max(-1))
        kw = kc * jnp.exp(lw - m_new[..., None])[..., None]
        dec = jnp.exp(bl + m - m_new)
        C = dec[..., None, None] * C + jnp.einsum('bhsd,bhse->bhde', kw, vc)
        n = dec[..., None] * n + kw.sum(-2)
        return (C, n, m_new), h

    init = (jnp.zeros((B, H, dh, dh), jnp.float32), jnp.zeros((B, H, dh), jnp.float32),
            jnp.zeros((B, H), jnp.float32))
    _, h = lax.scan(step, init, (_chunk(qh), _chunk(kh), _chunk(vh), _chunk(ig), _chunk(lf)))
    h = _head_norm(_unchunk(h), norm_g)
    return (h * jax.nn.sigmoid(o_pre.astype(jnp.float32))).astype(q.dtype)


def _rwkv7(z, mu, w0, w2, a0, a2, g2, k_k, k_a, r_k, ln_g, ln_b):
    B, T, _ = z.shape
    H, N = N_RWKV, RWKV_HEAD
    z = z + (_token_shift(z) - z) * mu
    r, k, v, wl, al, gl = _split(z, B_SIZES)
    w = -jax.nn.softplus(-(w0 + jnp.tanh(wl) @ w2).astype(jnp.float32)) - 0.5
    decay = jnp.exp(-jnp.exp(w))
    a = jax.nn.sigmoid((a0 + al @ a2).astype(jnp.float32))
    g = jax.nn.sigmoid(gl) @ g2
    kk = (k * k_k).astype(jnp.float32).reshape(B, T, H, N)
    kk = kk / jnp.maximum(jnp.linalg.norm(kk, axis=-1, keepdims=True), 1e-12)
    k = k.astype(jnp.float32) * (1.0 + (a - 1.0) * k_a)

    def hd(t):
        return t.astype(jnp.float32).reshape(B, T, H, N)

    rh, kh, vh, ah = hd(r), hd(k), hd(v), hd(a)

    def step(S, inp):
        rt, wt, kt, vt, at, bt = inp
        sa = jnp.einsum('bhij,bhj->bhi', S, at)
        S = S * wt[:, :, None, :] + sa[..., None] * bt[:, :, None, :] + vt[..., None] * kt[:, :, None, :]
        return S, jnp.einsum('bhij,bhj->bhi', S, rt)

    tm = lambda t: jnp.moveaxis(t, 1, 0)
    _, y = lax.scan(step, jnp.zeros((B, H, N, N), jnp.float32),
                    (tm(rh), tm(hd(decay)), tm(kh), tm(vh), tm(-kk), tm(kk * ah)))
    y = _head_norm(jnp.moveaxis(y, 0, 1), ln_g, ln_b, eps=RWKV_LN_EPS)
    bonus = ((rh * kh * r_k.reshape(H, N)).sum(-1, keepdims=True) * vh).reshape(B, T, -1)
    return ((y + bonus) * g).astype(z.dtype)


def _retention(q, k, v, g, positions, norm_g):
    B, T, _ = q.shape
    H, dh = N_RET, HEAD_DIM
    qh = _rotary(q.reshape(B, T, H, dh), positions).transpose(0, 2, 1, 3)
    kh = _rotary(k.reshape(B, T, H, dh), positions).transpose(0, 2, 1, 3) * dh ** -0.5
    vh = _heads(v, H)
    log_gamma = jnp.log1p(-jnp.exp2(-RET_GAMMA_BASE - jnp.arange(H, dtype=jnp.float32)))
    idx = jnp.arange(CHUNK, dtype=jnp.float32)
    causal = idx[:, None] >= idx[None, :]
    rel = jnp.where(causal, idx[:, None] - idx[None, :], 0.0)
    dmat = jnp.where(causal, jnp.exp(rel * log_gamma[:, None, None]), 0.0)
    zeta = jnp.exp((CHUNK - 1 - idx) * log_gamma[:, None])[:, :, None]
    xi = jnp.exp((idx + 1) * log_gamma[:, None])[:, :, None]
    chunk_decay = jnp.exp(CHUNK * log_gamma)[:, None, None]
    qc, kc, vc = _chunk(qh), _chunk(kh), _chunk(vh)
    intra = jnp.einsum('cbhts,cbhse->cbhte', jnp.einsum('cbhtd,cbhsd->cbhts', qc, kc) * dmat, vc)
    kv = jnp.einsum('cbhsd,cbhse->cbhde', kc * zeta, vc)

    def step(R, kv_c):
        return R * chunk_decay + kv_c, R

    _, r_prev = lax.scan(step, jnp.zeros((B, H, dh, dh), jnp.float32), kv)
    inter = jnp.einsum('cbhtd,cbhde->cbhte', qc, r_prev) * xi
    o = _head_norm(_unchunk(intra + inter), norm_g)
    return (o * jax.nn.silu(g.astype(jnp.float32))).astype(q.dtype)


def _gated_deltanet(q, k, v, z, a, b, conv_w, a_log, dt_bias, norm_g):
    B, T, _ = q.shape
    H, dh = N_GDN, HEAD_DIM
    f32 = jnp.float32
    qkv = jax.nn.silu(_causal_conv(jnp.concatenate([q, k, v], -1), conv_w))
    qs, ks, vs = jnp.split(qkv, 3, axis=-1)
    qh = _l2norm(_heads(qs, H)) * dh ** -0.5
    kh = _l2norm(_heads(ks, H))
    vh = _heads(vs, H)
    beta = jax.nn.sigmoid(b.astype(f32)).transpose(0, 2, 1)
    g = (-jnp.exp(a_log.astype(f32)) * jax.nn.softplus((a + dt_bias).astype(f32))).transpose(0, 2, 1)
    qc, kc, vc, bc = _chunk(qh), _chunk(kh), _chunk(vh), _chunk(beta)
    gc = jnp.cumsum(_chunk(g), -1)
    causal = jnp.tril(jnp.ones((CHUNK, CHUNK), bool))
    strict = jnp.tril(jnp.ones((CHUNK, CHUNK), bool), -1)
    gamma = jnp.exp(jnp.where(causal, gc[..., :, None] - gc[..., None, :], -jnp.inf))
    kb = kc * bc[..., None]
    ut = jnp.eye(CHUNK, dtype=f32) + jnp.where(strict, jnp.einsum('cbhtd,cbhsd->cbhts', kb, kc) * gamma, 0.0)
    u = lax.linalg.triangular_solve(ut, vc * bc[..., None], left_side=True, lower=True)
    w = lax.linalg.triangular_solve(ut, kb * jnp.exp(gc)[..., None], left_side=True, lower=True)
    att = jnp.einsum('cbhtd,cbhsd->cbhts', qc, kc) * gamma

    def step(S, inp):
        qi, ki, ui, wi, ai, gi = inp
        v_new = ui - jnp.einsum('bhtd,bhde->bhte', wi, S)
        o = (jnp.einsum('bhtd,bhde->bhte', qi * jnp.exp(gi)[..., None], S)
             + jnp.einsum('bhts,bhse->bhte', ai, v_new))
        g_last = gi[..., -1]
        S = (S * jnp.exp(g_last)[..., None, None]
             + jnp.einsum('bhsd,bhse->bhde', ki * jnp.exp(g_last[..., None] - gi)[..., None], v_new))
        return S, o

    _, o = lax.scan(step, jnp.zeros((B, H, dh, dh), f32), (qc, kc, u, w, att, gc))
    o = _head_norm(_unchunk(o), norm_g, eps=1e-6, center=False)
    return (o * jax.nn.silu(z.astype(f32))).astype(q.dtype)


def _mixer_ab(h, w_in, w_out, conv_w, gate_b, m_norm_g, mu, w0, w2, a0, a2, g2, k_k, k_a, r_k, r_ln_g, r_ln_b):
    proj = h @ w_in
    q, k, v, o, ig, fg = _split(proj[..., :A_COLS], A_SIZES)
    ya = _mlstm(q, k, v, o, ig, fg, conv_w, gate_b, m_norm_g)
    yb = _rwkv7(proj[..., A_COLS:], mu, w0, w2, a0, a2, g2, k_k, k_a, r_k, r_ln_g, r_ln_b)
    return jnp.concatenate([ya, yb], -1) @ w_out


def _mixer_cd(h, positions, w_in, w_out, ret_norm_g, conv_w, a_log, dt_bias, gdn_norm_g):
    proj = h @ w_in
    q, k, v, g = _split(proj[..., :C_COLS], C_SIZES)
    yc = _retention(q, k, v, g, positions, ret_norm_g)
    q, k, v, z, a, b = _split(proj[..., C_COLS:], D_SIZES)
    yd = _gated_deltanet(q, k, v, z, a, b, conv_w, a_log, dt_bias, gdn_norm_g)
    return jnp.concatenate([yc, yd], -1) @ w_out


def _ada(c, w, b):
    mod = jax.nn.silu(c) @ w + b
    shift, scale, gate = jnp.split(mod, 3, axis=-1)
    return shift[:, None], scale[:, None], gate[:, None]


def _swiglu(h, wg, wu, wd):
    return (jax.nn.silu(h @ wg) * (h @ wu)) @ wd


def setup_inputs(seed: int = 0) -> dict:
    key = jax.random.key(seed)
    ks = iter(jax.random.split(key, 48))
    f32 = jnp.float32
    NE, NO = (DEPTH + 1) // 2, DEPTH // 2

    def nrm(shape, scale):
        return jax.random.normal(next(ks), shape, f32) * scale

    def unif(shape, lo, hi):
        return jax.random.uniform(next(ks), shape, f32, lo, hi)

    x = nrm((BATCH, SEQ, D_MODEL), 1.0)
    c = nrm((BATCH, D_MODEL), 1.0)
    positions = (jax.random.randint(next(ks), (BATCH, 1), 0, 4096, jnp.int32)
                 + jnp.arange(SEQ, dtype=jnp.int32)[None, :])
    ada_w = nrm((DEPTH, 2, D_MODEL, 3 * D_MODEL), 0.1 * D_MODEL ** -0.5)
    ada_b = nrm((DEPTH, 2, 3 * D_MODEL), 0.02)
    ln_g = 1.0 + nrm((DEPTH, 2, D_MODEL), 0.02)
    ln_b = nrm((DEPTH, 2, D_MODEL), 0.02)
    ab_w_in = nrm((NE, D_MODEL, A_COLS + B_COLS), D_MODEL ** -0.5)
    ab_w_out = nrm((NE, 2 * D_GROUP, D_MODEL), BETA * (2 * D_GROUP) ** -0.5)
    mlstm_conv_w = nrm((NE, CONV_WIDTH, 2 * D_GROUP), CONV_WIDTH ** -0.5)
    mlstm_gate_b = jnp.concatenate([nrm((NE, N_MLSTM), 0.1),
                                    jnp.linspace(3.0, 6.0, N_MLSTM, dtype=f32) + nrm((NE, N_MLSTM), 0.1)], -1)
    mlstm_norm_g = 1.0 + nrm((NE, D_GROUP), 0.02)
    rwkv_mu = unif((NE, B_COLS), 0.0, 1.0)
    rwkv_w0 = unif((NE, D_GROUP), -6.0, -1.0)
    rwkv_w2 = nrm((NE, RWKV_DECAY_RANK, D_GROUP), 0.1 * RWKV_DECAY_RANK ** -0.5)
    rwkv_a0 = nrm((NE, D_GROUP), 0.1)
    rwkv_a2 = nrm((NE, RWKV_AAA_RANK, D_GROUP), RWKV_AAA_RANK ** -0.5)
    rwkv_g2 = nrm((NE, RWKV_GATE_RANK, D_GROUP), RWKV_GATE_RANK ** -0.5)
    rwkv_k_k = 0.85 + nrm((NE, D_GROUP), 0.05)
    rwkv_k_a = 1.0 + nrm((NE, D_GROUP), 0.05)
    rwkv_r_k = nrm((NE, D_GROUP), 0.1)
    rwkv_ln_g = 1.0 + nrm((NE, D_GROUP), 0.02)
    rwkv_ln_b = nrm((NE, D_GROUP), 0.02)
    cd_w_in = nrm((NO, D_MODEL, C_COLS + D_COLS), D_MODEL ** -0.5)
    cd_w_out = nrm((NO, 2 * D_GROUP, D_MODEL), BETA * (2 * D_GROUP) ** -0.5)
    ret_norm_g = 1.0 + nrm((NO, D_GROUP), 0.02)
    gdn_conv_w = nrm((NO, CONV_WIDTH, 3 * D_GROUP), CONV_WIDTH ** -0.5)
    gdn_a_log = jnp.log(unif((NO, N_GDN), 1.0, 16.0))
    dt = jnp.exp(unif((NO, N_GDN), -6.9, -2.3))
    gdn_dt_bias = dt + jnp.log(-jnp.expm1(-dt))
    gdn_norm_g = 1.0 + nrm((NO, D_GROUP), 0.02)
    ffn_w_gate = nrm((DEPTH, D_MODEL, D_FF), D_MODEL ** -0.5)
    ffn_w_up = nrm((DEPTH, D_MODEL, D_FF), D_MODEL ** -0.5)
    ffn_w_down = nrm((DEPTH, D_FF, D_MODEL), BETA * D_FF ** -0.5)
    return {'x': x, 'c': c, 'positions': positions,
            'ada_w': ada_w, 'ada_b': ada_b, 'ln_g': ln_g, 'ln_b': ln_b,
            'ab_w_in': ab_w_in, 'ab_w_out': ab_w_out, 'mlstm_conv_w': mlstm_conv_w,
            'mlstm_gate_b': mlstm_gate_b, 'mlstm_norm_g': mlstm_norm_g,
            'rwkv_mu': rwkv_mu, 'rwkv_w0': rwkv_w0, 'rwkv_w2': rwkv_w2, 'rwkv_a0': rwkv_a0,
            'rwkv_a2': rwkv_a2, 'rwkv_g2': rwkv_g2, 'rwkv_k_k': rwkv_k_k, 'rwkv_k_a': rwkv_k_a,
            'rwkv_r_k': rwkv_r_k, 'rwkv_ln_g': rwkv_ln_g, 'rwkv_ln_b': rwkv_ln_b,
            'cd_w_in': cd_w_in, 'cd_w_out': cd_w_out, 'ret_norm_g': ret_norm_g,
            'gdn_conv_w': gdn_conv_w, 'gdn_a_log': gdn_a_log, 'gdn_dt_bias': gdn_dt_bias,
            'gdn_norm_g': gdn_norm_g,
            'ffn_w_gate': ffn_w_gate, 'ffn_w_up': ffn_w_up, 'ffn_w_down': ffn_w_down}


def reference(x, c, positions, ada_w, ada_b, ln_g, ln_b,
              ab_w_in, ab_w_out, mlstm_conv_w, mlstm_gate_b, mlstm_norm_g,
              rwkv_mu, rwkv_w0, rwkv_w2, rwkv_a0, rwkv_a2, rwkv_g2, rwkv_k_k, rwkv_k_a,
              rwkv_r_k, rwkv_ln_g, rwkv_ln_b,
              cd_w_in, cd_w_out, ret_norm_g, gdn_conv_w, gdn_a_log, gdn_dt_bias, gdn_norm_g,
              ffn_w_gate, ffn_w_up, ffn_w_down):
    for layer in range(DEPTH):
        j = layer // 2
        shift, scale, gate = _ada(c, ada_w[layer, 0], ada_b[layer, 0])
        h = x * (1.0 + scale) + shift
        if layer % 2 == 0:
            y = _mixer_ab(h, ab_w_in[j], ab_w_out[j], mlstm_conv_w[j], mlstm_gate_b[j], mlstm_norm_g[j],
                          rwkv_mu[j], rwkv_w0[j], rwkv_w2[j], rwkv_a0[j], rwkv_a2[j], rwkv_g2[j],
                          rwkv_k_k[j], rwkv_k_a[j], rwkv_r_k[j], rwkv_ln_g[j], rwkv_ln_b[j])
        else:
            y = _mixer_cd(h, positions, cd_w_in[j], cd_w_out[j], ret_norm_g[j], gdn_conv_w[j],
                          gdn_a_log[j], gdn_dt_bias[j], gdn_norm_g[j])
        x = _layer_norm(ALPHA * x + (1.0 + gate) * y, ln_g[layer, 0], ln_b[layer, 0])
        shift, scale, gate = _ada(c, ada_w[layer, 1], ada_b[layer, 1])
        y = _swiglu(x * (1.0 + scale) + shift, ffn_w_gate[layer], ffn_w_up[layer], ffn_w_down[layer])
        x = _layer_norm(ALPHA * x + (1.0 + gate) * y, ln_g[layer, 1], ln_b[layer, 1])
    return x
```

```python
import functools
import math

import numpy as np
import jax
import jax.numpy as jnp
from jax import lax
from jax.experimental import pallas as pl
from jax.experimental.pallas import tpu as pltpu

F32 = jnp.float32
BF16 = jnp.bfloat16
HI = lax.Precision.HIGHEST

D_MODEL = 2048
D_GROUP = 1024
HEAD_DIM = 128
N_HEADS = 8
RWKV_HEAD = 64
N_RWKV = 16
CHUNK = 64
D_FF = 5632
DEPTH = 2
ALPHA = (2 * DEPTH) ** 0.25
LN_EPS = 1e-5
RWKV_LN_EPS = 64e-5
ROPE_BASE = 10000.0
RET_GAMMA_BASE = 5.0
LANES = 128
VMEM_LIMIT = 48 * 1024 * 1024


def _dot(a, b, prec=HI):
    return jnp.dot(a, b, precision=prec, preferred_element_type=F32)


def _dot_nt(a, b, prec=HI):
    return lax.dot_general(a, b, (((1,), (1,)), ((), ())), precision=prec, preferred_element_type=F32)


def _sigmoid(x):
    return 1.0 / (1.0 + jnp.exp(-x))


def _silu(x):
    return x * _sigmoid(x)


def _log_sigmoid(x):
    return jnp.minimum(x, 0.0) - jnp.log1p(jnp.exp(-jnp.abs(x)))


def _softplus(x):
    return jnp.maximum(x, 0.0) + jnp.log1p(jnp.exp(-jnp.abs(x)))


def _tri(n, strict=False):
    r = lax.broadcasted_iota(jnp.int32, (n, n), 0)
    c = lax.broadcasted_iota(jnp.int32, (n, n), 1)
    return (r > c) if strict else (r >= c)


def _conv_silu_rows(src_ref, halo_ref, w, first, r0, nrows, c0, ncols):
    cur = src_ref[0, r0:r0 + nrows, c0:c0 + ncols]
    acc = w[3:4] * cur
    if r0 == 0:
        hl = jnp.where(first, 0.0, halo_ref[0, :, c0:c0 + ncols])
        ext = jnp.concatenate([hl, cur[0:8]], axis=0)
        for j in range(3):
            head = ext[5 + j:13 + j]
            if nrows > 8:
                rest = src_ref[0, 5 + j:nrows - 3 + j, c0:c0 + ncols]
                sh = jnp.concatenate([head, rest], axis=0)
            else:
                sh = head
            acc = acc + w[j:j + 1] * sh
    else:
        for j in range(3):
            acc = acc + w[j:j + 1] * src_ref[0, r0 - 3 + j:r0 - 3 + j + nrows, c0:c0 + ncols]
    return _silu(acc)


def _head_norm_rows(h, g_row, eps, center=True):
    if center:
        h = h - jnp.mean(h, axis=-1, keepdims=True)
    return h * lax.rsqrt(jnp.mean(h * h, axis=-1, keepdims=True) + eps) * g_row


def _mlstm_kernel(q_ref, k_ref, v_ref, o_ref, g_ref, qh_ref, kh_ref, cw_ref, gb_ref, ng_ref,
                  out_ref, qc_ref, kc_ref, C_ref, n_ref, m_ref, *, tb):
    t = pl.program_id(1)
    first = t == 0

    @pl.when(first)
    def _():
        C_ref[...] = jnp.zeros_like(C_ref)
        n_ref[...] = jnp.zeros_like(n_ref)
        m_ref[...] = jnp.zeros_like(m_ref)

    for c in range(tb // CHUNK):
        for cb in range(D_GROUP // 256):
            cs = cb * 256
            qc_ref[c * CHUNK:(c + 1) * CHUNK, cs:cs + 256] = _conv_silu_rows(
                q_ref, qh_ref, cw_ref[:, cs:cs + 256], first, c * CHUNK, CHUNK, cs, 256)
            kc_ref[c * CHUNK:(c + 1) * CHUNK, cs:cs + 256] = _conv_silu_rows(
                k_ref, kh_ref, cw_ref[:, D_GROUP + cs:D_GROUP + cs + 256], first, c * CHUNK, CHUNK, cs, 256)

    causal = _tri(CHUNK)
    tril = causal.astype(F32)
    gb = gb_ref[...]
    scale = HEAD_DIM ** -0.5

    def chunk_body(c, carry):
        r0 = pl.multiple_of(c * CHUNK, CHUNK)
        rows = pl.ds(r0, CHUNK)
        z = g_ref[0, rows, :] + gb
        bb = _dot(tril, _log_sigmoid(z))
        zT = z.T
        bT = bb.T
        for h in range(N_HEADS):
            hs = slice(h * HEAD_DIM, (h + 1) * HEAD_DIM)
            ig_col, ig_row = z[:, h:h + 1], zT[h:h + 1, :]
            b_col, b_row = bb[:, 8 + h:9 + h], bT[8 + h:9 + h, :]
            m = m_ref[h:h + 1, 0:1]
            q = qc_ref[rows, hs]
            k = kc_ref[rows, hs] * scale
            v = v_ref[0, rows, hs]
            Cst = C_ref[h]
            nst = n_ref[h:h + 1, :]
            dlog = jnp.where(causal, b_col - b_row + ig_row, -1e30)
            inter = b_col + m
            mt = jnp.maximum(jnp.max(dlog, axis=-1, keepdims=True), inter)
            p = jnp.where(causal, jnp.exp(dlog - mt), 0.0)
            s = _dot_nt(q, k) * p
            sc = jnp.exp(inter - mt)
            num = _dot(s, v) + sc * _dot(q, Cst)
            den = jnp.sum(s, axis=-1, keepdims=True) + sc * jnp.sum(q * nst, axis=-1, keepdims=True)
            hh = num / jnp.maximum(jnp.abs(den), jnp.exp(-mt))
            bl = bb[CHUNK - 1:CHUNK, 8 + h:9 + h]
            lw = bl - b_col + ig_col
            m_new = jnp.maximum(bl + m, jnp.max(lw, axis=0, keepdims=True))
            kw = k * jnp.exp(lw - m_new)
            dec = jnp.exp(bl + m - m_new)
            C_ref[h] = dec * Cst + _dot(kw.T, v)
            n_ref[h:h + 1, :] = dec * nst + jnp.sum(kw, axis=0, keepdims=True)
            m_ref[h:h + 1, :] = jnp.broadcast_to(m_new, (1, LANES))
            hn = _head_norm_rows(hh, ng_ref[:, hs], LN_EPS)
            out_ref[0, rows, hs] = (hn * _sigmoid(o_ref[0, rows, hs])).astype(out_ref.dtype)
        return carry

    lax.fori_loop(0, tb // CHUNK, chunk_body, 0)


def _mlstm_call(proj, conv_w, gate_b, norm_g, *, gate_col, tb=256):
    B, T, _ = proj.shape
    nt = T // tb
    gb = jnp.zeros((1, LANES), F32).at[0, :2 * N_HEADS].set(gate_b)
    ng = norm_g.reshape(1, D_GROUP)
    colblk = lambda j: pl.BlockSpec((1, tb, D_GROUP), lambda b, t: (b, t, j))
    halo = lambda j: pl.BlockSpec((1, 8, D_GROUP), lambda b, t: (b, jnp.maximum(t * (tb // 8) - 1, 0), j))
    return pl.pallas_call(
        functools.partial(_mlstm_kernel, tb=tb),
        grid=(B, nt),
        in_specs=[colblk(0), colblk(1), colblk(2), colblk(3),
                  pl.BlockSpec((1, tb, LANES), lambda b, t: (b, t, gate_col // LANES)),
                  halo(0), halo(1),
                  pl.BlockSpec((4, 2 * D_GROUP), lambda b, t: (0, 0)),
                  pl.BlockSpec((1, LANES), lambda b, t: (0, 0)),
                  pl.BlockSpec((1, D_GROUP), lambda b, t: (0, 0))],
        out_specs=pl.BlockSpec((1, tb, D_GROUP), lambda b, t: (b, t, 0)),
        out_shape=jax.ShapeDtypeStruct((B, T, D_GROUP), BF16),
        scratch_shapes=[pltpu.VMEM((tb, D_GROUP), F32), pltpu.VMEM((tb, D_GROUP), F32),
                        pltpu.VMEM((N_HEADS, HEAD_DIM, HEAD_DIM), F32),
                        pltpu.VMEM((N_HEADS, LANES), F32), pltpu.VMEM((N_HEADS, LANES), F32)],
        compiler_params=pltpu.CompilerParams(dimension_semantics=("arbitrary", "arbitrary"),
                                             vmem_limit_bytes=VMEM_LIMIT),
        name="mlstm",
    )(proj, proj, proj, proj, proj, proj, proj, conv_w, gb, ng)


def _rope_kernel(pos_ref, inv_ref, cos_ref, sin_ref):
    ang = pos_ref[0].astype(F32) * inv_ref[...]
    lane = lax.broadcasted_iota(jnp.int32, ang.shape, 1)
    cos_ref[0] = jnp.cos(ang)
    sin_ref[0] = jnp.where(lane < HEAD_DIM // 2, -jnp.sin(ang), jnp.sin(ang))


def _rope_call(positions, *, tb=512):
    B, T = positions.shape
    half = HEAD_DIM // 2
    inv_freq = ROPE_BASE ** (-jnp.arange(half, dtype=F32) / half)
    inv2 = jnp.concatenate([inv_freq, inv_freq]).reshape(1, HEAD_DIM)
    spec = pl.BlockSpec((1, tb, HEAD_DIM), lambda b, t: (b, t, 0))
    return pl.pallas_call(
        _rope_kernel,
        grid=(B, T // tb),
        in_specs=[pl.BlockSpec((1, tb, 1), lambda b, t: (b, t, 0)),
                  pl.BlockSpec((1, HEAD_DIM), lambda b, t: (0, 0))],
        out_specs=[spec, spec],
        out_shape=[jax.ShapeDtypeStruct((B, T, HEAD_DIM), F32)] * 2,
        compiler_params=pltpu.CompilerParams(dimension_semantics=("arbitrary", "arbitrary")),
        name="rope_table",
    )(positions.reshape(B, T, 1), inv2)


def _ret_kernel(q_ref, k_ref, v_ref, g_ref, cos_ref, sin_ref, ng_ref, out_ref, R_ref, *, tb):
    t = pl.program_id(1)

    @pl.when(t == 0)
    def _():
        R_ref[...] = jnp.zeros_like(R_ref)

    causal = _tri(CHUNK)
    ri = lax.broadcasted_iota(jnp.int32, (CHUNK, CHUNK), 0)
    ci = lax.broadcasted_iota(jnp.int32, (CHUNK, CHUNK), 1)
    rel = (ri - ci).astype(F32)
    tcol = lax.broadcasted_iota(jnp.int32, (CHUNK, 1), 0).astype(F32)
    scale = HEAD_DIM ** -0.5

    def chunk_body(c, carry):
        r0 = pl.multiple_of(c * CHUNK, CHUNK)
        rows = pl.ds(r0, CHUNK)
        cos2 = cos_ref[0, rows, :]
        sin2 = sin_ref[0, rows, :]
        for h in range(N_HEADS):
            hs = slice(h * HEAD_DIM, (h + 1) * HEAD_DIM)
            lg = math.log1p(-2.0 ** (-RET_GAMMA_BASE - h))
            q = q_ref[0, rows, hs]
            k = k_ref[0, rows, hs]
            v = v_ref[0, rows, hs]
            qr = q * cos2 + pltpu.roll(q, HEAD_DIM // 2, 1) * sin2
            kr = (k * cos2 + pltpu.roll(k, HEAD_DIM // 2, 1) * sin2) * scale
            dmat = jnp.where(causal, jnp.exp(rel * lg), 0.0)
            xi = jnp.exp((tcol + 1.0) * lg)
            zeta = jnp.exp((CHUNK - 1.0 - tcol) * lg)
            Rst = R_ref[h]
            intra = _dot(_dot_nt(qr, kr) * dmat, v)
            inter = _dot(qr, Rst) * xi
            R_ref[h] = Rst * math.exp(CHUNK * lg) + _dot((kr * zeta).T, v)
            on = _head_norm_rows(intra + inter, ng_ref[:, hs], LN_EPS)
            out_ref[0, rows, hs] = (on * _silu(g_ref[0, rows, hs])).astype(out_ref.dtype)
        return carry

    lax.fori_loop(0, tb // CHUNK, chunk_body, 0)


def _ret_call(proj, cos2, sin2, norm_g, *, tb=256):
    B, T, _ = proj.shape
    colblk = lambda j: pl.BlockSpec((1, tb, D_GROUP), lambda b, t: (b, t, j))
    tab = pl.BlockSpec((1, tb, HEAD_DIM), lambda b, t: (b, t, 0))
    return pl.pallas_call(
        functools.partial(_ret_kernel, tb=tb),
        grid=(B, T // tb),
        in_specs=[colblk(0), colblk(1), colblk(2), colblk(3), tab, tab,
                  pl.BlockSpec((1, D_GROUP), lambda b, t: (0, 0))],
        out_specs=pl.BlockSpec((1, tb, D_GROUP), lambda b, t: (b, t, 0)),
        out_shape=jax.ShapeDtypeStruct((B, T, D_GROUP), BF16),
        scratch_shapes=[pltpu.VMEM((N_HEADS, HEAD_DIM, HEAD_DIM), F32)],
        compiler_params=pltpu.CompilerParams(dimension_semantics=("arbitrary", "arbitrary"),
                                             vmem_limit_bytes=VMEM_LIMIT),
        name="retention",
    )(proj, proj, proj, proj, cos2, sin2, norm_g.reshape(1, D_GROUP))


def _inv_unit_lower(nm):
    n = nm.shape[0]
    eye = (lax.broadcasted_iota(jnp.int32, (n, n), 0) == lax.broadcasted_iota(jnp.int32, (n, n), 1)).astype(F32)
    p = eye + nm
    x = nm
    for _ in range(int(math.log2(n)) - 1):
        x = _dot(x, x)
        p = p + _dot(p, x)
    return p


def _l2norm_rows(z):
    return z * lax.rsqrt(jnp.sum(z * z, axis=-1, keepdims=True) + 1e-6)


def _gdn_kernel(q_ref, k_ref, v_ref, z_ref, g_ref, qh_ref, kh_ref, vh_ref, cw_ref, an_ref, dt_ref, ng_ref,
                out_ref, qc_ref, kc_ref, vc_ref, S_ref, *, tb):
    t = pl.program_id(1)
    first = t == 0

    @pl.when(first)
    def _():
        S_ref[...] = jnp.zeros_like(S_ref)

    srcs = ((q_ref, qh_ref, qc_ref), (k_ref, kh_ref, kc_ref), (v_ref, vh_ref, vc_ref))
    for c in range(tb // CHUNK):
        for cb in range(D_GROUP // 256):
            cs = cb * 256
            for i, (src, halo, dst) in enumerate(srcs):
                w = cw_ref[:, i * D_GROUP + cs:i * D_GROUP + cs + 256]
                dst[c * CHUNK:(c + 1) * CHUNK, cs:cs + 256] = _conv_silu_rows(
                    src, halo, w, first, c * CHUNK, CHUNK, cs, 256)

    causal = _tri(CHUNK)
    strict = _tri(CHUNK, strict=True)
    tril = causal.astype(F32)
    a_neg = an_ref[...]
    dtb = dt_ref[...]
    scale = HEAD_DIM ** -0.5

    def chunk_body(c, carry):
        r0 = pl.multiple_of(c * CHUNK, CHUNK)
        rows = pl.ds(r0, CHUNK)
        gz = g_ref[0, rows, :]
        beta = _sigmoid(gz)
        gc = _dot(tril, a_neg * _softplus(gz + dtb))
        gcT = gc.T
        for h in range(N_HEADS):
            hs = slice(h * HEAD_DIM, (h + 1) * HEAD_DIM)
            gc_col, gc_row = gc[:, h:h + 1], gcT[h:h + 1, :]
            b_col = beta[:, 8 + h:9 + h]
            gamma = jnp.where(causal, jnp.exp(gc_col - gc_row), 0.0)
            q = _l2norm_rows(qc_ref[rows, hs]) * scale
            k = _l2norm_rows(kc_ref[rows, hs])
            v = vc_ref[rows, hs]
            kb = k * b_col
            eg = jnp.exp(gc_col)
            a_mat = jnp.where(strict, _dot_nt(kb, k) * gamma, 0.0)
            inv = _inv_unit_lower(-a_mat)
            u = _dot(inv, v * b_col)
            w = _dot(inv, kb * eg)
            att = _dot_nt(q, k) * gamma
            S = S_ref[h]
            v_new = u - _dot(w, S)
            o = _dot(q * eg, S) + _dot(att, v_new)
            g_last = gc[CHUNK - 1:CHUNK, h:h + 1]
            S_ref[h] = S * jnp.exp(g_last) + _dot((k * jnp.exp(g_last - gc_col)).T, v_new)
            on = _head_norm_rows(o, ng_ref[:, hs], 1e-6, center=False)
            out_ref[0, rows, hs] = (on * _silu(z_ref[0, rows, hs])).astype(out_ref.dtype)
        return carry

    lax.fori_loop(0, tb // CHUNK, chunk_body, 0)


def _gdn_call(proj, conv_w, a_log, dt_bias, norm_g, *, col0, tb=256):
    B, T, _ = proj.shape
    j0 = col0 // D_GROUP
    an = jnp.zeros((1, LANES), F32).at[0, :N_HEADS].set(-jnp.exp(a_log.astype(F32)))
    dtb = jnp.zeros((1, LANES), F32).at[0, :N_HEADS].set(dt_bias)
    colblk = lambda j: pl.BlockSpec((1, tb, D_GROUP), lambda b, t: (b, t, j0 + j))
    halo = lambda j: pl.BlockSpec((1, 8, D_GROUP), lambda b, t: (b, jnp.maximum(t * (tb // 8) - 1, 0), j0 + j))
    return pl.pallas_call(
        functools.partial(_gdn_kernel, tb=tb),
        grid=(B, T // tb),
        in_specs=[colblk(0), colblk(1), colblk(2), colblk(3),
                  pl.BlockSpec((1, tb, LANES), lambda b, t: (b, t, (col0 + 4 * D_GROUP) // LANES)),
                  halo(0), halo(1), halo(2),
                  pl.BlockSpec((4, 3 * D_GROUP), lambda b, t: (0, 0)),
                  pl.BlockSpec((1, LANES), lambda b, t: (0, 0)),
                  pl.BlockSpec((1, LANES), lambda b, t: (0, 0)),
                  pl.BlockSpec((1, D_GROUP), lambda b, t: (0, 0))],
        out_specs=pl.BlockSpec((1, tb, D_GROUP), lambda b, t: (b, t, 0)),
        out_shape=jax.ShapeDtypeStruct((B, T, D_GROUP), BF16),
        scratch_shapes=[pltpu.VMEM((tb, D_GROUP), F32), pltpu.VMEM((tb, D_GROUP), F32),
                        pltpu.VMEM((tb, D_GROUP), F32),
                        pltpu.VMEM((N_HEADS, HEAD_DIM, HEAD_DIM), F32)],
        compiler_params=pltpu.CompilerParams(dimension_semantics=("arbitrary", "arbitrary"),
                                             vmem_limit_bytes=VMEM_LIMIT),
        name="gdn",
    )(proj, proj, proj, proj, proj, proj, proj, proj, conv_w, an, dtb, norm_g.reshape(1, D_GROUP))


N_PAIRS = N_RWKV // 2
RWKV_LOW = 384


def _shift1_rows(src_ref, halo_ref, first, r0, nrows, c0, ncols):
    if r0 == 0:
        hl = jnp.where(first, 0.0, halo_ref[0, 7:8, c0:c0 + ncols])
        return jnp.concatenate([hl, src_ref[0, 0:nrows - 1, c0:c0 + ncols]], axis=0)
    return src_ref[0, r0 - 1:r0 - 1 + nrows, c0:c0 + ncols]


def _rwkv_kernel(r_ref, k_ref, v_ref, l0_ref, l1_ref, l2_ref,
                 rh_ref, kh_ref, vh_ref, l0h_ref, l1h_ref, l2h_ref,
                 mu_ref, mul_ref, w0_ref, w2_ref, a0_ref, a2_ref, g2_ref, kk_ref, ka_ref, rk_ref,
                 lng_ref, lnb_ref, out_ref,
                 rs_ref, ks_ref, vs_ref, lw_ref, an_ref, bn_ref, gs_ref, bo_ref, y_ref, S_ref, *, tb):
    t = pl.program_id(1)
    first = t == 0

    @pl.when(first)
    def _():
        S_ref[...] = jnp.zeros_like(S_ref)

    ri = lax.broadcasted_iota(jnp.int32, (LANES, LANES), 0)
    ci = lax.broadcasted_iota(jnp.int32, (LANES, LANES), 1)
    same_head = ((ri // RWKV_HEAD) == (ci // RWKV_HEAD)).astype(F32)

    def lerp(src, halo, mu, r0, c0, ncols):
        cur = src[0, r0:r0 + CHUNK, c0:c0 + ncols]
        return cur + (_shift1_rows(src, halo, first, r0, CHUNK, c0, ncols) - cur) * mu

    for c in range(tb // CHUNK):
        r0 = c * CHUNK
        wl = lerp(l0_ref, l0h_ref, mul_ref[:, 0:LANES], r0, 0, LANES)
        g1 = lerp(l1_ref, l1h_ref, mul_ref[:, LANES:2 * LANES], r0, 0, LANES)
        g2 = lerp(l2_ref, l2h_ref, mul_ref[:, 2 * LANES:3 * LANES], r0, 0, LANES)
        lane = lax.broadcasted_iota(jnp.int32, wl.shape, 1)
        wl_t = jnp.where(lane < 64, jnp.tanh(wl), 0.0)
        al = jnp.where(lane >= 64, wl, 0.0)
        sg1 = _sigmoid(g1)
        sg2 = jnp.where(lane < 32, _sigmoid(g2), 0.0)
        for p in range(N_PAIRS):
            ps = slice(p * LANES, (p + 1) * LANES)
            wpre = w0_ref[:, ps] + _dot(wl_t, w2_ref[:, ps])
            w = -_softplus(-wpre) - 0.5
            a = _sigmoid(a0_ref[:, ps] + _dot(al, a2_ref[:, ps]))
            g = _dot(sg1, g2_ref[0:LANES, ps]) + _dot(sg2, g2_ref[LANES:2 * LANES, ps])
            r = lerp(r_ref, rh_ref, mu_ref[:, ps], r0, p * LANES, LANES)
            k = lerp(k_ref, kh_ref, mu_ref[:, D_GROUP + p * LANES:D_GROUP + (p + 1) * LANES], r0, p * LANES, LANES)
            v = lerp(v_ref, vh_ref, mu_ref[:, 2 * D_GROUP + p * LANES:2 * D_GROUP + (p + 1) * LANES], r0, p * LANES, LANES)
            kk = k * kk_ref[:, ps]
            nrm = jnp.sqrt(_dot(kk * kk, same_head))
            kk = kk / jnp.maximum(nrm, 1e-12)
            k2 = k * (1.0 + (a - 1.0) * ka_ref[:, ps])
            rows = slice(r0, r0 + CHUNK)
            rs_ref[p, rows, :] = r
            ks_ref[p, rows, :] = k2
            vs_ref[p, rows, :] = v
            lw_ref[p, rows, :] = -jnp.exp(w)
            an_ref[p, rows, :] = -kk
            bn_ref[p, rows, :] = kk * a
            gs_ref[p, rows, :] = g
            bo_ref[p, rows, :] = _dot(r * k2 * rk_ref[:, ps], same_head) * v

    causal = _tri(CHUNK)
    strict = _tri(CHUNK, strict=True)
    tril = causal.astype(F32)
    lane1 = lax.broadcasted_iota(jnp.int32, (1, LANES), 1)
    m0 = (lane1 < RWKV_HEAD).astype(F32)
    m1 = 1.0 - m0
    lo = lax.broadcasted_iota(jnp.int32, (CHUNK, LANES), 1) < RWKV_HEAD

    def pair_step(p, rows):
        lw = lw_ref[p, rows, :]
        cs = _dot(tril, lw)
        w_in = jnp.exp(cs)
        w_inv = jnp.exp(-cs)
        rt = rs_ref[p, rows, :] * w_in
        at = an_ref[p, rows, :] * jnp.exp(cs - lw)
        bt = bn_ref[p, rows, :] * w_inv
        kt = ks_ref[p, rows, :] * w_inv
        v = vs_ref[p, rows, :]
        S = S_ref[p]
        lhs = jnp.concatenate([at * m0, at * m1, rt * m0, rt * m1], axis=0)
        pb = _dot_nt(lhs, bt)
        pk = _dot_nt(lhs, kt)
        xs, ys = [], []
        x0 = _dot_nt(at, S)
        for hh in range(2):
            a_ab = jnp.where(strict, pb[hh * CHUNK:(hh + 1) * CHUNK], 0.0)
            a_ak = jnp.where(strict, pk[hh * CHUNK:(hh + 1) * CHUNK], 0.0)
            inv = _inv_unit_lower(a_ab)
            xs.append((inv, a_ak))
        x = x0 + jnp.where(lo, _dot(xs[0][1], v), _dot(xs[1][1], v))
        u = jnp.where(lo, _dot(xs[0][0], x), _dot(xs[1][0], x))
        for hh in range(2):
            a_rb = jnp.where(causal, pb[(2 + hh) * CHUNK:(3 + hh) * CHUNK], 0.0)
            a_rk = jnp.where(causal, pk[(2 + hh) * CHUNK:(3 + hh) * CHUNK], 0.0)
            ys.append(_dot(a_rb, u) + _dot(a_rk, v))
        y_ref[p, rows, :] = _dot_nt(rt, S) + jnp.where(lo, ys[0], ys[1])
        upd = (_dot(u.T, bt) + _dot(v.T, kt)) * same_head
        S_ref[p] = (S + upd) * jnp.exp(cs[CHUNK - 1:CHUNK, :])

    def chunk_body(c, carry):
        rows = pl.ds(pl.multiple_of(c * CHUNK, CHUNK), CHUNK)
        for p in range(N_PAIRS):
            pair_step(p, rows)
        return carry

    lax.fori_loop(0, tb // CHUNK, chunk_body, 0)

    inv_n = 1.0 / RWKV_HEAD
    for c in range(tb // CHUNK):
        rows = slice(c * CHUNK, (c + 1) * CHUNK)
        for p in range(N_PAIRS):
            ps = slice(p * LANES, (p + 1) * LANES)
            y = y_ref[p, rows, :]
            yc = y - _dot(y, same_head) * inv_n
            var = _dot(yc * yc, same_head) * inv_n
            yn = yc * lax.rsqrt(var + RWKV_LN_EPS) * lng_ref[:, ps] + lnb_ref[:, ps]
            out_ref[0, rows, ps] = ((yn + bo_ref[p, rows, :]) * gs_ref[p, rows, :]).astype(out_ref.dtype)


def _rwkv_call(proj, mu, w0, w2, a0, a2, g2, k_k, k_a, r_k, ln_g, ln_b, *, col0, tb=256):
    B, T, _ = proj.shape
    j0 = col0 // D_GROUP
    l0 = (col0 + 3 * D_GROUP) // LANES
    row = lambda a: a.reshape(1, -1).astype(F32)
    mul = jnp.zeros((1, RWKV_LOW), F32).at[0, :288].set(mu[3 * D_GROUP:])
    w2p = jnp.zeros((LANES, D_GROUP), F32).at[:64].set(w2)
    a2p = jnp.zeros((LANES, D_GROUP), F32).at[64:].set(a2)
    g2p = jnp.zeros((2 * LANES, D_GROUP), F32).at[:160].set(g2)
    colblk = lambda j: pl.BlockSpec((1, tb, D_GROUP), lambda b, t: (b, t, j0 + j))
    lowblk = lambda j: pl.BlockSpec((1, tb, LANES), lambda b, t: (b, t, l0 + j))
    hrow = lambda t: jnp.maximum(t * (tb // 8) - 1, 0)
    halo = lambda j: pl.BlockSpec((1, 8, D_GROUP), lambda b, t: (b, hrow(t), j0 + j))
    lowhalo = lambda j: pl.BlockSpec((1, 8, LANES), lambda b, t: (b, hrow(t), l0 + j))
    full = lambda a: pl.BlockSpec(a.shape, lambda b, t: (0,) * a.ndim)
    params = [row(mu[:3 * D_GROUP]), mul, row(w0), w2p, row(a0), a2p, g2p, row(k_k), row(k_a), row(r_k),
              row(ln_g), row(ln_b)]
    big = pltpu.VMEM((N_PAIRS, tb, LANES), F32)
    return pl.pallas_call(
        functools.partial(_rwkv_kernel, tb=tb),
        grid=(B, T // tb),
        in_specs=[colblk(0), colblk(1), colblk(2), lowblk(0), lowblk(1), lowblk(2),
                  halo(0), halo(1), halo(2), lowhalo(0), lowhalo(1), lowhalo(2)] + [full(a) for a in params],
        out_specs=pl.BlockSpec((1, tb, D_GROUP), lambda b, t: (b, t, 0)),
        out_shape=jax.ShapeDtypeStruct((B, T, D_GROUP), BF16),
        scratch_shapes=[big] * 9 + [pltpu.VMEM((N_PAIRS, LANES, LANES), F32)],
        compiler_params=pltpu.CompilerParams(dimension_semantics=("arbitrary", "arbitrary"),
                                             vmem_limit_bytes=VMEM_LIMIT),
        name="rwkv7",
    )(*([proj] * 12), *params)


def _ada_kernel(c_ref, w_ref, b_ref, out_ref):
    sc = _silu(c_ref[...]).astype(BF16)
    out_ref[0] = jnp.dot(sc, w_ref[0].astype(BF16), preferred_element_type=F32) + b_ref[0]


def _ada_call(c, ada_w, ada_b, *, tn=1536):
    B = c.shape[0]
    n_mod = ada_w.shape[0] * ada_w.shape[1]
    w = ada_w.reshape(n_mod, D_MODEL, 3 * D_MODEL)
    b = ada_b.reshape(n_mod, 1, 3 * D_MODEL)
    return pl.pallas_call(
        _ada_kernel,
        grid=(n_mod, 3 * D_MODEL // tn),
        in_specs=[pl.BlockSpec((B, D_MODEL), lambda i, j: (0, 0)),
                  pl.BlockSpec((1, D_MODEL, tn), lambda i, j: (i, 0, j)),
                  pl.BlockSpec((1, 1, tn), lambda i, j: (i, 0, j))],
        out_specs=pl.BlockSpec((1, B, tn), lambda i, j: (i, 0, j)),
        out_shape=jax.ShapeDtypeStruct((n_mod, B, 3 * D_MODEL), F32),
        compiler_params=pltpu.CompilerParams(dimension_semantics=("arbitrary", "arbitrary"),
                                             vmem_limit_bytes=VMEM_LIMIT),
        name="adaln",
    )(c, w, b)


def _mod_spec(i, part, nb):
    return pl.BlockSpec((1, 1, D_MODEL), lambda b, t: (i * nb + b, 0, part))


def _modulate_kernel(x_ref, shift_ref, scale_ref, h_ref):
    h_ref[0] = (x_ref[0] * (1.0 + scale_ref[0]) + shift_ref[0]).astype(h_ref.dtype)


def _modulate_call(x, mods3, i, *, tb=512):
    B, T, _ = x.shape
    blk = pl.BlockSpec((1, tb, D_MODEL), lambda b, t: (b, t, 0))
    return pl.pallas_call(
        _modulate_kernel,
        grid=(B, T // tb),
        in_specs=[blk, _mod_spec(i, 0, B), _mod_spec(i, 1, B)],
        out_specs=blk,
        out_shape=jax.ShapeDtypeStruct(x.shape, BF16),
        compiler_params=pltpu.CompilerParams(dimension_semantics=("arbitrary", "arbitrary"),
                                             vmem_limit_bytes=VMEM_LIMIT),
        name="modulate",
    )(x, mods3, mods3)


def _ln_kernel(x_ref, y_ref, gate_ref, g_ref, b_ref, *rest, with_next):
    z = ALPHA * x_ref[0] + (1.0 + gate_ref[0]) * y_ref[0]
    mu = jnp.mean(z, axis=-1, keepdims=True)
    zc = z - mu
    var = jnp.mean(zc * zc, axis=-1, keepdims=True)
    xn = zc * lax.rsqrt(var + LN_EPS) * g_ref[...] + b_ref[...]
    if with_next:
        shift_ref, scale_ref, xo_ref, h_ref = rest
        xo_ref[0] = xn
        h_ref[0] = (xn * (1.0 + scale_ref[0]) + shift_ref[0]).astype(h_ref.dtype)
    else:
        (xo_ref,) = rest
        xo_ref[0] = xn


def _ln_call(x, y, mods3, i, g, b, *, with_next, tb=256):
    B, T, _ = x.shape
    blk = pl.BlockSpec((1, tb, D_MODEL), lambda b, t: (b, t, 0))
    row = pl.BlockSpec((1, D_MODEL), lambda b, t: (0, 0))
    in_specs = [blk, blk, _mod_spec(i, 2, B), row, row]
    args = [x, y, mods3, g.reshape(1, D_MODEL), b.reshape(1, D_MODEL)]
    out_specs = [blk]
    out_shape = [jax.ShapeDtypeStruct(x.shape, F32)]
    if with_next:
        in_specs += [_mod_spec(i + 1, 0, B), _mod_spec(i + 1, 1, B)]
        args += [mods3, mods3]
        out_specs.append(blk)
        out_shape.append(jax.ShapeDtypeStruct(x.shape, BF16))
    return pl.pallas_call(
        functools.partial(_ln_kernel, with_next=with_next),
        grid=(B, T // tb),
        in_specs=in_specs, out_specs=out_specs, out_shape=out_shape,
        compiler_params=pltpu.CompilerParams(dimension_semantics=("arbitrary", "arbitrary"),
                                             vmem_limit_bytes=VMEM_LIMIT),
        name="residual_ln",
    )(*args)


def _matmul_kernel(a_ref, b_ref, o_ref, *scratch, nk):
    if nk == 1:
        o_ref[...] = jnp.dot(a_ref[...], b_ref[...], preferred_element_type=F32).astype(o_ref.dtype)
        return
    (acc_ref,) = scratch
    k = pl.program_id(2)

    @pl.when(k == 0)
    def _():
        acc_ref[...] = jnp.zeros_like(acc_ref)

    acc_ref[...] += jnp.dot(a_ref[...], b_ref[...], preferred_element_type=F32)

    @pl.when(k == nk - 1)
    def _():
        o_ref[...] = acc_ref[...].astype(o_ref.dtype)


def _matmul_call(a, b, *, tm, tn, tk, out_dtype=F32):
    M, K = a.shape
    _, N = b.shape
    nk = K // tk
    return pl.pallas_call(
        functools.partial(_matmul_kernel, nk=nk),
        grid=(N // tn, M // tm, nk),
        in_specs=[pl.BlockSpec((tm, tk), lambda j, i, k: (i, k)),
                  pl.BlockSpec((tk, tn), lambda j, i, k: (k, j))],
        out_specs=pl.BlockSpec((tm, tn), lambda j, i, k: (i, j)),
        out_shape=jax.ShapeDtypeStruct((M, N), out_dtype),
        scratch_shapes=[] if nk == 1 else [pltpu.VMEM((tm, tn), F32)],
        compiler_params=pltpu.CompilerParams(dimension_semantics=("arbitrary", "arbitrary", "arbitrary"),
                                             vmem_limit_bytes=VMEM_LIMIT),
        name="matmul",
    )(a, b)


def _swiglu_up_kernel(h_ref, wg_ref, wu_ref, o_ref):
    h = h_ref[...]
    g = jnp.dot(h, wg_ref[...], preferred_element_type=F32)
    u = jnp.dot(h, wu_ref[...], preferred_element_type=F32)
    o_ref[...] = (_silu(g) * u).astype(o_ref.dtype)


def _swiglu_up_call(h, wg, wu, *, tm=512, tn=1408):
    M, K = h.shape
    _, N = wg.shape
    wspec = pl.BlockSpec((K, tn), lambda j, i: (0, j))
    return pl.pallas_call(
        _swiglu_up_kernel,
        grid=(N // tn, M // tm),
        in_specs=[pl.BlockSpec((tm, K), lambda j, i: (i, 0)), wspec, wspec],
        out_specs=pl.BlockSpec((tm, tn), lambda j, i: (i, j)),
        out_shape=jax.ShapeDtypeStruct((M, N), BF16),
        compiler_params=pltpu.CompilerParams(dimension_semantics=("arbitrary", "arbitrary"),
                                             vmem_limit_bytes=VMEM_LIMIT),
        name="swiglu_up",
    )(h, wg, wu)


AB_COLS = 7680
CD_COLS = 8320
RWKV_COL0 = 4 * D_GROUP
MLSTM_GATE_COL = 7 * D_GROUP + RWKV_LOW
GDN_COL0 = 4 * D_GROUP


def _pad_cols(w, n):
    return jnp.pad(w, ((0, 0), (0, n - w.shape[1])))


def _ab_weight(w_in):
    a_main, a_gates = w_in[:, :4 * D_GROUP], w_in[:, 4 * D_GROUP:4 * D_GROUP + 2 * N_HEADS]
    b0 = 4 * D_GROUP + 2 * N_HEADS
    b_main, b_low = w_in[:, b0:b0 + 3 * D_GROUP], w_in[:, b0 + 3 * D_GROUP:]
    return jnp.concatenate([a_main, b_main, _pad_cols(b_low, RWKV_LOW), _pad_cols(a_gates, LANES)],
                           axis=1).astype(BF16)


def _cd_weight(w_in):
    return _pad_cols(w_in, CD_COLS).astype(BF16)


def kernel(x, c, positions, ada_w, ada_b, ln_g, ln_b, ab_w_in, ab_w_out, mlstm_conv_w, mlstm_gate_b, mlstm_norm_g, rwkv_mu, rwkv_w0, rwkv_w2, rwkv_a0, rwkv_a2, rwkv_g2, rwkv_k_k, rwkv_k_a, rwkv_r_k, rwkv_ln_g, rwkv_ln_b, cd_w_in, cd_w_out, ret_norm_g, gdn_conv_w, gdn_a_log, gdn_dt_bias, gdn_norm_g, ffn_w_gate, ffn_w_up, ffn_w_down):
    B, T, D = x.shape
    M = B * T
    depth = ada_w.shape[0]
    mods = _ada_call(c, ada_w, ada_b)
    mods3 = mods.reshape(2 * depth * B, 1, 3 * D)
    cos2, sin2 = _rope_call(positions)
    h = _modulate_call(x, mods3, 0)
    for layer in range(depth):
        j = layer // 2
        i_mix, i_ffn = 2 * layer, 2 * layer + 1
        if layer % 2 == 0:
            proj = _matmul_call(h.reshape(M, D), _ab_weight(ab_w_in[j]), tm=512, tn=1280, tk=D)
            proj = proj.reshape(B, T, AB_COLS)
            ya = _mlstm_call(proj, mlstm_conv_w[j], mlstm_gate_b[j], mlstm_norm_g[j], gate_col=MLSTM_GATE_COL)
            yb = _rwkv_call(proj, rwkv_mu[j], rwkv_w0[j], rwkv_w2[j], rwkv_a0[j], rwkv_a2[j], rwkv_g2[j],
                            rwkv_k_k[j], rwkv_k_a[j], rwkv_r_k[j], rwkv_ln_g[j], rwkv_ln_b[j], col0=RWKV_COL0)
            w_out = ab_w_out[j]
        else:
            proj = _matmul_call(h.reshape(M, D), _cd_weight(cd_w_in[j]), tm=512, tn=1664, tk=D)
            proj = proj.reshape(B, T, CD_COLS)
            ya = _ret_call(proj, cos2, sin2, ret_norm_g[j])
            yb = _gdn_call(proj, gdn_conv_w[j], gdn_a_log[j], gdn_dt_bias[j], gdn_norm_g[j], col0=GDN_COL0)
            w_out = cd_w_out[j]
        ycat = jnp.concatenate([ya, yb], axis=-1).reshape(M, D)
        y = _matmul_call(ycat, w_out.astype(BF16), tm=512, tn=D, tk=D).reshape(B, T, D)
        x, h = _ln_call(x, y, mods3, i_mix, ln_g[layer, 0], ln_b[layer, 0], with_next=True)
        act = _swiglu_up_call(h.reshape(M, D), ffn_w_gate[layer].astype(BF16), ffn_w_up[layer].astype(BF16))
        y = _matmul_call(act, ffn_w_down[layer].astype(BF16), tm=512, tn=D, tk=1408).reshape(B, T, D)
        last = layer == depth - 1
        res = _ln_call(x, y, mods3, i_ffn, ln_g[layer, 1], ln_b[layer, 1], with_next=not last)
        if last:
            (x,) = res
        else:
            x, h = res
    return x
```

```python
import functools
import math

import numpy as np
import jax
import jax.numpy as jnp
from jax import lax
from jax.experimental import pallas as pl
from jax.experimental.pallas import tpu as pltpu

F32 = jnp.float32
BF16 = jnp.bfloat16

D_MODEL = 2048
D_GROUP = 1024
HEAD_DIM = 128
N_HEADS = 8
RWKV_HEAD = 64
N_RWKV = 16
CHUNK = 64
D_FF = 5632
DEPTH = 2
ALPHA = (2 * DEPTH) ** 0.25
LN_EPS = 1e-5
RWKV_LN_EPS = 64e-5
ROPE_BASE = 10000.0
RET_GAMMA_BASE = 5.0
LANES = 128
VMEM_LIMIT = 48 * 1024 * 1024


def _dot(a, b):
    return jnp.dot(a.astype(BF16), b.astype(BF16), preferred_element_type=F32)


def _dot_nt(a, b):
    return lax.dot_general(a.astype(BF16), b.astype(BF16), (((1,), (1,)), ((), ())),
                           preferred_element_type=F32)


def _split3(x):
    hi = x.astype(BF16)
    r1 = x - hi.astype(F32)
    mid = r1.astype(BF16)
    lo = (r1 - mid.astype(F32)).astype(BF16)
    return hi, mid, lo


def _cumsum_rows(x):
    n = x.shape[1]
    out = jnp.dot(_tri(CHUNK).astype(BF16), jnp.concatenate(_split3(x), axis=1), preferred_element_type=F32)
    return out[:, :n] + out[:, n:2 * n] + out[:, 2 * n:]


def _group_sum(x, ones01):
    m = x.shape[0]
    out = jnp.dot(jnp.concatenate(_split3(x), axis=0), ones01.astype(BF16), preferred_element_type=F32)
    return out[:m] + out[m:2 * m] + out[2 * m:]


def _sigmoid(x):
    return 1.0 / (1.0 + jnp.exp(-x))


def _silu(x):
    return x * _sigmoid(x)


def _log_sigmoid(x):
    return jnp.minimum(x, 0.0) - jnp.log1p(jnp.exp(-jnp.abs(x)))


def _softplus(x):
    return jnp.maximum(x, 0.0) + jnp.log1p(jnp.exp(-jnp.abs(x)))


def _tri(n, strict=False):
    r = lax.broadcasted_iota(jnp.int32, (n, n), 0)
    c = lax.broadcasted_iota(jnp.int32, (n, n), 1)
    return (r > c) if strict else (r >= c)


def _conv_silu_rows(src_ref, halo_ref, w, first, r0, nrows, c0, ncols):
    cur = src_ref[0, r0:r0 + nrows, c0:c0 + ncols]
    acc = w[3:4] * cur
    if r0 == 0:
        hl = jnp.where(first, 0.0, halo_ref[0, :, c0:c0 + ncols])
        ext = jnp.concatenate([hl, cur[0:8]], axis=0)
        for j in range(3):
            head = ext[5 + j:13 + j]
            if nrows > 8:
                rest = src_ref[0, 5 + j:nrows - 3 + j, c0:c0 + ncols]
                sh = jnp.concatenate([head, rest], axis=0)
            else:
                sh = head
            acc = acc + w[j:j + 1] * sh
    else:
        for j in range(3):
            acc = acc + w[j:j + 1] * src_ref[0, r0 - 3 + j:r0 - 3 + j + nrows, c0:c0 + ncols]
    return _silu(acc)


def _head_norm_rows(h, g_row, eps, center=True):
    if center:
        h = h - jnp.mean(h, axis=-1, keepdims=True)
    return h * lax.rsqrt(jnp.mean(h * h, axis=-1, keepdims=True) + eps) * g_row


def _mlstm_kernel(q_ref, k_ref, v_ref, o_ref, g_ref, qh_ref, kh_ref, cw_ref, gb_ref, ng_ref,
                  out_ref, qc_ref, kc_ref, C_ref, n_ref, m_ref, *, tb):
    t = pl.program_id(1)
    first = t == 0

    @pl.when(first)
    def _():
        C_ref[...] = jnp.zeros_like(C_ref)
        n_ref[...] = jnp.zeros_like(n_ref)
        m_ref[...] = jnp.zeros_like(m_ref)

    for c in range(tb // CHUNK):
        for cb in range(D_GROUP // 256):
            cs = cb * 256
            qc_ref[c * CHUNK:(c + 1) * CHUNK, cs:cs + 256] = _conv_silu_rows(
                q_ref, qh_ref, cw_ref[:, cs:cs + 256], first, c * CHUNK, CHUNK, cs, 256)
            kc_ref[c * CHUNK:(c + 1) * CHUNK, cs:cs + 256] = _conv_silu_rows(
                k_ref, kh_ref, cw_ref[:, D_GROUP + cs:D_GROUP + cs + 256], first, c * CHUNK, CHUNK, cs, 256)

    causal = _tri(CHUNK)
    gb = gb_ref[...]
    scale = HEAD_DIM ** -0.5

    def chunk_body(c, carry):
        r0 = pl.multiple_of(c * CHUNK, CHUNK)
        rows = pl.ds(r0, CHUNK)
        z = g_ref[0, rows, :] + gb
        bb = _cumsum_rows(_log_sigmoid(z))
        zT = z.T
        bT = bb.T
        heads = range(N_HEADS)
        hs = [slice(h * HEAD_DIM, (h + 1) * HEAD_DIM) for h in heads]
        ig_col = [z[:, h:h + 1] for h in heads]
        b_col = [bb[:, 8 + h:9 + h] for h in heads]
        m = [m_ref[h:h + 1, 0:1] for h in heads]
        q = [qc_ref[rows, hs[h]] for h in heads]
        k = [kc_ref[rows, hs[h]] * scale for h in heads]
        v = [v_ref[0, rows, hs[h]] for h in heads]
        Cst = [C_ref[h] for h in heads]
        nst = [n_ref[h:h + 1, :] for h in heads]
        qk = [_dot_nt(q[h], k[h]) for h in heads]
        qC = [_dot(q[h], Cst[h]) for h in heads]
        dlog = [jnp.where(causal, b_col[h] - bT[8 + h:9 + h, :] + zT[h:h + 1, :], -1e30) for h in heads]
        inter = [b_col[h] + m[h] for h in heads]
        mt = [jnp.maximum(jnp.max(dlog[h], axis=-1, keepdims=True), inter[h]) for h in heads]
        s = [qk[h] * jnp.where(causal, jnp.exp(dlog[h] - mt[h]), 0.0) for h in heads]
        sc = [jnp.exp(inter[h] - mt[h]) for h in heads]
        sv = [_dot(s[h], v[h]) for h in heads]
        bl = [bb[CHUNK - 1:CHUNK, 8 + h:9 + h] for h in heads]
        lw = [bl[h] - b_col[h] + ig_col[h] for h in heads]
        m_new = [jnp.maximum(bl[h] + m[h], jnp.max(lw[h], axis=0, keepdims=True)) for h in heads]
        kw = [k[h] * jnp.exp(lw[h] - m_new[h]) for h in heads]
        dec = [jnp.exp(bl[h] + m[h] - m_new[h]) for h in heads]
        kv = [_dot(kw[h].T, v[h]) for h in heads]
        for h in heads:
            C_ref[h] = dec[h] * Cst[h] + kv[h]
            n_ref[h:h + 1, :] = dec[h] * nst[h] + jnp.sum(kw[h], axis=0, keepdims=True)
            m_ref[h:h + 1, :] = jnp.broadcast_to(m_new[h], (1, LANES))
        for h in heads:
            num = sv[h] + sc[h] * qC[h]
            den = jnp.sum(s[h], axis=-1, keepdims=True) + sc[h] * jnp.sum(q[h] * nst[h], axis=-1, keepdims=True)
            hh = num / jnp.maximum(jnp.abs(den), jnp.exp(-mt[h]))
            hn = _head_norm_rows(hh, ng_ref[:, hs[h]], LN_EPS)
            out_ref[0, rows, hs[h]] = (hn * _sigmoid(o_ref[0, rows, hs[h]])).astype(out_ref.dtype)
        return carry

    lax.fori_loop(0, tb // CHUNK, chunk_body, 0)


def _mlstm_call(proj, conv_w, gate_b, norm_g, *, gate_col, tb=256):
    B, T, _ = proj.shape
    nt = T // tb
    gb = jnp.zeros((1, LANES), F32).at[0, :2 * N_HEADS].set(gate_b)
    ng = norm_g.reshape(1, D_GROUP)
    colblk = lambda j: pl.BlockSpec((1, tb, D_GROUP), lambda b, t: (b, t, j))
    halo = lambda j: pl.BlockSpec((1, 8, D_GROUP), lambda b, t: (b, jnp.maximum(t * (tb // 8) - 1, 0), j))
    return pl.pallas_call(
        functools.partial(_mlstm_kernel, tb=tb),
        grid=(B, nt),
        in_specs=[colblk(0), colblk(1), colblk(2), colblk(3),
                  pl.BlockSpec((1, tb, LANES), lambda b, t: (b, t, gate_col // LANES)),
                  halo(0), halo(1),
                  pl.BlockSpec((4, 2 * D_GROUP), lambda b, t: (0, 0)),
                  pl.BlockSpec((1, LANES), lambda b, t: (0, 0)),
                  pl.BlockSpec((1, D_GROUP), lambda b, t: (0, 0))],
        out_specs=pl.BlockSpec((1, tb, D_GROUP), lambda b, t: (b, t, 0)),
        out_shape=jax.ShapeDtypeStruct((B, T, D_GROUP), BF16),
        scratch_shapes=[pltpu.VMEM((tb, D_GROUP), F32), pltpu.VMEM((tb, D_GROUP), F32),
                        pltpu.VMEM((N_HEADS, HEAD_DIM, HEAD_DIM), F32),
                        pltpu.VMEM((N_HEADS, LANES), F32), pltpu.VMEM((N_HEADS, LANES), F32)],
        compiler_params=pltpu.CompilerParams(dimension_semantics=("arbitrary", "arbitrary"),
                                             vmem_limit_bytes=VMEM_LIMIT),
        name="mlstm",
    )(proj, proj, proj, proj, proj, proj, proj, conv_w, gb, ng)


def _rope_kernel(pos_ref, inv_ref, cos_ref, sin_ref):
    ang = pos_ref[0].astype(F32) * inv_ref[...]
    lane = lax.broadcasted_iota(jnp.int32, ang.shape, 1)
    cos_ref[0] = jnp.cos(ang)
    sin_ref[0] = jnp.where(lane < HEAD_DIM // 2, -jnp.sin(ang), jnp.sin(ang))


def _rope_call(positions, *, tb=512):
    B, T = positions.shape
    half = HEAD_DIM // 2
    inv_freq = ROPE_BASE ** (-jnp.arange(half, dtype=F32) / half)
    inv2 = jnp.concatenate([inv_freq, inv_freq]).reshape(1, HEAD_DIM)
    spec = pl.BlockSpec((1, tb, HEAD_DIM), lambda b, t: (b, t, 0))
    return pl.pallas_call(
        _rope_kernel,
        grid=(B, T // tb),
        in_specs=[pl.BlockSpec((1, tb, 1), lambda b, t: (b, t, 0)),
                  pl.BlockSpec((1, HEAD_DIM), lambda b, t: (0, 0))],
        out_specs=[spec, spec],
        out_shape=[jax.ShapeDtypeStruct((B, T, HEAD_DIM), F32)] * 2,
        compiler_params=pltpu.CompilerParams(dimension_semantics=("arbitrary", "arbitrary")),
        name="rope_table",
    )(positions.reshape(B, T, 1), inv2)


def _ret_kernel(q_ref, k_ref, v_ref, g_ref, cos_ref, sin_ref, ng_ref, out_ref, R_ref, *, tb):
    t = pl.program_id(1)

    @pl.when(t == 0)
    def _():
        R_ref[...] = jnp.zeros_like(R_ref)

    causal = _tri(CHUNK)
    ri = lax.broadcasted_iota(jnp.int32, (CHUNK, CHUNK), 0)
    ci = lax.broadcasted_iota(jnp.int32, (CHUNK, CHUNK), 1)
    rel = (ri - ci).astype(F32)
    tcol = lax.broadcasted_iota(jnp.int32, (CHUNK, 1), 0).astype(F32)
    scale = HEAD_DIM ** -0.5

    def chunk_body(c, carry):
        r0 = pl.multiple_of(c * CHUNK, CHUNK)
        rows = pl.ds(r0, CHUNK)
        cos2 = cos_ref[0, rows, :]
        sin2 = sin_ref[0, rows, :]
        for h in range(N_HEADS):
            hs = slice(h * HEAD_DIM, (h + 1) * HEAD_DIM)
            lg = math.log1p(-2.0 ** (-RET_GAMMA_BASE - h))
            q = q_ref[0, rows, hs]
            k = k_ref[0, rows, hs]
            v = v_ref[0, rows, hs]
            qr = q * cos2 + pltpu.roll(q, HEAD_DIM // 2, 1) * sin2
            kr = (k * cos2 + pltpu.roll(k, HEAD_DIM // 2, 1) * sin2) * scale
            dmat = jnp.where(causal, jnp.exp(rel * lg), 0.0)
            xi = jnp.exp((tcol + 1.0) * lg)
            zeta = jnp.exp((CHUNK - 1.0 - tcol) * lg)
            Rst = R_ref[h]
            intra = _dot(_dot_nt(qr, kr) * dmat, v)
            inter = _dot(qr, Rst) * xi
            R_ref[h] = Rst * math.exp(CHUNK * lg) + _dot((kr * zeta).T, v)
            on = _head_norm_rows(intra + inter, ng_ref[:, hs], LN_EPS)
            out_ref[0, rows, hs] = (on * _silu(g_ref[0, rows, hs])).astype(out_ref.dtype)
        return carry

    lax.fori_loop(0, tb // CHUNK, chunk_body, 0)


def _ret_call(proj, cos2, sin2, norm_g, *, tb=256):
    B, T, _ = proj.shape
    colblk = lambda j: pl.BlockSpec((1, tb, D_GROUP), lambda b, t: (b, t, j))
    tab = pl.BlockSpec((1, tb, HEAD_DIM), lambda b, t: (b, t, 0))
    return pl.pallas_call(
        functools.partial(_ret_kernel, tb=tb),
        grid=(B, T // tb),
        in_specs=[colblk(0), colblk(1), colblk(2), colblk(3), tab, tab,
                  pl.BlockSpec((1, D_GROUP), lambda b, t: (0, 0))],
        out_specs=pl.BlockSpec((1, tb, D_GROUP), lambda b, t: (b, t, 0)),
        out_shape=jax.ShapeDtypeStruct((B, T, D_GROUP), BF16),
        scratch_shapes=[pltpu.VMEM((N_HEADS, HEAD_DIM, HEAD_DIM), F32)],
        compiler_params=pltpu.CompilerParams(dimension_semantics=("arbitrary", "arbitrary"),
                                             vmem_limit_bytes=VMEM_LIMIT),
        name="retention",
    )(proj, proj, proj, proj, cos2, sin2, norm_g.reshape(1, D_GROUP))


def _solve_unit_lower(nms, rhss):
    n = nms[0].shape[0]
    eye = (lax.broadcasted_iota(jnp.int32, (n, n), 0) == lax.broadcasted_iota(jnp.int32, (n, n), 1)).astype(F32)
    ps = [eye + nm for nm in nms]
    xs = list(nms)
    for _ in range(int(math.log2(n)) - 1):
        xs = [_dot(x, x) for x in xs]
        ps = [p + _dot(p, x) for p, x in zip(ps, xs)]
    x0 = [_dot(p, r) for p, r in zip(ps, rhss)]
    resid = [r - a + _dot(nm, a) for r, a, nm in zip(rhss, x0, nms)]
    return [a + _dot(p, r) for a, p, r in zip(x0, ps, resid)]


def _l2norm_rows(z):
    return z * lax.rsqrt(jnp.sum(z * z, axis=-1, keepdims=True) + 1e-6)


def _gdn_kernel(q_ref, k_ref, v_ref, z_ref, g_ref, qh_ref, kh_ref, vh_ref, cw_ref, an_ref, dt_ref, ng_ref,
                out_ref, qc_ref, kc_ref, vc_ref, S_ref, *, tb):
    t = pl.program_id(1)
    first = t == 0

    @pl.when(first)
    def _():
        S_ref[...] = jnp.zeros_like(S_ref)

    srcs = ((q_ref, qh_ref, qc_ref), (k_ref, kh_ref, kc_ref), (v_ref, vh_ref, vc_ref))
    for c in range(tb // CHUNK):
        for cb in range(D_GROUP // 256):
            cs = cb * 256
            for i, (src, halo, dst) in enumerate(srcs):
                w = cw_ref[:, i * D_GROUP + cs:i * D_GROUP + cs + 256]
                dst[c * CHUNK:(c + 1) * CHUNK, cs:cs + 256] = _conv_silu_rows(
                    src, halo, w, first, c * CHUNK, CHUNK, cs, 256)

    causal = _tri(CHUNK)
    strict = _tri(CHUNK, strict=True)
    a_neg = an_ref[...]
    dtb = dt_ref[...]
    scale = HEAD_DIM ** -0.5

    def chunk_body(c, carry):
        r0 = pl.multiple_of(c * CHUNK, CHUNK)
        rows = pl.ds(r0, CHUNK)
        gz = g_ref[0, rows, :]
        beta = _sigmoid(gz)
        gc = _cumsum_rows(a_neg * _softplus(gz + dtb))
        gcT = gc.T
        heads = range(N_HEADS)
        hs = [slice(h * HEAD_DIM, (h + 1) * HEAD_DIM) for h in heads]
        gc_col = [gc[:, h:h + 1] for h in heads]
        b_col = [beta[:, 8 + h:9 + h] for h in heads]
        gamma = [jnp.where(causal, jnp.exp(gc_col[h] - gcT[h:h + 1, :]), 0.0) for h in heads]
        q = [_l2norm_rows(qc_ref[rows, hs[h]]) * scale for h in heads]
        k = [_l2norm_rows(kc_ref[rows, hs[h]]) for h in heads]
        kb = [k[h] * b_col[h] for h in heads]
        eg = [jnp.exp(gc_col[h]) for h in heads]
        kq = [_dot_nt(jnp.concatenate([kb[h], q[h]], axis=0), k[h]) for h in heads]
        a_mat = [jnp.where(strict, kq[h][:CHUNK] * gamma[h], 0.0) for h in heads]
        att = [kq[h][CHUNK:] * gamma[h] for h in heads]
        rhs = [jnp.concatenate([vc_ref[rows, hs[h]] * b_col[h], kb[h] * eg[h]], axis=1) for h in heads]
        uw = _solve_unit_lower([-a for a in a_mat], rhs)
        S = [S_ref[h] for h in heads]
        ws = [_dot(jnp.concatenate([uw[h][:, HEAD_DIM:], q[h] * eg[h]], axis=0), S[h]) for h in heads]
        v_new = [uw[h][:, :HEAD_DIM] - ws[h][:CHUNK] for h in heads]
        o = [ws[h][CHUNK:] + _dot(att[h], v_new[h]) for h in heads]
        g_last = [gc[CHUNK - 1:CHUNK, h:h + 1] for h in heads]
        kd = [(k[h] * jnp.exp(g_last[h] - gc_col[h])).T for h in heads]
        for h in heads:
            S_ref[h] = S[h] * jnp.exp(g_last[h]) + _dot(kd[h], v_new[h])
        for h in heads:
            on = _head_norm_rows(o[h], ng_ref[:, hs[h]], 1e-6, center=False)
            out_ref[0, rows, hs[h]] = (on * _silu(z_ref[0, rows, hs[h]])).astype(out_ref.dtype)
        return carry

    lax.fori_loop(0, tb // CHUNK, chunk_body, 0)


def _gdn_call(proj, conv_w, a_log, dt_bias, norm_g, *, col0, tb=256):
    B, T, _ = proj.shape
    j0 = col0 // D_GROUP
    an = jnp.zeros((1, LANES), F32).at[0, :N_HEADS].set(-jnp.exp(a_log.astype(F32)))
    dtb = jnp.zeros((1, LANES), F32).at[0, :N_HEADS].set(dt_bias)
    colblk = lambda j: pl.BlockSpec((1, tb, D_GROUP), lambda b, t: (b, t, j0 + j))
    halo = lambda j: pl.BlockSpec((1, 8, D_GROUP), lambda b, t: (b, jnp.maximum(t * (tb // 8) - 1, 0), j0 + j))
    return pl.pallas_call(
        functools.partial(_gdn_kernel, tb=tb),
        grid=(B, T // tb),
        in_specs=[colblk(0), colblk(1), colblk(2), colblk(3),
                  pl.BlockSpec((1, tb, LANES), lambda b, t: (b, t, (col0 + 4 * D_GROUP) // LANES)),
                  halo(0), halo(1), halo(2),
                  pl.BlockSpec((4, 3 * D_GROUP), lambda b, t: (0, 0)),
                  pl.BlockSpec((1, LANES), lambda b, t: (0, 0)),
                  pl.BlockSpec((1, LANES), lambda b, t: (0, 0)),
                  pl.BlockSpec((1, D_GROUP), lambda b, t: (0, 0))],
        out_specs=pl.BlockSpec((1, tb, D_GROUP), lambda b, t: (b, t, 0)),
        out_shape=jax.ShapeDtypeStruct((B, T, D_GROUP), BF16),
        scratch_shapes=[pltpu.VMEM((tb, D_GROUP), F32), pltpu.VMEM((tb, D_GROUP), F32),
                        pltpu.VMEM((tb, D_GROUP), F32),
                        pltpu.VMEM((N_HEADS, HEAD_DIM, HEAD_DIM), F32)],
        compiler_params=pltpu.CompilerParams(dimension_semantics=("arbitrary", "arbitrary"),
                                             vmem_limit_bytes=VMEM_LIMIT),
        name="gdn",
    )(proj, proj, proj, proj, proj, proj, proj, proj, conv_w, an, dtb, norm_g.reshape(1, D_GROUP))


N_PAIRS = N_RWKV // 2
RWKV_LOW = 384


def _shift1_rows(src_ref, halo_ref, first, r0, nrows, c0, ncols):
    if r0 == 0:
        hl = jnp.where(first, 0.0, halo_ref[0, 7:8, c0:c0 + ncols])
        return jnp.concatenate([hl, src_ref[0, 0:nrows - 1, c0:c0 + ncols]], axis=0)
    return src_ref[0, r0 - 1:r0 - 1 + nrows, c0:c0 + ncols]


def _rwkv_kernel(r_ref, k_ref, v_ref, l0_ref, l1_ref, l2_ref,
                 rh_ref, kh_ref, vh_ref, l0h_ref, l1h_ref, l2h_ref,
                 mu_ref, mul_ref, w0_ref, w2_ref, a0_ref, a2_ref, g2_ref, kk_ref, ka_ref, rk_ref,
                 lng_ref, lnb_ref, out_ref,
                 rs_ref, ks_ref, vs_ref, lw_ref, an_ref, bn_ref, gs_ref, bo_ref, y_ref, S_ref, *, tb):
    t = pl.program_id(1)
    first = t == 0

    @pl.when(first)
    def _():
        S_ref[...] = jnp.zeros_like(S_ref)

    ri = lax.broadcasted_iota(jnp.int32, (LANES, LANES), 0)
    ci = lax.broadcasted_iota(jnp.int32, (LANES, LANES), 1)
    same_head = ((ri // RWKV_HEAD) == (ci // RWKV_HEAD)).astype(F32)

    def lerp(src, halo, mu, r0, c0, ncols):
        cur = src[0, r0:r0 + CHUNK, c0:c0 + ncols]
        return cur + (_shift1_rows(src, halo, first, r0, CHUNK, c0, ncols) - cur) * mu

    for c in range(tb // CHUNK):
        r0 = c * CHUNK
        wl = lerp(l0_ref, l0h_ref, mul_ref[:, 0:LANES], r0, 0, LANES)
        g1 = lerp(l1_ref, l1h_ref, mul_ref[:, LANES:2 * LANES], r0, 0, LANES)
        g2 = lerp(l2_ref, l2h_ref, mul_ref[:, 2 * LANES:3 * LANES], r0, 0, LANES)
        lane = lax.broadcasted_iota(jnp.int32, wl.shape, 1)
        wl_t = jnp.where(lane < 64, jnp.tanh(wl), 0.0)
        al = jnp.where(lane >= 64, wl, 0.0)
        sg1 = _sigmoid(g1)
        sg2 = jnp.where(lane < 32, _sigmoid(g2), 0.0)
        for p in range(N_PAIRS):
            ps = slice(p * LANES, (p + 1) * LANES)
            wpre = w0_ref[:, ps] + _dot(wl_t, w2_ref[:, ps])
            w = -_softplus(-wpre) - 0.5
            a = _sigmoid(a0_ref[:, ps] + _dot(al, a2_ref[:, ps]))
            g = _dot(sg1, g2_ref[0:LANES, ps]) + _dot(sg2, g2_ref[LANES:2 * LANES, ps])
            r = lerp(r_ref, rh_ref, mu_ref[:, ps], r0, p * LANES, LANES)
            k = lerp(k_ref, kh_ref, mu_ref[:, D_GROUP + p * LANES:D_GROUP + (p + 1) * LANES], r0, p * LANES, LANES)
            v = lerp(v_ref, vh_ref, mu_ref[:, 2 * D_GROUP + p * LANES:2 * D_GROUP + (p + 1) * LANES], r0, p * LANES, LANES)
            kk = k * kk_ref[:, ps]
            nrm = jnp.sqrt(_group_sum(kk * kk, same_head))
            kk = kk / jnp.maximum(nrm, 1e-12)
            k2 = k * (1.0 + (a - 1.0) * ka_ref[:, ps])
            rows = slice(r0, r0 + CHUNK)
            rs_ref[p, rows, :] = r
            ks_ref[p, rows, :] = k2
            vs_ref[p, rows, :] = v
            lw_ref[p, rows, :] = -jnp.exp(w)
            an_ref[p, rows, :] = -kk
            bn_ref[p, rows, :] = kk * a
            gs_ref[p, rows, :] = g
            bo_ref[p, rows, :] = _group_sum(r * k2 * rk_ref[:, ps], same_head) * v

    strict = _tri(CHUNK, strict=True)
    lane1 = lax.broadcasted_iota(jnp.int32, (1, LANES), 1)
    m0 = (lane1 < RWKV_HEAD).astype(F32)
    m1 = 1.0 - m0
    t2 = lax.broadcasted_iota(jnp.int32, (CHUNK, LANES), 0)
    l2 = lax.broadcasted_iota(jnp.int32, (CHUNK, LANES), 1)
    lo = l2 < RWKV_HEAD
    s2 = l2 & (RWKV_HEAD - 1)
    causal2 = t2 >= s2
    strict_hi = (t2 > s2) & (l2 >= RWKV_HEAD)

    def chunk_body(c, carry):
        rows = pl.ds(pl.multiple_of(c * CHUNK, CHUNK), CHUNK)
        pairs = range(N_PAIRS)
        halves = [(p, hh) for p in pairs for hh in range(2)]
        lw = [lw_ref[p, rows, :] for p in pairs]
        cs = [_cumsum_rows(lw[p]) for p in pairs]
        w_inv = [jnp.exp(-cs[p]) for p in pairs]
        rt = [rs_ref[p, rows, :] * jnp.exp(cs[p]) for p in pairs]
        at = [an_ref[p, rows, :] * jnp.exp(cs[p] - lw[p]) for p in pairs]
        v = [vs_ref[p, rows, :] for p in pairs]
        bk = [jnp.concatenate([bn_ref[p, rows, :] * w_inv[p], ks_ref[p, rows, :] * w_inv[p]], axis=0).astype(BF16)
              for p in pairs]
        S = [S_ref[p] for p in pairs]
        pm = [_dot_nt(jnp.concatenate([at[p] * m0, at[p] * m1, rt[p] * m0, rt[p] * m1], axis=0), bk[p]) for p in pairs]
        xy0 = [_dot_nt(jnp.concatenate([at[p], rt[p]], axis=0), S[p]) for p in pairs]
        vv = [jnp.concatenate([v[p], v[p]], axis=0).astype(BF16) for p in pairs]
        pa = [pm[p][hh * CHUNK:(hh + 1) * CHUNK] for p, hh in halves]
        a_ab = [jnp.where(strict, a[:, :CHUNK], 0.0) for a in pa]
        x = [xy0[p][:CHUNK] + _dot(jnp.where(strict_hi, pa[2 * p + hh], 0.0), vv[p]) for p, hh in halves]
        us = _solve_unit_lower(a_ab, x)
        uv_f = [jnp.concatenate([jnp.where(lo, us[2 * p], us[2 * p + 1]), v[p]], axis=0) for p in pairs]
        uv = [a.astype(BF16) for a in uv_f]
        ys = [_dot(jnp.where(causal2, pm[p][(2 + hh) * CHUNK:(3 + hh) * CHUNK], 0.0), uv[p]) for p, hh in halves]
        for p in pairs:
            y_ref[p, rows, :] = xy0[p][CHUNK:] + jnp.where(lo, ys[2 * p], ys[2 * p + 1])
        upd = [_dot(uv_f[p].T, bk[p]) * same_head for p in pairs]
        for p in pairs:
            S_ref[p] = (S[p] + upd[p]) * jnp.exp(cs[p][CHUNK - 1:CHUNK, :])
        return carry

    lax.fori_loop(0, tb // CHUNK, chunk_body, 0)

    inv_n = 1.0 / RWKV_HEAD
    for c in range(tb // CHUNK):
        rows = slice(c * CHUNK, (c + 1) * CHUNK)
        for p in range(N_PAIRS):
            ps = slice(p * LANES, (p + 1) * LANES)
            y = y_ref[p, rows, :]
            yc = y - _group_sum(y, same_head) * inv_n
            var = _group_sum(yc * yc, same_head) * inv_n
            yn = yc * lax.rsqrt(var + RWKV_LN_EPS) * lng_ref[:, ps] + lnb_ref[:, ps]
            out_ref[0, rows, ps] = ((yn + bo_ref[p, rows, :]) * gs_ref[p, rows, :]).astype(out_ref.dtype)


def _rwkv_call(proj, mu, w0, w2, a0, a2, g2, k_k, k_a, r_k, ln_g, ln_b, *, col0, tb=256):
    B, T, _ = proj.shape
    j0 = col0 // D_GROUP
    l0 = (col0 + 3 * D_GROUP) // LANES
    row = lambda a: a.reshape(1, -1).astype(F32)
    mul = jnp.zeros((1, RWKV_LOW), F32).at[0, :288].set(mu[3 * D_GROUP:])
    w2p = jnp.zeros((LANES, D_GROUP), F32).at[:64].set(w2)
    a2p = jnp.zeros((LANES, D_GROUP), F32).at[64:].set(a2)
    g2p = jnp.zeros((2 * LANES, D_GROUP), F32).at[:160].set(g2)
    colblk = lambda j: pl.BlockSpec((1, tb, D_GROUP), lambda b, t: (b, t, j0 + j))
    lowblk = lambda j: pl.BlockSpec((1, tb, LANES), lambda b, t: (b, t, l0 + j))
    hrow = lambda t: jnp.maximum(t * (tb // 8) - 1, 0)
    halo = lambda j: pl.BlockSpec((1, 8, D_GROUP), lambda b, t: (b, hrow(t), j0 + j))
    lowhalo = lambda j: pl.BlockSpec((1, 8, LANES), lambda b, t: (b, hrow(t), l0 + j))
    full = lambda a: pl.BlockSpec(a.shape, lambda b, t: (0,) * a.ndim)
    params = [row(mu[:3 * D_GROUP]), mul, row(w0), w2p, row(a0), a2p, g2p, row(k_k), row(k_a), row(r_k),
              row(ln_g), row(ln_b)]
    big = pltpu.VMEM((N_PAIRS, tb, LANES), F32)
    return pl.pallas_call(
        functools.partial(_rwkv_kernel, tb=tb),
        grid=(B, T // tb),
        in_specs=[colblk(0), colblk(1), colblk(2), lowblk(0), lowblk(1), lowblk(2),
                  halo(0), halo(1), halo(2), lowhalo(0), lowhalo(1), lowhalo(2)] + [full(a) for a in params],
        out_specs=pl.BlockSpec((1, tb, D_GROUP), lambda b, t: (b, t, 0)),
        out_shape=jax.ShapeDtypeStruct((B, T, D_GROUP), BF16),
        scratch_shapes=[big] * 9 + [pltpu.VMEM((N_PAIRS, LANES, LANES), F32)],
        compiler_params=pltpu.CompilerParams(dimension_semantics=("arbitrary", "arbitrary"),
                                             vmem_limit_bytes=VMEM_LIMIT),
        name="rwkv7",
    )(*([proj] * 12), *params)


def _ada_kernel(c_ref, w_ref, b_ref, out_ref):
    sc = _silu(c_ref[...]).astype(BF16)
    out_ref[0] = jnp.dot(sc, w_ref[0].astype(BF16), preferred_element_type=F32) + b_ref[0]


def _ada_call(c, ada_w, ada_b, *, tn=1536):
    B = c.shape[0]
    n_mod = ada_w.shape[0] * ada_w.shape[1]
    w = ada_w.reshape(n_mod, D_MODEL, 3 * D_MODEL)
    b = ada_b.reshape(n_mod, 1, 3 * D_MODEL)
    return pl.pallas_call(
        _ada_kernel,
        grid=(n_mod, 3 * D_MODEL // tn),
        in_specs=[pl.BlockSpec((B, D_MODEL), lambda i, j: (0, 0)),
                  pl.BlockSpec((1, D_MODEL, tn), lambda i, j: (i, 0, j)),
                  pl.BlockSpec((1, 1, tn), lambda i, j: (i, 0, j))],
        out_specs=pl.BlockSpec((1, B, tn), lambda i, j: (i, 0, j)),
        out_shape=jax.ShapeDtypeStruct((n_mod, B, 3 * D_MODEL), F32),
        compiler_params=pltpu.CompilerParams(dimension_semantics=("arbitrary", "arbitrary"),
                                             vmem_limit_bytes=VMEM_LIMIT),
        name="adaln",
    )(c, w, b)


def _mod_spec(i, part, nb):
    return pl.BlockSpec((1, 1, D_MODEL), lambda b, t: (i * nb + b, 0, part))


def _modulate_kernel(x_ref, shift_ref, scale_ref, h_ref):
    h_ref[0] = (x_ref[0] * (1.0 + scale_ref[0]) + shift_ref[0]).astype(h_ref.dtype)


def _modulate_call(x, mods3, i, *, tb=512):
    B, T, _ = x.shape
    blk = pl.BlockSpec((1, tb, D_MODEL), lambda b, t: (b, t, 0))
    return pl.pallas_call(
        _modulate_kernel,
        grid=(B, T // tb),
        in_specs=[blk, _mod_spec(i, 0, B), _mod_spec(i, 1, B)],
        out_specs=blk,
        out_shape=jax.ShapeDtypeStruct(x.shape, BF16),
        compiler_params=pltpu.CompilerParams(dimension_semantics=("arbitrary", "arbitrary"),
                                             vmem_limit_bytes=VMEM_LIMIT),
        name="modulate",
    )(x, mods3, mods3)


def _ln_kernel(x_ref, y_ref, gate_ref, g_ref, b_ref, *rest, with_next):
    z = ALPHA * x_ref[0] + (1.0 + gate_ref[0]) * y_ref[0]
    mu = jnp.mean(z, axis=-1, keepdims=True)
    zc = z - mu
    var = jnp.mean(zc * zc, axis=-1, keepdims=True)
    xn = zc * lax.rsqrt(var + LN_EPS) * g_ref[...] + b_ref[...]
    if with_next:
        shift_ref, scale_ref, xo_ref, h_ref = rest
        xo_ref[0] = xn
        h_ref[0] = (xn * (1.0 + scale_ref[0]) + shift_ref[0]).astype(h_ref.dtype)
    else:
        (xo_ref,) = rest
        xo_ref[0] = xn


def _ln_call(x, y, mods3, i, g, b, *, with_next, tb=256):
    B, T, _ = x.shape
    blk = pl.BlockSpec((1, tb, D_MODEL), lambda b, t: (b, t, 0))
    row = pl.BlockSpec((1, D_MODEL), lambda b, t: (0, 0))
    in_specs = [blk, blk, _mod_spec(i, 2, B), row, row]
    args = [x, y, mods3, g.reshape(1, D_MODEL), b.reshape(1, D_MODEL)]
    out_specs = [blk]
    out_shape = [jax.ShapeDtypeStruct(x.shape, F32)]
    if with_next:
        in_specs += [_mod_spec(i + 1, 0, B), _mod_spec(i + 1, 1, B)]
        args += [mods3, mods3]
        out_specs.append(blk)
        out_shape.append(jax.ShapeDtypeStruct(x.shape, BF16))
    return pl.pallas_call(
        functools.partial(_ln_kernel, with_next=with_next),
        grid=(B, T // tb),
        in_specs=in_specs, out_specs=out_specs, out_shape=out_shape,
        compiler_params=pltpu.CompilerParams(dimension_semantics=("arbitrary", "arbitrary"),
                                             vmem_limit_bytes=VMEM_LIMIT),
        name="residual_ln",
    )(*args)


def _matmul_kernel(a_ref, b_ref, o_ref, *scratch, nk):
    if nk == 1:
        o_ref[...] = jnp.dot(a_ref[...], b_ref[...], preferred_element_type=F32).astype(o_ref.dtype)
        return
    (acc_ref,) = scratch
    k = pl.program_id(2)

    @pl.when(k == 0)
    def _():
        acc_ref[...] = jnp.zeros_like(acc_ref)

    acc_ref[...] += jnp.dot(a_ref[...], b_ref[...], preferred_element_type=F32)

    @pl.when(k == nk - 1)
    def _():
        o_ref[...] = acc_ref[...].astype(o_ref.dtype)


def _matmul_call(a, b, *, tm, tn, tk, out_dtype=F32):
    M, K = a.shape
    _, N = b.shape
    nk = K // tk
    return pl.pallas_call(
        functools.partial(_matmul_kernel, nk=nk),
        grid=(N // tn, M // tm, nk),
        in_specs=[pl.BlockSpec((tm, tk), lambda j, i, k: (i, k)),
                  pl.BlockSpec((tk, tn), lambda j, i, k: (k, j))],
        out_specs=pl.BlockSpec((tm, tn), lambda j, i, k: (i, j)),
        out_shape=jax.ShapeDtypeStruct((M, N), out_dtype),
        scratch_shapes=[] if nk == 1 else [pltpu.VMEM((tm, tn), F32)],
        compiler_params=pltpu.CompilerParams(dimension_semantics=("arbitrary", "arbitrary", "arbitrary"),
                                             vmem_limit_bytes=VMEM_LIMIT),
        name="matmul",
    )(a, b)


def _swiglu_up_kernel(h_ref, wg_ref, wu_ref, o_ref):
    h = h_ref[...]
    g = jnp.dot(h, wg_ref[...], preferred_element_type=F32)
    u = jnp.dot(h, wu_ref[...], preferred_element_type=F32)
    o_ref[...] = (_silu(g) * u).astype(o_ref.dtype)


def _swiglu_up_call(h, wg, wu, *, tm=512, tn=1408):
    M, K = h.shape
    _, N = wg.shape
    wspec = pl.BlockSpec((K, tn), lambda j, i: (0, j))
    return pl.pallas_call(
        _swiglu_up_kernel,
        grid=(N // tn, M // tm),
        in_specs=[pl.BlockSpec((tm, K), lambda j, i: (i, 0)), wspec, wspec],
        out_specs=pl.BlockSpec((tm, tn), lambda j, i: (i, j)),
        out_shape=jax.ShapeDtypeStruct((M, N), BF16),
        compiler_params=pltpu.CompilerParams(dimension_semantics=("arbitrary", "arbitrary"),
                                             vmem_limit_bytes=VMEM_LIMIT),
        name="swiglu_up",
    )(h, wg, wu)


AB_COLS = 7680
CD_COLS = 8320
RWKV_COL0 = 4 * D_GROUP
MLSTM_GATE_COL = 7 * D_GROUP + RWKV_LOW
GDN_COL0 = 4 * D_GROUP


def _pad_cols(w, n):
    return jnp.pad(w, ((0, 0), (0, n - w.shape[1])))


def _ab_weight(w_in):
    a_main, a_gates = w_in[:, :4 * D_GROUP], w_in[:, 4 * D_GROUP:4 * D_GROUP + 2 * N_HEADS]
    b0 = 4 * D_GROUP + 2 * N_HEADS
    b_main, b_low = w_in[:, b0:b0 + 3 * D_GROUP], w_in[:, b0 + 3 * D_GROUP:]
    return jnp.concatenate([a_main, b_main, _pad_cols(b_low, RWKV_LOW), _pad_cols(a_gates, LANES)],
                           axis=1).astype(BF16)


def _cd_weight(w_in):
    return _pad_cols(w_in, CD_COLS).astype(BF16)


def kernel(x, c, positions, ada_w, ada_b, ln_g, ln_b, ab_w_in, ab_w_out, mlstm_conv_w, mlstm_gate_b, mlstm_norm_g, rwkv_mu, rwkv_w0, rwkv_w2, rwkv_a0, rwkv_a2, rwkv_g2, rwkv_k_k, rwkv_k_a, rwkv_r_k, rwkv_ln_g, rwkv_ln_b, cd_w_in, cd_w_out, ret_norm_g, gdn_conv_w, gdn_a_log, gdn_dt_bias, gdn_norm_g, ffn_w_gate, ffn_w_up, ffn_w_down):
    B, T, D = x.shape
    M = B * T
    depth = ada_w.shape[0]
    mods = _ada_call(c, ada_w, ada_b)
    mods3 = mods.reshape(2 * depth * B, 1, 3 * D)
    cos2, sin2 = _rope_call(positions)
    h = _modulate_call(x, mods3, 0)
    for layer in range(depth):
        j = layer // 2
        i_mix, i_ffn = 2 * layer, 2 * layer + 1
        if layer % 2 == 0:
            proj = _matmul_call(h.reshape(M, D), _ab_weight(ab_w_in[j]), tm=512, tn=1280, tk=D)
            proj = proj.reshape(B, T, AB_COLS)
            ya = _mlstm_call(proj, mlstm_conv_w[j], mlstm_gate_b[j], mlstm_norm_g[j], gate_col=MLSTM_GATE_COL)
            yb = _rwkv_call(proj, rwkv_mu[j], rwkv_w0[j], rwkv_w2[j], rwkv_a0[j], rwkv_a2[j], rwkv_g2[j],
                            rwkv_k_k[j], rwkv_k_a[j], rwkv_r_k[j], rwkv_ln_g[j], rwkv_ln_b[j], col0=RWKV_COL0)
            w_out = ab_w_out[j]
        else:
            proj = _matmul_call(h.reshape(M, D), _cd_weight(cd_w_in[j]), tm=512, tn=1664, tk=D)
            proj = proj.reshape(B, T, CD_COLS)
            ya = _ret_call(proj, cos2, sin2, ret_norm_g[j])
            yb = _gdn_call(proj, gdn_conv_w[j], gdn_a_log[j], gdn_dt_bias[j], gdn_norm_g[j], col0=GDN_COL0)
            w_out = cd_w_out[j]
        ycat = jnp.concatenate([ya, yb], axis=-1).reshape(M, D)
        y = _matmul_call(ycat, w_out.astype(BF16), tm=512, tn=D, tk=D).reshape(B, T, D)
        x, h = _ln_call(x, y, mods3, i_mix, ln_g[layer, 0], ln_b[layer, 0], with_next=True)
        act = _swiglu_up_call(h.reshape(M, D), ffn_w_gate[layer].astype(BF16), ffn_w_up[layer].astype(BF16))
        y = _matmul_call(act, ffn_w_down[layer].astype(BF16), tm=512, tn=D, tk=1408).reshape(B, T, D)
        last = layer == depth - 1
        res = _ln_call(x, y, mods3, i_ffn, ln_g[layer, 1], ln_b[layer, 1], with_next=not last)
        if last:
            (x,) = res
        else:
            x, h = res
    return x
```

```python
import functools
import math

import numpy as np
import jax
import jax.numpy as jnp
from jax import lax
from jax.experimental import pallas as pl
from jax.experimental.pallas import tpu as pltpu

F32 = jnp.float32
BF16 = jnp.bfloat16

D_MODEL = 2048
D_GROUP = 1024
HEAD_DIM = 128
N_HEADS = 8
RWKV_HEAD = 64
N_RWKV = 16
CHUNK = 64
D_FF = 5632
DEPTH = 2
ALPHA = (2 * DEPTH) ** 0.25
LN_EPS = 1e-5
RWKV_LN_EPS = 64e-5
ROPE_BASE = 10000.0
RET_GAMMA_BASE = 5.0
LANES = 128
VMEM_LIMIT = 48 * 1024 * 1024


def _dot(a, b):
    return jnp.dot(a.astype(BF16), b.astype(BF16), preferred_element_type=F32)


def _dot_nt(a, b):
    return lax.dot_general(a.astype(BF16), b.astype(BF16), (((1,), (1,)), ((), ())),
                           preferred_element_type=F32)


def _split3(x):
    hi = x.astype(BF16)
    r1 = x - hi.astype(F32)
    mid = r1.astype(BF16)
    lo = (r1 - mid.astype(F32)).astype(BF16)
    return hi, mid, lo


def _cumsum_rows(x):
    n = x.shape[1]
    out = jnp.dot(_tri(CHUNK).astype(BF16), jnp.concatenate(_split3(x), axis=1), preferred_element_type=F32)
    return out[:, :n] + out[:, n:2 * n] + out[:, 2 * n:]


def _group_sum(x, ones01):
    m = x.shape[0]
    out = jnp.dot(jnp.concatenate(_split3(x), axis=0), ones01.astype(BF16), preferred_element_type=F32)
    return out[:m] + out[m:2 * m] + out[2 * m:]


def _sigmoid(x):
    return 1.0 / (1.0 + jnp.exp(-x))


def _silu(x):
    return x * _sigmoid(x)


def _log_sigmoid(x):
    return jnp.minimum(x, 0.0) - jnp.log1p(jnp.exp(-jnp.abs(x)))


def _softplus(x):
    return jnp.maximum(x, 0.0) + jnp.log1p(jnp.exp(-jnp.abs(x)))


def _tri(n, strict=False):
    r = lax.broadcasted_iota(jnp.int32, (n, n), 0)
    c = lax.broadcasted_iota(jnp.int32, (n, n), 1)
    return (r > c) if strict else (r >= c)


def _conv_silu_rows(src_ref, halo_ref, w, first, r0, nrows, c0, ncols):
    cur = src_ref[0, r0:r0 + nrows, c0:c0 + ncols]
    acc = w[3:4] * cur
    if r0 == 0:
        hl = jnp.where(first, 0.0, halo_ref[0, :, c0:c0 + ncols])
        ext = jnp.concatenate([hl, cur[0:8]], axis=0)
        for j in range(3):
            head = ext[5 + j:13 + j]
            if nrows > 8:
                rest = src_ref[0, 5 + j:nrows - 3 + j, c0:c0 + ncols]
                sh = jnp.concatenate([head, rest], axis=0)
            else:
                sh = head
            acc = acc + w[j:j + 1] * sh
    else:
        for j in range(3):
            acc = acc + w[j:j + 1] * src_ref[0, r0 - 3 + j:r0 - 3 + j + nrows, c0:c0 + ncols]
    return _silu(acc)


def _head_norm_rows(h, g_row, eps, center=True):
    if center:
        h = h - jnp.mean(h, axis=-1, keepdims=True)
    return h * lax.rsqrt(jnp.mean(h * h, axis=-1, keepdims=True) + eps) * g_row


def _mlstm_kernel(q_ref, k_ref, v_ref, o_ref, g_ref, qh_ref, kh_ref, cw_ref, gb_ref, ng_ref,
                  out_ref, qc_ref, kc_ref, C_ref, n_ref, m_ref, *, tb):
    t = pl.program_id(1)
    first = t == 0

    @pl.when(first)
    def _():
        C_ref[...] = jnp.zeros_like(C_ref)
        n_ref[...] = jnp.zeros_like(n_ref)
        m_ref[...] = jnp.zeros_like(m_ref)

    for c in range(tb // CHUNK):
        for cb in range(D_GROUP // 256):
            cs = cb * 256
            qc_ref[c * CHUNK:(c + 1) * CHUNK, cs:cs + 256] = _conv_silu_rows(
                q_ref, qh_ref, cw_ref[:, cs:cs + 256], first, c * CHUNK, CHUNK, cs, 256)
            kc_ref[c * CHUNK:(c + 1) * CHUNK, cs:cs + 256] = _conv_silu_rows(
                k_ref, kh_ref, cw_ref[:, D_GROUP + cs:D_GROUP + cs + 256], first, c * CHUNK, CHUNK, cs, 256)

    causal = _tri(CHUNK)
    gb = gb_ref[...]
    scale = HEAD_DIM ** -0.5

    def chunk_body(c, carry):
        r0 = pl.multiple_of(c * CHUNK, CHUNK)
        rows = pl.ds(r0, CHUNK)
        z = g_ref[0, rows, :] + gb
        bb = _cumsum_rows(_log_sigmoid(z))
        zT = z.T
        bT = bb.T
        heads = range(N_HEADS)
        hs = [slice(h * HEAD_DIM, (h + 1) * HEAD_DIM) for h in heads]
        ig_col = [z[:, h:h + 1] for h in heads]
        b_col = [bb[:, 8 + h:9 + h] for h in heads]
        m = [m_ref[h:h + 1, 0:1] for h in heads]
        q = [qc_ref[rows, hs[h]] for h in heads]
        k = [kc_ref[rows, hs[h]] * scale for h in heads]
        v = [v_ref[0, rows, hs[h]] for h in heads]
        Cst = [C_ref[h] for h in heads]
        nst = [n_ref[h:h + 1, :] for h in heads]
        qk = [_dot_nt(q[h], k[h]) for h in heads]
        qC = [_dot(q[h], Cst[h]) for h in heads]
        dlog = [jnp.where(causal, b_col[h] - bT[8 + h:9 + h, :] + zT[h:h + 1, :], -1e30) for h in heads]
        inter = [b_col[h] + m[h] for h in heads]
        mt = [jnp.maximum(jnp.max(dlog[h], axis=-1, keepdims=True), inter[h]) for h in heads]
        s = [qk[h] * jnp.where(causal, jnp.exp(dlog[h] - mt[h]), 0.0) for h in heads]
        sc = [jnp.exp(inter[h] - mt[h]) for h in heads]
        sv = [_dot(s[h], v[h]) for h in heads]
        bl = [bb[CHUNK - 1:CHUNK, 8 + h:9 + h] for h in heads]
        lw = [bl[h] - b_col[h] + ig_col[h] for h in heads]
        m_new = [jnp.maximum(bl[h] + m[h], jnp.max(lw[h], axis=0, keepdims=True)) for h in heads]
        kw = [k[h] * jnp.exp(lw[h] - m_new[h]) for h in heads]
        dec = [jnp.exp(bl[h] + m[h] - m_new[h]) for h in heads]
        kv = [_dot(kw[h].T, v[h]) for h in heads]
        for h in heads:
            C_ref[h] = dec[h] * Cst[h] + kv[h]
            n_ref[h:h + 1, :] = dec[h] * nst[h] + jnp.sum(kw[h], axis=0, keepdims=True)
            m_ref[h:h + 1, :] = jnp.broadcast_to(m_new[h], (1, LANES))
        for h in heads:
            num = sv[h] + sc[h] * qC[h]
            den = jnp.sum(s[h], axis=-1, keepdims=True) + sc[h] * jnp.sum(q[h] * nst[h], axis=-1, keepdims=True)
            hh = num / jnp.maximum(jnp.abs(den), jnp.exp(-mt[h]))
            hn = _head_norm_rows(hh, ng_ref[:, hs[h]], LN_EPS)
            out_ref[0, rows, hs[h]] = (hn * _sigmoid(o_ref[0, rows, hs[h]])).astype(out_ref.dtype)
        return carry

    lax.fori_loop(0, tb // CHUNK, chunk_body, 0)


def _mlstm_call(proj, conv_w, gate_b, norm_g, *, gate_col, tb=256):
    B, T, _ = proj.shape
    nt = T // tb
    gb = jnp.zeros((1, LANES), F32).at[0, :2 * N_HEADS].set(gate_b)
    ng = norm_g.reshape(1, D_GROUP)
    colblk = lambda j: pl.BlockSpec((1, tb, D_GROUP), lambda b, t: (b, t, j))
    halo = lambda j: pl.BlockSpec((1, 8, D_GROUP), lambda b, t: (b, jnp.maximum(t * (tb // 8) - 1, 0), j))
    return pl.pallas_call(
        functools.partial(_mlstm_kernel, tb=tb),
        grid=(B, nt),
        in_specs=[colblk(0), colblk(1), colblk(2), colblk(3),
                  pl.BlockSpec((1, tb, LANES), lambda b, t: (b, t, gate_col // LANES)),
                  halo(0), halo(1),
                  pl.BlockSpec((4, 2 * D_GROUP), lambda b, t: (0, 0)),
                  pl.BlockSpec((1, LANES), lambda b, t: (0, 0)),
                  pl.BlockSpec((1, D_GROUP), lambda b, t: (0, 0))],
        out_specs=pl.BlockSpec((1, tb, D_GROUP), lambda b, t: (b, t, 0)),
        out_shape=jax.ShapeDtypeStruct((B, T, D_GROUP), BF16),
        scratch_shapes=[pltpu.VMEM((tb, D_GROUP), F32), pltpu.VMEM((tb, D_GROUP), F32),
                        pltpu.VMEM((N_HEADS, HEAD_DIM, HEAD_DIM), F32),
                        pltpu.VMEM((N_HEADS, LANES), F32), pltpu.VMEM((N_HEADS, LANES), F32)],
        compiler_params=pltpu.CompilerParams(dimension_semantics=("arbitrary", "arbitrary"),
                                             vmem_limit_bytes=VMEM_LIMIT),
        name="mlstm",
    )(proj, proj, proj, proj, proj, proj, proj, conv_w, gb, ng)


def _rope_kernel(pos_ref, inv_ref, cos_ref, sin_ref):
    ang = pos_ref[0].astype(F32) * inv_ref[...]
    lane = lax.broadcasted_iota(jnp.int32, ang.shape, 1)
    cos_ref[0] = jnp.cos(ang)
    sin_ref[0] = jnp.where(lane < HEAD_DIM // 2, -jnp.sin(ang), jnp.sin(ang))


def _rope_call(positions, *, tb=512):
    B, T = positions.shape
    half = HEAD_DIM // 2
    inv_freq = ROPE_BASE ** (-jnp.arange(half, dtype=F32) / half)
    inv2 = jnp.concatenate([inv_freq, inv_freq]).reshape(1, HEAD_DIM)
    spec = pl.BlockSpec((1, tb, HEAD_DIM), lambda b, t: (b, t, 0))
    return pl.pallas_call(
        _rope_kernel,
        grid=(B, T // tb),
        in_specs=[pl.BlockSpec((1, tb, 1), lambda b, t: (b, t, 0)),
                  pl.BlockSpec((1, HEAD_DIM), lambda b, t: (0, 0))],
        out_specs=[spec, spec],
        out_shape=[jax.ShapeDtypeStruct((B, T, HEAD_DIM), F32)] * 2,
        compiler_params=pltpu.CompilerParams(dimension_semantics=("arbitrary", "arbitrary")),
        name="rope_table",
    )(positions.reshape(B, T, 1), inv2)


def _ret_kernel(q_ref, k_ref, v_ref, g_ref, cos_ref, sin_ref, ng_ref, out_ref, R_ref, *, tb):
    t = pl.program_id(1)

    @pl.when(t == 0)
    def _():
        R_ref[...] = jnp.zeros_like(R_ref)

    causal = _tri(CHUNK)
    ri = lax.broadcasted_iota(jnp.int32, (CHUNK, CHUNK), 0)
    ci = lax.broadcasted_iota(jnp.int32, (CHUNK, CHUNK), 1)
    rel = (ri - ci).astype(F32)
    tcol = lax.broadcasted_iota(jnp.int32, (CHUNK, 1), 0).astype(F32)
    scale = HEAD_DIM ** -0.5

    def chunk_body(c, carry):
        r0 = pl.multiple_of(c * CHUNK, CHUNK)
        rows = pl.ds(r0, CHUNK)
        cos2 = cos_ref[0, rows, :]
        sin2 = sin_ref[0, rows, :]
        for h in range(N_HEADS):
            hs = slice(h * HEAD_DIM, (h + 1) * HEAD_DIM)
            lg = math.log1p(-2.0 ** (-RET_GAMMA_BASE - h))
            q = q_ref[0, rows, hs]
            k = k_ref[0, rows, hs]
            v = v_ref[0, rows, hs]
            qr = q * cos2 + pltpu.roll(q, HEAD_DIM // 2, 1) * sin2
            kr = (k * cos2 + pltpu.roll(k, HEAD_DIM // 2, 1) * sin2) * scale
            dmat = jnp.where(causal, jnp.exp(rel * lg), 0.0)
            xi = jnp.exp((tcol + 1.0) * lg)
            zeta = jnp.exp((CHUNK - 1.0 - tcol) * lg)
            Rst = R_ref[h]
            intra = _dot(_dot_nt(qr, kr) * dmat, v)
            inter = _dot(qr, Rst) * xi
            R_ref[h] = Rst * math.exp(CHUNK * lg) + _dot((kr * zeta).T, v)
            on = _head_norm_rows(intra + inter, ng_ref[:, hs], LN_EPS)
            out_ref[0, rows, hs] = (on * _silu(g_ref[0, rows, hs])).astype(out_ref.dtype)
        return carry

    lax.fori_loop(0, tb // CHUNK, chunk_body, 0)


def _ret_call(proj, cos2, sin2, norm_g, *, tb=256):
    B, T, _ = proj.shape
    colblk = lambda j: pl.BlockSpec((1, tb, D_GROUP), lambda b, t: (b, t, j))
    tab = pl.BlockSpec((1, tb, HEAD_DIM), lambda b, t: (b, t, 0))
    return pl.pallas_call(
        functools.partial(_ret_kernel, tb=tb),
        grid=(B, T // tb),
        in_specs=[colblk(0), colblk(1), colblk(2), colblk(3), tab, tab,
                  pl.BlockSpec((1, D_GROUP), lambda b, t: (0, 0))],
        out_specs=pl.BlockSpec((1, tb, D_GROUP), lambda b, t: (b, t, 0)),
        out_shape=jax.ShapeDtypeStruct((B, T, D_GROUP), BF16),
        scratch_shapes=[pltpu.VMEM((N_HEADS, HEAD_DIM, HEAD_DIM), F32)],
        compiler_params=pltpu.CompilerParams(dimension_semantics=("arbitrary", "arbitrary"),
                                             vmem_limit_bytes=VMEM_LIMIT),
        name="retention",
    )(proj, proj, proj, proj, cos2, sin2, norm_g.reshape(1, D_GROUP))


def _solve_unit_lower(nms, rhss):
    n = nms[0].shape[0]
    eye = (lax.broadcasted_iota(jnp.int32, (n, n), 0) == lax.broadcasted_iota(jnp.int32, (n, n), 1)).astype(F32)
    ps = [eye + nm for nm in nms]
    xs = list(nms)
    for _ in range(int(math.log2(n)) - 1):
        xs = [_dot(x, x) for x in xs]
        ps = [p + _dot(p, x) for p, x in zip(ps, xs)]
    x0 = [_dot(p, r) for p, r in zip(ps, rhss)]
    resid = [r - a + _dot(nm, a) for r, a, nm in zip(rhss, x0, nms)]
    return [a + _dot(p, r) for a, p, r in zip(x0, ps, resid)]


def _l2norm_rows(z):
    return z * lax.rsqrt(jnp.sum(z * z, axis=-1, keepdims=True) + 1e-6)


def _gdn_kernel(q_ref, k_ref, v_ref, z_ref, g_ref, qh_ref, kh_ref, vh_ref, cw_ref, an_ref, dt_ref, ng_ref,
                out_ref, qc_ref, kc_ref, vc_ref, S_ref, *, tb):
    t = pl.program_id(1)
    first = t == 0

    @pl.when(first)
    def _():
        S_ref[...] = jnp.zeros_like(S_ref)

    srcs = ((q_ref, qh_ref, qc_ref), (k_ref, kh_ref, kc_ref), (v_ref, vh_ref, vc_ref))
    for c in range(tb // CHUNK):
        for cb in range(D_GROUP // 256):
            cs = cb * 256
            for i, (src, halo, dst) in enumerate(srcs):
                w = cw_ref[:, i * D_GROUP + cs:i * D_GROUP + cs + 256]
                dst[c * CHUNK:(c + 1) * CHUNK, cs:cs + 256] = _conv_silu_rows(
                    src, halo, w, first, c * CHUNK, CHUNK, cs, 256)

    causal = _tri(CHUNK)
    strict = _tri(CHUNK, strict=True)
    a_neg = an_ref[...]
    dtb = dt_ref[...]
    scale = HEAD_DIM ** -0.5

    def chunk_body(c, carry):
        r0 = pl.multiple_of(c * CHUNK, CHUNK)
        rows = pl.ds(r0, CHUNK)
        gz = g_ref[0, rows, :]
        beta = _sigmoid(gz)
        gc = _cumsum_rows(a_neg * _softplus(gz + dtb))
        gcT = gc.T
        heads = range(N_HEADS)
        hs = [slice(h * HEAD_DIM, (h + 1) * HEAD_DIM) for h in heads]
        gc_col = [gc[:, h:h + 1] for h in heads]
        b_col = [beta[:, 8 + h:9 + h] for h in heads]
        gamma = [jnp.where(causal, jnp.exp(gc_col[h] - gcT[h:h + 1, :]), 0.0) for h in heads]
        q = [_l2norm_rows(qc_ref[rows, hs[h]]) * scale for h in heads]
        k = [_l2norm_rows(kc_ref[rows, hs[h]]) for h in heads]
        kb = [k[h] * b_col[h] for h in heads]
        eg = [jnp.exp(gc_col[h]) for h in heads]
        kq = [_dot_nt(jnp.concatenate([kb[h], q[h]], axis=0), k[h]) for h in heads]
        a_mat = [jnp.where(strict, kq[h][:CHUNK] * gamma[h], 0.0) for h in heads]
        att = [kq[h][CHUNK:] * gamma[h] for h in heads]
        rhs = [jnp.concatenate([vc_ref[rows, hs[h]] * b_col[h], kb[h] * eg[h]], axis=1) for h in heads]
        uw = _solve_unit_lower([-a for a in a_mat], rhs)
        S = [S_ref[h] for h in heads]
        ws = [_dot(jnp.concatenate([uw[h][:, HEAD_DIM:], q[h] * eg[h]], axis=0), S[h]) for h in heads]
        v_new = [uw[h][:, :HEAD_DIM] - ws[h][:CHUNK] for h in heads]
        o = [ws[h][CHUNK:] + _dot(att[h], v_new[h]) for h in heads]
        g_last = [gc[CHUNK - 1:CHUNK, h:h + 1] for h in heads]
        kd = [(k[h] * jnp.exp(g_last[h] - gc_col[h])).T for h in heads]
        for h in heads:
            S_ref[h] = S[h] * jnp.exp(g_last[h]) + _dot(kd[h], v_new[h])
        for h in heads:
            on = _head_norm_rows(o[h], ng_ref[:, hs[h]], 1e-6, center=False)
            out_ref[0, rows, hs[h]] = (on * _silu(z_ref[0, rows, hs[h]])).astype(out_ref.dtype)
        return carry

    lax.fori_loop(0, tb // CHUNK, chunk_body, 0)


def _gdn_call(proj, conv_w, a_log, dt_bias, norm_g, *, col0, tb=256):
    B, T, _ = proj.shape
    j0 = col0 // D_GROUP
    an = jnp.zeros((1, LANES), F32).at[0, :N_HEADS].set(-jnp.exp(a_log.astype(F32)))
    dtb = jnp.zeros((1, LANES), F32).at[0, :N_HEADS].set(dt_bias)
    colblk = lambda j: pl.BlockSpec((1, tb, D_GROUP), lambda b, t: (b, t, j0 + j))
    halo = lambda j: pl.BlockSpec((1, 8, D_GROUP), lambda b, t: (b, jnp.maximum(t * (tb // 8) - 1, 0), j0 + j))
    return pl.pallas_call(
        functools.partial(_gdn_kernel, tb=tb),
        grid=(B, T // tb),
        in_specs=[colblk(0), colblk(1), colblk(2), colblk(3),
                  pl.BlockSpec((1, tb, LANES), lambda b, t: (b, t, (col0 + 4 * D_GROUP) // LANES)),
                  halo(0), halo(1), halo(2),
                  pl.BlockSpec((4, 3 * D_GROUP), lambda b, t: (0, 0)),
                  pl.BlockSpec((1, LANES), lambda b, t: (0, 0)),
                  pl.BlockSpec((1, LANES), lambda b, t: (0, 0)),
                  pl.BlockSpec((1, D_GROUP), lambda b, t: (0, 0))],
        out_specs=pl.BlockSpec((1, tb, D_GROUP), lambda b, t: (b, t, 0)),
        out_shape=jax.ShapeDtypeStruct((B, T, D_GROUP), BF16),
        scratch_shapes=[pltpu.VMEM((tb, D_GROUP), F32), pltpu.VMEM((tb, D_GROUP), F32),
                        pltpu.VMEM((tb, D_GROUP), F32),
                        pltpu.VMEM((N_HEADS, HEAD_DIM, HEAD_DIM), F32)],
        compiler_params=pltpu.CompilerParams(dimension_semantics=("arbitrary", "arbitrary"),
                                             vmem_limit_bytes=VMEM_LIMIT),
        name="gdn",
    )(proj, proj, proj, proj, proj, proj, proj, proj, conv_w, an, dtb, norm_g.reshape(1, D_GROUP))


N_PAIRS = N_RWKV // 2
RWKV_LOW = 384


def _shift1_rows(src_ref, halo_ref, first, r0, nrows, c0, ncols):
    if r0 == 0:
        hl = jnp.where(first, 0.0, halo_ref[0, 7:8, c0:c0 + ncols])
        return jnp.concatenate([hl, src_ref[0, 0:nrows - 1, c0:c0 + ncols]], axis=0)
    return src_ref[0, r0 - 1:r0 - 1 + nrows, c0:c0 + ncols]


def _rwkv_kernel(r_ref, k_ref, v_ref, l0_ref, l1_ref, l2_ref,
                 rh_ref, kh_ref, vh_ref, l0h_ref, l1h_ref, l2h_ref,
                 mu_ref, mul_ref, w0_ref, w2_ref, a0_ref, a2_ref, g2_ref, kk_ref, ka_ref, rk_ref,
                 lng_ref, lnb_ref, out_ref,
                 rs_ref, ks_ref, vs_ref, lw_ref, an_ref, bn_ref, gs_ref, bo_ref, y_ref, S_ref, *, tb):
    t = pl.program_id(1)
    first = t == 0

    @pl.when(first)
    def _():
        S_ref[...] = jnp.zeros_like(S_ref)

    ri = lax.broadcasted_iota(jnp.int32, (LANES, LANES), 0)
    ci = lax.broadcasted_iota(jnp.int32, (LANES, LANES), 1)
    same_head = ((ri // RWKV_HEAD) == (ci // RWKV_HEAD)).astype(F32)

    def lerp(src, halo, mu, r0, c0, ncols):
        cur = src[0, r0:r0 + CHUNK, c0:c0 + ncols]
        return cur + (_shift1_rows(src, halo, first, r0, CHUNK, c0, ncols) - cur) * mu

    for c in range(tb // CHUNK):
        r0 = c * CHUNK
        wl = lerp(l0_ref, l0h_ref, mul_ref[:, 0:LANES], r0, 0, LANES)
        g1 = lerp(l1_ref, l1h_ref, mul_ref[:, LANES:2 * LANES], r0, 0, LANES)
        g2 = lerp(l2_ref, l2h_ref, mul_ref[:, 2 * LANES:3 * LANES], r0, 0, LANES)
        lane = lax.broadcasted_iota(jnp.int32, wl.shape, 1)
        wl_t = jnp.where(lane < 64, jnp.tanh(wl), 0.0)
        al = jnp.where(lane >= 64, wl, 0.0)
        sg1 = _sigmoid(g1)
        sg2 = jnp.where(lane < 32, _sigmoid(g2), 0.0)
        for p in range(N_PAIRS):
            ps = slice(p * LANES, (p + 1) * LANES)
            wpre = w0_ref[:, ps] + _dot(wl_t, w2_ref[:, ps])
            w = -_softplus(-wpre) - 0.5
            a = _sigmoid(a0_ref[:, ps] + _dot(al, a2_ref[:, ps]))
            g = _dot(sg1, g2_ref[0:LANES, ps]) + _dot(sg2, g2_ref[LANES:2 * LANES, ps])
            r = lerp(r_ref, rh_ref, mu_ref[:, ps], r0, p * LANES, LANES)
            k = lerp(k_ref, kh_ref, mu_ref[:, D_GROUP + p * LANES:D_GROUP + (p + 1) * LANES], r0, p * LANES, LANES)
            v = lerp(v_ref, vh_ref, mu_ref[:, 2 * D_GROUP + p * LANES:2 * D_GROUP + (p + 1) * LANES], r0, p * LANES, LANES)
            kk = k * kk_ref[:, ps]
            nrm = jnp.sqrt(_group_sum(kk * kk, same_head))
            kk = kk / jnp.maximum(nrm, 1e-12)
            k2 = k * (1.0 + (a - 1.0) * ka_ref[:, ps])
            rows = slice(r0, r0 + CHUNK)
            rs_ref[p, rows, :] = r
            ks_ref[p, rows, :] = k2
            vs_ref[p, rows, :] = v
            lw_ref[p, rows, :] = -jnp.exp(w)
            an_ref[p, rows, :] = -kk
            bn_ref[p, rows, :] = kk * a
            gs_ref[p, rows, :] = g
            bo_ref[p, rows, :] = _group_sum(r * k2 * rk_ref[:, ps], same_head) * v

    strict = _tri(CHUNK, strict=True)
    lane1 = lax.broadcasted_iota(jnp.int32, (1, LANES), 1)
    m0 = (lane1 < RWKV_HEAD).astype(F32)
    m1 = 1.0 - m0
    t2 = lax.broadcasted_iota(jnp.int32, (CHUNK, LANES), 0)
    l2 = lax.broadcasted_iota(jnp.int32, (CHUNK, LANES), 1)
    lo = l2 < RWKV_HEAD
    s2 = l2 & (RWKV_HEAD - 1)
    causal2 = t2 >= s2
    strict_hi = (t2 > s2) & (l2 >= RWKV_HEAD)

    def chunk_body(c, carry):
        rows = pl.ds(pl.multiple_of(c * CHUNK, CHUNK), CHUNK)
        pairs = range(N_PAIRS)
        halves = [(p, hh) for p in pairs for hh in range(2)]
        lw = [lw_ref[p, rows, :] for p in pairs]
        cs = [_cumsum_rows(lw[p]) for p in pairs]
        w_inv = [jnp.exp(-cs[p]) for p in pairs]
        rt = [rs_ref[p, rows, :] * jnp.exp(cs[p]) for p in pairs]
        at = [an_ref[p, rows, :] * jnp.exp(cs[p] - lw[p]) for p in pairs]
        v = [vs_ref[p, rows, :] for p in pairs]
        bk = [jnp.concatenate([bn_ref[p, rows, :] * w_inv[p], ks_ref[p, rows, :] * w_inv[p]], axis=0).astype(BF16)
              for p in pairs]
        S = [S_ref[p] for p in pairs]
        pm = [_dot_nt(jnp.concatenate([at[p] * m0, at[p] * m1, rt[p] * m0, rt[p] * m1], axis=0), bk[p]) for p in pairs]
        xy0 = [_dot_nt(jnp.concatenate([at[p], rt[p]], axis=0), S[p]) for p in pairs]
        vv = [jnp.concatenate([v[p], v[p]], axis=0).astype(BF16) for p in pairs]
        pa = [pm[p][hh * CHUNK:(hh + 1) * CHUNK] for p, hh in halves]
        a_ab = [jnp.where(strict, a[:, :CHUNK], 0.0) for a in pa]
        x = [xy0[p][:CHUNK] + _dot(jnp.where(strict_hi, pa[2 * p + hh], 0.0), vv[p]) for p, hh in halves]
        us = _solve_unit_lower(a_ab, x)
        uv_f = [jnp.concatenate([jnp.where(lo, us[2 * p], us[2 * p + 1]), v[p]], axis=0) for p in pairs]
        uv = [a.astype(BF16) for a in uv_f]
        ys = [_dot(jnp.where(causal2, pm[p][(2 + hh) * CHUNK:(3 + hh) * CHUNK], 0.0), uv[p]) for p, hh in halves]
        for p in pairs:
            y_ref[p, rows, :] = xy0[p][CHUNK:] + jnp.where(lo, ys[2 * p], ys[2 * p + 1])
        upd = [_dot(uv_f[p].T, bk[p]) * same_head for p in pairs]
        for p in pairs:
            S_ref[p] = (S[p] + upd[p]) * jnp.exp(cs[p][CHUNK - 1:CHUNK, :])
        return carry

    lax.fori_loop(0, tb // CHUNK, chunk_body, 0)

    inv_n = 1.0 / RWKV_HEAD
    for c in range(tb // CHUNK):
        rows = slice(c * CHUNK, (c + 1) * CHUNK)
        for p in range(N_PAIRS):
            ps = slice(p * LANES, (p + 1) * LANES)
            y = y_ref[p, rows, :]
            yc = y - _group_sum(y, same_head) * inv_n
            var = _group_sum(yc * yc, same_head) * inv_n
            yn = yc * lax.rsqrt(var + RWKV_LN_EPS) * lng_ref[:, ps] + lnb_ref[:, ps]
            out_ref[0, rows, ps] = ((yn + bo_ref[p, rows, :]) * gs_ref[p, rows, :]).astype(out_ref.dtype)


def _rwkv_call(proj, mu, w0, w2, a0, a2, g2, k_k, k_a, r_k, ln_g, ln_b, *, col0, tb=256):
    B, T, _ = proj.shape
    j0 = col0 // D_GROUP
    l0 = (col0 + 3 * D_GROUP) // LANES
    row = lambda a: a.reshape(1, -1).astype(F32)
    mul = jnp.zeros((1, RWKV_LOW), F32).at[0, :288].set(mu[3 * D_GROUP:])
    w2p = jnp.zeros((LANES, D_GROUP), F32).at[:64].set(w2)
    a2p = jnp.zeros((LANES, D_GROUP), F32).at[64:].set(a2)
    g2p = jnp.zeros((2 * LANES, D_GROUP), F32).at[:160].set(g2)
    colblk = lambda j: pl.BlockSpec((1, tb, D_GROUP), lambda b, t: (b, t, j0 + j))
    lowblk = lambda j: pl.BlockSpec((1, tb, LANES), lambda b, t: (b, t, l0 + j))
    hrow = lambda t: jnp.maximum(t * (tb // 8) - 1, 0)
    halo = lambda j: pl.BlockSpec((1, 8, D_GROUP), lambda b, t: (b, hrow(t), j0 + j))
    lowhalo = lambda j: pl.BlockSpec((1, 8, LANES), lambda b, t: (b, hrow(t), l0 + j))
    full = lambda a: pl.BlockSpec(a.shape, lambda b, t: (0,) * a.ndim)
    params = [row(mu[:3 * D_GROUP]), mul, row(w0), w2p, row(a0), a2p, g2p, row(k_k), row(k_a), row(r_k),
              row(ln_g), row(ln_b)]
    big = pltpu.VMEM((N_PAIRS, tb, LANES), F32)
    return pl.pallas_call(
        functools.partial(_rwkv_kernel, tb=tb),
        grid=(B, T // tb),
        in_specs=[colblk(0), colblk(1), colblk(2), lowblk(0), lowblk(1), lowblk(2),
                  halo(0), halo(1), halo(2), lowhalo(0), lowhalo(1), lowhalo(2)] + [full(a) for a in params],
        out_specs=pl.BlockSpec((1, tb, D_GROUP), lambda b, t: (b, t, 0)),
        out_shape=jax.ShapeDtypeStruct((B, T, D_GROUP), BF16),
        scratch_shapes=[big] * 9 + [pltpu.VMEM((N_PAIRS, LANES, LANES), F32)],
        compiler_params=pltpu.CompilerParams(dimension_semantics=("arbitrary", "arbitrary"),
                                             vmem_limit_bytes=VMEM_LIMIT),
        name="rwkv7",
    )(*([proj] * 12), *params)


def _ada_kernel(c_ref, w_ref, b_ref, out_ref):
    sc = _silu(c_ref[...]).astype(BF16)
    out_ref[0] = jnp.dot(sc, w_ref[0].astype(BF16), preferred_element_type=F32) + b_ref[0]


def _ada_call(c, ada_w, ada_b, *, tn=1536):
    B = c.shape[0]
    n_mod = ada_w.shape[0] * ada_w.shape[1]
    w = ada_w.reshape(n_mod, D_MODEL, 3 * D_MODEL)
    b = ada_b.reshape(n_mod, 1, 3 * D_MODEL)
    return pl.pallas_call(
        _ada_kernel,
        grid=(n_mod, 3 * D_MODEL // tn),
        in_specs=[pl.BlockSpec((B, D_MODEL), lambda i, j: (0, 0)),
                  pl.BlockSpec((1, D_MODEL, tn), lambda i, j: (i, 0, j)),
                  pl.BlockSpec((1, 1, tn), lambda i, j: (i, 0, j))],
        out_specs=pl.BlockSpec((1, B, tn), lambda i, j: (i, 0, j)),
        out_shape=jax.ShapeDtypeStruct((n_mod, B, 3 * D_MODEL), F32),
        compiler_params=pltpu.CompilerParams(dimension_semantics=("arbitrary", "arbitrary"),
                                             vmem_limit_bytes=VMEM_LIMIT),
        name="adaln",
    )(c, w, b)


def _mod_spec(i, part, nb):
    return pl.BlockSpec((1, 1, D_MODEL), lambda b, t: (i * nb + b, 0, part))


def _modulate_kernel(x_ref, shift_ref, scale_ref, h_ref):
    h_ref[0] = (x_ref[0] * (1.0 + scale_ref[0]) + shift_ref[0]).astype(h_ref.dtype)


def _modulate_call(x, mods3, i, *, tb=512):
    B, T, _ = x.shape
    blk = pl.BlockSpec((1, tb, D_MODEL), lambda b, t: (b, t, 0))
    return pl.pallas_call(
        _modulate_kernel,
        grid=(B, T // tb),
        in_specs=[blk, _mod_spec(i, 0, B), _mod_spec(i, 1, B)],
        out_specs=blk,
        out_shape=jax.ShapeDtypeStruct(x.shape, BF16),
        compiler_params=pltpu.CompilerParams(dimension_semantics=("arbitrary", "arbitrary"),
                                             vmem_limit_bytes=VMEM_LIMIT),
        name="modulate",
    )(x, mods3, mods3)


LN_ROWS = 16


def _proj_ln_kernel(*refs, n_lhs, nk, tm, with_next):
    lhs = refs[:n_lhs]
    ws = refs[n_lhs:2 * n_lhs]
    x_ref, gate_ref, g_ref, b_ref = refs[2 * n_lhs:2 * n_lhs + 4]
    if with_next:
        shift_ref, scale_ref, xo_ref, h_ref, acc_ref = refs[2 * n_lhs + 4:]
    else:
        xo_ref, acc_ref = refs[2 * n_lhs + 4:]
    k = pl.program_id(2)
    part = jnp.dot(lhs[0][0], ws[0][...], preferred_element_type=F32)
    for j in range(1, n_lhs):
        part = part + jnp.dot(lhs[j][0], ws[j][...], preferred_element_type=F32)

    @pl.when(k == 0)
    def _():
        acc_ref[...] = part

    @pl.when(k > 0)
    def _():
        acc_ref[...] += part

    @pl.when(k == nk - 1)
    def _():
        gate1 = 1.0 + gate_ref[0]
        g, b = g_ref[...], b_ref[...]
        if with_next:
            scale1, shift = 1.0 + scale_ref[0], shift_ref[0]

        def rows_body(r, carry):
            rows = pl.ds(pl.multiple_of(r * LN_ROWS, LN_ROWS), LN_ROWS)
            z = ALPHA * x_ref[0, rows, :] + gate1 * acc_ref[rows, :]
            zc = z - jnp.mean(z, axis=-1, keepdims=True)
            var = jnp.mean(zc * zc, axis=-1, keepdims=True)
            xn = zc * lax.rsqrt(var + LN_EPS) * g + b
            xo_ref[0, rows, :] = xn
            if with_next:
                h_ref[0, rows, :] = (xn * scale1 + shift).astype(h_ref.dtype)
            return carry

        lax.fori_loop(0, tm // LN_ROWS, rows_body, 0, unroll=2)


def _proj_ln_call(lhs, ws, x, mods3, i, g, b, *, with_next, tm, tk):
    B, T, D = x.shape
    n_lhs = len(lhs)
    nk = lhs[0].shape[2] // tk
    blk = pl.BlockSpec((1, tm, D), lambda b, t, k: (b, t, 0))
    row = pl.BlockSpec((1, D), lambda b, t, k: (0, 0))
    mod = lambda ii, part: pl.BlockSpec((1, 1, D), lambda b, t, k: (ii * B + b, 0, part))
    in_specs = ([pl.BlockSpec((1, tm, tk), lambda b, t, k: (b, t, k))] * n_lhs
                + [pl.BlockSpec((tk, D), lambda b, t, k: (k, 0))] * n_lhs
                + [blk, mod(i, 2), row, row])
    args = list(lhs) + list(ws) + [x, mods3, g.reshape(1, D), b.reshape(1, D)]
    out_specs = [blk]
    out_shape = [jax.ShapeDtypeStruct(x.shape, F32)]
    if with_next:
        in_specs += [mod(i + 1, 0), mod(i + 1, 1)]
        args += [mods3, mods3]
        out_specs.append(blk)
        out_shape.append(jax.ShapeDtypeStruct(x.shape, BF16))
    return pl.pallas_call(
        functools.partial(_proj_ln_kernel, n_lhs=n_lhs, nk=nk, tm=tm, with_next=with_next),
        grid=(B, T // tm, nk),
        in_specs=in_specs, out_specs=out_specs, out_shape=out_shape,
        scratch_shapes=[pltpu.VMEM((tm, D), F32)],
        compiler_params=pltpu.CompilerParams(dimension_semantics=("arbitrary", "arbitrary", "arbitrary"),
                                             vmem_limit_bytes=VMEM_LIMIT),
        name="proj_residual_ln",
    )(*args)


def _matmul_kernel(a_ref, b_ref, o_ref, *scratch, nk):
    if nk == 1:
        o_ref[...] = jnp.dot(a_ref[...], b_ref[...], preferred_element_type=F32).astype(o_ref.dtype)
        return
    (acc_ref,) = scratch
    k = pl.program_id(2)

    @pl.when(k == 0)
    def _():
        acc_ref[...] = jnp.zeros_like(acc_ref)

    acc_ref[...] += jnp.dot(a_ref[...], b_ref[...], preferred_element_type=F32)

    @pl.when(k == nk - 1)
    def _():
        o_ref[...] = acc_ref[...].astype(o_ref.dtype)


def _matmul_call(a, b, *, tm, tn, tk, out_dtype=F32):
    M, K = a.shape
    _, N = b.shape
    nk = K // tk
    return pl.pallas_call(
        functools.partial(_matmul_kernel, nk=nk),
        grid=(N // tn, M // tm, nk),
        in_specs=[pl.BlockSpec((tm, tk), lambda j, i, k: (i, k)),
                  pl.BlockSpec((tk, tn), lambda j, i, k: (k, j))],
        out_specs=pl.BlockSpec((tm, tn), lambda j, i, k: (i, j)),
        out_shape=jax.ShapeDtypeStruct((M, N), out_dtype),
        scratch_shapes=[] if nk == 1 else [pltpu.VMEM((tm, tn), F32)],
        compiler_params=pltpu.CompilerParams(dimension_semantics=("arbitrary", "arbitrary", "arbitrary"),
                                             vmem_limit_bytes=VMEM_LIMIT),
        name="matmul",
    )(a, b)


def _swiglu_up_kernel(h_ref, wg_ref, wu_ref, o_ref, wgb_ref, wub_ref):
    @pl.when(pl.program_id(1) == 0)
    def _():
        wgb_ref[...] = wg_ref[...].astype(BF16)
        wub_ref[...] = wu_ref[...].astype(BF16)

    h = h_ref[...]
    g = jnp.dot(h, wgb_ref[...], preferred_element_type=F32)
    u = jnp.dot(h, wub_ref[...], preferred_element_type=F32)
    o_ref[...] = (_silu(g) * u).astype(o_ref.dtype)


def _swiglu_up_call(h, wg, wu, layer, *, tm=512, tn=512):
    M, K = h.shape
    _, _, N = wg.shape
    wspec = pl.BlockSpec((None, K, tn), lambda j, i: (layer, 0, j))
    return pl.pallas_call(
        _swiglu_up_kernel,
        grid=(N // tn, M // tm),
        in_specs=[pl.BlockSpec((tm, K), lambda j, i: (i, 0)), wspec, wspec],
        out_specs=pl.BlockSpec((tm, tn), lambda j, i: (i, j)),
        out_shape=jax.ShapeDtypeStruct((M, N), BF16),
        scratch_shapes=[pltpu.VMEM((K, tn), BF16), pltpu.VMEM((K, tn), BF16)],
        compiler_params=pltpu.CompilerParams(dimension_semantics=("arbitrary", "arbitrary"),
                                             vmem_limit_bytes=VMEM_LIMIT),
        name="swiglu_up",
    )(h, wg, wu)


AB_COLS = 7680
CD_COLS = 8320
RWKV_COL0 = 4 * D_GROUP
MLSTM_GATE_COL = 7 * D_GROUP + RWKV_LOW
GDN_COL0 = 4 * D_GROUP


def _pad_cols(w, n):
    return jnp.pad(w, ((0, 0), (0, n - w.shape[1])))


def _ab_weight(w_in):
    a_main, a_gates = w_in[:, :4 * D_GROUP], w_in[:, 4 * D_GROUP:4 * D_GROUP + 2 * N_HEADS]
    b0 = 4 * D_GROUP + 2 * N_HEADS
    b_main, b_low = w_in[:, b0:b0 + 3 * D_GROUP], w_in[:, b0 + 3 * D_GROUP:]
    return jnp.concatenate([a_main, b_main, _pad_cols(b_low, RWKV_LOW), _pad_cols(a_gates, LANES)],
                           axis=1).astype(BF16)


def _cd_weight(w_in):
    return _pad_cols(w_in, CD_COLS).astype(BF16)


def kernel(x, c, positions, ada_w, ada_b, ln_g, ln_b, ab_w_in, ab_w_out, mlstm_conv_w, mlstm_gate_b, mlstm_norm_g, rwkv_mu, rwkv_w0, rwkv_w2, rwkv_a0, rwkv_a2, rwkv_g2, rwkv_k_k, rwkv_k_a, rwkv_r_k, rwkv_ln_g, rwkv_ln_b, cd_w_in, cd_w_out, ret_norm_g, gdn_conv_w, gdn_a_log, gdn_dt_bias, gdn_norm_g, ffn_w_gate, ffn_w_up, ffn_w_down):
    B, T, D = x.shape
    M = B * T
    depth = ada_w.shape[0]
    mods = _ada_call(c, ada_w, ada_b)
    mods3 = mods.reshape(2 * depth * B, 1, 3 * D)
    cos2, sin2 = _rope_call(positions)
    h = _modulate_call(x, mods3, 0)
    for layer in range(depth):
        j = layer // 2
        i_mix, i_ffn = 2 * layer, 2 * layer + 1
        if layer % 2 == 0:
            proj = _matmul_call(h.reshape(M, D), _ab_weight(ab_w_in[j]), tm=512, tn=1280, tk=D)
            proj = proj.reshape(B, T, AB_COLS)
            ya = _mlstm_call(proj, mlstm_conv_w[j], mlstm_gate_b[j], mlstm_norm_g[j], gate_col=MLSTM_GATE_COL)
            yb = _rwkv_call(proj, rwkv_mu[j], rwkv_w0[j], rwkv_w2[j], rwkv_a0[j], rwkv_a2[j], rwkv_g2[j],
                            rwkv_k_k[j], rwkv_k_a[j], rwkv_r_k[j], rwkv_ln_g[j], rwkv_ln_b[j], col0=RWKV_COL0)
            w_out = ab_w_out[j]
        else:
            proj = _matmul_call(h.reshape(M, D), _cd_weight(cd_w_in[j]), tm=512, tn=1664, tk=D)
            proj = proj.reshape(B, T, CD_COLS)
            ya = _ret_call(proj, cos2, sin2, ret_norm_g[j])
            yb = _gdn_call(proj, gdn_conv_w[j], gdn_a_log[j], gdn_dt_bias[j], gdn_norm_g[j], col0=GDN_COL0)
            w_out = cd_w_out[j]
        w_out = w_out.astype(BF16)
        x, h = _proj_ln_call([ya, yb], [w_out[:D_GROUP], w_out[D_GROUP:]], x, mods3, i_mix,
                             ln_g[layer, 0], ln_b[layer, 0], with_next=True, tm=256, tk=D_GROUP)
        act = _swiglu_up_call(h.reshape(M, D), ffn_w_gate, ffn_w_up, layer)
        last = layer == depth - 1
        res = _proj_ln_call([act.reshape(B, T, D_FF)], [ffn_w_down[layer].astype(BF16)], x, mods3, i_ffn,
                            ln_g[layer, 1], ln_b[layer, 1], with_next=not last, tm=512, tk=512)
        if last:
            (x,) = res
        else:
            x, h = res
    return x
```

```python
import functools
import math

import numpy as np
import jax
import jax.numpy as jnp
from jax import lax
from jax.experimental import pallas as pl
from jax.experimental.pallas import tpu as pltpu

F32 = jnp.float32
BF16 = jnp.bfloat16

D_MODEL = 2048
D_GROUP = 1024
HEAD_DIM = 128
N_HEADS = 8
RWKV_HEAD = 64
N_RWKV = 16
CHUNK = 64
D_FF = 5632
DEPTH = 2
ALPHA = (2 * DEPTH) ** 0.25
LN_EPS = 1e-5
RWKV_LN_EPS = 64e-5
ROPE_BASE = 10000.0
RET_GAMMA_BASE = 5.0
LANES = 128
VMEM_LIMIT = 48 * 1024 * 1024


def _dot(a, b):
    return jnp.dot(a.astype(BF16), b.astype(BF16), preferred_element_type=F32)


def _dot_nt(a, b):
    return lax.dot_general(a.astype(BF16), b.astype(BF16), (((1,), (1,)), ((), ())),
                           preferred_element_type=F32)


def _split3(x):
    hi = x.astype(BF16)
    r1 = x - hi.astype(F32)
    mid = r1.astype(BF16)
    lo = (r1 - mid.astype(F32)).astype(BF16)
    return hi, mid, lo


def _cumsum_rows(x):
    n = x.shape[1]
    out = jnp.dot(_tri(CHUNK).astype(BF16), jnp.concatenate(_split3(x), axis=1), preferred_element_type=F32)
    return out[:, :n] + out[:, n:2 * n] + out[:, 2 * n:]


def _group_sum(x, ones01):
    m = x.shape[0]
    out = jnp.dot(jnp.concatenate(_split3(x), axis=0), ones01.astype(BF16), preferred_element_type=F32)
    return out[:m] + out[m:2 * m] + out[2 * m:]


def _sigmoid(x):
    return 1.0 / (1.0 + jnp.exp(-x))


def _silu(x):
    return x * _sigmoid(x)


def _log_sigmoid(x):
    return jnp.minimum(x, 0.0) - jnp.log1p(jnp.exp(-jnp.abs(x)))


def _softplus(x):
    return jnp.maximum(x, 0.0) + jnp.log1p(jnp.exp(-jnp.abs(x)))


def _tri(n, strict=False):
    r = lax.broadcasted_iota(jnp.int32, (n, n), 0)
    c = lax.broadcasted_iota(jnp.int32, (n, n), 1)
    return (r > c) if strict else (r >= c)


def _conv_silu_rows(src_ref, halo_ref, w, first, r0, nrows, c0, ncols):
    cur = src_ref[0, r0:r0 + nrows, c0:c0 + ncols]
    acc = w[3:4] * cur
    if r0 == 0:
        hl = jnp.where(first, 0.0, halo_ref[0, :, c0:c0 + ncols])
        ext = jnp.concatenate([hl, cur[0:8]], axis=0)
        for j in range(3):
            head = ext[5 + j:13 + j]
            if nrows > 8:
                rest = src_ref[0, 5 + j:nrows - 3 + j, c0:c0 + ncols]
                sh = jnp.concatenate([head, rest], axis=0)
            else:
                sh = head
            acc = acc + w[j:j + 1] * sh
    else:
        for j in range(3):
            acc = acc + w[j:j + 1] * src_ref[0, r0 - 3 + j:r0 - 3 + j + nrows, c0:c0 + ncols]
    return _silu(acc)


def _head_norm_rows(h, g_row, eps, center=True):
    if center:
        h = h - jnp.mean(h, axis=-1, keepdims=True)
    return h * lax.rsqrt(jnp.mean(h * h, axis=-1, keepdims=True) + eps) * g_row


def _mlstm_kernel(q_ref, k_ref, v_ref, o_ref, g_ref, qh_ref, kh_ref, cw_ref, gb_ref, ng_ref,
                  out_ref, qc_ref, kc_ref, C_ref, n_ref, m_ref, *, tb):
    t = pl.program_id(1)
    first = t == 0

    @pl.when(first)
    def _():
        C_ref[...] = jnp.zeros_like(C_ref)
        n_ref[...] = jnp.zeros_like(n_ref)
        m_ref[...] = jnp.zeros_like(m_ref)

    for c in range(tb // CHUNK):
        for cb in range(D_GROUP // 256):
            cs = cb * 256
            qc_ref[c * CHUNK:(c + 1) * CHUNK, cs:cs + 256] = _conv_silu_rows(
                q_ref, qh_ref, cw_ref[:, cs:cs + 256], first, c * CHUNK, CHUNK, cs, 256)
            kc_ref[c * CHUNK:(c + 1) * CHUNK, cs:cs + 256] = _conv_silu_rows(
                k_ref, kh_ref, cw_ref[:, D_GROUP + cs:D_GROUP + cs + 256], first, c * CHUNK, CHUNK, cs, 256)

    causal = _tri(CHUNK)
    gb = gb_ref[...]
    scale = HEAD_DIM ** -0.5

    def chunk_body(c, carry):
        r0 = pl.multiple_of(c * CHUNK, CHUNK)
        rows = pl.ds(r0, CHUNK)
        z = g_ref[0, rows, :] + gb
        bb = _cumsum_rows(_log_sigmoid(z))
        zT = z.T
        bT = bb.T
        heads = range(N_HEADS)
        hs = [slice(h * HEAD_DIM, (h + 1) * HEAD_DIM) for h in heads]
        ig_col = [z[:, h:h + 1] for h in heads]
        b_col = [bb[:, 8 + h:9 + h] for h in heads]
        m = [m_ref[h:h + 1, 0:1] for h in heads]
        q = [qc_ref[rows, hs[h]] for h in heads]
        k = [kc_ref[rows, hs[h]] * scale for h in heads]
        v = [v_ref[0, rows, hs[h]] for h in heads]
        Cst = [C_ref[h] for h in heads]
        nst = [n_ref[h:h + 1, :] for h in heads]
        qk = [_dot_nt(q[h], k[h]) for h in heads]
        qC = [_dot(q[h], Cst[h]) for h in heads]
        dlog = [jnp.where(causal, b_col[h] - bT[8 + h:9 + h, :] + zT[h:h + 1, :], -1e30) for h in heads]
        inter = [b_col[h] + m[h] for h in heads]
        mt = [jnp.maximum(jnp.max(dlog[h], axis=-1, keepdims=True), inter[h]) for h in heads]
        s = [qk[h] * jnp.where(causal, jnp.exp(dlog[h] - mt[h]), 0.0) for h in heads]
        sc = [jnp.exp(inter[h] - mt[h]) for h in heads]
        sv = [_dot(s[h], v[h]) for h in heads]
        bl = [bb[CHUNK - 1:CHUNK, 8 + h:9 + h] for h in heads]
        lw = [bl[h] - b_col[h] + ig_col[h] for h in heads]
        m_new = [jnp.maximum(bl[h] + m[h], jnp.max(lw[h], axis=0, keepdims=True)) for h in heads]
        kw = [k[h] * jnp.exp(lw[h] - m_new[h]) for h in heads]
        dec = [jnp.exp(bl[h] + m[h] - m_new[h]) for h in heads]
        kv = [_dot(kw[h].T, v[h]) for h in heads]
        for h in heads:
            C_ref[h] = dec[h] * Cst[h] + kv[h]
            n_ref[h:h + 1, :] = dec[h] * nst[h] + jnp.sum(kw[h], axis=0, keepdims=True)
            m_ref[h:h + 1, :] = jnp.broadcast_to(m_new[h], (1, LANES))
        for h in heads:
            num = sv[h] + sc[h] * qC[h]
            den = jnp.sum(s[h], axis=-1, keepdims=True) + sc[h] * jnp.sum(q[h] * nst[h], axis=-1, keepdims=True)
            hh = num / jnp.maximum(jnp.abs(den), jnp.exp(-mt[h]))
            hn = _head_norm_rows(hh, ng_ref[:, hs[h]], LN_EPS)
            out_ref[0, rows, hs[h]] = (hn * _sigmoid(o_ref[0, rows, hs[h]])).astype(out_ref.dtype)
        return carry

    lax.fori_loop(0, tb // CHUNK, chunk_body, 0)


def _mlstm_call(proj, conv_w, gate_b, norm_g, *, gate_col, tb=256):
    B, T, _ = proj.shape
    nt = T // tb
    gb = jnp.zeros((1, LANES), F32).at[0, :2 * N_HEADS].set(gate_b)
    ng = norm_g.reshape(1, D_GROUP)
    colblk = lambda j: pl.BlockSpec((1, tb, D_GROUP), lambda b, t: (b, t, j))
    halo = lambda j: pl.BlockSpec((1, 8, D_GROUP), lambda b, t: (b, jnp.maximum(t * (tb // 8) - 1, 0), j))
    return pl.pallas_call(
        functools.partial(_mlstm_kernel, tb=tb),
        grid=(B, nt),
        in_specs=[colblk(0), colblk(1), colblk(2), colblk(3),
                  pl.BlockSpec((1, tb, LANES), lambda b, t: (b, t, gate_col // LANES)),
                  halo(0), halo(1),
                  pl.BlockSpec((4, 2 * D_GROUP), lambda b, t: (0, 0)),
                  pl.BlockSpec((1, LANES), lambda b, t: (0, 0)),
                  pl.BlockSpec((1, D_GROUP), lambda b, t: (0, 0))],
        out_specs=pl.BlockSpec((1, tb, D_GROUP), lambda b, t: (b, t, 0)),
        out_shape=jax.ShapeDtypeStruct((B, T, D_GROUP), BF16),
        scratch_shapes=[pltpu.VMEM((tb, D_GROUP), F32), pltpu.VMEM((tb, D_GROUP), F32),
                        pltpu.VMEM((N_HEADS, HEAD_DIM, HEAD_DIM), F32),
                        pltpu.VMEM((N_HEADS, LANES), F32), pltpu.VMEM((N_HEADS, LANES), F32)],
        compiler_params=pltpu.CompilerParams(dimension_semantics=("arbitrary", "arbitrary"),
                                             vmem_limit_bytes=VMEM_LIMIT),
        name="mlstm",
    )(proj, proj, proj, proj, proj, proj, proj, conv_w, gb, ng)


def _rope_kernel(pos_ref, inv_ref, cos_ref, sin_ref):
    ang = pos_ref[0].astype(F32) * inv_ref[...]
    lane = lax.broadcasted_iota(jnp.int32, ang.shape, 1)
    cos_ref[0] = jnp.cos(ang)
    sin_ref[0] = jnp.where(lane < HEAD_DIM // 2, -jnp.sin(ang), jnp.sin(ang))


def _rope_call(positions, *, tb=512):
    B, T = positions.shape
    half = HEAD_DIM // 2
    inv_freq = ROPE_BASE ** (-jnp.arange(half, dtype=F32) / half)
    inv2 = jnp.concatenate([inv_freq, inv_freq]).reshape(1, HEAD_DIM)
    spec = pl.BlockSpec((1, tb, HEAD_DIM), lambda b, t: (b, t, 0))
    return pl.pallas_call(
        _rope_kernel,
        grid=(B, T // tb),
        in_specs=[pl.BlockSpec((1, tb, 1), lambda b, t: (b, t, 0)),
                  pl.BlockSpec((1, HEAD_DIM), lambda b, t: (0, 0))],
        out_specs=[spec, spec],
        out_shape=[jax.ShapeDtypeStruct((B, T, HEAD_DIM), F32)] * 2,
        compiler_params=pltpu.CompilerParams(dimension_semantics=("arbitrary", "arbitrary")),
        name="rope_table",
    )(positions.reshape(B, T, 1), inv2)


def _ret_kernel(q_ref, k_ref, v_ref, g_ref, cos_ref, sin_ref, ng_ref, out_ref, R_ref, *, tb):
    t = pl.program_id(1)

    @pl.when(t == 0)
    def _():
        R_ref[...] = jnp.zeros_like(R_ref)

    causal = _tri(CHUNK)
    ri = lax.broadcasted_iota(jnp.int32, (CHUNK, CHUNK), 0)
    ci = lax.broadcasted_iota(jnp.int32, (CHUNK, CHUNK), 1)
    rel = (ri - ci).astype(F32)
    tcol = lax.broadcasted_iota(jnp.int32, (CHUNK, 1), 0).astype(F32)
    scale = HEAD_DIM ** -0.5

    def chunk_body(c, carry):
        r0 = pl.multiple_of(c * CHUNK, CHUNK)
        rows = pl.ds(r0, CHUNK)
        cos2 = cos_ref[0, rows, :]
        sin2 = sin_ref[0, rows, :]
        for h in range(N_HEADS):
            hs = slice(h * HEAD_DIM, (h + 1) * HEAD_DIM)
            lg = math.log1p(-2.0 ** (-RET_GAMMA_BASE - h))
            q = q_ref[0, rows, hs]
            k = k_ref[0, rows, hs]
            v = v_ref[0, rows, hs]
            qr = q * cos2 + pltpu.roll(q, HEAD_DIM // 2, 1) * sin2
            kr = (k * cos2 + pltpu.roll(k, HEAD_DIM // 2, 1) * sin2) * scale
            dmat = jnp.where(causal, jnp.exp(rel * lg), 0.0)
            xi = jnp.exp((tcol + 1.0) * lg)
            zeta = jnp.exp((CHUNK - 1.0 - tcol) * lg)
            Rst = R_ref[h]
            intra = _dot(_dot_nt(qr, kr) * dmat, v)
            inter = _dot(qr, Rst) * xi
            R_ref[h] = Rst * math.exp(CHUNK * lg) + _dot((kr * zeta).T, v)
            on = _head_norm_rows(intra + inter, ng_ref[:, hs], LN_EPS)
            out_ref[0, rows, hs] = (on * _silu(g_ref[0, rows, hs])).astype(out_ref.dtype)
        return carry

    lax.fori_loop(0, tb // CHUNK, chunk_body, 0)


def _ret_call(proj, cos2, sin2, norm_g, *, tb=256):
    B, T, _ = proj.shape
    colblk = lambda j: pl.BlockSpec((1, tb, D_GROUP), lambda b, t: (b, t, j))
    tab = pl.BlockSpec((1, tb, HEAD_DIM), lambda b, t: (b, t, 0))
    return pl.pallas_call(
        functools.partial(_ret_kernel, tb=tb),
        grid=(B, T // tb),
        in_specs=[colblk(0), colblk(1), colblk(2), colblk(3), tab, tab,
                  pl.BlockSpec((1, D_GROUP), lambda b, t: (0, 0))],
        out_specs=pl.BlockSpec((1, tb, D_GROUP), lambda b, t: (b, t, 0)),
        out_shape=jax.ShapeDtypeStruct((B, T, D_GROUP), BF16),
        scratch_shapes=[pltpu.VMEM((N_HEADS, HEAD_DIM, HEAD_DIM), F32)],
        compiler_params=pltpu.CompilerParams(dimension_semantics=("arbitrary", "arbitrary"),
                                             vmem_limit_bytes=VMEM_LIMIT),
        name="retention",
    )(proj, proj, proj, proj, cos2, sin2, norm_g.reshape(1, D_GROUP))


def _solve_unit_lower(nms, rhss):
    n = nms[0].shape[0]
    eye = (lax.broadcasted_iota(jnp.int32, (n, n), 0) == lax.broadcasted_iota(jnp.int32, (n, n), 1)).astype(F32)
    ps = [eye + nm for nm in nms]
    xs = list(nms)
    for _ in range(int(math.log2(n)) - 1):
        xs = [_dot(x, x) for x in xs]
        ps = [p + _dot(p, x) for p, x in zip(ps, xs)]
    x0 = [_dot(p, r) for p, r in zip(ps, rhss)]
    resid = [r - a + _dot(nm, a) for r, a, nm in zip(rhss, x0, nms)]
    return [a + _dot(p, r) for a, p, r in zip(x0, ps, resid)]


def _l2norm_rows(z):
    return z * lax.rsqrt(jnp.sum(z * z, axis=-1, keepdims=True) + 1e-6)


def _gdn_kernel(q_ref, k_ref, v_ref, z_ref, g_ref, qh_ref, kh_ref, vh_ref, cw_ref, an_ref, dt_ref, ng_ref,
                out_ref, qc_ref, kc_ref, vc_ref, S_ref, *, tb):
    t = pl.program_id(1)
    first = t == 0

    @pl.when(first)
    def _():
        S_ref[...] = jnp.zeros_like(S_ref)

    srcs = ((q_ref, qh_ref, qc_ref), (k_ref, kh_ref, kc_ref), (v_ref, vh_ref, vc_ref))
    for c in range(tb // CHUNK):
        for cb in range(D_GROUP // 256):
            cs = cb * 256
            for i, (src, halo, dst) in enumerate(srcs):
                w = cw_ref[:, i * D_GROUP + cs:i * D_GROUP + cs + 256]
                dst[c * CHUNK:(c + 1) * CHUNK, cs:cs + 256] = _conv_silu_rows(
                    src, halo, w, first, c * CHUNK, CHUNK, cs, 256)

    causal = _tri(CHUNK)
    strict = _tri(CHUNK, strict=True)
    a_neg = an_ref[...]
    dtb = dt_ref[...]
    scale = HEAD_DIM ** -0.5

    def chunk_body(c, carry):
        r0 = pl.multiple_of(c * CHUNK, CHUNK)
        rows = pl.ds(r0, CHUNK)
        gz = g_ref[0, rows, :]
        beta = _sigmoid(gz)
        gc = _cumsum_rows(a_neg * _softplus(gz + dtb))
        gcT = gc.T
        heads = range(N_HEADS)
        hs = [slice(h * HEAD_DIM, (h + 1) * HEAD_DIM) for h in heads]
        gc_col = [gc[:, h:h + 1] for h in heads]
        b_col = [beta[:, 8 + h:9 + h] for h in heads]
        gamma = [jnp.where(causal, jnp.exp(gc_col[h] - gcT[h:h + 1, :]), 0.0) for h in heads]
        q = [_l2norm_rows(qc_ref[rows, hs[h]]) * scale for h in heads]
        k = [_l2norm_rows(kc_ref[rows, hs[h]]) for h in heads]
        kb = [k[h] * b_col[h] for h in heads]
        eg = [jnp.exp(gc_col[h]) for h in heads]
        kq = [_dot_nt(jnp.concatenate([kb[h], q[h]], axis=0), k[h]) for h in heads]
        a_mat = [jnp.where(strict, kq[h][:CHUNK] * gamma[h], 0.0) for h in heads]
        att = [kq[h][CHUNK:] * gamma[h] for h in heads]
        rhs = [jnp.concatenate([vc_ref[rows, hs[h]] * b_col[h], kb[h] * eg[h]], axis=1) for h in heads]
        uw = _solve_unit_lower([-a for a in a_mat], rhs)
        S = [S_ref[h] for h in heads]
        ws = [_dot(jnp.concatenate([uw[h][:, HEAD_DIM:], q[h] * eg[h]], axis=0), S[h]) for h in heads]
        v_new = [uw[h][:, :HEAD_DIM] - ws[h][:CHUNK] for h in heads]
        o = [ws[h][CHUNK:] + _dot(att[h], v_new[h]) for h in heads]
        g_last = [gc[CHUNK - 1:CHUNK, h:h + 1] for h in heads]
        kd = [(k[h] * jnp.exp(g_last[h] - gc_col[h])).T for h in heads]
        for h in heads:
            S_ref[h] = S[h] * jnp.exp(g_last[h]) + _dot(kd[h], v_new[h])
        for h in heads:
            on = _head_norm_rows(o[h], ng_ref[:, hs[h]], 1e-6, center=False)
            out_ref[0, rows, hs[h]] = (on * _silu(z_ref[0, rows, hs[h]])).astype(out_ref.dtype)
        return carry

    lax.fori_loop(0, tb // CHUNK, chunk_body, 0)


def _gdn_call(proj, conv_w, a_log, dt_bias, norm_g, *, col0, tb=256):
    B, T, _ = proj.shape
    j0 = col0 // D_GROUP
    an = jnp.zeros((1, LANES), F32).at[0, :N_HEADS].set(-jnp.exp(a_log.astype(F32)))
    dtb = jnp.zeros((1, LANES), F32).at[0, :N_HEADS].set(dt_bias)
    colblk = lambda j: pl.BlockSpec((1, tb, D_GROUP), lambda b, t: (b, t, j0 + j))
    halo = lambda j: pl.BlockSpec((1, 8, D_GROUP), lambda b, t: (b, jnp.maximum(t * (tb // 8) - 1, 0), j0 + j))
    return pl.pallas_call(
        functools.partial(_gdn_kernel, tb=tb),
        grid=(B, T // tb),
        in_specs=[colblk(0), colblk(1), colblk(2), colblk(3),
                  pl.BlockSpec((1, tb, LANES), lambda b, t: (b, t, (col0 + 4 * D_GROUP) // LANES)),
                  halo(0), halo(1), halo(2),
                  pl.BlockSpec((4, 3 * D_GROUP), lambda b, t: (0, 0)),
                  pl.BlockSpec((1, LANES), lambda b, t: (0, 0)),
                  pl.BlockSpec((1, LANES), lambda b, t: (0, 0)),
                  pl.BlockSpec((1, D_GROUP), lambda b, t: (0, 0))],
        out_specs=pl.BlockSpec((1, tb, D_GROUP), lambda b, t: (b, t, 0)),
        out_shape=jax.ShapeDtypeStruct((B, T, D_GROUP), BF16),
        scratch_shapes=[pltpu.VMEM((tb, D_GROUP), F32), pltpu.VMEM((tb, D_GROUP), F32),
                        pltpu.VMEM((tb, D_GROUP), F32),
                        pltpu.VMEM((N_HEADS, HEAD_DIM, HEAD_DIM), F32)],
        compiler_params=pltpu.CompilerParams(dimension_semantics=("arbitrary", "arbitrary"),
                                             vmem_limit_bytes=VMEM_LIMIT),
        name="gdn",
    )(proj, proj, proj, proj, proj, proj, proj, proj, conv_w, an, dtb, norm_g.reshape(1, D_GROUP))


N_PAIRS = N_RWKV // 2
RWKV_LOW = 384


def _shift1_rows(src_ref, halo_ref, first, r0, nrows, c0, ncols):
    if r0 == 0:
        hl = jnp.where(first, 0.0, halo_ref[0, 7:8, c0:c0 + ncols])
        return jnp.concatenate([hl, src_ref[0, 0:nrows - 1, c0:c0 + ncols]], axis=0)
    return src_ref[0, r0 - 1:r0 - 1 + nrows, c0:c0 + ncols]


def _rwkv_kernel(r_ref, k_ref, v_ref, l0_ref, l1_ref, l2_ref,
                 rh_ref, kh_ref, vh_ref, l0h_ref, l1h_ref, l2h_ref,
                 mu_ref, mul_ref, w0_ref, w2_ref, a0_ref, a2_ref, g2_ref, kk_ref, ka_ref, rk_ref,
                 lng_ref, lnb_ref, out_ref,
                 rs_ref, ks_ref, vs_ref, lw_ref, an_ref, bn_ref, gs_ref, bo_ref, y_ref, S_ref, *, tb):
    t = pl.program_id(1)
    first = t == 0

    @pl.when(first)
    def _():
        S_ref[...] = jnp.zeros_like(S_ref)

    ri = lax.broadcasted_iota(jnp.int32, (LANES, LANES), 0)
    ci = lax.broadcasted_iota(jnp.int32, (LANES, LANES), 1)
    same_head = ((ri // RWKV_HEAD) == (ci // RWKV_HEAD)).astype(F32)

    def lerp(src, halo, mu, r0, c0, ncols):
        cur = src[0, r0:r0 + CHUNK, c0:c0 + ncols]
        return cur + (_shift1_rows(src, halo, first, r0, CHUNK, c0, ncols) - cur) * mu

    for c in range(tb // CHUNK):
        r0 = c * CHUNK
        wl = lerp(l0_ref, l0h_ref, mul_ref[:, 0:LANES], r0, 0, LANES)
        g1 = lerp(l1_ref, l1h_ref, mul_ref[:, LANES:2 * LANES], r0, 0, LANES)
        g2 = lerp(l2_ref, l2h_ref, mul_ref[:, 2 * LANES:3 * LANES], r0, 0, LANES)
        lane = lax.broadcasted_iota(jnp.int32, wl.shape, 1)
        wl_t = jnp.where(lane < 64, jnp.tanh(wl), 0.0)
        al = jnp.where(lane >= 64, wl, 0.0)
        sg1 = _sigmoid(g1)
        sg2 = jnp.where(lane < 32, _sigmoid(g2), 0.0)
        for p in range(N_PAIRS):
            ps = slice(p * LANES, (p + 1) * LANES)
            wpre = w0_ref[:, ps] + _dot(wl_t, w2_ref[:, ps])
            w = -_softplus(-wpre) - 0.5
            a = _sigmoid(a0_ref[:, ps] + _dot(al, a2_ref[:, ps]))
            g = _dot(sg1, g2_ref[0:LANES, ps]) + _dot(sg2, g2_ref[LANES:2 * LANES, ps])
            r = lerp(r_ref, rh_ref, mu_ref[:, ps], r0, p * LANES, LANES)
            k = lerp(k_ref, kh_ref, mu_ref[:, D_GROUP + p * LANES:D_GROUP + (p + 1) * LANES], r0, p * LANES, LANES)
            v = lerp(v_ref, vh_ref, mu_ref[:, 2 * D_GROUP + p * LANES:2 * D_GROUP + (p + 1) * LANES], r0, p * LANES, LANES)
            kk = k * kk_ref[:, ps]
            nrm = jnp.sqrt(_group_sum(kk * kk, same_head))
            kk = kk / jnp.maximum(nrm, 1e-12)
            k2 = k * (1.0 + (a - 1.0) * ka_ref[:, ps])
            rows = slice(r0, r0 + CHUNK)
            rs_ref[p, rows, :] = r
            ks_ref[p, rows, :] = k2
            vs_ref[p, rows, :] = v
            lw_ref[p, rows, :] = -jnp.exp(w)
            an_ref[p, rows, :] = -kk
            bn_ref[p, rows, :] = kk * a
            gs_ref[p, rows, :] = g
            bo_ref[p, rows, :] = _group_sum(r * k2 * rk_ref[:, ps], same_head) * v

    strict = _tri(CHUNK, strict=True)
    lane1 = lax.broadcasted_iota(jnp.int32, (1, LANES), 1)
    m0 = (lane1 < RWKV_HEAD).astype(F32)
    m1 = 1.0 - m0
    t2 = lax.broadcasted_iota(jnp.int32, (CHUNK, LANES), 0)
    l2 = lax.broadcasted_iota(jnp.int32, (CHUNK, LANES), 1)
    lo = l2 < RWKV_HEAD
    s2 = l2 & (RWKV_HEAD - 1)
    causal2 = t2 >= s2
    strict_hi = (t2 > s2) & (l2 >= RWKV_HEAD)

    def chunk_body(c, carry):
        rows = pl.ds(pl.multiple_of(c * CHUNK, CHUNK), CHUNK)
        pairs = range(N_PAIRS)
        halves = [(p, hh) for p in pairs for hh in range(2)]
        lw = [lw_ref[p, rows, :] for p in pairs]
        cs = [_cumsum_rows(lw[p]) for p in pairs]
        w_inv = [jnp.exp(-cs[p]) for p in pairs]
        rt = [rs_ref[p, rows, :] * jnp.exp(cs[p]) for p in pairs]
        at = [an_ref[p, rows, :] * jnp.exp(cs[p] - lw[p]) for p in pairs]
        v = [vs_ref[p, rows, :] for p in pairs]
        bk = [jnp.concatenate([bn_ref[p, rows, :] * w_inv[p], ks_ref[p, rows, :] * w_inv[p]], axis=0).astype(BF16)
              for p in pairs]
        S = [S_ref[p] for p in pairs]
        pm = [_dot_nt(jnp.concatenate([at[p] * m0, at[p] * m1, rt[p] * m0, rt[p] * m1], axis=0), bk[p]) for p in pairs]
        xy0 = [_dot_nt(jnp.concatenate([at[p], rt[p]], axis=0), S[p]) for p in pairs]
        vv = [jnp.concatenate([v[p], v[p]], axis=0).astype(BF16) for p in pairs]
        pa = [pm[p][hh * CHUNK:(hh + 1) * CHUNK] for p, hh in halves]
        a_ab = [jnp.where(strict, a[:, :CHUNK], 0.0) for a in pa]
        x = [xy0[p][:CHUNK] + _dot(jnp.where(strict_hi, pa[2 * p + hh], 0.0), vv[p]) for p, hh in halves]
        us = _solve_unit_lower(a_ab, x)
        uv_f = [jnp.concatenate([jnp.where(lo, us[2 * p], us[2 * p + 1]), v[p]], axis=0) for p in pairs]
        uv = [a.astype(BF16) for a in uv_f]
        ys = [_dot(jnp.where(causal2, pm[p][(2 + hh) * CHUNK:(3 + hh) * CHUNK], 0.0), uv[p]) for p, hh in halves]
        for p in pairs:
            y_ref[p, rows, :] = xy0[p][CHUNK:] + jnp.where(lo, ys[2 * p], ys[2 * p + 1])
        upd = [_dot(uv_f[p].T, bk[p]) * same_head for p in pairs]
        for p in pairs:
            S_ref[p] = (S[p] + upd[p]) * jnp.exp(cs[p][CHUNK - 1:CHUNK, :])
        return carry

    lax.fori_loop(0, tb // CHUNK, chunk_body, 0)

    inv_n = 1.0 / RWKV_HEAD
    for c in range(tb // CHUNK):
        rows = slice(c * CHUNK, (c + 1) * CHUNK)
        for p in range(N_PAIRS):
            ps = slice(p * LANES, (p + 1) * LANES)
            y = y_ref[p, rows, :]
            yc = y - _group_sum(y, same_head) * inv_n
            var = _group_sum(yc * yc, same_head) * inv_n
            yn = yc * lax.rsqrt(var + RWKV_LN_EPS) * lng_ref[:, ps] + lnb_ref[:, ps]
            out_ref[0, rows, ps] = ((yn + bo_ref[p, rows, :]) * gs_ref[p, rows, :]).astype(out_ref.dtype)


def _rwkv_call(proj, mu, w0, w2, a0, a2, g2, k_k, k_a, r_k, ln_g, ln_b, *, col0, tb=256):
    B, T, _ = proj.shape
    j0 = col0 // D_GROUP
    l0 = (col0 + 3 * D_GROUP) // LANES
    row = lambda a: a.reshape(1, -1).astype(F32)
    mul = jnp.zeros((1, RWKV_LOW), F32).at[0, :288].set(mu[3 * D_GROUP:])
    w2p = jnp.zeros((LANES, D_GROUP), F32).at[:64].set(w2)
    a2p = jnp.zeros((LANES, D_GROUP), F32).at[64:].set(a2)
    g2p = jnp.zeros((2 * LANES, D_GROUP), F32).at[:160].set(g2)
    colblk = lambda j: pl.BlockSpec((1, tb, D_GROUP), lambda b, t: (b, t, j0 + j))
    lowblk = lambda j: pl.BlockSpec((1, tb, LANES), lambda b, t: (b, t, l0 + j))
    hrow = lambda t: jnp.maximum(t * (tb // 8) - 1, 0)
    halo = lambda j: pl.BlockSpec((1, 8, D_GROUP), lambda b, t: (b, hrow(t), j0 + j))
    lowhalo = lambda j: pl.BlockSpec((1, 8, LANES), lambda b, t: (b, hrow(t), l0 + j))
    full = lambda a: pl.BlockSpec(a.shape, lambda b, t: (0,) * a.ndim)
    params = [row(mu[:3 * D_GROUP]), mul, row(w0), w2p, row(a0), a2p, g2p, row(k_k), row(k_a), row(r_k),
              row(ln_g), row(ln_b)]
    big = pltpu.VMEM((N_PAIRS, tb, LANES), F32)
    return pl.pallas_call(
        functools.partial(_rwkv_kernel, tb=tb),
        grid=(B, T // tb),
        in_specs=[colblk(0), colblk(1), colblk(2), lowblk(0), lowblk(1), lowblk(2),
                  halo(0), halo(1), halo(2), lowhalo(0), lowhalo(1), lowhalo(2)] + [full(a) for a in params],
        out_specs=pl.BlockSpec((1, tb, D_GROUP), lambda b, t: (b, t, 0)),
        out_shape=jax.ShapeDtypeStruct((B, T, D_GROUP), BF16),
        scratch_shapes=[big] * 9 + [pltpu.VMEM((N_PAIRS, LANES, LANES), F32)],
        compiler_params=pltpu.CompilerParams(dimension_semantics=("arbitrary", "arbitrary"),
                                             vmem_limit_bytes=VMEM_LIMIT),
        name="rwkv7",
    )(*([proj] * 12), *params)


def _ada_kernel(c_ref, w_ref, b_ref, out_ref):
    sc = _silu(c_ref[...]).astype(BF16)
    out_ref[0] = jnp.dot(sc, w_ref[0].astype(BF16), preferred_element_type=F32) + b_ref[0]


def _ada_call(c, ada_w, ada_b, *, tn=1536):
    B = c.shape[0]
    n_mod = ada_w.shape[0] * ada_w.shape[1]
    w = ada_w.reshape(n_mod, D_MODEL, 3 * D_MODEL)
    b = ada_b.reshape(n_mod, 1, 3 * D_MODEL)
    return pl.pallas_call(
        _ada_kernel,
        grid=(n_mod, 3 * D_MODEL // tn),
        in_specs=[pl.BlockSpec((B, D_MODEL), lambda i, j: (0, 0)),
                  pl.BlockSpec((1, D_MODEL, tn), lambda i, j: (i, 0, j)),
                  pl.BlockSpec((1, 1, tn), lambda i, j: (i, 0, j))],
        out_specs=pl.BlockSpec((1, B, tn), lambda i, j: (i, 0, j)),
        out_shape=jax.ShapeDtypeStruct((n_mod, B, 3 * D_MODEL), F32),
        compiler_params=pltpu.CompilerParams(dimension_semantics=("arbitrary", "arbitrary"),
                                             vmem_limit_bytes=VMEM_LIMIT),
        name="adaln",
    )(c, w, b)


def _mod_spec(i, part, nb):
    return pl.BlockSpec((1, 1, D_MODEL), lambda b, t: (i * nb + b, 0, part))


def _modulate_kernel(x_ref, shift_ref, scale_ref, h_ref):
    h_ref[0] = (x_ref[0] * (1.0 + scale_ref[0]) + shift_ref[0]).astype(h_ref.dtype)


def _modulate_call(x, mods3, i, *, tb=512):
    B, T, _ = x.shape
    blk = pl.BlockSpec((1, tb, D_MODEL), lambda b, t: (b, t, 0))
    return pl.pallas_call(
        _modulate_kernel,
        grid=(B, T // tb),
        in_specs=[blk, _mod_spec(i, 0, B), _mod_spec(i, 1, B)],
        out_specs=blk,
        out_shape=jax.ShapeDtypeStruct(x.shape, BF16),
        compiler_params=pltpu.CompilerParams(dimension_semantics=("arbitrary", "arbitrary"),
                                             vmem_limit_bytes=VMEM_LIMIT),
        name="modulate",
    )(x, mods3, mods3)


LN_ROWS = 16


def _proj_ln_kernel(*refs, n_lhs, nk, tm, with_next):
    lhs = refs[:n_lhs]
    ws = refs[n_lhs:2 * n_lhs]
    x_ref, gate_ref, g_ref, b_ref = refs[2 * n_lhs:2 * n_lhs + 4]
    if with_next:
        shift_ref, scale_ref, xo_ref, h_ref, acc_ref = refs[2 * n_lhs + 4:]
    else:
        xo_ref, acc_ref = refs[2 * n_lhs + 4:]
    k = pl.program_id(2)
    def partial_product():
        part = jnp.dot(lhs[0][0], ws[0][...], preferred_element_type=F32)
        for j in range(1, n_lhs):
            part = part + jnp.dot(lhs[j][0], ws[j][...], preferred_element_type=F32)
        return part

    @pl.when(k == 0)
    def _():
        acc_ref[...] = partial_product()

    @pl.when(k > 0)
    def _():
        acc_ref[...] += partial_product()

    @pl.when(k == nk - 1)
    def _():
        gate1 = 1.0 + gate_ref[0]
        g, b = g_ref[...], b_ref[...]
        if with_next:
            scale1, shift = 1.0 + scale_ref[0], shift_ref[0]

        def rows_body(r, carry):
            rows = pl.ds(pl.multiple_of(r * LN_ROWS, LN_ROWS), LN_ROWS)
            z = ALPHA * x_ref[0, rows, :] + gate1 * acc_ref[rows, :]
            zc = z - jnp.mean(z, axis=-1, keepdims=True)
            var = jnp.mean(zc * zc, axis=-1, keepdims=True)
            xn = zc * lax.rsqrt(var + LN_EPS) * g + b
            xo_ref[0, rows, :] = xn
            if with_next:
                h_ref[0, rows, :] = (xn * scale1 + shift).astype(h_ref.dtype)
            return carry

        lax.fori_loop(0, tm // LN_ROWS, rows_body, 0, unroll=8)


def _proj_ln_call(lhs, ws, x, mods3, i, g, b, *, with_next, tm, tk):
    B, T, D = x.shape
    n_lhs = len(lhs)
    nk = lhs[0].shape[2] // tk
    blk = pl.BlockSpec((1, tm, D), lambda b, t, k: (b, t, 0))
    row = pl.BlockSpec((1, D), lambda b, t, k: (0, 0))
    mod = lambda ii, part: pl.BlockSpec((1, 1, D), lambda b, t, k: (ii * B + b, 0, part))
    in_specs = ([pl.BlockSpec((1, tm, tk), lambda b, t, k: (b, t, k))] * n_lhs
                + [pl.BlockSpec((tk, D), lambda b, t, k: (k, 0))] * n_lhs
                + [blk, mod(i, 2), row, row])
    args = list(lhs) + list(ws) + [x, mods3, g.reshape(1, D), b.reshape(1, D)]
    out_specs = [blk]
    out_shape = [jax.ShapeDtypeStruct(x.shape, F32)]
    if with_next:
        in_specs += [mod(i + 1, 0), mod(i + 1, 1)]
        args += [mods3, mods3]
        out_specs.append(blk)
        out_shape.append(jax.ShapeDtypeStruct(x.shape, BF16))
    return pl.pallas_call(
        functools.partial(_proj_ln_kernel, n_lhs=n_lhs, nk=nk, tm=tm, with_next=with_next),
        grid=(B, T // tm, nk),
        in_specs=in_specs, out_specs=out_specs, out_shape=out_shape,
        scratch_shapes=[pltpu.VMEM((tm, D), F32)],
        compiler_params=pltpu.CompilerParams(dimension_semantics=("arbitrary", "arbitrary", "arbitrary"),
                                             vmem_limit_bytes=VMEM_LIMIT),
        name="proj_residual_ln",
    )(*args)


def _matmul_kernel(a_ref, b_ref, o_ref, *scratch, nk):
    if nk == 1:
        o_ref[...] = jnp.dot(a_ref[...], b_ref[...], preferred_element_type=F32).astype(o_ref.dtype)
        return
    (acc_ref,) = scratch
    k = pl.program_id(2)

    @pl.when(k == 0)
    def _():
        acc_ref[...] = jnp.zeros_like(acc_ref)

    acc_ref[...] += jnp.dot(a_ref[...], b_ref[...], preferred_element_type=F32)

    @pl.when(k == nk - 1)
    def _():
        o_ref[...] = acc_ref[...].astype(o_ref.dtype)


def _matmul_call(a, b, *, tm, tn, tk, out_dtype=F32):
    M, K = a.shape
    _, N = b.shape
    nk = K // tk
    return pl.pallas_call(
        functools.partial(_matmul_kernel, nk=nk),
        grid=(N // tn, M // tm, nk),
        in_specs=[pl.BlockSpec((tm, tk), lambda j, i, k: (i, k)),
                  pl.BlockSpec((tk, tn), lambda j, i, k: (k, j))],
        out_specs=pl.BlockSpec((tm, tn), lambda j, i, k: (i, j)),
        out_shape=jax.ShapeDtypeStruct((M, N), out_dtype),
        scratch_shapes=[] if nk == 1 else [pltpu.VMEM((tm, tn), F32)],
        compiler_params=pltpu.CompilerParams(dimension_semantics=("arbitrary", "arbitrary", "arbitrary"),
                                             vmem_limit_bytes=VMEM_LIMIT),
        name="matmul",
    )(a, b)


def _swiglu_up_kernel(h_ref, wg_ref, wu_ref, o_ref, wgb_ref, wub_ref):
    @pl.when(pl.program_id(1) == 0)
    def _():
        wgb_ref[...] = wg_ref[...].astype(BF16)
        wub_ref[...] = wu_ref[...].astype(BF16)

    h = h_ref[...]
    g = jnp.dot(h, wgb_ref[...], preferred_element_type=F32)
    u = jnp.dot(h, wub_ref[...], preferred_element_type=F32)
    o_ref[...] = (_silu(g) * u).astype(o_ref.dtype)


def _swiglu_up_call(h, wg, wu, layer, *, tm=512, tn=512):
    M, K = h.shape
    _, _, N = wg.shape
    wspec = pl.BlockSpec((None, K, tn), lambda j, i: (layer, 0, j))
    return pl.pallas_call(
        _swiglu_up_kernel,
        grid=(N // tn, M // tm),
        in_specs=[pl.BlockSpec((tm, K), lambda j, i: (i, 0)), wspec, wspec],
        out_specs=pl.BlockSpec((tm, tn), lambda j, i: (i, j)),
        out_shape=jax.ShapeDtypeStruct((M, N), BF16),
        scratch_shapes=[pltpu.VMEM((K, tn), BF16), pltpu.VMEM((K, tn), BF16)],
        compiler_params=pltpu.CompilerParams(dimension_semantics=("arbitrary", "arbitrary"),
                                             vmem_limit_bytes=VMEM_LIMIT),
        name="swiglu_up",
    )(h, wg, wu)


AB_COLS = 7680
CD_COLS = 8320
RWKV_COL0 = 4 * D_GROUP
MLSTM_GATE_COL = 7 * D_GROUP + RWKV_LOW
GDN_COL0 = 4 * D_GROUP


def _pad_cols(w, n):
    return jnp.pad(w, ((0, 0), (0, n - w.shape[1])))


def _ab_weight(w_in):
    a_main, a_gates = w_in[:, :4 * D_GROUP], w_in[:, 4 * D_GROUP:4 * D_GROUP + 2 * N_HEADS]
    b0 = 4 * D_GROUP + 2 * N_HEADS
    b_main, b_low = w_in[:, b0:b0 + 3 * D_GROUP], w_in[:, b0 + 3 * D_GROUP:]
    return jnp.concatenate([a_main, b_main, _pad_cols(b_low, RWKV_LOW), _pad_cols(a_gates, LANES)],
                           axis=1).astype(BF16)


def _cd_weight(w_in):
    return _pad_cols(w_in, CD_COLS).astype(BF16)


def kernel(x, c, positions, ada_w, ada_b, ln_g, ln_b, ab_w_in, ab_w_out, mlstm_conv_w, mlstm_gate_b, mlstm_norm_g, rwkv_mu, rwkv_w0, rwkv_w2, rwkv_a0, rwkv_a2, rwkv_g2, rwkv_k_k, rwkv_k_a, rwkv_r_k, rwkv_ln_g, rwkv_ln_b, cd_w_in, cd_w_out, ret_norm_g, gdn_conv_w, gdn_a_log, gdn_dt_bias, gdn_norm_g, ffn_w_gate, ffn_w_up, ffn_w_down):
    B, T, D = x.shape
    M = B * T
    depth = ada_w.shape[0]
    mods = _ada_call(c, ada_w, ada_b)
    mods3 = mods.reshape(2 * depth * B, 1, 3 * D)
    cos2, sin2 = _rope_call(positions)
    h = _modulate_call(x, mods3, 0)
    for layer in range(depth):
        j = layer // 2
        i_mix, i_ffn = 2 * layer, 2 * layer + 1
        if layer % 2 == 0:
            proj = _matmul_call(h.reshape(M, D), _ab_weight(ab_w_in[j]), tm=512, tn=1280, tk=D)
            proj = proj.reshape(B, T, AB_COLS)
            ya = _mlstm_call(proj, mlstm_conv_w[j], mlstm_gate_b[j], mlstm_norm_g[j], gate_col=MLSTM_GATE_COL)
            yb = _rwkv_call(proj, rwkv_mu[j], rwkv_w0[j], rwkv_w2[j], rwkv_a0[j], rwkv_a2[j], rwkv_g2[j],
                            rwkv_k_k[j], rwkv_k_a[j], rwkv_r_k[j], rwkv_ln_g[j], rwkv_ln_b[j], col0=RWKV_COL0)
            w_out = ab_w_out[j]
        else:
            proj = _matmul_call(h.reshape(M, D), _cd_weight(cd_w_in[j]), tm=512, tn=1664, tk=D)
            proj = proj.reshape(B, T, CD_COLS)
            ya = _ret_call(proj, cos2, sin2, ret_norm_g[j])
            yb = _gdn_call(proj, gdn_conv_w[j], gdn_a_log[j], gdn_dt_bias[j], gdn_norm_g[j], col0=GDN_COL0)
            w_out = cd_w_out[j]
        w_out = w_out.astype(BF16)
        x, h = _proj_ln_call([ya, yb], [w_out[:D_GROUP], w_out[D_GROUP:]], x, mods3, i_mix,
                             ln_g[layer, 0], ln_b[layer, 0], with_next=True, tm=256, tk=D_GROUP)
        act = _swiglu_up_call(h.reshape(M, D), ffn_w_gate, ffn_w_up, layer)
        last = layer == depth - 1
        res = _proj_ln_call([act.reshape(B, T, D_FF)], [ffn_w_down[layer].astype(BF16)], x, mods3, i_ffn,
                            ln_g[layer, 1], ln_b[layer, 1], with_next=not last, tm=512, tk=1408)
        if last:
            (x,) = res
        else:
            x, h = res
    return x
```

```python
import functools
import math

import numpy as np
import jax
import jax.numpy as jnp
from jax import lax
from jax.experimental import pallas as pl
from jax.experimental.pallas import tpu as pltpu

F32 = jnp.float32
BF16 = jnp.bfloat16

D_MODEL = 2048
D_GROUP = 1024
HEAD_DIM = 128
N_HEADS = 8
RWKV_HEAD = 64
N_RWKV = 16
CHUNK = 64
D_FF = 5632
DEPTH = 2
ALPHA = (2 * DEPTH) ** 0.25
LN_EPS = 1e-5
RWKV_LN_EPS = 64e-5
ROPE_BASE = 10000.0
RET_GAMMA_BASE = 5.0
LANES = 128
VMEM_LIMIT = 48 * 1024 * 1024


def _dot(a, b):
    return jnp.dot(a.astype(BF16), b.astype(BF16), preferred_element_type=F32)


def _dot_nt(a, b):
    return lax.dot_general(a.astype(BF16), b.astype(BF16), (((1,), (1,)), ((), ())),
                           preferred_element_type=F32)


def _split3(x):
    hi = x.astype(BF16)
    r1 = x - hi.astype(F32)
    mid = r1.astype(BF16)
    lo = (r1 - mid.astype(F32)).astype(BF16)
    return hi, mid, lo


def _cumsum_rows(x):
    n = x.shape[1]
    out = jnp.dot(_tri(CHUNK).astype(BF16), jnp.concatenate(_split3(x), axis=1), preferred_element_type=F32)
    return out[:, :n] + out[:, n:2 * n] + out[:, 2 * n:]


def _group_sum(x, ones01):
    m = x.shape[0]
    hi = x.astype(BF16)
    lo = (x - hi.astype(F32)).astype(BF16)
    out = jnp.dot(jnp.concatenate([hi, lo], axis=0), ones01.astype(BF16), preferred_element_type=F32)
    return out[:m] + out[m:]


def _sigmoid(x):
    return 1.0 / (1.0 + jnp.exp(-x))


def _silu(x):
    return x * _sigmoid(x)


def _log_sigmoid(x):
    return jnp.minimum(x, 0.0) - jnp.log1p(jnp.exp(-jnp.abs(x)))


def _softplus(x):
    return jnp.maximum(x, 0.0) + jnp.log1p(jnp.exp(-jnp.abs(x)))


def _tri(n, strict=False):
    r = lax.broadcasted_iota(jnp.int32, (n, n), 0)
    c = lax.broadcasted_iota(jnp.int32, (n, n), 1)
    return (r > c) if strict else (r >= c)


def _conv_silu_rows(src_ref, halo_ref, w, first, r0, nrows, c0, ncols):
    cur = src_ref[0, r0:r0 + nrows, c0:c0 + ncols]
    acc = w[3:4] * cur
    if r0 == 0:
        hl = jnp.where(first, 0.0, halo_ref[0, :, c0:c0 + ncols])
        ext = jnp.concatenate([hl, cur[0:8]], axis=0)
        for j in range(3):
            head = ext[5 + j:13 + j]
            if nrows > 8:
                rest = src_ref[0, 5 + j:nrows - 3 + j, c0:c0 + ncols]
                sh = jnp.concatenate([head, rest], axis=0)
            else:
                sh = head
            acc = acc + w[j:j + 1] * sh
    else:
        for j in range(3):
            acc = acc + w[j:j + 1] * src_ref[0, r0 - 3 + j:r0 - 3 + j + nrows, c0:c0 + ncols]
    return _silu(acc)


def _head_norm_rows(h, g_row, eps, center=True):
    if center:
        h = h - jnp.mean(h, axis=-1, keepdims=True)
    return h * lax.rsqrt(jnp.mean(h * h, axis=-1, keepdims=True) + eps) * g_row


def _mlstm_kernel(q_ref, k_ref, v_ref, o_ref, g_ref, qh_ref, kh_ref, cw_ref, gb_ref, ng_ref,
                  out_ref, qc_ref, kc_ref, C_ref, n_ref, m_ref, *, tb):
    t = pl.program_id(1)
    first = t == 0

    @pl.when(first)
    def _():
        C_ref[...] = jnp.zeros_like(C_ref)
        n_ref[...] = jnp.zeros_like(n_ref)
        m_ref[...] = jnp.zeros_like(m_ref)

    for c in range(tb // CHUNK):
        for cb in range(D_GROUP // 256):
            cs = cb * 256
            qc_ref[c * CHUNK:(c + 1) * CHUNK, cs:cs + 256] = _conv_silu_rows(
                q_ref, qh_ref, cw_ref[:, cs:cs + 256], first, c * CHUNK, CHUNK, cs, 256)
            kc_ref[c * CHUNK:(c + 1) * CHUNK, cs:cs + 256] = _conv_silu_rows(
                k_ref, kh_ref, cw_ref[:, D_GROUP + cs:D_GROUP + cs + 256], first, c * CHUNK, CHUNK, cs, 256)

    causal = _tri(CHUNK)
    gb = gb_ref[...]
    scale = HEAD_DIM ** -0.5

    def chunk_body(c, carry):
        r0 = pl.multiple_of(c * CHUNK, CHUNK)
        rows = pl.ds(r0, CHUNK)
        z = g_ref[0, rows, :] + gb
        bb = _cumsum_rows(_log_sigmoid(z))
        zT = z.T
        bT = bb.T
        heads = range(N_HEADS)
        hs = [slice(h * HEAD_DIM, (h + 1) * HEAD_DIM) for h in heads]
        ig_col = [z[:, h:h + 1] for h in heads]
        b_col = [bb[:, 8 + h:9 + h] for h in heads]
        m = [m_ref[h:h + 1, 0:1] for h in heads]
        q = [qc_ref[rows, hs[h]] for h in heads]
        k = [kc_ref[rows, hs[h]] * scale for h in heads]
        v = [v_ref[0, rows, hs[h]] for h in heads]
        Cst = [C_ref[h] for h in heads]
        nst = [n_ref[h:h + 1, :] for h in heads]
        qk = [_dot_nt(q[h], k[h]) for h in heads]
        qC = [_dot(q[h], Cst[h]) for h in heads]
        dlog = [jnp.where(causal, b_col[h] - bT[8 + h:9 + h, :] + zT[h:h + 1, :], -1e30) for h in heads]
        inter = [b_col[h] + m[h] for h in heads]
        mt = [jnp.maximum(jnp.max(dlog[h], axis=-1, keepdims=True), inter[h]) for h in heads]
        s = [qk[h] * jnp.where(causal, jnp.exp(dlog[h] - mt[h]), 0.0) for h in heads]
        sc = [jnp.exp(inter[h] - mt[h]) for h in heads]
        sv = [_dot(s[h], v[h]) for h in heads]
        bl = [bb[CHUNK - 1:CHUNK, 8 + h:9 + h] for h in heads]
        lw = [bl[h] - b_col[h] + ig_col[h] for h in heads]
        m_new = [jnp.maximum(bl[h] + m[h], jnp.max(lw[h], axis=0, keepdims=True)) for h in heads]
        kw = [k[h] * jnp.exp(lw[h] - m_new[h]) for h in heads]
        dec = [jnp.exp(bl[h] + m[h] - m_new[h]) for h in heads]
        kv = [_dot(kw[h].T, v[h]) for h in heads]
        for h in heads:
            C_ref[h] = dec[h] * Cst[h] + kv[h]
            n_ref[h:h + 1, :] = dec[h] * nst[h] + jnp.sum(kw[h], axis=0, keepdims=True)
            m_ref[h:h + 1, :] = jnp.broadcast_to(m_new[h], (1, LANES))
        for h in heads:
            num = sv[h] + sc[h] * qC[h]
            den = jnp.sum(s[h], axis=-1, keepdims=True) + sc[h] * jnp.sum(q[h] * nst[h], axis=-1, keepdims=True)
            hh = num / jnp.maximum(jnp.abs(den), jnp.exp(-mt[h]))
            hn = _head_norm_rows(hh, ng_ref[:, hs[h]], LN_EPS)
            out_ref[0, rows, hs[h]] = (hn * _sigmoid(o_ref[0, rows, hs[h]])).astype(out_ref.dtype)
        return carry

    lax.fori_loop(0, tb // CHUNK, chunk_body, 0)


def _mlstm_call(proj, conv_w, gate_b, norm_g, *, gate_col, tb=256):
    B, T, _ = proj.shape
    nt = T // tb
    gb = jnp.zeros((1, LANES), F32).at[0, :2 * N_HEADS].set(gate_b)
    ng = norm_g.reshape(1, D_GROUP)
    colblk = lambda j: pl.BlockSpec((1, tb, D_GROUP), lambda b, t: (b, t, j))
    halo = lambda j: pl.BlockSpec((1, 8, D_GROUP), lambda b, t: (b, jnp.maximum(t * (tb // 8) - 1, 0), j))
    return pl.pallas_call(
        functools.partial(_mlstm_kernel, tb=tb),
        grid=(B, nt),
        in_specs=[colblk(0), colblk(1), colblk(2), colblk(3),
                  pl.BlockSpec((1, tb, LANES), lambda b, t: (b, t, gate_col // LANES)),
                  halo(0), halo(1),
                  pl.BlockSpec((4, 2 * D_GROUP), lambda b, t: (0, 0)),
                  pl.BlockSpec((1, LANES), lambda b, t: (0, 0)),
                  pl.BlockSpec((1, D_GROUP), lambda b, t: (0, 0))],
        out_specs=pl.BlockSpec((1, tb, D_GROUP), lambda b, t: (b, t, 0)),
        out_shape=jax.ShapeDtypeStruct((B, T, D_GROUP), BF16),
        scratch_shapes=[pltpu.VMEM((tb, D_GROUP), F32), pltpu.VMEM((tb, D_GROUP), F32),
                        pltpu.VMEM((N_HEADS, HEAD_DIM, HEAD_DIM), F32),
                        pltpu.VMEM((N_HEADS, LANES), F32), pltpu.VMEM((N_HEADS, LANES), F32)],
        compiler_params=pltpu.CompilerParams(dimension_semantics=("arbitrary", "arbitrary"),
                                             vmem_limit_bytes=VMEM_LIMIT),
        name="mlstm",
    )(proj, proj, proj, proj, proj, proj, proj, conv_w, gb, ng)


def _rope_kernel(pos_ref, inv_ref, cos_ref, sin_ref):
    ang = pos_ref[0].astype(F32) * inv_ref[...]
    lane = lax.broadcasted_iota(jnp.int32, ang.shape, 1)
    cos_ref[0] = jnp.cos(ang)
    sin_ref[0] = jnp.where(lane < HEAD_DIM // 2, -jnp.sin(ang), jnp.sin(ang))


def _rope_call(positions, *, tb=512):
    B, T = positions.shape
    half = HEAD_DIM // 2
    inv_freq = ROPE_BASE ** (-jnp.arange(half, dtype=F32) / half)
    inv2 = jnp.concatenate([inv_freq, inv_freq]).reshape(1, HEAD_DIM)
    spec = pl.BlockSpec((1, tb, HEAD_DIM), lambda b, t: (b, t, 0))
    return pl.pallas_call(
        _rope_kernel,
        grid=(B, T // tb),
        in_specs=[pl.BlockSpec((1, tb, 1), lambda b, t: (b, t, 0)),
                  pl.BlockSpec((1, HEAD_DIM), lambda b, t: (0, 0))],
        out_specs=[spec, spec],
        out_shape=[jax.ShapeDtypeStruct((B, T, HEAD_DIM), F32)] * 2,
        compiler_params=pltpu.CompilerParams(dimension_semantics=("arbitrary", "arbitrary")),
        name="rope_table",
    )(positions.reshape(B, T, 1), inv2)


def _ret_kernel(q_ref, k_ref, v_ref, g_ref, cos_ref, sin_ref, ng_ref, out_ref, R_ref, *, tb):
    t = pl.program_id(1)

    @pl.when(t == 0)
    def _():
        R_ref[...] = jnp.zeros_like(R_ref)

    causal = _tri(CHUNK)
    ri = lax.broadcasted_iota(jnp.int32, (CHUNK, CHUNK), 0)
    ci = lax.broadcasted_iota(jnp.int32, (CHUNK, CHUNK), 1)
    rel = (ri - ci).astype(F32)
    tcol = lax.broadcasted_iota(jnp.int32, (CHUNK, 1), 0).astype(F32)
    scale = HEAD_DIM ** -0.5

    def chunk_body(c, carry):
        r0 = pl.multiple_of(c * CHUNK, CHUNK)
        rows = pl.ds(r0, CHUNK)
        cos2 = cos_ref[0, rows, :]
        sin2 = sin_ref[0, rows, :]
        for h in range(N_HEADS):
            hs = slice(h * HEAD_DIM, (h + 1) * HEAD_DIM)
            lg = math.log1p(-2.0 ** (-RET_GAMMA_BASE - h))
            q = q_ref[0, rows, hs]
            k = k_ref[0, rows, hs]
            v = v_ref[0, rows, hs]
            qr = q * cos2 + pltpu.roll(q, HEAD_DIM // 2, 1) * sin2
            kr = (k * cos2 + pltpu.roll(k, HEAD_DIM // 2, 1) * sin2) * scale
            dmat = jnp.where(causal, jnp.exp(rel * lg), 0.0)
            xi = jnp.exp((tcol + 1.0) * lg)
            zeta = jnp.exp((CHUNK - 1.0 - tcol) * lg)
            Rst = R_ref[h]
            intra = _dot(_dot_nt(qr, kr) * dmat, v)
            inter = _dot(qr, Rst) * xi
            R_ref[h] = Rst * math.exp(CHUNK * lg) + _dot((kr * zeta).T, v)
            on = _head_norm_rows(intra + inter, ng_ref[:, hs], LN_EPS)
            out_ref[0, rows, hs] = (on * _silu(g_ref[0, rows, hs])).astype(out_ref.dtype)
        return carry

    lax.fori_loop(0, tb // CHUNK, chunk_body, 0)


def _ret_call(proj, cos2, sin2, norm_g, *, tb=256):
    B, T, _ = proj.shape
    colblk = lambda j: pl.BlockSpec((1, tb, D_GROUP), lambda b, t: (b, t, j))
    tab = pl.BlockSpec((1, tb, HEAD_DIM), lambda b, t: (b, t, 0))
    return pl.pallas_call(
        functools.partial(_ret_kernel, tb=tb),
        grid=(B, T // tb),
        in_specs=[colblk(0), colblk(1), colblk(2), colblk(3), tab, tab,
                  pl.BlockSpec((1, D_GROUP), lambda b, t: (0, 0))],
        out_specs=pl.BlockSpec((1, tb, D_GROUP), lambda b, t: (b, t, 0)),
        out_shape=jax.ShapeDtypeStruct((B, T, D_GROUP), BF16),
        scratch_shapes=[pltpu.VMEM((N_HEADS, HEAD_DIM, HEAD_DIM), F32)],
        compiler_params=pltpu.CompilerParams(dimension_semantics=("arbitrary", "arbitrary"),
                                             vmem_limit_bytes=VMEM_LIMIT),
        name="retention",
    )(proj, proj, proj, proj, cos2, sin2, norm_g.reshape(1, D_GROUP))


def _inv_unit_lower(nms):
    n = nms[0].shape[0]
    eye = (lax.broadcasted_iota(jnp.int32, (n, n), 0) == lax.broadcasted_iota(jnp.int32, (n, n), 1)).astype(F32)
    ps = [eye + nm for nm in nms]
    xs = [_dot(nm, nm) for nm in nms]
    for _ in range(int(math.log2(n)) - 2):
        px = [_dot(jnp.concatenate([p, x], axis=0), x) for p, x in zip(ps, xs)]
        ps = [p + y[:n] for p, y in zip(ps, px)]
        xs = [y[n:] for y in px]
    ps = [p + _dot(p, x) for p, x in zip(ps, xs)]
    resid = [eye - p + _dot(nm, p) for p, nm in zip(ps, nms)]
    return [p + _dot(p, r) for p, r in zip(ps, resid)]


def _solve_unit_lower(nms, rhss):
    n = nms[0].shape[0]
    eye = (lax.broadcasted_iota(jnp.int32, (n, n), 0) == lax.broadcasted_iota(jnp.int32, (n, n), 1)).astype(F32)
    ps = [eye + nm for nm in nms]
    xs = [_dot(nm, nm) for nm in nms]
    for _ in range(int(math.log2(n)) - 2):
        px = [_dot(jnp.concatenate([p, x], axis=0), x) for p, x in zip(ps, xs)]
        ps = [p + y[:n] for p, y in zip(ps, px)]
        xs = [y[n:] for y in px]
    ps = [p + _dot(p, x) for p, x in zip(ps, xs)]
    x0 = [_dot(p, r) for p, r in zip(ps, rhss)]
    resid = [r - a + _dot(nm, a) for r, a, nm in zip(rhss, x0, nms)]
    return [a + _dot(p, r) for a, p, r in zip(x0, ps, resid)]


def _l2norm_rows(z):
    return z * lax.rsqrt(jnp.sum(z * z, axis=-1, keepdims=True) + 1e-6)


GDN_CHUNK_GROUP = 2


def _gdn_kernel(q_ref, k_ref, v_ref, z_ref, g_ref, qh_ref, kh_ref, vh_ref, cw_ref, an_ref, dt_ref, ng_ref,
                out_ref, qc_ref, kc_ref, vc_ref, u_ref, w_ref, qe_ref, o_ref, att_ref, kdT_ref, gl_ref, S_ref,
                *, tb):
    t = pl.program_id(1)
    first = t == 0

    @pl.when(first)
    def _():
        S_ref[...] = jnp.zeros_like(S_ref)

    srcs = ((q_ref, qh_ref, qc_ref), (k_ref, kh_ref, kc_ref), (v_ref, vh_ref, vc_ref))
    for c in range(tb // CHUNK):
        for cb in range(D_GROUP // 256):
            cs = cb * 256
            for i, (src, halo, dst) in enumerate(srcs):
                w = cw_ref[:, i * D_GROUP + cs:i * D_GROUP + cs + 256]
                dst[c * CHUNK:(c + 1) * CHUNK, cs:cs + 256] = _conv_silu_rows(
                    src, halo, w, first, c * CHUNK, CHUNK, cs, 256)

    causal = _tri(CHUNK)
    strict = _tri(CHUNK, strict=True)
    a_neg = an_ref[...]
    dtb = dt_ref[...]
    scale = HEAD_DIM ** -0.5
    n_chunks = tb // CHUNK
    heads = range(N_HEADS)
    hs = [slice(h * HEAD_DIM, (h + 1) * HEAD_DIM) for h in heads]

    for c0 in range(0, n_chunks, GDN_CHUNK_GROUP):
        chunks = range(c0, c0 + GDN_CHUNK_GROUP)
        rows = {c: slice(c * CHUNK, (c + 1) * CHUNK) for c in chunks}
        gz = {c: g_ref[0, rows[c], :] for c in chunks}
        beta = {c: _sigmoid(gz[c]) for c in chunks}
        gc = {c: _cumsum_rows(a_neg * _softplus(gz[c] + dtb)) for c in chunks}
        gcT = {c: gc[c].T for c in chunks}
        for c in chunks:
            gl_ref[c:c + 1, :] = gc[c][CHUNK - 1:CHUNK, :]
        items = [(c, h) for c in chunks for h in heads]
        n_items = range(len(items))
        gc_col = [gc[c][:, h:h + 1] for c, h in items]
        b_col = [beta[c][:, 8 + h:9 + h] for c, h in items]
        gamma = [jnp.where(causal, jnp.exp(gc_col[i] - gcT[c][h:h + 1, :]), 0.0) for i, (c, h) in enumerate(items)]
        q = [_l2norm_rows(qc_ref[rows[c], hs[h]]) * scale for c, h in items]
        k = [_l2norm_rows(kc_ref[rows[c], hs[h]]) for c, h in items]
        kb = [k[i] * b_col[i] for i in n_items]
        eg = [jnp.exp(gc_col[i]) for i in n_items]
        kq = [_dot_nt(jnp.concatenate([kb[i], q[i]], axis=0), k[i]) for i in n_items]
        inv = _inv_unit_lower([-jnp.where(strict, kq[i][:CHUNK] * gamma[i], 0.0) for i in n_items])
        uw = [_dot(inv[i], jnp.concatenate([vc_ref[rows[c], hs[h]] * b_col[i], kb[i] * eg[i]], axis=1))
              for i, (c, h) in enumerate(items)]
        for i, (c, h) in enumerate(items):
            u_ref[rows[c], hs[h]] = uw[i][:, :HEAD_DIM]
            w_ref[rows[c], hs[h]] = uw[i][:, HEAD_DIM:]
            qe_ref[rows[c], hs[h]] = q[i] * eg[i]
            att_ref[h, rows[c], :] = kq[i][CHUNK:] * gamma[i]
            g_last = gc[c][CHUNK - 1:CHUNK, h:h + 1]
            kdT_ref[c * N_HEADS + h] = (k[i] * jnp.exp(g_last - gc_col[i])).T

    for c in range(n_chunks):
        rows = slice(c * CHUNK, (c + 1) * CHUNK)
        S = [S_ref[h] for h in heads]
        ws = [_dot(jnp.concatenate([w_ref[rows, hs[h]], qe_ref[rows, hs[h]]], axis=0), S[h]) for h in heads]
        v_new = [u_ref[rows, hs[h]] - ws[h][:CHUNK] for h in heads]
        av = [_dot(att_ref[h, rows, :], v_new[h]) for h in heads]
        kv = [_dot(kdT_ref[c * N_HEADS + h], v_new[h]) for h in heads]
        for h in heads:
            S_ref[h] = S[h] * jnp.exp(gl_ref[c:c + 1, h:h + 1]) + kv[h]
            o_ref[rows, hs[h]] = ws[h][CHUNK:] + av[h]

    for c in range(n_chunks):
        rows = slice(c * CHUNK, (c + 1) * CHUNK)
        for h in heads:
            on = _head_norm_rows(o_ref[rows, hs[h]], ng_ref[:, hs[h]], 1e-6, center=False)
            out_ref[0, rows, hs[h]] = (on * _silu(z_ref[0, rows, hs[h]])).astype(out_ref.dtype)


def _gdn_call(proj, conv_w, a_log, dt_bias, norm_g, *, col0, tb=256):
    B, T, _ = proj.shape
    j0 = col0 // D_GROUP
    an = jnp.zeros((1, LANES), F32).at[0, :N_HEADS].set(-jnp.exp(a_log.astype(F32)))
    dtb = jnp.zeros((1, LANES), F32).at[0, :N_HEADS].set(dt_bias)
    colblk = lambda j: pl.BlockSpec((1, tb, D_GROUP), lambda b, t: (b, t, j0 + j))
    halo = lambda j: pl.BlockSpec((1, 8, D_GROUP), lambda b, t: (b, jnp.maximum(t * (tb // 8) - 1, 0), j0 + j))
    return pl.pallas_call(
        functools.partial(_gdn_kernel, tb=tb),
        grid=(B, T // tb),
        in_specs=[colblk(0), colblk(1), colblk(2), colblk(3),
                  pl.BlockSpec((1, tb, LANES), lambda b, t: (b, t, (col0 + 4 * D_GROUP) // LANES)),
                  halo(0), halo(1), halo(2),
                  pl.BlockSpec((4, 3 * D_GROUP), lambda b, t: (0, 0)),
                  pl.BlockSpec((1, LANES), lambda b, t: (0, 0)),
                  pl.BlockSpec((1, LANES), lambda b, t: (0, 0)),
                  pl.BlockSpec((1, D_GROUP), lambda b, t: (0, 0))],
        out_specs=pl.BlockSpec((1, tb, D_GROUP), lambda b, t: (b, t, 0)),
        out_shape=jax.ShapeDtypeStruct((B, T, D_GROUP), BF16),
        scratch_shapes=[pltpu.VMEM((tb, D_GROUP), F32)] * 7
                       + [pltpu.VMEM((N_HEADS, tb, CHUNK), F32),
                          pltpu.VMEM((tb // CHUNK * N_HEADS, HEAD_DIM, CHUNK), F32),
                          pltpu.VMEM((max(tb // CHUNK, 8), LANES), F32),
                          pltpu.VMEM((N_HEADS, HEAD_DIM, HEAD_DIM), F32)],
        compiler_params=pltpu.CompilerParams(dimension_semantics=("arbitrary", "arbitrary"),
                                             vmem_limit_bytes=VMEM_LIMIT),
        name="gdn",
    )(proj, proj, proj, proj, proj, proj, proj, proj, conv_w, an, dtb, norm_g.reshape(1, D_GROUP))


N_PAIRS = N_RWKV // 2
RWKV_LOW = 384


def _shift1_rows(src_ref, halo_ref, first, r0, nrows, c0, ncols):
    if r0 == 0:
        hl = jnp.where(first, 0.0, halo_ref[0, 7:8, c0:c0 + ncols])
        return jnp.concatenate([hl, src_ref[0, 0:nrows - 1, c0:c0 + ncols]], axis=0)
    return src_ref[0, r0 - 1:r0 - 1 + nrows, c0:c0 + ncols]


def _rwkv_kernel(r_ref, k_ref, v_ref, l0_ref, l1_ref, l2_ref,
                 rh_ref, kh_ref, vh_ref, l0h_ref, l1h_ref, l2h_ref,
                 mu_ref, mul_ref, w0_ref, w2_ref, a0_ref, a2_ref, g2_ref, kk_ref, ka_ref, rk_ref,
                 lng_ref, lnb_ref, out_ref,
                 gs_ref, bo_ref, y_ref, atrt_ref, avk_ref, yk_ref, inv_ref, arb_ref, btT_ref, kvT_ref, wlT_ref,
                 H_ref, *, tb):
    t = pl.program_id(1)
    first = t == 0

    @pl.when(first)
    def _():
        H_ref[...] = jnp.zeros_like(H_ref)

    ri = lax.broadcasted_iota(jnp.int32, (LANES, LANES), 0)
    ci = lax.broadcasted_iota(jnp.int32, (LANES, LANES), 1)
    same_head = ((ri // RWKV_HEAD) == (ci // RWKV_HEAD)).astype(F32)
    causal = _tri(CHUNK)
    strict = _tri(CHUNK, strict=True)
    lane1 = lax.broadcasted_iota(jnp.int32, (1, LANES), 1)
    m0 = (lane1 < RWKV_HEAD).astype(F32)
    m1 = 1.0 - m0
    t2 = lax.broadcasted_iota(jnp.int32, (CHUNK, LANES), 0)
    l2 = lax.broadcasted_iota(jnp.int32, (CHUNK, LANES), 1)
    lo = l2 < RWKV_HEAD
    s2 = l2 & (RWKV_HEAD - 1)
    causal_hi = (t2 >= s2) & (l2 >= RWKV_HEAD)
    strict_hi = (t2 > s2) & (l2 >= RWKV_HEAD)
    n_chunks = tb // CHUNK
    pairs = range(N_PAIRS)
    halves = [(p, hh) for p in pairs for hh in range(2)]
    ps = [slice(p * LANES, (p + 1) * LANES) for p in pairs]

    def lerp(src, halo, mu, r0, c0, ncols):
        cur = src[0, r0:r0 + CHUNK, c0:c0 + ncols]
        return cur + (_shift1_rows(src, halo, first, r0, CHUNK, c0, ncols) - cur) * mu

    for c in range(n_chunks):
        r0 = c * CHUNK
        rows = slice(r0, r0 + CHUNK)
        wl = lerp(l0_ref, l0h_ref, mul_ref[:, 0:LANES], r0, 0, LANES)
        g1 = lerp(l1_ref, l1h_ref, mul_ref[:, LANES:2 * LANES], r0, 0, LANES)
        g2 = lerp(l2_ref, l2h_ref, mul_ref[:, 2 * LANES:3 * LANES], r0, 0, LANES)
        wl_t = jnp.where(lo, jnp.tanh(wl), 0.0)
        al = jnp.where(lo, 0.0, wl)
        sg1 = _sigmoid(g1)
        sg2 = jnp.where(l2 < 32, _sigmoid(g2), 0.0)
        w = [-_softplus(-(w0_ref[:, ps[p]] + _dot(wl_t, w2_ref[:, ps[p]]))) - 0.5 for p in pairs]
        a = [_sigmoid(a0_ref[:, ps[p]] + _dot(al, a2_ref[:, ps[p]])) for p in pairs]
        g = [_dot(sg1, g2_ref[0:LANES, ps[p]]) + _dot(sg2, g2_ref[LANES:2 * LANES, ps[p]]) for p in pairs]
        r = [lerp(r_ref, rh_ref, mu_ref[:, ps[p]], r0, p * LANES, LANES) for p in pairs]
        k = [lerp(k_ref, kh_ref, mu_ref[:, D_GROUP + p * LANES:D_GROUP + (p + 1) * LANES], r0, p * LANES, LANES)
             for p in pairs]
        v = [lerp(v_ref, vh_ref, mu_ref[:, 2 * D_GROUP + p * LANES:2 * D_GROUP + (p + 1) * LANES], r0, p * LANES, LANES)
             for p in pairs]
        kk = [k[p] * kk_ref[:, ps[p]] for p in pairs]
        nrm = [jnp.sqrt(_group_sum(kk[p] * kk[p], same_head)) for p in pairs]
        kk = [kk[p] / jnp.maximum(nrm[p], 1e-12) for p in pairs]
        k2 = [k[p] * (1.0 + (a[p] - 1.0) * ka_ref[:, ps[p]]) for p in pairs]
        rk = [_group_sum(r[p] * k2[p] * rk_ref[:, ps[p]], same_head) for p in pairs]
        for p in pairs:
            gs_ref[p, rows, :] = g[p]
            bo_ref[p, rows, :] = rk[p] * v[p]
        lw = [-jnp.exp(w[p]) for p in pairs]
        cs = [_cumsum_rows(lw[p]) for p in pairs]
        w_inv = [jnp.exp(-cs[p]) for p in pairs]
        w_end = [jnp.exp(cs[p][CHUNK - 1:CHUNK, :]) for p in pairs]
        rt = [r[p] * jnp.exp(cs[p]) for p in pairs]
        at = [-kk[p] * jnp.exp(cs[p] - lw[p]) for p in pairs]
        bt = [kk[p] * a[p] * w_inv[p] for p in pairs]
        kt = [k2[p] * w_inv[p] for p in pairs]
        pm = [_dot_nt(jnp.concatenate([at[p] * m0, at[p] * m1, rt[p] * m0, rt[p] * m1], axis=0),
                      jnp.concatenate([bt[p], kt[p]], axis=0)) for p in pairs]
        vv = [jnp.concatenate([v[p], v[p]], axis=0).astype(BF16) for p in pairs]
        pa = [pm[p][hh * CHUNK:(hh + 1) * CHUNK] for p, hh in halves]
        pr = [pm[p][(2 + hh) * CHUNK:(3 + hh) * CHUNK] for p, hh in halves]
        inv = _inv_unit_lower([jnp.where(strict, x[:, :CHUNK], 0.0) for x in pa])
        avk = [_dot(jnp.where(strict_hi, pa[2 * p + hh], 0.0), vv[p]) for p, hh in halves]
        yk = [_dot(jnp.where(causal_hi, pr[2 * p + hh], 0.0), vv[p]) for p, hh in halves]
        kv = [_dot((kt[p] * w_end[p]).T, v[p]) * same_head for p in pairs]
        for p in pairs:
            i = c * N_PAIRS + p
            atrt_ref[i] = jnp.concatenate([at[p], rt[p]], axis=0).astype(BF16)
            avk_ref[i] = jnp.where(lo, avk[2 * p], avk[2 * p + 1])
            yk_ref[i] = jnp.where(lo, yk[2 * p], yk[2 * p + 1])
            btT_ref[i] = (bt[p] * w_end[p]).T.astype(BF16)
            kvT_ref[i] = kv[p]
            wlT_ref[i] = jnp.broadcast_to(w_end[p], (LANES, LANES)).T
            for hh in range(2):
                inv_ref[2 * i + hh] = inv[2 * p + hh].astype(BF16)
                arb_ref[2 * i + hh] = jnp.where(causal, pr[2 * p + hh][:, :CHUNK], 0.0).astype(BF16)

    for c in range(n_chunks):
        rows = slice(c * CHUNK, (c + 1) * CHUNK)
        it = [c * N_PAIRS + p for p in pairs]
        H = [H_ref[p] for p in pairs]
        xy0 = [_dot(atrt_ref[it[p]], H[p]) for p in pairs]
        x = [xy0[p][:CHUNK] + avk_ref[it[p]] for p in pairs]
        us = [_dot(inv_ref[2 * it[p] + hh], x[p]) for p, hh in halves]
        u = [jnp.where(lo, us[2 * p], us[2 * p + 1]) for p in pairs]
        yb = [_dot(arb_ref[2 * it[p] + hh], u[p]) for p, hh in halves]
        bu = [_dot(btT_ref[it[p]], u[p]) for p in pairs]
        for p in pairs:
            y_ref[p, rows, :] = xy0[p][CHUNK:] + yk_ref[it[p]] + jnp.where(lo, yb[2 * p], yb[2 * p + 1])
            H_ref[p] = H[p] * wlT_ref[it[p]] + bu[p] * same_head + kvT_ref[it[p]]

    inv_n = 1.0 / RWKV_HEAD
    for c in range(n_chunks):
        rows = slice(c * CHUNK, (c + 1) * CHUNK)
        y = [y_ref[p, rows, :] for p in pairs]
        yc = [y[p] - _group_sum(y[p], same_head) * inv_n for p in pairs]
        var = [_group_sum(yc[p] * yc[p], same_head) * inv_n for p in pairs]
        for p in pairs:
            yn = yc[p] * lax.rsqrt(var[p] + RWKV_LN_EPS) * lng_ref[:, ps[p]] + lnb_ref[:, ps[p]]
            out_ref[0, rows, ps[p]] = ((yn + bo_ref[p, rows, :]) * gs_ref[p, rows, :]).astype(out_ref.dtype)


def _rwkv_call(proj, mu, w0, w2, a0, a2, g2, k_k, k_a, r_k, ln_g, ln_b, *, col0, tb=256):
    B, T, _ = proj.shape
    j0 = col0 // D_GROUP
    l0 = (col0 + 3 * D_GROUP) // LANES
    row = lambda a: a.reshape(1, -1).astype(F32)
    mul = jnp.zeros((1, RWKV_LOW), F32).at[0, :288].set(mu[3 * D_GROUP:])
    w2p = jnp.zeros((LANES, D_GROUP), F32).at[:64].set(w2)
    a2p = jnp.zeros((LANES, D_GROUP), F32).at[64:].set(a2)
    g2p = jnp.zeros((2 * LANES, D_GROUP), F32).at[:160].set(g2)
    colblk = lambda j: pl.BlockSpec((1, tb, D_GROUP), lambda b, t: (b, t, j0 + j))
    lowblk = lambda j: pl.BlockSpec((1, tb, LANES), lambda b, t: (b, t, l0 + j))
    hrow = lambda t: jnp.maximum(t * (tb // 8) - 1, 0)
    halo = lambda j: pl.BlockSpec((1, 8, D_GROUP), lambda b, t: (b, hrow(t), j0 + j))
    lowhalo = lambda j: pl.BlockSpec((1, 8, LANES), lambda b, t: (b, hrow(t), l0 + j))
    full = lambda a: pl.BlockSpec(a.shape, lambda b, t: (0,) * a.ndim)
    params = [row(mu[:3 * D_GROUP]), mul, row(w0), w2p, row(a0), a2p, g2p, row(k_k), row(k_a), row(r_k),
              row(ln_g), row(ln_b)]
    big = pltpu.VMEM((N_PAIRS, tb, LANES), F32)
    n_items = tb // CHUNK * N_PAIRS
    return pl.pallas_call(
        functools.partial(_rwkv_kernel, tb=tb),
        grid=(B, T // tb),
        in_specs=[colblk(0), colblk(1), colblk(2), lowblk(0), lowblk(1), lowblk(2),
                  halo(0), halo(1), halo(2), lowhalo(0), lowhalo(1), lowhalo(2)] + [full(a) for a in params],
        out_specs=pl.BlockSpec((1, tb, D_GROUP), lambda b, t: (b, t, 0)),
        out_shape=jax.ShapeDtypeStruct((B, T, D_GROUP), BF16),
        scratch_shapes=[big] * 3 + [
            pltpu.VMEM((n_items, 2 * CHUNK, LANES), BF16),
            pltpu.VMEM((n_items, CHUNK, LANES), F32),
            pltpu.VMEM((n_items, CHUNK, LANES), F32),
            pltpu.VMEM((2 * n_items, CHUNK, CHUNK), BF16),
            pltpu.VMEM((2 * n_items, CHUNK, CHUNK), BF16),
            pltpu.VMEM((n_items, LANES, CHUNK), BF16),
            pltpu.VMEM((n_items, LANES, LANES), F32),
            pltpu.VMEM((n_items, LANES, LANES), F32),
            pltpu.VMEM((N_PAIRS, LANES, LANES), F32)],
        compiler_params=pltpu.CompilerParams(dimension_semantics=("arbitrary", "arbitrary"),
                                             vmem_limit_bytes=VMEM_LIMIT),
        name="rwkv7",
    )(*([proj] * 12), *params)


def _ada_kernel(c_ref, w_ref, b_ref, out_ref):
    sc = _silu(c_ref[...]).astype(BF16)
    out_ref[0] = jnp.dot(sc, w_ref[0].astype(BF16), preferred_element_type=F32) + b_ref[0]


def _ada_call(c, ada_w, ada_b, *, tn=1536):
    B = c.shape[0]
    n_mod = ada_w.shape[0] * ada_w.shape[1]
    w = ada_w.reshape(n_mod, D_MODEL, 3 * D_MODEL)
    b = ada_b.reshape(n_mod, 1, 3 * D_MODEL)
    return pl.pallas_call(
        _ada_kernel,
        grid=(n_mod, 3 * D_MODEL // tn),
        in_specs=[pl.BlockSpec((B, D_MODEL), lambda i, j: (0, 0)),
                  pl.BlockSpec((1, D_MODEL, tn), lambda i, j: (i, 0, j)),
                  pl.BlockSpec((1, 1, tn), lambda i, j: (i, 0, j))],
        out_specs=pl.BlockSpec((1, B, tn), lambda i, j: (i, 0, j)),
        out_shape=jax.ShapeDtypeStruct((n_mod, B, 3 * D_MODEL), F32),
        compiler_params=pltpu.CompilerParams(dimension_semantics=("arbitrary", "arbitrary"),
                                             vmem_limit_bytes=VMEM_LIMIT),
        name="adaln",
    )(c, w, b)


def _mod_spec(i, part, nb):
    return pl.BlockSpec((1, 1, D_MODEL), lambda b, t: (i * nb + b, 0, part))


def _modulate_kernel(x_ref, shift_ref, scale_ref, h_ref):
    h_ref[0] = (x_ref[0] * (1.0 + scale_ref[0]) + shift_ref[0]).astype(h_ref.dtype)


def _modulate_call(x, mods3, i, *, tb=512):
    B, T, _ = x.shape
    blk = pl.BlockSpec((1, tb, D_MODEL), lambda b, t: (b, t, 0))
    return pl.pallas_call(
        _modulate_kernel,
        grid=(B, T // tb),
        in_specs=[blk, _mod_spec(i, 0, B), _mod_spec(i, 1, B)],
        out_specs=blk,
        out_shape=jax.ShapeDtypeStruct(x.shape, BF16),
        compiler_params=pltpu.CompilerParams(dimension_semantics=("arbitrary", "arbitrary"),
                                             vmem_limit_bytes=VMEM_LIMIT),
        name="modulate",
    )(x, mods3, mods3)


LN_ROWS = 16


def _proj_ln_kernel(*refs, n_lhs, nk, tm, with_next):
    lhs = refs[:n_lhs]
    ws = refs[n_lhs:2 * n_lhs]
    x_ref, gate_ref, g_ref, b_ref = refs[2 * n_lhs:2 * n_lhs + 4]
    if with_next:
        shift_ref, scale_ref, xo_ref, h_ref, acc_ref = refs[2 * n_lhs + 4:]
    else:
        xo_ref, acc_ref = refs[2 * n_lhs + 4:]
    k = pl.program_id(2)
    def partial_product():
        part = jnp.dot(lhs[0][0], ws[0][...], preferred_element_type=F32)
        for j in range(1, n_lhs):
            part = part + jnp.dot(lhs[j][0], ws[j][...], preferred_element_type=F32)
        return part

    @pl.when(k == 0)
    def _():
        acc_ref[...] = partial_product()

    @pl.when(k > 0)
    def _():
        acc_ref[...] += partial_product()

    @pl.when(k == nk - 1)
    def _():
        gate1 = 1.0 + gate_ref[0]
        g, b = g_ref[...], b_ref[...]
        if with_next:
            scale1, shift = 1.0 + scale_ref[0], shift_ref[0]

        def rows_body(r, carry):
            rows = pl.ds(pl.multiple_of(r * LN_ROWS, LN_ROWS), LN_ROWS)
            z = ALPHA * x_ref[0, rows, :] + gate1 * acc_ref[rows, :]
            zc = z - jnp.mean(z, axis=-1, keepdims=True)
            var = jnp.mean(zc * zc, axis=-1, keepdims=True)
            xn = zc * lax.rsqrt(var + LN_EPS) * g + b
            xo_ref[0, rows, :] = xn
            if with_next:
                h_ref[0, rows, :] = (xn * scale1 + shift).astype(h_ref.dtype)
            return carry

        lax.fori_loop(0, tm // LN_ROWS, rows_body, 0, unroll=8)


def _proj_ln_call(lhs, ws, x, mods3, i, g, b, *, with_next, tm, tk):
    B, T, D = x.shape
    n_lhs = len(lhs)
    nk = lhs[0].shape[2] // tk
    blk = pl.BlockSpec((1, tm, D), lambda b, t, k: (b, t, 0))
    row = pl.BlockSpec((1, D), lambda b, t, k: (0, 0))
    mod = lambda ii, part: pl.BlockSpec((1, 1, D), lambda b, t, k: (ii * B + b, 0, part))
    in_specs = ([pl.BlockSpec((1, tm, tk), lambda b, t, k: (b, t, k))] * n_lhs
                + [pl.BlockSpec((tk, D), lambda b, t, k: (k, 0))] * n_lhs
                + [blk, mod(i, 2), row, row])
    args = list(lhs) + list(ws) + [x, mods3, g.reshape(1, D), b.reshape(1, D)]
    out_specs = [blk]
    out_shape = [jax.ShapeDtypeStruct(x.shape, F32)]
    if with_next:
        in_specs += [mod(i + 1, 0), mod(i + 1, 1)]
        args += [mods3, mods3]
        out_specs.append(blk)
        out_shape.append(jax.ShapeDtypeStruct(x.shape, BF16))
    return pl.pallas_call(
        functools.partial(_proj_ln_kernel, n_lhs=n_lhs, nk=nk, tm=tm, with_next=with_next),
        grid=(B, T // tm, nk),
        in_specs=in_specs, out_specs=out_specs, out_shape=out_shape,
        scratch_shapes=[pltpu.VMEM((tm, D), F32)],
        compiler_params=pltpu.CompilerParams(dimension_semantics=("arbitrary", "arbitrary", "arbitrary"),
                                             vmem_limit_bytes=VMEM_LIMIT),
        name="proj_residual_ln",
    )(*args)


def _matmul_kernel(a_ref, b_ref, o_ref, *scratch, nk):
    if nk == 1:
        o_ref[...] = jnp.dot(a_ref[...], b_ref[...], preferred_element_type=F32).astype(o_ref.dtype)
        return
    (acc_ref,) = scratch
    k = pl.program_id(2)

    @pl.when(k == 0)
    def _():
        acc_ref[...] = jnp.zeros_like(acc_ref)

    acc_ref[...] += jnp.dot(a_ref[...], b_ref[...], preferred_element_type=F32)

    @pl.when(k == nk - 1)
    def _():
        o_ref[...] = acc_ref[...].astype(o_ref.dtype)


def _matmul_call(a, b, *, tm, tn, tk, out_dtype=F32):
    M, K = a.shape
    _, N = b.shape
    nk = K // tk
    return pl.pallas_call(
        functools.partial(_matmul_kernel, nk=nk),
        grid=(N // tn, M // tm, nk),
        in_specs=[pl.BlockSpec((tm, tk), lambda j, i, k: (i, k)),
                  pl.BlockSpec((tk, tn), lambda j, i, k: (k, j))],
        out_specs=pl.BlockSpec((tm, tn), lambda j, i, k: (i, j)),
        out_shape=jax.ShapeDtypeStruct((M, N), out_dtype),
        scratch_shapes=[] if nk == 1 else [pltpu.VMEM((tm, tn), F32)],
        compiler_params=pltpu.CompilerParams(dimension_semantics=("arbitrary", "arbitrary", "arbitrary"),
                                             vmem_limit_bytes=VMEM_LIMIT),
        name="matmul",
    )(a, b)


def _swiglu_up_kernel(h_ref, wg_ref, wu_ref, o_ref, wgb_ref, wub_ref):
    @pl.when(pl.program_id(1) == 0)
    def _():
        wgb_ref[...] = wg_ref[...].astype(BF16)
        wub_ref[...] = wu_ref[...].astype(BF16)

    h = h_ref[...]
    g = jnp.dot(h, wgb_ref[...], preferred_element_type=F32)
    u = jnp.dot(h, wub_ref[...], preferred_element_type=F32)
    o_ref[...] = (_silu(g) * u).astype(o_ref.dtype)


def _swiglu_up_call(h, wg, wu, layer, *, tm=512, tn=512):
    M, K = h.shape
    _, _, N = wg.shape
    wspec = pl.BlockSpec((None, K, tn), lambda j, i: (layer, 0, j))
    return pl.pallas_call(
        _swiglu_up_kernel,
        grid=(N // tn, M // tm),
        in_specs=[pl.BlockSpec((tm, K), lambda j, i: (i, 0)), wspec, wspec],
        out_specs=pl.BlockSpec((tm, tn), lambda j, i: (i, j)),
        out_shape=jax.ShapeDtypeStruct((M, N), BF16),
        scratch_shapes=[pltpu.VMEM((K, tn), BF16), pltpu.VMEM((K, tn), BF16)],
        compiler_params=pltpu.CompilerParams(dimension_semantics=("arbitrary", "arbitrary"),
                                             vmem_limit_bytes=VMEM_LIMIT),
        name="swiglu_up",
    )(h, wg, wu)


AB_COLS = 7680
CD_COLS = 8320
RWKV_COL0 = 4 * D_GROUP
MLSTM_GATE_COL = 7 * D_GROUP + RWKV_LOW
GDN_COL0 = 4 * D_GROUP


def _pad_cols(w, n):
    return jnp.pad(w, ((0, 0), (0, n - w.shape[1])))


def _ab_weight(w_in):
    a_main, a_gates = w_in[:, :4 * D_GROUP], w_in[:, 4 * D_GROUP:4 * D_GROUP + 2 * N_HEADS]
    b0 = 4 * D_GROUP + 2 * N_HEADS
    b_main, b_low = w_in[:, b0:b0 + 3 * D_GROUP], w_in[:, b0 + 3 * D_GROUP:]
    return jnp.concatenate([a_main, b_main, _pad_cols(b_low, RWKV_LOW), _pad_cols(a_gates, LANES)],
                           axis=1).astype(BF16)


def _cd_weight(w_in):
    return _pad_cols(w_in, CD_COLS).astype(BF16)


def kernel(x, c, positions, ada_w, ada_b, ln_g, ln_b, ab_w_in, ab_w_out, mlstm_conv_w, mlstm_gate_b, mlstm_norm_g, rwkv_mu, rwkv_w0, rwkv_w2, rwkv_a0, rwkv_a2, rwkv_g2, rwkv_k_k, rwkv_k_a, rwkv_r_k, rwkv_ln_g, rwkv_ln_b, cd_w_in, cd_w_out, ret_norm_g, gdn_conv_w, gdn_a_log, gdn_dt_bias, gdn_norm_g, ffn_w_gate, ffn_w_up, ffn_w_down):
    B, T, D = x.shape
    M = B * T
    depth = ada_w.shape[0]
    mods = _ada_call(c, ada_w, ada_b)
    mods3 = mods.reshape(2 * depth * B, 1, 3 * D)
    cos2, sin2 = _rope_call(positions)
    h = _modulate_call(x, mods3, 0)
    for layer in range(depth):
        j = layer // 2
        i_mix, i_ffn = 2 * layer, 2 * layer + 1
        if layer % 2 == 0:
            proj = _matmul_call(h.reshape(M, D), _ab_weight(ab_w_in[j]), tm=512, tn=1280, tk=D)
            proj = proj.reshape(B, T, AB_COLS)
            ya = _mlstm_call(proj, mlstm_conv_w[j], mlstm_gate_b[j], mlstm_norm_g[j], gate_col=MLSTM_GATE_COL)
            yb = _rwkv_call(proj, rwkv_mu[j], rwkv_w0[j], rwkv_w2[j], rwkv_a0[j], rwkv_a2[j], rwkv_g2[j],
                            rwkv_k_k[j], rwkv_k_a[j], rwkv_r_k[j], rwkv_ln_g[j], rwkv_ln_b[j], col0=RWKV_COL0)
            w_out = ab_w_out[j]
        else:
            proj = _matmul_call(h.reshape(M, D), _cd_weight(cd_w_in[j]), tm=512, tn=1664, tk=D)
            proj = proj.reshape(B, T, CD_COLS)
            ya = _ret_call(proj, cos2, sin2, ret_norm_g[j])
            yb = _gdn_call(proj, gdn_conv_w[j], gdn_a_log[j], gdn_dt_bias[j], gdn_norm_g[j], col0=GDN_COL0)
            w_out = cd_w_out[j]
        w_out = w_out.astype(BF16)
        x, h = _proj_ln_call([ya, yb], [w_out[:D_GROUP], w_out[D_GROUP:]], x, mods3, i_mix,
                             ln_g[layer, 0], ln_b[layer, 0], with_next=True, tm=256, tk=D_GROUP)
        act = _swiglu_up_call(h.reshape(M, D), ffn_w_gate, ffn_w_up, layer)
        last = layer == depth - 1
        res = _proj_ln_call([act.reshape(B, T, D_FF)], [ffn_w_down[layer].astype(BF16)], x, mods3, i_ffn,
                            ln_g[layer, 1], ln_b[layer, 1], with_next=not last, tm=512, tk=1408)
        if last:
            (x,) = res
        else:
            x, h = res
    return x
```

```python
import functools
import math

import numpy as np
import jax
import jax.numpy as jnp
from jax import lax
from jax.experimental import pallas as pl
from jax.experimental.pallas import tpu as pltpu

F32 = jnp.float32
BF16 = jnp.bfloat16

D_MODEL = 2048
D_GROUP = 1024
HEAD_DIM = 128
N_HEADS = 8
RWKV_HEAD = 64
N_RWKV = 16
CHUNK = 64
D_FF = 5632
DEPTH = 2
ALPHA = (2 * DEPTH) ** 0.25
LN_EPS = 1e-5
RWKV_LN_EPS = 64e-5
ROPE_BASE = 10000.0
RET_GAMMA_BASE = 5.0
LANES = 128
VMEM_LIMIT = 48 * 1024 * 1024


def _dot(a, b):
    return jnp.dot(a.astype(BF16), b.astype(BF16), preferred_element_type=F32)


def _dot_nt(a, b):
    return lax.dot_general(a.astype(BF16), b.astype(BF16), (((1,), (1,)), ((), ())),
                           preferred_element_type=F32)


def _split3(x):
    hi = x.astype(BF16)
    r1 = x - hi.astype(F32)
    mid = r1.astype(BF16)
    lo = (r1 - mid.astype(F32)).astype(BF16)
    return hi, mid, lo


def _cumsum_rows(x):
    n = x.shape[1]
    out = jnp.dot(_tri(CHUNK).astype(BF16), jnp.concatenate(_split3(x), axis=1), preferred_element_type=F32)
    return out[:, :n] + out[:, n:2 * n] + out[:, 2 * n:]


def _group_sum(x, ones01):
    m = x.shape[0]
    hi = x.astype(BF16)
    lo = (x - hi.astype(F32)).astype(BF16)
    out = jnp.dot(jnp.concatenate([hi, lo], axis=0), ones01.astype(BF16), preferred_element_type=F32)
    return out[:m] + out[m:]


def _sigmoid(x):
    return 1.0 / (1.0 + jnp.exp(-x))


def _silu(x):
    return x * _sigmoid(x)


def _log_sigmoid(x):
    return jnp.minimum(x, 0.0) - jnp.log1p(jnp.exp(-jnp.abs(x)))


def _softplus(x):
    return jnp.maximum(x, 0.0) + jnp.log1p(jnp.exp(-jnp.abs(x)))


def _tri(n, strict=False):
    r = lax.broadcasted_iota(jnp.int32, (n, n), 0)
    c = lax.broadcasted_iota(jnp.int32, (n, n), 1)
    return (r > c) if strict else (r >= c)


def _conv_silu_rows(src_ref, halo_ref, w, first, r0, nrows, c0, ncols):
    cur = src_ref[0, r0:r0 + nrows, c0:c0 + ncols]
    acc = w[3:4] * cur
    if r0 == 0:
        hl = jnp.where(first, 0.0, halo_ref[0, :, c0:c0 + ncols])
        ext = jnp.concatenate([hl, cur[0:8]], axis=0)
        for j in range(3):
            head = ext[5 + j:13 + j]
            if nrows > 8:
                rest = src_ref[0, 5 + j:nrows - 3 + j, c0:c0 + ncols]
                sh = jnp.concatenate([head, rest], axis=0)
            else:
                sh = head
            acc = acc + w[j:j + 1] * sh
    else:
        for j in range(3):
            acc = acc + w[j:j + 1] * src_ref[0, r0 - 3 + j:r0 - 3 + j + nrows, c0:c0 + ncols]
    return _silu(acc)


def _head_norm_rows(h, g_row, eps, center=True):
    if center:
        h = h - jnp.mean(h, axis=-1, keepdims=True)
    return h * lax.rsqrt(jnp.mean(h * h, axis=-1, keepdims=True) + eps) * g_row


MLSTM_CHUNK_GROUP = 2


def _cummax_rows(x):
    row = lax.broadcasted_iota(jnp.int32, x.shape, 0)
    d = 1
    while d < x.shape[0]:
        x = jnp.where(row >= d, jnp.maximum(x, pltpu.roll(x, d, 0)), x)
        d *= 2
    return x


def _bcast_head_cols(x, sel, pieces):
    m = x.shape[0]
    parts, rest = [], x
    for _ in range(pieces):
        hi = rest.astype(BF16)
        parts.append(hi)
        rest = rest - hi.astype(F32)
    out = jnp.dot(jnp.concatenate(parts, axis=0), sel, preferred_element_type=F32)
    acc = out[:m]
    for i in range(1, pieces):
        acc = acc + out[i * m:(i + 1) * m]
    return acc


def _mlstm_kernel(q_ref, k_ref, v_ref, o_ref, g_ref, qh_ref, kh_ref, cw_ref, gb_ref, ng_ref,
                  out_ref, qc_ref, kc_ref, sv_ref, rs_ref, hh_ref, b0_ref, cm_ref, kvn_ref, CN_ref, m_ref, *, tb):
    t = pl.program_id(1)
    first = t == 0

    @pl.when(first)
    def _():
        CN_ref[...] = jnp.zeros_like(CN_ref)
        m_ref[...] = jnp.zeros_like(m_ref)

    for c in range(tb // CHUNK):
        for cb in range(D_GROUP // 256):
            cs = cb * 256
            qc_ref[c * CHUNK:(c + 1) * CHUNK, cs:cs + 256] = _conv_silu_rows(
                q_ref, qh_ref, cw_ref[:, cs:cs + 256], first, c * CHUNK, CHUNK, cs, 256)
            kc_ref[c * CHUNK:(c + 1) * CHUNK, cs:cs + 256] = _conv_silu_rows(
                k_ref, kh_ref, cw_ref[:, D_GROUP + cs:D_GROUP + cs + 256], first, c * CHUNK, CHUNK, cs, 256)

    causal = _tri(CHUNK)
    gb = gb_ref[...]
    scale = HEAD_DIM ** -0.5
    n_chunks = tb // CHUNK
    heads = range(N_HEADS)
    hs = [slice(h * HEAD_DIM, (h + 1) * HEAD_DIM) for h in heads]
    head_lane = lax.broadcasted_iota(jnp.int32, (1, LANES), 1) < N_HEADS
    sel = (lax.broadcasted_iota(jnp.int32, (LANES, N_HEADS * LANES), 0)
           == (lax.broadcasted_iota(jnp.int32, (LANES, N_HEADS * LANES), 1) >> 7)).astype(BF16)
    ones = jnp.ones((CHUNK, HEAD_DIM), F32)
    last = slice(CHUNK - 1, CHUNK)

    for c0 in range(0, n_chunks, MLSTM_CHUNK_GROUP):
        chunks = range(c0, c0 + MLSTM_CHUNK_GROUP)
        rows = {c: slice(c * CHUNK, (c + 1) * CHUNK) for c in chunks}
        z = {c: g_ref[0, rows[c], :] + gb for c in chunks}
        b0 = {c: pltpu.roll(_cumsum_rows(_log_sigmoid(z[c])), LANES - N_HEADS, 1) for c in chunks}
        cv = {c: jnp.where(head_lane, z[c] - b0[c], 0.0) for c in chunks}
        cm = {c: _cummax_rows(cv[c]) for c in chunks}
        cT = {c: cv[c].T for c in chunks}
        cmb = {c: _bcast_head_cols(cm[c], sel, 3) for c in chunks}
        e1b = {c: _bcast_head_cols(jnp.where(head_lane, jnp.exp(cv[c] - cm[c][last]), 0.0), sel, 2) for c in chunks}
        for c in chunks:
            b0_ref[rows[c], :] = b0[c]
            cm_ref[rows[c], :] = cm[c]
        items = [(c, h) for c in chunks for h in heads]
        n_items = range(len(items))
        q = [qc_ref[rows[c], hs[h]] for c, h in items]
        k = [kc_ref[rows[c], hs[h]] * scale for c, h in items]
        vo = [jnp.concatenate([v_ref[0, rows[c], hs[h]], ones], axis=1).astype(BF16) for c, h in items]
        qk = [_dot_nt(q[i], k[i]) for i in n_items]
        s = [qk[i] * jnp.where(causal, jnp.exp(jnp.minimum(cT[c][h:h + 1, :] - cmb[c][:, h * LANES:h * LANES + CHUNK],
                                                           0.0)), 0.0) for i, (c, h) in enumerate(items)]
        s_hi = [s[i].astype(BF16) for i in n_items]
        s_lo = [(s[i] - s_hi[i].astype(F32)).astype(BF16) for i in n_items]
        svr = [jnp.dot(jnp.concatenate([s_hi[i], s_lo[i]], axis=0), vo[i], preferred_element_type=F32)
               for i in n_items]
        kvn = [_dot((k[i] * e1b[c][:, hs[h]]).T, vo[i]) for i, (c, h) in enumerate(items)]
        for i, (c, h) in enumerate(items):
            sv_ref[rows[c], hs[h]] = svr[i][:CHUNK, :HEAD_DIM] + svr[i][CHUNK:, :HEAD_DIM]
            rs_ref[rows[c], hs[h]] = svr[i][:CHUNK, HEAD_DIM:] + svr[i][CHUNK:, HEAD_DIM:]
            kvn_ref[c * N_HEADS + h] = kvn[i]

    for c in range(n_chunks):
        rows = slice(c * CHUNK, (c + 1) * CHUNK)
        m = m_ref[0:1, :]
        b0 = b0_ref[rows, :]
        cm = cm_ref[rows, :]
        mx = jnp.maximum(cm, m)
        m_new = jnp.maximum(b0[last] + m, b0[last] + cm[last])
        m_ref[0:1, :] = m_new
        zero = lambda x: jnp.where(head_lane, x, 0.0)
        fib = _bcast_head_cols(zero(jnp.exp(cm - mx)), sel, 2)
        scb = _bcast_head_cols(zero(jnp.exp(m - mx)), sel, 2)
        emtb = _bcast_head_cols(zero(jnp.exp(jnp.minimum(-(b0 + mx), 80.0))), sel, 2)
        dfb = _bcast_head_cols(jnp.concatenate([zero(jnp.exp(b0[last] + m - m_new)),
                                                zero(jnp.exp(b0[last] + cm[last] - m_new)),
                                                jnp.zeros((6, LANES), F32)], axis=0), sel, 2)
        CN = [CN_ref[h] for h in heads]
        qcn = [_dot(qc_ref[rows, hs[h]], CN[h]) for h in heads]
        for h in heads:
            num = fib[:, hs[h]] * sv_ref[rows, hs[h]] + scb[:, hs[h]] * qcn[h][:, :HEAD_DIM]
            den = fib[:, hs[h]] * rs_ref[rows, hs[h]] + scb[:, hs[h]] * qcn[h][:, HEAD_DIM:]
            hh_ref[rows, hs[h]] = num / jnp.maximum(jnp.abs(den), emtb[:, hs[h]])
            dec = jnp.concatenate([dfb[0:1, hs[h]]] * 2, axis=1)
            fkv = jnp.concatenate([dfb[1:2, hs[h]]] * 2, axis=1)
            CN_ref[h] = CN[h] * dec + kvn_ref[c * N_HEADS + h] * fkv

    for c in range(n_chunks):
        rows = slice(c * CHUNK, (c + 1) * CHUNK)
        hn = [_head_norm_rows(hh_ref[rows, hs[h]], ng_ref[:, hs[h]], LN_EPS) for h in heads]
        for h in heads:
            out_ref[0, rows, hs[h]] = (hn[h] * _sigmoid(o_ref[0, rows, hs[h]])).astype(out_ref.dtype)


def _mlstm_call(proj, conv_w, gate_b, norm_g, *, gate_col, tb=256):
    B, T, _ = proj.shape
    nt = T // tb
    gb = jnp.zeros((1, LANES), F32).at[0, :2 * N_HEADS].set(gate_b)
    ng = norm_g.reshape(1, D_GROUP)
    colblk = lambda j: pl.BlockSpec((1, tb, D_GROUP), lambda b, t: (b, t, j))
    halo = lambda j: pl.BlockSpec((1, 8, D_GROUP), lambda b, t: (b, jnp.maximum(t * (tb // 8) - 1, 0), j))
    return pl.pallas_call(
        functools.partial(_mlstm_kernel, tb=tb),
        grid=(B, nt),
        in_specs=[colblk(0), colblk(1), colblk(2), colblk(3),
                  pl.BlockSpec((1, tb, LANES), lambda b, t: (b, t, gate_col // LANES)),
                  halo(0), halo(1),
                  pl.BlockSpec((4, 2 * D_GROUP), lambda b, t: (0, 0)),
                  pl.BlockSpec((1, LANES), lambda b, t: (0, 0)),
                  pl.BlockSpec((1, D_GROUP), lambda b, t: (0, 0))],
        out_specs=pl.BlockSpec((1, tb, D_GROUP), lambda b, t: (b, t, 0)),
        out_shape=jax.ShapeDtypeStruct((B, T, D_GROUP), BF16),
        scratch_shapes=[pltpu.VMEM((tb, D_GROUP), F32)] * 5
                       + [pltpu.VMEM((tb, LANES), F32)] * 2
                       + [pltpu.VMEM((tb // CHUNK * N_HEADS, HEAD_DIM, 2 * HEAD_DIM), F32),
                          pltpu.VMEM((N_HEADS, HEAD_DIM, 2 * HEAD_DIM), F32),
                          pltpu.VMEM((8, LANES), F32)],
        compiler_params=pltpu.CompilerParams(dimension_semantics=("arbitrary", "arbitrary"),
                                             vmem_limit_bytes=VMEM_LIMIT),
        name="mlstm",
    )(proj, proj, proj, proj, proj, proj, proj, conv_w, gb, ng)


def _rope_kernel(pos_ref, inv_ref, cos_ref, sin_ref):
    ang = pos_ref[0].astype(F32) * inv_ref[...]
    lane = lax.broadcasted_iota(jnp.int32, ang.shape, 1)
    cos_ref[0] = jnp.cos(ang)
    sin_ref[0] = jnp.where(lane < HEAD_DIM // 2, -jnp.sin(ang), jnp.sin(ang))


def _rope_call(positions, *, tb=512):
    B, T = positions.shape
    half = HEAD_DIM // 2
    inv_freq = ROPE_BASE ** (-jnp.arange(half, dtype=F32) / half)
    inv2 = jnp.concatenate([inv_freq, inv_freq]).reshape(1, HEAD_DIM)
    spec = pl.BlockSpec((1, tb, HEAD_DIM), lambda b, t: (b, t, 0))
    return pl.pallas_call(
        _rope_kernel,
        grid=(B, T // tb),
        in_specs=[pl.BlockSpec((1, tb, 1), lambda b, t: (b, t, 0)),
                  pl.BlockSpec((1, HEAD_DIM), lambda b, t: (0, 0))],
        out_specs=[spec, spec],
        out_shape=[jax.ShapeDtypeStruct((B, T, HEAD_DIM), F32)] * 2,
        compiler_params=pltpu.CompilerParams(dimension_semantics=("arbitrary", "arbitrary")),
        name="rope_table",
    )(positions.reshape(B, T, 1), inv2)


def _ret_kernel(q_ref, k_ref, v_ref, g_ref, cos_ref, sin_ref, ng_ref, out_ref, R_ref, *, tb):
    t = pl.program_id(1)

    @pl.when(t == 0)
    def _():
        R_ref[...] = jnp.zeros_like(R_ref)

    causal = _tri(CHUNK)
    ri = lax.broadcasted_iota(jnp.int32, (CHUNK, CHUNK), 0)
    ci = lax.broadcasted_iota(jnp.int32, (CHUNK, CHUNK), 1)
    rel = (ri - ci).astype(F32)
    tcol = lax.broadcasted_iota(jnp.int32, (CHUNK, 1), 0).astype(F32)
    scale = HEAD_DIM ** -0.5

    def chunk_body(c, carry):
        r0 = pl.multiple_of(c * CHUNK, CHUNK)
        rows = pl.ds(r0, CHUNK)
        cos2 = cos_ref[0, rows, :]
        sin2 = sin_ref[0, rows, :]
        for h in range(N_HEADS):
            hs = slice(h * HEAD_DIM, (h + 1) * HEAD_DIM)
            lg = math.log1p(-2.0 ** (-RET_GAMMA_BASE - h))
            q = q_ref[0, rows, hs]
            k = k_ref[0, rows, hs]
            v = v_ref[0, rows, hs]
            qr = q * cos2 + pltpu.roll(q, HEAD_DIM // 2, 1) * sin2
            kr = (k * cos2 + pltpu.roll(k, HEAD_DIM // 2, 1) * sin2) * scale
            dmat = jnp.where(causal, jnp.exp(rel * lg), 0.0)
            xi = jnp.exp((tcol + 1.0) * lg)
            zeta = jnp.exp((CHUNK - 1.0 - tcol) * lg)
            Rst = R_ref[h]
            intra = _dot(_dot_nt(qr, kr) * dmat, v)
            inter = _dot(qr, Rst) * xi
            R_ref[h] = Rst * math.exp(CHUNK * lg) + _dot((kr * zeta).T, v)
            on = _head_norm_rows(intra + inter, ng_ref[:, hs], LN_EPS)
            out_ref[0, rows, hs] = (on * _silu(g_ref[0, rows, hs])).astype(out_ref.dtype)
        return carry

    lax.fori_loop(0, tb // CHUNK, chunk_body, 0)


def _ret_call(proj, cos2, sin2, norm_g, *, tb=256):
    B, T, _ = proj.shape
    colblk = lambda j: pl.BlockSpec((1, tb, D_GROUP), lambda b, t: (b, t, j))
    tab = pl.BlockSpec((1, tb, HEAD_DIM), lambda b, t: (b, t, 0))
    return pl.pallas_call(
        functools.partial(_ret_kernel, tb=tb),
        grid=(B, T // tb),
        in_specs=[colblk(0), colblk(1), colblk(2), colblk(3), tab, tab,
                  pl.BlockSpec((1, D_GROUP), lambda b, t: (0, 0))],
        out_specs=pl.BlockSpec((1, tb, D_GROUP), lambda b, t: (b, t, 0)),
        out_shape=jax.ShapeDtypeStruct((B, T, D_GROUP), BF16),
        scratch_shapes=[pltpu.VMEM((N_HEADS, HEAD_DIM, HEAD_DIM), F32)],
        compiler_params=pltpu.CompilerParams(dimension_semantics=("arbitrary", "arbitrary"),
                                             vmem_limit_bytes=VMEM_LIMIT),
        name="retention",
    )(proj, proj, proj, proj, cos2, sin2, norm_g.reshape(1, D_GROUP))


def _inv_unit_lower(nms):
    n = nms[0].shape[0]
    eye = (lax.broadcasted_iota(jnp.int32, (n, n), 0) == lax.broadcasted_iota(jnp.int32, (n, n), 1)).astype(F32)
    ps = [eye + nm for nm in nms]
    xs = [_dot(nm, nm) for nm in nms]
    for _ in range(int(math.log2(n)) - 2):
        px = [_dot(jnp.concatenate([p, x], axis=0), x) for p, x in zip(ps, xs)]
        ps = [p + y[:n] for p, y in zip(ps, px)]
        xs = [y[n:] for y in px]
    ps = [p + _dot(p, x) for p, x in zip(ps, xs)]
    resid = [eye - p + _dot(nm, p) for p, nm in zip(ps, nms)]
    return [p + _dot(p, r) for p, r in zip(ps, resid)]


def _solve_unit_lower(nms, rhss):
    n = nms[0].shape[0]
    eye = (lax.broadcasted_iota(jnp.int32, (n, n), 0) == lax.broadcasted_iota(jnp.int32, (n, n), 1)).astype(F32)
    ps = [eye + nm for nm in nms]
    xs = [_dot(nm, nm) for nm in nms]
    for _ in range(int(math.log2(n)) - 2):
        px = [_dot(jnp.concatenate([p, x], axis=0), x) for p, x in zip(ps, xs)]
        ps = [p + y[:n] for p, y in zip(ps, px)]
        xs = [y[n:] for y in px]
    ps = [p + _dot(p, x) for p, x in zip(ps, xs)]
    x0 = [_dot(p, r) for p, r in zip(ps, rhss)]
    resid = [r - a + _dot(nm, a) for r, a, nm in zip(rhss, x0, nms)]
    return [a + _dot(p, r) for a, p, r in zip(x0, ps, resid)]


def _l2norm_rows(z):
    return z * lax.rsqrt(jnp.sum(z * z, axis=-1, keepdims=True) + 1e-6)


GDN_CHUNK_GROUP = 2


def _gdn_kernel(q_ref, k_ref, v_ref, z_ref, g_ref, qh_ref, kh_ref, vh_ref, cw_ref, an_ref, dt_ref, ng_ref,
                out_ref, qc_ref, kc_ref, vc_ref, u_ref, w_ref, qe_ref, o_ref, att_ref, kdT_ref, gl_ref, S_ref,
                *, tb):
    t = pl.program_id(1)
    first = t == 0

    @pl.when(first)
    def _():
        S_ref[...] = jnp.zeros_like(S_ref)

    srcs = ((q_ref, qh_ref, qc_ref), (k_ref, kh_ref, kc_ref), (v_ref, vh_ref, vc_ref))
    for c in range(tb // CHUNK):
        for cb in range(D_GROUP // 256):
            cs = cb * 256
            for i, (src, halo, dst) in enumerate(srcs):
                w = cw_ref[:, i * D_GROUP + cs:i * D_GROUP + cs + 256]
                dst[c * CHUNK:(c + 1) * CHUNK, cs:cs + 256] = _conv_silu_rows(
                    src, halo, w, first, c * CHUNK, CHUNK, cs, 256)

    causal = _tri(CHUNK)
    strict = _tri(CHUNK, strict=True)
    a_neg = an_ref[...]
    dtb = dt_ref[...]
    scale = HEAD_DIM ** -0.5
    n_chunks = tb // CHUNK
    heads = range(N_HEADS)
    hs = [slice(h * HEAD_DIM, (h + 1) * HEAD_DIM) for h in heads]

    for c0 in range(0, n_chunks, GDN_CHUNK_GROUP):
        chunks = range(c0, c0 + GDN_CHUNK_GROUP)
        rows = {c: slice(c * CHUNK, (c + 1) * CHUNK) for c in chunks}
        gz = {c: g_ref[0, rows[c], :] for c in chunks}
        beta = {c: _sigmoid(gz[c]) for c in chunks}
        gc = {c: _cumsum_rows(a_neg * _softplus(gz[c] + dtb)) for c in chunks}
        gcT = {c: gc[c].T for c in chunks}
        for c in chunks:
            gl_ref[c:c + 1, :] = gc[c][CHUNK - 1:CHUNK, :]
        items = [(c, h) for c in chunks for h in heads]
        n_items = range(len(items))
        gc_col = [gc[c][:, h:h + 1] for c, h in items]
        b_col = [beta[c][:, 8 + h:9 + h] for c, h in items]
        gamma = [jnp.where(causal, jnp.exp(gc_col[i] - gcT[c][h:h + 1, :]), 0.0) for i, (c, h) in enumerate(items)]
        q = [_l2norm_rows(qc_ref[rows[c], hs[h]]) * scale for c, h in items]
        k = [_l2norm_rows(kc_ref[rows[c], hs[h]]) for c, h in items]
        kb = [k[i] * b_col[i] for i in n_items]
        eg = [jnp.exp(gc_col[i]) for i in n_items]
        kq = [_dot_nt(jnp.concatenate([kb[i], q[i]], axis=0), k[i]) for i in n_items]
        inv = _inv_unit_lower([-jnp.where(strict, kq[i][:CHUNK] * gamma[i], 0.0) for i in n_items])
        uw = [_dot(inv[i], jnp.concatenate([vc_ref[rows[c], hs[h]] * b_col[i], kb[i] * eg[i]], axis=1))
              for i, (c, h) in enumerate(items)]
        for i, (c, h) in enumerate(items):
            u_ref[rows[c], hs[h]] = uw[i][:, :HEAD_DIM]
            w_ref[rows[c], hs[h]] = uw[i][:, HEAD_DIM:]
            qe_ref[rows[c], hs[h]] = q[i] * eg[i]
            att_ref[h, rows[c], :] = kq[i][CHUNK:] * gamma[i]
            g_last = gc[c][CHUNK - 1:CHUNK, h:h + 1]
            kdT_ref[c * N_HEADS + h] = (k[i] * jnp.exp(g_last - gc_col[i])).T

    for c in range(n_chunks):
        rows = slice(c * CHUNK, (c + 1) * CHUNK)
        S = [S_ref[h] for h in heads]
        ws = [_dot(jnp.concatenate([w_ref[rows, hs[h]], qe_ref[rows, hs[h]]], axis=0), S[h]) for h in heads]
        v_new = [u_ref[rows, hs[h]] - ws[h][:CHUNK] for h in heads]
        av = [_dot(att_ref[h, rows, :], v_new[h]) for h in heads]
        kv = [_dot(kdT_ref[c * N_HEADS + h], v_new[h]) for h in heads]
        for h in heads:
            S_ref[h] = S[h] * jnp.exp(gl_ref[c:c + 1, h:h + 1]) + kv[h]
            o_ref[rows, hs[h]] = ws[h][CHUNK:] + av[h]

    for c in range(n_chunks):
        rows = slice(c * CHUNK, (c + 1) * CHUNK)
        for h in heads:
            on = _head_norm_rows(o_ref[rows, hs[h]], ng_ref[:, hs[h]], 1e-6, center=False)
            out_ref[0, rows, hs[h]] = (on * _silu(z_ref[0, rows, hs[h]])).astype(out_ref.dtype)


def _gdn_call(proj, conv_w, a_log, dt_bias, norm_g, *, col0, tb=256):
    B, T, _ = proj.shape
    j0 = col0 // D_GROUP
    an = jnp.zeros((1, LANES), F32).at[0, :N_HEADS].set(-jnp.exp(a_log.astype(F32)))
    dtb = jnp.zeros((1, LANES), F32).at[0, :N_HEADS].set(dt_bias)
    colblk = lambda j: pl.BlockSpec((1, tb, D_GROUP), lambda b, t: (b, t, j0 + j))
    halo = lambda j: pl.BlockSpec((1, 8, D_GROUP), lambda b, t: (b, jnp.maximum(t * (tb // 8) - 1, 0), j0 + j))
    return pl.pallas_call(
        functools.partial(_gdn_kernel, tb=tb),
        grid=(B, T // tb),
        in_specs=[colblk(0), colblk(1), colblk(2), colblk(3),
                  pl.BlockSpec((1, tb, LANES), lambda b, t: (b, t, (col0 + 4 * D_GROUP) // LANES)),
                  halo(0), halo(1), halo(2),
                  pl.BlockSpec((4, 3 * D_GROUP), lambda b, t: (0, 0)),
                  pl.BlockSpec((1, LANES), lambda b, t: (0, 0)),
                  pl.BlockSpec((1, LANES), lambda b, t: (0, 0)),
                  pl.BlockSpec((1, D_GROUP), lambda b, t: (0, 0))],
        out_specs=pl.BlockSpec((1, tb, D_GROUP), lambda b, t: (b, t, 0)),
        out_shape=jax.ShapeDtypeStruct((B, T, D_GROUP), BF16),
        scratch_shapes=[pltpu.VMEM((tb, D_GROUP), F32)] * 7
                       + [pltpu.VMEM((N_HEADS, tb, CHUNK), F32),
                          pltpu.VMEM((tb // CHUNK * N_HEADS, HEAD_DIM, CHUNK), F32),
                          pltpu.VMEM((max(tb // CHUNK, 8), LANES), F32),
                          pltpu.VMEM((N_HEADS, HEAD_DIM, HEAD_DIM), F32)],
        compiler_params=pltpu.CompilerParams(dimension_semantics=("arbitrary", "arbitrary"),
                                             vmem_limit_bytes=VMEM_LIMIT),
        name="gdn",
    )(proj, proj, proj, proj, proj, proj, proj, proj, conv_w, an, dtb, norm_g.reshape(1, D_GROUP))


N_PAIRS = N_RWKV // 2
RWKV_LOW = 384


def _shift1_rows(src_ref, halo_ref, first, r0, nrows, c0, ncols):
    if r0 == 0:
        hl = jnp.where(first, 0.0, halo_ref[0, 7:8, c0:c0 + ncols])
        return jnp.concatenate([hl, src_ref[0, 0:nrows - 1, c0:c0 + ncols]], axis=0)
    return src_ref[0, r0 - 1:r0 - 1 + nrows, c0:c0 + ncols]


def _rwkv_kernel(r_ref, k_ref, v_ref, l0_ref, l1_ref, l2_ref,
                 rh_ref, kh_ref, vh_ref, l0h_ref, l1h_ref, l2h_ref,
                 mu_ref, mul_ref, w0_ref, w2_ref, a0_ref, a2_ref, g2_ref, kk_ref, ka_ref, rk_ref,
                 lng_ref, lnb_ref, out_ref,
                 gs_ref, bo_ref, y_ref, atrt_ref, avk_ref, yk_ref, inv_ref, arb_ref, btT_ref, kvT_ref, wlT_ref,
                 H_ref, *, tb):
    t = pl.program_id(1)
    first = t == 0

    @pl.when(first)
    def _():
        H_ref[...] = jnp.zeros_like(H_ref)

    ri = lax.broadcasted_iota(jnp.int32, (LANES, LANES), 0)
    ci = lax.broadcasted_iota(jnp.int32, (LANES, LANES), 1)
    same_head = ((ri // RWKV_HEAD) == (ci // RWKV_HEAD)).astype(F32)
    causal = _tri(CHUNK)
    strict = _tri(CHUNK, strict=True)
    lane1 = lax.broadcasted_iota(jnp.int32, (1, LANES), 1)
    m0 = (lane1 < RWKV_HEAD).astype(F32)
    m1 = 1.0 - m0
    t2 = lax.broadcasted_iota(jnp.int32, (CHUNK, LANES), 0)
    l2 = lax.broadcasted_iota(jnp.int32, (CHUNK, LANES), 1)
    lo = l2 < RWKV_HEAD
    s2 = l2 & (RWKV_HEAD - 1)
    causal_hi = (t2 >= s2) & (l2 >= RWKV_HEAD)
    strict_hi = (t2 > s2) & (l2 >= RWKV_HEAD)
    n_chunks = tb // CHUNK
    pairs = range(N_PAIRS)
    halves = [(p, hh) for p in pairs for hh in range(2)]
    ps = [slice(p * LANES, (p + 1) * LANES) for p in pairs]

    def lerp(src, halo, mu, r0, c0, ncols):
        cur = src[0, r0:r0 + CHUNK, c0:c0 + ncols]
        return cur + (_shift1_rows(src, halo, first, r0, CHUNK, c0, ncols) - cur) * mu

    for c in range(n_chunks):
        r0 = c * CHUNK
        rows = slice(r0, r0 + CHUNK)
        wl = lerp(l0_ref, l0h_ref, mul_ref[:, 0:LANES], r0, 0, LANES)
        g1 = lerp(l1_ref, l1h_ref, mul_ref[:, LANES:2 * LANES], r0, 0, LANES)
        g2 = lerp(l2_ref, l2h_ref, mul_ref[:, 2 * LANES:3 * LANES], r0, 0, LANES)
        wl_t = jnp.where(lo, jnp.tanh(wl), 0.0)
        al = jnp.where(lo, 0.0, wl)
        sg1 = _sigmoid(g1)
        sg2 = jnp.where(l2 < 32, _sigmoid(g2), 0.0)
        w = [-_softplus(-(w0_ref[:, ps[p]] + _dot(wl_t, w2_ref[:, ps[p]]))) - 0.5 for p in pairs]
        a = [_sigmoid(a0_ref[:, ps[p]] + _dot(al, a2_ref[:, ps[p]])) for p in pairs]
        g = [_dot(sg1, g2_ref[0:LANES, ps[p]]) + _dot(sg2, g2_ref[LANES:2 * LANES, ps[p]]) for p in pairs]
        r = [lerp(r_ref, rh_ref, mu_ref[:, ps[p]], r0, p * LANES, LANES) for p in pairs]
        k = [lerp(k_ref, kh_ref, mu_ref[:, D_GROUP + p * LANES:D_GROUP + (p + 1) * LANES], r0, p * LANES, LANES)
             for p in pairs]
        v = [lerp(v_ref, vh_ref, mu_ref[:, 2 * D_GROUP + p * LANES:2 * D_GROUP + (p + 1) * LANES], r0, p * LANES, LANES)
             for p in pairs]
        kk = [k[p] * kk_ref[:, ps[p]] for p in pairs]
        nrm = [jnp.sqrt(_group_sum(kk[p] * kk[p], same_head)) for p in pairs]
        kk = [kk[p] / jnp.maximum(nrm[p], 1e-12) for p in pairs]
        k2 = [k[p] * (1.0 + (a[p] - 1.0) * ka_ref[:, ps[p]]) for p in pairs]
        rk = [_group_sum(r[p] * k2[p] * rk_ref[:, ps[p]], same_head) for p in pairs]
        for p in pairs:
            gs_ref[p, rows, :] = g[p]
            bo_ref[p, rows, :] = rk[p] * v[p]
        lw = [-jnp.exp(w[p]) for p in pairs]
        cs = [_cumsum_rows(lw[p]) for p in pairs]
        w_inv = [jnp.exp(-cs[p]) for p in pairs]
        w_end = [jnp.exp(cs[p][CHUNK - 1:CHUNK, :]) for p in pairs]
        rt = [r[p] * jnp.exp(cs[p]) for p in pairs]
        at = [-kk[p] * jnp.exp(cs[p] - lw[p]) for p in pairs]
        bt = [kk[p] * a[p] * w_inv[p] for p in pairs]
        kt = [k2[p] * w_inv[p] for p in pairs]
        pm = [_dot_nt(jnp.concatenate([at[p] * m0, at[p] * m1, rt[p] * m0, rt[p] * m1], axis=0),
                      jnp.concatenate([bt[p], kt[p]], axis=0)) for p in pairs]
        vv = [jnp.concatenate([v[p], v[p]], axis=0).astype(BF16) for p in pairs]
        pa = [pm[p][hh * CHUNK:(hh + 1) * CHUNK] for p, hh in halves]
        pr = [pm[p][(2 + hh) * CHUNK:(3 + hh) * CHUNK] for p, hh in halves]
        inv = _inv_unit_lower([jnp.where(strict, x[:, :CHUNK], 0.0) for x in pa])
        avk = [_dot(jnp.where(strict_hi, pa[2 * p + hh], 0.0), vv[p]) for p, hh in halves]
        yk = [_dot(jnp.where(causal_hi, pr[2 * p + hh], 0.0), vv[p]) for p, hh in halves]
        kv = [_dot((kt[p] * w_end[p]).T, v[p]) * same_head for p in pairs]
        for p in pairs:
            i = c * N_PAIRS + p
            atrt_ref[i] = jnp.concatenate([at[p], rt[p]], axis=0).astype(BF16)
            avk_ref[i] = jnp.where(lo, avk[2 * p], avk[2 * p + 1])
            yk_ref[i] = jnp.where(lo, yk[2 * p], yk[2 * p + 1])
            btT_ref[i] = (bt[p] * w_end[p]).T.astype(BF16)
            kvT_ref[i] = kv[p]
            wlT_ref[i] = jnp.broadcast_to(w_end[p], (LANES, LANES)).T
            for hh in range(2):
                inv_ref[2 * i + hh] = inv[2 * p + hh].astype(BF16)
                arb_ref[2 * i + hh] = jnp.where(causal, pr[2 * p + hh][:, :CHUNK], 0.0).astype(BF16)

    for c in range(n_chunks):
        rows = slice(c * CHUNK, (c + 1) * CHUNK)
        it = [c * N_PAIRS + p for p in pairs]
        H = [H_ref[p] for p in pairs]
        xy0 = [_dot(atrt_ref[it[p]], H[p]) for p in pairs]
        x = [xy0[p][:CHUNK] + avk_ref[it[p]] for p in pairs]
        us = [_dot(inv_ref[2 * it[p] + hh], x[p]) for p, hh in halves]
        u = [jnp.where(lo, us[2 * p], us[2 * p + 1]) for p in pairs]
        yb = [_dot(arb_ref[2 * it[p] + hh], u[p]) for p, hh in halves]
        bu = [_dot(btT_ref[it[p]], u[p]) for p in pairs]
        for p in pairs:
            y_ref[p, rows, :] = xy0[p][CHUNK:] + yk_ref[it[p]] + jnp.where(lo, yb[2 * p], yb[2 * p + 1])
            H_ref[p] = H[p] * wlT_ref[it[p]] + bu[p] * same_head + kvT_ref[it[p]]

    inv_n = 1.0 / RWKV_HEAD
    for c in range(n_chunks):
        rows = slice(c * CHUNK, (c + 1) * CHUNK)
        y = [y_ref[p, rows, :] for p in pairs]
        yc = [y[p] - _group_sum(y[p], same_head) * inv_n for p in pairs]
        var = [_group_sum(yc[p] * yc[p], same_head) * inv_n for p in pairs]
        for p in pairs:
            yn = yc[p] * lax.rsqrt(var[p] + RWKV_LN_EPS) * lng_ref[:, ps[p]] + lnb_ref[:, ps[p]]
            out_ref[0, rows, ps[p]] = ((yn + bo_ref[p, rows, :]) * gs_ref[p, rows, :]).astype(out_ref.dtype)


def _rwkv_call(proj, mu, w0, w2, a0, a2, g2, k_k, k_a, r_k, ln_g, ln_b, *, col0, tb=256):
    B, T, _ = proj.shape
    j0 = col0 // D_GROUP
    l0 = (col0 + 3 * D_GROUP) // LANES
    row = lambda a: a.reshape(1, -1).astype(F32)
    mul = jnp.zeros((1, RWKV_LOW), F32).at[0, :288].set(mu[3 * D_GROUP:])
    w2p = jnp.zeros((LANES, D_GROUP), F32).at[:64].set(w2)
    a2p = jnp.zeros((LANES, D_GROUP), F32).at[64:].set(a2)
    g2p = jnp.zeros((2 * LANES, D_GROUP), F32).at[:160].set(g2)
    colblk = lambda j: pl.BlockSpec((1, tb, D_GROUP), lambda b, t: (b, t, j0 + j))
    lowblk = lambda j: pl.BlockSpec((1, tb, LANES), lambda b, t: (b, t, l0 + j))
    hrow = lambda t: jnp.maximum(t * (tb // 8) - 1, 0)
    halo = lambda j: pl.BlockSpec((1, 8, D_GROUP), lambda b, t: (b, hrow(t), j0 + j))
    lowhalo = lambda j: pl.BlockSpec((1, 8, LANES), lambda b, t: (b, hrow(t), l0 + j))
    full = lambda a: pl.BlockSpec(a.shape, lambda b, t: (0,) * a.ndim)
    params = [row(mu[:3 * D_GROUP]), mul, row(w0), w2p, row(a0), a2p, g2p, row(k_k), row(k_a), row(r_k),
              row(ln_g), row(ln_b)]
    big = pltpu.VMEM((N_PAIRS, tb, LANES), F32)
    n_items = tb // CHUNK * N_PAIRS
    return pl.pallas_call(
        functools.partial(_rwkv_kernel, tb=tb),
        grid=(B, T // tb),
        in_specs=[colblk(0), colblk(1), colblk(2), lowblk(0), lowblk(1), lowblk(2),
                  halo(0), halo(1), halo(2), lowhalo(0), lowhalo(1), lowhalo(2)] + [full(a) for a in params],
        out_specs=pl.BlockSpec((1, tb, D_GROUP), lambda b, t: (b, t, 0)),
        out_shape=jax.ShapeDtypeStruct((B, T, D_GROUP), BF16),
        scratch_shapes=[big] * 3 + [
            pltpu.VMEM((n_items, 2 * CHUNK, LANES), BF16),
            pltpu.VMEM((n_items, CHUNK, LANES), F32),
            pltpu.VMEM((n_items, CHUNK, LANES), F32),
            pltpu.VMEM((2 * n_items, CHUNK, CHUNK), BF16),
            pltpu.VMEM((2 * n_items, CHUNK, CHUNK), BF16),
            pltpu.VMEM((n_items, LANES, CHUNK), BF16),
            pltpu.VMEM((n_items, LANES, LANES), F32),
            pltpu.VMEM((n_items, LANES, LANES), F32),
            pltpu.VMEM((N_PAIRS, LANES, LANES), F32)],
        compiler_params=pltpu.CompilerParams(dimension_semantics=("arbitrary", "arbitrary"),
                                             vmem_limit_bytes=VMEM_LIMIT),
        name="rwkv7",
    )(*([proj] * 12), *params)


def _ada_kernel(c_ref, w_ref, b_ref, out_ref):
    sc = _silu(c_ref[...]).astype(BF16)
    out_ref[0] = jnp.dot(sc, w_ref[0].astype(BF16), preferred_element_type=F32) + b_ref[0]


def _ada_call(c, ada_w, ada_b, *, tn=1536):
    B = c.shape[0]
    n_mod = ada_w.shape[0] * ada_w.shape[1]
    w = ada_w.reshape(n_mod, D_MODEL, 3 * D_MODEL)
    b = ada_b.reshape(n_mod, 1, 3 * D_MODEL)
    return pl.pallas_call(
        _ada_kernel,
        grid=(n_mod, 3 * D_MODEL // tn),
        in_specs=[pl.BlockSpec((B, D_MODEL), lambda i, j: (0, 0)),
                  pl.BlockSpec((1, D_MODEL, tn), lambda i, j: (i, 0, j)),
                  pl.BlockSpec((1, 1, tn), lambda i, j: (i, 0, j))],
        out_specs=pl.BlockSpec((1, B, tn), lambda i, j: (i, 0, j)),
        out_shape=jax.ShapeDtypeStruct((n_mod, B, 3 * D_MODEL), F32),
        compiler_params=pltpu.CompilerParams(dimension_semantics=("arbitrary", "arbitrary"),
                                             vmem_limit_bytes=VMEM_LIMIT),
        name="adaln",
    )(c, w, b)


def _mod_spec(i, part, nb):
    return pl.BlockSpec((1, 1, D_MODEL), lambda b, t: (i * nb + b, 0, part))


def _modulate_kernel(x_ref, shift_ref, scale_ref, h_ref):
    h_ref[0] = (x_ref[0] * (1.0 + scale_ref[0]) + shift_ref[0]).astype(h_ref.dtype)


def _modulate_call(x, mods3, i, *, tb=512):
    B, T, _ = x.shape
    blk = pl.BlockSpec((1, tb, D_MODEL), lambda b, t: (b, t, 0))
    return pl.pallas_call(
        _modulate_kernel,
        grid=(B, T // tb),
        in_specs=[blk, _mod_spec(i, 0, B), _mod_spec(i, 1, B)],
        out_specs=blk,
        out_shape=jax.ShapeDtypeStruct(x.shape, BF16),
        compiler_params=pltpu.CompilerParams(dimension_semantics=("arbitrary", "arbitrary"),
                                             vmem_limit_bytes=VMEM_LIMIT),
        name="modulate",
    )(x, mods3, mods3)


LN_ROWS = 16


def _proj_ln_kernel(*refs, n_lhs, nk, tm, with_next):
    lhs = refs[:n_lhs]
    ws = refs[n_lhs:2 * n_lhs]
    x_ref, gate_ref, g_ref, b_ref = refs[2 * n_lhs:2 * n_lhs + 4]
    if with_next:
        shift_ref, scale_ref, xo_ref, h_ref, acc_ref = refs[2 * n_lhs + 4:]
    else:
        xo_ref, acc_ref = refs[2 * n_lhs + 4:]
    k = pl.program_id(2)
    def partial_product():
        part = jnp.dot(lhs[0][0], ws[0][...], preferred_element_type=F32)
        for j in range(1, n_lhs):
            part = part + jnp.dot(lhs[j][0], ws[j][...], preferred_element_type=F32)
        return part

    @pl.when(k == 0)
    def _():
        acc_ref[...] = partial_product()

    @pl.when(k > 0)
    def _():
        acc_ref[...] += partial_product()

    @pl.when(k == nk - 1)
    def _():
        gate1 = 1.0 + gate_ref[0]
        g, b = g_ref[...], b_ref[...]
        if with_next:
            scale1, shift = 1.0 + scale_ref[0], shift_ref[0]

        def rows_body(r, carry):
            rows = pl.ds(pl.multiple_of(r * LN_ROWS, LN_ROWS), LN_ROWS)
            z = ALPHA * x_ref[0, rows, :] + gate1 * acc_ref[rows, :]
            zc = z - jnp.mean(z, axis=-1, keepdims=True)
            var = jnp.mean(zc * zc, axis=-1, keepdims=True)
            xn = zc * lax.rsqrt(var + LN_EPS) * g + b
            xo_ref[0, rows, :] = xn
            if with_next:
                h_ref[0, rows, :] = (xn * scale1 + shift).astype(h_ref.dtype)
            return carry

        lax.fori_loop(0, tm // LN_ROWS, rows_body, 0, unroll=8)


def _proj_ln_call(lhs, ws, x, mods3, i, g, b, *, with_next, tm, tk):
    B, T, D = x.shape
    n_lhs = len(lhs)
    nk = lhs[0].shape[2] // tk
    blk = pl.BlockSpec((1, tm, D), lambda b, t, k: (b, t, 0))
    row = pl.BlockSpec((1, D), lambda b, t, k: (0, 0))
    mod = lambda ii, part: pl.BlockSpec((1, 1, D), lambda b, t, k: (ii * B + b, 0, part))
    in_specs = ([pl.BlockSpec((1, tm, tk), lambda b, t, k: (b, t, k))] * n_lhs
                + [pl.BlockSpec((tk, D), lambda b, t, k: (k, 0))] * n_lhs
                + [blk, mod(i, 2), row, row])
    args = list(lhs) + list(ws) + [x, mods3, g.reshape(1, D), b.reshape(1, D)]
    out_specs = [blk]
    out_shape = [jax.ShapeDtypeStruct(x.shape, F32)]
    if with_next:
        in_specs += [mod(i + 1, 0), mod(i + 1, 1)]
        args += [mods3, mods3]
        out_specs.append(blk)
        out_shape.append(jax.ShapeDtypeStruct(x.shape, BF16))
    return pl.pallas_call(
        functools.partial(_proj_ln_kernel, n_lhs=n_lhs, nk=nk, tm=tm, with_next=with_next),
        grid=(B, T // tm, nk),
        in_specs=in_specs, out_specs=out_specs, out_shape=out_shape,
        scratch_shapes=[pltpu.VMEM((tm, D), F32)],
        compiler_params=pltpu.CompilerParams(dimension_semantics=("arbitrary", "arbitrary", "arbitrary"),
                                             vmem_limit_bytes=VMEM_LIMIT),
        name="proj_residual_ln",
    )(*args)


def _matmul_kernel(a_ref, b_ref, o_ref, *scratch, nk):
    if nk == 1:
        o_ref[...] = jnp.dot(a_ref[...], b_ref[...], preferred_element_type=F32).astype(o_ref.dtype)
        return
    (acc_ref,) = scratch
    k = pl.program_id(2)

    @pl.when(k == 0)
    def _():
        acc_ref[...] = jnp.zeros_like(acc_ref)

    acc_ref[...] += jnp.dot(a_ref[...], b_ref[...], preferred_element_type=F32)

    @pl.when(k == nk - 1)
    def _():
        o_ref[...] = acc_ref[...].astype(o_ref.dtype)


def _matmul_call(a, b, *, tm, tn, tk, out_dtype=F32):
    M, K = a.shape
    _, N = b.shape
    nk = K // tk
    return pl.pallas_call(
        functools.partial(_matmul_kernel, nk=nk),
        grid=(N // tn, M // tm, nk),
        in_specs=[pl.BlockSpec((tm, tk), lambda j, i, k: (i, k)),
                  pl.BlockSpec((tk, tn), lambda j, i, k: (k, j))],
        out_specs=pl.BlockSpec((tm, tn), lambda j, i, k: (i, j)),
        out_shape=jax.ShapeDtypeStruct((M, N), out_dtype),
        scratch_shapes=[] if nk == 1 else [pltpu.VMEM((tm, tn), F32)],
        compiler_params=pltpu.CompilerParams(dimension_semantics=("arbitrary", "arbitrary", "arbitrary"),
                                             vmem_limit_bytes=VMEM_LIMIT),
        name="matmul",
    )(a, b)


def _swiglu_up_kernel(h_ref, wg_ref, wu_ref, o_ref, wgb_ref, wub_ref):
    @pl.when(pl.program_id(1) == 0)
    def _():
        wgb_ref[...] = wg_ref[...].astype(BF16)
        wub_ref[...] = wu_ref[...].astype(BF16)

    h = h_ref[...]
    g = jnp.dot(h, wgb_ref[...], preferred_element_type=F32)
    u = jnp.dot(h, wub_ref[...], preferred_element_type=F32)
    o_ref[...] = (_silu(g) * u).astype(o_ref.dtype)


def _swiglu_up_call(h, wg, wu, layer, *, tm=512, tn=512):
    M, K = h.shape
    _, _, N = wg.shape
    wspec = pl.BlockSpec((None, K, tn), lambda j, i: (layer, 0, j))
    return pl.pallas_call(
        _swiglu_up_kernel,
        grid=(N // tn, M // tm),
        in_specs=[pl.BlockSpec((tm, K), lambda j, i: (i, 0)), wspec, wspec],
        out_specs=pl.BlockSpec((tm, tn), lambda j, i: (i, j)),
        out_shape=jax.ShapeDtypeStruct((M, N), BF16),
        scratch_shapes=[pltpu.VMEM((K, tn), BF16), pltpu.VMEM((K, tn), BF16)],
        compiler_params=pltpu.CompilerParams(dimension_semantics=("arbitrary", "arbitrary"),
                                             vmem_limit_bytes=VMEM_LIMIT),
        name="swiglu_up",
    )(h, wg, wu)


AB_COLS = 7680
CD_COLS = 8320
RWKV_COL0 = 4 * D_GROUP
MLSTM_GATE_COL = 7 * D_GROUP + RWKV_LOW
GDN_COL0 = 4 * D_GROUP


def _pad_cols(w, n):
    return jnp.pad(w, ((0, 0), (0, n - w.shape[1])))


def _ab_weight(w_in):
    a_main, a_gates = w_in[:, :4 * D_GROUP], w_in[:, 4 * D_GROUP:4 * D_GROUP + 2 * N_HEADS]
    b0 = 4 * D_GROUP + 2 * N_HEADS
    b_main, b_low = w_in[:, b0:b0 + 3 * D_GROUP], w_in[:, b0 + 3 * D_GROUP:]
    return jnp.concatenate([a_main, b_main, _pad_cols(b_low, RWKV_LOW), _pad_cols(a_gates, LANES)],
                           axis=1).astype(BF16)


def _cd_weight(w_in):
    return _pad_cols(w_in, CD_COLS).astype(BF16)


def kernel(x, c, positions, ada_w, ada_b, ln_g, ln_b, ab_w_in, ab_w_out, mlstm_conv_w, mlstm_gate_b, mlstm_norm_g, rwkv_mu, rwkv_w0, rwkv_w2, rwkv_a0, rwkv_a2, rwkv_g2, rwkv_k_k, rwkv_k_a, rwkv_r_k, rwkv_ln_g, rwkv_ln_b, cd_w_in, cd_w_out, ret_norm_g, gdn_conv_w, gdn_a_log, gdn_dt_bias, gdn_norm_g, ffn_w_gate, ffn_w_up, ffn_w_down):
    B, T, D = x.shape
    M = B * T
    depth = ada_w.shape[0]
    mods = _ada_call(c, ada_w, ada_b)
    mods3 = mods.reshape(2 * depth * B, 1, 3 * D)
    cos2, sin2 = _rope_call(positions)
    h = _modulate_call(x, mods3, 0)
    for layer in range(depth):
        j = layer // 2
        i_mix, i_ffn = 2 * layer, 2 * layer + 1
        if layer % 2 == 0:
            proj = _matmul_call(h.reshape(M, D), _ab_weight(ab_w_in[j]), tm=512, tn=1280, tk=D)
            proj = proj.reshape(B, T, AB_COLS)
            ya = _mlstm_call(proj, mlstm_conv_w[j], mlstm_gate_b[j], mlstm_norm_g[j], gate_col=MLSTM_GATE_COL)
            yb = _rwkv_call(proj, rwkv_mu[j], rwkv_w0[j], rwkv_w2[j], rwkv_a0[j], rwkv_a2[j], rwkv_g2[j],
                            rwkv_k_k[j], rwkv_k_a[j], rwkv_r_k[j], rwkv_ln_g[j], rwkv_ln_b[j], col0=RWKV_COL0)
            w_out = ab_w_out[j]
        else:
            proj = _matmul_call(h.reshape(M, D), _cd_weight(cd_w_in[j]), tm=512, tn=1664, tk=D)
            proj = proj.reshape(B, T, CD_COLS)
            ya = _ret_call(proj, cos2, sin2, ret_norm_g[j])
            yb = _gdn_call(proj, gdn_conv_w[j], gdn_a_log[j], gdn_dt_bias[j], gdn_norm_g[j], col0=GDN_COL0)
            w_out = cd_w_out[j]
        w_out = w_out.astype(BF16)
        x, h = _proj_ln_call([ya, yb], [w_out[:D_GROUP], w_out[D_GROUP:]], x, mods3, i_mix,
                             ln_g[layer, 0], ln_b[layer, 0], with_next=True, tm=256, tk=D_GROUP)
        act = _swiglu_up_call(h.reshape(M, D), ffn_w_gate, ffn_w_up, layer)
        last = layer == depth - 1
        res = _proj_ln_call([act.reshape(B, T, D_FF)], [ffn_w_down[layer].astype(BF16)], x, mods3, i_ffn,
                            ln_g[layer, 1], ln_b[layer, 1], with_next=not last, tm=512, tk=1408)
        if last:
            (x,) = res
        else:
            x, h = res
    return x
```

```python
import functools
import math

import numpy as np
import jax
import jax.numpy as jnp
from jax import lax
from jax.experimental import pallas as pl
from jax.experimental.pallas import tpu as pltpu

F32 = jnp.float32
BF16 = jnp.bfloat16

D_MODEL = 2048
D_GROUP = 1024
HEAD_DIM = 128
N_HEADS = 8
RWKV_HEAD = 64
N_RWKV = 16
CHUNK = 64
D_FF = 5632
DEPTH = 2
ALPHA = (2 * DEPTH) ** 0.25
LN_EPS = 1e-5
RWKV_LN_EPS = 64e-5
ROPE_BASE = 10000.0
RET_GAMMA_BASE = 5.0
LANES = 128
VMEM_LIMIT = 48 * 1024 * 1024


def _dot(a, b):
    return jnp.dot(a.astype(BF16), b.astype(BF16), preferred_element_type=F32)


def _dot_nt(a, b):
    return lax.dot_general(a.astype(BF16), b.astype(BF16), (((1,), (1,)), ((), ())),
                           preferred_element_type=F32)


def _split3(x):
    hi = x.astype(BF16)
    r1 = x - hi.astype(F32)
    mid = r1.astype(BF16)
    lo = (r1 - mid.astype(F32)).astype(BF16)
    return hi, mid, lo


def _cumsum_rows(x):
    n = x.shape[1]
    out = jnp.dot(_tri(CHUNK).astype(BF16), jnp.concatenate(_split3(x), axis=1), preferred_element_type=F32)
    return out[:, :n] + out[:, n:2 * n] + out[:, 2 * n:]


def _group_sum(x, ones01):
    m = x.shape[0]
    hi = x.astype(BF16)
    lo = (x - hi.astype(F32)).astype(BF16)
    out = jnp.dot(jnp.concatenate([hi, lo], axis=0), ones01.astype(BF16), preferred_element_type=F32)
    return out[:m] + out[m:]


def _sigmoid(x):
    return 1.0 / (1.0 + jnp.exp(-x))


def _silu(x):
    return x * _sigmoid(x)


def _log_sigmoid(x):
    return jnp.minimum(x, 0.0) - jnp.log1p(jnp.exp(-jnp.abs(x)))


def _softplus(x):
    return jnp.maximum(x, 0.0) + jnp.log1p(jnp.exp(-jnp.abs(x)))


def _tri(n, strict=False):
    r = lax.broadcasted_iota(jnp.int32, (n, n), 0)
    c = lax.broadcasted_iota(jnp.int32, (n, n), 1)
    return (r > c) if strict else (r >= c)


def _conv_silu_rows(src_ref, halo_ref, w, first, r0, nrows, c0, ncols):
    cur = src_ref[0, r0:r0 + nrows, c0:c0 + ncols]
    acc = w[3:4] * cur
    if r0 == 0:
        hl = jnp.where(first, 0.0, halo_ref[0, :, c0:c0 + ncols])
        ext = jnp.concatenate([hl, cur[0:8]], axis=0)
        for j in range(3):
            head = ext[5 + j:13 + j]
            if nrows > 8:
                rest = src_ref[0, 5 + j:nrows - 3 + j, c0:c0 + ncols]
                sh = jnp.concatenate([head, rest], axis=0)
            else:
                sh = head
            acc = acc + w[j:j + 1] * sh
    else:
        for j in range(3):
            acc = acc + w[j:j + 1] * src_ref[0, r0 - 3 + j:r0 - 3 + j + nrows, c0:c0 + ncols]
    return _silu(acc)


def _head_norm_rows(h, g_row, eps, center=True):
    if center:
        h = h - jnp.mean(h, axis=-1, keepdims=True)
    return h * lax.rsqrt(jnp.mean(h * h, axis=-1, keepdims=True) + eps) * g_row


MLSTM_CHUNK_GROUP = 2


def _cummax_rows(x):
    row = lax.broadcasted_iota(jnp.int32, x.shape, 0)
    d = 1
    while d < x.shape[0]:
        x = jnp.where(row >= d, jnp.maximum(x, pltpu.roll(x, d, 0)), x)
        d *= 2
    return x


def _bcast_head_cols(x, sel, pieces):
    m = x.shape[0]
    parts, rest = [], x
    for _ in range(pieces):
        hi = rest.astype(BF16)
        parts.append(hi)
        rest = rest - hi.astype(F32)
    out = jnp.dot(jnp.concatenate(parts, axis=0), sel, preferred_element_type=F32)
    acc = out[:m]
    for i in range(1, pieces):
        acc = acc + out[i * m:(i + 1) * m]
    return acc


def _mlstm_kernel(q_ref, k_ref, v_ref, o_ref, g_ref, qh_ref, kh_ref, cw_ref, gb_ref, ng_ref,
                  out_ref, qc_ref, kc_ref, sv_ref, rs_ref, hh_ref, b0_ref, cm_ref, kvn_ref, CN_ref, m_ref, *, tb):
    t = pl.program_id(1)
    first = t == 0

    @pl.when(first)
    def _():
        CN_ref[...] = jnp.zeros_like(CN_ref)
        m_ref[...] = jnp.zeros_like(m_ref)

    for c in range(tb // CHUNK):
        for cb in range(D_GROUP // 256):
            cs = cb * 256
            qc_ref[c * CHUNK:(c + 1) * CHUNK, cs:cs + 256] = _conv_silu_rows(
                q_ref, qh_ref, cw_ref[:, cs:cs + 256], first, c * CHUNK, CHUNK, cs, 256)
            kc_ref[c * CHUNK:(c + 1) * CHUNK, cs:cs + 256] = _conv_silu_rows(
                k_ref, kh_ref, cw_ref[:, D_GROUP + cs:D_GROUP + cs + 256], first, c * CHUNK, CHUNK, cs, 256)

    causal = _tri(CHUNK)
    gb = gb_ref[...]
    scale = HEAD_DIM ** -0.5
    n_chunks = tb // CHUNK
    heads = range(N_HEADS)
    hs = [slice(h * HEAD_DIM, (h + 1) * HEAD_DIM) for h in heads]
    head_lane = lax.broadcasted_iota(jnp.int32, (1, LANES), 1) < N_HEADS
    sel = (lax.broadcasted_iota(jnp.int32, (LANES, N_HEADS * LANES), 0)
           == (lax.broadcasted_iota(jnp.int32, (LANES, N_HEADS * LANES), 1) >> 7)).astype(BF16)
    ones = jnp.ones((CHUNK, HEAD_DIM), F32)
    last = slice(CHUNK - 1, CHUNK)

    for c0 in range(0, n_chunks, MLSTM_CHUNK_GROUP):
        chunks = range(c0, c0 + MLSTM_CHUNK_GROUP)
        rows = {c: slice(c * CHUNK, (c + 1) * CHUNK) for c in chunks}
        z = {c: g_ref[0, rows[c], :] + gb for c in chunks}
        b0 = {c: pltpu.roll(_cumsum_rows(_log_sigmoid(z[c])), LANES - N_HEADS, 1) for c in chunks}
        cv = {c: jnp.where(head_lane, z[c] - b0[c], 0.0) for c in chunks}
        cm = {c: _cummax_rows(cv[c]) for c in chunks}
        cT = {c: cv[c].T for c in chunks}
        cmb = {c: _bcast_head_cols(cm[c], sel, 3) for c in chunks}
        e1b = {c: _bcast_head_cols(jnp.where(head_lane, jnp.exp(cv[c] - cm[c][last]), 0.0), sel, 2) for c in chunks}
        for c in chunks:
            b0_ref[rows[c], :] = b0[c]
            cm_ref[rows[c], :] = cm[c]
        items = [(c, h) for c in chunks for h in heads]
        n_items = range(len(items))
        q = [qc_ref[rows[c], hs[h]] for c, h in items]
        k = [kc_ref[rows[c], hs[h]] * scale for c, h in items]
        vo = [jnp.concatenate([v_ref[0, rows[c], hs[h]], ones], axis=1).astype(BF16) for c, h in items]
        qk = [_dot_nt(q[i], k[i]) for i in n_items]
        s = [qk[i] * jnp.where(causal, jnp.exp(jnp.minimum(cT[c][h:h + 1, :] - cmb[c][:, h * LANES:h * LANES + CHUNK],
                                                           0.0)), 0.0) for i, (c, h) in enumerate(items)]
        s_hi = [s[i].astype(BF16) for i in n_items]
        s_lo = [(s[i] - s_hi[i].astype(F32)).astype(BF16) for i in n_items]
        svr = [jnp.dot(jnp.concatenate([s_hi[i], s_lo[i]], axis=0), vo[i], preferred_element_type=F32)
               for i in n_items]
        kvn = [_dot((k[i] * e1b[c][:, hs[h]]).T, vo[i]) for i, (c, h) in enumerate(items)]
        for i, (c, h) in enumerate(items):
            sv_ref[rows[c], hs[h]] = svr[i][:CHUNK, :HEAD_DIM] + svr[i][CHUNK:, :HEAD_DIM]
            rs_ref[rows[c], hs[h]] = svr[i][:CHUNK, HEAD_DIM:] + svr[i][CHUNK:, HEAD_DIM:]
            kvn_ref[c * N_HEADS + h] = kvn[i]

    for c in range(n_chunks):
        rows = slice(c * CHUNK, (c + 1) * CHUNK)
        m = m_ref[0:1, :]
        b0 = b0_ref[rows, :]
        cm = cm_ref[rows, :]
        mx = jnp.maximum(cm, m)
        m_new = jnp.maximum(b0[last] + m, b0[last] + cm[last])
        m_ref[0:1, :] = m_new
        zero = lambda x: jnp.where(head_lane, x, 0.0)
        fib = _bcast_head_cols(zero(jnp.exp(cm - mx)), sel, 2)
        scb = _bcast_head_cols(zero(jnp.exp(m - mx)), sel, 2)
        emtb = _bcast_head_cols(zero(jnp.exp(jnp.minimum(-(b0 + mx), 80.0))), sel, 2)
        dfb = _bcast_head_cols(jnp.concatenate([zero(jnp.exp(b0[last] + m - m_new)),
                                                zero(jnp.exp(b0[last] + cm[last] - m_new)),
                                                jnp.zeros((6, LANES), F32)], axis=0), sel, 2)
        CN = [CN_ref[h] for h in heads]
        qcn = [_dot(qc_ref[rows, hs[h]], CN[h]) for h in heads]
        for h in heads:
            num = fib[:, hs[h]] * sv_ref[rows, hs[h]] + scb[:, hs[h]] * qcn[h][:, :HEAD_DIM]
            den = fib[:, hs[h]] * rs_ref[rows, hs[h]] + scb[:, hs[h]] * qcn[h][:, HEAD_DIM:]
            hh_ref[rows, hs[h]] = num / jnp.maximum(jnp.abs(den), emtb[:, hs[h]])
            dec = jnp.concatenate([dfb[0:1, hs[h]]] * 2, axis=1)
            fkv = jnp.concatenate([dfb[1:2, hs[h]]] * 2, axis=1)
            CN_ref[h] = CN[h] * dec + kvn_ref[c * N_HEADS + h] * fkv

    for c in range(n_chunks):
        rows = slice(c * CHUNK, (c + 1) * CHUNK)
        hn = [_head_norm_rows(hh_ref[rows, hs[h]], ng_ref[:, hs[h]], LN_EPS) for h in heads]
        for h in heads:
            out_ref[0, rows, hs[h]] = (hn[h] * _sigmoid(o_ref[0, rows, hs[h]])).astype(out_ref.dtype)


def _mlstm_call(proj, gproj, conv_w, gate_b, norm_g, *, gate_col, tb=256):
    B, T, _ = proj.shape
    nt = T // tb
    gb = jnp.zeros((1, LANES), F32).at[0, :2 * N_HEADS].set(gate_b)
    ng = norm_g.reshape(1, D_GROUP)
    colblk = lambda j: pl.BlockSpec((1, tb, D_GROUP), lambda b, t: (b, t, j))
    halo = lambda j: pl.BlockSpec((1, 8, D_GROUP), lambda b, t: (b, jnp.maximum(t * (tb // 8) - 1, 0), j))
    return pl.pallas_call(
        functools.partial(_mlstm_kernel, tb=tb),
        grid=(B, nt),
        in_specs=[colblk(0), colblk(1), colblk(2), colblk(3),
                  pl.BlockSpec((1, tb, LANES), lambda b, t: (b, t, gate_col // LANES)),
                  halo(0), halo(1),
                  pl.BlockSpec((4, 2 * D_GROUP), lambda b, t: (0, 0)),
                  pl.BlockSpec((1, LANES), lambda b, t: (0, 0)),
                  pl.BlockSpec((1, D_GROUP), lambda b, t: (0, 0))],
        out_specs=pl.BlockSpec((1, tb, D_GROUP), lambda b, t: (b, t, 0)),
        out_shape=jax.ShapeDtypeStruct((B, T, D_GROUP), BF16),
        scratch_shapes=[pltpu.VMEM((tb, D_GROUP), F32)] * 5
                       + [pltpu.VMEM((tb, LANES), F32)] * 2
                       + [pltpu.VMEM((tb // CHUNK * N_HEADS, HEAD_DIM, 2 * HEAD_DIM), F32),
                          pltpu.VMEM((N_HEADS, HEAD_DIM, 2 * HEAD_DIM), F32),
                          pltpu.VMEM((8, LANES), F32)],
        compiler_params=pltpu.CompilerParams(dimension_semantics=("arbitrary", "arbitrary"),
                                             vmem_limit_bytes=VMEM_LIMIT),
        name="mlstm",
    )(proj, proj, proj, proj, gproj, proj, proj, conv_w, gb, ng)


def _rope_kernel(pos_ref, inv_ref, cos_ref, sin_ref):
    ang = pos_ref[0].astype(F32) * inv_ref[...]
    lane = lax.broadcasted_iota(jnp.int32, ang.shape, 1)
    cos_ref[0] = jnp.cos(ang)
    sin_ref[0] = jnp.where(lane < HEAD_DIM // 2, -jnp.sin(ang), jnp.sin(ang))


def _rope_call(positions, *, tb=512):
    B, T = positions.shape
    half = HEAD_DIM // 2
    inv_freq = ROPE_BASE ** (-jnp.arange(half, dtype=F32) / half)
    inv2 = jnp.concatenate([inv_freq, inv_freq]).reshape(1, HEAD_DIM)
    spec = pl.BlockSpec((1, tb, HEAD_DIM), lambda b, t: (b, t, 0))
    return pl.pallas_call(
        _rope_kernel,
        grid=(B, T // tb),
        in_specs=[pl.BlockSpec((1, tb, 1), lambda b, t: (b, t, 0)),
                  pl.BlockSpec((1, HEAD_DIM), lambda b, t: (0, 0))],
        out_specs=[spec, spec],
        out_shape=[jax.ShapeDtypeStruct((B, T, HEAD_DIM), F32)] * 2,
        compiler_params=pltpu.CompilerParams(dimension_semantics=("arbitrary", "arbitrary")),
        name="rope_table",
    )(positions.reshape(B, T, 1), inv2)


def _ret_kernel(q_ref, k_ref, v_ref, g_ref, cos_ref, sin_ref, ng_ref, out_ref, R_ref, *, tb):
    t = pl.program_id(1)

    @pl.when(t == 0)
    def _():
        R_ref[...] = jnp.zeros_like(R_ref)

    causal = _tri(CHUNK)
    ri = lax.broadcasted_iota(jnp.int32, (CHUNK, CHUNK), 0)
    ci = lax.broadcasted_iota(jnp.int32, (CHUNK, CHUNK), 1)
    rel = (ri - ci).astype(F32)
    tcol = lax.broadcasted_iota(jnp.int32, (CHUNK, 1), 0).astype(F32)
    scale = HEAD_DIM ** -0.5

    def chunk_body(c, carry):
        r0 = pl.multiple_of(c * CHUNK, CHUNK)
        rows = pl.ds(r0, CHUNK)
        cos2 = cos_ref[0, rows, :]
        sin2 = sin_ref[0, rows, :]
        for h in range(N_HEADS):
            hs = slice(h * HEAD_DIM, (h + 1) * HEAD_DIM)
            lg = math.log1p(-2.0 ** (-RET_GAMMA_BASE - h))
            q = q_ref[0, rows, hs]
            k = k_ref[0, rows, hs]
            v = v_ref[0, rows, hs]
            qr = q * cos2 + pltpu.roll(q, HEAD_DIM // 2, 1) * sin2
            kr = (k * cos2 + pltpu.roll(k, HEAD_DIM // 2, 1) * sin2) * scale
            dmat = jnp.where(causal, jnp.exp(rel * lg), 0.0)
            xi = jnp.exp((tcol + 1.0) * lg)
            zeta = jnp.exp((CHUNK - 1.0 - tcol) * lg)
            Rst = R_ref[h]
            intra = _dot(_dot_nt(qr, kr) * dmat, v)
            inter = _dot(qr, Rst) * xi
            R_ref[h] = Rst * math.exp(CHUNK * lg) + _dot((kr * zeta).T, v)
            on = _head_norm_rows(intra + inter, ng_ref[:, hs], LN_EPS)
            out_ref[0, rows, hs] = (on * _silu(g_ref[0, rows, hs])).astype(out_ref.dtype)
        return carry

    lax.fori_loop(0, tb // CHUNK, chunk_body, 0)


def _ret_call(proj, cos2, sin2, norm_g, *, tb=256):
    B, T, _ = proj.shape
    colblk = lambda j: pl.BlockSpec((1, tb, D_GROUP), lambda b, t: (b, t, j))
    tab = pl.BlockSpec((1, tb, HEAD_DIM), lambda b, t: (b, t, 0))
    return pl.pallas_call(
        functools.partial(_ret_kernel, tb=tb),
        grid=(B, T // tb),
        in_specs=[colblk(0), colblk(1), colblk(2), colblk(3), tab, tab,
                  pl.BlockSpec((1, D_GROUP), lambda b, t: (0, 0))],
        out_specs=pl.BlockSpec((1, tb, D_GROUP), lambda b, t: (b, t, 0)),
        out_shape=jax.ShapeDtypeStruct((B, T, D_GROUP), BF16),
        scratch_shapes=[pltpu.VMEM((N_HEADS, HEAD_DIM, HEAD_DIM), F32)],
        compiler_params=pltpu.CompilerParams(dimension_semantics=("arbitrary", "arbitrary"),
                                             vmem_limit_bytes=VMEM_LIMIT),
        name="retention",
    )(proj, proj, proj, proj, cos2, sin2, norm_g.reshape(1, D_GROUP))


def _inv_unit_lower(nms):
    n = nms[0].shape[0]
    eye = (lax.broadcasted_iota(jnp.int32, (n, n), 0) == lax.broadcasted_iota(jnp.int32, (n, n), 1)).astype(F32)
    ps = [eye + nm for nm in nms]
    xs = [_dot(nm, nm) for nm in nms]
    for _ in range(int(math.log2(n)) - 2):
        px = [_dot(jnp.concatenate([p, x], axis=0), x) for p, x in zip(ps, xs)]
        ps = [p + y[:n] for p, y in zip(ps, px)]
        xs = [y[n:] for y in px]
    ps = [p + _dot(p, x) for p, x in zip(ps, xs)]
    resid = [eye - p + _dot(nm, p) for p, nm in zip(ps, nms)]
    return [p + _dot(p, r) for p, r in zip(ps, resid)]


def _inv_unit_lower_packed(nms, m0, m1):
    n = nms[0].shape[0]
    r = lax.broadcasted_iota(jnp.int32, (n, 2 * n), 0)
    c = lax.broadcasted_iota(jnp.int32, (n, 2 * n), 1)
    eye2 = (r == (c & (n - 1))).astype(F32)
    bd = lambda x: jnp.concatenate([x * m0, x * m1], axis=0)
    ps = [eye2 + nm for nm in nms]
    xs = [_dot(nm, bd(nm)) for nm in nms]
    for _ in range(int(math.log2(n)) - 2):
        px = [_dot(jnp.concatenate([p, x], axis=0), bd(x)) for p, x in zip(ps, xs)]
        ps = [p + y[:n] for p, y in zip(ps, px)]
        xs = [y[n:] for y in px]
    ps = [p + _dot(p, bd(x)) for p, x in zip(ps, xs)]
    resid = [eye2 - p + _dot(nm, bd(p)) for p, nm in zip(ps, nms)]
    return [p + _dot(p, bd(r_)) for p, r_ in zip(ps, resid)]


def _solve_unit_lower(nms, rhss):
    n = nms[0].shape[0]
    eye = (lax.broadcasted_iota(jnp.int32, (n, n), 0) == lax.broadcasted_iota(jnp.int32, (n, n), 1)).astype(F32)
    ps = [eye + nm for nm in nms]
    xs = [_dot(nm, nm) for nm in nms]
    for _ in range(int(math.log2(n)) - 2):
        px = [_dot(jnp.concatenate([p, x], axis=0), x) for p, x in zip(ps, xs)]
        ps = [p + y[:n] for p, y in zip(ps, px)]
        xs = [y[n:] for y in px]
    ps = [p + _dot(p, x) for p, x in zip(ps, xs)]
    x0 = [_dot(p, r) for p, r in zip(ps, rhss)]
    resid = [r - a + _dot(nm, a) for r, a, nm in zip(rhss, x0, nms)]
    return [a + _dot(p, r) for a, p, r in zip(x0, ps, resid)]


def _l2norm_rows(z):
    return z * lax.rsqrt(jnp.sum(z * z, axis=-1, keepdims=True) + 1e-6)


GDN_CHUNK_GROUP = 2


def _gdn_kernel(q_ref, k_ref, v_ref, z_ref, g_ref, qh_ref, kh_ref, vh_ref, cw_ref, an_ref, dt_ref, ng_ref,
                out_ref, qc_ref, kc_ref, vc_ref, u_ref, w_ref, qe_ref, o_ref, att_ref, kdT_ref, gl_ref, S_ref,
                *, tb):
    t = pl.program_id(1)
    first = t == 0

    @pl.when(first)
    def _():
        S_ref[...] = jnp.zeros_like(S_ref)

    srcs = ((q_ref, qh_ref, qc_ref), (k_ref, kh_ref, kc_ref), (v_ref, vh_ref, vc_ref))
    for c in range(tb // CHUNK):
        for cb in range(D_GROUP // 256):
            cs = cb * 256
            for i, (src, halo, dst) in enumerate(srcs):
                w = cw_ref[:, i * D_GROUP + cs:i * D_GROUP + cs + 256]
                dst[c * CHUNK:(c + 1) * CHUNK, cs:cs + 256] = _conv_silu_rows(
                    src, halo, w, first, c * CHUNK, CHUNK, cs, 256)

    causal = _tri(CHUNK)
    strict = _tri(CHUNK, strict=True)
    a_neg = an_ref[...]
    dtb = dt_ref[...]
    scale = HEAD_DIM ** -0.5
    n_chunks = tb // CHUNK
    heads = range(N_HEADS)
    hs = [slice(h * HEAD_DIM, (h + 1) * HEAD_DIM) for h in heads]

    for c0 in range(0, n_chunks, GDN_CHUNK_GROUP):
        chunks = range(c0, c0 + GDN_CHUNK_GROUP)
        rows = {c: slice(c * CHUNK, (c + 1) * CHUNK) for c in chunks}
        gz = {c: g_ref[0, rows[c], :] for c in chunks}
        beta = {c: _sigmoid(gz[c]) for c in chunks}
        gc = {c: _cumsum_rows(a_neg * _softplus(gz[c] + dtb)) for c in chunks}
        gcT = {c: gc[c].T for c in chunks}
        for c in chunks:
            gl_ref[c:c + 1, :] = gc[c][CHUNK - 1:CHUNK, :]
        items = [(c, h) for c in chunks for h in heads]
        n_items = range(len(items))
        gc_col = [gc[c][:, h:h + 1] for c, h in items]
        b_col = [beta[c][:, 8 + h:9 + h] for c, h in items]
        gamma = [jnp.where(causal, jnp.exp(gc_col[i] - gcT[c][h:h + 1, :]), 0.0) for i, (c, h) in enumerate(items)]
        q = [_l2norm_rows(qc_ref[rows[c], hs[h]]) * scale for c, h in items]
        k = [_l2norm_rows(kc_ref[rows[c], hs[h]]) for c, h in items]
        kb = [k[i] * b_col[i] for i in n_items]
        eg = [jnp.exp(gc_col[i]) for i in n_items]
        kq = [_dot_nt(jnp.concatenate([kb[i], q[i]], axis=0), k[i]) for i in n_items]
        inv = _inv_unit_lower([-jnp.where(strict, kq[i][:CHUNK] * gamma[i], 0.0) for i in n_items])
        uw = [_dot(inv[i], jnp.concatenate([vc_ref[rows[c], hs[h]] * b_col[i], kb[i] * eg[i]], axis=1))
              for i, (c, h) in enumerate(items)]
        for i, (c, h) in enumerate(items):
            u_ref[rows[c], hs[h]] = uw[i][:, :HEAD_DIM]
            w_ref[rows[c], hs[h]] = uw[i][:, HEAD_DIM:]
            qe_ref[rows[c], hs[h]] = q[i] * eg[i]
            att_ref[h, rows[c], :] = kq[i][CHUNK:] * gamma[i]
            g_last = gc[c][CHUNK - 1:CHUNK, h:h + 1]
            kdT_ref[c * N_HEADS + h] = (k[i] * jnp.exp(g_last - gc_col[i])).T

    for c in range(n_chunks):
        rows = slice(c * CHUNK, (c + 1) * CHUNK)
        S = [S_ref[h] for h in heads]
        ws = [_dot(jnp.concatenate([w_ref[rows, hs[h]], qe_ref[rows, hs[h]]], axis=0), S[h]) for h in heads]
        v_new = [u_ref[rows, hs[h]] - ws[h][:CHUNK] for h in heads]
        av = [_dot(att_ref[h, rows, :], v_new[h]) for h in heads]
        kv = [_dot(kdT_ref[c * N_HEADS + h], v_new[h]) for h in heads]
        for h in heads:
            S_ref[h] = S[h] * jnp.exp(gl_ref[c:c + 1, h:h + 1]) + kv[h]
            o_ref[rows, hs[h]] = ws[h][CHUNK:] + av[h]

    for c in range(n_chunks):
        rows = slice(c * CHUNK, (c + 1) * CHUNK)
        for h in heads:
            on = _head_norm_rows(o_ref[rows, hs[h]], ng_ref[:, hs[h]], 1e-6, center=False)
            out_ref[0, rows, hs[h]] = (on * _silu(z_ref[0, rows, hs[h]])).astype(out_ref.dtype)


def _gdn_call(proj, gproj, conv_w, a_log, dt_bias, norm_g, *, col0, gate_col, tb=256):
    B, T, _ = proj.shape
    j0 = col0 // D_GROUP
    an = jnp.zeros((1, LANES), F32).at[0, :N_HEADS].set(-jnp.exp(a_log.astype(F32)))
    dtb = jnp.zeros((1, LANES), F32).at[0, :N_HEADS].set(dt_bias)
    colblk = lambda j: pl.BlockSpec((1, tb, D_GROUP), lambda b, t: (b, t, j0 + j))
    halo = lambda j: pl.BlockSpec((1, 8, D_GROUP), lambda b, t: (b, jnp.maximum(t * (tb // 8) - 1, 0), j0 + j))
    return pl.pallas_call(
        functools.partial(_gdn_kernel, tb=tb),
        grid=(B, T // tb),
        in_specs=[colblk(0), colblk(1), colblk(2), colblk(3),
                  pl.BlockSpec((1, tb, LANES), lambda b, t: (b, t, gate_col // LANES)),
                  halo(0), halo(1), halo(2),
                  pl.BlockSpec((4, 3 * D_GROUP), lambda b, t: (0, 0)),
                  pl.BlockSpec((1, LANES), lambda b, t: (0, 0)),
                  pl.BlockSpec((1, LANES), lambda b, t: (0, 0)),
                  pl.BlockSpec((1, D_GROUP), lambda b, t: (0, 0))],
        out_specs=pl.BlockSpec((1, tb, D_GROUP), lambda b, t: (b, t, 0)),
        out_shape=jax.ShapeDtypeStruct((B, T, D_GROUP), BF16),
        scratch_shapes=[pltpu.VMEM((tb, D_GROUP), F32)] * 7
                       + [pltpu.VMEM((N_HEADS, tb, CHUNK), F32),
                          pltpu.VMEM((tb // CHUNK * N_HEADS, HEAD_DIM, CHUNK), F32),
                          pltpu.VMEM((max(tb // CHUNK, 8), LANES), F32),
                          pltpu.VMEM((N_HEADS, HEAD_DIM, HEAD_DIM), F32)],
        compiler_params=pltpu.CompilerParams(dimension_semantics=("arbitrary", "arbitrary"),
                                             vmem_limit_bytes=VMEM_LIMIT),
        name="gdn",
    )(proj, proj, proj, proj, gproj, proj, proj, proj, conv_w, an, dtb, norm_g.reshape(1, D_GROUP))


N_PAIRS = N_RWKV // 2
RWKV_LOW = 384


def _shift1_rows(src_ref, halo_ref, first, r0, nrows, c0, ncols):
    if r0 == 0:
        hl = jnp.where(first, 0.0, halo_ref[0, 7:8, c0:c0 + ncols])
        return jnp.concatenate([hl, src_ref[0, 0:nrows - 1, c0:c0 + ncols]], axis=0)
    return src_ref[0, r0 - 1:r0 - 1 + nrows, c0:c0 + ncols]


def _rwkv_kernel(r_ref, k_ref, v_ref, l0_ref, l1_ref, l2_ref,
                 rh_ref, kh_ref, vh_ref, l0h_ref, l1h_ref, l2h_ref,
                 mu_ref, mul_ref, w0_ref, w2_ref, a0_ref, a2_ref, g2_ref, kk_ref, ka_ref, rk_ref,
                 lng_ref, lnb_ref, out_ref,
                 gs_ref, bo_ref, y_ref, atrt_ref, avk_ref, yk_ref, inv_ref, arb_ref, btT_ref, kvT_ref, wlT_ref,
                 H_ref, *, tb):
    t = pl.program_id(1)
    first = t == 0

    @pl.when(first)
    def _():
        H_ref[...] = jnp.zeros_like(H_ref)

    ri = lax.broadcasted_iota(jnp.int32, (LANES, LANES), 0)
    ci = lax.broadcasted_iota(jnp.int32, (LANES, LANES), 1)
    same_head = ((ri // RWKV_HEAD) == (ci // RWKV_HEAD)).astype(F32)
    causal = _tri(CHUNK)
    strict = _tri(CHUNK, strict=True)
    lane1 = lax.broadcasted_iota(jnp.int32, (1, LANES), 1)
    m0 = (lane1 < RWKV_HEAD).astype(F32)
    m1 = 1.0 - m0
    t2 = lax.broadcasted_iota(jnp.int32, (CHUNK, LANES), 0)
    l2 = lax.broadcasted_iota(jnp.int32, (CHUNK, LANES), 1)
    lo = l2 < RWKV_HEAD
    s2 = l2 & (RWKV_HEAD - 1)
    causal2 = t2 >= s2
    strict2 = t2 > s2
    n_chunks = tb // CHUNK
    pairs = range(N_PAIRS)
    halves = [(p, hh) for p in pairs for hh in range(2)]
    ps = [slice(p * LANES, (p + 1) * LANES) for p in pairs]

    def lerp(src, halo, mu, r0, c0, ncols):
        cur = src[0, r0:r0 + CHUNK, c0:c0 + ncols]
        return cur + (_shift1_rows(src, halo, first, r0, CHUNK, c0, ncols) - cur) * mu

    for c in range(n_chunks):
        r0 = c * CHUNK
        rows = slice(r0, r0 + CHUNK)
        wl = lerp(l0_ref, l0h_ref, mul_ref[:, 0:LANES], r0, 0, LANES)
        g1 = lerp(l1_ref, l1h_ref, mul_ref[:, LANES:2 * LANES], r0, 0, LANES)
        g2 = lerp(l2_ref, l2h_ref, mul_ref[:, 2 * LANES:3 * LANES], r0, 0, LANES)
        wl_t = jnp.where(lo, jnp.tanh(wl), 0.0)
        al = jnp.where(lo, 0.0, wl)
        sg1 = _sigmoid(g1)
        sg2 = jnp.where(l2 < 32, _sigmoid(g2), 0.0)
        w = [-_softplus(-(w0_ref[:, ps[p]] + _dot(wl_t, w2_ref[:, ps[p]]))) - 0.5 for p in pairs]
        a = [_sigmoid(a0_ref[:, ps[p]] + _dot(al, a2_ref[:, ps[p]])) for p in pairs]
        g = [_dot(sg1, g2_ref[0:LANES, ps[p]]) + _dot(sg2, g2_ref[LANES:2 * LANES, ps[p]]) for p in pairs]
        r = [lerp(r_ref, rh_ref, mu_ref[:, ps[p]], r0, p * LANES, LANES) for p in pairs]
        k = [lerp(k_ref, kh_ref, mu_ref[:, D_GROUP + p * LANES:D_GROUP + (p + 1) * LANES], r0, p * LANES, LANES)
             for p in pairs]
        v = [lerp(v_ref, vh_ref, mu_ref[:, 2 * D_GROUP + p * LANES:2 * D_GROUP + (p + 1) * LANES], r0, p * LANES, LANES)
             for p in pairs]
        kk = [k[p] * kk_ref[:, ps[p]] for p in pairs]
        nrm = [jnp.sqrt(_group_sum(kk[p] * kk[p], same_head)) for p in pairs]
        kk = [kk[p] / jnp.maximum(nrm[p], 1e-12) for p in pairs]
        k2 = [k[p] * (1.0 + (a[p] - 1.0) * ka_ref[:, ps[p]]) for p in pairs]
        rk = [_group_sum(r[p] * k2[p] * rk_ref[:, ps[p]], same_head) for p in pairs]
        for p in pairs:
            gs_ref[p, rows, :] = g[p]
            bo_ref[p, rows, :] = rk[p] * v[p]
        lw = [-jnp.exp(w[p]) for p in pairs]
        cs = [_cumsum_rows(lw[p]) for p in pairs]
        w_inv = [jnp.exp(-cs[p]) for p in pairs]
        w_end = [jnp.exp(cs[p][CHUNK - 1:CHUNK, :]) for p in pairs]
        rt = [r[p] * jnp.exp(cs[p]) for p in pairs]
        at = [-kk[p] * jnp.exp(cs[p] - lw[p]) for p in pairs]
        bt = [kk[p] * a[p] * w_inv[p] for p in pairs]
        kt = [k2[p] * w_inv[p] for p in pairs]
        pm0 = [_dot_nt(jnp.concatenate([at[p] * m0, rt[p] * m0], axis=0), jnp.concatenate([bt[p], kt[p]], axis=0))
               for p in pairs]
        pm1 = [_dot_nt(jnp.concatenate([at[p] * m1, rt[p] * m1], axis=0), jnp.concatenate([kt[p], bt[p]], axis=0))
               for p in pairs]
        n_ab = [jnp.where(strict2, jnp.where(lo, pm0[p][:CHUNK], pm1[p][:CHUNK]), 0.0) for p in pairs]
        a_rb = [jnp.where(causal2, jnp.where(lo, pm0[p][CHUNK:], pm1[p][CHUNK:]), 0.0) for p in pairs]
        akrk = [jnp.concatenate([jnp.where(strict2, jnp.where(lo, pm1[p][:CHUNK], pm0[p][:CHUNK]), 0.0),
                                 jnp.where(causal2, jnp.where(lo, pm1[p][CHUNK:], pm0[p][CHUNK:]), 0.0)], axis=0)
                for p in pairs]
        vk = [_dot(akrk[p], jnp.concatenate([v[p] * m1, v[p] * m0], axis=0)) for p in pairs]
        inv = _inv_unit_lower_packed(n_ab, m0, m1)
        kv = [_dot((kt[p] * w_end[p]).T, v[p]) * same_head for p in pairs]
        for p in pairs:
            i = c * N_PAIRS + p
            atrt_ref[i] = jnp.concatenate([at[p], rt[p]], axis=0).astype(BF16)
            avk_ref[i] = vk[p][:CHUNK]
            yk_ref[i] = vk[p][CHUNK:]
            btT_ref[i] = (bt[p] * w_end[p]).T.astype(BF16)
            kvT_ref[i] = kv[p]
            wlT_ref[i] = jnp.broadcast_to(w_end[p], (LANES, LANES)).T
            inv_ref[i] = inv[p].astype(BF16)
            arb_ref[i] = a_rb[p].astype(BF16)

    for c in range(n_chunks):
        rows = slice(c * CHUNK, (c + 1) * CHUNK)
        it = [c * N_PAIRS + p for p in pairs]
        H = [H_ref[p] for p in pairs]
        xy0 = [_dot(atrt_ref[it[p]], H[p]) for p in pairs]
        x = [xy0[p][:CHUNK] + avk_ref[it[p]] for p in pairs]
        u = [_dot(inv_ref[it[p]], jnp.concatenate([x[p] * m0, x[p] * m1], axis=0)) for p in pairs]
        yb = [_dot(arb_ref[it[p]], jnp.concatenate([u[p] * m0, u[p] * m1], axis=0)) for p in pairs]
        bu = [_dot(btT_ref[it[p]], u[p]) for p in pairs]
        for p in pairs:
            y_ref[p, rows, :] = xy0[p][CHUNK:] + yk_ref[it[p]] + yb[p]
            H_ref[p] = H[p] * wlT_ref[it[p]] + bu[p] * same_head + kvT_ref[it[p]]

    inv_n = 1.0 / RWKV_HEAD
    for c in range(n_chunks):
        rows = slice(c * CHUNK, (c + 1) * CHUNK)
        y = [y_ref[p, rows, :] for p in pairs]
        yc = [y[p] - _group_sum(y[p], same_head) * inv_n for p in pairs]
        var = [_group_sum(yc[p] * yc[p], same_head) * inv_n for p in pairs]
        for p in pairs:
            yn = yc[p] * lax.rsqrt(var[p] + RWKV_LN_EPS) * lng_ref[:, ps[p]] + lnb_ref[:, ps[p]]
            out_ref[0, rows, ps[p]] = ((yn + bo_ref[p, rows, :]) * gs_ref[p, rows, :]).astype(out_ref.dtype)


def _rwkv_call(proj, mu, w0, w2, a0, a2, g2, k_k, k_a, r_k, ln_g, ln_b, *, col0, tb=256):
    B, T, _ = proj.shape
    j0 = col0 // D_GROUP
    l0 = (col0 + 3 * D_GROUP) // LANES
    row = lambda a: a.reshape(1, -1).astype(F32)
    mul = jnp.zeros((1, RWKV_LOW), F32).at[0, :288].set(mu[3 * D_GROUP:])
    w2p = jnp.zeros((LANES, D_GROUP), F32).at[:64].set(w2)
    a2p = jnp.zeros((LANES, D_GROUP), F32).at[64:].set(a2)
    g2p = jnp.zeros((2 * LANES, D_GROUP), F32).at[:160].set(g2)
    colblk = lambda j: pl.BlockSpec((1, tb, D_GROUP), lambda b, t: (b, t, j0 + j))
    lowblk = lambda j: pl.BlockSpec((1, tb, LANES), lambda b, t: (b, t, l0 + j))
    hrow = lambda t: jnp.maximum(t * (tb // 8) - 1, 0)
    halo = lambda j: pl.BlockSpec((1, 8, D_GROUP), lambda b, t: (b, hrow(t), j0 + j))
    lowhalo = lambda j: pl.BlockSpec((1, 8, LANES), lambda b, t: (b, hrow(t), l0 + j))
    full = lambda a: pl.BlockSpec(a.shape, lambda b, t: (0,) * a.ndim)
    params = [row(mu[:3 * D_GROUP]), mul, row(w0), w2p, row(a0), a2p, g2p, row(k_k), row(k_a), row(r_k),
              row(ln_g), row(ln_b)]
    big = pltpu.VMEM((N_PAIRS, tb, LANES), F32)
    n_items = tb // CHUNK * N_PAIRS
    return pl.pallas_call(
        functools.partial(_rwkv_kernel, tb=tb),
        grid=(B, T // tb),
        in_specs=[colblk(0), colblk(1), colblk(2), lowblk(0), lowblk(1), lowblk(2),
                  halo(0), halo(1), halo(2), lowhalo(0), lowhalo(1), lowhalo(2)] + [full(a) for a in params],
        out_specs=pl.BlockSpec((1, tb, D_GROUP), lambda b, t: (b, t, 0)),
        out_shape=jax.ShapeDtypeStruct((B, T, D_GROUP), BF16),
        scratch_shapes=[big] * 3 + [
            pltpu.VMEM((n_items, 2 * CHUNK, LANES), BF16),
            pltpu.VMEM((n_items, CHUNK, LANES), F32),
            pltpu.VMEM((n_items, CHUNK, LANES), F32),
            pltpu.VMEM((n_items, CHUNK, LANES), BF16),
            pltpu.VMEM((n_items, CHUNK, LANES), BF16),
            pltpu.VMEM((n_items, LANES, CHUNK), BF16),
            pltpu.VMEM((n_items, LANES, LANES), F32),
            pltpu.VMEM((n_items, LANES, LANES), F32),
            pltpu.VMEM((N_PAIRS, LANES, LANES), F32)],
        compiler_params=pltpu.CompilerParams(dimension_semantics=("arbitrary", "arbitrary"),
                                             vmem_limit_bytes=VMEM_LIMIT),
        name="rwkv7",
    )(*([proj] * 12), *params)


def _ada_kernel(c_ref, w_ref, b_ref, out_ref):
    sc = _silu(c_ref[...]).astype(BF16)
    out_ref[0] = jnp.dot(sc, w_ref[0].astype(BF16), preferred_element_type=F32) + b_ref[0]


def _ada_call(c, ada_w, ada_b, *, tn=1536):
    B = c.shape[0]
    n_mod = ada_w.shape[0] * ada_w.shape[1]
    w = ada_w.reshape(n_mod, D_MODEL, 3 * D_MODEL)
    b = ada_b.reshape(n_mod, 1, 3 * D_MODEL)
    return pl.pallas_call(
        _ada_kernel,
        grid=(n_mod, 3 * D_MODEL // tn),
        in_specs=[pl.BlockSpec((B, D_MODEL), lambda i, j: (0, 0)),
                  pl.BlockSpec((1, D_MODEL, tn), lambda i, j: (i, 0, j)),
                  pl.BlockSpec((1, 1, tn), lambda i, j: (i, 0, j))],
        out_specs=pl.BlockSpec((1, B, tn), lambda i, j: (i, 0, j)),
        out_shape=jax.ShapeDtypeStruct((n_mod, B, 3 * D_MODEL), F32),
        compiler_params=pltpu.CompilerParams(dimension_semantics=("arbitrary", "arbitrary"),
                                             vmem_limit_bytes=VMEM_LIMIT),
        name="adaln",
    )(c, w, b)


def _mod_spec(i, part, nb):
    return pl.BlockSpec((1, 1, D_MODEL), lambda b, t: (i * nb + b, 0, part))


def _modulate_kernel(x_ref, shift_ref, scale_ref, h_ref):
    h_ref[0] = (x_ref[0] * (1.0 + scale_ref[0]) + shift_ref[0]).astype(h_ref.dtype)


def _modulate_call(x, mods3, i, *, tb=512):
    B, T, _ = x.shape
    blk = pl.BlockSpec((1, tb, D_MODEL), lambda b, t: (b, t, 0))
    return pl.pallas_call(
        _modulate_kernel,
        grid=(B, T // tb),
        in_specs=[blk, _mod_spec(i, 0, B), _mod_spec(i, 1, B)],
        out_specs=blk,
        out_shape=jax.ShapeDtypeStruct(x.shape, BF16),
        compiler_params=pltpu.CompilerParams(dimension_semantics=("arbitrary", "arbitrary"),
                                             vmem_limit_bytes=VMEM_LIMIT),
        name="modulate",
    )(x, mods3, mods3)


LN_ROWS = 16


def _proj_ln_kernel(*refs, n_lhs, nk, tm, with_next):
    lhs = refs[:n_lhs]
    ws = refs[n_lhs:2 * n_lhs]
    x_ref, gate_ref, g_ref, b_ref = refs[2 * n_lhs:2 * n_lhs + 4]
    if with_next:
        shift_ref, scale_ref, xo_ref, h_ref, acc_ref = refs[2 * n_lhs + 4:]
    else:
        xo_ref, acc_ref = refs[2 * n_lhs + 4:]
    k = pl.program_id(2)
    def partial_product():
        part = jnp.dot(lhs[0][0], ws[0][...], preferred_element_type=F32)
        for j in range(1, n_lhs):
            part = part + jnp.dot(lhs[j][0], ws[j][...], preferred_element_type=F32)
        return part

    @pl.when(k == 0)
    def _():
        acc_ref[...] = partial_product()

    @pl.when(k > 0)
    def _():
        acc_ref[...] += partial_product()

    @pl.when(k == nk - 1)
    def _():
        gate1 = 1.0 + gate_ref[0]
        g, b = g_ref[...], b_ref[...]
        if with_next:
            scale1, shift = 1.0 + scale_ref[0], shift_ref[0]

        def rows_body(r, carry):
            rows = pl.ds(pl.multiple_of(r * LN_ROWS, LN_ROWS), LN_ROWS)
            z = ALPHA * x_ref[0, rows, :] + gate1 * acc_ref[rows, :]
            zc = z - jnp.mean(z, axis=-1, keepdims=True)
            var = jnp.mean(zc * zc, axis=-1, keepdims=True)
            xn = zc * lax.rsqrt(var + LN_EPS) * g + b
            xo_ref[0, rows, :] = xn
            if with_next:
                h_ref[0, rows, :] = (xn * scale1 + shift).astype(h_ref.dtype)
            return carry

        lax.fori_loop(0, tm // LN_ROWS, rows_body, 0, unroll=8)


def _proj_ln_call(lhs, ws, x, mods3, i, g, b, *, with_next, tm, tk):
    B, T, D = x.shape
    n_lhs = len(lhs)
    nk = lhs[0].shape[2] // tk
    blk = pl.BlockSpec((1, tm, D), lambda b, t, k: (b, t, 0))
    row = pl.BlockSpec((1, D), lambda b, t, k: (0, 0))
    mod = lambda ii, part: pl.BlockSpec((1, 1, D), lambda b, t, k: (ii * B + b, 0, part))
    in_specs = ([pl.BlockSpec((1, tm, tk), lambda b, t, k: (b, t, k))] * n_lhs
                + [pl.BlockSpec((tk, D), lambda b, t, k: (k, 0))] * n_lhs
                + [blk, mod(i, 2), row, row])
    args = list(lhs) + list(ws) + [x, mods3, g.reshape(1, D), b.reshape(1, D)]
    out_specs = [blk]
    out_shape = [jax.ShapeDtypeStruct(x.shape, F32)]
    if with_next:
        in_specs += [mod(i + 1, 0), mod(i + 1, 1)]
        args += [mods3, mods3]
        out_specs.append(blk)
        out_shape.append(jax.ShapeDtypeStruct(x.shape, BF16))
    return pl.pallas_call(
        functools.partial(_proj_ln_kernel, n_lhs=n_lhs, nk=nk, tm=tm, with_next=with_next),
        grid=(B, T // tm, nk),
        in_specs=in_specs, out_specs=out_specs, out_shape=out_shape,
        scratch_shapes=[pltpu.VMEM((tm, D), F32)],
        compiler_params=pltpu.CompilerParams(dimension_semantics=("arbitrary", "arbitrary", "arbitrary"),
                                             vmem_limit_bytes=VMEM_LIMIT),
        name="proj_residual_ln",
    )(*args)


def _matmul_kernel(a_ref, b_ref, o_ref, *scratch, nk):
    if nk == 1:
        o_ref[...] = jnp.dot(a_ref[...], b_ref[...], preferred_element_type=F32).astype(o_ref.dtype)
        return
    (acc_ref,) = scratch
    k = pl.program_id(2)

    @pl.when(k == 0)
    def _():
        acc_ref[...] = jnp.zeros_like(acc_ref)

    acc_ref[...] += jnp.dot(a_ref[...], b_ref[...], preferred_element_type=F32)

    @pl.when(k == nk - 1)
    def _():
        o_ref[...] = acc_ref[...].astype(o_ref.dtype)


def _matmul_call(a, b, *, tm, tn, tk, out_dtype=F32):
    M, K = a.shape
    _, N = b.shape
    nk = K // tk
    return pl.pallas_call(
        functools.partial(_matmul_kernel, nk=nk),
        grid=(N // tn, M // tm, nk),
        in_specs=[pl.BlockSpec((tm, tk), lambda j, i, k: (i, k)),
                  pl.BlockSpec((tk, tn), lambda j, i, k: (k, j))],
        out_specs=pl.BlockSpec((tm, tn), lambda j, i, k: (i, j)),
        out_shape=jax.ShapeDtypeStruct((M, N), out_dtype),
        scratch_shapes=[] if nk == 1 else [pltpu.VMEM((tm, tn), F32)],
        compiler_params=pltpu.CompilerParams(dimension_semantics=("arbitrary", "arbitrary", "arbitrary"),
                                             vmem_limit_bytes=VMEM_LIMIT),
        name="matmul",
    )(a, b)


def _matmul_f32w_kernel(a_ref, w_ref, o_ref, wb_ref):
    @pl.when(pl.program_id(1) == 0)
    def _():
        wb_ref[...] = w_ref[...].astype(BF16)

    o_ref[...] = jnp.dot(a_ref[...], wb_ref[...], preferred_element_type=F32)


def _matmul_f32w_call(a, w, idx, ncols, *, tm=512, tn=1024):
    M, K = a.shape
    return pl.pallas_call(
        _matmul_f32w_kernel,
        grid=(ncols // tn, M // tm),
        in_specs=[pl.BlockSpec((tm, K), lambda j, i: (i, 0)),
                  pl.BlockSpec((None, K, tn), lambda j, i: (idx, 0, j))],
        out_specs=pl.BlockSpec((tm, tn), lambda j, i: (i, j)),
        out_shape=jax.ShapeDtypeStruct((M, ncols), F32),
        scratch_shapes=[pltpu.VMEM((K, tn), BF16)],
        compiler_params=pltpu.CompilerParams(dimension_semantics=("arbitrary", "arbitrary"),
                                             vmem_limit_bytes=VMEM_LIMIT),
        name="matmul_f32w",
    )(a, w)


def _swiglu_up_kernel(h_ref, wg_ref, wu_ref, o_ref, wgb_ref, wub_ref):
    @pl.when(pl.program_id(1) == 0)
    def _():
        wgb_ref[...] = wg_ref[...].astype(BF16)
        wub_ref[...] = wu_ref[...].astype(BF16)

    h = h_ref[...]
    g = jnp.dot(h, wgb_ref[...], preferred_element_type=F32)
    u = jnp.dot(h, wub_ref[...], preferred_element_type=F32)
    o_ref[...] = (_silu(g) * u).astype(o_ref.dtype)


def _swiglu_up_call(h, wg, wu, layer, *, tm=512, tn=512):
    M, K = h.shape
    _, _, N = wg.shape
    wspec = pl.BlockSpec((None, K, tn), lambda j, i: (layer, 0, j))
    return pl.pallas_call(
        _swiglu_up_kernel,
        grid=(N // tn, M // tm),
        in_specs=[pl.BlockSpec((tm, K), lambda j, i: (i, 0)), wspec, wspec],
        out_specs=pl.BlockSpec((tm, tn), lambda j, i: (i, j)),
        out_shape=jax.ShapeDtypeStruct((M, N), BF16),
        scratch_shapes=[pltpu.VMEM((K, tn), BF16), pltpu.VMEM((K, tn), BF16)],
        compiler_params=pltpu.CompilerParams(dimension_semantics=("arbitrary", "arbitrary"),
                                             vmem_limit_bytes=VMEM_LIMIT),
        name="swiglu_up",
    )(h, wg, wu)


A_MAIN = 4 * D_GROUP
B_COLS_PAD = 3 * D_GROUP + RWKV_LOW + LANES
MLSTM_GATE_COL = 3 * D_GROUP + RWKV_LOW
CD_MAIN = 8 * D_GROUP
GDN_COL0 = 4 * D_GROUP


def _pad_cols(w, n):
    return jnp.pad(w, ((0, 0), (0, n - w.shape[1])))


def _b_weight(w_in):
    a_gates = w_in[:, A_MAIN:A_MAIN + 2 * N_HEADS]
    b0 = A_MAIN + 2 * N_HEADS
    b_main, b_low = w_in[:, b0:b0 + 3 * D_GROUP], w_in[:, b0 + 3 * D_GROUP:]
    return jnp.concatenate([b_main, _pad_cols(b_low, RWKV_LOW), _pad_cols(a_gates, LANES)], axis=1).astype(BF16)


def kernel(x, c, positions, ada_w, ada_b, ln_g, ln_b, ab_w_in, ab_w_out, mlstm_conv_w, mlstm_gate_b, mlstm_norm_g, rwkv_mu, rwkv_w0, rwkv_w2, rwkv_a0, rwkv_a2, rwkv_g2, rwkv_k_k, rwkv_k_a, rwkv_r_k, rwkv_ln_g, rwkv_ln_b, cd_w_in, cd_w_out, ret_norm_g, gdn_conv_w, gdn_a_log, gdn_dt_bias, gdn_norm_g, ffn_w_gate, ffn_w_up, ffn_w_down):
    B, T, D = x.shape
    M = B * T
    depth = ada_w.shape[0]
    mods = _ada_call(c, ada_w, ada_b)
    mods3 = mods.reshape(2 * depth * B, 1, 3 * D)
    cos2, sin2 = _rope_call(positions)
    h = _modulate_call(x, mods3, 0)
    for layer in range(depth):
        j = layer // 2
        i_mix, i_ffn = 2 * layer, 2 * layer + 1
        if layer % 2 == 0:
            h2 = h.reshape(M, D)
            proj_a = _matmul_f32w_call(h2, ab_w_in, j, A_MAIN).reshape(B, T, A_MAIN)
            proj_b = _matmul_call(h2, _b_weight(ab_w_in[j]), tm=512, tn=B_COLS_PAD // 2, tk=D)
            proj_b = proj_b.reshape(B, T, B_COLS_PAD)
            ya = _mlstm_call(proj_a, proj_b, mlstm_conv_w[j], mlstm_gate_b[j], mlstm_norm_g[j],
                             gate_col=MLSTM_GATE_COL)
            yb = _rwkv_call(proj_b, rwkv_mu[j], rwkv_w0[j], rwkv_w2[j], rwkv_a0[j], rwkv_a2[j], rwkv_g2[j],
                            rwkv_k_k[j], rwkv_k_a[j], rwkv_r_k[j], rwkv_ln_g[j], rwkv_ln_b[j], col0=0)
            w_out = ab_w_out[j]
        else:
            h2 = h.reshape(M, D)
            proj = _matmul_f32w_call(h2, cd_w_in, j, CD_MAIN).reshape(B, T, CD_MAIN)
            w_gates = _pad_cols(cd_w_in[j][:, CD_MAIN:], LANES).astype(BF16)
            proj_g = _matmul_call(h2, w_gates, tm=512, tn=LANES, tk=D).reshape(B, T, LANES)
            ya = _ret_call(proj, cos2, sin2, ret_norm_g[j])
            yb = _gdn_call(proj, proj_g, gdn_conv_w[j], gdn_a_log[j], gdn_dt_bias[j], gdn_norm_g[j],
                           col0=GDN_COL0, gate_col=0)
            w_out = cd_w_out[j]
        w_out = w_out.astype(BF16)
        x, h = _proj_ln_call([ya, yb], [w_out[:D_GROUP], w_out[D_GROUP:]], x, mods3, i_mix,
                             ln_g[layer, 0], ln_b[layer, 0], with_next=True, tm=256, tk=D_GROUP)
        act = _swiglu_up_call(h.reshape(M, D), ffn_w_gate, ffn_w_up, layer)
        last = layer == depth - 1
        res = _proj_ln_call([act.reshape(B, T, D_FF)], [ffn_w_down[layer].astype(BF16)], x, mods3, i_ffn,
                            ln_g[layer, 1], ln_b[layer, 1], with_next=not last, tm=512, tk=1408)
        if last:
            (x,) = res
        else:
            x, h = res
    return x
```

```python
import functools
import math

import numpy as np
import jax
import jax.numpy as jnp
from jax import lax
from jax.experimental import pallas as pl
from jax.experimental.pallas import tpu as pltpu

F32 = jnp.float32
BF16 = jnp.bfloat16

D_MODEL = 2048
D_GROUP = 1024
HEAD_DIM = 128
N_HEADS = 8
RWKV_HEAD = 64
N_RWKV = 16
CHUNK = 64
D_FF = 5632
DEPTH = 2
ALPHA = (2 * DEPTH) ** 0.25
LN_EPS = 1e-5
RWKV_LN_EPS = 64e-5
ROPE_BASE = 10000.0
RET_GAMMA_BASE = 5.0
LANES = 128
VMEM_LIMIT = 48 * 1024 * 1024


def _dot(a, b):
    return jnp.dot(a.astype(BF16), b.astype(BF16), preferred_element_type=F32)


def _dot_nt(a, b):
    return lax.dot_general(a.astype(BF16), b.astype(BF16), (((1,), (1,)), ((), ())),
                           preferred_element_type=F32)


def _split3(x):
    hi = x.astype(BF16)
    r1 = x - hi.astype(F32)
    mid = r1.astype(BF16)
    lo = (r1 - mid.astype(F32)).astype(BF16)
    return hi, mid, lo


def _cumsum_rows(x):
    n = x.shape[1]
    out = jnp.dot(_tri(CHUNK).astype(BF16), jnp.concatenate(_split3(x), axis=1), preferred_element_type=F32)
    return out[:, :n] + out[:, n:2 * n] + out[:, 2 * n:]


def _group_sum(x, ones01):
    m = x.shape[0]
    hi = x.astype(BF16)
    lo = (x - hi.astype(F32)).astype(BF16)
    out = jnp.dot(jnp.concatenate([hi, lo], axis=0), ones01.astype(BF16), preferred_element_type=F32)
    return out[:m] + out[m:]


def _sigmoid(x):
    return 1.0 / (1.0 + jnp.exp(-x))


def _silu(x):
    return x * _sigmoid(x)


def _log_sigmoid(x):
    return jnp.minimum(x, 0.0) - jnp.log1p(jnp.exp(-jnp.abs(x)))


def _softplus(x):
    return jnp.maximum(x, 0.0) + jnp.log1p(jnp.exp(-jnp.abs(x)))


def _tri(n, strict=False):
    r = lax.broadcasted_iota(jnp.int32, (n, n), 0)
    c = lax.broadcasted_iota(jnp.int32, (n, n), 1)
    return (r > c) if strict else (r >= c)


def _conv_silu_rows(src_ref, halo_ref, w, first, r0, nrows, c0, ncols):
    cur = src_ref[0, r0:r0 + nrows, c0:c0 + ncols]
    acc = w[3:4] * cur
    if r0 == 0:
        hl = jnp.where(first, 0.0, halo_ref[0, :, c0:c0 + ncols])
        ext = jnp.concatenate([hl, cur[0:8]], axis=0)
        for j in range(3):
            head = ext[5 + j:13 + j]
            if nrows > 8:
                rest = src_ref[0, 5 + j:nrows - 3 + j, c0:c0 + ncols]
                sh = jnp.concatenate([head, rest], axis=0)
            else:
                sh = head
            acc = acc + w[j:j + 1] * sh
    else:
        for j in range(3):
            acc = acc + w[j:j + 1] * src_ref[0, r0 - 3 + j:r0 - 3 + j + nrows, c0:c0 + ncols]
    return _silu(acc)


def _head_norm_rows(h, g_row, eps, center=True):
    if center:
        h = h - jnp.mean(h, axis=-1, keepdims=True)
    return h * lax.rsqrt(jnp.mean(h * h, axis=-1, keepdims=True) + eps) * g_row


MLSTM_CHUNK_GROUP = 2


def _cummax_rows(x):
    row = lax.broadcasted_iota(jnp.int32, x.shape, 0)
    d = 1
    while d < x.shape[0]:
        x = jnp.where(row >= d, jnp.maximum(x, pltpu.roll(x, d, 0)), x)
        d *= 2
    return x


def _bcast_head_cols(x, sel, pieces):
    m = x.shape[0]
    parts, rest = [], x
    for _ in range(pieces):
        hi = rest.astype(BF16)
        parts.append(hi)
        rest = rest - hi.astype(F32)
    out = jnp.dot(jnp.concatenate(parts, axis=0), sel, preferred_element_type=F32)
    acc = out[:m]
    for i in range(1, pieces):
        acc = acc + out[i * m:(i + 1) * m]
    return acc


def _mlstm_kernel(q_ref, k_ref, v_ref, o_ref, g_ref, qh_ref, kh_ref, cw_ref, gb_ref, ng_ref,
                  out_ref, qc_ref, kc_ref, sv_ref, rs_ref, hh_ref, b0_ref, cm_ref, kvn_ref, CN_ref, m_ref, *, tb):
    t = pl.program_id(1)
    first = t == 0

    @pl.when(first)
    def _():
        CN_ref[...] = jnp.zeros_like(CN_ref)
        m_ref[...] = jnp.zeros_like(m_ref)

    for c in range(tb // CHUNK):
        for cb in range(D_GROUP // 256):
            cs = cb * 256
            qc_ref[c * CHUNK:(c + 1) * CHUNK, cs:cs + 256] = _conv_silu_rows(
                q_ref, qh_ref, cw_ref[:, cs:cs + 256], first, c * CHUNK, CHUNK, cs, 256)
            kc_ref[c * CHUNK:(c + 1) * CHUNK, cs:cs + 256] = _conv_silu_rows(
                k_ref, kh_ref, cw_ref[:, D_GROUP + cs:D_GROUP + cs + 256], first, c * CHUNK, CHUNK, cs, 256)

    causal = _tri(CHUNK)
    gb = gb_ref[...]
    scale = HEAD_DIM ** -0.5
    n_chunks = tb // CHUNK
    heads = range(N_HEADS)
    hs = [slice(h * HEAD_DIM, (h + 1) * HEAD_DIM) for h in heads]
    head_lane = lax.broadcasted_iota(jnp.int32, (1, LANES), 1) < N_HEADS
    sel = (lax.broadcasted_iota(jnp.int32, (LANES, N_HEADS * LANES), 0)
           == (lax.broadcasted_iota(jnp.int32, (LANES, N_HEADS * LANES), 1) >> 7)).astype(BF16)
    ones = jnp.ones((CHUNK, HEAD_DIM), F32)
    last = slice(CHUNK - 1, CHUNK)

    for c0 in range(0, n_chunks, MLSTM_CHUNK_GROUP):
        chunks = range(c0, c0 + MLSTM_CHUNK_GROUP)
        rows = {c: slice(c * CHUNK, (c + 1) * CHUNK) for c in chunks}
        z = {c: g_ref[0, rows[c], :] + gb for c in chunks}
        b0 = {c: pltpu.roll(_cumsum_rows(_log_sigmoid(z[c])), LANES - N_HEADS, 1) for c in chunks}
        cv = {c: jnp.where(head_lane, z[c] - b0[c], 0.0) for c in chunks}
        cm = {c: _cummax_rows(cv[c]) for c in chunks}
        cT = {c: cv[c].T for c in chunks}
        cmb = {c: _bcast_head_cols(cm[c], sel, 3) for c in chunks}
        e1b = {c: _bcast_head_cols(jnp.where(head_lane, jnp.exp(cv[c] - cm[c][last]), 0.0), sel, 2) for c in chunks}
        for c in chunks:
            b0_ref[rows[c], :] = b0[c]
            cm_ref[rows[c], :] = cm[c]
        items = [(c, h) for c in chunks for h in heads]
        n_items = range(len(items))
        q = [qc_ref[rows[c], hs[h]] for c, h in items]
        k = [kc_ref[rows[c], hs[h]] * scale for c, h in items]
        vo = [jnp.concatenate([v_ref[0, rows[c], hs[h]], ones], axis=1).astype(BF16) for c, h in items]
        qk = [_dot_nt(q[i], k[i]) for i in n_items]
        s = [qk[i] * jnp.where(causal, jnp.exp(jnp.minimum(cT[c][h:h + 1, :] - cmb[c][:, h * LANES:h * LANES + CHUNK],
                                                           0.0)), 0.0) for i, (c, h) in enumerate(items)]
        s_hi = [s[i].astype(BF16) for i in n_items]
        s_lo = [(s[i] - s_hi[i].astype(F32)).astype(BF16) for i in n_items]
        svr = [jnp.dot(jnp.concatenate([s_hi[i], s_lo[i]], axis=0), vo[i], preferred_element_type=F32)
               for i in n_items]
        kvn = [_dot((k[i] * e1b[c][:, hs[h]]).T, vo[i]) for i, (c, h) in enumerate(items)]
        for i, (c, h) in enumerate(items):
            sv_ref[rows[c], hs[h]] = svr[i][:CHUNK, :HEAD_DIM] + svr[i][CHUNK:, :HEAD_DIM]
            rs_ref[rows[c], hs[h]] = svr[i][:CHUNK, HEAD_DIM:] + svr[i][CHUNK:, HEAD_DIM:]
            kvn_ref[c * N_HEADS + h] = kvn[i]

    for c in range(n_chunks):
        rows = slice(c * CHUNK, (c + 1) * CHUNK)
        m = m_ref[0:1, :]
        b0 = b0_ref[rows, :]
        cm = cm_ref[rows, :]
        mx = jnp.maximum(cm, m)
        m_new = jnp.maximum(b0[last] + m, b0[last] + cm[last])
        m_ref[0:1, :] = m_new
        zero = lambda x: jnp.where(head_lane, x, 0.0)
        fib = _bcast_head_cols(zero(jnp.exp(cm - mx)), sel, 2)
        scb = _bcast_head_cols(zero(jnp.exp(m - mx)), sel, 2)
        emtb = _bcast_head_cols(zero(jnp.exp(jnp.minimum(-(b0 + mx), 80.0))), sel, 2)
        dfb = _bcast_head_cols(jnp.concatenate([zero(jnp.exp(b0[last] + m - m_new)),
                                                zero(jnp.exp(b0[last] + cm[last] - m_new)),
                                                jnp.zeros((6, LANES), F32)], axis=0), sel, 2)
        CN = [CN_ref[h] for h in heads]
        qcn = [_dot(qc_ref[rows, hs[h]], CN[h]) for h in heads]
        for h in heads:
            num = fib[:, hs[h]] * sv_ref[rows, hs[h]] + scb[:, hs[h]] * qcn[h][:, :HEAD_DIM]
            den = fib[:, hs[h]] * rs_ref[rows, hs[h]] + scb[:, hs[h]] * qcn[h][:, HEAD_DIM:]
            hh_ref[rows, hs[h]] = num / jnp.maximum(jnp.abs(den), emtb[:, hs[h]])
            dec = jnp.concatenate([dfb[0:1, hs[h]]] * 2, axis=1)
            fkv = jnp.concatenate([dfb[1:2, hs[h]]] * 2, axis=1)
            CN_ref[h] = CN[h] * dec + kvn_ref[c * N_HEADS + h] * fkv

    for c in range(n_chunks):
        rows = slice(c * CHUNK, (c + 1) * CHUNK)
        hn = [_head_norm_rows(hh_ref[rows, hs[h]], ng_ref[:, hs[h]], LN_EPS) for h in heads]
        for h in heads:
            out_ref[0, rows, hs[h]] = (hn[h] * _sigmoid(o_ref[0, rows, hs[h]])).astype(out_ref.dtype)


def _mlstm_call(proj, gproj, conv_w, gate_b, norm_g, *, gate_col, tb=256):
    B, T, _ = proj.shape
    nt = T // tb
    gb = jnp.zeros((1, LANES), F32).at[0, :2 * N_HEADS].set(gate_b)
    ng = norm_g.reshape(1, D_GROUP)
    colblk = lambda j: pl.BlockSpec((1, tb, D_GROUP), lambda b, t: (b, t, j))
    halo = lambda j: pl.BlockSpec((1, 8, D_GROUP), lambda b, t: (b, jnp.maximum(t * (tb // 8) - 1, 0), j))
    return pl.pallas_call(
        functools.partial(_mlstm_kernel, tb=tb),
        grid=(B, nt),
        in_specs=[colblk(0), colblk(1), colblk(2), colblk(3),
                  pl.BlockSpec((1, tb, LANES), lambda b, t: (b, t, gate_col // LANES)),
                  halo(0), halo(1),
                  pl.BlockSpec((4, 2 * D_GROUP), lambda b, t: (0, 0)),
                  pl.BlockSpec((1, LANES), lambda b, t: (0, 0)),
                  pl.BlockSpec((1, D_GROUP), lambda b, t: (0, 0))],
        out_specs=pl.BlockSpec((1, tb, D_GROUP), lambda b, t: (b, t, 0)),
        out_shape=jax.ShapeDtypeStruct((B, T, D_GROUP), BF16),
        scratch_shapes=[pltpu.VMEM((tb, D_GROUP), F32)] * 5
                       + [pltpu.VMEM((tb, LANES), F32)] * 2
                       + [pltpu.VMEM((tb // CHUNK * N_HEADS, HEAD_DIM, 2 * HEAD_DIM), F32),
                          pltpu.VMEM((N_HEADS, HEAD_DIM, 2 * HEAD_DIM), F32),
                          pltpu.VMEM((8, LANES), F32)],
        compiler_params=pltpu.CompilerParams(dimension_semantics=("arbitrary", "arbitrary"),
                                             vmem_limit_bytes=VMEM_LIMIT),
        name="mlstm",
    )(proj, proj, proj, proj, gproj, proj, proj, conv_w, gb, ng)


def _rope_kernel(pos_ref, inv_ref, cos_ref, sin_ref):
    ang = pos_ref[0].astype(F32) * inv_ref[...]
    lane = lax.broadcasted_iota(jnp.int32, ang.shape, 1)
    cos_ref[0] = jnp.cos(ang)
    sin_ref[0] = jnp.where(lane < HEAD_DIM // 2, -jnp.sin(ang), jnp.sin(ang))


def _rope_call(positions, *, tb=512):
    B, T = positions.shape
    half = HEAD_DIM // 2
    inv_freq = ROPE_BASE ** (-jnp.arange(half, dtype=F32) / half)
    inv2 = jnp.concatenate([inv_freq, inv_freq]).reshape(1, HEAD_DIM)
    spec = pl.BlockSpec((1, tb, HEAD_DIM), lambda b, t: (b, t, 0))
    return pl.pallas_call(
        _rope_kernel,
        grid=(B, T // tb),
        in_specs=[pl.BlockSpec((1, tb, 1), lambda b, t: (b, t, 0)),
                  pl.BlockSpec((1, HEAD_DIM), lambda b, t: (0, 0))],
        out_specs=[spec, spec],
        out_shape=[jax.ShapeDtypeStruct((B, T, HEAD_DIM), F32)] * 2,
        compiler_params=pltpu.CompilerParams(dimension_semantics=("arbitrary", "arbitrary")),
        name="rope_table",
    )(positions.reshape(B, T, 1), inv2)


def _ret_kernel(q_ref, k_ref, v_ref, g_ref, cos_ref, sin_ref, ng_ref, out_ref, R_ref, *, tb):
    t = pl.program_id(1)

    @pl.when(t == 0)
    def _():
        R_ref[...] = jnp.zeros_like(R_ref)

    causal = _tri(CHUNK)
    ri = lax.broadcasted_iota(jnp.int32, (CHUNK, CHUNK), 0)
    ci = lax.broadcasted_iota(jnp.int32, (CHUNK, CHUNK), 1)
    rel = (ri - ci).astype(F32)
    tcol = lax.broadcasted_iota(jnp.int32, (CHUNK, 1), 0).astype(F32)
    scale = HEAD_DIM ** -0.5

    def chunk_body(c, carry):
        r0 = pl.multiple_of(c * CHUNK, CHUNK)
        rows = pl.ds(r0, CHUNK)
        cos2 = cos_ref[0, rows, :]
        sin2 = sin_ref[0, rows, :]
        for h in range(N_HEADS):
            hs = slice(h * HEAD_DIM, (h + 1) * HEAD_DIM)
            lg = math.log1p(-2.0 ** (-RET_GAMMA_BASE - h))
            q = q_ref[0, rows, hs]
            k = k_ref[0, rows, hs]
            v = v_ref[0, rows, hs]
            qr = q * cos2 + pltpu.roll(q, HEAD_DIM // 2, 1) * sin2
            kr = (k * cos2 + pltpu.roll(k, HEAD_DIM // 2, 1) * sin2) * scale
            dmat = jnp.where(causal, jnp.exp(rel * lg), 0.0)
            xi = jnp.exp((tcol + 1.0) * lg)
            zeta = jnp.exp((CHUNK - 1.0 - tcol) * lg)
            Rst = R_ref[h]
            intra = _dot(_dot_nt(qr, kr) * dmat, v)
            inter = _dot(qr, Rst) * xi
            R_ref[h] = Rst * math.exp(CHUNK * lg) + _dot((kr * zeta).T, v)
            on = _head_norm_rows(intra + inter, ng_ref[:, hs], LN_EPS)
            out_ref[0, rows, hs] = (on * _silu(g_ref[0, rows, hs])).astype(out_ref.dtype)
        return carry

    lax.fori_loop(0, tb // CHUNK, chunk_body, 0)


def _ret_call(proj, cos2, sin2, norm_g, *, tb=256):
    B, T, _ = proj.shape
    colblk = lambda j: pl.BlockSpec((1, tb, D_GROUP), lambda b, t: (b, t, j))
    tab = pl.BlockSpec((1, tb, HEAD_DIM), lambda b, t: (b, t, 0))
    return pl.pallas_call(
        functools.partial(_ret_kernel, tb=tb),
        grid=(B, T // tb),
        in_specs=[colblk(0), colblk(1), colblk(2), colblk(3), tab, tab,
                  pl.BlockSpec((1, D_GROUP), lambda b, t: (0, 0))],
        out_specs=pl.BlockSpec((1, tb, D_GROUP), lambda b, t: (b, t, 0)),
        out_shape=jax.ShapeDtypeStruct((B, T, D_GROUP), BF16),
        scratch_shapes=[pltpu.VMEM((N_HEADS, HEAD_DIM, HEAD_DIM), F32)],
        compiler_params=pltpu.CompilerParams(dimension_semantics=("arbitrary", "arbitrary"),
                                             vmem_limit_bytes=VMEM_LIMIT),
        name="retention",
    )(proj, proj, proj, proj, cos2, sin2, norm_g.reshape(1, D_GROUP))


def _inv_unit_lower(nms):
    n = nms[0].shape[0]
    eye = (lax.broadcasted_iota(jnp.int32, (n, n), 0) == lax.broadcasted_iota(jnp.int32, (n, n), 1)).astype(F32)
    ps = [eye + nm for nm in nms]
    xs = [_dot(nm, nm) for nm in nms]
    for _ in range(int(math.log2(n)) - 2):
        px = [_dot(jnp.concatenate([p, x], axis=0), x) for p, x in zip(ps, xs)]
        ps = [p + y[:n] for p, y in zip(ps, px)]
        xs = [y[n:] for y in px]
    ps = [p + _dot(p, x) for p, x in zip(ps, xs)]
    resid = [eye - p + _dot(nm, p) for p, nm in zip(ps, nms)]
    return [p + _dot(p, r) for p, r in zip(ps, resid)]


def _inv_unit_lower_packed(nms, m0, m1):
    n = nms[0].shape[0]
    r = lax.broadcasted_iota(jnp.int32, (n, 2 * n), 0)
    c = lax.broadcasted_iota(jnp.int32, (n, 2 * n), 1)
    eye2 = (r == (c & (n - 1))).astype(F32)
    bd = lambda x: jnp.concatenate([x * m0, x * m1], axis=0)
    ps = [eye2 + nm for nm in nms]
    xs = [_dot(nm, bd(nm)) for nm in nms]
    for _ in range(int(math.log2(n)) - 2):
        px = [_dot(jnp.concatenate([p, x], axis=0), bd(x)) for p, x in zip(ps, xs)]
        ps = [p + y[:n] for p, y in zip(ps, px)]
        xs = [y[n:] for y in px]
    ps = [p + _dot(p, bd(x)) for p, x in zip(ps, xs)]
    resid = [eye2 - p + _dot(nm, bd(p)) for p, nm in zip(ps, nms)]
    return [p + _dot(p, bd(r_)) for p, r_ in zip(ps, resid)]


def _solve_unit_lower(nms, rhss):
    n = nms[0].shape[0]
    eye = (lax.broadcasted_iota(jnp.int32, (n, n), 0) == lax.broadcasted_iota(jnp.int32, (n, n), 1)).astype(F32)
    ps = [eye + nm for nm in nms]
    xs = [_dot(nm, nm) for nm in nms]
    for _ in range(int(math.log2(n)) - 2):
        px = [_dot(jnp.concatenate([p, x], axis=0), x) for p, x in zip(ps, xs)]
        ps = [p + y[:n] for p, y in zip(ps, px)]
        xs = [y[n:] for y in px]
    ps = [p + _dot(p, x) for p, x in zip(ps, xs)]
    x0 = [_dot(p, r) for p, r in zip(ps, rhss)]
    resid = [r - a + _dot(nm, a) for r, a, nm in zip(rhss, x0, nms)]
    return [a + _dot(p, r) for a, p, r in zip(x0, ps, resid)]


def _l2norm_rows(z):
    return z * lax.rsqrt(jnp.sum(z * z, axis=-1, keepdims=True) + 1e-6)


GDN_CHUNK_GROUP = 2


def _gdn_kernel(q_ref, k_ref, v_ref, z_ref, g_ref, qh_ref, kh_ref, vh_ref, cw_ref, an_ref, dt_ref, ng_ref,
                out_ref, qc_ref, kc_ref, vc_ref, u_ref, w_ref, qe_ref, o_ref, att_ref, kdT_ref, gl_ref, S_ref,
                *, tb):
    t = pl.program_id(1)
    first = t == 0

    @pl.when(first)
    def _():
        S_ref[...] = jnp.zeros_like(S_ref)

    srcs = ((q_ref, qh_ref, qc_ref), (k_ref, kh_ref, kc_ref), (v_ref, vh_ref, vc_ref))
    for c in range(tb // CHUNK):
        for cb in range(D_GROUP // 256):
            cs = cb * 256
            for i, (src, halo, dst) in enumerate(srcs):
                w = cw_ref[:, i * D_GROUP + cs:i * D_GROUP + cs + 256]
                dst[c * CHUNK:(c + 1) * CHUNK, cs:cs + 256] = _conv_silu_rows(
                    src, halo, w, first, c * CHUNK, CHUNK, cs, 256)

    causal = _tri(CHUNK)
    strict = _tri(CHUNK, strict=True)
    a_neg = an_ref[...]
    dtb = dt_ref[...]
    scale = HEAD_DIM ** -0.5
    n_chunks = tb // CHUNK
    heads = range(N_HEADS)
    hs = [slice(h * HEAD_DIM, (h + 1) * HEAD_DIM) for h in heads]

    for c0 in range(0, n_chunks, GDN_CHUNK_GROUP):
        chunks = range(c0, c0 + GDN_CHUNK_GROUP)
        rows = {c: slice(c * CHUNK, (c + 1) * CHUNK) for c in chunks}
        gz = {c: g_ref[0, rows[c], :] for c in chunks}
        beta = {c: _sigmoid(gz[c]) for c in chunks}
        gc = {c: _cumsum_rows(a_neg * _softplus(gz[c] + dtb)) for c in chunks}
        gcT = {c: gc[c].T for c in chunks}
        for c in chunks:
            gl_ref[c:c + 1, :] = gc[c][CHUNK - 1:CHUNK, :]
        items = [(c, h) for c in chunks for h in heads]
        n_items = range(len(items))
        gc_col = [gc[c][:, h:h + 1] for c, h in items]
        b_col = [beta[c][:, 8 + h:9 + h] for c, h in items]
        gamma = [jnp.where(causal, jnp.exp(gc_col[i] - gcT[c][h:h + 1, :]), 0.0) for i, (c, h) in enumerate(items)]
        q = [_l2norm_rows(qc_ref[rows[c], hs[h]]) * scale for c, h in items]
        k = [_l2norm_rows(kc_ref[rows[c], hs[h]]) for c, h in items]
        kb = [k[i] * b_col[i] for i in n_items]
        eg = [jnp.exp(gc_col[i]) for i in n_items]
        kq = [_dot_nt(jnp.concatenate([kb[i], q[i]], axis=0), k[i]) for i in n_items]
        inv = _inv_unit_lower([-jnp.where(strict, kq[i][:CHUNK] * gamma[i], 0.0) for i in n_items])
        uw = [_dot(inv[i], jnp.concatenate([vc_ref[rows[c], hs[h]] * b_col[i], kb[i] * eg[i]], axis=1))
              for i, (c, h) in enumerate(items)]
        for i, (c, h) in enumerate(items):
            u_ref[rows[c], hs[h]] = uw[i][:, :HEAD_DIM]
            w_ref[rows[c], hs[h]] = uw[i][:, HEAD_DIM:]
            qe_ref[rows[c], hs[h]] = q[i] * eg[i]
            att_ref[h, rows[c], :] = kq[i][CHUNK:] * gamma[i]
            g_last = gc[c][CHUNK - 1:CHUNK, h:h + 1]
            kdT_ref[c * N_HEADS + h] = (k[i] * jnp.exp(g_last - gc_col[i])).T

    for c in range(n_chunks):
        rows = slice(c * CHUNK, (c + 1) * CHUNK)
        S = [S_ref[h] for h in heads]
        ws = [_dot(jnp.concatenate([w_ref[rows, hs[h]], qe_ref[rows, hs[h]]], axis=0), S[h]) for h in heads]
        v_new = [u_ref[rows, hs[h]] - ws[h][:CHUNK] for h in heads]
        av = [_dot(att_ref[h, rows, :], v_new[h]) for h in heads]
        kv = [_dot(kdT_ref[c * N_HEADS + h], v_new[h]) for h in heads]
        for h in heads:
            S_ref[h] = S[h] * jnp.exp(gl_ref[c:c + 1, h:h + 1]) + kv[h]
            o_ref[rows, hs[h]] = ws[h][CHUNK:] + av[h]

    for c in range(n_chunks):
        rows = slice(c * CHUNK, (c + 1) * CHUNK)
        for h in heads:
            on = _head_norm_rows(o_ref[rows, hs[h]], ng_ref[:, hs[h]], 1e-6, center=False)
            out_ref[0, rows, hs[h]] = (on * _silu(z_ref[0, rows, hs[h]])).astype(out_ref.dtype)


def _gdn_call(proj, gproj, conv_w, a_log, dt_bias, norm_g, *, col0, gate_col, tb=256):
    B, T, _ = proj.shape
    j0 = col0 // D_GROUP
    an = jnp.zeros((1, LANES), F32).at[0, :N_HEADS].set(-jnp.exp(a_log.astype(F32)))
    dtb = jnp.zeros((1, LANES), F32).at[0, :N_HEADS].set(dt_bias)
    colblk = lambda j: pl.BlockSpec((1, tb, D_GROUP), lambda b, t: (b, t, j0 + j))
    halo = lambda j: pl.BlockSpec((1, 8, D_GROUP), lambda b, t: (b, jnp.maximum(t * (tb // 8) - 1, 0), j0 + j))
    return pl.pallas_call(
        functools.partial(_gdn_kernel, tb=tb),
        grid=(B, T // tb),
        in_specs=[colblk(0), colblk(1), colblk(2), colblk(3),
                  pl.BlockSpec((1, tb, LANES), lambda b, t: (b, t, gate_col // LANES)),
                  halo(0), halo(1), halo(2),
                  pl.BlockSpec((4, 3 * D_GROUP), lambda b, t: (0, 0)),
                  pl.BlockSpec((1, LANES), lambda b, t: (0, 0)),
                  pl.BlockSpec((1, LANES), lambda b, t: (0, 0)),
                  pl.BlockSpec((1, D_GROUP), lambda b, t: (0, 0))],
        out_specs=pl.BlockSpec((1, tb, D_GROUP), lambda b, t: (b, t, 0)),
        out_shape=jax.ShapeDtypeStruct((B, T, D_GROUP), BF16),
        scratch_shapes=[pltpu.VMEM((tb, D_GROUP), F32)] * 7
                       + [pltpu.VMEM((N_HEADS, tb, CHUNK), F32),
                          pltpu.VMEM((tb // CHUNK * N_HEADS, HEAD_DIM, CHUNK), F32),
                          pltpu.VMEM((max(tb // CHUNK, 8), LANES), F32),
                          pltpu.VMEM((N_HEADS, HEAD_DIM, HEAD_DIM), F32)],
        compiler_params=pltpu.CompilerParams(dimension_semantics=("arbitrary", "arbitrary"),
                                             vmem_limit_bytes=VMEM_LIMIT),
        name="gdn",
    )(proj, proj, proj, proj, gproj, proj, proj, proj, conv_w, an, dtb, norm_g.reshape(1, D_GROUP))


N_PAIRS = N_RWKV // 2
RWKV_LOW = 384


def _shift1_rows(src_ref, halo_ref, first, r0, nrows, c0, ncols):
    if r0 == 0:
        hl = jnp.where(first, 0.0, halo_ref[0, 7:8, c0:c0 + ncols])
        return jnp.concatenate([hl, src_ref[0, 0:nrows - 1, c0:c0 + ncols]], axis=0)
    return src_ref[0, r0 - 1:r0 - 1 + nrows, c0:c0 + ncols]


def _rwkv_kernel(r_ref, k_ref, v_ref, l0_ref, l1_ref, l2_ref,
                 rh_ref, kh_ref, vh_ref, l0h_ref, l1h_ref, l2h_ref,
                 mu_ref, mul_ref, w0_ref, w2_ref, a0_ref, a2_ref, g2_ref, kk_ref, ka_ref, rk_ref,
                 lng_ref, lnb_ref, out_ref,
                 gs_ref, bo_ref, y_ref, atrt_ref, avk_ref, yk_ref, inv_ref, arb_ref, btT_ref, kvT_ref, wlT_ref,
                 H_ref, *, tb):
    t = pl.program_id(1)
    first = t == 0

    @pl.when(first)
    def _():
        H_ref[...] = jnp.zeros_like(H_ref)

    ri = lax.broadcasted_iota(jnp.int32, (LANES, LANES), 0)
    ci = lax.broadcasted_iota(jnp.int32, (LANES, LANES), 1)
    same_head = ((ri // RWKV_HEAD) == (ci // RWKV_HEAD)).astype(F32)
    causal = _tri(CHUNK)
    strict = _tri(CHUNK, strict=True)
    lane1 = lax.broadcasted_iota(jnp.int32, (1, LANES), 1)
    m0 = (lane1 < RWKV_HEAD).astype(F32)
    m1 = 1.0 - m0
    t2 = lax.broadcasted_iota(jnp.int32, (CHUNK, LANES), 0)
    l2 = lax.broadcasted_iota(jnp.int32, (CHUNK, LANES), 1)
    lo = l2 < RWKV_HEAD
    s2 = l2 & (RWKV_HEAD - 1)
    causal2 = t2 >= s2
    strict2 = t2 > s2
    n_chunks = tb // CHUNK
    pairs = range(N_PAIRS)
    halves = [(p, hh) for p in pairs for hh in range(2)]
    ps = [slice(p * LANES, (p + 1) * LANES) for p in pairs]

    def lerp(src, halo, mu, r0, c0, ncols):
        cur = src[0, r0:r0 + CHUNK, c0:c0 + ncols]
        return cur + (_shift1_rows(src, halo, first, r0, CHUNK, c0, ncols) - cur) * mu

    for c in range(n_chunks):
        r0 = c * CHUNK
        rows = slice(r0, r0 + CHUNK)
        wl = lerp(l0_ref, l0h_ref, mul_ref[:, 0:LANES], r0, 0, LANES)
        g1 = lerp(l1_ref, l1h_ref, mul_ref[:, LANES:2 * LANES], r0, 0, LANES)
        g2 = lerp(l2_ref, l2h_ref, mul_ref[:, 2 * LANES:3 * LANES], r0, 0, LANES)
        wl_t = jnp.where(lo, jnp.tanh(wl), 0.0)
        al = jnp.where(lo, 0.0, wl)
        sg1 = _sigmoid(g1)
        sg2 = jnp.where(l2 < 32, _sigmoid(g2), 0.0)
        w = [-_softplus(-(w0_ref[:, ps[p]] + _dot(wl_t, w2_ref[:, ps[p]]))) - 0.5 for p in pairs]
        a = [_sigmoid(a0_ref[:, ps[p]] + _dot(al, a2_ref[:, ps[p]])) for p in pairs]
        g = [_dot(sg1, g2_ref[0:LANES, ps[p]]) + _dot(sg2, g2_ref[LANES:2 * LANES, ps[p]]) for p in pairs]
        r = [lerp(r_ref, rh_ref, mu_ref[:, ps[p]], r0, p * LANES, LANES) for p in pairs]
        k = [lerp(k_ref, kh_ref, mu_ref[:, D_GROUP + p * LANES:D_GROUP + (p + 1) * LANES], r0, p * LANES, LANES)
             for p in pairs]
        v = [lerp(v_ref, vh_ref, mu_ref[:, 2 * D_GROUP + p * LANES:2 * D_GROUP + (p + 1) * LANES], r0, p * LANES, LANES)
             for p in pairs]
        kk = [k[p] * kk_ref[:, ps[p]] for p in pairs]
        nrm = [jnp.sqrt(_group_sum(kk[p] * kk[p], same_head)) for p in pairs]
        kk = [kk[p] / jnp.maximum(nrm[p], 1e-12) for p in pairs]
        k2 = [k[p] * (1.0 + (a[p] - 1.0) * ka_ref[:, ps[p]]) for p in pairs]
        rk = [_group_sum(r[p] * k2[p] * rk_ref[:, ps[p]], same_head) for p in pairs]
        for p in pairs:
            gs_ref[p, rows, :] = g[p]
            bo_ref[p, rows, :] = rk[p] * v[p]
        lw = [-jnp.exp(w[p]) for p in pairs]
        cs = [_cumsum_rows(lw[p]) for p in pairs]
        w_inv = [jnp.exp(-cs[p]) for p in pairs]
        w_end = [jnp.exp(cs[p][CHUNK - 1:CHUNK, :]) for p in pairs]
        rt = [r[p] * jnp.exp(cs[p]) for p in pairs]
        at = [-kk[p] * jnp.exp(cs[p] - lw[p]) for p in pairs]
        bt = [kk[p] * a[p] * w_inv[p] for p in pairs]
        kt = [k2[p] * w_inv[p] for p in pairs]
        pm0 = [_dot_nt(jnp.concatenate([at[p] * m0, rt[p] * m0], axis=0), jnp.concatenate([bt[p], kt[p]], axis=0))
               for p in pairs]
        pm1 = [_dot_nt(jnp.concatenate([at[p] * m1, rt[p] * m1], axis=0), jnp.concatenate([kt[p], bt[p]], axis=0))
               for p in pairs]
        n_ab = [jnp.where(strict2, jnp.where(lo, pm0[p][:CHUNK], pm1[p][:CHUNK]), 0.0) for p in pairs]
        a_rb = [jnp.where(causal2, jnp.where(lo, pm0[p][CHUNK:], pm1[p][CHUNK:]), 0.0) for p in pairs]
        akrk = [jnp.concatenate([jnp.where(strict2, jnp.where(lo, pm1[p][:CHUNK], pm0[p][:CHUNK]), 0.0),
                                 jnp.where(causal2, jnp.where(lo, pm1[p][CHUNK:], pm0[p][CHUNK:]), 0.0)], axis=0)
                for p in pairs]
        vk = [_dot(akrk[p], jnp.concatenate([v[p] * m1, v[p] * m0], axis=0)) for p in pairs]
        inv = _inv_unit_lower_packed(n_ab, m0, m1)
        kv = [_dot((kt[p] * w_end[p]).T, v[p]) * same_head for p in pairs]
        for p in pairs:
            i = c * N_PAIRS + p
            atrt_ref[i] = jnp.concatenate([at[p], rt[p]], axis=0).astype(BF16)
            avk_ref[i] = vk[p][:CHUNK]
            yk_ref[i] = vk[p][CHUNK:]
            btT_ref[i] = (bt[p] * w_end[p]).T.astype(BF16)
            kvT_ref[i] = kv[p]
            wlT_ref[i] = jnp.broadcast_to(w_end[p], (LANES, LANES)).T
            inv_ref[i] = inv[p].astype(BF16)
            arb_ref[i] = a_rb[p].astype(BF16)

    for c in range(n_chunks):
        rows = slice(c * CHUNK, (c + 1) * CHUNK)
        it = [c * N_PAIRS + p for p in pairs]
        H = [H_ref[p] for p in pairs]
        xy0 = [_dot(atrt_ref[it[p]], H[p]) for p in pairs]
        x = [xy0[p][:CHUNK] + avk_ref[it[p]] for p in pairs]
        u = [_dot(inv_ref[it[p]], jnp.concatenate([x[p] * m0, x[p] * m1], axis=0)) for p in pairs]
        yb = [_dot(arb_ref[it[p]], jnp.concatenate([u[p] * m0, u[p] * m1], axis=0)) for p in pairs]
        bu = [_dot(btT_ref[it[p]], u[p]) for p in pairs]
        for p in pairs:
            y_ref[p, rows, :] = xy0[p][CHUNK:] + yk_ref[it[p]] + yb[p]
            H_ref[p] = H[p] * wlT_ref[it[p]] + bu[p] * same_head + kvT_ref[it[p]]

    inv_n = 1.0 / RWKV_HEAD
    for c in range(n_chunks):
        rows = slice(c * CHUNK, (c + 1) * CHUNK)
        y = [y_ref[p, rows, :] for p in pairs]
        yc = [y[p] - _group_sum(y[p], same_head) * inv_n for p in pairs]
        var = [_group_sum(yc[p] * yc[p], same_head) * inv_n for p in pairs]
        for p in pairs:
            yn = yc[p] * lax.rsqrt(var[p] + RWKV_LN_EPS) * lng_ref[:, ps[p]] + lnb_ref[:, ps[p]]
            out_ref[0, rows, ps[p]] = ((yn + bo_ref[p, rows, :]) * gs_ref[p, rows, :]).astype(out_ref.dtype)


def _rwkv_call(proj, mu, w0, w2, a0, a2, g2, k_k, k_a, r_k, ln_g, ln_b, *, col0, tb=256):
    B, T, _ = proj.shape
    j0 = col0 // D_GROUP
    l0 = (col0 + 3 * D_GROUP) // LANES
    row = lambda a: a.reshape(1, -1).astype(F32)
    mul = jnp.zeros((1, RWKV_LOW), F32).at[0, :288].set(mu[3 * D_GROUP:])
    w2p = jnp.zeros((LANES, D_GROUP), F32).at[:64].set(w2)
    a2p = jnp.zeros((LANES, D_GROUP), F32).at[64:].set(a2)
    g2p = jnp.zeros((2 * LANES, D_GROUP), F32).at[:160].set(g2)
    colblk = lambda j: pl.BlockSpec((1, tb, D_GROUP), lambda b, t: (b, t, j0 + j))
    lowblk = lambda j: pl.BlockSpec((1, tb, LANES), lambda b, t: (b, t, l0 + j))
    hrow = lambda t: jnp.maximum(t * (tb // 8) - 1, 0)
    halo = lambda j: pl.BlockSpec((1, 8, D_GROUP), lambda b, t: (b, hrow(t), j0 + j))
    lowhalo = lambda j: pl.BlockSpec((1, 8, LANES), lambda b, t: (b, hrow(t), l0 + j))
    full = lambda a: pl.BlockSpec(a.shape, lambda b, t: (0,) * a.ndim)
    params = [row(mu[:3 * D_GROUP]), mul, row(w0), w2p, row(a0), a2p, g2p, row(k_k), row(k_a), row(r_k),
              row(ln_g), row(ln_b)]
    big = pltpu.VMEM((N_PAIRS, tb, LANES), F32)
    n_items = tb // CHUNK * N_PAIRS
    return pl.pallas_call(
        functools.partial(_rwkv_kernel, tb=tb),
        grid=(B, T // tb),
        in_specs=[colblk(0), colblk(1), colblk(2), lowblk(0), lowblk(1), lowblk(2),
                  halo(0), halo(1), halo(2), lowhalo(0), lowhalo(1), lowhalo(2)] + [full(a) for a in params],
        out_specs=pl.BlockSpec((1, tb, D_GROUP), lambda b, t: (b, t, 0)),
        out_shape=jax.ShapeDtypeStruct((B, T, D_GROUP), BF16),
        scratch_shapes=[big] * 3 + [
            pltpu.VMEM((n_items, 2 * CHUNK, LANES), BF16),
            pltpu.VMEM((n_items, CHUNK, LANES), F32),
            pltpu.VMEM((n_items, CHUNK, LANES), F32),
            pltpu.VMEM((n_items, CHUNK, LANES), BF16),
            pltpu.VMEM((n_items, CHUNK, LANES), BF16),
            pltpu.VMEM((n_items, LANES, CHUNK), BF16),
            pltpu.VMEM((n_items, LANES, LANES), F32),
            pltpu.VMEM((n_items, LANES, LANES), F32),
            pltpu.VMEM((N_PAIRS, LANES, LANES), F32)],
        compiler_params=pltpu.CompilerParams(dimension_semantics=("arbitrary", "arbitrary"),
                                             vmem_limit_bytes=VMEM_LIMIT),
        name="rwkv7",
    )(*([proj] * 12), *params)


def _ada_kernel(c_ref, w_ref, b_ref, out_ref):
    sc = _silu(c_ref[...]).astype(BF16)
    out_ref[0] = jnp.dot(sc, w_ref[0].astype(BF16), preferred_element_type=F32) + b_ref[0]


def _ada_call(c, ada_w, ada_b, *, tn=1536):
    B = c.shape[0]
    n_mod = ada_w.shape[0] * ada_w.shape[1]
    w = ada_w.reshape(n_mod, D_MODEL, 3 * D_MODEL)
    b = ada_b.reshape(n_mod, 1, 3 * D_MODEL)
    return pl.pallas_call(
        _ada_kernel,
        grid=(n_mod, 3 * D_MODEL // tn),
        in_specs=[pl.BlockSpec((B, D_MODEL), lambda i, j: (0, 0)),
                  pl.BlockSpec((1, D_MODEL, tn), lambda i, j: (i, 0, j)),
                  pl.BlockSpec((1, 1, tn), lambda i, j: (i, 0, j))],
        out_specs=pl.BlockSpec((1, B, tn), lambda i, j: (i, 0, j)),
        out_shape=jax.ShapeDtypeStruct((n_mod, B, 3 * D_MODEL), F32),
        compiler_params=pltpu.CompilerParams(dimension_semantics=("arbitrary", "arbitrary"),
                                             vmem_limit_bytes=VMEM_LIMIT),
        name="adaln",
    )(c, w, b)


def _mod_spec(i, part, nb):
    return pl.BlockSpec((1, 1, D_MODEL), lambda b, t: (i * nb + b, 0, part))


def _modulate_kernel(x_ref, shift_ref, scale_ref, h_ref):
    h_ref[0] = (x_ref[0] * (1.0 + scale_ref[0]) + shift_ref[0]).astype(h_ref.dtype)


def _modulate_call(x, mods3, i, *, tb=512):
    B, T, _ = x.shape
    blk = pl.BlockSpec((1, tb, D_MODEL), lambda b, t: (b, t, 0))
    return pl.pallas_call(
        _modulate_kernel,
        grid=(B, T // tb),
        in_specs=[blk, _mod_spec(i, 0, B), _mod_spec(i, 1, B)],
        out_specs=blk,
        out_shape=jax.ShapeDtypeStruct(x.shape, BF16),
        compiler_params=pltpu.CompilerParams(dimension_semantics=("arbitrary", "arbitrary"),
                                             vmem_limit_bytes=VMEM_LIMIT),
        name="modulate",
    )(x, mods3, mods3)


LN_ROWS = 16


def _proj_ln_kernel(*refs, n_lhs, nk, tm, with_next):
    lhs = refs[:n_lhs]
    ws = refs[n_lhs:2 * n_lhs]
    x_ref, gate_ref, g_ref, b_ref = refs[2 * n_lhs:2 * n_lhs + 4]
    if with_next:
        shift_ref, scale_ref, xo_ref, h_ref, acc_ref = refs[2 * n_lhs + 4:]
    else:
        xo_ref, acc_ref = refs[2 * n_lhs + 4:]
    k = pl.program_id(2)
    def partial_product():
        part = jnp.dot(lhs[0][0], ws[0][...], preferred_element_type=F32)
        for j in range(1, n_lhs):
            part = part + jnp.dot(lhs[j][0], ws[j][...], preferred_element_type=F32)
        return part

    @pl.when(k == 0)
    def _():
        acc_ref[...] = partial_product()

    @pl.when(k > 0)
    def _():
        acc_ref[...] += partial_product()

    @pl.when(k == nk - 1)
    def _():
        gate1 = 1.0 + gate_ref[0]
        g, b = g_ref[...], b_ref[...]
        if with_next:
            scale1, shift = 1.0 + scale_ref[0], shift_ref[0]

        def rows_body(r, carry):
            rows = pl.ds(pl.multiple_of(r * LN_ROWS, LN_ROWS), LN_ROWS)
            z = ALPHA * x_ref[0, rows, :] + gate1 * acc_ref[rows, :]
            zc = z - jnp.mean(z, axis=-1, keepdims=True)
            var = jnp.mean(zc * zc, axis=-1, keepdims=True)
            xn = zc * lax.rsqrt(var + LN_EPS) * g + b
            xo_ref[0, rows, :] = xn
            if with_next:
                h_ref[0, rows, :] = (xn * scale1 + shift).astype(h_ref.dtype)
            return carry

        lax.fori_loop(0, tm // LN_ROWS, rows_body, 0, unroll=8)


def _proj_ln_call(lhs, ws, x, mods3, i, g, b, *, with_next, tm, tk):
    B, T, D = x.shape
    n_lhs = len(lhs)
    nk = lhs[0].shape[2] // tk
    blk = pl.BlockSpec((1, tm, D), lambda b, t, k: (b, t, 0))
    row = pl.BlockSpec((1, D), lambda b, t, k: (0, 0))
    mod = lambda ii, part: pl.BlockSpec((1, 1, D), lambda b, t, k: (ii * B + b, 0, part))
    in_specs = ([pl.BlockSpec((1, tm, tk), lambda b, t, k: (b, t, k))] * n_lhs
                + [pl.BlockSpec((tk, D), lambda b, t, k: (k, 0))] * n_lhs
                + [blk, mod(i, 2), row, row])
    args = list(lhs) + list(ws) + [x, mods3, g.reshape(1, D), b.reshape(1, D)]
    out_specs = [blk]
    out_shape = [jax.ShapeDtypeStruct(x.shape, F32)]
    if with_next:
        in_specs += [mod(i + 1, 0), mod(i + 1, 1)]
        args += [mods3, mods3]
        out_specs.append(blk)
        out_shape.append(jax.ShapeDtypeStruct(x.shape, BF16))
    return pl.pallas_call(
        functools.partial(_proj_ln_kernel, n_lhs=n_lhs, nk=nk, tm=tm, with_next=with_next),
        grid=(B, T // tm, nk),
        in_specs=in_specs, out_specs=out_specs, out_shape=out_shape,
        scratch_shapes=[pltpu.VMEM((tm, D), F32)],
        compiler_params=pltpu.CompilerParams(dimension_semantics=("arbitrary", "arbitrary", "arbitrary"),
                                             vmem_limit_bytes=VMEM_LIMIT),
        name="proj_residual_ln",
    )(*args)


def _matmul_kernel(a_ref, b_ref, o_ref, *scratch, nk):
    if nk == 1:
        o_ref[...] = jnp.dot(a_ref[...], b_ref[...], preferred_element_type=F32).astype(o_ref.dtype)
        return
    (acc_ref,) = scratch
    k = pl.program_id(2)

    @pl.when(k == 0)
    def _():
        acc_ref[...] = jnp.zeros_like(acc_ref)

    acc_ref[...] += jnp.dot(a_ref[...], b_ref[...], preferred_element_type=F32)

    @pl.when(k == nk - 1)
    def _():
        o_ref[...] = acc_ref[...].astype(o_ref.dtype)


def _matmul_call(a, b, *, tm, tn, tk, out_dtype=F32):
    M, K = a.shape
    _, N = b.shape
    nk = K // tk
    return pl.pallas_call(
        functools.partial(_matmul_kernel, nk=nk),
        grid=(N // tn, M // tm, nk),
        in_specs=[pl.BlockSpec((tm, tk), lambda j, i, k: (i, k)),
                  pl.BlockSpec((tk, tn), lambda j, i, k: (k, j))],
        out_specs=pl.BlockSpec((tm, tn), lambda j, i, k: (i, j)),
        out_shape=jax.ShapeDtypeStruct((M, N), out_dtype),
        scratch_shapes=[] if nk == 1 else [pltpu.VMEM((tm, tn), F32)],
        compiler_params=pltpu.CompilerParams(dimension_semantics=("arbitrary", "arbitrary", "arbitrary"),
                                             vmem_limit_bytes=VMEM_LIMIT),
        name="matmul",
    )(a, b)


XPOSE_ROWS = 256


def _matmul_f32wt_kernel(a_ref, wt_ref, o_ref, wb_ref):
    @pl.when(pl.program_id(1) == 0)
    def _():
        for r in range(0, wt_ref.shape[0], XPOSE_ROWS):
            wb_ref[:, r:r + XPOSE_ROWS] = wt_ref[r:r + XPOSE_ROWS, :].T.astype(BF16)

    o_ref[...] = jnp.dot(a_ref[...], wb_ref[...], preferred_element_type=F32)


def _matmul_f32wt_call(a, wt, idx, ncols, *, tm=512, tn=1024):
    M, K = a.shape
    return pl.pallas_call(
        _matmul_f32wt_kernel,
        grid=(ncols // tn, M // tm),
        in_specs=[pl.BlockSpec((tm, K), lambda j, i: (i, 0)),
                  pl.BlockSpec((None, tn, K), lambda j, i: (idx, j, 0))],
        out_specs=pl.BlockSpec((tm, tn), lambda j, i: (i, j)),
        out_shape=jax.ShapeDtypeStruct((M, ncols), F32),
        scratch_shapes=[pltpu.VMEM((K, tn), BF16)],
        compiler_params=pltpu.CompilerParams(dimension_semantics=("arbitrary", "arbitrary"),
                                             vmem_limit_bytes=VMEM_LIMIT),
        name="matmul_f32wt",
    )(a, wt)


def _regroup_rows_kernel(wt_ref, o_ref, *, segments):
    tc = wt_ref.shape[1]
    pieces, pos = [], 0
    for src, width, dst in segments:
        if dst > pos:
            pieces.append(jnp.zeros((dst - pos, tc), F32))
        pieces.append(wt_ref[src:src + width, :])
        pos = dst + width
    if pos < o_ref.shape[1]:
        pieces.append(jnp.zeros((o_ref.shape[1] - pos, tc), F32))
    stacked = jnp.concatenate(pieces, axis=0)
    for r in range(0, o_ref.shape[1], LANES):
        o_ref[:, r:r + LANES] = stacked[r:r + LANES, :].T.astype(o_ref.dtype)


def _regroup_rows_call(wt, idx, segments, out_cols, *, block_rows, block_index, tc):
    _, _, K = wt.shape
    return pl.pallas_call(
        functools.partial(_regroup_rows_kernel, segments=segments),
        grid=(K // tc,),
        in_specs=[pl.BlockSpec((None, block_rows, tc), lambda c: (idx, block_index, c))],
        out_specs=pl.BlockSpec((tc, out_cols), lambda c: (c, 0)),
        out_shape=jax.ShapeDtypeStruct((K, out_cols), BF16),
        compiler_params=pltpu.CompilerParams(dimension_semantics=("arbitrary",), vmem_limit_bytes=VMEM_LIMIT),
        name="regroup_rows",
    )(wt)


def _swiglu_up_kernel(h_ref, wg_ref, wu_ref, o_ref, wgb_ref, wub_ref):
    @pl.when(pl.program_id(1) == 0)
    def _():
        wgb_ref[...] = wg_ref[...].astype(BF16)
        wub_ref[...] = wu_ref[...].astype(BF16)

    h = h_ref[...]
    g = jnp.dot(h, wgb_ref[...], preferred_element_type=F32)
    u = jnp.dot(h, wub_ref[...], preferred_element_type=F32)
    o_ref[...] = (_silu(g) * u).astype(o_ref.dtype)


def _swiglu_up_call(h, wg, wu, layer, *, tm=512, tn=512):
    M, K = h.shape
    _, _, N = wg.shape
    wspec = pl.BlockSpec((None, K, tn), lambda j, i: (layer, 0, j))
    return pl.pallas_call(
        _swiglu_up_kernel,
        grid=(N // tn, M // tm),
        in_specs=[pl.BlockSpec((tm, K), lambda j, i: (i, 0)), wspec, wspec],
        out_specs=pl.BlockSpec((tm, tn), lambda j, i: (i, j)),
        out_shape=jax.ShapeDtypeStruct((M, N), BF16),
        scratch_shapes=[pltpu.VMEM((K, tn), BF16), pltpu.VMEM((K, tn), BF16)],
        compiler_params=pltpu.CompilerParams(dimension_semantics=("arbitrary", "arbitrary"),
                                             vmem_limit_bytes=VMEM_LIMIT),
        name="swiglu_up",
    )(h, wg, wu)


A_MAIN = 4 * D_GROUP
B_COLS_PAD = 3 * D_GROUP + RWKV_LOW + LANES
MLSTM_GATE_COL = 3 * D_GROUP + RWKV_LOW
CD_MAIN = 8 * D_GROUP
GDN_COL0 = 4 * D_GROUP


N_GATES = 2 * N_HEADS
B_SRC = A_MAIN + N_GATES
B_LOW = 64 + 64 + 160
B_SEGMENTS = ((B_SRC, 3 * D_GROUP, 0), (B_SRC + 3 * D_GROUP, B_LOW, 3 * D_GROUP), (A_MAIN, N_GATES, MLSTM_GATE_COL))
CD_GATE_SEGMENTS = ((0, N_GATES, 0),)


def kernel(x, c, positions, ada_w, ada_b, ln_g, ln_b, ab_w_in, ab_w_out, mlstm_conv_w, mlstm_gate_b, mlstm_norm_g, rwkv_mu, rwkv_w0, rwkv_w2, rwkv_a0, rwkv_a2, rwkv_g2, rwkv_k_k, rwkv_k_a, rwkv_r_k, rwkv_ln_g, rwkv_ln_b, cd_w_in, cd_w_out, ret_norm_g, gdn_conv_w, gdn_a_log, gdn_dt_bias, gdn_norm_g, ffn_w_gate, ffn_w_up, ffn_w_down):
    B, T, D = x.shape
    M = B * T
    depth = ada_w.shape[0]
    mods = _ada_call(c, ada_w, ada_b)
    mods3 = mods.reshape(2 * depth * B, 1, 3 * D)
    cos2, sin2 = _rope_call(positions)
    h = _modulate_call(x, mods3, 0)
    for layer in range(depth):
        j = layer // 2
        i_mix, i_ffn = 2 * layer, 2 * layer + 1
        if layer % 2 == 0:
            h2 = h.reshape(M, D)
            wt = jnp.swapaxes(ab_w_in, 1, 2)
            proj_a = _matmul_f32wt_call(h2, wt, j, A_MAIN).reshape(B, T, A_MAIN)
            w_b = _regroup_rows_call(wt, j, B_SEGMENTS, B_COLS_PAD, block_rows=wt.shape[1], block_index=0, tc=256)
            proj_b = _matmul_call(h2, w_b, tm=512, tn=B_COLS_PAD // 2, tk=D)
            proj_b = proj_b.reshape(B, T, B_COLS_PAD)
            ya = _mlstm_call(proj_a, proj_b, mlstm_conv_w[j], mlstm_gate_b[j], mlstm_norm_g[j],
                             gate_col=MLSTM_GATE_COL)
            yb = _rwkv_call(proj_b, rwkv_mu[j], rwkv_w0[j], rwkv_w2[j], rwkv_a0[j], rwkv_a2[j], rwkv_g2[j],
                            rwkv_k_k[j], rwkv_k_a[j], rwkv_r_k[j], rwkv_ln_g[j], rwkv_ln_b[j], col0=0)
            w_out = ab_w_out[j]
        else:
            h2 = h.reshape(M, D)
            wt = jnp.swapaxes(cd_w_in, 1, 2)
            proj = _matmul_f32wt_call(h2, wt, j, CD_MAIN).reshape(B, T, CD_MAIN)
            w_gates = _regroup_rows_call(wt, j, CD_GATE_SEGMENTS, LANES, block_rows=N_GATES,
                                         block_index=CD_MAIN // N_GATES, tc=D)
            proj_g = _matmul_call(h2, w_gates, tm=512, tn=LANES, tk=D).reshape(B, T, LANES)
            ya = _ret_call(proj, cos2, sin2, ret_norm_g[j])
            yb = _gdn_call(proj, proj_g, gdn_conv_w[j], gdn_a_log[j], gdn_dt_bias[j], gdn_norm_g[j],
                           col0=GDN_COL0, gate_col=0)
            w_out = cd_w_out[j]
        w_out = w_out.astype(BF16)
        x, h = _proj_ln_call([ya, yb], [w_out[:D_GROUP], w_out[D_GROUP:]], x, mods3, i_mix,
                             ln_g[layer, 0], ln_b[layer, 0], with_next=True, tm=256, tk=D_GROUP)
        act = _swiglu_up_call(h.reshape(M, D), ffn_w_gate, ffn_w_up, layer)
        last = layer == depth - 1
        res = _proj_ln_call([act.reshape(B, T, D_FF)], [ffn_w_down[layer].astype(BF16)], x, mods3, i_ffn,
                            ln_g[layer, 1], ln_b[layer, 1], with_next=not last, tm=512, tk=1408)
        if last:
            (x,) = res
        else:
            x, h = res
    return x
```

```python
import functools
import math

import numpy as np
import jax
import jax.numpy as jnp
from jax import lax
from jax.experimental import pallas as pl
from jax.experimental.pallas import tpu as pltpu

F32 = jnp.float32
BF16 = jnp.bfloat16

D_MODEL = 2048
D_GROUP = 1024
HEAD_DIM = 128
N_HEADS = 8
RWKV_HEAD = 64
N_RWKV = 16
CHUNK = 64
D_FF = 5632
DEPTH = 2
ALPHA = (2 * DEPTH) ** 0.25
LN_EPS = 1e-5
RWKV_LN_EPS = 64e-5
ROPE_BASE = 10000.0
RET_GAMMA_BASE = 5.0
LANES = 128
VMEM_LIMIT = 48 * 1024 * 1024


def _dot(a, b):
    return jnp.dot(a.astype(BF16), b.astype(BF16), preferred_element_type=F32)


def _dot_nt(a, b):
    return lax.dot_general(a.astype(BF16), b.astype(BF16), (((1,), (1,)), ((), ())),
                           preferred_element_type=F32)


def _split3(x):
    hi = x.astype(BF16)
    r1 = x - hi.astype(F32)
    mid = r1.astype(BF16)
    lo = (r1 - mid.astype(F32)).astype(BF16)
    return hi, mid, lo


def _cumsum_rows(x):
    n = x.shape[1]
    out = jnp.dot(_tri(CHUNK).astype(BF16), jnp.concatenate(_split3(x), axis=1), preferred_element_type=F32)
    return out[:, :n] + out[:, n:2 * n] + out[:, 2 * n:]


def _half_lane_sums(x, m0, lo):
    s0 = jnp.sum(x * m0, axis=-1, keepdims=True)
    s1 = jnp.sum(x * (1.0 - m0), axis=-1, keepdims=True)
    return jnp.where(lo, s0, s1)


def _sigmoid(x):
    return 1.0 / (1.0 + jnp.exp(-x))


def _silu(x):
    return x * _sigmoid(x)


def _log_sigmoid(x):
    return jnp.minimum(x, 0.0) - jnp.log1p(jnp.exp(-jnp.abs(x)))


def _softplus(x):
    return jnp.maximum(x, 0.0) + jnp.log1p(jnp.exp(-jnp.abs(x)))


def _tri(n, strict=False):
    r = lax.broadcasted_iota(jnp.int32, (n, n), 0)
    c = lax.broadcasted_iota(jnp.int32, (n, n), 1)
    return (r > c) if strict else (r >= c)


def _conv_silu_rows(src_ref, halo_ref, w, first, r0, nrows, c0, ncols):
    cur = src_ref[0, r0:r0 + nrows, c0:c0 + ncols]
    acc = w[3:4] * cur
    if r0 == 0:
        hl = jnp.where(first, 0.0, halo_ref[0, :, c0:c0 + ncols])
        ext = jnp.concatenate([hl, cur[0:8]], axis=0)
        for j in range(3):
            head = ext[5 + j:13 + j]
            if nrows > 8:
                rest = src_ref[0, 5 + j:nrows - 3 + j, c0:c0 + ncols]
                sh = jnp.concatenate([head, rest], axis=0)
            else:
                sh = head
            acc = acc + w[j:j + 1] * sh
    else:
        for j in range(3):
            acc = acc + w[j:j + 1] * src_ref[0, r0 - 3 + j:r0 - 3 + j + nrows, c0:c0 + ncols]
    return _silu(acc)


def _head_norm_rows(h, g_row, eps, center=True):
    if center:
        h = h - jnp.mean(h, axis=-1, keepdims=True)
    return h * lax.rsqrt(jnp.mean(h * h, axis=-1, keepdims=True) + eps) * g_row


MLSTM_CHUNK_GROUP = 2


def _cummax_rows(x):
    row = lax.broadcasted_iota(jnp.int32, x.shape, 0)
    d = 1
    while d < x.shape[0]:
        x = jnp.where(row >= d, jnp.maximum(x, pltpu.roll(x, d, 0)), x)
        d *= 2
    return x


def _bcast_head_cols(x, sel, pieces):
    m = x.shape[0]
    parts, rest = [], x
    for _ in range(pieces):
        hi = rest.astype(BF16)
        parts.append(hi)
        rest = rest - hi.astype(F32)
    out = jnp.dot(jnp.concatenate(parts, axis=0), sel, preferred_element_type=F32)
    acc = out[:m]
    for i in range(1, pieces):
        acc = acc + out[i * m:(i + 1) * m]
    return acc


def _mlstm_kernel(q_ref, k_ref, v_ref, o_ref, g_ref, qh_ref, kh_ref, cw_ref, gb_ref, ng_ref,
                  out_ref, qc_ref, kc_ref, sv_ref, rs_ref, hh_ref, b0_ref, cm_ref, kvn_ref, CN_ref, m_ref, *, tb):
    t = pl.program_id(1)
    first = t == 0

    @pl.when(first)
    def _():
        CN_ref[...] = jnp.zeros_like(CN_ref)
        m_ref[...] = jnp.zeros_like(m_ref)

    for c in range(tb // CHUNK):
        for cb in range(D_GROUP // 256):
            cs = cb * 256
            qc_ref[c * CHUNK:(c + 1) * CHUNK, cs:cs + 256] = _conv_silu_rows(
                q_ref, qh_ref, cw_ref[:, cs:cs + 256], first, c * CHUNK, CHUNK, cs, 256)
            kc_ref[c * CHUNK:(c + 1) * CHUNK, cs:cs + 256] = _conv_silu_rows(
                k_ref, kh_ref, cw_ref[:, D_GROUP + cs:D_GROUP + cs + 256], first, c * CHUNK, CHUNK, cs, 256)

    causal = _tri(CHUNK)
    gb = gb_ref[...]
    scale = HEAD_DIM ** -0.5
    n_chunks = tb // CHUNK
    heads = range(N_HEADS)
    hs = [slice(h * HEAD_DIM, (h + 1) * HEAD_DIM) for h in heads]
    head_lane = lax.broadcasted_iota(jnp.int32, (1, LANES), 1) < N_HEADS
    sel = (lax.broadcasted_iota(jnp.int32, (LANES, N_HEADS * LANES), 0)
           == (lax.broadcasted_iota(jnp.int32, (LANES, N_HEADS * LANES), 1) >> 7)).astype(BF16)
    ones = jnp.ones((CHUNK, HEAD_DIM), F32)
    last = slice(CHUNK - 1, CHUNK)

    for c0 in range(0, n_chunks, MLSTM_CHUNK_GROUP):
        chunks = range(c0, c0 + MLSTM_CHUNK_GROUP)
        rows = {c: slice(c * CHUNK, (c + 1) * CHUNK) for c in chunks}
        z = {c: g_ref[0, rows[c], :] + gb for c in chunks}
        b0 = {c: pltpu.roll(_cumsum_rows(_log_sigmoid(z[c])), LANES - N_HEADS, 1) for c in chunks}
        cv = {c: jnp.where(head_lane, z[c] - b0[c], 0.0) for c in chunks}
        cm = {c: _cummax_rows(cv[c]) for c in chunks}
        cT = {c: cv[c].T for c in chunks}
        cmb = {c: _bcast_head_cols(cm[c], sel, 3) for c in chunks}
        e1b = {c: _bcast_head_cols(jnp.where(head_lane, jnp.exp(cv[c] - cm[c][last]), 0.0), sel, 2) for c in chunks}
        for c in chunks:
            b0_ref[rows[c], :] = b0[c]
            cm_ref[rows[c], :] = cm[c]
        items = [(c, h) for c in chunks for h in heads]
        n_items = range(len(items))
        q = [qc_ref[rows[c], hs[h]] for c, h in items]
        k = [kc_ref[rows[c], hs[h]] * scale for c, h in items]
        vo = [jnp.concatenate([v_ref[0, rows[c], hs[h]], ones], axis=1).astype(BF16) for c, h in items]
        qk = [_dot_nt(q[i], k[i]) for i in n_items]
        s = [qk[i] * jnp.where(causal, jnp.exp(jnp.minimum(cT[c][h:h + 1, :] - cmb[c][:, h * LANES:h * LANES + CHUNK],
                                                           0.0)), 0.0) for i, (c, h) in enumerate(items)]
        s_hi = [s[i].astype(BF16) for i in n_items]
        s_lo = [(s[i] - s_hi[i].astype(F32)).astype(BF16) for i in n_items]
        svr = [jnp.dot(jnp.concatenate([s_hi[i], s_lo[i]], axis=0), vo[i], preferred_element_type=F32)
               for i in n_items]
        kvn = [_dot((k[i] * e1b[c][:, hs[h]]).T, vo[i]) for i, (c, h) in enumerate(items)]
        for i, (c, h) in enumerate(items):
            sv_ref[rows[c], hs[h]] = svr[i][:CHUNK, :HEAD_DIM] + svr[i][CHUNK:, :HEAD_DIM]
            rs_ref[rows[c], hs[h]] = svr[i][:CHUNK, HEAD_DIM:] + svr[i][CHUNK:, HEAD_DIM:]
            kvn_ref[c * N_HEADS + h] = kvn[i]

    for c in range(n_chunks):
        rows = slice(c * CHUNK, (c + 1) * CHUNK)
        m = m_ref[0:1, :]
        b0 = b0_ref[rows, :]
        cm = cm_ref[rows, :]
        mx = jnp.maximum(cm, m)
        m_new = jnp.maximum(b0[last] + m, b0[last] + cm[last])
        m_ref[0:1, :] = m_new
        zero = lambda x: jnp.where(head_lane, x, 0.0)
        fib = _bcast_head_cols(zero(jnp.exp(cm - mx)), sel, 2)
        scb = _bcast_head_cols(zero(jnp.exp(m - mx)), sel, 2)
        emtb = _bcast_head_cols(zero(jnp.exp(jnp.minimum(-(b0 + mx), 80.0))), sel, 2)
        dfb = _bcast_head_cols(jnp.concatenate([zero(jnp.exp(b0[last] + m - m_new)),
                                                zero(jnp.exp(b0[last] + cm[last] - m_new)),
                                                jnp.zeros((6, LANES), F32)], axis=0), sel, 2)
        CN = [CN_ref[h] for h in heads]
        qcn = [_dot(qc_ref[rows, hs[h]], CN[h]) for h in heads]
        for h in heads:
            num = fib[:, hs[h]] * sv_ref[rows, hs[h]] + scb[:, hs[h]] * qcn[h][:, :HEAD_DIM]
            den = fib[:, hs[h]] * rs_ref[rows, hs[h]] + scb[:, hs[h]] * qcn[h][:, HEAD_DIM:]
            hh_ref[rows, hs[h]] = num / jnp.maximum(jnp.abs(den), emtb[:, hs[h]])
            dec = jnp.concatenate([dfb[0:1, hs[h]]] * 2, axis=1)
            fkv = jnp.concatenate([dfb[1:2, hs[h]]] * 2, axis=1)
            CN_ref[h] = CN[h] * dec + kvn_ref[c * N_HEADS + h] * fkv

    for c in range(n_chunks):
        rows = slice(c * CHUNK, (c + 1) * CHUNK)
        hn = [_head_norm_rows(hh_ref[rows, hs[h]], ng_ref[:, hs[h]], LN_EPS) for h in heads]
        for h in heads:
            out_ref[0, rows, hs[h]] = (hn[h] * _sigmoid(o_ref[0, rows, hs[h]])).astype(out_ref.dtype)


def _mlstm_call(proj, gproj, conv_w, gate_b, norm_g, *, gate_col, tb=256):
    B, T, _ = proj.shape
    nt = T // tb
    gb = jnp.zeros((1, LANES), F32).at[0, :2 * N_HEADS].set(gate_b)
    ng = norm_g.reshape(1, D_GROUP)
    colblk = lambda j: pl.BlockSpec((1, tb, D_GROUP), lambda b, t: (b, t, j))
    halo = lambda j: pl.BlockSpec((1, 8, D_GROUP), lambda b, t: (b, jnp.maximum(t * (tb // 8) - 1, 0), j))
    return pl.pallas_call(
        functools.partial(_mlstm_kernel, tb=tb),
        grid=(B, nt),
        in_specs=[colblk(0), colblk(1), colblk(2), colblk(3),
                  pl.BlockSpec((1, tb, LANES), lambda b, t: (b, t, gate_col // LANES)),
                  halo(0), halo(1),
                  pl.BlockSpec((4, 2 * D_GROUP), lambda b, t: (0, 0)),
                  pl.BlockSpec((1, LANES), lambda b, t: (0, 0)),
                  pl.BlockSpec((1, D_GROUP), lambda b, t: (0, 0))],
        out_specs=pl.BlockSpec((1, tb, D_GROUP), lambda b, t: (b, t, 0)),
        out_shape=jax.ShapeDtypeStruct((B, T, D_GROUP), BF16),
        scratch_shapes=[pltpu.VMEM((tb, D_GROUP), F32)] * 5
                       + [pltpu.VMEM((tb, LANES), F32)] * 2
                       + [pltpu.VMEM((tb // CHUNK * N_HEADS, HEAD_DIM, 2 * HEAD_DIM), F32),
                          pltpu.VMEM((N_HEADS, HEAD_DIM, 2 * HEAD_DIM), F32),
                          pltpu.VMEM((8, LANES), F32)],
        compiler_params=pltpu.CompilerParams(dimension_semantics=("arbitrary", "arbitrary"),
                                             vmem_limit_bytes=VMEM_LIMIT),
        name="mlstm",
    )(proj, proj, proj, proj, gproj, proj, proj, conv_w, gb, ng)


def _rope_kernel(pos_ref, inv_ref, cos_ref, sin_ref):
    ang = pos_ref[0].astype(F32) * inv_ref[...]
    lane = lax.broadcasted_iota(jnp.int32, ang.shape, 1)
    cos_ref[0] = jnp.cos(ang)
    sin_ref[0] = jnp.where(lane < HEAD_DIM // 2, -jnp.sin(ang), jnp.sin(ang))


def _rope_call(positions, *, tb=512):
    B, T = positions.shape
    half = HEAD_DIM // 2
    inv_freq = ROPE_BASE ** (-jnp.arange(half, dtype=F32) / half)
    inv2 = jnp.concatenate([inv_freq, inv_freq]).reshape(1, HEAD_DIM)
    spec = pl.BlockSpec((1, tb, HEAD_DIM), lambda b, t: (b, t, 0))
    return pl.pallas_call(
        _rope_kernel,
        grid=(B, T // tb),
        in_specs=[pl.BlockSpec((1, tb, 1), lambda b, t: (b, t, 0)),
                  pl.BlockSpec((1, HEAD_DIM), lambda b, t: (0, 0))],
        out_specs=[spec, spec],
        out_shape=[jax.ShapeDtypeStruct((B, T, HEAD_DIM), F32)] * 2,
        compiler_params=pltpu.CompilerParams(dimension_semantics=("arbitrary", "arbitrary")),
        name="rope_table",
    )(positions.reshape(B, T, 1), inv2)


def _ret_kernel(q_ref, k_ref, v_ref, g_ref, cos_ref, sin_ref, ng_ref, out_ref, R_ref, *, tb):
    t = pl.program_id(1)

    @pl.when(t == 0)
    def _():
        R_ref[...] = jnp.zeros_like(R_ref)

    causal = _tri(CHUNK)
    ri = lax.broadcasted_iota(jnp.int32, (CHUNK, CHUNK), 0)
    ci = lax.broadcasted_iota(jnp.int32, (CHUNK, CHUNK), 1)
    rel = (ri - ci).astype(F32)
    tcol = lax.broadcasted_iota(jnp.int32, (CHUNK, 1), 0).astype(F32)
    scale = HEAD_DIM ** -0.5

    def chunk_body(c, carry):
        r0 = pl.multiple_of(c * CHUNK, CHUNK)
        rows = pl.ds(r0, CHUNK)
        cos2 = cos_ref[0, rows, :]
        sin2 = sin_ref[0, rows, :]
        for h in range(N_HEADS):
            hs = slice(h * HEAD_DIM, (h + 1) * HEAD_DIM)
            lg = math.log1p(-2.0 ** (-RET_GAMMA_BASE - h))
            q = q_ref[0, rows, hs]
            k = k_ref[0, rows, hs]
            v = v_ref[0, rows, hs]
            qr = q * cos2 + pltpu.roll(q, HEAD_DIM // 2, 1) * sin2
            kr = (k * cos2 + pltpu.roll(k, HEAD_DIM // 2, 1) * sin2) * scale
            dmat = jnp.where(causal, jnp.exp(rel * lg), 0.0)
            xi = jnp.exp((tcol + 1.0) * lg)
            zeta = jnp.exp((CHUNK - 1.0 - tcol) * lg)
            Rst = R_ref[h]
            intra = _dot(_dot_nt(qr, kr) * dmat, v)
            inter = _dot(qr, Rst) * xi
            R_ref[h] = Rst * math.exp(CHUNK * lg) + _dot((kr * zeta).T, v)
            on = _head_norm_rows(intra + inter, ng_ref[:, hs], LN_EPS)
            out_ref[0, rows, hs] = (on * _silu(g_ref[0, rows, hs])).astype(out_ref.dtype)
        return carry

    lax.fori_loop(0, tb // CHUNK, chunk_body, 0)


def _ret_call(proj, cos2, sin2, norm_g, *, tb=256):
    B, T, _ = proj.shape
    colblk = lambda j: pl.BlockSpec((1, tb, D_GROUP), lambda b, t: (b, t, j))
    tab = pl.BlockSpec((1, tb, HEAD_DIM), lambda b, t: (b, t, 0))
    return pl.pallas_call(
        functools.partial(_ret_kernel, tb=tb),
        grid=(B, T // tb),
        in_specs=[colblk(0), colblk(1), colblk(2), colblk(3), tab, tab,
                  pl.BlockSpec((1, D_GROUP), lambda b, t: (0, 0))],
        out_specs=pl.BlockSpec((1, tb, D_GROUP), lambda b, t: (b, t, 0)),
        out_shape=jax.ShapeDtypeStruct((B, T, D_GROUP), BF16),
        scratch_shapes=[pltpu.VMEM((N_HEADS, HEAD_DIM, HEAD_DIM), F32)],
        compiler_params=pltpu.CompilerParams(dimension_semantics=("arbitrary", "arbitrary"),
                                             vmem_limit_bytes=VMEM_LIMIT),
        name="retention",
    )(proj, proj, proj, proj, cos2, sin2, norm_g.reshape(1, D_GROUP))


def _inv_unit_lower(nms):
    n = nms[0].shape[0]
    eye = (lax.broadcasted_iota(jnp.int32, (n, n), 0) == lax.broadcasted_iota(jnp.int32, (n, n), 1)).astype(F32)
    ps = [eye + nm for nm in nms]
    xs = [_dot(nm, nm) for nm in nms]
    for _ in range(int(math.log2(n)) - 2):
        px = [_dot(jnp.concatenate([p, x], axis=0), x) for p, x in zip(ps, xs)]
        ps = [p + y[:n] for p, y in zip(ps, px)]
        xs = [y[n:] for y in px]
    ps = [p + _dot(p, x) for p, x in zip(ps, xs)]
    resid = [eye - p + _dot(nm, p) for p, nm in zip(ps, nms)]
    return [p + _dot(p, r) for p, r in zip(ps, resid)]


def _inv_unit_lower_packed(nms, m0, m1):
    n = nms[0].shape[0]
    r = lax.broadcasted_iota(jnp.int32, (n, 2 * n), 0)
    c = lax.broadcasted_iota(jnp.int32, (n, 2 * n), 1)
    eye2 = (r == (c & (n - 1))).astype(F32)
    bd = lambda x: jnp.concatenate([x * m0, x * m1], axis=0)
    ps = [eye2 + nm for nm in nms]
    xs = [_dot(nm, bd(nm)) for nm in nms]
    for _ in range(int(math.log2(n)) - 2):
        px = [_dot(jnp.concatenate([p, x], axis=0), bd(x)) for p, x in zip(ps, xs)]
        ps = [p + y[:n] for p, y in zip(ps, px)]
        xs = [y[n:] for y in px]
    ps = [p + _dot(p, bd(x)) for p, x in zip(ps, xs)]
    resid = [eye2 - p + _dot(nm, bd(p)) for p, nm in zip(ps, nms)]
    return [p + _dot(p, bd(r_)) for p, r_ in zip(ps, resid)]


def _solve_unit_lower(nms, rhss):
    n = nms[0].shape[0]
    eye = (lax.broadcasted_iota(jnp.int32, (n, n), 0) == lax.broadcasted_iota(jnp.int32, (n, n), 1)).astype(F32)
    ps = [eye + nm for nm in nms]
    xs = [_dot(nm, nm) for nm in nms]
    for _ in range(int(math.log2(n)) - 2):
        px = [_dot(jnp.concatenate([p, x], axis=0), x) for p, x in zip(ps, xs)]
        ps = [p + y[:n] for p, y in zip(ps, px)]
        xs = [y[n:] for y in px]
    ps = [p + _dot(p, x) for p, x in zip(ps, xs)]
    x0 = [_dot(p, r) for p, r in zip(ps, rhss)]
    resid = [r - a + _dot(nm, a) for r, a, nm in zip(rhss, x0, nms)]
    return [a + _dot(p, r) for a, p, r in zip(x0, ps, resid)]


def _l2norm_rows(z):
    return z * lax.rsqrt(jnp.sum(z * z, axis=-1, keepdims=True) + 1e-6)


GDN_CHUNK_GROUP = 2


def _gdn_kernel(q_ref, k_ref, v_ref, z_ref, g_ref, qh_ref, kh_ref, vh_ref, cw_ref, an_ref, dt_ref, ng_ref,
                out_ref, qc_ref, kc_ref, vc_ref, u_ref, w_ref, qe_ref, o_ref, att_ref, kdT_ref, gl_ref, S_ref,
                *, tb):
    t = pl.program_id(1)
    first = t == 0

    @pl.when(first)
    def _():
        S_ref[...] = jnp.zeros_like(S_ref)

    srcs = ((q_ref, qh_ref, qc_ref), (k_ref, kh_ref, kc_ref), (v_ref, vh_ref, vc_ref))
    for c in range(tb // CHUNK):
        for cb in range(D_GROUP // 256):
            cs = cb * 256
            for i, (src, halo, dst) in enumerate(srcs):
                w = cw_ref[:, i * D_GROUP + cs:i * D_GROUP + cs + 256]
                dst[c * CHUNK:(c + 1) * CHUNK, cs:cs + 256] = _conv_silu_rows(
                    src, halo, w, first, c * CHUNK, CHUNK, cs, 256)

    causal = _tri(CHUNK)
    strict = _tri(CHUNK, strict=True)
    a_neg = an_ref[...]
    dtb = dt_ref[...]
    scale = HEAD_DIM ** -0.5
    n_chunks = tb // CHUNK
    heads = range(N_HEADS)
    hs = [slice(h * HEAD_DIM, (h + 1) * HEAD_DIM) for h in heads]

    for c0 in range(0, n_chunks, GDN_CHUNK_GROUP):
        chunks = range(c0, c0 + GDN_CHUNK_GROUP)
        rows = {c: slice(c * CHUNK, (c + 1) * CHUNK) for c in chunks}
        gz = {c: g_ref[0, rows[c], :] for c in chunks}
        beta = {c: _sigmoid(gz[c]) for c in chunks}
        gc = {c: _cumsum_rows(a_neg * _softplus(gz[c] + dtb)) for c in chunks}
        gcT = {c: gc[c].T for c in chunks}
        for c in chunks:
            gl_ref[c:c + 1, :] = gc[c][CHUNK - 1:CHUNK, :]
        items = [(c, h) for c in chunks for h in heads]
        n_items = range(len(items))
        gc_col = [gc[c][:, h:h + 1] for c, h in items]
        b_col = [beta[c][:, 8 + h:9 + h] for c, h in items]
        gamma = [jnp.where(causal, jnp.exp(gc_col[i] - gcT[c][h:h + 1, :]), 0.0) for i, (c, h) in enumerate(items)]
        q = [_l2norm_rows(qc_ref[rows[c], hs[h]]) * scale for c, h in items]
        k = [_l2norm_rows(kc_ref[rows[c], hs[h]]) for c, h in items]
        kb = [k[i] * b_col[i] for i in n_items]
        eg = [jnp.exp(gc_col[i]) for i in n_items]
        kq = [_dot_nt(jnp.concatenate([kb[i], q[i]], axis=0), k[i]) for i in n_items]
        inv = _inv_unit_lower([-jnp.where(strict, kq[i][:CHUNK] * gamma[i], 0.0) for i in n_items])
        uw = [_dot(inv[i], jnp.concatenate([vc_ref[rows[c], hs[h]] * b_col[i], kb[i] * eg[i]], axis=1))
              for i, (c, h) in enumerate(items)]
        for i, (c, h) in enumerate(items):
            u_ref[rows[c], hs[h]] = uw[i][:, :HEAD_DIM]
            w_ref[rows[c], hs[h]] = uw[i][:, HEAD_DIM:]
            qe_ref[rows[c], hs[h]] = q[i] * eg[i]
            att_ref[h, rows[c], :] = kq[i][CHUNK:] * gamma[i]
            g_last = gc[c][CHUNK - 1:CHUNK, h:h + 1]
            kdT_ref[c * N_HEADS + h] = (k[i] * jnp.exp(g_last - gc_col[i])).T

    for c in range(n_chunks):
        rows = slice(c * CHUNK, (c + 1) * CHUNK)
        S = [S_ref[h] for h in heads]
        ws = [_dot(jnp.concatenate([w_ref[rows, hs[h]], qe_ref[rows, hs[h]]], axis=0), S[h]) for h in heads]
        v_new = [u_ref[rows, hs[h]] - ws[h][:CHUNK] for h in heads]
        av = [_dot(att_ref[h, rows, :], v_new[h]) for h in heads]
        kv = [_dot(kdT_ref[c * N_HEADS + h], v_new[h]) for h in heads]
        for h in heads:
            S_ref[h] = S[h] * jnp.exp(gl_ref[c:c + 1, h:h + 1]) + kv[h]
            o_ref[rows, hs[h]] = ws[h][CHUNK:] + av[h]

    for c in range(n_chunks):
        rows = slice(c * CHUNK, (c + 1) * CHUNK)
        for h in heads:
            on = _head_norm_rows(o_ref[rows, hs[h]], ng_ref[:, hs[h]], 1e-6, center=False)
            out_ref[0, rows, hs[h]] = (on * _silu(z_ref[0, rows, hs[h]])).astype(out_ref.dtype)


def _gdn_call(proj, gproj, conv_w, a_log, dt_bias, norm_g, *, col0, gate_col, tb=256):
    B, T, _ = proj.shape
    j0 = col0 // D_GROUP
    an = jnp.zeros((1, LANES), F32).at[0, :N_HEADS].set(-jnp.exp(a_log.astype(F32)))
    dtb = jnp.zeros((1, LANES), F32).at[0, :N_HEADS].set(dt_bias)
    colblk = lambda j: pl.BlockSpec((1, tb, D_GROUP), lambda b, t: (b, t, j0 + j))
    halo = lambda j: pl.BlockSpec((1, 8, D_GROUP), lambda b, t: (b, jnp.maximum(t * (tb // 8) - 1, 0), j0 + j))
    return pl.pallas_call(
        functools.partial(_gdn_kernel, tb=tb),
        grid=(B, T // tb),
        in_specs=[colblk(0), colblk(1), colblk(2), colblk(3),
                  pl.BlockSpec((1, tb, LANES), lambda b, t: (b, t, gate_col // LANES)),
                  halo(0), halo(1), halo(2),
                  pl.BlockSpec((4, 3 * D_GROUP), lambda b, t: (0, 0)),
                  pl.BlockSpec((1, LANES), lambda b, t: (0, 0)),
                  pl.BlockSpec((1, LANES), lambda b, t: (0, 0)),
                  pl.BlockSpec((1, D_GROUP), lambda b, t: (0, 0))],
        out_specs=pl.BlockSpec((1, tb, D_GROUP), lambda b, t: (b, t, 0)),
        out_shape=jax.ShapeDtypeStruct((B, T, D_GROUP), BF16),
        scratch_shapes=[pltpu.VMEM((tb, D_GROUP), F32)] * 7
                       + [pltpu.VMEM((N_HEADS, tb, CHUNK), F32),
                          pltpu.VMEM((tb // CHUNK * N_HEADS, HEAD_DIM, CHUNK), F32),
                          pltpu.VMEM((max(tb // CHUNK, 8), LANES), F32),
                          pltpu.VMEM((N_HEADS, HEAD_DIM, HEAD_DIM), F32)],
        compiler_params=pltpu.CompilerParams(dimension_semantics=("arbitrary", "arbitrary"),
                                             vmem_limit_bytes=VMEM_LIMIT),
        name="gdn",
    )(proj, proj, proj, proj, gproj, proj, proj, proj, conv_w, an, dtb, norm_g.reshape(1, D_GROUP))


N_PAIRS = N_RWKV // 2
RWKV_LOW = 384


def _shift1_rows(src_ref, halo_ref, first, r0, nrows, c0, ncols):
    if r0 == 0:
        hl = jnp.where(first, 0.0, halo_ref[0, 7:8, c0:c0 + ncols])
        return jnp.concatenate([hl, src_ref[0, 0:nrows - 1, c0:c0 + ncols]], axis=0)
    return src_ref[0, r0 - 1:r0 - 1 + nrows, c0:c0 + ncols]


def _rwkv_kernel(r_ref, k_ref, v_ref, l0_ref, l1_ref, l2_ref,
                 rh_ref, kh_ref, vh_ref, l0h_ref, l1h_ref, l2h_ref,
                 mu_ref, mul_ref, w0_ref, w2_ref, a0_ref, a2_ref, g2_ref, kk_ref, ka_ref, rk_ref,
                 lng_ref, lnb_ref, out_ref,
                 gs_ref, bo_ref, y_ref, atrt_ref, avk_ref, yk_ref, inv_ref, arb_ref, btT_ref, kvT_ref, wlT_ref,
                 H_ref, *, tb):
    t = pl.program_id(1)
    first = t == 0

    @pl.when(first)
    def _():
        H_ref[...] = jnp.zeros_like(H_ref)

    ri = lax.broadcasted_iota(jnp.int32, (LANES, LANES), 0)
    ci = lax.broadcasted_iota(jnp.int32, (LANES, LANES), 1)
    same_head = ((ri // RWKV_HEAD) == (ci // RWKV_HEAD)).astype(F32)
    causal = _tri(CHUNK)
    strict = _tri(CHUNK, strict=True)
    lane1 = lax.broadcasted_iota(jnp.int32, (1, LANES), 1)
    m0 = (lane1 < RWKV_HEAD).astype(F32)
    m1 = 1.0 - m0
    t2 = lax.broadcasted_iota(jnp.int32, (CHUNK, LANES), 0)
    l2 = lax.broadcasted_iota(jnp.int32, (CHUNK, LANES), 1)
    lo = l2 < RWKV_HEAD
    s2 = l2 & (RWKV_HEAD - 1)
    causal2 = t2 >= s2
    strict2 = t2 > s2
    n_chunks = tb // CHUNK
    pairs = range(N_PAIRS)
    halves = [(p, hh) for p in pairs for hh in range(2)]
    ps = [slice(p * LANES, (p + 1) * LANES) for p in pairs]

    def lerp(src, halo, mu, r0, c0, ncols):
        cur = src[0, r0:r0 + CHUNK, c0:c0 + ncols]
        return cur + (_shift1_rows(src, halo, first, r0, CHUNK, c0, ncols) - cur) * mu

    for c in range(n_chunks):
        r0 = c * CHUNK
        rows = slice(r0, r0 + CHUNK)
        wl = lerp(l0_ref, l0h_ref, mul_ref[:, 0:LANES], r0, 0, LANES)
        g1 = lerp(l1_ref, l1h_ref, mul_ref[:, LANES:2 * LANES], r0, 0, LANES)
        g2 = lerp(l2_ref, l2h_ref, mul_ref[:, 2 * LANES:3 * LANES], r0, 0, LANES)
        wl_t = jnp.where(lo, jnp.tanh(wl), 0.0)
        al = jnp.where(lo, 0.0, wl)
        sg1 = _sigmoid(g1)
        sg2 = jnp.where(l2 < 32, _sigmoid(g2), 0.0)
        lw = [-math.exp(-0.5) * _sigmoid(w0_ref[:, ps[p]] + _dot(wl_t, w2_ref[:, ps[p]])) for p in pairs]
        a = [_sigmoid(a0_ref[:, ps[p]] + _dot(al, a2_ref[:, ps[p]])) for p in pairs]
        g = [_dot(sg1, g2_ref[0:LANES, ps[p]]) + _dot(sg2, g2_ref[LANES:2 * LANES, ps[p]]) for p in pairs]
        r = [lerp(r_ref, rh_ref, mu_ref[:, ps[p]], r0, p * LANES, LANES) for p in pairs]
        k = [lerp(k_ref, kh_ref, mu_ref[:, D_GROUP + p * LANES:D_GROUP + (p + 1) * LANES], r0, p * LANES, LANES)
             for p in pairs]
        v = [lerp(v_ref, vh_ref, mu_ref[:, 2 * D_GROUP + p * LANES:2 * D_GROUP + (p + 1) * LANES], r0, p * LANES, LANES)
             for p in pairs]
        kk = [k[p] * kk_ref[:, ps[p]] for p in pairs]
        nrm = [jnp.sqrt(_half_lane_sums(kk[p] * kk[p], m0, lo)) for p in pairs]
        kk = [kk[p] / jnp.maximum(nrm[p], 1e-12) for p in pairs]
        k2 = [k[p] * (1.0 + (a[p] - 1.0) * ka_ref[:, ps[p]]) for p in pairs]
        rk = [_half_lane_sums(r[p] * k2[p] * rk_ref[:, ps[p]], m0, lo) for p in pairs]
        for p in pairs:
            gs_ref[p, rows, :] = g[p]
            bo_ref[p, rows, :] = rk[p] * v[p]
        cs = [_cumsum_rows(lw[p]) for p in pairs]
        w_inv = [jnp.exp(-cs[p]) for p in pairs]
        w_end = [jnp.exp(cs[p][CHUNK - 1:CHUNK, :]) for p in pairs]
        rt = [r[p] * jnp.exp(cs[p]) for p in pairs]
        at = [-kk[p] * jnp.exp(cs[p] - lw[p]) for p in pairs]
        bt = [kk[p] * a[p] * w_inv[p] for p in pairs]
        kt = [k2[p] * w_inv[p] for p in pairs]
        pm0 = [_dot_nt(jnp.concatenate([at[p] * m0, rt[p] * m0], axis=0), jnp.concatenate([bt[p], kt[p]], axis=0))
               for p in pairs]
        pm1 = [_dot_nt(jnp.concatenate([at[p] * m1, rt[p] * m1], axis=0), jnp.concatenate([kt[p], bt[p]], axis=0))
               for p in pairs]
        n_ab = [jnp.where(strict2, jnp.where(lo, pm0[p][:CHUNK], pm1[p][:CHUNK]), 0.0) for p in pairs]
        a_rb = [jnp.where(causal2, jnp.where(lo, pm0[p][CHUNK:], pm1[p][CHUNK:]), 0.0) for p in pairs]
        akrk = [jnp.concatenate([jnp.where(strict2, jnp.where(lo, pm1[p][:CHUNK], pm0[p][:CHUNK]), 0.0),
                                 jnp.where(causal2, jnp.where(lo, pm1[p][CHUNK:], pm0[p][CHUNK:]), 0.0)], axis=0)
                for p in pairs]
        vk = [_dot(akrk[p], jnp.concatenate([v[p] * m1, v[p] * m0], axis=0)) for p in pairs]
        inv = _inv_unit_lower_packed(n_ab, m0, m1)
        kv = [_dot((kt[p] * w_end[p]).T, v[p]) * same_head for p in pairs]
        for p in pairs:
            i = c * N_PAIRS + p
            atrt_ref[i] = jnp.concatenate([at[p], rt[p]], axis=0).astype(BF16)
            avk_ref[i] = vk[p][:CHUNK]
            yk_ref[i] = vk[p][CHUNK:]
            btT_ref[i] = (bt[p] * w_end[p]).T.astype(BF16)
            kvT_ref[i] = kv[p]
            wlT_ref[i] = jnp.broadcast_to(w_end[p], (LANES, LANES)).T
            inv_ref[i] = inv[p].astype(BF16)
            arb_ref[i] = a_rb[p].astype(BF16)

    for c in range(n_chunks):
        rows = slice(c * CHUNK, (c + 1) * CHUNK)
        it = [c * N_PAIRS + p for p in pairs]
        H = [H_ref[p] for p in pairs]
        xy0 = [_dot(atrt_ref[it[p]], H[p]) for p in pairs]
        x = [xy0[p][:CHUNK] + avk_ref[it[p]] for p in pairs]
        u = [_dot(inv_ref[it[p]], jnp.concatenate([x[p] * m0, x[p] * m1], axis=0)) for p in pairs]
        yb = [_dot(arb_ref[it[p]], jnp.concatenate([u[p] * m0, u[p] * m1], axis=0)) for p in pairs]
        bu = [_dot(btT_ref[it[p]], u[p]) for p in pairs]
        for p in pairs:
            y_ref[p, rows, :] = xy0[p][CHUNK:] + yk_ref[it[p]] + yb[p]
            H_ref[p] = H[p] * wlT_ref[it[p]] + bu[p] * same_head + kvT_ref[it[p]]

    inv_n = 1.0 / RWKV_HEAD
    for c in range(n_chunks):
        rows = slice(c * CHUNK, (c + 1) * CHUNK)
        y = [y_ref[p, rows, :] for p in pairs]
        yc = [y[p] - _half_lane_sums(y[p], m0, lo) * inv_n for p in pairs]
        var = [_half_lane_sums(yc[p] * yc[p], m0, lo) * inv_n for p in pairs]
        for p in pairs:
            yn = yc[p] * lax.rsqrt(var[p] + RWKV_LN_EPS) * lng_ref[:, ps[p]] + lnb_ref[:, ps[p]]
            out_ref[0, rows, ps[p]] = ((yn + bo_ref[p, rows, :]) * gs_ref[p, rows, :]).astype(out_ref.dtype)


def _rwkv_call(proj, mu, w0, w2, a0, a2, g2, k_k, k_a, r_k, ln_g, ln_b, *, col0, tb=256):
    B, T, _ = proj.shape
    j0 = col0 // D_GROUP
    l0 = (col0 + 3 * D_GROUP) // LANES
    row = lambda a: a.reshape(1, -1).astype(F32)
    mul = jnp.zeros((1, RWKV_LOW), F32).at[0, :288].set(mu[3 * D_GROUP:])
    w2p = jnp.zeros((LANES, D_GROUP), F32).at[:64].set(w2)
    a2p = jnp.zeros((LANES, D_GROUP), F32).at[64:].set(a2)
    g2p = jnp.zeros((2 * LANES, D_GROUP), F32).at[:160].set(g2)
    colblk = lambda j: pl.BlockSpec((1, tb, D_GROUP), lambda b, t: (b, t, j0 + j))
    lowblk = lambda j: pl.BlockSpec((1, tb, LANES), lambda b, t: (b, t, l0 + j))
    hrow = lambda t: jnp.maximum(t * (tb // 8) - 1, 0)
    halo = lambda j: pl.BlockSpec((1, 8, D_GROUP), lambda b, t: (b, hrow(t), j0 + j))
    lowhalo = lambda j: pl.BlockSpec((1, 8, LANES), lambda b, t: (b, hrow(t), l0 + j))
    full = lambda a: pl.BlockSpec(a.shape, lambda b, t: (0,) * a.ndim)
    params = [row(mu[:3 * D_GROUP]), mul, row(w0), w2p, row(a0), a2p, g2p, row(k_k), row(k_a), row(r_k),
              row(ln_g), row(ln_b)]
    big = pltpu.VMEM((N_PAIRS, tb, LANES), F32)
    n_items = tb // CHUNK * N_PAIRS
    return pl.pallas_call(
        functools.partial(_rwkv_kernel, tb=tb),
        grid=(B, T // tb),
        in_specs=[colblk(0), colblk(1), colblk(2), lowblk(0), lowblk(1), lowblk(2),
                  halo(0), halo(1), halo(2), lowhalo(0), lowhalo(1), lowhalo(2)] + [full(a) for a in params],
        out_specs=pl.BlockSpec((1, tb, D_GROUP), lambda b, t: (b, t, 0)),
        out_shape=jax.ShapeDtypeStruct((B, T, D_GROUP), BF16),
        scratch_shapes=[big] * 3 + [
            pltpu.VMEM((n_items, 2 * CHUNK, LANES), BF16),
            pltpu.VMEM((n_items, CHUNK, LANES), F32),
            pltpu.VMEM((n_items, CHUNK, LANES), F32),
            pltpu.VMEM((n_items, CHUNK, LANES), BF16),
            pltpu.VMEM((n_items, CHUNK, LANES), BF16),
            pltpu.VMEM((n_items, LANES, CHUNK), BF16),
            pltpu.VMEM((n_items, LANES, LANES), F32),
            pltpu.VMEM((n_items, LANES, LANES), F32),
            pltpu.VMEM((N_PAIRS, LANES, LANES), F32)],
        compiler_params=pltpu.CompilerParams(dimension_semantics=("arbitrary", "arbitrary"),
                                             vmem_limit_bytes=VMEM_LIMIT),
        name="rwkv7",
    )(*([proj] * 12), *params)


def _ada_kernel(c_ref, w_ref, b_ref, out_ref):
    sc = _silu(c_ref[...]).astype(BF16)
    out_ref[0] = jnp.dot(sc, w_ref[0].astype(BF16), preferred_element_type=F32) + b_ref[0]


def _ada_call(c, ada_w, ada_b, *, tn=1536):
    B = c.shape[0]
    n_mod = ada_w.shape[0] * ada_w.shape[1]
    w = ada_w.reshape(n_mod, D_MODEL, 3 * D_MODEL)
    b = ada_b.reshape(n_mod, 1, 3 * D_MODEL)
    return pl.pallas_call(
        _ada_kernel,
        grid=(n_mod, 3 * D_MODEL // tn),
        in_specs=[pl.BlockSpec((B, D_MODEL), lambda i, j: (0, 0)),
                  pl.BlockSpec((1, D_MODEL, tn), lambda i, j: (i, 0, j)),
                  pl.BlockSpec((1, 1, tn), lambda i, j: (i, 0, j))],
        out_specs=pl.BlockSpec((1, B, tn), lambda i, j: (i, 0, j)),
        out_shape=jax.ShapeDtypeStruct((n_mod, B, 3 * D_MODEL), F32),
        compiler_params=pltpu.CompilerParams(dimension_semantics=("arbitrary", "arbitrary"),
                                             vmem_limit_bytes=VMEM_LIMIT),
        name="adaln",
    )(c, w, b)


def _mod_spec(i, part, nb):
    return pl.BlockSpec((1, 1, D_MODEL), lambda b, t: (i * nb + b, 0, part))


def _modulate_kernel(x_ref, shift_ref, scale_ref, h_ref):
    h_ref[0] = (x_ref[0] * (1.0 + scale_ref[0]) + shift_ref[0]).astype(h_ref.dtype)


def _modulate_call(x, mods3, i, *, tb=512):
    B, T, _ = x.shape
    blk = pl.BlockSpec((1, tb, D_MODEL), lambda b, t: (b, t, 0))
    return pl.pallas_call(
        _modulate_kernel,
        grid=(B, T // tb),
        in_specs=[blk, _mod_spec(i, 0, B), _mod_spec(i, 1, B)],
        out_specs=blk,
        out_shape=jax.ShapeDtypeStruct(x.shape, BF16),
        compiler_params=pltpu.CompilerParams(dimension_semantics=("arbitrary", "arbitrary"),
                                             vmem_limit_bytes=VMEM_LIMIT),
        name="modulate",
    )(x, mods3, mods3)


LN_ROWS = 16


def _proj_ln_kernel(*refs, n_lhs, nk, tm, with_next):
    lhs = refs[:n_lhs]
    ws = refs[n_lhs:2 * n_lhs]
    x_ref, gate_ref, g_ref, b_ref = refs[2 * n_lhs:2 * n_lhs + 4]
    if with_next:
        shift_ref, scale_ref, xo_ref, h_ref, acc_ref = refs[2 * n_lhs + 4:]
    else:
        xo_ref, acc_ref = refs[2 * n_lhs + 4:]
    k = pl.program_id(2)
    def partial_product():
        part = jnp.dot(lhs[0][0], ws[0][...], preferred_element_type=F32)
        for j in range(1, n_lhs):
            part = part + jnp.dot(lhs[j][0], ws[j][...], preferred_element_type=F32)
        return part

    @pl.when(k == 0)
    def _():
        acc_ref[...] = partial_product()

    @pl.when(k > 0)
    def _():
        acc_ref[...] += partial_product()

    @pl.when(k == nk - 1)
    def _():
        gate1 = 1.0 + gate_ref[0]
        g, b = g_ref[...], b_ref[...]
        if with_next:
            scale1, shift = 1.0 + scale_ref[0], shift_ref[0]

        def rows_body(r, carry):
            rows = pl.ds(pl.multiple_of(r * LN_ROWS, LN_ROWS), LN_ROWS)
            z = ALPHA * x_ref[0, rows, :] + gate1 * acc_ref[rows, :]
            zc = z - jnp.mean(z, axis=-1, keepdims=True)
            var = jnp.mean(zc * zc, axis=-1, keepdims=True)
            xn = zc * lax.rsqrt(var + LN_EPS) * g + b
            xo_ref[0, rows, :] = xn
            if with_next:
                h_ref[0, rows, :] = (xn * scale1 + shift).astype(h_ref.dtype)
            return carry

        lax.fori_loop(0, tm // LN_ROWS, rows_body, 0, unroll=8)


def _proj_ln_call(lhs, ws, x, mods3, i, g, b, *, with_next, tm, tk):
    B, T, D = x.shape
    n_lhs = len(lhs)
    nk = lhs[0].shape[2] // tk
    blk = pl.BlockSpec((1, tm, D), lambda b, t, k: (b, t, 0))
    row = pl.BlockSpec((1, D), lambda b, t, k: (0, 0))
    mod = lambda ii, part: pl.BlockSpec((1, 1, D), lambda b, t, k: (ii * B + b, 0, part))
    wspec = lambda idx, k0: pl.BlockSpec((None, tk, D), lambda b, t, k: (idx, k0 + k, 0))
    in_specs = ([pl.BlockSpec((1, tm, tk), lambda b, t, k: (b, t, k))] * n_lhs
                + [wspec(idx, k0) for _, idx, k0 in ws]
                + [blk, mod(i, 2), row, row])
    args = list(lhs) + [w for w, _, _ in ws] + [x, mods3, g.reshape(1, D), b.reshape(1, D)]
    out_specs = [blk]
    out_shape = [jax.ShapeDtypeStruct(x.shape, F32)]
    if with_next:
        in_specs += [mod(i + 1, 0), mod(i + 1, 1)]
        args += [mods3, mods3]
        out_specs.append(blk)
        out_shape.append(jax.ShapeDtypeStruct(x.shape, BF16))
    return pl.pallas_call(
        functools.partial(_proj_ln_kernel, n_lhs=n_lhs, nk=nk, tm=tm, with_next=with_next),
        grid=(B, T // tm, nk),
        in_specs=in_specs, out_specs=out_specs, out_shape=out_shape,
        scratch_shapes=[pltpu.VMEM((tm, D), F32)],
        compiler_params=pltpu.CompilerParams(dimension_semantics=("arbitrary", "arbitrary", "arbitrary"),
                                             vmem_limit_bytes=VMEM_LIMIT),
        name="proj_residual_ln",
    )(*args)


def _matmul_kernel(a_ref, b_ref, o_ref, *scratch, nk):
    if nk == 1:
        o_ref[...] = jnp.dot(a_ref[...], b_ref[...], preferred_element_type=F32).astype(o_ref.dtype)
        return
    (acc_ref,) = scratch
    k = pl.program_id(2)

    @pl.when(k == 0)
    def _():
        acc_ref[...] = jnp.zeros_like(acc_ref)

    acc_ref[...] += jnp.dot(a_ref[...], b_ref[...], preferred_element_type=F32)

    @pl.when(k == nk - 1)
    def _():
        o_ref[...] = acc_ref[...].astype(o_ref.dtype)


def _matmul_call(a, b, *, tm, tn, tk, out_dtype=F32):
    M, K = a.shape
    _, N = b.shape
    nk = K // tk
    return pl.pallas_call(
        functools.partial(_matmul_kernel, nk=nk),
        grid=(N // tn, M // tm, nk),
        in_specs=[pl.BlockSpec((tm, tk), lambda j, i, k: (i, k)),
                  pl.BlockSpec((tk, tn), lambda j, i, k: (k, j))],
        out_specs=pl.BlockSpec((tm, tn), lambda j, i, k: (i, j)),
        out_shape=jax.ShapeDtypeStruct((M, N), out_dtype),
        scratch_shapes=[] if nk == 1 else [pltpu.VMEM((tm, tn), F32)],
        compiler_params=pltpu.CompilerParams(dimension_semantics=("arbitrary", "arbitrary", "arbitrary"),
                                             vmem_limit_bytes=VMEM_LIMIT),
        name="matmul",
    )(a, b)


XPOSE_ROWS = 256


def _matmul_f32wt_kernel(a_ref, wt_ref, o_ref, wb_ref):
    @pl.when(pl.program_id(1) == 0)
    def _():
        for r in range(0, wt_ref.shape[0], XPOSE_ROWS):
            wb_ref[:, r:r + XPOSE_ROWS] = wt_ref[r:r + XPOSE_ROWS, :].T.astype(BF16)

    o_ref[...] = jnp.dot(a_ref[...], wb_ref[...], preferred_element_type=F32)


def _matmul_f32wt_call(a, wt, idx, ncols, *, tm=512, tn=1024):
    M, K = a.shape
    return pl.pallas_call(
        _matmul_f32wt_kernel,
        grid=(ncols // tn, M // tm),
        in_specs=[pl.BlockSpec((tm, K), lambda j, i: (i, 0)),
                  pl.BlockSpec((None, tn, K), lambda j, i: (idx, j, 0))],
        out_specs=pl.BlockSpec((tm, tn), lambda j, i: (i, j)),
        out_shape=jax.ShapeDtypeStruct((M, ncols), F32),
        scratch_shapes=[pltpu.VMEM((K, tn), BF16)],
        compiler_params=pltpu.CompilerParams(dimension_semantics=("arbitrary", "arbitrary"),
                                             vmem_limit_bytes=VMEM_LIMIT),
        name="matmul_f32wt",
    )(a, wt)


def _regroup_rows_kernel(wt_ref, o_ref, *, segments):
    tc = wt_ref.shape[1]
    pieces, pos = [], 0
    for src, width, dst in segments:
        if dst > pos:
            pieces.append(jnp.zeros((dst - pos, tc), F32))
        pieces.append(wt_ref[src:src + width, :])
        pos = dst + width
    if pos < o_ref.shape[1]:
        pieces.append(jnp.zeros((o_ref.shape[1] - pos, tc), F32))
    stacked = jnp.concatenate(pieces, axis=0)
    for r in range(0, o_ref.shape[1], LANES):
        o_ref[:, r:r + LANES] = stacked[r:r + LANES, :].T.astype(o_ref.dtype)


def _regroup_rows_call(wt, idx, segments, out_cols, *, block_rows, block_index, tc):
    _, _, K = wt.shape
    return pl.pallas_call(
        functools.partial(_regroup_rows_kernel, segments=segments),
        grid=(K // tc,),
        in_specs=[pl.BlockSpec((None, block_rows, tc), lambda c: (idx, block_index, c))],
        out_specs=pl.BlockSpec((tc, out_cols), lambda c: (c, 0)),
        out_shape=jax.ShapeDtypeStruct((K, out_cols), BF16),
        compiler_params=pltpu.CompilerParams(dimension_semantics=("arbitrary",), vmem_limit_bytes=VMEM_LIMIT),
        name="regroup_rows",
    )(wt)


def _swiglu_up_kernel(h_ref, wg_ref, wu_ref, o_ref, wgb_ref, wub_ref):
    @pl.when(pl.program_id(1) == 0)
    def _():
        wgb_ref[...] = wg_ref[...].astype(BF16)
        wub_ref[...] = wu_ref[...].astype(BF16)

    h = h_ref[...]
    g = jnp.dot(h, wgb_ref[...], preferred_element_type=F32)
    u = jnp.dot(h, wub_ref[...], preferred_element_type=F32)
    o_ref[...] = (_silu(g) * u).astype(o_ref.dtype)


def _swiglu_up_call(h, wg, wu, layer, *, tm=512, tn=512):
    M, K = h.shape
    _, _, N = wg.shape
    wspec = pl.BlockSpec((None, K, tn), lambda j, i: (layer, 0, j))
    return pl.pallas_call(
        _swiglu_up_kernel,
        grid=(N // tn, M // tm),
        in_specs=[pl.BlockSpec((tm, K), lambda j, i: (i, 0)), wspec, wspec],
        out_specs=pl.BlockSpec((tm, tn), lambda j, i: (i, j)),
        out_shape=jax.ShapeDtypeStruct((M, N), BF16),
        scratch_shapes=[pltpu.VMEM((K, tn), BF16), pltpu.VMEM((K, tn), BF16)],
        compiler_params=pltpu.CompilerParams(dimension_semantics=("arbitrary", "arbitrary"),
                                             vmem_limit_bytes=VMEM_LIMIT),
        name="swiglu_up",
    )(h, wg, wu)


A_MAIN = 4 * D_GROUP
B_COLS_PAD = 3 * D_GROUP + RWKV_LOW + LANES
MLSTM_GATE_COL = 3 * D_GROUP + RWKV_LOW
CD_MAIN = 8 * D_GROUP
GDN_COL0 = 4 * D_GROUP


N_GATES = 2 * N_HEADS
B_SRC = A_MAIN + N_GATES
B_LOW = 64 + 64 + 160
B_SEGMENTS = ((B_SRC, 3 * D_GROUP, 0), (B_SRC + 3 * D_GROUP, B_LOW, 3 * D_GROUP), (A_MAIN, N_GATES, MLSTM_GATE_COL))
CD_GATE_SEGMENTS = ((0, N_GATES, 0),)


def kernel(x, c, positions, ada_w, ada_b, ln_g, ln_b, ab_w_in, ab_w_out, mlstm_conv_w, mlstm_gate_b, mlstm_norm_g, rwkv_mu, rwkv_w0, rwkv_w2, rwkv_a0, rwkv_a2, rwkv_g2, rwkv_k_k, rwkv_k_a, rwkv_r_k, rwkv_ln_g, rwkv_ln_b, cd_w_in, cd_w_out, ret_norm_g, gdn_conv_w, gdn_a_log, gdn_dt_bias, gdn_norm_g, ffn_w_gate, ffn_w_up, ffn_w_down):
    B, T, D = x.shape
    M = B * T
    depth = ada_w.shape[0]
    mods = _ada_call(c, ada_w, ada_b)
    mods3 = mods.reshape(2 * depth * B, 1, 3 * D)
    cos2, sin2 = _rope_call(positions)
    h = _modulate_call(x, mods3, 0)
    w_down = ffn_w_down.astype(BF16)
    for layer in range(depth):
        j = layer // 2
        i_mix, i_ffn = 2 * layer, 2 * layer + 1
        if layer % 2 == 0:
            h2 = h.reshape(M, D)
            wt = jnp.swapaxes(ab_w_in, 1, 2)
            proj_a = _matmul_f32wt_call(h2, wt, j, A_MAIN).reshape(B, T, A_MAIN)
            w_b = _regroup_rows_call(wt, j, B_SEGMENTS, B_COLS_PAD, block_rows=wt.shape[1], block_index=0, tc=256)
            proj_b = _matmul_call(h2, w_b, tm=512, tn=B_COLS_PAD // 2, tk=D)
            proj_b = proj_b.reshape(B, T, B_COLS_PAD)
            ya = _mlstm_call(proj_a, proj_b, mlstm_conv_w[j], mlstm_gate_b[j], mlstm_norm_g[j],
                             gate_col=MLSTM_GATE_COL)
            yb = _rwkv_call(proj_b, rwkv_mu[j], rwkv_w0[j], rwkv_w2[j], rwkv_a0[j], rwkv_a2[j], rwkv_g2[j],
                            rwkv_k_k[j], rwkv_k_a[j], rwkv_r_k[j], rwkv_ln_g[j], rwkv_ln_b[j], col0=0)
            w_out = ab_w_out
        else:
            h2 = h.reshape(M, D)
            wt = jnp.swapaxes(cd_w_in, 1, 2)
            proj = _matmul_f32wt_call(h2, wt, j, CD_MAIN).reshape(B, T, CD_MAIN)
            w_gates = _regroup_rows_call(wt, j, CD_GATE_SEGMENTS, LANES, block_rows=N_GATES,
                                         block_index=CD_MAIN // N_GATES, tc=D)
            proj_g = _matmul_call(h2, w_gates, tm=512, tn=LANES, tk=D).reshape(B, T, LANES)
            ya = _ret_call(proj, cos2, sin2, ret_norm_g[j])
            yb = _gdn_call(proj, proj_g, gdn_conv_w[j], gdn_a_log[j], gdn_dt_bias[j], gdn_norm_g[j],
                           col0=GDN_COL0, gate_col=0)
            w_out = cd_w_out
        w_out = w_out.astype(BF16)
        x, h = _proj_ln_call([ya, yb], [(w_out, j, 0), (w_out, j, 1)], x, mods3, i_mix,
                             ln_g[layer, 0], ln_b[layer, 0], with_next=True, tm=256, tk=D_GROUP)
        act = _swiglu_up_call(h.reshape(M, D), ffn_w_gate, ffn_w_up, layer)
        last = layer == depth - 1
        res = _proj_ln_call([act.reshape(B, T, D_FF)], [(w_down, layer, 0)], x, mods3, i_ffn,
                            ln_g[layer, 1], ln_b[layer, 1], with_next=not last, tm=512, tk=1408)
        if last:
            (x,) = res
        else:
            x, h = res
    return x
```

```python
import functools
import math

import numpy as np
import jax
import jax.numpy as jnp
from jax import lax
from jax.experimental import pallas as pl
from jax.experimental.pallas import tpu as pltpu

F32 = jnp.float32
BF16 = jnp.bfloat16

D_MODEL = 2048
D_GROUP = 1024
HEAD_DIM = 128
N_HEADS = 8
RWKV_HEAD = 64
N_RWKV = 16
CHUNK = 64
D_FF = 5632
DEPTH = 2
ALPHA = (2 * DEPTH) ** 0.25
LN_EPS = 1e-5
RWKV_LN_EPS = 64e-5
ROPE_BASE = 10000.0
RET_GAMMA_BASE = 5.0
LANES = 128
VMEM_LIMIT = 48 * 1024 * 1024


def _dot(a, b):
    return jnp.dot(a.astype(BF16), b.astype(BF16), preferred_element_type=F32)


def _dot_nt(a, b):
    return lax.dot_general(a.astype(BF16), b.astype(BF16), (((1,), (1,)), ((), ())),
                           preferred_element_type=F32)


def _split3(x):
    hi = x.astype(BF16)
    r1 = x - hi.astype(F32)
    mid = r1.astype(BF16)
    lo = (r1 - mid.astype(F32)).astype(BF16)
    return hi, mid, lo


def _cumsum_rows(x):
    n = x.shape[1]
    out = jnp.dot(_tri(CHUNK).astype(BF16), jnp.concatenate(_split3(x), axis=1), preferred_element_type=F32)
    return out[:, :n] + out[:, n:2 * n] + out[:, 2 * n:]


def _half_lane_sums(x, m0, lo):
    s0 = jnp.sum(x * m0, axis=-1, keepdims=True)
    s1 = jnp.sum(x * (1.0 - m0), axis=-1, keepdims=True)
    return jnp.where(lo, s0, s1)


def _sigmoid(x):
    return 1.0 / (1.0 + jnp.exp(-x))


def _silu(x):
    return x * _sigmoid(x)


def _log_sigmoid(x):
    return jnp.minimum(x, 0.0) - jnp.log1p(jnp.exp(-jnp.abs(x)))


def _softplus(x):
    return jnp.maximum(x, 0.0) + jnp.log1p(jnp.exp(-jnp.abs(x)))


def _tri(n, strict=False):
    r = lax.broadcasted_iota(jnp.int32, (n, n), 0)
    c = lax.broadcasted_iota(jnp.int32, (n, n), 1)
    return (r > c) if strict else (r >= c)


def _conv_silu_rows(src_ref, halo_ref, w, first, r0, nrows, c0, ncols):
    cur = src_ref[0, r0:r0 + nrows, c0:c0 + ncols]
    acc = w[3:4] * cur
    if r0 == 0:
        hl = jnp.where(first, 0.0, halo_ref[0, :, c0:c0 + ncols])
        ext = jnp.concatenate([hl, cur[0:8]], axis=0)
        for j in range(3):
            head = ext[5 + j:13 + j]
            if nrows > 8:
                rest = src_ref[0, 5 + j:nrows - 3 + j, c0:c0 + ncols]
                sh = jnp.concatenate([head, rest], axis=0)
            else:
                sh = head
            acc = acc + w[j:j + 1] * sh
    else:
        for j in range(3):
            acc = acc + w[j:j + 1] * src_ref[0, r0 - 3 + j:r0 - 3 + j + nrows, c0:c0 + ncols]
    return _silu(acc)


def _head_norm_rows(h, g_row, eps, center=True):
    if center:
        h = h - jnp.mean(h, axis=-1, keepdims=True)
    return h * lax.rsqrt(jnp.mean(h * h, axis=-1, keepdims=True) + eps) * g_row


MLSTM_CHUNK_GROUP = 2


def _cummax_rows(x):
    row = lax.broadcasted_iota(jnp.int32, x.shape, 0)
    d = 1
    while d < x.shape[0]:
        x = jnp.where(row >= d, jnp.maximum(x, pltpu.roll(x, d, 0)), x)
        d *= 2
    return x


def _bcast_head_cols(x, sel, pieces):
    m = x.shape[0]
    parts, rest = [], x
    for _ in range(pieces):
        hi = rest.astype(BF16)
        parts.append(hi)
        rest = rest - hi.astype(F32)
    out = jnp.dot(jnp.concatenate(parts, axis=0), sel, preferred_element_type=F32)
    acc = out[:m]
    for i in range(1, pieces):
        acc = acc + out[i * m:(i + 1) * m]
    return acc


def _mlstm_kernel(q_ref, k_ref, v_ref, o_ref, g_ref, qh_ref, kh_ref, cw_ref, gb_ref, ng_ref,
                  out_ref, qc_ref, kc_ref, sv_ref, rs_ref, hh_ref, b0_ref, cm_ref, kvn_ref, CN_ref, m_ref, *, tb):
    t = pl.program_id(1)
    first = t == 0

    @pl.when(first)
    def _():
        CN_ref[...] = jnp.zeros_like(CN_ref)
        m_ref[...] = jnp.zeros_like(m_ref)

    for c in range(tb // CHUNK):
        for cb in range(D_GROUP // 256):
            cs = cb * 256
            qc_ref[c * CHUNK:(c + 1) * CHUNK, cs:cs + 256] = _conv_silu_rows(
                q_ref, qh_ref, cw_ref[:, cs:cs + 256], first, c * CHUNK, CHUNK, cs, 256)
            kc_ref[c * CHUNK:(c + 1) * CHUNK, cs:cs + 256] = _conv_silu_rows(
                k_ref, kh_ref, cw_ref[:, D_GROUP + cs:D_GROUP + cs + 256], first, c * CHUNK, CHUNK, cs, 256)

    causal = _tri(CHUNK)
    gb = gb_ref[...]
    scale = HEAD_DIM ** -0.5
    n_chunks = tb // CHUNK
    heads = range(N_HEADS)
    hs = [slice(h * HEAD_DIM, (h + 1) * HEAD_DIM) for h in heads]
    head_lane = lax.broadcasted_iota(jnp.int32, (1, LANES), 1) < N_HEADS
    sel = (lax.broadcasted_iota(jnp.int32, (LANES, N_HEADS * LANES), 0)
           == (lax.broadcasted_iota(jnp.int32, (LANES, N_HEADS * LANES), 1) >> 7)).astype(BF16)
    ones = jnp.ones((CHUNK, HEAD_DIM), F32)
    last = slice(CHUNK - 1, CHUNK)

    for c0 in range(0, n_chunks, MLSTM_CHUNK_GROUP):
        chunks = range(c0, c0 + MLSTM_CHUNK_GROUP)
        rows = {c: slice(c * CHUNK, (c + 1) * CHUNK) for c in chunks}
        z = {c: g_ref[0, rows[c], :] + gb for c in chunks}
        b0 = {c: pltpu.roll(_cumsum_rows(_log_sigmoid(z[c])), LANES - N_HEADS, 1) for c in chunks}
        cv = {c: jnp.where(head_lane, z[c] - b0[c], 0.0) for c in chunks}
        cm = {c: _cummax_rows(cv[c]) for c in chunks}
        cT = {c: cv[c].T for c in chunks}
        cmb = {c: _bcast_head_cols(cm[c], sel, 3) for c in chunks}
        e1b = {c: _bcast_head_cols(jnp.where(head_lane, jnp.exp(cv[c] - cm[c][last]), 0.0), sel, 2) for c in chunks}
        for c in chunks:
            b0_ref[rows[c], :] = b0[c]
            cm_ref[rows[c], :] = cm[c]
        items = [(c, h) for c in chunks for h in heads]
        n_items = range(len(items))
        q = [qc_ref[rows[c], hs[h]] for c, h in items]
        k = [kc_ref[rows[c], hs[h]] * scale for c, h in items]
        vo = [jnp.concatenate([v_ref[0, rows[c], hs[h]], ones], axis=1).astype(BF16) for c, h in items]
        qk = [_dot_nt(q[i], k[i]) for i in n_items]
        s = [qk[i] * jnp.where(causal, jnp.exp(jnp.minimum(cT[c][h:h + 1, :] - cmb[c][:, h * LANES:h * LANES + CHUNK],
                                                           0.0)), 0.0) for i, (c, h) in enumerate(items)]
        s_hi = [s[i].astype(BF16) for i in n_items]
        s_lo = [(s[i] - s_hi[i].astype(F32)).astype(BF16) for i in n_items]
        svr = [jnp.dot(jnp.concatenate([s_hi[i], s_lo[i]], axis=0), vo[i], preferred_element_type=F32)
               for i in n_items]
        kvn = [_dot((k[i] * e1b[c][:, hs[h]]).T, vo[i]) for i, (c, h) in enumerate(items)]
        for i, (c, h) in enumerate(items):
            sv_ref[rows[c], hs[h]] = svr[i][:CHUNK, :HEAD_DIM] + svr[i][CHUNK:, :HEAD_DIM]
            rs_ref[rows[c], hs[h]] = svr[i][:CHUNK, HEAD_DIM:] + svr[i][CHUNK:, HEAD_DIM:]
            kvn_ref[c * N_HEADS + h] = kvn[i]

    for c in range(n_chunks):
        rows = slice(c * CHUNK, (c + 1) * CHUNK)
        m = m_ref[0:1, :]
        b0 = b0_ref[rows, :]
        cm = cm_ref[rows, :]
        mx = jnp.maximum(cm, m)
        m_new = jnp.maximum(b0[last] + m, b0[last] + cm[last])
        m_ref[0:1, :] = m_new
        zero = lambda x: jnp.where(head_lane, x, 0.0)
        fib = _bcast_head_cols(zero(jnp.exp(cm - mx)), sel, 2)
        scb = _bcast_head_cols(zero(jnp.exp(m - mx)), sel, 2)
        emtb = _bcast_head_cols(zero(jnp.exp(jnp.minimum(-(b0 + mx), 80.0))), sel, 2)
        dfb = _bcast_head_cols(jnp.concatenate([zero(jnp.exp(b0[last] + m - m_new)),
                                                zero(jnp.exp(b0[last] + cm[last] - m_new)),
                                                jnp.zeros((6, LANES), F32)], axis=0), sel, 2)
        CN = [CN_ref[h] for h in heads]
        qcn = [_dot(qc_ref[rows, hs[h]], CN[h]) for h in heads]
        for h in heads:
            num = fib[:, hs[h]] * sv_ref[rows, hs[h]] + scb[:, hs[h]] * qcn[h][:, :HEAD_DIM]
            den = fib[:, hs[h]] * rs_ref[rows, hs[h]] + scb[:, hs[h]] * qcn[h][:, HEAD_DIM:]
            hh_ref[rows, hs[h]] = num / jnp.maximum(jnp.abs(den), emtb[:, hs[h]])
            dec = jnp.concatenate([dfb[0:1, hs[h]]] * 2, axis=1)
            fkv = jnp.concatenate([dfb[1:2, hs[h]]] * 2, axis=1)
            CN_ref[h] = CN[h] * dec + kvn_ref[c * N_HEADS + h] * fkv

    for c in range(n_chunks):
        rows = slice(c * CHUNK, (c + 1) * CHUNK)
        hn = [_head_norm_rows(hh_ref[rows, hs[h]], ng_ref[:, hs[h]], LN_EPS) for h in heads]
        for h in heads:
            out_ref[0, rows, hs[h]] = (hn[h] * _sigmoid(o_ref[0, rows, hs[h]])).astype(out_ref.dtype)


def _mlstm_call(proj, gproj, conv_w, gate_b, norm_g, *, gate_col, tb=256):
    B, T, _ = proj.shape
    nt = T // tb
    gb = jnp.zeros((1, LANES), F32).at[0, :2 * N_HEADS].set(gate_b)
    ng = norm_g.reshape(1, D_GROUP)
    colblk = lambda j: pl.BlockSpec((1, tb, D_GROUP), lambda b, t: (b, t, j))
    halo = lambda j: pl.BlockSpec((1, 8, D_GROUP), lambda b, t: (b, jnp.maximum(t * (tb // 8) - 1, 0), j))
    return pl.pallas_call(
        functools.partial(_mlstm_kernel, tb=tb),
        grid=(B, nt),
        in_specs=[colblk(0), colblk(1), colblk(2), colblk(3),
                  pl.BlockSpec((1, tb, LANES), lambda b, t: (b, t, gate_col // LANES)),
                  halo(0), halo(1),
                  pl.BlockSpec((4, 2 * D_GROUP), lambda b, t: (0, 0)),
                  pl.BlockSpec((1, LANES), lambda b, t: (0, 0)),
                  pl.BlockSpec((1, D_GROUP), lambda b, t: (0, 0))],
        out_specs=pl.BlockSpec((1, tb, D_GROUP), lambda b, t: (b, t, 0)),
        out_shape=jax.ShapeDtypeStruct((B, T, D_GROUP), BF16),
        scratch_shapes=[pltpu.VMEM((tb, D_GROUP), F32)] * 5
                       + [pltpu.VMEM((tb, LANES), F32)] * 2
                       + [pltpu.VMEM((tb // CHUNK * N_HEADS, HEAD_DIM, 2 * HEAD_DIM), F32),
                          pltpu.VMEM((N_HEADS, HEAD_DIM, 2 * HEAD_DIM), F32),
                          pltpu.VMEM((8, LANES), F32)],
        compiler_params=pltpu.CompilerParams(dimension_semantics=("arbitrary", "arbitrary"),
                                             vmem_limit_bytes=VMEM_LIMIT),
        name="mlstm",
    )(proj, proj, proj, proj, gproj, proj, proj, conv_w, gb, ng)


def _rope_kernel(pos_ref, inv_ref, cos_ref, sin_ref):
    ang = pos_ref[0].astype(F32) * inv_ref[...]
    lane = lax.broadcasted_iota(jnp.int32, ang.shape, 1)
    cos_ref[0] = jnp.cos(ang)
    sin_ref[0] = jnp.where(lane < HEAD_DIM // 2, -jnp.sin(ang), jnp.sin(ang))


def _rope_call(positions, *, tb=512):
    B, T = positions.shape
    half = HEAD_DIM // 2
    inv_freq = ROPE_BASE ** (-jnp.arange(half, dtype=F32) / half)
    inv2 = jnp.concatenate([inv_freq, inv_freq]).reshape(1, HEAD_DIM)
    spec = pl.BlockSpec((1, tb, HEAD_DIM), lambda b, t: (b, t, 0))
    return pl.pallas_call(
        _rope_kernel,
        grid=(B, T // tb),
        in_specs=[pl.BlockSpec((1, tb, 1), lambda b, t: (b, t, 0)),
                  pl.BlockSpec((1, HEAD_DIM), lambda b, t: (0, 0))],
        out_specs=[spec, spec],
        out_shape=[jax.ShapeDtypeStruct((B, T, HEAD_DIM), F32)] * 2,
        compiler_params=pltpu.CompilerParams(dimension_semantics=("arbitrary", "arbitrary")),
        name="rope_table",
    )(positions.reshape(B, T, 1), inv2)


def _ret_kernel(q_ref, k_ref, v_ref, g_ref, cos_ref, sin_ref, ng_ref, out_ref, R_ref, *, tb):
    t = pl.program_id(1)

    @pl.when(t == 0)
    def _():
        R_ref[...] = jnp.zeros_like(R_ref)

    causal = _tri(CHUNK)
    ri = lax.broadcasted_iota(jnp.int32, (CHUNK, CHUNK), 0)
    ci = lax.broadcasted_iota(jnp.int32, (CHUNK, CHUNK), 1)
    rel = (ri - ci).astype(F32)
    tcol = lax.broadcasted_iota(jnp.int32, (CHUNK, 1), 0).astype(F32)
    scale = HEAD_DIM ** -0.5

    def chunk_body(c, carry):
        r0 = pl.multiple_of(c * CHUNK, CHUNK)
        rows = pl.ds(r0, CHUNK)
        cos2 = cos_ref[0, rows, :]
        sin2 = sin_ref[0, rows, :]
        for h in range(N_HEADS):
            hs = slice(h * HEAD_DIM, (h + 1) * HEAD_DIM)
            lg = math.log1p(-2.0 ** (-RET_GAMMA_BASE - h))
            q = q_ref[0, rows, hs]
            k = k_ref[0, rows, hs]
            v = v_ref[0, rows, hs]
            qr = q * cos2 + pltpu.roll(q, HEAD_DIM // 2, 1) * sin2
            kr = (k * cos2 + pltpu.roll(k, HEAD_DIM // 2, 1) * sin2) * scale
            dmat = jnp.where(causal, jnp.exp(rel * lg), 0.0)
            xi = jnp.exp((tcol + 1.0) * lg)
            zeta = jnp.exp((CHUNK - 1.0 - tcol) * lg)
            Rst = R_ref[h]
            intra = _dot(_dot_nt(qr, kr) * dmat, v)
            inter = _dot(qr, Rst) * xi
            R_ref[h] = Rst * math.exp(CHUNK * lg) + _dot((kr * zeta).T, v)
            on = _head_norm_rows(intra + inter, ng_ref[:, hs], LN_EPS)
            out_ref[0, rows, hs] = (on * _silu(g_ref[0, rows, hs])).astype(out_ref.dtype)
        return carry

    lax.fori_loop(0, tb // CHUNK, chunk_body, 0)


def _ret_call(proj, cos2, sin2, norm_g, *, tb=256):
    B, T, _ = proj.shape
    colblk = lambda j: pl.BlockSpec((1, tb, D_GROUP), lambda b, t: (b, t, j))
    tab = pl.BlockSpec((1, tb, HEAD_DIM), lambda b, t: (b, t, 0))
    return pl.pallas_call(
        functools.partial(_ret_kernel, tb=tb),
        grid=(B, T // tb),
        in_specs=[colblk(0), colblk(1), colblk(2), colblk(3), tab, tab,
                  pl.BlockSpec((1, D_GROUP), lambda b, t: (0, 0))],
        out_specs=pl.BlockSpec((1, tb, D_GROUP), lambda b, t: (b, t, 0)),
        out_shape=jax.ShapeDtypeStruct((B, T, D_GROUP), BF16),
        scratch_shapes=[pltpu.VMEM((N_HEADS, HEAD_DIM, HEAD_DIM), F32)],
        compiler_params=pltpu.CompilerParams(dimension_semantics=("arbitrary", "arbitrary"),
                                             vmem_limit_bytes=VMEM_LIMIT),
        name="retention",
    )(proj, proj, proj, proj, cos2, sin2, norm_g.reshape(1, D_GROUP))


def _inv_unit_lower(nms):
    n = nms[0].shape[0]
    eye = (lax.broadcasted_iota(jnp.int32, (n, n), 0) == lax.broadcasted_iota(jnp.int32, (n, n), 1)).astype(F32)
    ps = [eye + nm for nm in nms]
    xs = [_dot(nm, nm) for nm in nms]
    for _ in range(int(math.log2(n)) - 2):
        px = [_dot(jnp.concatenate([p, x], axis=0), x) for p, x in zip(ps, xs)]
        ps = [p + y[:n] for p, y in zip(ps, px)]
        xs = [y[n:] for y in px]
    ps = [p + _dot(p, x) for p, x in zip(ps, xs)]
    resid = [eye - p + _dot(nm, p) for p, nm in zip(ps, nms)]
    return [p + _dot(p, r) for p, r in zip(ps, resid)]


def _inv_unit_lower_packed(nms, m0, m1):
    n = nms[0].shape[0]
    r = lax.broadcasted_iota(jnp.int32, (n, 2 * n), 0)
    c = lax.broadcasted_iota(jnp.int32, (n, 2 * n), 1)
    eye2 = (r == (c & (n - 1))).astype(F32)
    bd = lambda x: jnp.concatenate([x * m0, x * m1], axis=0)
    ps = [eye2 + nm for nm in nms]
    xs = [_dot(nm, bd(nm)) for nm in nms]
    for _ in range(int(math.log2(n)) - 2):
        px = [_dot(jnp.concatenate([p, x], axis=0), bd(x)) for p, x in zip(ps, xs)]
        ps = [p + y[:n] for p, y in zip(ps, px)]
        xs = [y[n:] for y in px]
    ps = [p + _dot(p, bd(x)) for p, x in zip(ps, xs)]
    resid = [eye2 - p + _dot(nm, bd(p)) for p, nm in zip(ps, nms)]
    return [p + _dot(p, bd(r_)) for p, r_ in zip(ps, resid)]


def _solve_unit_lower(nms, rhss):
    n = nms[0].shape[0]
    eye = (lax.broadcasted_iota(jnp.int32, (n, n), 0) == lax.broadcasted_iota(jnp.int32, (n, n), 1)).astype(F32)
    ps = [eye + nm for nm in nms]
    xs = [_dot(nm, nm) for nm in nms]
    for _ in range(int(math.log2(n)) - 2):
        px = [_dot(jnp.concatenate([p, x], axis=0), x) for p, x in zip(ps, xs)]
        ps = [p + y[:n] for p, y in zip(ps, px)]
        xs = [y[n:] for y in px]
    ps = [p + _dot(p, x) for p, x in zip(ps, xs)]
    x0 = [_dot(p, r) for p, r in zip(ps, rhss)]
    resid = [r - a + _dot(nm, a) for r, a, nm in zip(rhss, x0, nms)]
    return [a + _dot(p, r) for a, p, r in zip(x0, ps, resid)]


def _l2norm_rows(z):
    return z * lax.rsqrt(jnp.sum(z * z, axis=-1, keepdims=True) + 1e-6)


GDN_CHUNK_GROUP = 2


def _gdn_kernel(q_ref, k_ref, v_ref, z_ref, g_ref, qh_ref, kh_ref, vh_ref, cw_ref, an_ref, dt_ref, ng_ref,
                out_ref, qc_ref, kc_ref, vc_ref, u_ref, w_ref, qe_ref, o_ref, att_ref, kdT_ref, gl_ref, S_ref,
                *, tb):
    t = pl.program_id(1)
    first = t == 0

    @pl.when(first)
    def _():
        S_ref[...] = jnp.zeros_like(S_ref)

    srcs = ((q_ref, qh_ref, qc_ref), (k_ref, kh_ref, kc_ref), (v_ref, vh_ref, vc_ref))
    for c in range(tb // CHUNK):
        for cb in range(D_GROUP // 256):
            cs = cb * 256
            for i, (src, halo, dst) in enumerate(srcs):
                w = cw_ref[:, i * D_GROUP + cs:i * D_GROUP + cs + 256]
                dst[c * CHUNK:(c + 1) * CHUNK, cs:cs + 256] = _conv_silu_rows(
                    src, halo, w, first, c * CHUNK, CHUNK, cs, 256)

    causal = _tri(CHUNK)
    strict = _tri(CHUNK, strict=True)
    a_neg = an_ref[...]
    dtb = dt_ref[...]
    scale = HEAD_DIM ** -0.5
    n_chunks = tb // CHUNK
    heads = range(N_HEADS)
    hs = [slice(h * HEAD_DIM, (h + 1) * HEAD_DIM) for h in heads]

    for c0 in range(0, n_chunks, GDN_CHUNK_GROUP):
        chunks = range(c0, c0 + GDN_CHUNK_GROUP)
        rows = {c: slice(c * CHUNK, (c + 1) * CHUNK) for c in chunks}
        gz = {c: g_ref[0, rows[c], :] for c in chunks}
        beta = {c: _sigmoid(gz[c]) for c in chunks}
        gc = {c: _cumsum_rows(a_neg * _softplus(gz[c] + dtb)) for c in chunks}
        gcT = {c: gc[c].T for c in chunks}
        for c in chunks:
            gl_ref[c:c + 1, :] = gc[c][CHUNK - 1:CHUNK, :]
        items = [(c, h) for c in chunks for h in heads]
        n_items = range(len(items))
        gc_col = [gc[c][:, h:h + 1] for c, h in items]
        b_col = [beta[c][:, 8 + h:9 + h] for c, h in items]
        gamma = [jnp.where(causal, jnp.exp(gc_col[i] - gcT[c][h:h + 1, :]), 0.0) for i, (c, h) in enumerate(items)]
        q = [_l2norm_rows(qc_ref[rows[c], hs[h]]) * scale for c, h in items]
        k = [_l2norm_rows(kc_ref[rows[c], hs[h]]) for c, h in items]
        kb = [k[i] * b_col[i] for i in n_items]
        eg = [jnp.exp(gc_col[i]) for i in n_items]
        kq = [_dot_nt(jnp.concatenate([kb[i], q[i]], axis=0), k[i]) for i in n_items]
        inv = _inv_unit_lower([-jnp.where(strict, kq[i][:CHUNK] * gamma[i], 0.0) for i in n_items])
        uw = [_dot(inv[i], jnp.concatenate([vc_ref[rows[c], hs[h]] * b_col[i], kb[i] * eg[i]], axis=1))
              for i, (c, h) in enumerate(items)]
        for i, (c, h) in enumerate(items):
            u_ref[rows[c], hs[h]] = uw[i][:, :HEAD_DIM]
            w_ref[rows[c], hs[h]] = uw[i][:, HEAD_DIM:]
            qe_ref[rows[c], hs[h]] = q[i] * eg[i]
            att_ref[h, rows[c], :] = kq[i][CHUNK:] * gamma[i]
            g_last = gc[c][CHUNK - 1:CHUNK, h:h + 1]
            kdT_ref[c * N_HEADS + h] = (k[i] * jnp.exp(g_last - gc_col[i])).T

    for c in range(n_chunks):
        rows = slice(c * CHUNK, (c + 1) * CHUNK)
        S = [S_ref[h] for h in heads]
        ws = [_dot(jnp.concatenate([w_ref[rows, hs[h]], qe_ref[rows, hs[h]]], axis=0), S[h]) for h in heads]
        v_new = [u_ref[rows, hs[h]] - ws[h][:CHUNK] for h in heads]
        av = [_dot(att_ref[h, rows, :], v_new[h]) for h in heads]
        kv = [_dot(kdT_ref[c * N_HEADS + h], v_new[h]) for h in heads]
        for h in heads:
            S_ref[h] = S[h] * jnp.exp(gl_ref[c:c + 1, h:h + 1]) + kv[h]
            o_ref[rows, hs[h]] = ws[h][CHUNK:] + av[h]

    for c in range(n_chunks):
        rows = slice(c * CHUNK, (c + 1) * CHUNK)
        for h in heads:
            on = _head_norm_rows(o_ref[rows, hs[h]], ng_ref[:, hs[h]], 1e-6, center=False)
            out_ref[0, rows, hs[h]] = (on * _silu(z_ref[0, rows, hs[h]])).astype(out_ref.dtype)


def _gdn_call(proj, gproj, conv_w, a_log, dt_bias, norm_g, *, col0, gate_col, tb=256):
    B, T, _ = proj.shape
    j0 = col0 // D_GROUP
    an = jnp.zeros((1, LANES), F32).at[0, :N_HEADS].set(-jnp.exp(a_log.astype(F32)))
    dtb = jnp.zeros((1, LANES), F32).at[0, :N_HEADS].set(dt_bias)
    colblk = lambda j: pl.BlockSpec((1, tb, D_GROUP), lambda b, t: (b, t, j0 + j))
    halo = lambda j: pl.BlockSpec((1, 8, D_GROUP), lambda b, t: (b, jnp.maximum(t * (tb // 8) - 1, 0), j0 + j))
    return pl.pallas_call(
        functools.partial(_gdn_kernel, tb=tb),
        grid=(B, T // tb),
        in_specs=[colblk(0), colblk(1), colblk(2), colblk(3),
                  pl.BlockSpec((1, tb, LANES), lambda b, t: (b, t, gate_col // LANES)),
                  halo(0), halo(1), halo(2),
                  pl.BlockSpec((4, 3 * D_GROUP), lambda b, t: (0, 0)),
                  pl.BlockSpec((1, LANES), lambda b, t: (0, 0)),
                  pl.BlockSpec((1, LANES), lambda b, t: (0, 0)),
                  pl.BlockSpec((1, D_GROUP), lambda b, t: (0, 0))],
        out_specs=pl.BlockSpec((1, tb, D_GROUP), lambda b, t: (b, t, 0)),
        out_shape=jax.ShapeDtypeStruct((B, T, D_GROUP), BF16),
        scratch_shapes=[pltpu.VMEM((tb, D_GROUP), F32)] * 7
                       + [pltpu.VMEM((N_HEADS, tb, CHUNK), F32),
                          pltpu.VMEM((tb // CHUNK * N_HEADS, HEAD_DIM, CHUNK), F32),
                          pltpu.VMEM((max(tb // CHUNK, 8), LANES), F32),
                          pltpu.VMEM((N_HEADS, HEAD_DIM, HEAD_DIM), F32)],
        compiler_params=pltpu.CompilerParams(dimension_semantics=("arbitrary", "arbitrary"),
                                             vmem_limit_bytes=VMEM_LIMIT),
        name="gdn",
    )(proj, proj, proj, proj, gproj, proj, proj, proj, conv_w, an, dtb, norm_g.reshape(1, D_GROUP))


N_PAIRS = N_RWKV // 2
RWKV_LOW = 384


def _shift1_rows(src_ref, halo_ref, first, r0, nrows, c0, ncols):
    if r0 == 0:
        hl = jnp.where(first, 0.0, halo_ref[0, 7:8, c0:c0 + ncols])
        return jnp.concatenate([hl, src_ref[0, 0:nrows - 1, c0:c0 + ncols]], axis=0)
    return src_ref[0, r0 - 1:r0 - 1 + nrows, c0:c0 + ncols]


def _rwkv_kernel(r_ref, k_ref, v_ref, l0_ref, l1_ref, l2_ref,
                 rh_ref, kh_ref, vh_ref, l0h_ref, l1h_ref, l2h_ref,
                 mu_ref, mul_ref, w0_ref, w2_ref, a0_ref, a2_ref, g2_ref, kk_ref, ka_ref, rk_ref,
                 lng_ref, lnb_ref, out_ref,
                 gs_ref, bo_ref, y_ref, atrt_ref, avk_ref, yk_ref, inv_ref, arb_ref, btT_ref, kvT_ref, wlT_ref,
                 H_ref, *, tb):
    t = pl.program_id(1)
    first = t == 0

    @pl.when(first)
    def _():
        H_ref[...] = jnp.zeros_like(H_ref)

    ri = lax.broadcasted_iota(jnp.int32, (LANES, LANES), 0)
    ci = lax.broadcasted_iota(jnp.int32, (LANES, LANES), 1)
    same_head = ((ri // RWKV_HEAD) == (ci // RWKV_HEAD)).astype(F32)
    causal = _tri(CHUNK)
    strict = _tri(CHUNK, strict=True)
    lane1 = lax.broadcasted_iota(jnp.int32, (1, LANES), 1)
    m0 = (lane1 < RWKV_HEAD).astype(F32)
    m1 = 1.0 - m0
    t2 = lax.broadcasted_iota(jnp.int32, (CHUNK, LANES), 0)
    l2 = lax.broadcasted_iota(jnp.int32, (CHUNK, LANES), 1)
    lo = l2 < RWKV_HEAD
    s2 = l2 & (RWKV_HEAD - 1)
    causal2 = t2 >= s2
    strict2 = t2 > s2
    n_chunks = tb // CHUNK
    pairs = range(N_PAIRS)
    halves = [(p, hh) for p in pairs for hh in range(2)]
    ps = [slice(p * LANES, (p + 1) * LANES) for p in pairs]

    def lerp(src, halo, mu, r0, c0, ncols):
        cur = src[0, r0:r0 + CHUNK, c0:c0 + ncols]
        return cur + (_shift1_rows(src, halo, first, r0, CHUNK, c0, ncols) - cur) * mu

    for c in range(n_chunks):
        r0 = c * CHUNK
        rows = slice(r0, r0 + CHUNK)
        wl = lerp(l0_ref, l0h_ref, mul_ref[:, 0:LANES], r0, 0, LANES)
        g1 = lerp(l1_ref, l1h_ref, mul_ref[:, LANES:2 * LANES], r0, 0, LANES)
        g2 = lerp(l2_ref, l2h_ref, mul_ref[:, 2 * LANES:3 * LANES], r0, 0, LANES)
        wl_t = jnp.where(lo, jnp.tanh(wl), 0.0)
        al = jnp.where(lo, 0.0, wl)
        sg1 = _sigmoid(g1)
        sg2 = jnp.where(l2 < 32, _sigmoid(g2), 0.0)
        lw = [-math.exp(-0.5) * _sigmoid(w0_ref[:, ps[p]] + _dot(wl_t, w2_ref[:, ps[p]])) for p in pairs]
        a = [_sigmoid(a0_ref[:, ps[p]] + _dot(al, a2_ref[:, ps[p]])) for p in pairs]
        g = [_dot(sg1, g2_ref[0:LANES, ps[p]]) + _dot(sg2, g2_ref[LANES:2 * LANES, ps[p]]) for p in pairs]
        r = [lerp(r_ref, rh_ref, mu_ref[:, ps[p]], r0, p * LANES, LANES) for p in pairs]
        k = [lerp(k_ref, kh_ref, mu_ref[:, D_GROUP + p * LANES:D_GROUP + (p + 1) * LANES], r0, p * LANES, LANES)
             for p in pairs]
        v = [lerp(v_ref, vh_ref, mu_ref[:, 2 * D_GROUP + p * LANES:2 * D_GROUP + (p + 1) * LANES], r0, p * LANES, LANES)
             for p in pairs]
        kk = [k[p] * kk_ref[:, ps[p]] for p in pairs]
        nrm = [jnp.sqrt(_half_lane_sums(kk[p] * kk[p], m0, lo)) for p in pairs]
        kk = [kk[p] / jnp.maximum(nrm[p], 1e-12) for p in pairs]
        k2 = [k[p] * (1.0 + (a[p] - 1.0) * ka_ref[:, ps[p]]) for p in pairs]
        rk = [_half_lane_sums(r[p] * k2[p] * rk_ref[:, ps[p]], m0, lo) for p in pairs]
        for p in pairs:
            gs_ref[p, rows, :] = g[p]
            bo_ref[p, rows, :] = rk[p] * v[p]
        cs = [_cumsum_rows(lw[p]) for p in pairs]
        w_inv = [jnp.exp(-cs[p]) for p in pairs]
        w_end = [jnp.exp(cs[p][CHUNK - 1:CHUNK, :]) for p in pairs]
        rt = [r[p] * jnp.exp(cs[p]) for p in pairs]
        at = [-kk[p] * jnp.exp(cs[p] - lw[p]) for p in pairs]
        bt = [kk[p] * a[p] * w_inv[p] for p in pairs]
        kt = [k2[p] * w_inv[p] for p in pairs]
        pm0 = [_dot_nt(jnp.concatenate([at[p] * m0, rt[p] * m0], axis=0), jnp.concatenate([bt[p], kt[p]], axis=0))
               for p in pairs]
        pm1 = [_dot_nt(jnp.concatenate([at[p] * m1, rt[p] * m1], axis=0), jnp.concatenate([kt[p], bt[p]], axis=0))
               for p in pairs]
        n_ab = [jnp.where(strict2, jnp.where(lo, pm0[p][:CHUNK], pm1[p][:CHUNK]), 0.0) for p in pairs]
        a_rb = [jnp.where(causal2, jnp.where(lo, pm0[p][CHUNK:], pm1[p][CHUNK:]), 0.0) for p in pairs]
        akrk = [jnp.concatenate([jnp.where(strict2, jnp.where(lo, pm1[p][:CHUNK], pm0[p][:CHUNK]), 0.0),
                                 jnp.where(causal2, jnp.where(lo, pm1[p][CHUNK:], pm0[p][CHUNK:]), 0.0)], axis=0)
                for p in pairs]
        vk = [_dot(akrk[p], jnp.concatenate([v[p] * m1, v[p] * m0], axis=0)) for p in pairs]
        inv = _inv_unit_lower_packed(n_ab, m0, m1)
        kv = [_dot((kt[p] * w_end[p]).T, v[p]) * same_head for p in pairs]
        for p in pairs:
            i = c * N_PAIRS + p
            atrt_ref[i] = jnp.concatenate([at[p], rt[p]], axis=0).astype(BF16)
            avk_ref[i] = vk[p][:CHUNK]
            yk_ref[i] = vk[p][CHUNK:]
            btT_ref[i] = (bt[p] * w_end[p]).T.astype(BF16)
            kvT_ref[i] = kv[p]
            wlT_ref[i] = jnp.broadcast_to(w_end[p], (LANES, LANES)).T
            inv_ref[i] = inv[p].astype(BF16)
            arb_ref[i] = a_rb[p].astype(BF16)

    for c in range(n_chunks):
        rows = slice(c * CHUNK, (c + 1) * CHUNK)
        it = [c * N_PAIRS + p for p in pairs]
        H = [H_ref[p] for p in pairs]
        xy0 = [_dot(atrt_ref[it[p]], H[p]) for p in pairs]
        x = [xy0[p][:CHUNK] + avk_ref[it[p]] for p in pairs]
        u = [_dot(inv_ref[it[p]], jnp.concatenate([x[p] * m0, x[p] * m1], axis=0)) for p in pairs]
        yb = [_dot(arb_ref[it[p]], jnp.concatenate([u[p] * m0, u[p] * m1], axis=0)) for p in pairs]
        bu = [_dot(btT_ref[it[p]], u[p]) for p in pairs]
        for p in pairs:
            y_ref[p, rows, :] = xy0[p][CHUNK:] + yk_ref[it[p]] + yb[p]
            H_ref[p] = H[p] * wlT_ref[it[p]] + bu[p] * same_head + kvT_ref[it[p]]

    inv_n = 1.0 / RWKV_HEAD
    for c in range(n_chunks):
        rows = slice(c * CHUNK, (c + 1) * CHUNK)
        y = [y_ref[p, rows, :] for p in pairs]
        yc = [y[p] - _half_lane_sums(y[p], m0, lo) * inv_n for p in pairs]
        var = [_half_lane_sums(yc[p] * yc[p], m0, lo) * inv_n for p in pairs]
        for p in pairs:
            yn = yc[p] * lax.rsqrt(var[p] + RWKV_LN_EPS) * lng_ref[:, ps[p]] + lnb_ref[:, ps[p]]
            out_ref[0, rows, ps[p]] = ((yn + bo_ref[p, rows, :]) * gs_ref[p, rows, :]).astype(out_ref.dtype)


def _rwkv_call(proj, mu, w0, w2, a0, a2, g2, k_k, k_a, r_k, ln_g, ln_b, *, col0, tb=256):
    B, T, _ = proj.shape
    j0 = col0 // D_GROUP
    l0 = (col0 + 3 * D_GROUP) // LANES
    row = lambda a: a.reshape(1, -1).astype(F32)
    mul = jnp.zeros((1, RWKV_LOW), F32).at[0, :288].set(mu[3 * D_GROUP:])
    w2p = jnp.zeros((LANES, D_GROUP), F32).at[:64].set(w2)
    a2p = jnp.zeros((LANES, D_GROUP), F32).at[64:].set(a2)
    g2p = jnp.zeros((2 * LANES, D_GROUP), F32).at[:160].set(g2)
    colblk = lambda j: pl.BlockSpec((1, tb, D_GROUP), lambda b, t: (b, t, j0 + j))
    lowblk = lambda j: pl.BlockSpec((1, tb, LANES), lambda b, t: (b, t, l0 + j))
    hrow = lambda t: jnp.maximum(t * (tb // 8) - 1, 0)
    halo = lambda j: pl.BlockSpec((1, 8, D_GROUP), lambda b, t: (b, hrow(t), j0 + j))
    lowhalo = lambda j: pl.BlockSpec((1, 8, LANES), lambda b, t: (b, hrow(t), l0 + j))
    full = lambda a: pl.BlockSpec(a.shape, lambda b, t: (0,) * a.ndim)
    params = [row(mu[:3 * D_GROUP]), mul, row(w0), w2p, row(a0), a2p, g2p, row(k_k), row(k_a), row(r_k),
              row(ln_g), row(ln_b)]
    big = pltpu.VMEM((N_PAIRS, tb, LANES), F32)
    n_items = tb // CHUNK * N_PAIRS
    return pl.pallas_call(
        functools.partial(_rwkv_kernel, tb=tb),
        grid=(B, T // tb),
        in_specs=[colblk(0), colblk(1), colblk(2), lowblk(0), lowblk(1), lowblk(2),
                  halo(0), halo(1), halo(2), lowhalo(0), lowhalo(1), lowhalo(2)] + [full(a) for a in params],
        out_specs=pl.BlockSpec((1, tb, D_GROUP), lambda b, t: (b, t, 0)),
        out_shape=jax.ShapeDtypeStruct((B, T, D_GROUP), BF16),
        scratch_shapes=[big] * 3 + [
            pltpu.VMEM((n_items, 2 * CHUNK, LANES), BF16),
            pltpu.VMEM((n_items, CHUNK, LANES), F32),
            pltpu.VMEM((n_items, CHUNK, LANES), F32),
            pltpu.VMEM((n_items, CHUNK, LANES), BF16),
            pltpu.VMEM((n_items, CHUNK, LANES), BF16),
            pltpu.VMEM((n_items, LANES, CHUNK), BF16),
            pltpu.VMEM((n_items, LANES, LANES), F32),
            pltpu.VMEM((n_items, LANES, LANES), F32),
            pltpu.VMEM((N_PAIRS, LANES, LANES), F32)],
        compiler_params=pltpu.CompilerParams(dimension_semantics=("arbitrary", "arbitrary"),
                                             vmem_limit_bytes=VMEM_LIMIT),
        name="rwkv7",
    )(*([proj] * 12), *params)


def _ada_kernel(c_ref, w_ref, b_ref, out_ref):
    sc = _silu(c_ref[...]).astype(BF16)
    out_ref[0] = jnp.dot(sc, w_ref[0].astype(BF16), preferred_element_type=F32) + b_ref[0]


def _ada_call(c, ada_w, ada_b, *, tn=1536):
    B = c.shape[0]
    n_mod = ada_w.shape[0] * ada_w.shape[1]
    w = ada_w.reshape(n_mod, D_MODEL, 3 * D_MODEL)
    b = ada_b.reshape(n_mod, 1, 3 * D_MODEL)
    return pl.pallas_call(
        _ada_kernel,
        grid=(n_mod, 3 * D_MODEL // tn),
        in_specs=[pl.BlockSpec((B, D_MODEL), lambda i, j: (0, 0)),
                  pl.BlockSpec((1, D_MODEL, tn), lambda i, j: (i, 0, j)),
                  pl.BlockSpec((1, 1, tn), lambda i, j: (i, 0, j))],
        out_specs=pl.BlockSpec((1, B, tn), lambda i, j: (i, 0, j)),
        out_shape=jax.ShapeDtypeStruct((n_mod, B, 3 * D_MODEL), F32),
        compiler_params=pltpu.CompilerParams(dimension_semantics=("arbitrary", "arbitrary"),
                                             vmem_limit_bytes=VMEM_LIMIT),
        name="adaln",
    )(c, w, b)


def _mod_spec(i, part, nb):
    return pl.BlockSpec((1, 1, D_MODEL), lambda b, t: (i * nb + b, 0, part))


def _modulate_kernel(x_ref, shift_ref, scale_ref, h_ref):
    h_ref[0] = (x_ref[0] * (1.0 + scale_ref[0]) + shift_ref[0]).astype(h_ref.dtype)


def _modulate_call(x, mods3, i, *, tb=512):
    B, T, _ = x.shape
    blk = pl.BlockSpec((1, tb, D_MODEL), lambda b, t: (b, t, 0))
    return pl.pallas_call(
        _modulate_kernel,
        grid=(B, T // tb),
        in_specs=[blk, _mod_spec(i, 0, B), _mod_spec(i, 1, B)],
        out_specs=blk,
        out_shape=jax.ShapeDtypeStruct(x.shape, BF16),
        compiler_params=pltpu.CompilerParams(dimension_semantics=("arbitrary", "arbitrary"),
                                             vmem_limit_bytes=VMEM_LIMIT),
        name="modulate",
    )(x, mods3, mods3)


LN_ROWS = 16


def _proj_ln_kernel(*refs, n_lhs, nk, tm, with_next):
    lhs = refs[:n_lhs]
    ws = refs[n_lhs:2 * n_lhs]
    x_ref, gate_ref, g_ref, b_ref = refs[2 * n_lhs:2 * n_lhs + 4]
    if with_next:
        shift_ref, scale_ref, xo_ref, h_ref, acc_ref = refs[2 * n_lhs + 4:]
    else:
        xo_ref, acc_ref = refs[2 * n_lhs + 4:]
    k = pl.program_id(2)
    def partial_product():
        part = jnp.dot(lhs[0][0], ws[0][...], preferred_element_type=F32)
        for j in range(1, n_lhs):
            part = part + jnp.dot(lhs[j][0], ws[j][...], preferred_element_type=F32)
        return part

    @pl.when(k == 0)
    def _():
        acc_ref[...] = partial_product()

    @pl.when(k > 0)
    def _():
        acc_ref[...] += partial_product()

    @pl.when(k == nk - 1)
    def _():
        gate1 = 1.0 + gate_ref[0]
        g, b = g_ref[...], b_ref[...]
        if with_next:
            scale1, shift = 1.0 + scale_ref[0], shift_ref[0]

        def rows_body(r, carry):
            rows = pl.ds(pl.multiple_of(r * LN_ROWS, LN_ROWS), LN_ROWS)
            z = ALPHA * x_ref[0, rows, :] + gate1 * acc_ref[rows, :]
            zc = z - jnp.mean(z, axis=-1, keepdims=True)
            var = jnp.mean(zc * zc, axis=-1, keepdims=True)
            xn = zc * lax.rsqrt(var + LN_EPS) * g + b
            xo_ref[0, rows, :] = xn
            if with_next:
                h_ref[0, rows, :] = (xn * scale1 + shift).astype(h_ref.dtype)
            return carry

        lax.fori_loop(0, tm // LN_ROWS, rows_body, 0, unroll=8)


def _proj_ln_call(lhs, ws, x, mods3, i, g, b, *, with_next, tm, tk):
    B, T, D = x.shape
    n_lhs = len(lhs)
    nk = lhs[0].shape[2] // tk
    blk = pl.BlockSpec((1, tm, D), lambda b, t, k: (b, t, 0))
    row = pl.BlockSpec((1, D), lambda b, t, k: (0, 0))
    mod = lambda ii, part: pl.BlockSpec((1, 1, D), lambda b, t, k: (ii * B + b, 0, part))
    wspec = lambda idx, k0: pl.BlockSpec((None, tk, D), lambda b, t, k: (idx, k0 + k, 0))
    in_specs = ([pl.BlockSpec((1, tm, tk), lambda b, t, k: (b, t, k))] * n_lhs
                + [wspec(idx, k0) for _, idx, k0 in ws]
                + [blk, mod(i, 2), row, row])
    args = list(lhs) + [w for w, _, _ in ws] + [x, mods3, g.reshape(1, D), b.reshape(1, D)]
    out_specs = [blk]
    out_shape = [jax.ShapeDtypeStruct(x.shape, F32)]
    if with_next:
        in_specs += [mod(i + 1, 0), mod(i + 1, 1)]
        args += [mods3, mods3]
        out_specs.append(blk)
        out_shape.append(jax.ShapeDtypeStruct(x.shape, BF16))
    return pl.pallas_call(
        functools.partial(_proj_ln_kernel, n_lhs=n_lhs, nk=nk, tm=tm, with_next=with_next),
        grid=(B, T // tm, nk),
        in_specs=in_specs, out_specs=out_specs, out_shape=out_shape,
        scratch_shapes=[pltpu.VMEM((tm, D), F32)],
        compiler_params=pltpu.CompilerParams(dimension_semantics=("arbitrary", "arbitrary", "arbitrary"),
                                             vmem_limit_bytes=VMEM_LIMIT),
        name="proj_residual_ln",
    )(*args)


def _matmul_kernel(a_ref, b_ref, o_ref, *scratch, nk):
    if nk == 1:
        o_ref[...] = jnp.dot(a_ref[...], b_ref[...], preferred_element_type=F32).astype(o_ref.dtype)
        return
    (acc_ref,) = scratch
    k = pl.program_id(2)

    @pl.when(k == 0)
    def _():
        acc_ref[...] = jnp.zeros_like(acc_ref)

    acc_ref[...] += jnp.dot(a_ref[...], b_ref[...], preferred_element_type=F32)

    @pl.when(k == nk - 1)
    def _():
        o_ref[...] = acc_ref[...].astype(o_ref.dtype)


def _matmul_call(a, b, *, tm, tn, tk, out_dtype=F32):
    M, K = a.shape
    _, N = b.shape
    nk = K // tk
    return pl.pallas_call(
        functools.partial(_matmul_kernel, nk=nk),
        grid=(N // tn, M // tm, nk),
        in_specs=[pl.BlockSpec((tm, tk), lambda j, i, k: (i, k)),
                  pl.BlockSpec((tk, tn), lambda j, i, k: (k, j))],
        out_specs=pl.BlockSpec((tm, tn), lambda j, i, k: (i, j)),
        out_shape=jax.ShapeDtypeStruct((M, N), out_dtype),
        scratch_shapes=[] if nk == 1 else [pltpu.VMEM((tm, tn), F32)],
        compiler_params=pltpu.CompilerParams(dimension_semantics=("arbitrary", "arbitrary", "arbitrary"),
                                             vmem_limit_bytes=VMEM_LIMIT),
        name="matmul",
    )(a, b)


XPOSE_ROWS = 256


def _matmul_f32wt_kernel(a_ref, wt_ref, o_ref, wb_ref):
    @pl.when(pl.program_id(1) == 0)
    def _():
        for r in range(0, wt_ref.shape[0], XPOSE_ROWS):
            wb_ref[:, r:r + XPOSE_ROWS] = wt_ref[r:r + XPOSE_ROWS, :].T.astype(BF16)

    o_ref[...] = jnp.dot(a_ref[...], wb_ref[...], preferred_element_type=F32)


def _matmul_f32wt_call(a, wt, idx, ncols, *, tm=1024, tn=1024):
    M, K = a.shape
    tm = min(tm, M)
    return pl.pallas_call(
        _matmul_f32wt_kernel,
        grid=(ncols // tn, M // tm),
        in_specs=[pl.BlockSpec((tm, K), lambda j, i: (i, 0)),
                  pl.BlockSpec((None, tn, K), lambda j, i: (idx, j, 0))],
        out_specs=pl.BlockSpec((tm, tn), lambda j, i: (i, j)),
        out_shape=jax.ShapeDtypeStruct((M, ncols), F32),
        scratch_shapes=[pltpu.VMEM((K, tn), BF16)],
        compiler_params=pltpu.CompilerParams(dimension_semantics=("arbitrary", "arbitrary"),
                                             vmem_limit_bytes=VMEM_LIMIT),
        name="matmul_f32wt",
    )(a, wt)


def _regroup_rows_kernel(wt_ref, o_ref, *, segments):
    tc = wt_ref.shape[1]
    pieces, pos = [], 0
    for src, width, dst in segments:
        if dst > pos:
            pieces.append(jnp.zeros((dst - pos, tc), F32))
        pieces.append(wt_ref[src:src + width, :])
        pos = dst + width
    if pos < o_ref.shape[1]:
        pieces.append(jnp.zeros((o_ref.shape[1] - pos, tc), F32))
    stacked = jnp.concatenate(pieces, axis=0)
    for r in range(0, o_ref.shape[1], LANES):
        o_ref[:, r:r + LANES] = stacked[r:r + LANES, :].T.astype(o_ref.dtype)


def _regroup_rows_call(wt, idx, segments, out_cols, *, block_rows, block_index, tc):
    _, _, K = wt.shape
    return pl.pallas_call(
        functools.partial(_regroup_rows_kernel, segments=segments),
        grid=(K // tc,),
        in_specs=[pl.BlockSpec((None, block_rows, tc), lambda c: (idx, block_index, c))],
        out_specs=pl.BlockSpec((tc, out_cols), lambda c: (c, 0)),
        out_shape=jax.ShapeDtypeStruct((K, out_cols), BF16),
        compiler_params=pltpu.CompilerParams(dimension_semantics=("arbitrary",), vmem_limit_bytes=VMEM_LIMIT),
        name="regroup_rows",
    )(wt)


def _swiglu_up_kernel(h_ref, wg_ref, wu_ref, o_ref, wgb_ref, wub_ref):
    @pl.when(pl.program_id(1) == 0)
    def _():
        wgb_ref[...] = wg_ref[...].astype(BF16)
        wub_ref[...] = wu_ref[...].astype(BF16)

    h = h_ref[...]
    g = jnp.dot(h, wgb_ref[...], preferred_element_type=F32)
    u = jnp.dot(h, wub_ref[...], preferred_element_type=F32)
    o_ref[...] = (_silu(g) * u).astype(o_ref.dtype)


def _swiglu_up_call(h, wg, wu, layer, *, tm=1024, tn=512):
    M, K = h.shape
    _, _, N = wg.shape
    tm = min(tm, M)
    wspec = pl.BlockSpec((None, K, tn), lambda j, i: (layer, 0, j))
    return pl.pallas_call(
        _swiglu_up_kernel,
        grid=(N // tn, M // tm),
        in_specs=[pl.BlockSpec((tm, K), lambda j, i: (i, 0)), wspec, wspec],
        out_specs=pl.BlockSpec((tm, tn), lambda j, i: (i, j)),
        out_shape=jax.ShapeDtypeStruct((M, N), BF16),
        scratch_shapes=[pltpu.VMEM((K, tn), BF16), pltpu.VMEM((K, tn), BF16)],
        compiler_params=pltpu.CompilerParams(dimension_semantics=("arbitrary", "arbitrary"),
                                             vmem_limit_bytes=VMEM_LIMIT),
        name="swiglu_up",
    )(h, wg, wu)


A_MAIN = 4 * D_GROUP
B_COLS_PAD = 3 * D_GROUP + RWKV_LOW + LANES
MLSTM_GATE_COL = 3 * D_GROUP + RWKV_LOW
CD_MAIN = 8 * D_GROUP
GDN_COL0 = 4 * D_GROUP


N_GATES = 2 * N_HEADS
B_SRC = A_MAIN + N_GATES
B_LOW = 64 + 64 + 160
B_SEGMENTS = ((B_SRC, 3 * D_GROUP, 0), (B_SRC + 3 * D_GROUP, B_LOW, 3 * D_GROUP), (A_MAIN, N_GATES, MLSTM_GATE_COL))
CD_GATE_SEGMENTS = ((0, N_GATES, 0),)


def kernel(x, c, positions, ada_w, ada_b, ln_g, ln_b, ab_w_in, ab_w_out, mlstm_conv_w, mlstm_gate_b, mlstm_norm_g, rwkv_mu, rwkv_w0, rwkv_w2, rwkv_a0, rwkv_a2, rwkv_g2, rwkv_k_k, rwkv_k_a, rwkv_r_k, rwkv_ln_g, rwkv_ln_b, cd_w_in, cd_w_out, ret_norm_g, gdn_conv_w, gdn_a_log, gdn_dt_bias, gdn_norm_g, ffn_w_gate, ffn_w_up, ffn_w_down):
    B, T, D = x.shape
    M = B * T
    depth = ada_w.shape[0]
    mods = _ada_call(c, ada_w, ada_b)
    mods3 = mods.reshape(2 * depth * B, 1, 3 * D)
    cos2, sin2 = _rope_call(positions)
    h = _modulate_call(x, mods3, 0)
    w_down = ffn_w_down.astype(BF16)
    for layer in range(depth):
        j = layer // 2
        i_mix, i_ffn = 2 * layer, 2 * layer + 1
        if layer % 2 == 0:
            h2 = h.reshape(M, D)
            wt = jnp.swapaxes(ab_w_in, 1, 2)
            proj_a = _matmul_f32wt_call(h2, wt, j, A_MAIN).reshape(B, T, A_MAIN)
            w_b = _regroup_rows_call(wt, j, B_SEGMENTS, B_COLS_PAD, block_rows=wt.shape[1], block_index=0, tc=256)
            proj_b = _matmul_call(h2, w_b, tm=512, tn=B_COLS_PAD // 2, tk=D)
            proj_b = proj_b.reshape(B, T, B_COLS_PAD)
            ya = _mlstm_call(proj_a, proj_b, mlstm_conv_w[j], mlstm_gate_b[j], mlstm_norm_g[j],
                             gate_col=MLSTM_GATE_COL)
            yb = _rwkv_call(proj_b, rwkv_mu[j], rwkv_w0[j], rwkv_w2[j], rwkv_a0[j], rwkv_a2[j], rwkv_g2[j],
                            rwkv_k_k[j], rwkv_k_a[j], rwkv_r_k[j], rwkv_ln_g[j], rwkv_ln_b[j], col0=0)
            w_out = ab_w_out
        else:
            h2 = h.reshape(M, D)
            wt = jnp.swapaxes(cd_w_in, 1, 2)
            proj = _matmul_f32wt_call(h2, wt, j, CD_MAIN).reshape(B, T, CD_MAIN)
            w_gates = _regroup_rows_call(wt, j, CD_GATE_SEGMENTS, LANES, block_rows=N_GATES,
                                         block_index=CD_MAIN // N_GATES, tc=D)
            proj_g = _matmul_call(h2, w_gates, tm=512, tn=LANES, tk=D).reshape(B, T, LANES)
            ya = _ret_call(proj, cos2, sin2, ret_norm_g[j])
            yb = _gdn_call(proj, proj_g, gdn_conv_w[j], gdn_a_log[j], gdn_dt_bias[j], gdn_norm_g[j],
                           col0=GDN_COL0, gate_col=0)
            w_out = cd_w_out
        w_out = w_out.astype(BF16)
        x, h = _proj_ln_call([ya, yb], [(w_out, j, 0), (w_out, j, 1)], x, mods3, i_mix,
                             ln_g[layer, 0], ln_b[layer, 0], with_next=True, tm=256, tk=D_GROUP)
        act = _swiglu_up_call(h.reshape(M, D), ffn_w_gate, ffn_w_up, layer)
        last = layer == depth - 1
        res = _proj_ln_call([act.reshape(B, T, D_FF)], [(w_down, layer, 0)], x, mods3, i_ffn,
                            ln_g[layer, 1], ln_b[layer, 1], with_next=not last, tm=512, tk=1408)
        if last:
            (x,) = res
        else:
            x, h = res
    return x
```

```python
import functools
import math

import numpy as np
import jax
import jax.numpy as jnp
from jax import lax
from jax.experimental import pallas as pl
from jax.experimental.pallas import tpu as pltpu

F32 = jnp.float32
BF16 = jnp.bfloat16

D_MODEL = 2048
D_GROUP = 1024
HEAD_DIM = 128
N_HEADS = 8
RWKV_HEAD = 64
N_RWKV = 16
CHUNK = 64
D_FF = 5632
DEPTH = 2
ALPHA = (2 * DEPTH) ** 0.25
LN_EPS = 1e-5
RWKV_LN_EPS = 64e-5
ROPE_BASE = 10000.0
RET_GAMMA_BASE = 5.0
LANES = 128
VMEM_LIMIT = 48 * 1024 * 1024


def _dot(a, b):
    return jnp.dot(a.astype(BF16), b.astype(BF16), preferred_element_type=F32)


def _dot_nt(a, b):
    return lax.dot_general(a.astype(BF16), b.astype(BF16), (((1,), (1,)), ((), ())),
                           preferred_element_type=F32)


def _split3(x):
    hi = x.astype(BF16)
    r1 = x - hi.astype(F32)
    mid = r1.astype(BF16)
    lo = (r1 - mid.astype(F32)).astype(BF16)
    return hi, mid, lo


def _cumsum_rows(x):
    n = x.shape[1]
    out = jnp.dot(_tri(CHUNK).astype(BF16), jnp.concatenate(_split3(x), axis=1), preferred_element_type=F32)
    return out[:, :n] + out[:, n:2 * n] + out[:, 2 * n:]


def _half_lane_sums(x, m0, lo):
    s0 = jnp.sum(x * m0, axis=-1, keepdims=True)
    s1 = jnp.sum(x * (1.0 - m0), axis=-1, keepdims=True)
    return jnp.where(lo, s0, s1)


def _sigmoid(x):
    return 1.0 / (1.0 + jnp.exp(-x))


def _silu(x):
    return x * _sigmoid(x)


def _log_sigmoid(x):
    return jnp.minimum(x, 0.0) - jnp.log1p(jnp.exp(-jnp.abs(x)))


def _softplus(x):
    return jnp.maximum(x, 0.0) + jnp.log1p(jnp.exp(-jnp.abs(x)))


def _tri(n, strict=False):
    r = lax.broadcasted_iota(jnp.int32, (n, n), 0)
    c = lax.broadcasted_iota(jnp.int32, (n, n), 1)
    return (r > c) if strict else (r >= c)


def _conv_silu_rows(src_ref, halo_ref, w, first, r0, nrows, c0, ncols):
    cur = src_ref[0, r0:r0 + nrows, c0:c0 + ncols]
    acc = w[3:4] * cur
    if r0 == 0:
        hl = jnp.where(first, 0.0, halo_ref[0, :, c0:c0 + ncols])
        ext = jnp.concatenate([hl, cur[0:8]], axis=0)
        for j in range(3):
            head = ext[5 + j:13 + j]
            if nrows > 8:
                rest = src_ref[0, 5 + j:nrows - 3 + j, c0:c0 + ncols]
                sh = jnp.concatenate([head, rest], axis=0)
            else:
                sh = head
            acc = acc + w[j:j + 1] * sh
    else:
        for j in range(3):
            acc = acc + w[j:j + 1] * src_ref[0, r0 - 3 + j:r0 - 3 + j + nrows, c0:c0 + ncols]
    return _silu(acc)


def _head_norm_rows(h, g_row, eps, center=True):
    if center:
        h = h - jnp.mean(h, axis=-1, keepdims=True)
    return h * lax.rsqrt(jnp.mean(h * h, axis=-1, keepdims=True) + eps) * g_row


MLSTM_CHUNK_GROUP = 2


def _cummax_rows(x):
    row = lax.broadcasted_iota(jnp.int32, x.shape, 0)
    d = 1
    while d < x.shape[0]:
        x = jnp.where(row >= d, jnp.maximum(x, pltpu.roll(x, d, 0)), x)
        d *= 2
    return x


def _bcast_head_cols(x, sel, pieces):
    m = x.shape[0]
    parts, rest = [], x
    for _ in range(pieces):
        hi = rest.astype(BF16)
        parts.append(hi)
        rest = rest - hi.astype(F32)
    out = jnp.dot(jnp.concatenate(parts, axis=0), sel, preferred_element_type=F32)
    acc = out[:m]
    for i in range(1, pieces):
        acc = acc + out[i * m:(i + 1) * m]
    return acc


def _mlstm_kernel(q_ref, k_ref, v_ref, o_ref, g_ref, qh_ref, kh_ref, cw_ref, gb_ref, ng_ref,
                  out_ref, qc_ref, kc_ref, sv_ref, rs_ref, hh_ref, b0_ref, cm_ref, kvn_ref, CN_ref, m_ref, *, tb):
    t = pl.program_id(1)
    first = t == 0

    @pl.when(first)
    def _():
        CN_ref[...] = jnp.zeros_like(CN_ref)
        m_ref[...] = jnp.zeros_like(m_ref)

    for c in range(tb // CHUNK):
        for cb in range(D_GROUP // 256):
            cs = cb * 256
            qc_ref[c * CHUNK:(c + 1) * CHUNK, cs:cs + 256] = _conv_silu_rows(
                q_ref, qh_ref, cw_ref[:, cs:cs + 256], first, c * CHUNK, CHUNK, cs, 256)
            kc_ref[c * CHUNK:(c + 1) * CHUNK, cs:cs + 256] = _conv_silu_rows(
                k_ref, kh_ref, cw_ref[:, D_GROUP + cs:D_GROUP + cs + 256], first, c * CHUNK, CHUNK, cs, 256)

    causal = _tri(CHUNK)
    gb = gb_ref[...]
    scale = HEAD_DIM ** -0.5
    n_chunks = tb // CHUNK
    heads = range(N_HEADS)
    hs = [slice(h * HEAD_DIM, (h + 1) * HEAD_DIM) for h in heads]
    head_lane = lax.broadcasted_iota(jnp.int32, (1, LANES), 1) < N_HEADS
    sel = (lax.broadcasted_iota(jnp.int32, (LANES, N_HEADS * LANES), 0)
           == (lax.broadcasted_iota(jnp.int32, (LANES, N_HEADS * LANES), 1) >> 7)).astype(BF16)
    ones = jnp.ones((CHUNK, HEAD_DIM), F32)
    last = slice(CHUNK - 1, CHUNK)

    for c0 in range(0, n_chunks, MLSTM_CHUNK_GROUP):
        chunks = range(c0, c0 + MLSTM_CHUNK_GROUP)
        rows = {c: slice(c * CHUNK, (c + 1) * CHUNK) for c in chunks}
        z = {c: g_ref[0, rows[c], :] + gb for c in chunks}
        b0 = {c: pltpu.roll(_cumsum_rows(_log_sigmoid(z[c])), LANES - N_HEADS, 1) for c in chunks}
        cv = {c: jnp.where(head_lane, z[c] - b0[c], 0.0) for c in chunks}
        cm = {c: _cummax_rows(cv[c]) for c in chunks}
        cT = {c: cv[c].T for c in chunks}
        cmb = {c: _bcast_head_cols(cm[c], sel, 3) for c in chunks}
        e1b = {c: _bcast_head_cols(jnp.where(head_lane, jnp.exp(cv[c] - cm[c][last]), 0.0), sel, 2) for c in chunks}
        for c in chunks:
            b0_ref[rows[c], :] = b0[c]
            cm_ref[rows[c], :] = cm[c]
        items = [(c, h) for c in chunks for h in heads]
        n_items = range(len(items))
        q = [qc_ref[rows[c], hs[h]] for c, h in items]
        k = [kc_ref[rows[c], hs[h]] * scale for c, h in items]
        vo = [jnp.concatenate([v_ref[0, rows[c], hs[h]], ones], axis=1).astype(BF16) for c, h in items]
        qk = [_dot_nt(q[i], k[i]) for i in n_items]
        s = [qk[i] * jnp.where(causal, jnp.exp(jnp.minimum(cT[c][h:h + 1, :] - cmb[c][:, h * LANES:h * LANES + CHUNK],
                                                           0.0)), 0.0) for i, (c, h) in enumerate(items)]
        s_hi = [s[i].astype(BF16) for i in n_items]
        s_lo = [(s[i] - s_hi[i].astype(F32)).astype(BF16) for i in n_items]
        svr = [jnp.dot(jnp.concatenate([s_hi[i], s_lo[i]], axis=0), vo[i], preferred_element_type=F32)
               for i in n_items]
        kvn = [_dot((k[i] * e1b[c][:, hs[h]]).T, vo[i]) for i, (c, h) in enumerate(items)]
        for i, (c, h) in enumerate(items):
            sv_ref[rows[c], hs[h]] = svr[i][:CHUNK, :HEAD_DIM] + svr[i][CHUNK:, :HEAD_DIM]
            rs_ref[rows[c], hs[h]] = svr[i][:CHUNK, HEAD_DIM:] + svr[i][CHUNK:, HEAD_DIM:]
            kvn_ref[c * N_HEADS + h] = kvn[i]

    for c in range(n_chunks):
        rows = slice(c * CHUNK, (c + 1) * CHUNK)
        m = m_ref[0:1, :]
        b0 = b0_ref[rows, :]
        cm = cm_ref[rows, :]
        mx = jnp.maximum(cm, m)
        m_new = jnp.maximum(b0[last] + m, b0[last] + cm[last])
        m_ref[0:1, :] = m_new
        zero = lambda x: jnp.where(head_lane, x, 0.0)
        fib = _bcast_head_cols(zero(jnp.exp(cm - mx)), sel, 2)
        scb = _bcast_head_cols(zero(jnp.exp(m - mx)), sel, 2)
        emtb = _bcast_head_cols(zero(jnp.exp(jnp.minimum(-(b0 + mx), 80.0))), sel, 2)
        dfb = _bcast_head_cols(jnp.concatenate([zero(jnp.exp(b0[last] + m - m_new)),
                                                zero(jnp.exp(b0[last] + cm[last] - m_new)),
                                                jnp.zeros((6, LANES), F32)], axis=0), sel, 2)
        CN = [CN_ref[h] for h in heads]
        qcn = [_dot(qc_ref[rows, hs[h]], CN[h]) for h in heads]
        for h in heads:
            num = fib[:, hs[h]] * sv_ref[rows, hs[h]] + scb[:, hs[h]] * qcn[h][:, :HEAD_DIM]
            den = fib[:, hs[h]] * rs_ref[rows, hs[h]] + scb[:, hs[h]] * qcn[h][:, HEAD_DIM:]
            hh_ref[rows, hs[h]] = num / jnp.maximum(jnp.abs(den), emtb[:, hs[h]])
            dec = jnp.concatenate([dfb[0:1, hs[h]]] * 2, axis=1)
            fkv = jnp.concatenate([dfb[1:2, hs[h]]] * 2, axis=1)
            CN_ref[h] = CN[h] * dec + kvn_ref[c * N_HEADS + h] * fkv

    for c in range(n_chunks):
        rows = slice(c * CHUNK, (c + 1) * CHUNK)
        hn = [_head_norm_rows(hh_ref[rows, hs[h]], ng_ref[:, hs[h]], LN_EPS) for h in heads]
        for h in heads:
            out_ref[0, rows, hs[h]] = (hn[h] * _sigmoid(o_ref[0, rows, hs[h]])).astype(out_ref.dtype)


def _mlstm_call(proj, gproj, conv_w, gate_b, norm_g, *, gate_col, tb=256):
    B, T, _ = proj.shape
    nt = T // tb
    gb = jnp.zeros((1, LANES), F32).at[0, :2 * N_HEADS].set(gate_b)
    ng = norm_g.reshape(1, D_GROUP)
    colblk = lambda j: pl.BlockSpec((1, tb, D_GROUP), lambda b, t: (b, t, j))
    halo = lambda j: pl.BlockSpec((1, 8, D_GROUP), lambda b, t: (b, jnp.maximum(t * (tb // 8) - 1, 0), j))
    return pl.pallas_call(
        functools.partial(_mlstm_kernel, tb=tb),
        grid=(B, nt),
        in_specs=[colblk(0), colblk(1), colblk(2), colblk(3),
                  pl.BlockSpec((1, tb, LANES), lambda b, t: (b, t, gate_col // LANES)),
                  halo(0), halo(1),
                  pl.BlockSpec((4, 2 * D_GROUP), lambda b, t: (0, 0)),
                  pl.BlockSpec((1, LANES), lambda b, t: (0, 0)),
                  pl.BlockSpec((1, D_GROUP), lambda b, t: (0, 0))],
        out_specs=pl.BlockSpec((1, tb, D_GROUP), lambda b, t: (b, t, 0)),
        out_shape=jax.ShapeDtypeStruct((B, T, D_GROUP), BF16),
        scratch_shapes=[pltpu.VMEM((tb, D_GROUP), F32)] * 5
                       + [pltpu.VMEM((tb, LANES), F32)] * 2
                       + [pltpu.VMEM((tb // CHUNK * N_HEADS, HEAD_DIM, 2 * HEAD_DIM), F32),
                          pltpu.VMEM((N_HEADS, HEAD_DIM, 2 * HEAD_DIM), F32),
                          pltpu.VMEM((8, LANES), F32)],
        compiler_params=pltpu.CompilerParams(dimension_semantics=("arbitrary", "arbitrary"),
                                             vmem_limit_bytes=VMEM_LIMIT),
        name="mlstm",
    )(proj, proj, proj, proj, gproj, proj, proj, conv_w, gb, ng)


def _rope_kernel(pos_ref, inv_ref, cos_ref, sin_ref):
    ang = pos_ref[0].astype(F32) * inv_ref[...]
    lane = lax.broadcasted_iota(jnp.int32, ang.shape, 1)
    cos_ref[0] = jnp.cos(ang)
    sin_ref[0] = jnp.where(lane < HEAD_DIM // 2, -jnp.sin(ang), jnp.sin(ang))


def _rope_call(positions, *, tb=512):
    B, T = positions.shape
    half = HEAD_DIM // 2
    inv_freq = ROPE_BASE ** (-jnp.arange(half, dtype=F32) / half)
    inv2 = jnp.concatenate([inv_freq, inv_freq]).reshape(1, HEAD_DIM)
    spec = pl.BlockSpec((1, tb, HEAD_DIM), lambda b, t: (b, t, 0))
    return pl.pallas_call(
        _rope_kernel,
        grid=(B, T // tb),
        in_specs=[pl.BlockSpec((1, tb, 1), lambda b, t: (b, t, 0)),
                  pl.BlockSpec((1, HEAD_DIM), lambda b, t: (0, 0))],
        out_specs=[spec, spec],
        out_shape=[jax.ShapeDtypeStruct((B, T, HEAD_DIM), F32)] * 2,
        compiler_params=pltpu.CompilerParams(dimension_semantics=("arbitrary", "arbitrary")),
        name="rope_table",
    )(positions.reshape(B, T, 1), inv2)


def _ret_kernel(q_ref, k_ref, v_ref, g_ref, cos_ref, sin_ref, ng_ref, out_ref, R_ref, *, tb):
    t = pl.program_id(1)

    @pl.when(t == 0)
    def _():
        R_ref[...] = jnp.zeros_like(R_ref)

    causal = _tri(CHUNK)
    ri = lax.broadcasted_iota(jnp.int32, (CHUNK, CHUNK), 0)
    ci = lax.broadcasted_iota(jnp.int32, (CHUNK, CHUNK), 1)
    rel = (ri - ci).astype(F32)
    tcol = lax.broadcasted_iota(jnp.int32, (CHUNK, 1), 0).astype(F32)
    scale = HEAD_DIM ** -0.5

    def chunk_body(c, carry):
        r0 = pl.multiple_of(c * CHUNK, CHUNK)
        rows = pl.ds(r0, CHUNK)
        cos2 = cos_ref[0, rows, :]
        sin2 = sin_ref[0, rows, :]
        for h in range(N_HEADS):
            hs = slice(h * HEAD_DIM, (h + 1) * HEAD_DIM)
            lg = math.log1p(-2.0 ** (-RET_GAMMA_BASE - h))
            q = q_ref[0, rows, hs]
            k = k_ref[0, rows, hs]
            v = v_ref[0, rows, hs]
            qr = q * cos2 + pltpu.roll(q, HEAD_DIM // 2, 1) * sin2
            kr = (k * cos2 + pltpu.roll(k, HEAD_DIM // 2, 1) * sin2) * scale
            dmat = jnp.where(causal, jnp.exp(rel * lg), 0.0)
            xi = jnp.exp((tcol + 1.0) * lg)
            zeta = jnp.exp((CHUNK - 1.0 - tcol) * lg)
            Rst = R_ref[h]
            intra = _dot(_dot_nt(qr, kr) * dmat, v)
            inter = _dot(qr, Rst) * xi
            R_ref[h] = Rst * math.exp(CHUNK * lg) + _dot((kr * zeta).T, v)
            on = _head_norm_rows(intra + inter, ng_ref[:, hs], LN_EPS)
            out_ref[0, rows, hs] = (on * _silu(g_ref[0, rows, hs])).astype(out_ref.dtype)
        return carry

    lax.fori_loop(0, tb // CHUNK, chunk_body, 0)


def _ret_call(proj, cos2, sin2, norm_g, *, tb=256):
    B, T, _ = proj.shape
    colblk = lambda j: pl.BlockSpec((1, tb, D_GROUP), lambda b, t: (b, t, j))
    tab = pl.BlockSpec((1, tb, HEAD_DIM), lambda b, t: (b, t, 0))
    return pl.pallas_call(
        functools.partial(_ret_kernel, tb=tb),
        grid=(B, T // tb),
        in_specs=[colblk(0), colblk(1), colblk(2), colblk(3), tab, tab,
                  pl.BlockSpec((1, D_GROUP), lambda b, t: (0, 0))],
        out_specs=pl.BlockSpec((1, tb, D_GROUP), lambda b, t: (b, t, 0)),
        out_shape=jax.ShapeDtypeStruct((B, T, D_GROUP), BF16),
        scratch_shapes=[pltpu.VMEM((N_HEADS, HEAD_DIM, HEAD_DIM), F32)],
        compiler_params=pltpu.CompilerParams(dimension_semantics=("arbitrary", "arbitrary"),
                                             vmem_limit_bytes=VMEM_LIMIT),
        name="retention",
    )(proj, proj, proj, proj, cos2, sin2, norm_g.reshape(1, D_GROUP))


def _inv_unit_lower(nms):
    n = nms[0].shape[0]
    eye = (lax.broadcasted_iota(jnp.int32, (n, n), 0) == lax.broadcasted_iota(jnp.int32, (n, n), 1)).astype(F32)
    ps = [eye + nm for nm in nms]
    xs = [_dot(nm, nm) for nm in nms]
    for _ in range(int(math.log2(n)) - 2):
        px = [_dot(jnp.concatenate([p, x], axis=0), x) for p, x in zip(ps, xs)]
        ps = [p + y[:n] for p, y in zip(ps, px)]
        xs = [y[n:] for y in px]
    ps = [p + _dot(p, x) for p, x in zip(ps, xs)]
    resid = [eye - p + _dot(nm, p) for p, nm in zip(ps, nms)]
    return [p + _dot(p, r) for p, r in zip(ps, resid)]


def _inv_unit_lower_packed(nms, m0, m1):
    n = nms[0].shape[0]
    r = lax.broadcasted_iota(jnp.int32, (n, 2 * n), 0)
    c = lax.broadcasted_iota(jnp.int32, (n, 2 * n), 1)
    eye2 = (r == (c & (n - 1))).astype(F32)
    bd = lambda x: jnp.concatenate([x * m0, x * m1], axis=0)
    ps = [eye2 + nm for nm in nms]
    xs = [_dot(nm, bd(nm)) for nm in nms]
    for _ in range(int(math.log2(n)) - 2):
        px = [_dot(jnp.concatenate([p, x], axis=0), bd(x)) for p, x in zip(ps, xs)]
        ps = [p + y[:n] for p, y in zip(ps, px)]
        xs = [y[n:] for y in px]
    ps = [p + _dot(p, bd(x)) for p, x in zip(ps, xs)]
    resid = [eye2 - p + _dot(nm, bd(p)) for p, nm in zip(ps, nms)]
    return [p + _dot(p, bd(r_)) for p, r_ in zip(ps, resid)]


def _solve_unit_lower(nms, rhss):
    n = nms[0].shape[0]
    eye = (lax.broadcasted_iota(jnp.int32, (n, n), 0) == lax.broadcasted_iota(jnp.int32, (n, n), 1)).astype(F32)
    ps = [eye + nm for nm in nms]
    xs = [_dot(nm, nm) for nm in nms]
    for _ in range(int(math.log2(n)) - 2):
        px = [_dot(jnp.concatenate([p, x], axis=0), x) for p, x in zip(ps, xs)]
        ps = [p + y[:n] for p, y in zip(ps, px)]
        xs = [y[n:] for y in px]
    ps = [p + _dot(p, x) for p, x in zip(ps, xs)]
    x0 = [_dot(p, r) for p, r in zip(ps, rhss)]
    resid = [r - a + _dot(nm, a) for r, a, nm in zip(rhss, x0, nms)]
    return [a + _dot(p, r) for a, p, r in zip(x0, ps, resid)]


def _l2norm_rows(z):
    return z * lax.rsqrt(jnp.sum(z * z, axis=-1, keepdims=True) + 1e-6)


GDN_CHUNK_GROUP = 2


def _gdn_kernel(q_ref, k_ref, v_ref, z_ref, g_ref, qh_ref, kh_ref, vh_ref, cw_ref, an_ref, dt_ref, ng_ref,
                out_ref, qc_ref, kc_ref, vc_ref, u_ref, w_ref, qe_ref, o_ref, att_ref, kdT_ref, gl_ref, S_ref,
                *, tb):
    t = pl.program_id(1)
    first = t == 0

    @pl.when(first)
    def _():
        S_ref[...] = jnp.zeros_like(S_ref)

    srcs = ((q_ref, qh_ref, qc_ref), (k_ref, kh_ref, kc_ref), (v_ref, vh_ref, vc_ref))
    for c in range(tb // CHUNK):
        for cb in range(D_GROUP // 256):
            cs = cb * 256
            for i, (src, halo, dst) in enumerate(srcs):
                w = cw_ref[:, i * D_GROUP + cs:i * D_GROUP + cs + 256]
                dst[c * CHUNK:(c + 1) * CHUNK, cs:cs + 256] = _conv_silu_rows(
                    src, halo, w, first, c * CHUNK, CHUNK, cs, 256)

    causal = _tri(CHUNK)
    strict = _tri(CHUNK, strict=True)
    a_neg = an_ref[...]
    dtb = dt_ref[...]
    scale = HEAD_DIM ** -0.5
    n_chunks = tb // CHUNK
    heads = range(N_HEADS)
    hs = [slice(h * HEAD_DIM, (h + 1) * HEAD_DIM) for h in heads]

    for c0 in range(0, n_chunks, GDN_CHUNK_GROUP):
        chunks = range(c0, c0 + GDN_CHUNK_GROUP)
        rows = {c: slice(c * CHUNK, (c + 1) * CHUNK) for c in chunks}
        gz = {c: g_ref[0, rows[c], :] for c in chunks}
        beta = {c: _sigmoid(gz[c]) for c in chunks}
        gc = {c: _cumsum_rows(a_neg * _softplus(gz[c] + dtb)) for c in chunks}
        gcT = {c: gc[c].T for c in chunks}
        for c in chunks:
            gl_ref[c:c + 1, :] = gc[c][CHUNK - 1:CHUNK, :]
        items = [(c, h) for c in chunks for h in heads]
        n_items = range(len(items))
        gc_col = [gc[c][:, h:h + 1] for c, h in items]
        b_col = [beta[c][:, 8 + h:9 + h] for c, h in items]
        gamma = [jnp.where(causal, jnp.exp(gc_col[i] - gcT[c][h:h + 1, :]), 0.0) for i, (c, h) in enumerate(items)]
        q = [_l2norm_rows(qc_ref[rows[c], hs[h]]) * scale for c, h in items]
        k = [_l2norm_rows(kc_ref[rows[c], hs[h]]) for c, h in items]
        kb = [k[i] * b_col[i] for i in n_items]
        eg = [jnp.exp(gc_col[i]) for i in n_items]
        kq = [_dot_nt(jnp.concatenate([kb[i], q[i]], axis=0), k[i]) for i in n_items]
        inv = _inv_unit_lower([-jnp.where(strict, kq[i][:CHUNK] * gamma[i], 0.0) for i in n_items])
        uw = [_dot(inv[i], jnp.concatenate([vc_ref[rows[c], hs[h]] * b_col[i], kb[i] * eg[i]], axis=1))
              for i, (c, h) in enumerate(items)]
        for i, (c, h) in enumerate(items):
            u_ref[rows[c], hs[h]] = uw[i][:, :HEAD_DIM]
            w_ref[rows[c], hs[h]] = uw[i][:, HEAD_DIM:]
            qe_ref[rows[c], hs[h]] = q[i] * eg[i]
            att_ref[h, rows[c], :] = kq[i][CHUNK:] * gamma[i]
            g_last = gc[c][CHUNK - 1:CHUNK, h:h + 1]
            kdT_ref[c * N_HEADS + h] = (k[i] * jnp.exp(g_last - gc_col[i])).T

    for c in range(n_chunks):
        rows = slice(c * CHUNK, (c + 1) * CHUNK)
        S = [S_ref[h] for h in heads]
        ws = [_dot(jnp.concatenate([w_ref[rows, hs[h]], qe_ref[rows, hs[h]]], axis=0), S[h]) for h in heads]
        v_new = [u_ref[rows, hs[h]] - ws[h][:CHUNK] for h in heads]
        av = [_dot(att_ref[h, rows, :], v_new[h]) for h in heads]
        kv = [_dot(kdT_ref[c * N_HEADS + h], v_new[h]) for h in heads]
        for h in heads:
            S_ref[h] = S[h] * jnp.exp(gl_ref[c:c + 1, h:h + 1]) + kv[h]
            o_ref[rows, hs[h]] = ws[h][CHUNK:] + av[h]

    for c in range(n_chunks):
        rows = slice(c * CHUNK, (c + 1) * CHUNK)
        for h in heads:
            on = _head_norm_rows(o_ref[rows, hs[h]], ng_ref[:, hs[h]], 1e-6, center=False)
            out_ref[0, rows, hs[h]] = (on * _silu(z_ref[0, rows, hs[h]])).astype(out_ref.dtype)


def _gdn_call(proj, gproj, conv_w, a_log, dt_bias, norm_g, *, col0, gate_col, tb=256):
    B, T, _ = proj.shape
    j0 = col0 // D_GROUP
    an = jnp.zeros((1, LANES), F32).at[0, :N_HEADS].set(-jnp.exp(a_log.astype(F32)))
    dtb = jnp.zeros((1, LANES), F32).at[0, :N_HEADS].set(dt_bias)
    colblk = lambda j: pl.BlockSpec((1, tb, D_GROUP), lambda b, t: (b, t, j0 + j))
    halo = lambda j: pl.BlockSpec((1, 8, D_GROUP), lambda b, t: (b, jnp.maximum(t * (tb // 8) - 1, 0), j0 + j))
    return pl.pallas_call(
        functools.partial(_gdn_kernel, tb=tb),
        grid=(B, T // tb),
        in_specs=[colblk(0), colblk(1), colblk(2), colblk(3),
                  pl.BlockSpec((1, tb, LANES), lambda b, t: (b, t, gate_col // LANES)),
                  halo(0), halo(1), halo(2),
                  pl.BlockSpec((4, 3 * D_GROUP), lambda b, t: (0, 0)),
                  pl.BlockSpec((1, LANES), lambda b, t: (0, 0)),
                  pl.BlockSpec((1, LANES), lambda b, t: (0, 0)),
                  pl.BlockSpec((1, D_GROUP), lambda b, t: (0, 0))],
        out_specs=pl.BlockSpec((1, tb, D_GROUP), lambda b, t: (b, t, 0)),
        out_shape=jax.ShapeDtypeStruct((B, T, D_GROUP), BF16),
        scratch_shapes=[pltpu.VMEM((tb, D_GROUP), F32)] * 7
                       + [pltpu.VMEM((N_HEADS, tb, CHUNK), F32),
                          pltpu.VMEM((tb // CHUNK * N_HEADS, HEAD_DIM, CHUNK), F32),
                          pltpu.VMEM((max(tb // CHUNK, 8), LANES), F32),
                          pltpu.VMEM((N_HEADS, HEAD_DIM, HEAD_DIM), F32)],
        compiler_params=pltpu.CompilerParams(dimension_semantics=("arbitrary", "arbitrary"),
                                             vmem_limit_bytes=VMEM_LIMIT),
        name="gdn",
    )(proj, proj, proj, proj, gproj, proj, proj, proj, conv_w, an, dtb, norm_g.reshape(1, D_GROUP))


N_PAIRS = N_RWKV // 2
RWKV_LOW = 384


def _shift1_rows(src_ref, halo_ref, first, r0, nrows, c0, ncols):
    if r0 == 0:
        hl = jnp.where(first, 0.0, halo_ref[0, 7:8, c0:c0 + ncols])
        return jnp.concatenate([hl, src_ref[0, 0:nrows - 1, c0:c0 + ncols]], axis=0)
    return src_ref[0, r0 - 1:r0 - 1 + nrows, c0:c0 + ncols]


def _rwkv_kernel(r_ref, k_ref, v_ref, l0_ref, l1_ref, l2_ref,
                 rh_ref, kh_ref, vh_ref, l0h_ref, l1h_ref, l2h_ref,
                 mu_ref, mul_ref, w0_ref, w2_ref, a0_ref, a2_ref, g2_ref, kk_ref, ka_ref, rk_ref,
                 lng_ref, lnb_ref, out_ref,
                 gs_ref, bo_ref, y_ref, atrt_ref, avk_ref, yk_ref, inv_ref, arb_ref, btT_ref, kvT_ref, wlT_ref,
                 H_ref, *, tb):
    t = pl.program_id(1)
    first = t == 0

    @pl.when(first)
    def _():
        H_ref[...] = jnp.zeros_like(H_ref)

    ri = lax.broadcasted_iota(jnp.int32, (LANES, LANES), 0)
    ci = lax.broadcasted_iota(jnp.int32, (LANES, LANES), 1)
    same_head = ((ri // RWKV_HEAD) == (ci // RWKV_HEAD)).astype(F32)
    causal = _tri(CHUNK)
    strict = _tri(CHUNK, strict=True)
    lane1 = lax.broadcasted_iota(jnp.int32, (1, LANES), 1)
    m0 = (lane1 < RWKV_HEAD).astype(F32)
    m1 = 1.0 - m0
    t2 = lax.broadcasted_iota(jnp.int32, (CHUNK, LANES), 0)
    l2 = lax.broadcasted_iota(jnp.int32, (CHUNK, LANES), 1)
    lo = l2 < RWKV_HEAD
    s2 = l2 & (RWKV_HEAD - 1)
    causal2 = t2 >= s2
    strict2 = t2 > s2
    n_chunks = tb // CHUNK
    pairs = range(N_PAIRS)
    halves = [(p, hh) for p in pairs for hh in range(2)]
    ps = [slice(p * LANES, (p + 1) * LANES) for p in pairs]

    def lerp(src, halo, mu, r0, c0, ncols):
        cur = src[0, r0:r0 + CHUNK, c0:c0 + ncols]
        return cur + (_shift1_rows(src, halo, first, r0, CHUNK, c0, ncols) - cur) * mu

    for c in range(n_chunks):
        r0 = c * CHUNK
        rows = slice(r0, r0 + CHUNK)
        wl = lerp(l0_ref, l0h_ref, mul_ref[:, 0:LANES], r0, 0, LANES)
        g1 = lerp(l1_ref, l1h_ref, mul_ref[:, LANES:2 * LANES], r0, 0, LANES)
        g2 = lerp(l2_ref, l2h_ref, mul_ref[:, 2 * LANES:3 * LANES], r0, 0, LANES)
        wl_t = jnp.where(lo, jnp.tanh(wl), 0.0)
        al = jnp.where(lo, 0.0, wl)
        sg1 = _sigmoid(g1)
        sg2 = jnp.where(l2 < 32, _sigmoid(g2), 0.0)
        lw = [-math.exp(-0.5) * _sigmoid(w0_ref[:, ps[p]] + _dot(wl_t, w2_ref[:, ps[p]])) for p in pairs]
        a = [_sigmoid(a0_ref[:, ps[p]] + _dot(al, a2_ref[:, ps[p]])) for p in pairs]
        g = [_dot(sg1, g2_ref[0:LANES, ps[p]]) + _dot(sg2, g2_ref[LANES:2 * LANES, ps[p]]) for p in pairs]
        r = [lerp(r_ref, rh_ref, mu_ref[:, ps[p]], r0, p * LANES, LANES) for p in pairs]
        k = [lerp(k_ref, kh_ref, mu_ref[:, D_GROUP + p * LANES:D_GROUP + (p + 1) * LANES], r0, p * LANES, LANES)
             for p in pairs]
        v = [lerp(v_ref, vh_ref, mu_ref[:, 2 * D_GROUP + p * LANES:2 * D_GROUP + (p + 1) * LANES], r0, p * LANES, LANES)
             for p in pairs]
        kk = [k[p] * kk_ref[:, ps[p]] for p in pairs]
        nrm = [jnp.sqrt(_half_lane_sums(kk[p] * kk[p], m0, lo)) for p in pairs]
        kk = [kk[p] / jnp.maximum(nrm[p], 1e-12) for p in pairs]
        k2 = [k[p] * (1.0 + (a[p] - 1.0) * ka_ref[:, ps[p]]) for p in pairs]
        rk = [_half_lane_sums(r[p] * k2[p] * rk_ref[:, ps[p]], m0, lo) for p in pairs]
        for p in pairs:
            gs_ref[p, rows, :] = g[p]
            bo_ref[p, rows, :] = rk[p] * v[p]
        cs = [_cumsum_rows(lw[p]) for p in pairs]
        w_inv = [jnp.exp(-cs[p]) for p in pairs]
        w_end = [jnp.exp(cs[p][CHUNK - 1:CHUNK, :]) for p in pairs]
        rt = [r[p] * jnp.exp(cs[p]) for p in pairs]
        at = [-kk[p] * jnp.exp(cs[p] - lw[p]) for p in pairs]
        bt = [kk[p] * a[p] * w_inv[p] for p in pairs]
        kt = [k2[p] * w_inv[p] for p in pairs]
        pm0 = [_dot_nt(jnp.concatenate([at[p] * m0, rt[p] * m0], axis=0), jnp.concatenate([bt[p], kt[p]], axis=0))
               for p in pairs]
        pm1 = [_dot_nt(jnp.concatenate([at[p] * m1, rt[p] * m1], axis=0), jnp.concatenate([kt[p], bt[p]], axis=0))
               for p in pairs]
        n_ab = [jnp.where(strict2, jnp.where(lo, pm0[p][:CHUNK], pm1[p][:CHUNK]), 0.0) for p in pairs]
        a_rb = [jnp.where(causal2, jnp.where(lo, pm0[p][CHUNK:], pm1[p][CHUNK:]), 0.0) for p in pairs]
        akrk = [jnp.concatenate([jnp.where(strict2, jnp.where(lo, pm1[p][:CHUNK], pm0[p][:CHUNK]), 0.0),
                                 jnp.where(causal2, jnp.where(lo, pm1[p][CHUNK:], pm0[p][CHUNK:]), 0.0)], axis=0)
                for p in pairs]
        vk = [_dot(akrk[p], jnp.concatenate([v[p] * m1, v[p] * m0], axis=0)) for p in pairs]
        inv = _inv_unit_lower_packed(n_ab, m0, m1)
        kv = [_dot((kt[p] * w_end[p]).T, v[p]) * same_head for p in pairs]
        for p in pairs:
            i = c * N_PAIRS + p
            atrt_ref[i] = jnp.concatenate([at[p], rt[p]], axis=0).astype(BF16)
            avk_ref[i] = vk[p][:CHUNK]
            yk_ref[i] = vk[p][CHUNK:]
            btT_ref[i] = (bt[p] * w_end[p]).T.astype(BF16)
            kvT_ref[i] = kv[p]
            wlT_ref[i] = jnp.broadcast_to(w_end[p], (LANES, LANES)).T
            inv_ref[i] = inv[p].astype(BF16)
            arb_ref[i] = a_rb[p].astype(BF16)

    for c in range(n_chunks):
        rows = slice(c * CHUNK, (c + 1) * CHUNK)
        it = [c * N_PAIRS + p for p in pairs]
        H = [H_ref[p] for p in pairs]
        xy0 = [_dot(atrt_ref[it[p]], H[p]) for p in pairs]
        x = [xy0[p][:CHUNK] + avk_ref[it[p]] for p in pairs]
        u = [_dot(inv_ref[it[p]], jnp.concatenate([x[p] * m0, x[p] * m1], axis=0)) for p in pairs]
        yb = [_dot(arb_ref[it[p]], jnp.concatenate([u[p] * m0, u[p] * m1], axis=0)) for p in pairs]
        bu = [_dot(btT_ref[it[p]], u[p]) for p in pairs]
        for p in pairs:
            y_ref[p, rows, :] = xy0[p][CHUNK:] + yk_ref[it[p]] + yb[p]
            H_ref[p] = H[p] * wlT_ref[it[p]] + bu[p] * same_head + kvT_ref[it[p]]

    inv_n = 1.0 / RWKV_HEAD
    for c in range(n_chunks):
        rows = slice(c * CHUNK, (c + 1) * CHUNK)
        y = [y_ref[p, rows, :] for p in pairs]
        yc = [y[p] - _half_lane_sums(y[p], m0, lo) * inv_n for p in pairs]
        var = [_half_lane_sums(yc[p] * yc[p], m0, lo) * inv_n for p in pairs]
        for p in pairs:
            yn = yc[p] * lax.rsqrt(var[p] + RWKV_LN_EPS) * lng_ref[:, ps[p]] + lnb_ref[:, ps[p]]
            out_ref[0, rows, ps[p]] = ((yn + bo_ref[p, rows, :]) * gs_ref[p, rows, :]).astype(out_ref.dtype)


def _rwkv_call(proj, mu, w0, w2, a0, a2, g2, k_k, k_a, r_k, ln_g, ln_b, *, col0, tb=256):
    B, T, _ = proj.shape
    j0 = col0 // D_GROUP
    l0 = (col0 + 3 * D_GROUP) // LANES
    row = lambda a: a.reshape(1, -1).astype(F32)
    mul = jnp.zeros((1, RWKV_LOW), F32).at[0, :288].set(mu[3 * D_GROUP:])
    w2p = jnp.zeros((LANES, D_GROUP), F32).at[:64].set(w2)
    a2p = jnp.zeros((LANES, D_GROUP), F32).at[64:].set(a2)
    g2p = jnp.zeros((2 * LANES, D_GROUP), F32).at[:160].set(g2)
    colblk = lambda j: pl.BlockSpec((1, tb, D_GROUP), lambda b, t: (b, t, j0 + j))
    lowblk = lambda j: pl.BlockSpec((1, tb, LANES), lambda b, t: (b, t, l0 + j))
    hrow = lambda t: jnp.maximum(t * (tb // 8) - 1, 0)
    halo = lambda j: pl.BlockSpec((1, 8, D_GROUP), lambda b, t: (b, hrow(t), j0 + j))
    lowhalo = lambda j: pl.BlockSpec((1, 8, LANES), lambda b, t: (b, hrow(t), l0 + j))
    full = lambda a: pl.BlockSpec(a.shape, lambda b, t: (0,) * a.ndim)
    params = [row(mu[:3 * D_GROUP]), mul, row(w0), w2p, row(a0), a2p, g2p, row(k_k), row(k_a), row(r_k),
              row(ln_g), row(ln_b)]
    big = pltpu.VMEM((N_PAIRS, tb, LANES), F32)
    n_items = tb // CHUNK * N_PAIRS
    return pl.pallas_call(
        functools.partial(_rwkv_kernel, tb=tb),
        grid=(B, T // tb),
        in_specs=[colblk(0), colblk(1), colblk(2), lowblk(0), lowblk(1), lowblk(2),
                  halo(0), halo(1), halo(2), lowhalo(0), lowhalo(1), lowhalo(2)] + [full(a) for a in params],
        out_specs=pl.BlockSpec((1, tb, D_GROUP), lambda b, t: (b, t, 0)),
        out_shape=jax.ShapeDtypeStruct((B, T, D_GROUP), BF16),
        scratch_shapes=[big] * 3 + [
            pltpu.VMEM((n_items, 2 * CHUNK, LANES), BF16),
            pltpu.VMEM((n_items, CHUNK, LANES), F32),
            pltpu.VMEM((n_items, CHUNK, LANES), F32),
            pltpu.VMEM((n_items, CHUNK, LANES), BF16),
            pltpu.VMEM((n_items, CHUNK, LANES), BF16),
            pltpu.VMEM((n_items, LANES, CHUNK), BF16),
            pltpu.VMEM((n_items, LANES, LANES), F32),
            pltpu.VMEM((n_items, LANES, LANES), F32),
            pltpu.VMEM((N_PAIRS, LANES, LANES), F32)],
        compiler_params=pltpu.CompilerParams(dimension_semantics=("arbitrary", "arbitrary"),
                                             vmem_limit_bytes=VMEM_LIMIT),
        name="rwkv7",
    )(*([proj] * 12), *params)


def _ada_kernel(c_ref, w_ref, b_ref, out_ref):
    sc = _silu(c_ref[...]).astype(BF16)
    out_ref[0] = jnp.dot(sc, w_ref[0].astype(BF16), preferred_element_type=F32) + b_ref[0]


def _ada_call(c, ada_w, ada_b, *, tn=1536):
    B = c.shape[0]
    n_mod = ada_w.shape[0] * ada_w.shape[1]
    w = ada_w.reshape(n_mod, D_MODEL, 3 * D_MODEL)
    b = ada_b.reshape(n_mod, 1, 3 * D_MODEL)
    return pl.pallas_call(
        _ada_kernel,
        grid=(n_mod, 3 * D_MODEL // tn),
        in_specs=[pl.BlockSpec((B, D_MODEL), lambda i, j: (0, 0)),
                  pl.BlockSpec((1, D_MODEL, tn), lambda i, j: (i, 0, j)),
                  pl.BlockSpec((1, 1, tn), lambda i, j: (i, 0, j))],
        out_specs=pl.BlockSpec((1, B, tn), lambda i, j: (i, 0, j)),
        out_shape=jax.ShapeDtypeStruct((n_mod, B, 3 * D_MODEL), F32),
        compiler_params=pltpu.CompilerParams(dimension_semantics=("arbitrary", "arbitrary"),
                                             vmem_limit_bytes=VMEM_LIMIT),
        name="adaln",
    )(c, w, b)


def _mod_spec(i, part, nb):
    return pl.BlockSpec((1, 1, D_MODEL), lambda b, t: (i * nb + b, 0, part))


def _modulate_kernel(x_ref, shift_ref, scale_ref, h_ref):
    h_ref[0] = (x_ref[0] * (1.0 + scale_ref[0]) + shift_ref[0]).astype(h_ref.dtype)


def _modulate_call(x, mods3, i, *, tb=512):
    B, T, _ = x.shape
    blk = pl.BlockSpec((1, tb, D_MODEL), lambda b, t: (b, t, 0))
    return pl.pallas_call(
        _modulate_kernel,
        grid=(B, T // tb),
        in_specs=[blk, _mod_spec(i, 0, B), _mod_spec(i, 1, B)],
        out_specs=blk,
        out_shape=jax.ShapeDtypeStruct(x.shape, BF16),
        compiler_params=pltpu.CompilerParams(dimension_semantics=("arbitrary", "arbitrary"),
                                             vmem_limit_bytes=VMEM_LIMIT),
        name="modulate",
    )(x, mods3, mods3)


LN_ROWS = 16


def _proj_ln_kernel(*refs, n_lhs, nk, tm, with_next):
    lhs = refs[:n_lhs]
    ws = refs[n_lhs:2 * n_lhs]
    x_ref, gate_ref, g_ref, b_ref = refs[2 * n_lhs:2 * n_lhs + 4]
    if with_next:
        shift_ref, scale_ref, xo_ref, h_ref, acc_ref = refs[2 * n_lhs + 4:]
    else:
        xo_ref, acc_ref = refs[2 * n_lhs + 4:]
    k = pl.program_id(2)
    def partial_product():
        part = jnp.dot(lhs[0][0], ws[0][...], preferred_element_type=F32)
        for j in range(1, n_lhs):
            part = part + jnp.dot(lhs[j][0], ws[j][...], preferred_element_type=F32)
        return part

    @pl.when(k == 0)
    def _():
        acc_ref[...] = partial_product()

    @pl.when(k > 0)
    def _():
        acc_ref[...] += partial_product()

    @pl.when(k == nk - 1)
    def _():
        gate1 = 1.0 + gate_ref[0]
        g, b = g_ref[...], b_ref[...]
        if with_next:
            scale1, shift = 1.0 + scale_ref[0], shift_ref[0]

        def rows_body(r, carry):
            rows = pl.ds(pl.multiple_of(r * LN_ROWS, LN_ROWS), LN_ROWS)
            z = ALPHA * x_ref[0, rows, :] + gate1 * acc_ref[rows, :]
            zc = z - jnp.mean(z, axis=-1, keepdims=True)
            var = jnp.mean(zc * zc, axis=-1, keepdims=True)
            xn = zc * lax.rsqrt(var + LN_EPS) * g + b
            xo_ref[0, rows, :] = xn
            if with_next:
                h_ref[0, rows, :] = (xn * scale1 + shift).astype(h_ref.dtype)
            return carry

        lax.fori_loop(0, tm // LN_ROWS, rows_body, 0, unroll=8)


def _proj_ln_call(lhs, ws, x, mods3, i, g, b, *, with_next, tm, tk):
    B, T, D = x.shape
    n_lhs = len(lhs)
    nk = lhs[0].shape[2] // tk
    blk = pl.BlockSpec((1, tm, D), lambda b, t, k: (b, t, 0))
    row = pl.BlockSpec((1, D), lambda b, t, k: (0, 0))
    mod = lambda ii, part: pl.BlockSpec((1, 1, D), lambda b, t, k: (ii * B + b, 0, part))
    wspec = lambda idx, k0: pl.BlockSpec((None, tk, D), lambda b, t, k: (idx, k0 + k, 0))
    in_specs = ([pl.BlockSpec((1, tm, tk), lambda b, t, k: (b, t, k))] * n_lhs
                + [wspec(idx, k0) for _, idx, k0 in ws]
                + [blk, mod(i, 2), row, row])
    args = list(lhs) + [w for w, _, _ in ws] + [x, mods3, g.reshape(1, D), b.reshape(1, D)]
    out_specs = [blk]
    out_shape = [jax.ShapeDtypeStruct(x.shape, F32)]
    if with_next:
        in_specs += [mod(i + 1, 0), mod(i + 1, 1)]
        args += [mods3, mods3]
        out_specs.append(blk)
        out_shape.append(jax.ShapeDtypeStruct(x.shape, BF16))
    return pl.pallas_call(
        functools.partial(_proj_ln_kernel, n_lhs=n_lhs, nk=nk, tm=tm, with_next=with_next),
        grid=(B, T // tm, nk),
        in_specs=in_specs, out_specs=out_specs, out_shape=out_shape,
        scratch_shapes=[pltpu.VMEM((tm, D), F32)],
        compiler_params=pltpu.CompilerParams(dimension_semantics=("arbitrary", "arbitrary", "arbitrary"),
                                             vmem_limit_bytes=VMEM_LIMIT),
        name="proj_residual_ln",
    )(*args)


def _matmul_kernel(a_ref, b_ref, o_ref, *scratch, nk):
    if nk == 1:
        o_ref[...] = jnp.dot(a_ref[...], b_ref[...], preferred_element_type=F32).astype(o_ref.dtype)
        return
    (acc_ref,) = scratch
    k = pl.program_id(2)

    @pl.when(k == 0)
    def _():
        acc_ref[...] = jnp.zeros_like(acc_ref)

    acc_ref[...] += jnp.dot(a_ref[...], b_ref[...], preferred_element_type=F32)

    @pl.when(k == nk - 1)
    def _():
        o_ref[...] = acc_ref[...].astype(o_ref.dtype)


def _matmul_call(a, b, *, tm, tn, tk, out_dtype=F32):
    M, K = a.shape
    _, N = b.shape
    nk = K // tk
    return pl.pallas_call(
        functools.partial(_matmul_kernel, nk=nk),
        grid=(N // tn, M // tm, nk),
        in_specs=[pl.BlockSpec((tm, tk), lambda j, i, k: (i, k)),
                  pl.BlockSpec((tk, tn), lambda j, i, k: (k, j))],
        out_specs=pl.BlockSpec((tm, tn), lambda j, i, k: (i, j)),
        out_shape=jax.ShapeDtypeStruct((M, N), out_dtype),
        scratch_shapes=[] if nk == 1 else [pltpu.VMEM((tm, tn), F32)],
        compiler_params=pltpu.CompilerParams(dimension_semantics=("arbitrary", "arbitrary", "arbitrary"),
                                             vmem_limit_bytes=VMEM_LIMIT),
        name="matmul",
    )(a, b)


XPOSE_ROWS = 256


def _matmul_f32wt_kernel(a_ref, wt_ref, o_ref, wb_ref):
    @pl.when(pl.program_id(1) == 0)
    def _():
        for r in range(0, wt_ref.shape[0], XPOSE_ROWS):
            wb_ref[:, r:r + XPOSE_ROWS] = wt_ref[r:r + XPOSE_ROWS, :].T.astype(BF16)

    o_ref[...] = jnp.dot(a_ref[...], wb_ref[...], preferred_element_type=F32)


def _matmul_f32wt_call(a, wt, idx, ncols, *, tm=1024, tn=1024):
    M, K = a.shape
    tm = min(tm, M)
    return pl.pallas_call(
        _matmul_f32wt_kernel,
        grid=(ncols // tn, M // tm),
        in_specs=[pl.BlockSpec((tm, K), lambda j, i: (i, 0)),
                  pl.BlockSpec((None, tn, K), lambda j, i: (idx, j, 0))],
        out_specs=pl.BlockSpec((tm, tn), lambda j, i: (i, j)),
        out_shape=jax.ShapeDtypeStruct((M, ncols), F32),
        scratch_shapes=[pltpu.VMEM((K, tn), BF16)],
        compiler_params=pltpu.CompilerParams(dimension_semantics=("arbitrary", "arbitrary"),
                                             vmem_limit_bytes=VMEM_LIMIT),
        name="matmul_f32wt",
    )(a, wt)


def _regroup_rows_kernel(wt_ref, o_ref, *, segments):
    tc = wt_ref.shape[1]
    pieces, pos = [], 0
    for src, width, dst in segments:
        if dst > pos:
            pieces.append(jnp.zeros((dst - pos, tc), F32))
        pieces.append(wt_ref[src:src + width, :])
        pos = dst + width
    if pos < o_ref.shape[1]:
        pieces.append(jnp.zeros((o_ref.shape[1] - pos, tc), F32))
    stacked = jnp.concatenate(pieces, axis=0)
    for r in range(0, o_ref.shape[1], LANES):
        o_ref[:, r:r + LANES] = stacked[r:r + LANES, :].T.astype(o_ref.dtype)


def _regroup_rows_call(wt, idx, segments, out_cols, *, block_rows, block_index, tc):
    _, _, K = wt.shape
    return pl.pallas_call(
        functools.partial(_regroup_rows_kernel, segments=segments),
        grid=(K // tc,),
        in_specs=[pl.BlockSpec((None, block_rows, tc), lambda c: (idx, block_index, c))],
        out_specs=pl.BlockSpec((tc, out_cols), lambda c: (c, 0)),
        out_shape=jax.ShapeDtypeStruct((K, out_cols), BF16),
        compiler_params=pltpu.CompilerParams(dimension_semantics=("arbitrary",), vmem_limit_bytes=VMEM_LIMIT),
        name="regroup_rows",
    )(wt)


def _swiglu_up_kernel(h_ref, wg_ref, wu_ref, o_ref, wgb_ref, wub_ref):
    @pl.when(pl.program_id(1) == 0)
    def _():
        wgb_ref[...] = wg_ref[...].astype(BF16)
        wub_ref[...] = wu_ref[...].astype(BF16)

    h = h_ref[...]
    g = jnp.dot(h, wgb_ref[...], preferred_element_type=F32)
    u = jnp.dot(h, wub_ref[...], preferred_element_type=F32)
    o_ref[...] = (_silu(g) * u).astype(o_ref.dtype)


def _swiglu_up_call(h, wg, wu, layer, *, tm=1024, tn=512):
    M, K = h.shape
    _, _, N = wg.shape
    tm = min(tm, M)
    wspec = pl.BlockSpec((None, K, tn), lambda j, i: (layer, 0, j))
    return pl.pallas_call(
        _swiglu_up_kernel,
        grid=(N // tn, M // tm),
        in_specs=[pl.BlockSpec((tm, K), lambda j, i: (i, 0)), wspec, wspec],
        out_specs=pl.BlockSpec((tm, tn), lambda j, i: (i, j)),
        out_shape=jax.ShapeDtypeStruct((M, N), BF16),
        scratch_shapes=[pltpu.VMEM((K, tn), BF16), pltpu.VMEM((K, tn), BF16)],
        compiler_params=pltpu.CompilerParams(dimension_semantics=("arbitrary", "arbitrary"),
                                             vmem_limit_bytes=VMEM_LIMIT),
        name="swiglu_up",
    )(h, wg, wu)


A_MAIN = 4 * D_GROUP
B_COLS_PAD = 3 * D_GROUP + RWKV_LOW + LANES
MLSTM_GATE_COL = 3 * D_GROUP + RWKV_LOW
CD_MAIN = 8 * D_GROUP
GDN_COL0 = 4 * D_GROUP


N_GATES = 2 * N_HEADS
B_SRC = A_MAIN + N_GATES
B_LOW = 64 + 64 + 160
B_SEGMENTS = ((B_SRC, 3 * D_GROUP, 0), (B_SRC + 3 * D_GROUP, B_LOW, 3 * D_GROUP), (A_MAIN, N_GATES, MLSTM_GATE_COL))
CD_GATE_SEGMENTS = ((0, N_GATES, 0),)


def kernel(x, c, positions, ada_w, ada_b, ln_g, ln_b, ab_w_in, ab_w_out, mlstm_conv_w, mlstm_gate_b, mlstm_norm_g, rwkv_mu, rwkv_w0, rwkv_w2, rwkv_a0, rwkv_a2, rwkv_g2, rwkv_k_k, rwkv_k_a, rwkv_r_k, rwkv_ln_g, rwkv_ln_b, cd_w_in, cd_w_out, ret_norm_g, gdn_conv_w, gdn_a_log, gdn_dt_bias, gdn_norm_g, ffn_w_gate, ffn_w_up, ffn_w_down):
    B, T, D = x.shape
    M = B * T
    depth = ada_w.shape[0]
    mods = _ada_call(c, ada_w, ada_b)
    mods3 = mods.reshape(2 * depth * B, 1, 3 * D)
    cos2, sin2 = _rope_call(positions)
    h = _modulate_call(x, mods3, 0)
    w_down = ffn_w_down.astype(BF16)
    for layer in range(depth):
        j = layer // 2
        i_mix, i_ffn = 2 * layer, 2 * layer + 1
        if layer % 2 == 0:
            h2 = h.reshape(M, D)
            wt = jnp.swapaxes(ab_w_in, 1, 2)
            proj_a = _matmul_f32wt_call(h2, wt, j, A_MAIN).reshape(B, T, A_MAIN)
            w_b = _regroup_rows_call(wt, j, B_SEGMENTS, B_COLS_PAD, block_rows=wt.shape[1], block_index=0, tc=256)
            proj_b = _matmul_call(h2, w_b, tm=min(1024, M), tn=B_COLS_PAD // 2, tk=D)
            proj_b = proj_b.reshape(B, T, B_COLS_PAD)
            ya = _mlstm_call(proj_a, proj_b, mlstm_conv_w[j], mlstm_gate_b[j], mlstm_norm_g[j],
                             gate_col=MLSTM_GATE_COL)
            yb = _rwkv_call(proj_b, rwkv_mu[j], rwkv_w0[j], rwkv_w2[j], rwkv_a0[j], rwkv_a2[j], rwkv_g2[j],
                            rwkv_k_k[j], rwkv_k_a[j], rwkv_r_k[j], rwkv_ln_g[j], rwkv_ln_b[j], col0=0)
            w_out = ab_w_out
        else:
            h2 = h.reshape(M, D)
            wt = jnp.swapaxes(cd_w_in, 1, 2)
            proj = _matmul_f32wt_call(h2, wt, j, CD_MAIN).reshape(B, T, CD_MAIN)
            w_gates = _regroup_rows_call(wt, j, CD_GATE_SEGMENTS, LANES, block_rows=N_GATES,
                                         block_index=CD_MAIN // N_GATES, tc=D)
            proj_g = _matmul_call(h2, w_gates, tm=512, tn=LANES, tk=D).reshape(B, T, LANES)
            ya = _ret_call(proj, cos2, sin2, ret_norm_g[j])
            yb = _gdn_call(proj, proj_g, gdn_conv_w[j], gdn_a_log[j], gdn_dt_bias[j], gdn_norm_g[j],
                           col0=GDN_COL0, gate_col=0)
            w_out = cd_w_out
        w_out = w_out.astype(BF16)
        x, h = _proj_ln_call([ya, yb], [(w_out, j, 0), (w_out, j, 1)], x, mods3, i_mix,
                             ln_g[layer, 0], ln_b[layer, 0], with_next=True, tm=512, tk=D_GROUP)
        act = _swiglu_up_call(h.reshape(M, D), ffn_w_gate, ffn_w_up, layer)
        last = layer == depth - 1
        res = _proj_ln_call([act.reshape(B, T, D_FF)], [(w_down, layer, 0)], x, mods3, i_ffn,
                            ln_g[layer, 1], ln_b[layer, 1], with_next=not last, tm=512, tk=1408)
        if last:
            (x,) = res
        else:
            x, h = res
    return x
```

```python
import functools
import math

import numpy as np
import jax
import jax.numpy as jnp
from jax import lax
from jax.experimental import pallas as pl
from jax.experimental.pallas import tpu as pltpu

F32 = jnp.float32
BF16 = jnp.bfloat16

D_MODEL = 2048
D_GROUP = 1024
HEAD_DIM = 128
N_HEADS = 8
RWKV_HEAD = 64
N_RWKV = 16
CHUNK = 64
D_FF = 5632
DEPTH = 2
ALPHA = (2 * DEPTH) ** 0.25
LN_EPS = 1e-5
RWKV_LN_EPS = 64e-5
ROPE_BASE = 10000.0
RET_GAMMA_BASE = 5.0
LANES = 128
VMEM_LIMIT = 48 * 1024 * 1024


def _dot(a, b):
    return jnp.dot(a.astype(BF16), b.astype(BF16), preferred_element_type=F32)


def _dot_nt(a, b):
    return lax.dot_general(a.astype(BF16), b.astype(BF16), (((1,), (1,)), ((), ())),
                           preferred_element_type=F32)


def _split3(x):
    hi = x.astype(BF16)
    r1 = x - hi.astype(F32)
    mid = r1.astype(BF16)
    lo = (r1 - mid.astype(F32)).astype(BF16)
    return hi, mid, lo


def _cumsum_rows(x):
    n = x.shape[1]
    out = jnp.dot(_tri(CHUNK).astype(BF16), jnp.concatenate(_split3(x), axis=1), preferred_element_type=F32)
    return out[:, :n] + out[:, n:2 * n] + out[:, 2 * n:]


def _half_lane_sums(x, m0, lo):
    s0 = jnp.sum(x * m0, axis=-1, keepdims=True)
    s1 = jnp.sum(x * (1.0 - m0), axis=-1, keepdims=True)
    return jnp.where(lo, s0, s1)


def _sigmoid(x):
    return 1.0 / (1.0 + jnp.exp(-x))


def _silu(x):
    return x * _sigmoid(x)


def _log_sigmoid(x):
    return jnp.minimum(x, 0.0) - jnp.log1p(jnp.exp(-jnp.abs(x)))


def _softplus(x):
    return jnp.maximum(x, 0.0) + jnp.log1p(jnp.exp(-jnp.abs(x)))


def _tri(n, strict=False):
    r = lax.broadcasted_iota(jnp.int32, (n, n), 0)
    c = lax.broadcasted_iota(jnp.int32, (n, n), 1)
    return (r > c) if strict else (r >= c)


def _conv_silu_rows(src_ref, halo_ref, w, first, r0, nrows, c0, ncols):
    cur = src_ref[0, r0:r0 + nrows, c0:c0 + ncols]
    acc = w[3:4] * cur
    if r0 == 0:
        hl = jnp.where(first, 0.0, halo_ref[0, :, c0:c0 + ncols])
        ext = jnp.concatenate([hl, cur[0:8]], axis=0)
        for j in range(3):
            head = ext[5 + j:13 + j]
            if nrows > 8:
                rest = src_ref[0, 5 + j:nrows - 3 + j, c0:c0 + ncols]
                sh = jnp.concatenate([head, rest], axis=0)
            else:
                sh = head
            acc = acc + w[j:j + 1] * sh
    else:
        for j in range(3):
            acc = acc + w[j:j + 1] * src_ref[0, r0 - 3 + j:r0 - 3 + j + nrows, c0:c0 + ncols]
    return _silu(acc)


def _head_norm_rows(h, g_row, eps, center=True):
    if center:
        h = h - jnp.mean(h, axis=-1, keepdims=True)
    return h * lax.rsqrt(jnp.mean(h * h, axis=-1, keepdims=True) + eps) * g_row


MLSTM_CHUNK_GROUP = 2


def _cummax_rows(x):
    row = lax.broadcasted_iota(jnp.int32, x.shape, 0)
    d = 1
    while d < x.shape[0]:
        x = jnp.where(row >= d, jnp.maximum(x, pltpu.roll(x, d, 0)), x)
        d *= 2
    return x


def _bcast_head_cols(x, sel, pieces):
    m = x.shape[0]
    parts, rest = [], x
    for _ in range(pieces):
        hi = rest.astype(BF16)
        parts.append(hi)
        rest = rest - hi.astype(F32)
    out = jnp.dot(jnp.concatenate(parts, axis=0), sel, preferred_element_type=F32)
    acc = out[:m]
    for i in range(1, pieces):
        acc = acc + out[i * m:(i + 1) * m]
    return acc


def _mlstm_kernel(q_ref, k_ref, v_ref, o_ref, g_ref, qh_ref, kh_ref, cw_ref, gb_ref, ng_ref,
                  out_ref, qc_ref, kc_ref, sv_ref, rs_ref, hh_ref, b0_ref, cm_ref, kvn_ref, CN_ref, m_ref, *, tb):
    t = pl.program_id(1)
    first = t == 0

    @pl.when(first)
    def _():
        CN_ref[...] = jnp.zeros_like(CN_ref)
        m_ref[...] = jnp.zeros_like(m_ref)

    for c in range(tb // CHUNK):
        for cb in range(D_GROUP // 256):
            cs = cb * 256
            qc_ref[c * CHUNK:(c + 1) * CHUNK, cs:cs + 256] = _conv_silu_rows(
                q_ref, qh_ref, cw_ref[:, cs:cs + 256], first, c * CHUNK, CHUNK, cs, 256)
            kc_ref[c * CHUNK:(c + 1) * CHUNK, cs:cs + 256] = _conv_silu_rows(
                k_ref, kh_ref, cw_ref[:, D_GROUP + cs:D_GROUP + cs + 256], first, c * CHUNK, CHUNK, cs, 256)

    causal = _tri(CHUNK)
    gb = gb_ref[...]
    scale = HEAD_DIM ** -0.5
    n_chunks = tb // CHUNK
    heads = range(N_HEADS)
    hs = [slice(h * HEAD_DIM, (h + 1) * HEAD_DIM) for h in heads]
    head_lane = lax.broadcasted_iota(jnp.int32, (1, LANES), 1) < N_HEADS
    sel = (lax.broadcasted_iota(jnp.int32, (LANES, N_HEADS * LANES), 0)
           == (lax.broadcasted_iota(jnp.int32, (LANES, N_HEADS * LANES), 1) >> 7)).astype(BF16)
    ones = jnp.ones((CHUNK, HEAD_DIM), F32)
    last = slice(CHUNK - 1, CHUNK)

    for c0 in range(0, n_chunks, MLSTM_CHUNK_GROUP):
        chunks = range(c0, c0 + MLSTM_CHUNK_GROUP)
        rows = {c: slice(c * CHUNK, (c + 1) * CHUNK) for c in chunks}
        z = {c: g_ref[0, rows[c], :] + gb for c in chunks}
        b0 = {c: pltpu.roll(_cumsum_rows(_log_sigmoid(z[c])), LANES - N_HEADS, 1) for c in chunks}
        cv = {c: jnp.where(head_lane, z[c] - b0[c], 0.0) for c in chunks}
        cm = {c: _cummax_rows(cv[c]) for c in chunks}
        cT = {c: cv[c].T for c in chunks}
        cmb = {c: _bcast_head_cols(cm[c], sel, 3) for c in chunks}
        e1b = {c: _bcast_head_cols(jnp.where(head_lane, jnp.exp(cv[c] - cm[c][last]), 0.0), sel, 2) for c in chunks}
        for c in chunks:
            b0_ref[rows[c], :] = b0[c]
            cm_ref[rows[c], :] = cm[c]
        items = [(c, h) for c in chunks for h in heads]
        n_items = range(len(items))
        q = [qc_ref[rows[c], hs[h]] for c, h in items]
        k = [kc_ref[rows[c], hs[h]] * scale for c, h in items]
        vo = [jnp.concatenate([v_ref[0, rows[c], hs[h]], ones], axis=1).astype(BF16) for c, h in items]
        qk = [_dot_nt(q[i], k[i]) for i in n_items]
        s = [qk[i] * jnp.where(causal, jnp.exp(jnp.minimum(cT[c][h:h + 1, :] - cmb[c][:, h * LANES:h * LANES + CHUNK],
                                                           0.0)), 0.0) for i, (c, h) in enumerate(items)]
        s_hi = [s[i].astype(BF16) for i in n_items]
        s_lo = [(s[i] - s_hi[i].astype(F32)).astype(BF16) for i in n_items]
        svr = [jnp.dot(jnp.concatenate([s_hi[i], s_lo[i]], axis=0), vo[i], preferred_element_type=F32)
               for i in n_items]
        kvn = [_dot((k[i] * e1b[c][:, hs[h]]).T, vo[i]) for i, (c, h) in enumerate(items)]
        for i, (c, h) in enumerate(items):
            sv_ref[rows[c], hs[h]] = svr[i][:CHUNK, :HEAD_DIM] + svr[i][CHUNK:, :HEAD_DIM]
            rs_ref[rows[c], hs[h]] = svr[i][:CHUNK, HEAD_DIM:] + svr[i][CHUNK:, HEAD_DIM:]
            kvn_ref[c * N_HEADS + h] = kvn[i]

    for c in range(n_chunks):
        rows = slice(c * CHUNK, (c + 1) * CHUNK)
        m = m_ref[0:1, :]
        b0 = b0_ref[rows, :]
        cm = cm_ref[rows, :]
        mx = jnp.maximum(cm, m)
        m_new = jnp.maximum(b0[last] + m, b0[last] + cm[last])
        m_ref[0:1, :] = m_new
        zero = lambda x: jnp.where(head_lane, x, 0.0)
        fib = _bcast_head_cols(zero(jnp.exp(cm - mx)), sel, 2)
        scb = _bcast_head_cols(zero(jnp.exp(m - mx)), sel, 2)
        emtb = _bcast_head_cols(zero(jnp.exp(jnp.minimum(-(b0 + mx), 80.0))), sel, 2)
        dfb = _bcast_head_cols(jnp.concatenate([zero(jnp.exp(b0[last] + m - m_new)),
                                                zero(jnp.exp(b0[last] + cm[last] - m_new)),
                                                jnp.zeros((6, LANES), F32)], axis=0), sel, 2)
        CN = [CN_ref[h] for h in heads]
        qcn = [_dot(qc_ref[rows, hs[h]], CN[h]) for h in heads]
        for h in heads:
            num = fib[:, hs[h]] * sv_ref[rows, hs[h]] + scb[:, hs[h]] * qcn[h][:, :HEAD_DIM]
            den = fib[:, hs[h]] * rs_ref[rows, hs[h]] + scb[:, hs[h]] * qcn[h][:, HEAD_DIM:]
            hh_ref[rows, hs[h]] = num / jnp.maximum(jnp.abs(den), emtb[:, hs[h]])
            dec = jnp.concatenate([dfb[0:1, hs[h]]] * 2, axis=1)
            fkv = jnp.concatenate([dfb[1:2, hs[h]]] * 2, axis=1)
            CN_ref[h] = CN[h] * dec + kvn_ref[c * N_HEADS + h] * fkv

    for c in range(n_chunks):
        rows = slice(c * CHUNK, (c + 1) * CHUNK)
        hn = [_head_norm_rows(hh_ref[rows, hs[h]], ng_ref[:, hs[h]], LN_EPS) for h in heads]
        for h in heads:
            out_ref[0, rows, hs[h]] = (hn[h] * _sigmoid(o_ref[0, rows, hs[h]])).astype(out_ref.dtype)


def _mlstm_call(proj, gproj, conv_w, gate_b, norm_g, *, gate_col, tb=256):
    B, T, _ = proj.shape
    nt = T // tb
    gb = jnp.zeros((1, LANES), F32).at[0, :2 * N_HEADS].set(gate_b)
    ng = norm_g.reshape(1, D_GROUP)
    colblk = lambda j: pl.BlockSpec((1, tb, D_GROUP), lambda b, t: (b, t, j))
    halo = lambda j: pl.BlockSpec((1, 8, D_GROUP), lambda b, t: (b, jnp.maximum(t * (tb // 8) - 1, 0), j))
    return pl.pallas_call(
        functools.partial(_mlstm_kernel, tb=tb),
        grid=(B, nt),
        in_specs=[colblk(0), colblk(1), colblk(2), colblk(3),
                  pl.BlockSpec((1, tb, LANES), lambda b, t: (b, t, gate_col // LANES)),
                  halo(0), halo(1),
                  pl.BlockSpec((4, 2 * D_GROUP), lambda b, t: (0, 0)),
                  pl.BlockSpec((1, LANES), lambda b, t: (0, 0)),
                  pl.BlockSpec((1, D_GROUP), lambda b, t: (0, 0))],
        out_specs=pl.BlockSpec((1, tb, D_GROUP), lambda b, t: (b, t, 0)),
        out_shape=jax.ShapeDtypeStruct((B, T, D_GROUP), BF16),
        scratch_shapes=[pltpu.VMEM((tb, D_GROUP), F32)] * 5
                       + [pltpu.VMEM((tb, LANES), F32)] * 2
                       + [pltpu.VMEM((tb // CHUNK * N_HEADS, HEAD_DIM, 2 * HEAD_DIM), F32),
                          pltpu.VMEM((N_HEADS, HEAD_DIM, 2 * HEAD_DIM), F32),
                          pltpu.VMEM((8, LANES), F32)],
        compiler_params=pltpu.CompilerParams(dimension_semantics=("arbitrary", "arbitrary"),
                                             vmem_limit_bytes=VMEM_LIMIT),
        name="mlstm",
    )(proj, proj, proj, proj, gproj, proj, proj, conv_w, gb, ng)


RET_CHUNK_GROUP = 2


def _ret_kernel(q_ref, k_ref, v_ref, g_ref, pos_ref, inv_ref, ng_ref, out_ref, qr_ref, o_ref, kv_ref, R_ref, *, tb):
    t = pl.program_id(1)

    @pl.when(t == 0)
    def _():
        R_ref[...] = jnp.zeros_like(R_ref)

    n_chunks = tb // CHUNK
    heads = range(N_HEADS)
    hs = [slice(h * HEAD_DIM, (h + 1) * HEAD_DIM) for h in heads]
    lg = [math.log1p(-2.0 ** (-RET_GAMMA_BASE - h)) for h in heads]
    causal = _tri(CHUNK)
    ri = lax.broadcasted_iota(jnp.int32, (CHUNK, CHUNK), 0)
    ci = lax.broadcasted_iota(jnp.int32, (CHUNK, CHUNK), 1)
    rel = (ri - ci).astype(F32)
    tcol = lax.broadcasted_iota(jnp.int32, (CHUNK, 1), 0).astype(F32)
    scale = HEAD_DIM ** -0.5
    lane = lax.broadcasted_iota(jnp.int32, (1, HEAD_DIM), 1)
    sign = jnp.where(lane < HEAD_DIM // 2, -1.0, 1.0)
    dmat = [jnp.where(causal, jnp.exp(rel * lg[h]), 0.0) for h in heads]

    for c0 in range(0, n_chunks, RET_CHUNK_GROUP):
        chunks = range(c0, c0 + RET_CHUNK_GROUP)
        rows = {c: slice(c * CHUNK, (c + 1) * CHUNK) for c in chunks}
        ang = {c: pos_ref[0, rows[c], :].astype(F32) * inv_ref[...] for c in chunks}
        cos2 = {c: jnp.cos(ang[c]) for c in chunks}
        sin2 = {c: jnp.sin(ang[c]) * sign for c in chunks}
        items = [(c, h) for c in chunks for h in heads]
        n_items = range(len(items))
        rot = lambda z, c: z * cos2[c] + pltpu.roll(z, HEAD_DIM // 2, 1) * sin2[c]
        qr = [rot(q_ref[0, rows[c], hs[h]], c) for c, h in items]
        kr = [rot(k_ref[0, rows[c], hs[h]], c) * scale for c, h in items]
        v = [v_ref[0, rows[c], hs[h]] for c, h in items]
        qk = [_dot_nt(qr[i], kr[i]) * dmat[h] for i, (c, h) in enumerate(items)]
        intra = [_dot(qk[i], v[i]) for i in n_items]
        kv = [_dot((kr[i] * jnp.exp((CHUNK - 1.0 - tcol) * lg[h])).T, v[i]) for i, (c, h) in enumerate(items)]
        for i, (c, h) in enumerate(items):
            qr_ref[rows[c], hs[h]] = qr[i]
            o_ref[rows[c], hs[h]] = intra[i]
            kv_ref[c * N_HEADS + h] = kv[i]

    for h in heads:
        R = R_ref[h]
        for c in range(n_chunks):
            inc = kv_ref[c * N_HEADS + h]
            kv_ref[c * N_HEADS + h] = R
            R = R * math.exp(CHUNK * lg[h]) + inc
        R_ref[h] = R
    for c in range(n_chunks):
        rows = slice(c * CHUNK, (c + 1) * CHUNK)
        inter = [_dot(qr_ref[rows, hs[h]], kv_ref[c * N_HEADS + h]) * jnp.exp((tcol + 1.0) * lg[h]) for h in heads]
        on = [_head_norm_rows(o_ref[rows, hs[h]] + inter[h], ng_ref[:, hs[h]], LN_EPS) for h in heads]
        for h in heads:
            out_ref[0, rows, hs[h]] = (on[h] * _silu(g_ref[0, rows, hs[h]])).astype(out_ref.dtype)


def _ret_call(proj, positions, norm_g, *, tb=256):
    B, T, _ = proj.shape
    half = HEAD_DIM // 2
    inv_freq = ROPE_BASE ** (-jnp.arange(half, dtype=F32) / half)
    inv2 = jnp.concatenate([inv_freq, inv_freq]).reshape(1, HEAD_DIM)
    colblk = lambda j: pl.BlockSpec((1, tb, D_GROUP), lambda b, t: (b, t, j))
    big = pltpu.VMEM((tb, D_GROUP), F32)
    return pl.pallas_call(
        functools.partial(_ret_kernel, tb=tb),
        grid=(B, T // tb),
        in_specs=[colblk(0), colblk(1), colblk(2), colblk(3),
                  pl.BlockSpec((1, tb, 1), lambda b, t: (b, t, 0)),
                  pl.BlockSpec((1, HEAD_DIM), lambda b, t: (0, 0)),
                  pl.BlockSpec((1, D_GROUP), lambda b, t: (0, 0))],
        out_specs=pl.BlockSpec((1, tb, D_GROUP), lambda b, t: (b, t, 0)),
        out_shape=jax.ShapeDtypeStruct((B, T, D_GROUP), BF16),
        scratch_shapes=[big, big, pltpu.VMEM((tb // CHUNK * N_HEADS, HEAD_DIM, HEAD_DIM), F32),
                        pltpu.VMEM((N_HEADS, HEAD_DIM, HEAD_DIM), F32)],
        compiler_params=pltpu.CompilerParams(dimension_semantics=("arbitrary", "arbitrary"),
                                             vmem_limit_bytes=VMEM_LIMIT),
        name="retention",
    )(proj, proj, proj, proj, positions.reshape(B, T, 1), inv2, norm_g.reshape(1, D_GROUP))


def _inv_unit_lower(nms):
    n = nms[0].shape[0]
    eye = (lax.broadcasted_iota(jnp.int32, (n, n), 0) == lax.broadcasted_iota(jnp.int32, (n, n), 1)).astype(F32)
    ps = [eye + nm for nm in nms]
    xs = [_dot(nm, nm) for nm in nms]
    for _ in range(int(math.log2(n)) - 2):
        px = [_dot(jnp.concatenate([p, x], axis=0), x) for p, x in zip(ps, xs)]
        ps = [p + y[:n] for p, y in zip(ps, px)]
        xs = [y[n:] for y in px]
    ps = [p + _dot(p, x) for p, x in zip(ps, xs)]
    resid = [eye - p + _dot(nm, p) for p, nm in zip(ps, nms)]
    return [p + _dot(p, r) for p, r in zip(ps, resid)]


def _blockdiag2(x, m0, m1):
    xb = x.astype(BF16)
    return jnp.concatenate([xb * m0.astype(BF16), xb * m1.astype(BF16)], axis=0)


def _inv_unit_lower_packed(nms, m0, m1):
    n = nms[0].shape[0]
    r = lax.broadcasted_iota(jnp.int32, (n, 2 * n), 0)
    c = lax.broadcasted_iota(jnp.int32, (n, 2 * n), 1)
    eye2 = (r == (c & (n - 1))).astype(F32)
    bd = lambda x: _blockdiag2(x, m0, m1)
    ps = [eye2 + nm for nm in nms]
    xs = [_dot(nm, bd(nm)) for nm in nms]
    for _ in range(int(math.log2(n)) - 2):
        px = [_dot(jnp.concatenate([p, x], axis=0), bd(x)) for p, x in zip(ps, xs)]
        ps = [p + y[:n] for p, y in zip(ps, px)]
        xs = [y[n:] for y in px]
    ps = [p + _dot(p, bd(x)) for p, x in zip(ps, xs)]
    resid = [eye2 - p + _dot(nm, bd(p)) for p, nm in zip(ps, nms)]
    return [p + _dot(p, bd(r_)) for p, r_ in zip(ps, resid)]


def _solve_unit_lower(nms, rhss):
    n = nms[0].shape[0]
    eye = (lax.broadcasted_iota(jnp.int32, (n, n), 0) == lax.broadcasted_iota(jnp.int32, (n, n), 1)).astype(F32)
    ps = [eye + nm for nm in nms]
    xs = [_dot(nm, nm) for nm in nms]
    for _ in range(int(math.log2(n)) - 2):
        px = [_dot(jnp.concatenate([p, x], axis=0), x) for p, x in zip(ps, xs)]
        ps = [p + y[:n] for p, y in zip(ps, px)]
        xs = [y[n:] for y in px]
    ps = [p + _dot(p, x) for p, x in zip(ps, xs)]
    x0 = [_dot(p, r) for p, r in zip(ps, rhss)]
    resid = [r - a + _dot(nm, a) for r, a, nm in zip(rhss, x0, nms)]
    return [a + _dot(p, r) for a, p, r in zip(x0, ps, resid)]


def _l2norm_rows(z):
    return z * lax.rsqrt(jnp.sum(z * z, axis=-1, keepdims=True) + 1e-6)


GDN_CHUNK_GROUP = 2


def _gdn_kernel(q_ref, k_ref, v_ref, z_ref, g_ref, qh_ref, kh_ref, vh_ref, cw_ref, an_ref, dt_ref, ng_ref,
                out_ref, qc_ref, kc_ref, vc_ref, u_ref, w_ref, qe_ref, o_ref, att_ref, kdT_ref, gl_ref, S_ref,
                *, tb):
    t = pl.program_id(1)
    first = t == 0

    @pl.when(first)
    def _():
        S_ref[...] = jnp.zeros_like(S_ref)

    srcs = ((q_ref, qh_ref, qc_ref), (k_ref, kh_ref, kc_ref), (v_ref, vh_ref, vc_ref))
    for c in range(tb // CHUNK):
        for cb in range(D_GROUP // 256):
            cs = cb * 256
            for i, (src, halo, dst) in enumerate(srcs):
                w = cw_ref[:, i * D_GROUP + cs:i * D_GROUP + cs + 256]
                dst[c * CHUNK:(c + 1) * CHUNK, cs:cs + 256] = _conv_silu_rows(
                    src, halo, w, first, c * CHUNK, CHUNK, cs, 256)

    causal = _tri(CHUNK)
    strict = _tri(CHUNK, strict=True)
    a_neg = an_ref[...]
    dtb = dt_ref[...]
    scale = HEAD_DIM ** -0.5
    n_chunks = tb // CHUNK
    heads = range(N_HEADS)
    hs = [slice(h * HEAD_DIM, (h + 1) * HEAD_DIM) for h in heads]

    for c0 in range(0, n_chunks, GDN_CHUNK_GROUP):
        chunks = range(c0, c0 + GDN_CHUNK_GROUP)
        rows = {c: slice(c * CHUNK, (c + 1) * CHUNK) for c in chunks}
        gz = {c: g_ref[0, rows[c], :] for c in chunks}
        beta = {c: _sigmoid(gz[c]) for c in chunks}
        gc = {c: _cumsum_rows(a_neg * _softplus(gz[c] + dtb)) for c in chunks}
        gcT = {c: gc[c].T for c in chunks}
        for c in chunks:
            gl_ref[c:c + 1, :] = gc[c][CHUNK - 1:CHUNK, :]
        items = [(c, h) for c in chunks for h in heads]
        n_items = range(len(items))
        gc_col = [gc[c][:, h:h + 1] for c, h in items]
        b_col = [beta[c][:, 8 + h:9 + h] for c, h in items]
        gamma = [jnp.where(causal, jnp.exp(gc_col[i] - gcT[c][h:h + 1, :]), 0.0) for i, (c, h) in enumerate(items)]
        q = [_l2norm_rows(qc_ref[rows[c], hs[h]]) * scale for c, h in items]
        k = [_l2norm_rows(kc_ref[rows[c], hs[h]]) for c, h in items]
        kb = [k[i] * b_col[i] for i in n_items]
        eg = [jnp.exp(gc_col[i]) for i in n_items]
        kq = [_dot_nt(jnp.concatenate([kb[i], q[i]], axis=0), k[i]) for i in n_items]
        inv = _inv_unit_lower([-jnp.where(strict, kq[i][:CHUNK] * gamma[i], 0.0) for i in n_items])
        uw = [_dot(inv[i], jnp.concatenate([vc_ref[rows[c], hs[h]] * b_col[i], kb[i] * eg[i]], axis=1))
              for i, (c, h) in enumerate(items)]
        for i, (c, h) in enumerate(items):
            u_ref[rows[c], hs[h]] = uw[i][:, :HEAD_DIM]
            w_ref[rows[c], hs[h]] = uw[i][:, HEAD_DIM:]
            qe_ref[rows[c], hs[h]] = q[i] * eg[i]
            att_ref[h, rows[c], :] = kq[i][CHUNK:] * gamma[i]
            g_last = gc[c][CHUNK - 1:CHUNK, h:h + 1]
            kdT_ref[c * N_HEADS + h] = (k[i] * jnp.exp(g_last - gc_col[i])).T

    for c in range(n_chunks):
        rows = slice(c * CHUNK, (c + 1) * CHUNK)
        S = [S_ref[h] for h in heads]
        ws = [_dot(jnp.concatenate([w_ref[rows, hs[h]], qe_ref[rows, hs[h]]], axis=0), S[h]) for h in heads]
        v_new = [u_ref[rows, hs[h]] - ws[h][:CHUNK] for h in heads]
        av = [_dot(att_ref[h, rows, :], v_new[h]) for h in heads]
        kv = [_dot(kdT_ref[c * N_HEADS + h], v_new[h]) for h in heads]
        for h in heads:
            S_ref[h] = S[h] * jnp.exp(gl_ref[c:c + 1, h:h + 1]) + kv[h]
            o_ref[rows, hs[h]] = ws[h][CHUNK:] + av[h]

    for c in range(n_chunks):
        rows = slice(c * CHUNK, (c + 1) * CHUNK)
        for h in heads:
            on = _head_norm_rows(o_ref[rows, hs[h]], ng_ref[:, hs[h]], 1e-6, center=False)
            out_ref[0, rows, hs[h]] = (on * _silu(z_ref[0, rows, hs[h]])).astype(out_ref.dtype)


def _gdn_call(proj, gproj, conv_w, a_log, dt_bias, norm_g, *, col0, gate_col, tb=256):
    B, T, _ = proj.shape
    j0 = col0 // D_GROUP
    an = jnp.zeros((1, LANES), F32).at[0, :N_HEADS].set(-jnp.exp(a_log.astype(F32)))
    dtb = jnp.zeros((1, LANES), F32).at[0, :N_HEADS].set(dt_bias)
    colblk = lambda j: pl.BlockSpec((1, tb, D_GROUP), lambda b, t: (b, t, j0 + j))
    halo = lambda j: pl.BlockSpec((1, 8, D_GROUP), lambda b, t: (b, jnp.maximum(t * (tb // 8) - 1, 0), j0 + j))
    return pl.pallas_call(
        functools.partial(_gdn_kernel, tb=tb),
        grid=(B, T // tb),
        in_specs=[colblk(0), colblk(1), colblk(2), colblk(3),
                  pl.BlockSpec((1, tb, LANES), lambda b, t: (b, t, gate_col // LANES)),
                  halo(0), halo(1), halo(2),
                  pl.BlockSpec((4, 3 * D_GROUP), lambda b, t: (0, 0)),
                  pl.BlockSpec((1, LANES), lambda b, t: (0, 0)),
                  pl.BlockSpec((1, LANES), lambda b, t: (0, 0)),
                  pl.BlockSpec((1, D_GROUP), lambda b, t: (0, 0))],
        out_specs=pl.BlockSpec((1, tb, D_GROUP), lambda b, t: (b, t, 0)),
        out_shape=jax.ShapeDtypeStruct((B, T, D_GROUP), BF16),
        scratch_shapes=[pltpu.VMEM((tb, D_GROUP), F32)] * 7
                       + [pltpu.VMEM((N_HEADS, tb, CHUNK), F32),
                          pltpu.VMEM((tb // CHUNK * N_HEADS, HEAD_DIM, CHUNK), F32),
                          pltpu.VMEM((max(tb // CHUNK, 8), LANES), F32),
                          pltpu.VMEM((N_HEADS, HEAD_DIM, HEAD_DIM), F32)],
        compiler_params=pltpu.CompilerParams(dimension_semantics=("arbitrary", "arbitrary"),
                                             vmem_limit_bytes=VMEM_LIMIT),
        name="gdn",
    )(proj, proj, proj, proj, gproj, proj, proj, proj, conv_w, an, dtb, norm_g.reshape(1, D_GROUP))


N_PAIRS = N_RWKV // 2
RWKV_LOW = 384


def _shift1_rows(src_ref, halo_ref, first, r0, nrows, c0, ncols):
    if r0 == 0:
        hl = jnp.where(first, 0.0, halo_ref[0, 7:8, c0:c0 + ncols])
        return jnp.concatenate([hl, src_ref[0, 0:nrows - 1, c0:c0 + ncols]], axis=0)
    return src_ref[0, r0 - 1:r0 - 1 + nrows, c0:c0 + ncols]


def _rwkv_kernel(r_ref, k_ref, v_ref, l0_ref, l1_ref, l2_ref,
                 rh_ref, kh_ref, vh_ref, l0h_ref, l1h_ref, l2h_ref,
                 mu_ref, mul_ref, w0_ref, w2_ref, a0_ref, a2_ref, g2_ref, kk_ref, ka_ref, rk_ref,
                 lng_ref, lnb_ref, out_ref,
                 gs_ref, bo_ref, y_ref, atrt_ref, avk_ref, yk_ref, inv_ref, arb_ref, btT_ref, kvT_ref, wlT_ref,
                 H_ref, *, tb):
    t = pl.program_id(1)
    first = t == 0

    @pl.when(first)
    def _():
        H_ref[...] = jnp.zeros_like(H_ref)

    ri = lax.broadcasted_iota(jnp.int32, (LANES, LANES), 0)
    ci = lax.broadcasted_iota(jnp.int32, (LANES, LANES), 1)
    same_head = ((ri // RWKV_HEAD) == (ci // RWKV_HEAD)).astype(F32)
    causal = _tri(CHUNK)
    strict = _tri(CHUNK, strict=True)
    lane1 = lax.broadcasted_iota(jnp.int32, (1, LANES), 1)
    m0 = (lane1 < RWKV_HEAD).astype(F32)
    m1 = 1.0 - m0
    t2 = lax.broadcasted_iota(jnp.int32, (CHUNK, LANES), 0)
    l2 = lax.broadcasted_iota(jnp.int32, (CHUNK, LANES), 1)
    lo = l2 < RWKV_HEAD
    s2 = l2 & (RWKV_HEAD - 1)
    causal2 = t2 >= s2
    strict2 = t2 > s2
    n_chunks = tb // CHUNK
    pairs = range(N_PAIRS)
    halves = [(p, hh) for p in pairs for hh in range(2)]
    ps = [slice(p * LANES, (p + 1) * LANES) for p in pairs]

    def lerp(src, halo, mu, r0, c0, ncols):
        cur = src[0, r0:r0 + CHUNK, c0:c0 + ncols]
        return cur + (_shift1_rows(src, halo, first, r0, CHUNK, c0, ncols) - cur) * mu

    for c in range(n_chunks):
        r0 = c * CHUNK
        rows = slice(r0, r0 + CHUNK)
        wl = lerp(l0_ref, l0h_ref, mul_ref[:, 0:LANES], r0, 0, LANES)
        g1 = lerp(l1_ref, l1h_ref, mul_ref[:, LANES:2 * LANES], r0, 0, LANES)
        g2 = lerp(l2_ref, l2h_ref, mul_ref[:, 2 * LANES:3 * LANES], r0, 0, LANES)
        wl_t = jnp.where(lo, jnp.tanh(wl), 0.0)
        al = jnp.where(lo, 0.0, wl)
        sg1 = _sigmoid(g1)
        sg2 = jnp.where(l2 < 32, _sigmoid(g2), 0.0)
        lw = [-math.exp(-0.5) * _sigmoid(w0_ref[:, ps[p]] + _dot(wl_t, w2_ref[:, ps[p]])) for p in pairs]
        a = [_sigmoid(a0_ref[:, ps[p]] + _dot(al, a2_ref[:, ps[p]])) for p in pairs]
        g = [_dot(sg1, g2_ref[0:LANES, ps[p]]) + _dot(sg2, g2_ref[LANES:2 * LANES, ps[p]]) for p in pairs]
        r = [lerp(r_ref, rh_ref, mu_ref[:, ps[p]], r0, p * LANES, LANES) for p in pairs]
        k = [lerp(k_ref, kh_ref, mu_ref[:, D_GROUP + p * LANES:D_GROUP + (p + 1) * LANES], r0, p * LANES, LANES)
             for p in pairs]
        v = [lerp(v_ref, vh_ref, mu_ref[:, 2 * D_GROUP + p * LANES:2 * D_GROUP + (p + 1) * LANES], r0, p * LANES, LANES)
             for p in pairs]
        kk = [k[p] * kk_ref[:, ps[p]] for p in pairs]
        nrm = [jnp.sqrt(_half_lane_sums(kk[p] * kk[p], m0, lo)) for p in pairs]
        kk = [kk[p] / jnp.maximum(nrm[p], 1e-12) for p in pairs]
        k2 = [k[p] * (1.0 + (a[p] - 1.0) * ka_ref[:, ps[p]]) for p in pairs]
        rk = [_half_lane_sums(r[p] * k2[p] * rk_ref[:, ps[p]], m0, lo) for p in pairs]
        for p in pairs:
            gs_ref[p, rows, :] = g[p]
            bo_ref[p, rows, :] = rk[p] * v[p]
        cs = [_cumsum_rows(lw[p]) for p in pairs]
        w_inv = [jnp.exp(-cs[p]) for p in pairs]
        w_end = [jnp.exp(cs[p][CHUNK - 1:CHUNK, :]) for p in pairs]
        rt = [r[p] * jnp.exp(cs[p]) for p in pairs]
        at = [-kk[p] * jnp.exp(cs[p] - lw[p]) for p in pairs]
        bt = [kk[p] * a[p] * w_inv[p] for p in pairs]
        kt = [k2[p] * w_inv[p] for p in pairs]
        atrt = [jnp.concatenate([at[p], rt[p]], axis=0).astype(BF16) for p in pairs]
        pm0 = [_dot_nt(atrt[p] * m0.astype(BF16), jnp.concatenate([bt[p], kt[p]], axis=0)) for p in pairs]
        pm1 = [_dot_nt(atrt[p] * m1.astype(BF16), jnp.concatenate([kt[p], bt[p]], axis=0)) for p in pairs]
        n_ab = [jnp.where(strict2, jnp.where(lo, pm0[p][:CHUNK], pm1[p][:CHUNK]), 0.0) for p in pairs]
        a_rb = [jnp.where(causal2, jnp.where(lo, pm0[p][CHUNK:], pm1[p][CHUNK:]), 0.0) for p in pairs]
        akrk = [jnp.concatenate([jnp.where(strict2, jnp.where(lo, pm1[p][:CHUNK], pm0[p][:CHUNK]), 0.0),
                                 jnp.where(causal2, jnp.where(lo, pm1[p][CHUNK:], pm0[p][CHUNK:]), 0.0)], axis=0)
                for p in pairs]
        vk = [_dot(akrk[p], _blockdiag2(v[p], m1, m0)) for p in pairs]
        inv = _inv_unit_lower_packed(n_ab, m0, m1)
        kv = [_dot((kt[p] * w_end[p]).T, v[p]) * same_head for p in pairs]
        for p in pairs:
            i = c * N_PAIRS + p
            atrt_ref[i] = atrt[p]
            avk_ref[i] = vk[p][:CHUNK]
            yk_ref[i] = vk[p][CHUNK:]
            btT_ref[i] = (bt[p] * w_end[p]).T.astype(BF16)
            kvT_ref[i] = kv[p]
            wlT_ref[i] = jnp.broadcast_to(w_end[p], (LANES, LANES)).T
            inv_ref[i] = inv[p].astype(BF16)
            arb_ref[i] = a_rb[p].astype(BF16)

    for c in range(n_chunks):
        rows = slice(c * CHUNK, (c + 1) * CHUNK)
        it = [c * N_PAIRS + p for p in pairs]
        H = [H_ref[p] for p in pairs]
        xy0 = [_dot(atrt_ref[it[p]], H[p]) for p in pairs]
        x = [xy0[p][:CHUNK] + avk_ref[it[p]] for p in pairs]
        u = [_dot(inv_ref[it[p]], _blockdiag2(x[p], m0, m1)) for p in pairs]
        yb = [_dot(arb_ref[it[p]], _blockdiag2(u[p], m0, m1)) for p in pairs]
        bu = [_dot(btT_ref[it[p]], u[p]) for p in pairs]
        for p in pairs:
            y_ref[p, rows, :] = xy0[p][CHUNK:] + yk_ref[it[p]] + yb[p]
            H_ref[p] = H[p] * wlT_ref[it[p]] + bu[p] * same_head + kvT_ref[it[p]]

    inv_n = 1.0 / RWKV_HEAD
    for c in range(n_chunks):
        rows = slice(c * CHUNK, (c + 1) * CHUNK)
        y = [y_ref[p, rows, :] for p in pairs]
        yc = [y[p] - _half_lane_sums(y[p], m0, lo) * inv_n for p in pairs]
        var = [_half_lane_sums(yc[p] * yc[p], m0, lo) * inv_n for p in pairs]
        for p in pairs:
            yn = yc[p] * lax.rsqrt(var[p] + RWKV_LN_EPS) * lng_ref[:, ps[p]] + lnb_ref[:, ps[p]]
            out_ref[0, rows, ps[p]] = ((yn + bo_ref[p, rows, :]) * gs_ref[p, rows, :]).astype(out_ref.dtype)


def _rwkv_call(proj, mu, w0, w2, a0, a2, g2, k_k, k_a, r_k, ln_g, ln_b, *, col0, tb=256):
    B, T, _ = proj.shape
    j0 = col0 // D_GROUP
    l0 = (col0 + 3 * D_GROUP) // LANES
    row = lambda a: a.reshape(1, -1).astype(F32)
    mul = jnp.zeros((1, RWKV_LOW), F32).at[0, :288].set(mu[3 * D_GROUP:])
    w2p = jnp.zeros((LANES, D_GROUP), F32).at[:64].set(w2)
    a2p = jnp.zeros((LANES, D_GROUP), F32).at[64:].set(a2)
    g2p = jnp.zeros((2 * LANES, D_GROUP), F32).at[:160].set(g2)
    colblk = lambda j: pl.BlockSpec((1, tb, D_GROUP), lambda b, t: (b, t, j0 + j))
    lowblk = lambda j: pl.BlockSpec((1, tb, LANES), lambda b, t: (b, t, l0 + j))
    hrow = lambda t: jnp.maximum(t * (tb // 8) - 1, 0)
    halo = lambda j: pl.BlockSpec((1, 8, D_GROUP), lambda b, t: (b, hrow(t), j0 + j))
    lowhalo = lambda j: pl.BlockSpec((1, 8, LANES), lambda b, t: (b, hrow(t), l0 + j))
    full = lambda a: pl.BlockSpec(a.shape, lambda b, t: (0,) * a.ndim)
    params = [row(mu[:3 * D_GROUP]), mul, row(w0), w2p, row(a0), a2p, g2p, row(k_k), row(k_a), row(r_k),
              row(ln_g), row(ln_b)]
    big = pltpu.VMEM((N_PAIRS, tb, LANES), F32)
    n_items = tb // CHUNK * N_PAIRS
    return pl.pallas_call(
        functools.partial(_rwkv_kernel, tb=tb),
        grid=(B, T // tb),
        in_specs=[colblk(0), colblk(1), colblk(2), lowblk(0), lowblk(1), lowblk(2),
                  halo(0), halo(1), halo(2), lowhalo(0), lowhalo(1), lowhalo(2)] + [full(a) for a in params],
        out_specs=pl.BlockSpec((1, tb, D_GROUP), lambda b, t: (b, t, 0)),
        out_shape=jax.ShapeDtypeStruct((B, T, D_GROUP), BF16),
        scratch_shapes=[big] * 3 + [
            pltpu.VMEM((n_items, 2 * CHUNK, LANES), BF16),
            pltpu.VMEM((n_items, CHUNK, LANES), F32),
            pltpu.VMEM((n_items, CHUNK, LANES), F32),
            pltpu.VMEM((n_items, CHUNK, LANES), BF16),
            pltpu.VMEM((n_items, CHUNK, LANES), BF16),
            pltpu.VMEM((n_items, LANES, CHUNK), BF16),
            pltpu.VMEM((n_items, LANES, LANES), F32),
            pltpu.VMEM((n_items, LANES, LANES), F32),
            pltpu.VMEM((N_PAIRS, LANES, LANES), F32)],
        compiler_params=pltpu.CompilerParams(dimension_semantics=("arbitrary", "arbitrary"),
                                             vmem_limit_bytes=VMEM_LIMIT),
        name="rwkv7",
    )(*([proj] * 12), *params)


def _ada_kernel(c_ref, w_ref, b_ref, out_ref):
    sc = _silu(c_ref[...]).astype(BF16)
    out_ref[0] = jnp.dot(sc, w_ref[0].astype(BF16), preferred_element_type=F32) + b_ref[0]


def _ada_call(c, ada_w, ada_b, *, tn=1536):
    B = c.shape[0]
    n_mod = ada_w.shape[0] * ada_w.shape[1]
    w = ada_w.reshape(n_mod, D_MODEL, 3 * D_MODEL)
    b = ada_b.reshape(n_mod, 1, 3 * D_MODEL)
    return pl.pallas_call(
        _ada_kernel,
        grid=(n_mod, 3 * D_MODEL // tn),
        in_specs=[pl.BlockSpec((B, D_MODEL), lambda i, j: (0, 0)),
                  pl.BlockSpec((1, D_MODEL, tn), lambda i, j: (i, 0, j)),
                  pl.BlockSpec((1, 1, tn), lambda i, j: (i, 0, j))],
        out_specs=pl.BlockSpec((1, B, tn), lambda i, j: (i, 0, j)),
        out_shape=jax.ShapeDtypeStruct((n_mod, B, 3 * D_MODEL), F32),
        compiler_params=pltpu.CompilerParams(dimension_semantics=("arbitrary", "arbitrary"),
                                             vmem_limit_bytes=VMEM_LIMIT),
        name="adaln",
    )(c, w, b)


def _mod_spec(i, part, nb):
    return pl.BlockSpec((1, 1, D_MODEL), lambda b, t: (i * nb + b, 0, part))


def _modulate_kernel(x_ref, shift_ref, scale_ref, h_ref):
    h_ref[0] = (x_ref[0] * (1.0 + scale_ref[0]) + shift_ref[0]).astype(h_ref.dtype)


def _modulate_call(x, mods3, i, *, tb=512):
    B, T, _ = x.shape
    blk = pl.BlockSpec((1, tb, D_MODEL), lambda b, t: (b, t, 0))
    return pl.pallas_call(
        _modulate_kernel,
        grid=(B, T // tb),
        in_specs=[blk, _mod_spec(i, 0, B), _mod_spec(i, 1, B)],
        out_specs=blk,
        out_shape=jax.ShapeDtypeStruct(x.shape, BF16),
        compiler_params=pltpu.CompilerParams(dimension_semantics=("arbitrary", "arbitrary"),
                                             vmem_limit_bytes=VMEM_LIMIT),
        name="modulate",
    )(x, mods3, mods3)


LN_ROWS = 16


def _proj_ln_kernel(*refs, n_lhs, nk, tm, with_next):
    lhs = refs[:n_lhs]
    ws = refs[n_lhs:2 * n_lhs]
    x_ref, gate_ref, g_ref, b_ref = refs[2 * n_lhs:2 * n_lhs + 4]
    if with_next:
        shift_ref, scale_ref, xo_ref, h_ref, acc_ref = refs[2 * n_lhs + 4:]
    else:
        xo_ref, acc_ref = refs[2 * n_lhs + 4:]
    k = pl.program_id(2)
    def partial_product():
        part = jnp.dot(lhs[0][0], ws[0][...], preferred_element_type=F32)
        for j in range(1, n_lhs):
            part = part + jnp.dot(lhs[j][0], ws[j][...], preferred_element_type=F32)
        return part

    @pl.when(k == 0)
    def _():
        acc_ref[...] = partial_product()

    @pl.when(k > 0)
    def _():
        acc_ref[...] += partial_product()

    @pl.when(k == nk - 1)
    def _():
        gate1 = 1.0 + gate_ref[0]
        g, b = g_ref[...], b_ref[...]
        if with_next:
            scale1, shift = 1.0 + scale_ref[0], shift_ref[0]

        def rows_body(r, carry):
            rows = pl.ds(pl.multiple_of(r * LN_ROWS, LN_ROWS), LN_ROWS)
            z = ALPHA * x_ref[0, rows, :] + gate1 * acc_ref[rows, :]
            zc = z - jnp.mean(z, axis=-1, keepdims=True)
            var = jnp.mean(zc * zc, axis=-1, keepdims=True)
            xn = zc * lax.rsqrt(var + LN_EPS) * g + b
            xo_ref[0, rows, :] = xn
            if with_next:
                h_ref[0, rows, :] = (xn * scale1 + shift).astype(h_ref.dtype)
            return carry

        lax.fori_loop(0, tm // LN_ROWS, rows_body, 0, unroll=8)


def _proj_ln_call(lhs, ws, x, mods3, i, g, b, *, with_next, tm, tk):
    B, T, D = x.shape
    n_lhs = len(lhs)
    nk = lhs[0].shape[2] // tk
    blk = pl.BlockSpec((1, tm, D), lambda b, t, k: (b, t, 0))
    row = pl.BlockSpec((1, D), lambda b, t, k: (0, 0))
    mod = lambda ii, part: pl.BlockSpec((1, 1, D), lambda b, t, k: (ii * B + b, 0, part))
    wspec = lambda idx, k0: pl.BlockSpec((None, tk, D), lambda b, t, k: (idx, k0 + k, 0))
    in_specs = ([pl.BlockSpec((1, tm, tk), lambda b, t, k: (b, t, k))] * n_lhs
                + [wspec(idx, k0) for _, idx, k0 in ws]
                + [blk, mod(i, 2), row, row])
    args = list(lhs) + [w for w, _, _ in ws] + [x, mods3, g.reshape(1, D), b.reshape(1, D)]
    out_specs = [blk]
    out_shape = [jax.ShapeDtypeStruct(x.shape, F32)]
    if with_next:
        in_specs += [mod(i + 1, 0), mod(i + 1, 1)]
        args += [mods3, mods3]
        out_specs.append(blk)
        out_shape.append(jax.ShapeDtypeStruct(x.shape, BF16))
    return pl.pallas_call(
        functools.partial(_proj_ln_kernel, n_lhs=n_lhs, nk=nk, tm=tm, with_next=with_next),
        grid=(B, T // tm, nk),
        in_specs=in_specs, out_specs=out_specs, out_shape=out_shape,
        scratch_shapes=[pltpu.VMEM((tm, D), F32)],
        compiler_params=pltpu.CompilerParams(dimension_semantics=("arbitrary", "arbitrary", "arbitrary"),
                                             vmem_limit_bytes=VMEM_LIMIT),
        name="proj_residual_ln",
    )(*args)


def _matmul_kernel(a_ref, b_ref, o_ref, *scratch, nk):
    if nk == 1:
        o_ref[...] = jnp.dot(a_ref[...], b_ref[...], preferred_element_type=F32).astype(o_ref.dtype)
        return
    (acc_ref,) = scratch
    k = pl.program_id(2)

    @pl.when(k == 0)
    def _():
        acc_ref[...] = jnp.zeros_like(acc_ref)

    acc_ref[...] += jnp.dot(a_ref[...], b_ref[...], preferred_element_type=F32)

    @pl.when(k == nk - 1)
    def _():
        o_ref[...] = acc_ref[...].astype(o_ref.dtype)


def _matmul_call(a, b, *, tm, tn, tk, out_dtype=F32):
    M, K = a.shape
    _, N = b.shape
    nk = K // tk
    return pl.pallas_call(
        functools.partial(_matmul_kernel, nk=nk),
        grid=(N // tn, M // tm, nk),
        in_specs=[pl.BlockSpec((tm, tk), lambda j, i, k: (i, k)),
                  pl.BlockSpec((tk, tn), lambda j, i, k: (k, j))],
        out_specs=pl.BlockSpec((tm, tn), lambda j, i, k: (i, j)),
        out_shape=jax.ShapeDtypeStruct((M, N), out_dtype),
        scratch_shapes=[] if nk == 1 else [pltpu.VMEM((tm, tn), F32)],
        compiler_params=pltpu.CompilerParams(dimension_semantics=("arbitrary", "arbitrary", "arbitrary"),
                                             vmem_limit_bytes=VMEM_LIMIT),
        name="matmul",
    )(a, b)


XPOSE_ROWS = 256


def _matmul_f32wt_kernel(a_ref, wt_ref, o_ref, wb_ref):
    @pl.when(pl.program_id(1) == 0)
    def _():
        for r in range(0, wt_ref.shape[0], XPOSE_ROWS):
            wb_ref[:, r:r + XPOSE_ROWS] = wt_ref[r:r + XPOSE_ROWS, :].T.astype(BF16)

    o_ref[...] = jnp.dot(a_ref[...], wb_ref[...], preferred_element_type=F32)


def _matmul_f32wt_call(a, wt, idx, ncols, *, tm=1024, tn=1024):
    M, K = a.shape
    tm = min(tm, M)
    return pl.pallas_call(
        _matmul_f32wt_kernel,
        grid=(ncols // tn, M // tm),
        in_specs=[pl.BlockSpec((tm, K), lambda j, i: (i, 0)),
                  pl.BlockSpec((None, tn, K), lambda j, i: (idx, j, 0))],
        out_specs=pl.BlockSpec((tm, tn), lambda j, i: (i, j)),
        out_shape=jax.ShapeDtypeStruct((M, ncols), F32),
        scratch_shapes=[pltpu.VMEM((K, tn), BF16)],
        compiler_params=pltpu.CompilerParams(dimension_semantics=("arbitrary", "arbitrary"),
                                             vmem_limit_bytes=VMEM_LIMIT),
        name="matmul_f32wt",
    )(a, wt)


def _regroup_rows_kernel(wt_ref, o_ref, *, segments):
    tc = wt_ref.shape[1]
    pieces, pos = [], 0
    for src, width, dst in segments:
        if dst > pos:
            pieces.append(jnp.zeros((dst - pos, tc), F32))
        pieces.append(wt_ref[src:src + width, :])
        pos = dst + width
    if pos < o_ref.shape[1]:
        pieces.append(jnp.zeros((o_ref.shape[1] - pos, tc), F32))
    stacked = jnp.concatenate(pieces, axis=0)
    for r in range(0, o_ref.shape[1], LANES):
        o_ref[:, r:r + LANES] = stacked[r:r + LANES, :].T.astype(o_ref.dtype)


def _regroup_rows_call(wt, idx, segments, out_cols, *, block_rows, block_index, tc):
    _, _, K = wt.shape
    return pl.pallas_call(
        functools.partial(_regroup_rows_kernel, segments=segments),
        grid=(K // tc,),
        in_specs=[pl.BlockSpec((None, block_rows, tc), lambda c: (idx, block_index, c))],
        out_specs=pl.BlockSpec((tc, out_cols), lambda c: (c, 0)),
        out_shape=jax.ShapeDtypeStruct((K, out_cols), BF16),
        compiler_params=pltpu.CompilerParams(dimension_semantics=("arbitrary",), vmem_limit_bytes=VMEM_LIMIT),
        name="regroup_rows",
    )(wt)


def _swiglu_up_kernel(h_ref, wg_ref, wu_ref, o_ref, wgb_ref, wub_ref):
    @pl.when(pl.program_id(1) == 0)
    def _():
        wgb_ref[...] = wg_ref[...].astype(BF16)
        wub_ref[...] = wu_ref[...].astype(BF16)

    h = h_ref[...]
    g = jnp.dot(h, wgb_ref[...], preferred_element_type=F32)
    u = jnp.dot(h, wub_ref[...], preferred_element_type=F32)
    o_ref[...] = (_silu(g) * u).astype(o_ref.dtype)


def _swiglu_up_call(h, wg, wu, layer, *, tm=1024, tn=512):
    M, K = h.shape
    _, _, N = wg.shape
    tm = min(tm, M)
    wspec = pl.BlockSpec((None, K, tn), lambda j, i: (layer, 0, j))
    return pl.pallas_call(
        _swiglu_up_kernel,
        grid=(N // tn, M // tm),
        in_specs=[pl.BlockSpec((tm, K), lambda j, i: (i, 0)), wspec, wspec],
        out_specs=pl.BlockSpec((tm, tn), lambda j, i: (i, j)),
        out_shape=jax.ShapeDtypeStruct((M, N), BF16),
        scratch_shapes=[pltpu.VMEM((K, tn), BF16), pltpu.VMEM((K, tn), BF16)],
        compiler_params=pltpu.CompilerParams(dimension_semantics=("arbitrary", "arbitrary"),
                                             vmem_limit_bytes=VMEM_LIMIT),
        name="swiglu_up",
    )(h, wg, wu)


A_MAIN = 4 * D_GROUP
B_COLS_PAD = 3 * D_GROUP + RWKV_LOW + LANES
MLSTM_GATE_COL = 3 * D_GROUP + RWKV_LOW
CD_MAIN = 8 * D_GROUP
GDN_COL0 = 4 * D_GROUP


N_GATES = 2 * N_HEADS
B_SRC = A_MAIN + N_GATES
B_LOW = 64 + 64 + 160
B_SEGMENTS = ((B_SRC, 3 * D_GROUP, 0), (B_SRC + 3 * D_GROUP, B_LOW, 3 * D_GROUP), (A_MAIN, N_GATES, MLSTM_GATE_COL))
CD_GATE_SEGMENTS = ((0, N_GATES, 0),)


def kernel(x, c, positions, ada_w, ada_b, ln_g, ln_b, ab_w_in, ab_w_out, mlstm_conv_w, mlstm_gate_b, mlstm_norm_g, rwkv_mu, rwkv_w0, rwkv_w2, rwkv_a0, rwkv_a2, rwkv_g2, rwkv_k_k, rwkv_k_a, rwkv_r_k, rwkv_ln_g, rwkv_ln_b, cd_w_in, cd_w_out, ret_norm_g, gdn_conv_w, gdn_a_log, gdn_dt_bias, gdn_norm_g, ffn_w_gate, ffn_w_up, ffn_w_down):
    B, T, D = x.shape
    M = B * T
    depth = ada_w.shape[0]
    mods = _ada_call(c, ada_w, ada_b)
    mods3 = mods.reshape(2 * depth * B, 1, 3 * D)
    h = _modulate_call(x, mods3, 0)
    w_down = ffn_w_down.astype(BF16)
    for layer in range(depth):
        j = layer // 2
        i_mix, i_ffn = 2 * layer, 2 * layer + 1
        if layer % 2 == 0:
            h2 = h.reshape(M, D)
            wt = jnp.swapaxes(ab_w_in, 1, 2)
            proj_a = _matmul_f32wt_call(h2, wt, j, A_MAIN).reshape(B, T, A_MAIN)
            w_b = _regroup_rows_call(wt, j, B_SEGMENTS, B_COLS_PAD, block_rows=wt.shape[1], block_index=0, tc=256)
            proj_b = _matmul_call(h2, w_b, tm=min(1024, M), tn=B_COLS_PAD // 2, tk=D)
            proj_b = proj_b.reshape(B, T, B_COLS_PAD)
            ya = _mlstm_call(proj_a, proj_b, mlstm_conv_w[j], mlstm_gate_b[j], mlstm_norm_g[j],
                             gate_col=MLSTM_GATE_COL)
            yb = _rwkv_call(proj_b, rwkv_mu[j], rwkv_w0[j], rwkv_w2[j], rwkv_a0[j], rwkv_a2[j], rwkv_g2[j],
                            rwkv_k_k[j], rwkv_k_a[j], rwkv_r_k[j], rwkv_ln_g[j], rwkv_ln_b[j], col0=0)
            w_out = ab_w_out
        else:
            h2 = h.reshape(M, D)
            wt = jnp.swapaxes(cd_w_in, 1, 2)
            proj = _matmul_f32wt_call(h2, wt, j, CD_MAIN).reshape(B, T, CD_MAIN)
            w_gates = _regroup_rows_call(wt, j, CD_GATE_SEGMENTS, LANES, block_rows=N_GATES,
                                         block_index=CD_MAIN // N_GATES, tc=D)
            proj_g = _matmul_call(h2, w_gates, tm=512, tn=LANES, tk=D).reshape(B, T, LANES)
            ya = _ret_call(proj, positions, ret_norm_g[j])
            yb = _gdn_call(proj, proj_g, gdn_conv_w[j], gdn_a_log[j], gdn_dt_bias[j], gdn_norm_g[j],
                           col0=GDN_COL0, gate_col=0)
            w_out = cd_w_out
        w_out = w_out.astype(BF16)
        x, h = _proj_ln_call([ya, yb], [(w_out, j, 0), (w_out, j, 1)], x, mods3, i_mix,
                             ln_g[layer, 0], ln_b[layer, 0], with_next=True, tm=512, tk=D_GROUP)
        act = _swiglu_up_call(h.reshape(M, D), ffn_w_gate, ffn_w_up, layer)
        last = layer == depth - 1
        res = _proj_ln_call([act.reshape(B, T, D_FF)], [(w_down, layer, 0)], x, mods3, i_ffn,
                            ln_g[layer, 1], ln_b[layer, 1], with_next=not last, tm=512, tk=1408)
        if last:
            (x,) = res
        else:
            x, h = res
    return x
```

```python
import functools
import math

import numpy as np
import jax
import jax.numpy as jnp
from jax import lax
from jax.experimental import pallas as pl
from jax.experimental.pallas import tpu as pltpu

F32 = jnp.float32
BF16 = jnp.bfloat16

D_MODEL = 2048
D_GROUP = 1024
HEAD_DIM = 128
N_HEADS = 8
RWKV_HEAD = 64
N_RWKV = 16
CHUNK = 64
D_FF = 5632
DEPTH = 2
ALPHA = (2 * DEPTH) ** 0.25
LN_EPS = 1e-5
RWKV_LN_EPS = 64e-5
ROPE_BASE = 10000.0
RET_GAMMA_BASE = 5.0
LANES = 128
VMEM_LIMIT = 48 * 1024 * 1024


def _dot(a, b):
    return jnp.dot(a.astype(BF16), b.astype(BF16), preferred_element_type=F32)


def _dot_nt(a, b):
    return lax.dot_general(a.astype(BF16), b.astype(BF16), (((1,), (1,)), ((), ())),
                           preferred_element_type=F32)


def _split3(x):
    hi = x.astype(BF16)
    r1 = x - hi.astype(F32)
    mid = r1.astype(BF16)
    lo = (r1 - mid.astype(F32)).astype(BF16)
    return hi, mid, lo


def _cumsum_rows(x):
    n = x.shape[1]
    out = jnp.dot(_tri(CHUNK).astype(BF16), jnp.concatenate(_split3(x), axis=1), preferred_element_type=F32)
    return out[:, :n] + out[:, n:2 * n] + out[:, 2 * n:]


def _half_lane_sums(x, m0, lo):
    s0 = jnp.sum(x * m0, axis=-1, keepdims=True)
    s1 = jnp.sum(x * (1.0 - m0), axis=-1, keepdims=True)
    return jnp.where(lo, s0, s1)


def _sigmoid(x):
    return 1.0 / (1.0 + jnp.exp(-x))


def _silu(x):
    return x * _sigmoid(x)


def _log_sigmoid(x):
    return jnp.minimum(x, 0.0) - jnp.log1p(jnp.exp(-jnp.abs(x)))


def _softplus(x):
    return jnp.maximum(x, 0.0) + jnp.log1p(jnp.exp(-jnp.abs(x)))


def _tri(n, strict=False):
    r = lax.broadcasted_iota(jnp.int32, (n, n), 0)
    c = lax.broadcasted_iota(jnp.int32, (n, n), 1)
    return (r > c) if strict else (r >= c)


def _conv_silu_rows(src_ref, halo_ref, w, first, r0, nrows, c0, ncols):
    cur = src_ref[0, r0:r0 + nrows, c0:c0 + ncols]
    acc = w[3:4] * cur
    if r0 == 0:
        hl = jnp.where(first, 0.0, halo_ref[0, :, c0:c0 + ncols])
        ext = jnp.concatenate([hl, cur[0:8]], axis=0)
        for j in range(3):
            head = ext[5 + j:13 + j]
            if nrows > 8:
                rest = src_ref[0, 5 + j:nrows - 3 + j, c0:c0 + ncols]
                sh = jnp.concatenate([head, rest], axis=0)
            else:
                sh = head
            acc = acc + w[j:j + 1] * sh
    else:
        for j in range(3):
            acc = acc + w[j:j + 1] * src_ref[0, r0 - 3 + j:r0 - 3 + j + nrows, c0:c0 + ncols]
    return _silu(acc)


def _head_norm_rows(h, g_row, eps, center=True):
    if center:
        h = h - jnp.mean(h, axis=-1, keepdims=True)
    return h * lax.rsqrt(jnp.mean(h * h, axis=-1, keepdims=True) + eps) * g_row


MLSTM_CHUNK_GROUP = 4


def _cummax_rows(x):
    row = lax.broadcasted_iota(jnp.int32, x.shape, 0)
    d = 1
    while d < x.shape[0]:
        x = jnp.where(row >= d, jnp.maximum(x, pltpu.roll(x, d, 0)), x)
        d *= 2
    return x


def _bcast_head_cols(x, sel, pieces):
    m = x.shape[0]
    parts, rest = [], x
    for _ in range(pieces):
        hi = rest.astype(BF16)
        parts.append(hi)
        rest = rest - hi.astype(F32)
    out = jnp.dot(jnp.concatenate(parts, axis=0), sel, preferred_element_type=F32)
    acc = out[:m]
    for i in range(1, pieces):
        acc = acc + out[i * m:(i + 1) * m]
    return acc


def _mlstm_kernel(q_ref, k_ref, v_ref, o_ref, g_ref, qh_ref, kh_ref, cw_ref, gb_ref, ng_ref,
                  out_ref, qc_ref, kc_ref, sv_ref, rs_ref, hh_ref, b0_ref, cm_ref, kvn_ref, CN_ref, m_ref, *, tb):
    t = pl.program_id(1)
    first = t == 0

    @pl.when(first)
    def _():
        CN_ref[...] = jnp.zeros_like(CN_ref)
        m_ref[...] = jnp.zeros_like(m_ref)

    for c in range(tb // CHUNK):
        for cb in range(D_GROUP // 256):
            cs = cb * 256
            qc_ref[c * CHUNK:(c + 1) * CHUNK, cs:cs + 256] = _conv_silu_rows(
                q_ref, qh_ref, cw_ref[:, cs:cs + 256], first, c * CHUNK, CHUNK, cs, 256)
            kc_ref[c * CHUNK:(c + 1) * CHUNK, cs:cs + 256] = _conv_silu_rows(
                k_ref, kh_ref, cw_ref[:, D_GROUP + cs:D_GROUP + cs + 256], first, c * CHUNK, CHUNK, cs, 256)

    causal = _tri(CHUNK)
    gb = gb_ref[...]
    scale = HEAD_DIM ** -0.5
    n_chunks = tb // CHUNK
    heads = range(N_HEADS)
    hs = [slice(h * HEAD_DIM, (h + 1) * HEAD_DIM) for h in heads]
    head_lane = lax.broadcasted_iota(jnp.int32, (1, LANES), 1) < N_HEADS
    sel = (lax.broadcasted_iota(jnp.int32, (LANES, N_HEADS * LANES), 0)
           == (lax.broadcasted_iota(jnp.int32, (LANES, N_HEADS * LANES), 1) >> 7)).astype(BF16)
    ones = jnp.ones((CHUNK, HEAD_DIM), F32)
    last = slice(CHUNK - 1, CHUNK)

    for c0 in range(0, n_chunks, MLSTM_CHUNK_GROUP):
        chunks = range(c0, c0 + MLSTM_CHUNK_GROUP)
        rows = {c: slice(c * CHUNK, (c + 1) * CHUNK) for c in chunks}
        z = {c: g_ref[0, rows[c], :] + gb for c in chunks}
        b0 = {c: pltpu.roll(_cumsum_rows(_log_sigmoid(z[c])), LANES - N_HEADS, 1) for c in chunks}
        cv = {c: jnp.where(head_lane, z[c] - b0[c], 0.0) for c in chunks}
        cm = {c: _cummax_rows(cv[c]) for c in chunks}
        cT = {c: cv[c].T for c in chunks}
        cmb = {c: _bcast_head_cols(cm[c], sel, 3) for c in chunks}
        e1b = {c: _bcast_head_cols(jnp.where(head_lane, jnp.exp(cv[c] - cm[c][last]), 0.0), sel, 2) for c in chunks}
        for c in chunks:
            b0_ref[rows[c], :] = b0[c]
            cm_ref[rows[c], :] = cm[c]
        items = [(c, h) for c in chunks for h in heads]
        n_items = range(len(items))
        q = [qc_ref[rows[c], hs[h]] for c, h in items]
        k = [kc_ref[rows[c], hs[h]] * scale for c, h in items]
        vo = [jnp.concatenate([v_ref[0, rows[c], hs[h]], ones], axis=1).astype(BF16) for c, h in items]
        qk = [_dot_nt(q[i], k[i]) for i in n_items]
        s = [qk[i] * jnp.where(causal, jnp.exp(jnp.minimum(cT[c][h:h + 1, :] - cmb[c][:, h * LANES:h * LANES + CHUNK],
                                                           0.0)), 0.0) for i, (c, h) in enumerate(items)]
        s_hi = [s[i].astype(BF16) for i in n_items]
        s_lo = [(s[i] - s_hi[i].astype(F32)).astype(BF16) for i in n_items]
        svr = [jnp.dot(jnp.concatenate([s_hi[i], s_lo[i]], axis=0), vo[i], preferred_element_type=F32)
               for i in n_items]
        kvn = [_dot((k[i] * e1b[c][:, hs[h]]).T, vo[i]) for i, (c, h) in enumerate(items)]
        for i, (c, h) in enumerate(items):
            sv_ref[rows[c], hs[h]] = svr[i][:CHUNK, :HEAD_DIM] + svr[i][CHUNK:, :HEAD_DIM]
            rs_ref[rows[c], hs[h]] = svr[i][:CHUNK, HEAD_DIM:] + svr[i][CHUNK:, HEAD_DIM:]
            kvn_ref[c * N_HEADS + h] = kvn[i]

    for c in range(n_chunks):
        rows = slice(c * CHUNK, (c + 1) * CHUNK)
        m = m_ref[0:1, :]
        b0 = b0_ref[rows, :]
        cm = cm_ref[rows, :]
        mx = jnp.maximum(cm, m)
        m_new = jnp.maximum(b0[last] + m, b0[last] + cm[last])
        m_ref[0:1, :] = m_new
        zero = lambda x: jnp.where(head_lane, x, 0.0)
        fib = _bcast_head_cols(zero(jnp.exp(cm - mx)), sel, 2)
        scb = _bcast_head_cols(zero(jnp.exp(m - mx)), sel, 2)
        emtb = _bcast_head_cols(zero(jnp.exp(jnp.minimum(-(b0 + mx), 80.0))), sel, 2)
        dfb = _bcast_head_cols(jnp.concatenate([zero(jnp.exp(b0[last] + m - m_new)),
                                                zero(jnp.exp(b0[last] + cm[last] - m_new)),
                                                jnp.zeros((6, LANES), F32)], axis=0), sel, 2)
        CN = [CN_ref[h] for h in heads]
        qcn = [_dot(qc_ref[rows, hs[h]], CN[h]) for h in heads]
        for h in heads:
            num = fib[:, hs[h]] * sv_ref[rows, hs[h]] + scb[:, hs[h]] * qcn[h][:, :HEAD_DIM]
            den = fib[:, hs[h]] * rs_ref[rows, hs[h]] + scb[:, hs[h]] * qcn[h][:, HEAD_DIM:]
            hh_ref[rows, hs[h]] = num / jnp.maximum(jnp.abs(den), emtb[:, hs[h]])
            dec = jnp.concatenate([dfb[0:1, hs[h]]] * 2, axis=1)
            fkv = jnp.concatenate([dfb[1:2, hs[h]]] * 2, axis=1)
            CN_ref[h] = CN[h] * dec + kvn_ref[c * N_HEADS + h] * fkv

    for c in range(n_chunks):
        rows = slice(c * CHUNK, (c + 1) * CHUNK)
        hn = [_head_norm_rows(hh_ref[rows, hs[h]], ng_ref[:, hs[h]], LN_EPS) for h in heads]
        for h in heads:
            out_ref[0, rows, hs[h]] = (hn[h] * _sigmoid(o_ref[0, rows, hs[h]])).astype(out_ref.dtype)


def _mlstm_call(proj, gproj, conv_w, gate_b, norm_g, *, gate_col, tb=256):
    B, T, _ = proj.shape
    nt = T // tb
    gb = jnp.zeros((1, LANES), F32).at[0, :2 * N_HEADS].set(gate_b)
    ng = norm_g.reshape(1, D_GROUP)
    colblk = lambda j: pl.BlockSpec((1, tb, D_GROUP), lambda b, t: (b, t, j))
    halo = lambda j: pl.BlockSpec((1, 8, D_GROUP), lambda b, t: (b, jnp.maximum(t * (tb // 8) - 1, 0), j))
    return pl.pallas_call(
        functools.partial(_mlstm_kernel, tb=tb),
        grid=(B, nt),
        in_specs=[colblk(0), colblk(1), colblk(2), colblk(3),
                  pl.BlockSpec((1, tb, LANES), lambda b, t: (b, t, gate_col // LANES)),
                  halo(0), halo(1),
                  pl.BlockSpec((4, 2 * D_GROUP), lambda b, t: (0, 0)),
                  pl.BlockSpec((1, LANES), lambda b, t: (0, 0)),
                  pl.BlockSpec((1, D_GROUP), lambda b, t: (0, 0))],
        out_specs=pl.BlockSpec((1, tb, D_GROUP), lambda b, t: (b, t, 0)),
        out_shape=jax.ShapeDtypeStruct((B, T, D_GROUP), BF16),
        scratch_shapes=[pltpu.VMEM((tb, D_GROUP), F32)] * 5
                       + [pltpu.VMEM((tb, LANES), F32)] * 2
                       + [pltpu.VMEM((tb // CHUNK * N_HEADS, HEAD_DIM, 2 * HEAD_DIM), F32),
                          pltpu.VMEM((N_HEADS, HEAD_DIM, 2 * HEAD_DIM), F32),
                          pltpu.VMEM((8, LANES), F32)],
        compiler_params=pltpu.CompilerParams(dimension_semantics=("arbitrary", "arbitrary"),
                                             vmem_limit_bytes=VMEM_LIMIT),
        name="mlstm",
    )(proj, proj, proj, proj, gproj, proj, proj, conv_w, gb, ng)


RET_CHUNK_GROUP = 1


def _ret_kernel(q_ref, k_ref, v_ref, g_ref, pos_ref, inv_ref, ng_ref, out_ref, qr_ref, o_ref, kv_ref, R_ref, *, tb):
    t = pl.program_id(1)

    @pl.when(t == 0)
    def _():
        R_ref[...] = jnp.zeros_like(R_ref)

    n_chunks = tb // CHUNK
    heads = range(N_HEADS)
    hs = [slice(h * HEAD_DIM, (h + 1) * HEAD_DIM) for h in heads]
    lg = [math.log1p(-2.0 ** (-RET_GAMMA_BASE - h)) for h in heads]
    causal = _tri(CHUNK)
    ri = lax.broadcasted_iota(jnp.int32, (CHUNK, CHUNK), 0)
    ci = lax.broadcasted_iota(jnp.int32, (CHUNK, CHUNK), 1)
    rel = (ri - ci).astype(F32)
    tcol = lax.broadcasted_iota(jnp.int32, (CHUNK, 1), 0).astype(F32)
    scale = HEAD_DIM ** -0.5
    lane = lax.broadcasted_iota(jnp.int32, (1, HEAD_DIM), 1)
    sign = jnp.where(lane < HEAD_DIM // 2, -1.0, 1.0)
    dmat = [jnp.where(causal, jnp.exp(rel * lg[h]), 0.0) for h in heads]

    for c0 in range(0, n_chunks, RET_CHUNK_GROUP):
        chunks = range(c0, c0 + RET_CHUNK_GROUP)
        rows = {c: slice(c * CHUNK, (c + 1) * CHUNK) for c in chunks}
        ang = {c: pos_ref[0, rows[c], :].astype(F32) * inv_ref[...] for c in chunks}
        cos2 = {c: jnp.cos(ang[c]) for c in chunks}
        sin2 = {c: jnp.sin(ang[c]) * sign for c in chunks}
        items = [(c, h) for c in chunks for h in heads]
        n_items = range(len(items))
        rot = lambda z, c: z * cos2[c] + pltpu.roll(z, HEAD_DIM // 2, 1) * sin2[c]
        qr = [rot(q_ref[0, rows[c], hs[h]], c) for c, h in items]
        kr = [rot(k_ref[0, rows[c], hs[h]], c) * scale for c, h in items]
        v = [v_ref[0, rows[c], hs[h]] for c, h in items]
        qk = [_dot_nt(qr[i], kr[i]) * dmat[h] for i, (c, h) in enumerate(items)]
        intra = [_dot(qk[i], v[i]) for i in n_items]
        kv = [_dot((kr[i] * jnp.exp((CHUNK - 1.0 - tcol) * lg[h])).T, v[i]) for i, (c, h) in enumerate(items)]
        for i, (c, h) in enumerate(items):
            qr_ref[rows[c], hs[h]] = qr[i]
            o_ref[rows[c], hs[h]] = intra[i]
            kv_ref[c * N_HEADS + h] = kv[i]

    for h in heads:
        R = R_ref[h]
        for c in range(n_chunks):
            inc = kv_ref[c * N_HEADS + h]
            kv_ref[c * N_HEADS + h] = R
            R = R * math.exp(CHUNK * lg[h]) + inc
        R_ref[h] = R
    for c in range(n_chunks):
        rows = slice(c * CHUNK, (c + 1) * CHUNK)
        inter = [_dot(qr_ref[rows, hs[h]], kv_ref[c * N_HEADS + h]) * jnp.exp((tcol + 1.0) * lg[h]) for h in heads]
        on = [_head_norm_rows(o_ref[rows, hs[h]] + inter[h], ng_ref[:, hs[h]], LN_EPS) for h in heads]
        for h in heads:
            out_ref[0, rows, hs[h]] = (on[h] * _silu(g_ref[0, rows, hs[h]])).astype(out_ref.dtype)


def _ret_call(proj, positions, norm_g, *, tb=256):
    B, T, _ = proj.shape
    half = HEAD_DIM // 2
    inv_freq = ROPE_BASE ** (-jnp.arange(half, dtype=F32) / half)
    inv2 = jnp.concatenate([inv_freq, inv_freq]).reshape(1, HEAD_DIM)
    colblk = lambda j: pl.BlockSpec((1, tb, D_GROUP), lambda b, t: (b, t, j))
    big = pltpu.VMEM((tb, D_GROUP), F32)
    return pl.pallas_call(
        functools.partial(_ret_kernel, tb=tb),
        grid=(B, T // tb),
        in_specs=[colblk(0), colblk(1), colblk(2), colblk(3),
                  pl.BlockSpec((1, tb, 1), lambda b, t: (b, t, 0)),
                  pl.BlockSpec((1, HEAD_DIM), lambda b, t: (0, 0)),
                  pl.BlockSpec((1, D_GROUP), lambda b, t: (0, 0))],
        out_specs=pl.BlockSpec((1, tb, D_GROUP), lambda b, t: (b, t, 0)),
        out_shape=jax.ShapeDtypeStruct((B, T, D_GROUP), BF16),
        scratch_shapes=[big, big, pltpu.VMEM((tb // CHUNK * N_HEADS, HEAD_DIM, HEAD_DIM), F32),
                        pltpu.VMEM((N_HEADS, HEAD_DIM, HEAD_DIM), F32)],
        compiler_params=pltpu.CompilerParams(dimension_semantics=("arbitrary", "arbitrary"),
                                             vmem_limit_bytes=VMEM_LIMIT),
        name="retention",
    )(proj, proj, proj, proj, positions.reshape(B, T, 1), inv2, norm_g.reshape(1, D_GROUP))


def _inv_unit_lower(nms):
    n = nms[0].shape[0]
    eye = (lax.broadcasted_iota(jnp.int32, (n, n), 0) == lax.broadcasted_iota(jnp.int32, (n, n), 1)).astype(F32)
    ps = [eye + nm for nm in nms]
    xs = [_dot(nm, nm) for nm in nms]
    for _ in range(int(math.log2(n)) - 2):
        px = [_dot(jnp.concatenate([p, x], axis=0), x) for p, x in zip(ps, xs)]
        ps = [p + y[:n] for p, y in zip(ps, px)]
        xs = [y[n:] for y in px]
    ps = [p + _dot(p, x) for p, x in zip(ps, xs)]
    resid = [eye - p + _dot(nm, p) for p, nm in zip(ps, nms)]
    return [p + _dot(p, r) for p, r in zip(ps, resid)]


def _blockdiag2(x, m0, m1):
    xb = x.astype(BF16)
    return jnp.concatenate([xb * m0.astype(BF16), xb * m1.astype(BF16)], axis=0)


def _inv_unit_lower_packed(nms, m0, m1):
    n = nms[0].shape[0]
    r = lax.broadcasted_iota(jnp.int32, (n, 2 * n), 0)
    c = lax.broadcasted_iota(jnp.int32, (n, 2 * n), 1)
    eye2 = (r == (c & (n - 1))).astype(F32)
    bd = lambda x: _blockdiag2(x, m0, m1)
    ps = [eye2 + nm for nm in nms]
    xs = [_dot(nm, bd(nm)) for nm in nms]
    for _ in range(int(math.log2(n)) - 2):
        px = [_dot(jnp.concatenate([p, x], axis=0), bd(x)) for p, x in zip(ps, xs)]
        ps = [p + y[:n] for p, y in zip(ps, px)]
        xs = [y[n:] for y in px]
    ps = [p + _dot(p, bd(x)) for p, x in zip(ps, xs)]
    resid = [eye2 - p + _dot(nm, bd(p)) for p, nm in zip(ps, nms)]
    return [p + _dot(p, bd(r_)) for p, r_ in zip(ps, resid)]


def _solve_unit_lower(nms, rhss):
    n = nms[0].shape[0]
    eye = (lax.broadcasted_iota(jnp.int32, (n, n), 0) == lax.broadcasted_iota(jnp.int32, (n, n), 1)).astype(F32)
    ps = [eye + nm for nm in nms]
    xs = [_dot(nm, nm) for nm in nms]
    for _ in range(int(math.log2(n)) - 2):
        px = [_dot(jnp.concatenate([p, x], axis=0), x) for p, x in zip(ps, xs)]
        ps = [p + y[:n] for p, y in zip(ps, px)]
        xs = [y[n:] for y in px]
    ps = [p + _dot(p, x) for p, x in zip(ps, xs)]
    x0 = [_dot(p, r) for p, r in zip(ps, rhss)]
    resid = [r - a + _dot(nm, a) for r, a, nm in zip(rhss, x0, nms)]
    return [a + _dot(p, r) for a, p, r in zip(x0, ps, resid)]


def _l2norm_rows(z):
    return z * lax.rsqrt(jnp.sum(z * z, axis=-1, keepdims=True) + 1e-6)


GDN_CHUNK_GROUP = 2


def _gdn_kernel(q_ref, k_ref, v_ref, z_ref, g_ref, qh_ref, kh_ref, vh_ref, cw_ref, an_ref, dt_ref, ng_ref,
                out_ref, qc_ref, kc_ref, vc_ref, u_ref, w_ref, qe_ref, o_ref, att_ref, kdT_ref, gl_ref, S_ref,
                *, tb):
    t = pl.program_id(1)
    first = t == 0

    @pl.when(first)
    def _():
        S_ref[...] = jnp.zeros_like(S_ref)

    srcs = ((q_ref, qh_ref, qc_ref), (k_ref, kh_ref, kc_ref), (v_ref, vh_ref, vc_ref))
    for c in range(tb // CHUNK):
        for cb in range(D_GROUP // 256):
            cs = cb * 256
            for i, (src, halo, dst) in enumerate(srcs):
                w = cw_ref[:, i * D_GROUP + cs:i * D_GROUP + cs + 256]
                dst[c * CHUNK:(c + 1) * CHUNK, cs:cs + 256] = _conv_silu_rows(
                    src, halo, w, first, c * CHUNK, CHUNK, cs, 256)

    causal = _tri(CHUNK)
    strict = _tri(CHUNK, strict=True)
    a_neg = an_ref[...]
    dtb = dt_ref[...]
    scale = HEAD_DIM ** -0.5
    n_chunks = tb // CHUNK
    heads = range(N_HEADS)
    hs = [slice(h * HEAD_DIM, (h + 1) * HEAD_DIM) for h in heads]

    for c0 in range(0, n_chunks, GDN_CHUNK_GROUP):
        chunks = range(c0, c0 + GDN_CHUNK_GROUP)
        rows = {c: slice(c * CHUNK, (c + 1) * CHUNK) for c in chunks}
        gz = {c: g_ref[0, rows[c], :] for c in chunks}
        beta = {c: _sigmoid(gz[c]) for c in chunks}
        gc = {c: _cumsum_rows(a_neg * _softplus(gz[c] + dtb)) for c in chunks}
        gcT = {c: gc[c].T for c in chunks}
        for c in chunks:
            gl_ref[c:c + 1, :] = gc[c][CHUNK - 1:CHUNK, :]
        items = [(c, h) for c in chunks for h in heads]
        n_items = range(len(items))
        gc_col = [gc[c][:, h:h + 1] for c, h in items]
        b_col = [beta[c][:, 8 + h:9 + h] for c, h in items]
        gamma = [jnp.where(causal, jnp.exp(gc_col[i] - gcT[c][h:h + 1, :]), 0.0) for i, (c, h) in enumerate(items)]
        q = [_l2norm_rows(qc_ref[rows[c], hs[h]]) * scale for c, h in items]
        k = [_l2norm_rows(kc_ref[rows[c], hs[h]]) for c, h in items]
        kb = [k[i] * b_col[i] for i in n_items]
        eg = [jnp.exp(gc_col[i]) for i in n_items]
        kq = [_dot_nt(jnp.concatenate([kb[i], q[i]], axis=0), k[i]) for i in n_items]
        inv = _inv_unit_lower([-jnp.where(strict, kq[i][:CHUNK] * gamma[i], 0.0) for i in n_items])
        uw = [_dot(inv[i], jnp.concatenate([vc_ref[rows[c], hs[h]] * b_col[i], kb[i] * eg[i]], axis=1))
              for i, (c, h) in enumerate(items)]
        for i, (c, h) in enumerate(items):
            u_ref[rows[c], hs[h]] = uw[i][:, :HEAD_DIM]
            w_ref[rows[c], hs[h]] = uw[i][:, HEAD_DIM:]
            qe_ref[rows[c], hs[h]] = q[i] * eg[i]
            att_ref[h, rows[c], :] = kq[i][CHUNK:] * gamma[i]
            g_last = gc[c][CHUNK - 1:CHUNK, h:h + 1]
            kdT_ref[c * N_HEADS + h] = (k[i] * jnp.exp(g_last - gc_col[i])).T

    for c in range(n_chunks):
        rows = slice(c * CHUNK, (c + 1) * CHUNK)
        S = [S_ref[h] for h in heads]
        ws = [_dot(jnp.concatenate([w_ref[rows, hs[h]], qe_ref[rows, hs[h]]], axis=0), S[h]) for h in heads]
        v_new = [u_ref[rows, hs[h]] - ws[h][:CHUNK] for h in heads]
        av = [_dot(att_ref[h, rows, :], v_new[h]) for h in heads]
        kv = [_dot(kdT_ref[c * N_HEADS + h], v_new[h]) for h in heads]
        for h in heads:
            S_ref[h] = S[h] * jnp.exp(gl_ref[c:c + 1, h:h + 1]) + kv[h]
            o_ref[rows, hs[h]] = ws[h][CHUNK:] + av[h]

    for c in range(n_chunks):
        rows = slice(c * CHUNK, (c + 1) * CHUNK)
        for h in heads:
            on = _head_norm_rows(o_ref[rows, hs[h]], ng_ref[:, hs[h]], 1e-6, center=False)
            out_ref[0, rows, hs[h]] = (on * _silu(z_ref[0, rows, hs[h]])).astype(out_ref.dtype)


def _gdn_call(proj, gproj, conv_w, a_log, dt_bias, norm_g, *, col0, gate_col, tb=256):
    B, T, _ = proj.shape
    j0 = col0 // D_GROUP
    an = jnp.zeros((1, LANES), F32).at[0, :N_HEADS].set(-jnp.exp(a_log.astype(F32)))
    dtb = jnp.zeros((1, LANES), F32).at[0, :N_HEADS].set(dt_bias)
    colblk = lambda j: pl.BlockSpec((1, tb, D_GROUP), lambda b, t: (b, t, j0 + j))
    halo = lambda j: pl.BlockSpec((1, 8, D_GROUP), lambda b, t: (b, jnp.maximum(t * (tb // 8) - 1, 0), j0 + j))
    return pl.pallas_call(
        functools.partial(_gdn_kernel, tb=tb),
        grid=(B, T // tb),
        in_specs=[colblk(0), colblk(1), colblk(2), colblk(3),
                  pl.BlockSpec((1, tb, LANES), lambda b, t: (b, t, gate_col // LANES)),
                  halo(0), halo(1), halo(2),
                  pl.BlockSpec((4, 3 * D_GROUP), lambda b, t: (0, 0)),
                  pl.BlockSpec((1, LANES), lambda b, t: (0, 0)),
                  pl.BlockSpec((1, LANES), lambda b, t: (0, 0)),
                  pl.BlockSpec((1, D_GROUP), lambda b, t: (0, 0))],
        out_specs=pl.BlockSpec((1, tb, D_GROUP), lambda b, t: (b, t, 0)),
        out_shape=jax.ShapeDtypeStruct((B, T, D_GROUP), BF16),
        scratch_shapes=[pltpu.VMEM((tb, D_GROUP), F32)] * 7
                       + [pltpu.VMEM((N_HEADS, tb, CHUNK), F32),
                          pltpu.VMEM((tb // CHUNK * N_HEADS, HEAD_DIM, CHUNK), F32),
                          pltpu.VMEM((max(tb // CHUNK, 8), LANES), F32),
                          pltpu.VMEM((N_HEADS, HEAD_DIM, HEAD_DIM), F32)],
        compiler_params=pltpu.CompilerParams(dimension_semantics=("arbitrary", "arbitrary"),
                                             vmem_limit_bytes=VMEM_LIMIT),
        name="gdn",
    )(proj, proj, proj, proj, gproj, proj, proj, proj, conv_w, an, dtb, norm_g.reshape(1, D_GROUP))


N_PAIRS = N_RWKV // 2
RWKV_LOW = 384


def _shift1_rows(src_ref, halo_ref, first, r0, nrows, c0, ncols):
    if r0 == 0:
        hl = jnp.where(first, 0.0, halo_ref[0, 7:8, c0:c0 + ncols])
        return jnp.concatenate([hl, src_ref[0, 0:nrows - 1, c0:c0 + ncols]], axis=0)
    return src_ref[0, r0 - 1:r0 - 1 + nrows, c0:c0 + ncols]


def _rwkv_kernel(r_ref, k_ref, v_ref, l0_ref, l1_ref, l2_ref,
                 rh_ref, kh_ref, vh_ref, l0h_ref, l1h_ref, l2h_ref,
                 mu_ref, mul_ref, w0_ref, w2_ref, a0_ref, a2_ref, g2_ref, kk_ref, ka_ref, rk_ref,
                 lng_ref, lnb_ref, out_ref,
                 gs_ref, bo_ref, y_ref, atrt_ref, avk_ref, yk_ref, inv_ref, arb_ref, btT_ref, kvT_ref, wlT_ref,
                 H_ref, *, tb):
    t = pl.program_id(1)
    first = t == 0

    @pl.when(first)
    def _():
        H_ref[...] = jnp.zeros_like(H_ref)

    ri = lax.broadcasted_iota(jnp.int32, (LANES, LANES), 0)
    ci = lax.broadcasted_iota(jnp.int32, (LANES, LANES), 1)
    same_head = ((ri // RWKV_HEAD) == (ci // RWKV_HEAD)).astype(F32)
    causal = _tri(CHUNK)
    strict = _tri(CHUNK, strict=True)
    lane1 = lax.broadcasted_iota(jnp.int32, (1, LANES), 1)
    m0 = (lane1 < RWKV_HEAD).astype(F32)
    m1 = 1.0 - m0
    t2 = lax.broadcasted_iota(jnp.int32, (CHUNK, LANES), 0)
    l2 = lax.broadcasted_iota(jnp.int32, (CHUNK, LANES), 1)
    lo = l2 < RWKV_HEAD
    s2 = l2 & (RWKV_HEAD - 1)
    causal2 = t2 >= s2
    strict2 = t2 > s2
    n_chunks = tb // CHUNK
    pairs = range(N_PAIRS)
    halves = [(p, hh) for p in pairs for hh in range(2)]
    ps = [slice(p * LANES, (p + 1) * LANES) for p in pairs]

    def lerp(src, halo, mu, r0, c0, ncols):
        cur = src[0, r0:r0 + CHUNK, c0:c0 + ncols]
        return cur + (_shift1_rows(src, halo, first, r0, CHUNK, c0, ncols) - cur) * mu

    for c in range(n_chunks):
        r0 = c * CHUNK
        rows = slice(r0, r0 + CHUNK)
        wl = lerp(l0_ref, l0h_ref, mul_ref[:, 0:LANES], r0, 0, LANES)
        g1 = lerp(l1_ref, l1h_ref, mul_ref[:, LANES:2 * LANES], r0, 0, LANES)
        g2 = lerp(l2_ref, l2h_ref, mul_ref[:, 2 * LANES:3 * LANES], r0, 0, LANES)
        wl_t = jnp.where(lo, jnp.tanh(wl), 0.0)
        al = jnp.where(lo, 0.0, wl)
        sg1 = _sigmoid(g1)
        sg2 = jnp.where(l2 < 32, _sigmoid(g2), 0.0)
        lw = [-math.exp(-0.5) * _sigmoid(w0_ref[:, ps[p]] + _dot(wl_t, w2_ref[:, ps[p]])) for p in pairs]
        a = [_sigmoid(a0_ref[:, ps[p]] + _dot(al, a2_ref[:, ps[p]])) for p in pairs]
        g = [_dot(sg1, g2_ref[0:LANES, ps[p]]) + _dot(sg2, g2_ref[LANES:2 * LANES, ps[p]]) for p in pairs]
        r = [lerp(r_ref, rh_ref, mu_ref[:, ps[p]], r0, p * LANES, LANES) for p in pairs]
        k = [lerp(k_ref, kh_ref, mu_ref[:, D_GROUP + p * LANES:D_GROUP + (p + 1) * LANES], r0, p * LANES, LANES)
             for p in pairs]
        v = [lerp(v_ref, vh_ref, mu_ref[:, 2 * D_GROUP + p * LANES:2 * D_GROUP + (p + 1) * LANES], r0, p * LANES, LANES)
             for p in pairs]
        kk = [k[p] * kk_ref[:, ps[p]] for p in pairs]
        nrm = [jnp.sqrt(_half_lane_sums(kk[p] * kk[p], m0, lo)) for p in pairs]
        kk = [kk[p] / jnp.maximum(nrm[p], 1e-12) for p in pairs]
        k2 = [k[p] * (1.0 + (a[p] - 1.0) * ka_ref[:, ps[p]]) for p in pairs]
        rk = [_half_lane_sums(r[p] * k2[p] * rk_ref[:, ps[p]], m0, lo) for p in pairs]
        for p in pairs:
            gs_ref[p, rows, :] = g[p]
            bo_ref[p, rows, :] = rk[p] * v[p]
        cs = [_cumsum_rows(lw[p]) for p in pairs]
        w_inv = [jnp.exp(-cs[p]) for p in pairs]
        w_end = [jnp.exp(cs[p][CHUNK - 1:CHUNK, :]) for p in pairs]
        rt = [r[p] * jnp.exp(cs[p]) for p in pairs]
        at = [-kk[p] * jnp.exp(cs[p] - lw[p]) for p in pairs]
        bt = [kk[p] * a[p] * w_inv[p] for p in pairs]
        kt = [k2[p] * w_inv[p] for p in pairs]
        atrt = [jnp.concatenate([at[p], rt[p]], axis=0).astype(BF16) for p in pairs]
        pm0 = [_dot_nt(atrt[p] * m0.astype(BF16), jnp.concatenate([bt[p], kt[p]], axis=0)) for p in pairs]
        pm1 = [_dot_nt(atrt[p] * m1.astype(BF16), jnp.concatenate([kt[p], bt[p]], axis=0)) for p in pairs]
        n_ab = [jnp.where(strict2, jnp.where(lo, pm0[p][:CHUNK], pm1[p][:CHUNK]), 0.0) for p in pairs]
        a_rb = [jnp.where(causal2, jnp.where(lo, pm0[p][CHUNK:], pm1[p][CHUNK:]), 0.0) for p in pairs]
        akrk = [jnp.concatenate([jnp.where(strict2, jnp.where(lo, pm1[p][:CHUNK], pm0[p][:CHUNK]), 0.0),
                                 jnp.where(causal2, jnp.where(lo, pm1[p][CHUNK:], pm0[p][CHUNK:]), 0.0)], axis=0)
                for p in pairs]
        vk = [_dot(akrk[p], _blockdiag2(v[p], m1, m0)) for p in pairs]
        inv = _inv_unit_lower_packed(n_ab, m0, m1)
        kv = [_dot((kt[p] * w_end[p]).T, v[p]) * same_head for p in pairs]
        for p in pairs:
            i = c * N_PAIRS + p
            atrt_ref[i] = atrt[p]
            avk_ref[i] = vk[p][:CHUNK]
            yk_ref[i] = vk[p][CHUNK:]
            btT_ref[i] = (bt[p] * w_end[p]).T.astype(BF16)
            kvT_ref[i] = kv[p]
            wlT_ref[i] = jnp.broadcast_to(w_end[p], (LANES, LANES)).T
            inv_ref[i] = inv[p].astype(BF16)
            arb_ref[i] = a_rb[p].astype(BF16)

    for c in range(n_chunks):
        rows = slice(c * CHUNK, (c + 1) * CHUNK)
        it = [c * N_PAIRS + p for p in pairs]
        H = [H_ref[p] for p in pairs]
        xy0 = [_dot(atrt_ref[it[p]], H[p]) for p in pairs]
        x = [xy0[p][:CHUNK] + avk_ref[it[p]] for p in pairs]
        u = [_dot(inv_ref[it[p]], _blockdiag2(x[p], m0, m1)) for p in pairs]
        yb = [_dot(arb_ref[it[p]], _blockdiag2(u[p], m0, m1)) for p in pairs]
        bu = [_dot(btT_ref[it[p]], u[p]) for p in pairs]
        for p in pairs:
            y_ref[p, rows, :] = xy0[p][CHUNK:] + yk_ref[it[p]] + yb[p]
            H_ref[p] = H[p] * wlT_ref[it[p]] + bu[p] * same_head + kvT_ref[it[p]]

    inv_n = 1.0 / RWKV_HEAD
    for c in range(n_chunks):
        rows = slice(c * CHUNK, (c + 1) * CHUNK)
        y = [y_ref[p, rows, :] for p in pairs]
        yc = [y[p] - _half_lane_sums(y[p], m0, lo) * inv_n for p in pairs]
        var = [_half_lane_sums(yc[p] * yc[p], m0, lo) * inv_n for p in pairs]
        for p in pairs:
            yn = yc[p] * lax.rsqrt(var[p] + RWKV_LN_EPS) * lng_ref[:, ps[p]] + lnb_ref[:, ps[p]]
            out_ref[0, rows, ps[p]] = ((yn + bo_ref[p, rows, :]) * gs_ref[p, rows, :]).astype(out_ref.dtype)


def _rwkv_call(proj, mu, w0, w2, a0, a2, g2, k_k, k_a, r_k, ln_g, ln_b, *, col0, tb=256):
    B, T, _ = proj.shape
    j0 = col0 // D_GROUP
    l0 = (col0 + 3 * D_GROUP) // LANES
    row = lambda a: a.reshape(1, -1).astype(F32)
    mul = jnp.zeros((1, RWKV_LOW), F32).at[0, :288].set(mu[3 * D_GROUP:])
    w2p = jnp.zeros((LANES, D_GROUP), F32).at[:64].set(w2)
    a2p = jnp.zeros((LANES, D_GROUP), F32).at[64:].set(a2)
    g2p = jnp.zeros((2 * LANES, D_GROUP), F32).at[:160].set(g2)
    colblk = lambda j: pl.BlockSpec((1, tb, D_GROUP), lambda b, t: (b, t, j0 + j))
    lowblk = lambda j: pl.BlockSpec((1, tb, LANES), lambda b, t: (b, t, l0 + j))
    hrow = lambda t: jnp.maximum(t * (tb // 8) - 1, 0)
    halo = lambda j: pl.BlockSpec((1, 8, D_GROUP), lambda b, t: (b, hrow(t), j0 + j))
    lowhalo = lambda j: pl.BlockSpec((1, 8, LANES), lambda b, t: (b, hrow(t), l0 + j))
    full = lambda a: pl.BlockSpec(a.shape, lambda b, t: (0,) * a.ndim)
    params = [row(mu[:3 * D_GROUP]), mul, row(w0), w2p, row(a0), a2p, g2p, row(k_k), row(k_a), row(r_k),
              row(ln_g), row(ln_b)]
    big = pltpu.VMEM((N_PAIRS, tb, LANES), F32)
    n_items = tb // CHUNK * N_PAIRS
    return pl.pallas_call(
        functools.partial(_rwkv_kernel, tb=tb),
        grid=(B, T // tb),
        in_specs=[colblk(0), colblk(1), colblk(2), lowblk(0), lowblk(1), lowblk(2),
                  halo(0), halo(1), halo(2), lowhalo(0), lowhalo(1), lowhalo(2)] + [full(a) for a in params],
        out_specs=pl.BlockSpec((1, tb, D_GROUP), lambda b, t: (b, t, 0)),
        out_shape=jax.ShapeDtypeStruct((B, T, D_GROUP), BF16),
        scratch_shapes=[big] * 3 + [
            pltpu.VMEM((n_items, 2 * CHUNK, LANES), BF16),
            pltpu.VMEM((n_items, CHUNK, LANES), F32),
            pltpu.VMEM((n_items, CHUNK, LANES), F32),
            pltpu.VMEM((n_items, CHUNK, LANES), BF16),
            pltpu.VMEM((n_items, CHUNK, LANES), BF16),
            pltpu.VMEM((n_items, LANES, CHUNK), BF16),
            pltpu.VMEM((n_items, LANES, LANES), F32),
            pltpu.VMEM((n_items, LANES, LANES), F32),
            pltpu.VMEM((N_PAIRS, LANES, LANES), F32)],
        compiler_params=pltpu.CompilerParams(dimension_semantics=("arbitrary", "arbitrary"),
                                             vmem_limit_bytes=VMEM_LIMIT),
        name="rwkv7",
    )(*([proj] * 12), *params)


def _ada_kernel(c_ref, w_ref, b_ref, out_ref):
    sc = _silu(c_ref[...]).astype(BF16)
    out_ref[0] = jnp.dot(sc, w_ref[0].astype(BF16), preferred_element_type=F32) + b_ref[0]


def _ada_call(c, ada_w, ada_b, *, tn=1536):
    B = c.shape[0]
    n_mod = ada_w.shape[0] * ada_w.shape[1]
    w = ada_w.reshape(n_mod, D_MODEL, 3 * D_MODEL)
    b = ada_b.reshape(n_mod, 1, 3 * D_MODEL)
    return pl.pallas_call(
        _ada_kernel,
        grid=(n_mod, 3 * D_MODEL // tn),
        in_specs=[pl.BlockSpec((B, D_MODEL), lambda i, j: (0, 0)),
                  pl.BlockSpec((1, D_MODEL, tn), lambda i, j: (i, 0, j)),
                  pl.BlockSpec((1, 1, tn), lambda i, j: (i, 0, j))],
        out_specs=pl.BlockSpec((1, B, tn), lambda i, j: (i, 0, j)),
        out_shape=jax.ShapeDtypeStruct((n_mod, B, 3 * D_MODEL), F32),
        compiler_params=pltpu.CompilerParams(dimension_semantics=("arbitrary", "arbitrary"),
                                             vmem_limit_bytes=VMEM_LIMIT),
        name="adaln",
    )(c, w, b)


def _mod_spec(i, part, nb):
    return pl.BlockSpec((1, 1, D_MODEL), lambda b, t: (i * nb + b, 0, part))


def _modulate_kernel(x_ref, shift_ref, scale_ref, h_ref):
    h_ref[0] = (x_ref[0] * (1.0 + scale_ref[0]) + shift_ref[0]).astype(h_ref.dtype)


def _modulate_call(x, mods3, i, *, tb=512):
    B, T, _ = x.shape
    blk = pl.BlockSpec((1, tb, D_MODEL), lambda b, t: (b, t, 0))
    return pl.pallas_call(
        _modulate_kernel,
        grid=(B, T // tb),
        in_specs=[blk, _mod_spec(i, 0, B), _mod_spec(i, 1, B)],
        out_specs=blk,
        out_shape=jax.ShapeDtypeStruct(x.shape, BF16),
        compiler_params=pltpu.CompilerParams(dimension_semantics=("arbitrary", "arbitrary"),
                                             vmem_limit_bytes=VMEM_LIMIT),
        name="modulate",
    )(x, mods3, mods3)


LN_ROWS = 16


def _proj_ln_kernel(*refs, n_lhs, nk, tm, with_next):
    lhs = refs[:n_lhs]
    ws = refs[n_lhs:2 * n_lhs]
    x_ref, gate_ref, g_ref, b_ref = refs[2 * n_lhs:2 * n_lhs + 4]
    if with_next:
        shift_ref, scale_ref, xo_ref, h_ref, acc_ref = refs[2 * n_lhs + 4:]
    else:
        xo_ref, acc_ref = refs[2 * n_lhs + 4:]
    k = pl.program_id(2)

    def partial_product():
        part = jnp.dot(lhs[0][0], ws[0][...], preferred_element_type=F32)
        for j in range(1, n_lhs):
            part = part + jnp.dot(lhs[j][0], ws[j][...], preferred_element_type=F32)
        return part

    @pl.when(k == 0)
    def _():
        acc_ref[...] = partial_product()

    @pl.when(k > 0)
    def _():
        acc_ref[...] += partial_product()

    @pl.when(k == nk - 1)
    def _():
        gate1 = 1.0 + gate_ref[0]
        g, b = g_ref[...], b_ref[...]
        if with_next:
            scale1, shift = 1.0 + scale_ref[0], shift_ref[0]

        def rows_body(r, carry):
            rows = pl.ds(pl.multiple_of(r * LN_ROWS, LN_ROWS), LN_ROWS)
            z = ALPHA * x_ref[0, rows, :] + gate1 * acc_ref[rows, :]
            zc = z - jnp.mean(z, axis=-1, keepdims=True)
            var = jnp.mean(zc * zc, axis=-1, keepdims=True)
            xn = zc * lax.rsqrt(var + LN_EPS) * g + b
            xo_ref[0, rows, :] = xn
            if with_next:
                h_ref[0, rows, :] = (xn * scale1 + shift).astype(h_ref.dtype)
            return carry

        lax.fori_loop(0, tm // LN_ROWS, rows_body, 0, unroll=8)


def _proj_ln_call(lhs, ws, x, mods3, i, g, b, *, with_next, tm, tk):
    B, T, D = x.shape
    n_lhs = len(lhs)
    nk = lhs[0].shape[2] // tk
    blk = pl.BlockSpec((1, tm, D), lambda b, t, k: (b, t, 0))
    row = pl.BlockSpec((1, D), lambda b, t, k: (0, 0))
    mod = lambda ii, part: pl.BlockSpec((1, 1, D), lambda b, t, k: (ii * B + b, 0, part))
    wspec = lambda idx, k0: pl.BlockSpec((None, tk, D), lambda b, t, k: (idx, k0 + k, 0))
    in_specs = ([pl.BlockSpec((1, tm, tk), lambda b, t, k: (b, t, k))] * n_lhs
                + [wspec(idx, k0) for _, idx, k0 in ws]
                + [blk, mod(i, 2), row, row])
    args = list(lhs) + [w for w, _, _ in ws] + [x, mods3, g.reshape(1, D), b.reshape(1, D)]
    out_specs = [blk]
    out_shape = [jax.ShapeDtypeStruct(x.shape, F32)]
    if with_next:
        in_specs += [mod(i + 1, 0), mod(i + 1, 1)]
        args += [mods3, mods3]
        out_specs.append(blk)
        out_shape.append(jax.ShapeDtypeStruct(x.shape, BF16))
    return pl.pallas_call(
        functools.partial(_proj_ln_kernel, n_lhs=n_lhs, nk=nk, tm=tm, with_next=with_next),
        grid=(B, T // tm, nk),
        in_specs=in_specs, out_specs=out_specs, out_shape=out_shape,
        scratch_shapes=[pltpu.VMEM((tm, D), F32)],
        compiler_params=pltpu.CompilerParams(dimension_semantics=("arbitrary", "arbitrary", "arbitrary"),
                                             vmem_limit_bytes=VMEM_LIMIT),
        name="proj_residual_ln",
    )(*args)


def _matmul_kernel(a_ref, b_ref, o_ref, *scratch, nk):
    if nk == 1:
        o_ref[...] = jnp.dot(a_ref[...], b_ref[...], preferred_element_type=F32).astype(o_ref.dtype)
        return
    (acc_ref,) = scratch
    k = pl.program_id(2)

    @pl.when(k == 0)
    def _():
        acc_ref[...] = jnp.zeros_like(acc_ref)

    acc_ref[...] += jnp.dot(a_ref[...], b_ref[...], preferred_element_type=F32)

    @pl.when(k == nk - 1)
    def _():
        o_ref[...] = acc_ref[...].astype(o_ref.dtype)


def _matmul_call(a, b, *, tm, tn, tk, out_dtype=F32):
    M, K = a.shape
    _, N = b.shape
    nk = K // tk
    return pl.pallas_call(
        functools.partial(_matmul_kernel, nk=nk),
        grid=(N // tn, M // tm, nk),
        in_specs=[pl.BlockSpec((tm, tk), lambda j, i, k: (i, k)),
                  pl.BlockSpec((tk, tn), lambda j, i, k: (k, j))],
        out_specs=pl.BlockSpec((tm, tn), lambda j, i, k: (i, j)),
        out_shape=jax.ShapeDtypeStruct((M, N), out_dtype),
        scratch_shapes=[] if nk == 1 else [pltpu.VMEM((tm, tn), F32)],
        compiler_params=pltpu.CompilerParams(dimension_semantics=("arbitrary", "arbitrary", "arbitrary"),
                                             vmem_limit_bytes=VMEM_LIMIT),
        name="matmul",
    )(a, b)


XPOSE_ROWS = 256


def _matmul_f32wt_kernel(a_ref, wt_ref, o_ref, wb_ref):
    @pl.when(pl.program_id(1) == 0)
    def _():
        for r in range(0, wt_ref.shape[0], XPOSE_ROWS):
            wb_ref[:, r:r + XPOSE_ROWS] = wt_ref[r:r + XPOSE_ROWS, :].T.astype(BF16)

    o_ref[...] = jnp.dot(a_ref[...], wb_ref[...], preferred_element_type=F32)


def _matmul_f32wt_call(a, wt, idx, ncols, *, tm=1024, tn=1024):
    M, K = a.shape
    tm = min(tm, M)
    return pl.pallas_call(
        _matmul_f32wt_kernel,
        grid=(ncols // tn, M // tm),
        in_specs=[pl.BlockSpec((tm, K), lambda j, i: (i, 0)),
                  pl.BlockSpec((None, tn, K), lambda j, i: (idx, j, 0))],
        out_specs=pl.BlockSpec((tm, tn), lambda j, i: (i, j)),
        out_shape=jax.ShapeDtypeStruct((M, ncols), F32),
        scratch_shapes=[pltpu.VMEM((K, tn), BF16)],
        compiler_params=pltpu.CompilerParams(dimension_semantics=("arbitrary", "arbitrary"),
                                             vmem_limit_bytes=VMEM_LIMIT),
        name="matmul_f32wt",
    )(a, wt)


def _regroup_rows_kernel(wt_ref, o_ref, *, segments):
    tc = wt_ref.shape[1]
    pieces, pos = [], 0
    for src, width, dst in segments:
        if dst > pos:
            pieces.append(jnp.zeros((dst - pos, tc), F32))
        pieces.append(wt_ref[src:src + width, :])
        pos = dst + width
    if pos < o_ref.shape[1]:
        pieces.append(jnp.zeros((o_ref.shape[1] - pos, tc), F32))
    stacked = jnp.concatenate(pieces, axis=0)
    for r in range(0, o_ref.shape[1], LANES):
        o_ref[:, r:r + LANES] = stacked[r:r + LANES, :].T.astype(o_ref.dtype)


def _regroup_rows_call(wt, idx, segments, out_cols, *, block_rows, block_index, tc):
    _, _, K = wt.shape
    return pl.pallas_call(
        functools.partial(_regroup_rows_kernel, segments=segments),
        grid=(K // tc,),
        in_specs=[pl.BlockSpec((None, block_rows, tc), lambda c: (idx, block_index, c))],
        out_specs=pl.BlockSpec((tc, out_cols), lambda c: (c, 0)),
        out_shape=jax.ShapeDtypeStruct((K, out_cols), BF16),
        compiler_params=pltpu.CompilerParams(dimension_semantics=("arbitrary",), vmem_limit_bytes=VMEM_LIMIT),
        name="regroup_rows",
    )(wt)


def _swiglu_up_kernel(h_ref, wg_ref, wu_ref, o_ref, wgb_ref, wub_ref):
    @pl.when(pl.program_id(1) == 0)
    def _():
        wgb_ref[...] = wg_ref[...].astype(BF16)
        wub_ref[...] = wu_ref[...].astype(BF16)

    h = h_ref[...]
    g = jnp.dot(h, wgb_ref[...], preferred_element_type=F32)
    u = jnp.dot(h, wub_ref[...], preferred_element_type=F32)
    o_ref[...] = (_silu(g) * u).astype(o_ref.dtype)


def _swiglu_up_call(h, wg, wu, layer, *, tm=1024, tn=512):
    M, K = h.shape
    _, _, N = wg.shape
    tm = min(tm, M)
    wspec = pl.BlockSpec((None, K, tn), lambda j, i: (layer, 0, j))
    return pl.pallas_call(
        _swiglu_up_kernel,
        grid=(N // tn, M // tm),
        in_specs=[pl.BlockSpec((tm, K), lambda j, i: (i, 0)), wspec, wspec],
        out_specs=pl.BlockSpec((tm, tn), lambda j, i: (i, j)),
        out_shape=jax.ShapeDtypeStruct((M, N), BF16),
        scratch_shapes=[pltpu.VMEM((K, tn), BF16), pltpu.VMEM((K, tn), BF16)],
        compiler_params=pltpu.CompilerParams(dimension_semantics=("arbitrary", "arbitrary"),
                                             vmem_limit_bytes=VMEM_LIMIT),
        name="swiglu_up",
    )(h, wg, wu)


A_MAIN = 4 * D_GROUP
B_COLS_PAD = 3 * D_GROUP + RWKV_LOW + LANES
MLSTM_GATE_COL = 3 * D_GROUP + RWKV_LOW
CD_MAIN = 8 * D_GROUP
GDN_COL0 = 4 * D_GROUP


N_GATES = 2 * N_HEADS
B_SRC = A_MAIN + N_GATES
B_LOW = 64 + 64 + 160
B_SEGMENTS = ((B_SRC, 3 * D_GROUP, 0), (B_SRC + 3 * D_GROUP, B_LOW, 3 * D_GROUP), (A_MAIN, N_GATES, MLSTM_GATE_COL))
CD_GATE_SEGMENTS = ((0, N_GATES, 0),)


def kernel(x, c, positions, ada_w, ada_b, ln_g, ln_b, ab_w_in, ab_w_out, mlstm_conv_w, mlstm_gate_b, mlstm_norm_g, rwkv_mu, rwkv_w0, rwkv_w2, rwkv_a0, rwkv_a2, rwkv_g2, rwkv_k_k, rwkv_k_a, rwkv_r_k, rwkv_ln_g, rwkv_ln_b, cd_w_in, cd_w_out, ret_norm_g, gdn_conv_w, gdn_a_log, gdn_dt_bias, gdn_norm_g, ffn_w_gate, ffn_w_up, ffn_w_down):
    B, T, D = x.shape
    M = B * T
    depth = ada_w.shape[0]
    mods = _ada_call(c, ada_w, ada_b)
    mods3 = mods.reshape(2 * depth * B, 1, 3 * D)
    h = _modulate_call(x, mods3, 0)
    w_down = ffn_w_down.astype(BF16)
    for layer in range(depth):
        j = layer // 2
        i_mix, i_ffn = 2 * layer, 2 * layer + 1
        if layer % 2 == 0:
            h2 = h.reshape(M, D)
            wt = jnp.swapaxes(ab_w_in, 1, 2)
            proj_a = _matmul_f32wt_call(h2, wt, j, A_MAIN).reshape(B, T, A_MAIN)
            w_b = _regroup_rows_call(wt, j, B_SEGMENTS, B_COLS_PAD, block_rows=wt.shape[1], block_index=0, tc=256)
            proj_b = _matmul_call(h2, w_b, tm=min(1024, M), tn=B_COLS_PAD // 2, tk=D)
            proj_b = proj_b.reshape(B, T, B_COLS_PAD)
            ya = _mlstm_call(proj_a, proj_b, mlstm_conv_w[j], mlstm_gate_b[j], mlstm_norm_g[j],
                             gate_col=MLSTM_GATE_COL)
            yb = _rwkv_call(proj_b, rwkv_mu[j], rwkv_w0[j], rwkv_w2[j], rwkv_a0[j], rwkv_a2[j], rwkv_g2[j],
                            rwkv_k_k[j], rwkv_k_a[j], rwkv_r_k[j], rwkv_ln_g[j], rwkv_ln_b[j], col0=0)
            w_out = ab_w_out
        else:
            h2 = h.reshape(M, D)
            wt = jnp.swapaxes(cd_w_in, 1, 2)
            proj = _matmul_f32wt_call(h2, wt, j, CD_MAIN).reshape(B, T, CD_MAIN)
            w_gates = _regroup_rows_call(wt, j, CD_GATE_SEGMENTS, LANES, block_rows=N_GATES,
                                         block_index=CD_MAIN // N_GATES, tc=D)
            proj_g = _matmul_call(h2, w_gates, tm=512, tn=LANES, tk=D).reshape(B, T, LANES)
            ya = _ret_call(proj, positions, ret_norm_g[j])
            yb = _gdn_call(proj, proj_g, gdn_conv_w[j], gdn_a_log[j], gdn_dt_bias[j], gdn_norm_g[j],
                           col0=GDN_COL0, gate_col=0)
            w_out = cd_w_out
        w_out = w_out.astype(BF16)
        x, h = _proj_ln_call([ya, yb], [(w_out, j, 0), (w_out, j, 1)], x, mods3, i_mix,
                             ln_g[layer, 0], ln_b[layer, 0], with_next=True, tm=512, tk=D_GROUP)
        act = _swiglu_up_call(h.reshape(M, D), ffn_w_gate, ffn_w_up, layer)
        last = layer == depth - 1
        res = _proj_ln_call([act.reshape(B, T, D_FF)], [(w_down, layer, 0)], x, mods3, i_ffn,
                            ln_g[layer, 1], ln_b[layer, 1], with_next=not last, tm=512, tk=1408)
        if last:
            (x,) = res
        else:
            x, h = res
    return x
```

```python
import functools
import math

import numpy as np
import jax
import jax.numpy as jnp
from jax import lax
from jax.experimental import pallas as pl
from jax.experimental.pallas import tpu as pltpu

F32 = jnp.float32
BF16 = jnp.bfloat16

D_MODEL = 2048
D_GROUP = 1024
HEAD_DIM = 128
N_HEADS = 8
RWKV_HEAD = 64
N_RWKV = 16
CHUNK = 64
D_FF = 5632
DEPTH = 2
ALPHA = (2 * DEPTH) ** 0.25
LN_EPS = 1e-5
RWKV_LN_EPS = 64e-5
ROPE_BASE = 10000.0
RET_GAMMA_BASE = 5.0
LANES = 128
VMEM_LIMIT = 48 * 1024 * 1024
VMEM_LIMIT_DOWN = 58 * 1024 * 1024


def _dot(a, b):
    return jnp.dot(a.astype(BF16), b.astype(BF16), preferred_element_type=F32)


def _dot_nt(a, b):
    return lax.dot_general(a.astype(BF16), b.astype(BF16), (((1,), (1,)), ((), ())),
                           preferred_element_type=F32)


def _split3(x):
    hi = x.astype(BF16)
    r1 = x - hi.astype(F32)
    mid = r1.astype(BF16)
    lo = (r1 - mid.astype(F32)).astype(BF16)
    return hi, mid, lo


def _cumsum_rows(x):
    n = x.shape[1]
    out = jnp.dot(_tri(CHUNK).astype(BF16), jnp.concatenate(_split3(x), axis=1), preferred_element_type=F32)
    return out[:, :n] + out[:, n:2 * n] + out[:, 2 * n:]


def _half_lane_sums(x, m0, lo):
    s0 = jnp.sum(x * m0, axis=-1, keepdims=True)
    s1 = jnp.sum(x * (1.0 - m0), axis=-1, keepdims=True)
    return jnp.where(lo, s0, s1)


def _sigmoid(x):
    return 1.0 / (1.0 + jnp.exp(-x))


def _silu(x):
    return x * _sigmoid(x)


def _log_sigmoid(x):
    return jnp.minimum(x, 0.0) - jnp.log1p(jnp.exp(-jnp.abs(x)))


def _softplus(x):
    return jnp.maximum(x, 0.0) + jnp.log1p(jnp.exp(-jnp.abs(x)))


def _tri(n, strict=False):
    r = lax.broadcasted_iota(jnp.int32, (n, n), 0)
    c = lax.broadcasted_iota(jnp.int32, (n, n), 1)
    return (r > c) if strict else (r >= c)


def _conv_silu_rows(src_ref, halo_ref, w, first, r0, nrows, c0, ncols):
    cur = src_ref[0, r0:r0 + nrows, c0:c0 + ncols]
    acc = w[3:4] * cur
    if r0 == 0:
        hl = jnp.where(first, 0.0, halo_ref[0, :, c0:c0 + ncols])
        ext = jnp.concatenate([hl, cur[0:8]], axis=0)
        for j in range(3):
            head = ext[5 + j:13 + j]
            if nrows > 8:
                rest = src_ref[0, 5 + j:nrows - 3 + j, c0:c0 + ncols]
                sh = jnp.concatenate([head, rest], axis=0)
            else:
                sh = head
            acc = acc + w[j:j + 1] * sh
    else:
        for j in range(3):
            acc = acc + w[j:j + 1] * src_ref[0, r0 - 3 + j:r0 - 3 + j + nrows, c0:c0 + ncols]
    return _silu(acc)


def _head_norm_rows(h, g_row, eps, center=True):
    if center:
        h = h - jnp.mean(h, axis=-1, keepdims=True)
    return h * lax.rsqrt(jnp.mean(h * h, axis=-1, keepdims=True) + eps) * g_row


MLSTM_CHUNK_GROUP = 4


def _cummax_rows(x):
    row = lax.broadcasted_iota(jnp.int32, x.shape, 0)
    d = 1
    while d < x.shape[0]:
        x = jnp.where(row >= d, jnp.maximum(x, pltpu.roll(x, d, 0)), x)
        d *= 2
    return x


def _bcast_head_cols(x, sel, pieces):
    m = x.shape[0]
    parts, rest = [], x
    for _ in range(pieces):
        hi = rest.astype(BF16)
        parts.append(hi)
        rest = rest - hi.astype(F32)
    out = jnp.dot(jnp.concatenate(parts, axis=0), sel, preferred_element_type=F32)
    acc = out[:m]
    for i in range(1, pieces):
        acc = acc + out[i * m:(i + 1) * m]
    return acc


def _mlstm_kernel(q_ref, k_ref, v_ref, o_ref, g_ref, qh_ref, kh_ref, cw_ref, gb_ref, ng_ref,
                  out_ref, qc_ref, kc_ref, sv_ref, rs_ref, hh_ref, b0_ref, cm_ref, kvn_ref, CN_ref, m_ref, *, tb):
    t = pl.program_id(1)
    first = t == 0

    @pl.when(first)
    def _():
        CN_ref[...] = jnp.zeros_like(CN_ref)
        m_ref[...] = jnp.zeros_like(m_ref)

    for c in range(tb // CHUNK):
        for cb in range(D_GROUP // 256):
            cs = cb * 256
            qc_ref[c * CHUNK:(c + 1) * CHUNK, cs:cs + 256] = _conv_silu_rows(
                q_ref, qh_ref, cw_ref[:, cs:cs + 256], first, c * CHUNK, CHUNK, cs, 256)
            kc_ref[c * CHUNK:(c + 1) * CHUNK, cs:cs + 256] = _conv_silu_rows(
                k_ref, kh_ref, cw_ref[:, D_GROUP + cs:D_GROUP + cs + 256], first, c * CHUNK, CHUNK, cs, 256)

    causal = _tri(CHUNK)
    gb = gb_ref[...]
    scale = HEAD_DIM ** -0.5
    n_chunks = tb // CHUNK
    heads = range(N_HEADS)
    hs = [slice(h * HEAD_DIM, (h + 1) * HEAD_DIM) for h in heads]
    head_lane = lax.broadcasted_iota(jnp.int32, (1, LANES), 1) < N_HEADS
    sel = (lax.broadcasted_iota(jnp.int32, (LANES, N_HEADS * LANES), 0)
           == (lax.broadcasted_iota(jnp.int32, (LANES, N_HEADS * LANES), 1) >> 7)).astype(BF16)
    ones = jnp.ones((CHUNK, HEAD_DIM), F32)
    last = slice(CHUNK - 1, CHUNK)

    for c0 in range(0, n_chunks, MLSTM_CHUNK_GROUP):
        chunks = range(c0, c0 + MLSTM_CHUNK_GROUP)
        rows = {c: slice(c * CHUNK, (c + 1) * CHUNK) for c in chunks}
        z = {c: g_ref[0, rows[c], :] + gb for c in chunks}
        b0 = {c: pltpu.roll(_cumsum_rows(_log_sigmoid(z[c])), LANES - N_HEADS, 1) for c in chunks}
        cv = {c: jnp.where(head_lane, z[c] - b0[c], 0.0) for c in chunks}
        cm = {c: _cummax_rows(cv[c]) for c in chunks}
        cT = {c: cv[c].T for c in chunks}
        cmb = {c: _bcast_head_cols(cm[c], sel, 3) for c in chunks}
        e1b = {c: _bcast_head_cols(jnp.where(head_lane, jnp.exp(cv[c] - cm[c][last]), 0.0), sel, 2) for c in chunks}
        for c in chunks:
            b0_ref[rows[c], :] = b0[c]
            cm_ref[rows[c], :] = cm[c]
        items = [(c, h) for c in chunks for h in heads]
        n_items = range(len(items))
        q = [qc_ref[rows[c], hs[h]] for c, h in items]
        k = [kc_ref[rows[c], hs[h]] * scale for c, h in items]
        vo = [jnp.concatenate([v_ref[0, rows[c], hs[h]], ones], axis=1).astype(BF16) for c, h in items]
        qk = [_dot_nt(q[i], k[i]) for i in n_items]
        s = [qk[i] * jnp.where(causal, jnp.exp(jnp.minimum(cT[c][h:h + 1, :] - cmb[c][:, h * LANES:h * LANES + CHUNK],
                                                           0.0)), 0.0) for i, (c, h) in enumerate(items)]
        s_hi = [s[i].astype(BF16) for i in n_items]
        s_lo = [(s[i] - s_hi[i].astype(F32)).astype(BF16) for i in n_items]
        svr = [jnp.dot(jnp.concatenate([s_hi[i], s_lo[i]], axis=0), vo[i], preferred_element_type=F32)
               for i in n_items]
        kvn = [_dot((k[i] * e1b[c][:, hs[h]]).T, vo[i]) for i, (c, h) in enumerate(items)]
        for i, (c, h) in enumerate(items):
            sv_ref[rows[c], hs[h]] = svr[i][:CHUNK, :HEAD_DIM] + svr[i][CHUNK:, :HEAD_DIM]
            rs_ref[rows[c], hs[h]] = svr[i][:CHUNK, HEAD_DIM:] + svr[i][CHUNK:, HEAD_DIM:]
            kvn_ref[c * N_HEADS + h] = kvn[i]

    for c in range(n_chunks):
        rows = slice(c * CHUNK, (c + 1) * CHUNK)
        m = m_ref[0:1, :]
        b0 = b0_ref[rows, :]
        cm = cm_ref[rows, :]
        mx = jnp.maximum(cm, m)
        m_new = jnp.maximum(b0[last] + m, b0[last] + cm[last])
        m_ref[0:1, :] = m_new
        zero = lambda x: jnp.where(head_lane, x, 0.0)
        fib = _bcast_head_cols(zero(jnp.exp(cm - mx)), sel, 2)
        scb = _bcast_head_cols(zero(jnp.exp(m - mx)), sel, 2)
        emtb = _bcast_head_cols(zero(jnp.exp(jnp.minimum(-(b0 + mx), 80.0))), sel, 2)
        dfb = _bcast_head_cols(jnp.concatenate([zero(jnp.exp(b0[last] + m - m_new)),
                                                zero(jnp.exp(b0[last] + cm[last] - m_new)),
                                                jnp.zeros((6, LANES), F32)], axis=0), sel, 2)
        CN = [CN_ref[h] for h in heads]
        qcn = [_dot(qc_ref[rows, hs[h]], CN[h]) for h in heads]
        for h in heads:
            num = fib[:, hs[h]] * sv_ref[rows, hs[h]] + scb[:, hs[h]] * qcn[h][:, :HEAD_DIM]
            den = fib[:, hs[h]] * rs_ref[rows, hs[h]] + scb[:, hs[h]] * qcn[h][:, HEAD_DIM:]
            hh_ref[rows, hs[h]] = num / jnp.maximum(jnp.abs(den), emtb[:, hs[h]])
            dec = jnp.concatenate([dfb[0:1, hs[h]]] * 2, axis=1)
            fkv = jnp.concatenate([dfb[1:2, hs[h]]] * 2, axis=1)
            CN_ref[h] = CN[h] * dec + kvn_ref[c * N_HEADS + h] * fkv

    for c in range(n_chunks):
        rows = slice(c * CHUNK, (c + 1) * CHUNK)
        hn = [_head_norm_rows(hh_ref[rows, hs[h]], ng_ref[:, hs[h]], LN_EPS) for h in heads]
        for h in heads:
            out_ref[0, rows, hs[h]] = (hn[h] * _sigmoid(o_ref[0, rows, hs[h]])).astype(out_ref.dtype)


def _mlstm_call(proj, gproj, conv_w, gate_b, norm_g, *, gate_col, tb=256):
    B, T, _ = proj.shape
    nt = T // tb
    gb = jnp.zeros((1, LANES), F32).at[0, :2 * N_HEADS].set(gate_b)
    ng = norm_g.reshape(1, D_GROUP)
    colblk = lambda j: pl.BlockSpec((1, tb, D_GROUP), lambda b, t: (b, t, j))
    halo = lambda j: pl.BlockSpec((1, 8, D_GROUP), lambda b, t: (b, jnp.maximum(t * (tb // 8) - 1, 0), j))
    return pl.pallas_call(
        functools.partial(_mlstm_kernel, tb=tb),
        grid=(B, nt),
        in_specs=[colblk(0), colblk(1), colblk(2), colblk(3),
                  pl.BlockSpec((1, tb, LANES), lambda b, t: (b, t, gate_col // LANES)),
                  halo(0), halo(1),
                  pl.BlockSpec((4, 2 * D_GROUP), lambda b, t: (0, 0)),
                  pl.BlockSpec((1, LANES), lambda b, t: (0, 0)),
                  pl.BlockSpec((1, D_GROUP), lambda b, t: (0, 0))],
        out_specs=pl.BlockSpec((1, tb, D_GROUP), lambda b, t: (b, t, 0)),
        out_shape=jax.ShapeDtypeStruct((B, T, D_GROUP), BF16),
        scratch_shapes=[pltpu.VMEM((tb, D_GROUP), F32)] * 5
                       + [pltpu.VMEM((tb, LANES), F32)] * 2
                       + [pltpu.VMEM((tb // CHUNK * N_HEADS, HEAD_DIM, 2 * HEAD_DIM), F32),
                          pltpu.VMEM((N_HEADS, HEAD_DIM, 2 * HEAD_DIM), F32),
                          pltpu.VMEM((8, LANES), F32)],
        compiler_params=pltpu.CompilerParams(dimension_semantics=("arbitrary", "arbitrary"),
                                             vmem_limit_bytes=VMEM_LIMIT),
        name="mlstm",
    )(proj, proj, proj, proj, gproj, proj, proj, conv_w, gb, ng)


RET_CHUNK_GROUP = 1


def _ret_kernel(q_ref, k_ref, v_ref, g_ref, pos_ref, inv_ref, ng_ref, out_ref, qr_ref, o_ref, kv_ref, R_ref, *, tb):
    t = pl.program_id(1)

    @pl.when(t == 0)
    def _():
        R_ref[...] = jnp.zeros_like(R_ref)

    n_chunks = tb // CHUNK
    heads = range(N_HEADS)
    hs = [slice(h * HEAD_DIM, (h + 1) * HEAD_DIM) for h in heads]
    lg = [math.log1p(-2.0 ** (-RET_GAMMA_BASE - h)) for h in heads]
    causal = _tri(CHUNK)
    ri = lax.broadcasted_iota(jnp.int32, (CHUNK, CHUNK), 0)
    ci = lax.broadcasted_iota(jnp.int32, (CHUNK, CHUNK), 1)
    rel = (ri - ci).astype(F32)
    tcol = lax.broadcasted_iota(jnp.int32, (CHUNK, 1), 0).astype(F32)
    scale = HEAD_DIM ** -0.5
    lane = lax.broadcasted_iota(jnp.int32, (1, HEAD_DIM), 1)
    sign = jnp.where(lane < HEAD_DIM // 2, -1.0, 1.0)
    dmat = [jnp.where(causal, jnp.exp(rel * lg[h]), 0.0) for h in heads]

    for c0 in range(0, n_chunks, RET_CHUNK_GROUP):
        chunks = range(c0, c0 + RET_CHUNK_GROUP)
        rows = {c: slice(c * CHUNK, (c + 1) * CHUNK) for c in chunks}
        ang = {c: pos_ref[0, rows[c], :].astype(F32) * inv_ref[...] for c in chunks}
        cos2 = {c: jnp.cos(ang[c]) for c in chunks}
        sin2 = {c: jnp.sin(ang[c]) * sign for c in chunks}
        items = [(c, h) for c in chunks for h in heads]
        n_items = range(len(items))
        rot = lambda z, c: z * cos2[c] + pltpu.roll(z, HEAD_DIM // 2, 1) * sin2[c]
        qr = [rot(q_ref[0, rows[c], hs[h]], c) for c, h in items]
        kr = [rot(k_ref[0, rows[c], hs[h]], c) * scale for c, h in items]
        v = [v_ref[0, rows[c], hs[h]] for c, h in items]
        qk = [_dot_nt(qr[i], kr[i]) * dmat[h] for i, (c, h) in enumerate(items)]
        intra = [_dot(qk[i], v[i]) for i in n_items]
        kv = [_dot((kr[i] * jnp.exp((CHUNK - 1.0 - tcol) * lg[h])).T, v[i]) for i, (c, h) in enumerate(items)]
        for i, (c, h) in enumerate(items):
            qr_ref[rows[c], hs[h]] = qr[i]
            o_ref[rows[c], hs[h]] = intra[i]
            kv_ref[c * N_HEADS + h] = kv[i]

    for h in heads:
        R = R_ref[h]
        for c in range(n_chunks):
            inc = kv_ref[c * N_HEADS + h]
            kv_ref[c * N_HEADS + h] = R
            R = R * math.exp(CHUNK * lg[h]) + inc
        R_ref[h] = R
    for c in range(n_chunks):
        rows = slice(c * CHUNK, (c + 1) * CHUNK)
        inter = [_dot(qr_ref[rows, hs[h]], kv_ref[c * N_HEADS + h]) * jnp.exp((tcol + 1.0) * lg[h]) for h in heads]
        on = [_head_norm_rows(o_ref[rows, hs[h]] + inter[h], ng_ref[:, hs[h]], LN_EPS) for h in heads]
        for h in heads:
            out_ref[0, rows, hs[h]] = (on[h] * _silu(g_ref[0, rows, hs[h]])).astype(out_ref.dtype)


def _ret_call(proj, positions, norm_g, *, tb=256):
    B, T, _ = proj.shape
    half = HEAD_DIM // 2
    inv_freq = ROPE_BASE ** (-jnp.arange(half, dtype=F32) / half)
    inv2 = jnp.concatenate([inv_freq, inv_freq]).reshape(1, HEAD_DIM)
    colblk = lambda j: pl.BlockSpec((1, tb, D_GROUP), lambda b, t: (b, t, j))
    big = pltpu.VMEM((tb, D_GROUP), F32)
    return pl.pallas_call(
        functools.partial(_ret_kernel, tb=tb),
        grid=(B, T // tb),
        in_specs=[colblk(0), colblk(1), colblk(2), colblk(3),
                  pl.BlockSpec((1, tb, 1), lambda b, t: (b, t, 0)),
                  pl.BlockSpec((1, HEAD_DIM), lambda b, t: (0, 0)),
                  pl.BlockSpec((1, D_GROUP), lambda b, t: (0, 0))],
        out_specs=pl.BlockSpec((1, tb, D_GROUP), lambda b, t: (b, t, 0)),
        out_shape=jax.ShapeDtypeStruct((B, T, D_GROUP), BF16),
        scratch_shapes=[big, big, pltpu.VMEM((tb // CHUNK * N_HEADS, HEAD_DIM, HEAD_DIM), F32),
                        pltpu.VMEM((N_HEADS, HEAD_DIM, HEAD_DIM), F32)],
        compiler_params=pltpu.CompilerParams(dimension_semantics=("arbitrary", "arbitrary"),
                                             vmem_limit_bytes=VMEM_LIMIT),
        name="retention",
    )(proj, proj, proj, proj, positions.reshape(B, T, 1), inv2, norm_g.reshape(1, D_GROUP))


def _inv_unit_lower(nms):
    n = nms[0].shape[0]
    eye = (lax.broadcasted_iota(jnp.int32, (n, n), 0) == lax.broadcasted_iota(jnp.int32, (n, n), 1)).astype(F32)
    ps = [eye + nm for nm in nms]
    xs = [_dot(nm, nm) for nm in nms]
    for _ in range(int(math.log2(n)) - 2):
        px = [_dot(jnp.concatenate([p, x], axis=0), x) for p, x in zip(ps, xs)]
        ps = [p + y[:n] for p, y in zip(ps, px)]
        xs = [y[n:] for y in px]
    ps = [p + _dot(p, x) for p, x in zip(ps, xs)]
    resid = [eye - p + _dot(nm, p) for p, nm in zip(ps, nms)]
    return [p + _dot(p, r) for p, r in zip(ps, resid)]


def _blockdiag2(x, m0, m1):
    xb = x.astype(BF16)
    return jnp.concatenate([xb * m0.astype(BF16), xb * m1.astype(BF16)], axis=0)


def _inv_unit_lower_packed(nms, m0, m1):
    n = nms[0].shape[0]
    r = lax.broadcasted_iota(jnp.int32, (n, 2 * n), 0)
    c = lax.broadcasted_iota(jnp.int32, (n, 2 * n), 1)
    eye2 = (r == (c & (n - 1))).astype(F32)
    bd = lambda x: _blockdiag2(x, m0, m1)
    ps = [eye2 + nm for nm in nms]
    xs = [_dot(nm, bd(nm)) for nm in nms]
    for _ in range(int(math.log2(n)) - 2):
        px = [_dot(jnp.concatenate([p, x], axis=0), bd(x)) for p, x in zip(ps, xs)]
        ps = [p + y[:n] for p, y in zip(ps, px)]
        xs = [y[n:] for y in px]
    ps = [p + _dot(p, bd(x)) for p, x in zip(ps, xs)]
    resid = [eye2 - p + _dot(nm, bd(p)) for p, nm in zip(ps, nms)]
    return [p + _dot(p, bd(r_)) for p, r_ in zip(ps, resid)]


def _solve_unit_lower(nms, rhss):
    n = nms[0].shape[0]
    eye = (lax.broadcasted_iota(jnp.int32, (n, n), 0) == lax.broadcasted_iota(jnp.int32, (n, n), 1)).astype(F32)
    ps = [eye + nm for nm in nms]
    xs = [_dot(nm, nm) for nm in nms]
    for _ in range(int(math.log2(n)) - 2):
        px = [_dot(jnp.concatenate([p, x], axis=0), x) for p, x in zip(ps, xs)]
        ps = [p + y[:n] for p, y in zip(ps, px)]
        xs = [y[n:] for y in px]
    ps = [p + _dot(p, x) for p, x in zip(ps, xs)]
    x0 = [_dot(p, r) for p, r in zip(ps, rhss)]
    resid = [r - a + _dot(nm, a) for r, a, nm in zip(rhss, x0, nms)]
    return [a + _dot(p, r) for a, p, r in zip(x0, ps, resid)]


def _l2norm_rows(z):
    return z * lax.rsqrt(jnp.sum(z * z, axis=-1, keepdims=True) + 1e-6)


GDN_CHUNK_GROUP = 2


def _gdn_kernel(q_ref, k_ref, v_ref, z_ref, g_ref, qh_ref, kh_ref, vh_ref, cw_ref, an_ref, dt_ref, ng_ref,
                out_ref, qc_ref, kc_ref, vc_ref, u_ref, w_ref, qe_ref, o_ref, att_ref, kdT_ref, gl_ref, S_ref,
                *, tb):
    t = pl.program_id(1)
    first = t == 0

    @pl.when(first)
    def _():
        S_ref[...] = jnp.zeros_like(S_ref)

    srcs = ((q_ref, qh_ref, qc_ref), (k_ref, kh_ref, kc_ref), (v_ref, vh_ref, vc_ref))
    for c in range(tb // CHUNK):
        for cb in range(D_GROUP // 256):
            cs = cb * 256
            for i, (src, halo, dst) in enumerate(srcs):
                w = cw_ref[:, i * D_GROUP + cs:i * D_GROUP + cs + 256]
                dst[c * CHUNK:(c + 1) * CHUNK, cs:cs + 256] = _conv_silu_rows(
                    src, halo, w, first, c * CHUNK, CHUNK, cs, 256)

    causal = _tri(CHUNK)
    strict = _tri(CHUNK, strict=True)
    a_neg = an_ref[...]
    dtb = dt_ref[...]
    scale = HEAD_DIM ** -0.5
    n_chunks = tb // CHUNK
    heads = range(N_HEADS)
    hs = [slice(h * HEAD_DIM, (h + 1) * HEAD_DIM) for h in heads]

    for c0 in range(0, n_chunks, GDN_CHUNK_GROUP):
        chunks = range(c0, c0 + GDN_CHUNK_GROUP)
        rows = {c: slice(c * CHUNK, (c + 1) * CHUNK) for c in chunks}
        gz = {c: g_ref[0, rows[c], :] for c in chunks}
        beta = {c: _sigmoid(gz[c]) for c in chunks}
        gc = {c: _cumsum_rows(a_neg * _softplus(gz[c] + dtb)) for c in chunks}
        gcT = {c: gc[c].T for c in chunks}
        for c in chunks:
            gl_ref[c:c + 1, :] = gc[c][CHUNK - 1:CHUNK, :]
        items = [(c, h) for c in chunks for h in heads]
        n_items = range(len(items))
        gc_col = [gc[c][:, h:h + 1] for c, h in items]
        b_col = [beta[c][:, 8 + h:9 + h] for c, h in items]
        gamma = [jnp.where(causal, jnp.exp(gc_col[i] - gcT[c][h:h + 1, :]), 0.0) for i, (c, h) in enumerate(items)]
        q = [_l2norm_rows(qc_ref[rows[c], hs[h]]) * scale for c, h in items]
        k = [_l2norm_rows(kc_ref[rows[c], hs[h]]) for c, h in items]
        kb = [k[i] * b_col[i] for i in n_items]
        eg = [jnp.exp(gc_col[i]) for i in n_items]
        kq = [_dot_nt(jnp.concatenate([kb[i], q[i]], axis=0), k[i]) for i in n_items]
        inv = _inv_unit_lower([-jnp.where(strict, kq[i][:CHUNK] * gamma[i], 0.0) for i in n_items])
        uw = [_dot(inv[i], jnp.concatenate([vc_ref[rows[c], hs[h]] * b_col[i], kb[i] * eg[i]], axis=1))
              for i, (c, h) in enumerate(items)]
        for i, (c, h) in enumerate(items):
            u_ref[rows[c], hs[h]] = uw[i][:, :HEAD_DIM]
            w_ref[rows[c], hs[h]] = uw[i][:, HEAD_DIM:]
            qe_ref[rows[c], hs[h]] = q[i] * eg[i]
            att_ref[h, rows[c], :] = kq[i][CHUNK:] * gamma[i]
            g_last = gc[c][CHUNK - 1:CHUNK, h:h + 1]
            kdT_ref[c * N_HEADS + h] = (k[i] * jnp.exp(g_last - gc_col[i])).T

    for c in range(n_chunks):
        rows = slice(c * CHUNK, (c + 1) * CHUNK)
        S = [S_ref[h] for h in heads]
        ws = [_dot(jnp.concatenate([w_ref[rows, hs[h]], qe_ref[rows, hs[h]]], axis=0), S[h]) for h in heads]
        v_new = [u_ref[rows, hs[h]] - ws[h][:CHUNK] for h in heads]
        av = [_dot(att_ref[h, rows, :], v_new[h]) for h in heads]
        kv = [_dot(kdT_ref[c * N_HEADS + h], v_new[h]) for h in heads]
        for h in heads:
            S_ref[h] = S[h] * jnp.exp(gl_ref[c:c + 1, h:h + 1]) + kv[h]
            o_ref[rows, hs[h]] = ws[h][CHUNK:] + av[h]

    for c in range(n_chunks):
        rows = slice(c * CHUNK, (c + 1) * CHUNK)
        for h in heads:
            on = _head_norm_rows(o_ref[rows, hs[h]], ng_ref[:, hs[h]], 1e-6, center=False)
            out_ref[0, rows, hs[h]] = (on * _silu(z_ref[0, rows, hs[h]])).astype(out_ref.dtype)


def _gdn_call(proj, gproj, conv_w, a_log, dt_bias, norm_g, *, col0, gate_col, tb=256):
    B, T, _ = proj.shape
    j0 = col0 // D_GROUP
    an = jnp.zeros((1, LANES), F32).at[0, :N_HEADS].set(-jnp.exp(a_log.astype(F32)))
    dtb = jnp.zeros((1, LANES), F32).at[0, :N_HEADS].set(dt_bias)
    colblk = lambda j: pl.BlockSpec((1, tb, D_GROUP), lambda b, t: (b, t, j0 + j))
    halo = lambda j: pl.BlockSpec((1, 8, D_GROUP), lambda b, t: (b, jnp.maximum(t * (tb // 8) - 1, 0), j0 + j))
    return pl.pallas_call(
        functools.partial(_gdn_kernel, tb=tb),
        grid=(B, T // tb),
        in_specs=[colblk(0), colblk(1), colblk(2), colblk(3),
                  pl.BlockSpec((1, tb, LANES), lambda b, t: (b, t, gate_col // LANES)),
                  halo(0), halo(1), halo(2),
                  pl.BlockSpec((4, 3 * D_GROUP), lambda b, t: (0, 0)),
                  pl.BlockSpec((1, LANES), lambda b, t: (0, 0)),
                  pl.BlockSpec((1, LANES), lambda b, t: (0, 0)),
                  pl.BlockSpec((1, D_GROUP), lambda b, t: (0, 0))],
        out_specs=pl.BlockSpec((1, tb, D_GROUP), lambda b, t: (b, t, 0)),
        out_shape=jax.ShapeDtypeStruct((B, T, D_GROUP), BF16),
        scratch_shapes=[pltpu.VMEM((tb, D_GROUP), F32)] * 7
                       + [pltpu.VMEM((N_HEADS, tb, CHUNK), F32),
                          pltpu.VMEM((tb // CHUNK * N_HEADS, HEAD_DIM, CHUNK), F32),
                          pltpu.VMEM((max(tb // CHUNK, 8), LANES), F32),
                          pltpu.VMEM((N_HEADS, HEAD_DIM, HEAD_DIM), F32)],
        compiler_params=pltpu.CompilerParams(dimension_semantics=("arbitrary", "arbitrary"),
                                             vmem_limit_bytes=VMEM_LIMIT),
        name="gdn",
    )(proj, proj, proj, proj, gproj, proj, proj, proj, conv_w, an, dtb, norm_g.reshape(1, D_GROUP))


N_PAIRS = N_RWKV // 2
RWKV_PAIR_GROUP = 8
RWKV_LOW = 384


def _shift1_rows(src_ref, halo_ref, first, r0, nrows, c0, ncols):
    if r0 == 0:
        hl = jnp.where(first, 0.0, halo_ref[0, 7:8, c0:c0 + ncols])
        return jnp.concatenate([hl, src_ref[0, 0:nrows - 1, c0:c0 + ncols]], axis=0)
    return src_ref[0, r0 - 1:r0 - 1 + nrows, c0:c0 + ncols]


def _rwkv_kernel(r_ref, k_ref, v_ref, l0_ref, l1_ref, l2_ref,
                 rh_ref, kh_ref, vh_ref, l0h_ref, l1h_ref, l2h_ref,
                 mu_ref, mul_ref, w0_ref, w2_ref, a0_ref, a2_ref, g2_ref, kk_ref, ka_ref, rk_ref,
                 lng_ref, lnb_ref, out_ref,
                 gs_ref, bo_ref, y_ref, atrt_ref, avk_ref, yk_ref, inv_ref, arb_ref, btT_ref, kvT_ref, wlT_ref,
                 H_ref, *, tb):
    t = pl.program_id(1)
    first = t == 0

    @pl.when(first)
    def _():
        H_ref[...] = jnp.zeros_like(H_ref)

    ri = lax.broadcasted_iota(jnp.int32, (LANES, LANES), 0)
    ci = lax.broadcasted_iota(jnp.int32, (LANES, LANES), 1)
    same_head = ((ri // RWKV_HEAD) == (ci // RWKV_HEAD)).astype(F32)
    causal = _tri(CHUNK)
    strict = _tri(CHUNK, strict=True)
    lane1 = lax.broadcasted_iota(jnp.int32, (1, LANES), 1)
    m0 = (lane1 < RWKV_HEAD).astype(F32)
    m1 = 1.0 - m0
    t2 = lax.broadcasted_iota(jnp.int32, (CHUNK, LANES), 0)
    l2 = lax.broadcasted_iota(jnp.int32, (CHUNK, LANES), 1)
    lo = l2 < RWKV_HEAD
    s2 = l2 & (RWKV_HEAD - 1)
    causal2 = t2 >= s2
    strict2 = t2 > s2
    n_chunks = tb // CHUNK
    pairs = range(N_PAIRS)
    halves = [(p, hh) for p in pairs for hh in range(2)]
    ps = [slice(p * LANES, (p + 1) * LANES) for p in pairs]

    def lerp(src, halo, mu, r0, c0, ncols):
        cur = src[0, r0:r0 + CHUNK, c0:c0 + ncols]
        return cur + (_shift1_rows(src, halo, first, r0, CHUNK, c0, ncols) - cur) * mu

    for c in range(n_chunks):
        r0 = c * CHUNK
        rows = slice(r0, r0 + CHUNK)
        wl = lerp(l0_ref, l0h_ref, mul_ref[:, 0:LANES], r0, 0, LANES)
        g1 = lerp(l1_ref, l1h_ref, mul_ref[:, LANES:2 * LANES], r0, 0, LANES)
        g2 = lerp(l2_ref, l2h_ref, mul_ref[:, 2 * LANES:3 * LANES], r0, 0, LANES)
        wl_t = jnp.where(lo, jnp.tanh(wl), 0.0)
        al = jnp.where(lo, 0.0, wl)
        sg1 = _sigmoid(g1)
        sg2 = jnp.where(l2 < 32, _sigmoid(g2), 0.0)
        for g0 in range(0, N_PAIRS, RWKV_PAIR_GROUP):
            grp = range(g0, g0 + RWKV_PAIR_GROUP)
            lw = {p: -math.exp(-0.5) * _sigmoid(w0_ref[:, ps[p]] + _dot(wl_t, w2_ref[:, ps[p]])) for p in grp}
            a = {p: _sigmoid(a0_ref[:, ps[p]] + _dot(al, a2_ref[:, ps[p]])) for p in grp}
            g = {p: _dot(sg1, g2_ref[0:LANES, ps[p]]) + _dot(sg2, g2_ref[LANES:2 * LANES, ps[p]]) for p in grp}
            r = {p: lerp(r_ref, rh_ref, mu_ref[:, ps[p]], r0, p * LANES, LANES) for p in grp}
            k = {p: lerp(k_ref, kh_ref, mu_ref[:, D_GROUP + p * LANES:D_GROUP + (p + 1) * LANES], r0, p * LANES, LANES)
                 for p in grp}
            v = {p: lerp(v_ref, vh_ref, mu_ref[:, 2 * D_GROUP + p * LANES:2 * D_GROUP + (p + 1) * LANES], r0,
                         p * LANES, LANES) for p in grp}
            kk = {p: k[p] * kk_ref[:, ps[p]] for p in grp}
            nrm = {p: jnp.sqrt(_half_lane_sums(kk[p] * kk[p], m0, lo)) for p in grp}
            kk = {p: kk[p] / jnp.maximum(nrm[p], 1e-12) for p in grp}
            k2 = {p: k[p] * (1.0 + (a[p] - 1.0) * ka_ref[:, ps[p]]) for p in grp}
            rk = {p: _half_lane_sums(r[p] * k2[p] * rk_ref[:, ps[p]], m0, lo) for p in grp}
            for p in grp:
                gs_ref[p, rows, :] = g[p]
                bo_ref[p, rows, :] = rk[p] * v[p]
            cs = {p: _cumsum_rows(lw[p]) for p in grp}
            w_inv = {p: jnp.exp(-cs[p]) for p in grp}
            w_end = {p: jnp.exp(cs[p][CHUNK - 1:CHUNK, :]) for p in grp}
            rt = {p: r[p] * jnp.exp(cs[p]) for p in grp}
            at = {p: -kk[p] * jnp.exp(cs[p] - lw[p]) for p in grp}
            bt = {p: kk[p] * a[p] * w_inv[p] for p in grp}
            kt = {p: k2[p] * w_inv[p] for p in grp}
            atrt = {p: jnp.concatenate([at[p], rt[p]], axis=0).astype(BF16) for p in grp}
            pm0 = {p: _dot_nt(atrt[p] * m0.astype(BF16), jnp.concatenate([bt[p], kt[p]], axis=0)) for p in grp}
            pm1 = {p: _dot_nt(atrt[p] * m1.astype(BF16), jnp.concatenate([kt[p], bt[p]], axis=0)) for p in grp}
            n_ab = {p: jnp.where(strict2, jnp.where(lo, pm0[p][:CHUNK], pm1[p][:CHUNK]), 0.0) for p in grp}
            a_rb = {p: jnp.where(causal2, jnp.where(lo, pm0[p][CHUNK:], pm1[p][CHUNK:]), 0.0) for p in grp}
            akrk = {p: jnp.concatenate([jnp.where(strict2, jnp.where(lo, pm1[p][:CHUNK], pm0[p][:CHUNK]), 0.0),
                                        jnp.where(causal2, jnp.where(lo, pm1[p][CHUNK:], pm0[p][CHUNK:]), 0.0)], axis=0)
                    for p in grp}
            vk = {p: _dot(akrk[p], _blockdiag2(v[p], m1, m0)) for p in grp}
            inv = dict(zip(grp, _inv_unit_lower_packed([n_ab[p] for p in grp], m0, m1)))
            kv = {p: _dot((kt[p] * w_end[p]).T, v[p]) * same_head for p in grp}
            for p in grp:
                i = c * N_PAIRS + p
                atrt_ref[i] = atrt[p]
                avk_ref[i] = vk[p][:CHUNK]
                yk_ref[i] = vk[p][CHUNK:]
                btT_ref[i] = (bt[p] * w_end[p]).T.astype(BF16)
                kvT_ref[i] = kv[p]
                wlT_ref[i] = jnp.broadcast_to(w_end[p], (LANES, LANES)).T
                inv_ref[i] = inv[p].astype(BF16)
                arb_ref[i] = a_rb[p].astype(BF16)

    for c in range(n_chunks):
        rows = slice(c * CHUNK, (c + 1) * CHUNK)
        it = [c * N_PAIRS + p for p in pairs]
        H = [H_ref[p] for p in pairs]
        xy0 = [_dot(atrt_ref[it[p]], H[p]) for p in pairs]
        x = [xy0[p][:CHUNK] + avk_ref[it[p]] for p in pairs]
        u = [_dot(inv_ref[it[p]], _blockdiag2(x[p], m0, m1)) for p in pairs]
        yb = [_dot(arb_ref[it[p]], _blockdiag2(u[p], m0, m1)) for p in pairs]
        bu = [_dot(btT_ref[it[p]], u[p]) for p in pairs]
        for p in pairs:
            y_ref[p, rows, :] = xy0[p][CHUNK:] + yk_ref[it[p]] + yb[p]
            H_ref[p] = H[p] * wlT_ref[it[p]] + bu[p] * same_head + kvT_ref[it[p]]

    inv_n = 1.0 / RWKV_HEAD
    for c in range(n_chunks):
        rows = slice(c * CHUNK, (c + 1) * CHUNK)
        y = [y_ref[p, rows, :] for p in pairs]
        yc = [y[p] - _half_lane_sums(y[p], m0, lo) * inv_n for p in pairs]
        var = [_half_lane_sums(yc[p] * yc[p], m0, lo) * inv_n for p in pairs]
        for p in pairs:
            yn = yc[p] * lax.rsqrt(var[p] + RWKV_LN_EPS) * lng_ref[:, ps[p]] + lnb_ref[:, ps[p]]
            out_ref[0, rows, ps[p]] = ((yn + bo_ref[p, rows, :]) * gs_ref[p, rows, :]).astype(out_ref.dtype)


def _rwkv_call(proj, mu, w0, w2, a0, a2, g2, k_k, k_a, r_k, ln_g, ln_b, *, col0, tb=256):
    B, T, _ = proj.shape
    j0 = col0 // D_GROUP
    l0 = (col0 + 3 * D_GROUP) // LANES
    row = lambda a: a.reshape(1, -1).astype(F32)
    mul = jnp.zeros((1, RWKV_LOW), F32).at[0, :288].set(mu[3 * D_GROUP:])
    w2p = jnp.zeros((LANES, D_GROUP), F32).at[:64].set(w2)
    a2p = jnp.zeros((LANES, D_GROUP), F32).at[64:].set(a2)
    g2p = jnp.zeros((2 * LANES, D_GROUP), F32).at[:160].set(g2)
    colblk = lambda j: pl.BlockSpec((1, tb, D_GROUP), lambda b, t: (b, t, j0 + j))
    lowblk = lambda j: pl.BlockSpec((1, tb, LANES), lambda b, t: (b, t, l0 + j))
    hrow = lambda t: jnp.maximum(t * (tb // 8) - 1, 0)
    halo = lambda j: pl.BlockSpec((1, 8, D_GROUP), lambda b, t: (b, hrow(t), j0 + j))
    lowhalo = lambda j: pl.BlockSpec((1, 8, LANES), lambda b, t: (b, hrow(t), l0 + j))
    full = lambda a: pl.BlockSpec(a.shape, lambda b, t: (0,) * a.ndim)
    params = [row(mu[:3 * D_GROUP]), mul, row(w0), w2p, row(a0), a2p, g2p, row(k_k), row(k_a), row(r_k),
              row(ln_g), row(ln_b)]
    big = pltpu.VMEM((N_PAIRS, tb, LANES), F32)
    n_items = tb // CHUNK * N_PAIRS
    return pl.pallas_call(
        functools.partial(_rwkv_kernel, tb=tb),
        grid=(B, T // tb),
        in_specs=[colblk(0), colblk(1), colblk(2), lowblk(0), lowblk(1), lowblk(2),
                  halo(0), halo(1), halo(2), lowhalo(0), lowhalo(1), lowhalo(2)] + [full(a) for a in params],
        out_specs=pl.BlockSpec((1, tb, D_GROUP), lambda b, t: (b, t, 0)),
        out_shape=jax.ShapeDtypeStruct((B, T, D_GROUP), BF16),
        scratch_shapes=[big] * 3 + [
            pltpu.VMEM((n_items, 2 * CHUNK, LANES), BF16),
            pltpu.VMEM((n_items, CHUNK, LANES), F32),
            pltpu.VMEM((n_items, CHUNK, LANES), F32),
            pltpu.VMEM((n_items, CHUNK, LANES), BF16),
            pltpu.VMEM((n_items, CHUNK, LANES), BF16),
            pltpu.VMEM((n_items, LANES, CHUNK), BF16),
            pltpu.VMEM((n_items, LANES, LANES), F32),
            pltpu.VMEM((n_items, LANES, LANES), F32),
            pltpu.VMEM((N_PAIRS, LANES, LANES), F32)],
        compiler_params=pltpu.CompilerParams(dimension_semantics=("arbitrary", "arbitrary"),
                                             vmem_limit_bytes=VMEM_LIMIT),
        name="rwkv7",
    )(*([proj] * 12), *params)


def _ada_kernel(c_ref, w_ref, b_ref, out_ref):
    sc = _silu(c_ref[...]).astype(BF16)
    out_ref[0] = jnp.dot(sc, w_ref[0].astype(BF16), preferred_element_type=F32) + b_ref[0]


def _ada_call(c, ada_w, ada_b, *, tn=1536):
    B = c.shape[0]
    n_mod = ada_w.shape[0] * ada_w.shape[1]
    w = ada_w.reshape(n_mod, D_MODEL, 3 * D_MODEL)
    b = ada_b.reshape(n_mod, 1, 3 * D_MODEL)
    return pl.pallas_call(
        _ada_kernel,
        grid=(n_mod, 3 * D_MODEL // tn),
        in_specs=[pl.BlockSpec((B, D_MODEL), lambda i, j: (0, 0)),
                  pl.BlockSpec((1, D_MODEL, tn), lambda i, j: (i, 0, j)),
                  pl.BlockSpec((1, 1, tn), lambda i, j: (i, 0, j))],
        out_specs=pl.BlockSpec((1, B, tn), lambda i, j: (i, 0, j)),
        out_shape=jax.ShapeDtypeStruct((n_mod, B, 3 * D_MODEL), F32),
        compiler_params=pltpu.CompilerParams(dimension_semantics=("arbitrary", "arbitrary"),
                                             vmem_limit_bytes=VMEM_LIMIT),
        name="adaln",
    )(c, w, b)


def _mod_spec(i, part, nb):
    return pl.BlockSpec((1, 1, D_MODEL), lambda b, t: (i * nb + b, 0, part))


def _modulate_kernel(x_ref, shift_ref, scale_ref, h_ref):
    h_ref[0] = (x_ref[0] * (1.0 + scale_ref[0]) + shift_ref[0]).astype(h_ref.dtype)


def _modulate_call(x, mods3, i, *, tb=512):
    B, T, _ = x.shape
    blk = pl.BlockSpec((1, tb, D_MODEL), lambda b, t: (b, t, 0))
    return pl.pallas_call(
        _modulate_kernel,
        grid=(B, T // tb),
        in_specs=[blk, _mod_spec(i, 0, B), _mod_spec(i, 1, B)],
        out_specs=blk,
        out_shape=jax.ShapeDtypeStruct(x.shape, BF16),
        compiler_params=pltpu.CompilerParams(dimension_semantics=("arbitrary", "arbitrary"),
                                             vmem_limit_bytes=VMEM_LIMIT),
        name="modulate",
    )(x, mods3, mods3)


LN_ROWS = 16


def _proj_ln_kernel(*refs, n_lhs, nk, tm, with_next):
    lhs = refs[:n_lhs]
    ws = refs[n_lhs:2 * n_lhs]
    x_ref, gate_ref, g_ref, b_ref = refs[2 * n_lhs:2 * n_lhs + 4]
    if with_next:
        shift_ref, scale_ref, xo_ref, h_ref, acc_ref = refs[2 * n_lhs + 4:]
    else:
        xo_ref, acc_ref = refs[2 * n_lhs + 4:]
    k = pl.program_id(2)

    def partial_product():
        part = jnp.dot(lhs[0][0], ws[0][...], preferred_element_type=F32)
        for j in range(1, n_lhs):
            part = part + jnp.dot(lhs[j][0], ws[j][...], preferred_element_type=F32)
        return part

    @pl.when(k == 0)
    def _():
        acc_ref[...] = partial_product()

    @pl.when(k > 0)
    def _():
        acc_ref[...] += partial_product()

    @pl.when(k == nk - 1)
    def _():
        gate1 = 1.0 + gate_ref[0]
        g, b = g_ref[...], b_ref[...]
        if with_next:
            scale1, shift = 1.0 + scale_ref[0], shift_ref[0]

        def rows_body(r, carry):
            rows = pl.ds(pl.multiple_of(r * LN_ROWS, LN_ROWS), LN_ROWS)
            z = ALPHA * x_ref[0, rows, :] + gate1 * acc_ref[rows, :]
            zc = z - jnp.mean(z, axis=-1, keepdims=True)
            var = jnp.mean(zc * zc, axis=-1, keepdims=True)
            xn = zc * lax.rsqrt(var + LN_EPS) * g + b
            xo_ref[0, rows, :] = xn
            if with_next:
                h_ref[0, rows, :] = (xn * scale1 + shift).astype(h_ref.dtype)
            return carry

        lax.fori_loop(0, tm // LN_ROWS, rows_body, 0, unroll=8)


def _proj_ln_call(lhs, ws, x, mods3, i, g, b, *, with_next, tm, tk, vmem_limit=VMEM_LIMIT):
    B, T, D = x.shape
    n_lhs = len(lhs)
    nk = lhs[0].shape[2] // tk
    blk = pl.BlockSpec((1, tm, D), lambda b, t, k: (b, t, 0))
    row = pl.BlockSpec((1, D), lambda b, t, k: (0, 0))
    mod = lambda ii, part: pl.BlockSpec((1, 1, D), lambda b, t, k: (ii * B + b, 0, part))
    wspec = lambda idx, k0: pl.BlockSpec((None, tk, D), lambda b, t, k: (idx, k0 + k, 0))
    in_specs = ([pl.BlockSpec((1, tm, tk), lambda b, t, k: (b, t, k))] * n_lhs
                + [wspec(idx, k0) for _, idx, k0 in ws]
                + [blk, mod(i, 2), row, row])
    args = list(lhs) + [w for w, _, _ in ws] + [x, mods3, g.reshape(1, D), b.reshape(1, D)]
    out_specs = [blk]
    out_shape = [jax.ShapeDtypeStruct(x.shape, F32)]
    if with_next:
        in_specs += [mod(i + 1, 0), mod(i + 1, 1)]
        args += [mods3, mods3]
        out_specs.append(blk)
        out_shape.append(jax.ShapeDtypeStruct(x.shape, BF16))
    return pl.pallas_call(
        functools.partial(_proj_ln_kernel, n_lhs=n_lhs, nk=nk, tm=tm, with_next=with_next),
        grid=(B, T // tm, nk),
        in_specs=in_specs, out_specs=out_specs, out_shape=out_shape,
        scratch_shapes=[pltpu.VMEM((tm, D), F32)],
        compiler_params=pltpu.CompilerParams(dimension_semantics=("arbitrary", "arbitrary", "arbitrary"),
                                             vmem_limit_bytes=vmem_limit),
        name="proj_residual_ln",
    )(*args)


def _matmul_kernel(a_ref, b_ref, o_ref, *scratch, nk):
    if nk == 1:
        o_ref[...] = jnp.dot(a_ref[...], b_ref[...], preferred_element_type=F32).astype(o_ref.dtype)
        return
    (acc_ref,) = scratch
    k = pl.program_id(2)

    @pl.when(k == 0)
    def _():
        acc_ref[...] = jnp.zeros_like(acc_ref)

    acc_ref[...] += jnp.dot(a_ref[...], b_ref[...], preferred_element_type=F32)

    @pl.when(k == nk - 1)
    def _():
        o_ref[...] = acc_ref[...].astype(o_ref.dtype)


def _matmul_call(a, b, *, tm, tn, tk, out_dtype=F32):
    M, K = a.shape
    _, N = b.shape
    nk = K // tk
    return pl.pallas_call(
        functools.partial(_matmul_kernel, nk=nk),
        grid=(N // tn, M // tm, nk),
        in_specs=[pl.BlockSpec((tm, tk), lambda j, i, k: (i, k)),
                  pl.BlockSpec((tk, tn), lambda j, i, k: (k, j))],
        out_specs=pl.BlockSpec((tm, tn), lambda j, i, k: (i, j)),
        out_shape=jax.ShapeDtypeStruct((M, N), out_dtype),
        scratch_shapes=[] if nk == 1 else [pltpu.VMEM((tm, tn), F32)],
        compiler_params=pltpu.CompilerParams(dimension_semantics=("arbitrary", "arbitrary", "arbitrary"),
                                             vmem_limit_bytes=VMEM_LIMIT),
        name="matmul",
    )(a, b)


XPOSE_ROWS = 256


def _matmul_f32wt_kernel(a_ref, wt_ref, o_ref, wb_ref):
    @pl.when(pl.program_id(1) == 0)
    def _():
        for r in range(0, wt_ref.shape[0], XPOSE_ROWS):
            wb_ref[:, r:r + XPOSE_ROWS] = wt_ref[r:r + XPOSE_ROWS, :].T.astype(BF16)

    o_ref[...] = jnp.dot(a_ref[...], wb_ref[...], preferred_element_type=F32)


def _matmul_f32wt_call(a, wt, idx, ncols, *, tm=1024, tn=1024):
    M, K = a.shape
    tm = min(tm, M)
    return pl.pallas_call(
        _matmul_f32wt_kernel,
        grid=(ncols // tn, M // tm),
        in_specs=[pl.BlockSpec((tm, K), lambda j, i: (i, 0)),
                  pl.BlockSpec((None, tn, K), lambda j, i: (idx, j, 0))],
        out_specs=pl.BlockSpec((tm, tn), lambda j, i: (i, j)),
        out_shape=jax.ShapeDtypeStruct((M, ncols), F32),
        scratch_shapes=[pltpu.VMEM((K, tn), BF16)],
        compiler_params=pltpu.CompilerParams(dimension_semantics=("arbitrary", "arbitrary"),
                                             vmem_limit_bytes=VMEM_LIMIT),
        name="matmul_f32wt",
    )(a, wt)


def _regroup_rows_kernel(wt_ref, o_ref, *, segments):
    tc = wt_ref.shape[1]
    pieces, pos = [], 0
    for src, width, dst in segments:
        if dst > pos:
            pieces.append(jnp.zeros((dst - pos, tc), F32))
        pieces.append(wt_ref[src:src + width, :])
        pos = dst + width
    if pos < o_ref.shape[1]:
        pieces.append(jnp.zeros((o_ref.shape[1] - pos, tc), F32))
    stacked = jnp.concatenate(pieces, axis=0)
    for r in range(0, o_ref.shape[1], LANES):
        o_ref[:, r:r + LANES] = stacked[r:r + LANES, :].T.astype(o_ref.dtype)


def _regroup_rows_call(wt, idx, segments, out_cols, *, block_rows, block_index, tc):
    _, _, K = wt.shape
    return pl.pallas_call(
        functools.partial(_regroup_rows_kernel, segments=segments),
        grid=(K // tc,),
        in_specs=[pl.BlockSpec((None, block_rows, tc), lambda c: (idx, block_index, c))],
        out_specs=pl.BlockSpec((tc, out_cols), lambda c: (c, 0)),
        out_shape=jax.ShapeDtypeStruct((K, out_cols), BF16),
        compiler_params=pltpu.CompilerParams(dimension_semantics=("arbitrary",), vmem_limit_bytes=VMEM_LIMIT),
        name="regroup_rows",
    )(wt)


def _swiglu_up_kernel(h_ref, wg_ref, wu_ref, o_ref, wgb_ref, wub_ref):
    @pl.when(pl.program_id(1) == 0)
    def _():
        wgb_ref[...] = wg_ref[...].astype(BF16)
        wub_ref[...] = wu_ref[...].astype(BF16)

    h = h_ref[...]
    g = jnp.dot(h, wgb_ref[...], preferred_element_type=F32)
    u = jnp.dot(h, wub_ref[...], preferred_element_type=F32)
    o_ref[...] = (_silu(g) * u).astype(o_ref.dtype)


def _swiglu_up_call(h, wg, wu, layer, *, tm=1024, tn=512):
    M, K = h.shape
    _, _, N = wg.shape
    tm = min(tm, M)
    wspec = pl.BlockSpec((None, K, tn), lambda j, i: (layer, 0, j))
    return pl.pallas_call(
        _swiglu_up_kernel,
        grid=(N // tn, M // tm),
        in_specs=[pl.BlockSpec((tm, K), lambda j, i: (i, 0)), wspec, wspec],
        out_specs=pl.BlockSpec((tm, tn), lambda j, i: (i, j)),
        out_shape=jax.ShapeDtypeStruct((M, N), BF16),
        scratch_shapes=[pltpu.VMEM((K, tn), BF16), pltpu.VMEM((K, tn), BF16)],
        compiler_params=pltpu.CompilerParams(dimension_semantics=("arbitrary", "arbitrary"),
                                             vmem_limit_bytes=VMEM_LIMIT),
        name="swiglu_up",
    )(h, wg, wu)


A_MAIN = 4 * D_GROUP
B_COLS_PAD = 3 * D_GROUP + RWKV_LOW + LANES
MLSTM_GATE_COL = 3 * D_GROUP + RWKV_LOW
CD_MAIN = 8 * D_GROUP
GDN_COL0 = 4 * D_GROUP


N_GATES = 2 * N_HEADS
B_SRC = A_MAIN + N_GATES
B_LOW = 64 + 64 + 160
B_SEGMENTS = ((B_SRC, 3 * D_GROUP, 0), (B_SRC + 3 * D_GROUP, B_LOW, 3 * D_GROUP), (A_MAIN, N_GATES, MLSTM_GATE_COL))
CD_GATE_SEGMENTS = ((0, N_GATES, 0),)


def kernel(x, c, positions, ada_w, ada_b, ln_g, ln_b, ab_w_in, ab_w_out, mlstm_conv_w, mlstm_gate_b, mlstm_norm_g, rwkv_mu, rwkv_w0, rwkv_w2, rwkv_a0, rwkv_a2, rwkv_g2, rwkv_k_k, rwkv_k_a, rwkv_r_k, rwkv_ln_g, rwkv_ln_b, cd_w_in, cd_w_out, ret_norm_g, gdn_conv_w, gdn_a_log, gdn_dt_bias, gdn_norm_g, ffn_w_gate, ffn_w_up, ffn_w_down):
    B, T, D = x.shape
    M = B * T
    depth = ada_w.shape[0]
    mods = _ada_call(c, ada_w, ada_b)
    mods3 = mods.reshape(2 * depth * B, 1, 3 * D)
    h = _modulate_call(x, mods3, 0)
    w_down = ffn_w_down.astype(BF16)
    for layer in range(depth):
        j = layer // 2
        i_mix, i_ffn = 2 * layer, 2 * layer + 1
        if layer % 2 == 0:
            h2 = h.reshape(M, D)
            wt = jnp.swapaxes(ab_w_in, 1, 2)
            proj_a = _matmul_f32wt_call(h2, wt, j, A_MAIN).reshape(B, T, A_MAIN)
            w_b = _regroup_rows_call(wt, j, B_SEGMENTS, B_COLS_PAD, block_rows=wt.shape[1], block_index=0, tc=256)
            proj_b = _matmul_call(h2, w_b, tm=min(1024, M), tn=B_COLS_PAD // 2, tk=D)
            proj_b = proj_b.reshape(B, T, B_COLS_PAD)
            ya = _mlstm_call(proj_a, proj_b, mlstm_conv_w[j], mlstm_gate_b[j], mlstm_norm_g[j],
                             gate_col=MLSTM_GATE_COL)
            yb = _rwkv_call(proj_b, rwkv_mu[j], rwkv_w0[j], rwkv_w2[j], rwkv_a0[j], rwkv_a2[j], rwkv_g2[j],
                            rwkv_k_k[j], rwkv_k_a[j], rwkv_r_k[j], rwkv_ln_g[j], rwkv_ln_b[j], col0=0)
            w_out = ab_w_out
        else:
            h2 = h.reshape(M, D)
            wt = jnp.swapaxes(cd_w_in, 1, 2)
            proj = _matmul_f32wt_call(h2, wt, j, CD_MAIN).reshape(B, T, CD_MAIN)
            w_gates = _regroup_rows_call(wt, j, CD_GATE_SEGMENTS, LANES, block_rows=N_GATES,
                                         block_index=CD_MAIN // N_GATES, tc=D)
            proj_g = _matmul_call(h2, w_gates, tm=512, tn=LANES, tk=D).reshape(B, T, LANES)
            ya = _ret_call(proj, positions, ret_norm_g[j])
            yb = _gdn_call(proj, proj_g, gdn_conv_w[j], gdn_a_log[j], gdn_dt_bias[j], gdn_norm_g[j],
                           col0=GDN_COL0, gate_col=0)
            w_out = cd_w_out
        w_out = w_out.astype(BF16)
        x, h = _proj_ln_call([ya, yb], [(w_out, j, 0), (w_out, j, 1)], x, mods3, i_mix,
                             ln_g[layer, 0], ln_b[layer, 0], with_next=True, tm=512, tk=D_GROUP)
        act = _swiglu_up_call(h.reshape(M, D), ffn_w_gate, ffn_w_up, layer)
        last = layer == depth - 1
        res = _proj_ln_call([act.reshape(B, T, D_FF)], [(w_down, layer, 0)], x, mods3, i_ffn,
                            ln_g[layer, 1], ln_b[layer, 1], with_next=not last, tm=512, tk=D_FF // 2,
                            vmem_limit=VMEM_LIMIT_DOWN)
        if last:
            (x,) = res
        else:
            x, h = res
    return x
```

```python
import functools
import math

import numpy as np
import jax
import jax.numpy as jnp
from jax import lax
from jax.experimental import pallas as pl
from jax.experimental.pallas import tpu as pltpu

F32 = jnp.float32
BF16 = jnp.bfloat16

D_MODEL = 2048
D_GROUP = 1024
HEAD_DIM = 128
N_HEADS = 8
RWKV_HEAD = 64
N_RWKV = 16
CHUNK = 64
D_FF = 5632
DEPTH = 2
ALPHA = (2 * DEPTH) ** 0.25
LN_EPS = 1e-5
RWKV_LN_EPS = 64e-5
ROPE_BASE = 10000.0
RET_GAMMA_BASE = 5.0
LANES = 128
VMEM_LIMIT = 48 * 1024 * 1024
VMEM_LIMIT_DOWN = 58 * 1024 * 1024


def _dot(a, b):
    return jnp.dot(a.astype(BF16), b.astype(BF16), preferred_element_type=F32)


def _dot_nt(a, b):
    return lax.dot_general(a.astype(BF16), b.astype(BF16), (((1,), (1,)), ((), ())),
                           preferred_element_type=F32)


def _split3(x):
    hi = x.astype(BF16)
    r1 = x - hi.astype(F32)
    mid = r1.astype(BF16)
    lo = (r1 - mid.astype(F32)).astype(BF16)
    return hi, mid, lo


def _cumsum_rows(x):
    n = x.shape[1]
    out = jnp.dot(_tri(CHUNK).astype(BF16), jnp.concatenate(_split3(x), axis=1), preferred_element_type=F32)
    return out[:, :n] + out[:, n:2 * n] + out[:, 2 * n:]


def _half_lane_sums(x, m0, lo):
    s0 = jnp.sum(x * m0, axis=-1, keepdims=True)
    s1 = jnp.sum(x * (1.0 - m0), axis=-1, keepdims=True)
    return jnp.where(lo, s0, s1)


def _sigmoid(x):
    return 1.0 / (1.0 + jnp.exp(-x))


def _silu(x):
    return x * _sigmoid(x)


def _log_sigmoid(x):
    return jnp.minimum(x, 0.0) - jnp.log1p(jnp.exp(-jnp.abs(x)))


def _softplus(x):
    return jnp.maximum(x, 0.0) + jnp.log1p(jnp.exp(-jnp.abs(x)))


def _tri(n, strict=False):
    r = lax.broadcasted_iota(jnp.int32, (n, n), 0)
    c = lax.broadcasted_iota(jnp.int32, (n, n), 1)
    return (r > c) if strict else (r >= c)


def _conv_silu_rows(src_ref, halo_ref, w, first, r0, nrows, c0, ncols):
    cur = src_ref[0, r0:r0 + nrows, c0:c0 + ncols]
    acc = w[3:4] * cur
    if r0 == 0:
        hl = jnp.where(first, 0.0, halo_ref[0, :, c0:c0 + ncols])
        ext = jnp.concatenate([hl, cur[0:8]], axis=0)
        for j in range(3):
            head = ext[5 + j:13 + j]
            if nrows > 8:
                rest = src_ref[0, 5 + j:nrows - 3 + j, c0:c0 + ncols]
                sh = jnp.concatenate([head, rest], axis=0)
            else:
                sh = head
            acc = acc + w[j:j + 1] * sh
    else:
        for j in range(3):
            acc = acc + w[j:j + 1] * src_ref[0, r0 - 3 + j:r0 - 3 + j + nrows, c0:c0 + ncols]
    return _silu(acc)


def _head_norm_rows(h, g_row, eps, center=True):
    if center:
        h = h - jnp.mean(h, axis=-1, keepdims=True)
    return h * lax.rsqrt(jnp.mean(h * h, axis=-1, keepdims=True) + eps) * g_row


MLSTM_CHUNK_GROUP = 4


def _cummax_rows(x):
    row = lax.broadcasted_iota(jnp.int32, x.shape, 0)
    d = 1
    while d < x.shape[0]:
        x = jnp.where(row >= d, jnp.maximum(x, pltpu.roll(x, d, 0)), x)
        d *= 2
    return x


def _bcast_head_cols(x, sel, pieces):
    m = x.shape[0]
    parts, rest = [], x
    for _ in range(pieces):
        hi = rest.astype(BF16)
        parts.append(hi)
        rest = rest - hi.astype(F32)
    out = jnp.dot(jnp.concatenate(parts, axis=0), sel, preferred_element_type=F32)
    acc = out[:m]
    for i in range(1, pieces):
        acc = acc + out[i * m:(i + 1) * m]
    return acc


def _mlstm_kernel(q_ref, k_ref, v_ref, o_ref, g_ref, qh_ref, kh_ref, cw_ref, gb_ref, ng_ref,
                  out_ref, qc_ref, kc_ref, sv_ref, rs_ref, hh_ref, b0_ref, cm_ref, kvn_ref, CN_ref, m_ref, *, tb):
    t = pl.program_id(1)
    first = t == 0

    @pl.when(first)
    def _():
        CN_ref[...] = jnp.zeros_like(CN_ref)
        m_ref[...] = jnp.zeros_like(m_ref)

    for c in range(tb // CHUNK):
        for cb in range(D_GROUP // 256):
            cs = cb * 256
            qc_ref[c * CHUNK:(c + 1) * CHUNK, cs:cs + 256] = _conv_silu_rows(
                q_ref, qh_ref, cw_ref[:, cs:cs + 256], first, c * CHUNK, CHUNK, cs, 256)
            kc_ref[c * CHUNK:(c + 1) * CHUNK, cs:cs + 256] = _conv_silu_rows(
                k_ref, kh_ref, cw_ref[:, D_GROUP + cs:D_GROUP + cs + 256], first, c * CHUNK, CHUNK, cs, 256)

    causal = _tri(CHUNK)
    gb = gb_ref[...]
    scale = HEAD_DIM ** -0.5
    n_chunks = tb // CHUNK
    heads = range(N_HEADS)
    hs = [slice(h * HEAD_DIM, (h + 1) * HEAD_DIM) for h in heads]
    head_lane = lax.broadcasted_iota(jnp.int32, (1, LANES), 1) < N_HEADS
    sel = (lax.broadcasted_iota(jnp.int32, (LANES, N_HEADS * LANES), 0)
           == (lax.broadcasted_iota(jnp.int32, (LANES, N_HEADS * LANES), 1) >> 7)).astype(BF16)
    ones = jnp.ones((CHUNK, HEAD_DIM), F32)
    last = slice(CHUNK - 1, CHUNK)

    for c0 in range(0, n_chunks, MLSTM_CHUNK_GROUP):
        chunks = range(c0, c0 + MLSTM_CHUNK_GROUP)
        rows = {c: slice(c * CHUNK, (c + 1) * CHUNK) for c in chunks}
        z = {c: g_ref[0, rows[c], :] + gb for c in chunks}
        b0 = {c: pltpu.roll(_cumsum_rows(_log_sigmoid(z[c])), LANES - N_HEADS, 1) for c in chunks}
        cv = {c: jnp.where(head_lane, z[c] - b0[c], 0.0) for c in chunks}
        cm = {c: _cummax_rows(cv[c]) for c in chunks}
        cT = {c: cv[c].T for c in chunks}
        cmb = {c: _bcast_head_cols(cm[c], sel, 3) for c in chunks}
        e1b = {c: _bcast_head_cols(jnp.where(head_lane, jnp.exp(cv[c] - cm[c][last]), 0.0), sel, 2) for c in chunks}
        for c in chunks:
            b0_ref[rows[c], :] = b0[c]
            cm_ref[rows[c], :] = cm[c]
        items = [(c, h) for c in chunks for h in heads]
        n_items = range(len(items))
        q = [qc_ref[rows[c], hs[h]] for c, h in items]
        k = [kc_ref[rows[c], hs[h]] * scale for c, h in items]
        vo = [jnp.concatenate([v_ref[0, rows[c], hs[h]], ones], axis=1).astype(BF16) for c, h in items]
        qk = [_dot_nt(q[i], k[i]) for i in n_items]
        s = [qk[i] * jnp.where(causal, jnp.exp(jnp.minimum(cT[c][h:h + 1, :] - cmb[c][:, h * LANES:h * LANES + CHUNK],
                                                           0.0)), 0.0) for i, (c, h) in enumerate(items)]
        s_hi = [s[i].astype(BF16) for i in n_items]
        s_lo = [(s[i] - s_hi[i].astype(F32)).astype(BF16) for i in n_items]
        svr = [jnp.dot(jnp.concatenate([s_hi[i], s_lo[i]], axis=0), vo[i], preferred_element_type=F32)
               for i in n_items]
        kvn = [_dot((k[i] * e1b[c][:, hs[h]]).T, vo[i]) for i, (c, h) in enumerate(items)]
        for i, (c, h) in enumerate(items):
            sv_ref[rows[c], hs[h]] = svr[i][:CHUNK, :HEAD_DIM] + svr[i][CHUNK:, :HEAD_DIM]
            rs_ref[rows[c], hs[h]] = svr[i][:CHUNK, HEAD_DIM:] + svr[i][CHUNK:, HEAD_DIM:]
            kvn_ref[c * N_HEADS + h] = kvn[i]

    for c in range(n_chunks):
        rows = slice(c * CHUNK, (c + 1) * CHUNK)
        m = m_ref[0:1, :]
        b0 = b0_ref[rows, :]
        cm = cm_ref[rows, :]
        mx = jnp.maximum(cm, m)
        m_new = jnp.maximum(b0[last] + m, b0[last] + cm[last])
        m_ref[0:1, :] = m_new
        zero = lambda x: jnp.where(head_lane, x, 0.0)
        fib = _bcast_head_cols(zero(jnp.exp(cm - mx)), sel, 2)
        scb = _bcast_head_cols(zero(jnp.exp(m - mx)), sel, 2)
        emtb = _bcast_head_cols(zero(jnp.exp(jnp.minimum(-(b0 + mx), 80.0))), sel, 2)
        dfb = _bcast_head_cols(jnp.concatenate([zero(jnp.exp(b0[last] + m - m_new)),
                                                zero(jnp.exp(b0[last] + cm[last] - m_new)),
                                                jnp.zeros((6, LANES), F32)], axis=0), sel, 2)
        CN = [CN_ref[h] for h in heads]
        qcn = [_dot(qc_ref[rows, hs[h]], CN[h]) for h in heads]
        for h in heads:
            num = fib[:, hs[h]] * sv_ref[rows, hs[h]] + scb[:, hs[h]] * qcn[h][:, :HEAD_DIM]
            den = fib[:, hs[h]] * rs_ref[rows, hs[h]] + scb[:, hs[h]] * qcn[h][:, HEAD_DIM:]
            hh_ref[rows, hs[h]] = num / jnp.maximum(jnp.abs(den), emtb[:, hs[h]])
            dec = jnp.concatenate([dfb[0:1, hs[h]]] * 2, axis=1)
            fkv = jnp.concatenate([dfb[1:2, hs[h]]] * 2, axis=1)
            CN_ref[h] = CN[h] * dec + kvn_ref[c * N_HEADS + h] * fkv

    for c in range(n_chunks):
        rows = slice(c * CHUNK, (c + 1) * CHUNK)
        hn = [_head_norm_rows(hh_ref[rows, hs[h]], ng_ref[:, hs[h]], LN_EPS) for h in heads]
        for h in heads:
            out_ref[0, rows, hs[h]] = (hn[h] * _sigmoid(o_ref[0, rows, hs[h]])).astype(out_ref.dtype)


def _mlstm_call(proj, gproj, conv_w, gate_b, norm_g, *, gate_col, tb=256):
    B, T, _ = proj.shape
    nt = T // tb
    gb = jnp.zeros((1, LANES), F32).at[0, :2 * N_HEADS].set(gate_b)
    ng = norm_g.reshape(1, D_GROUP)
    colblk = lambda j: pl.BlockSpec((1, tb, D_GROUP), lambda b, t: (b, t, j))
    halo = lambda j: pl.BlockSpec((1, 8, D_GROUP), lambda b, t: (b, jnp.maximum(t * (tb // 8) - 1, 0), j))
    return pl.pallas_call(
        functools.partial(_mlstm_kernel, tb=tb),
        grid=(B, nt),
        in_specs=[colblk(0), colblk(1), colblk(2), colblk(3),
                  pl.BlockSpec((1, tb, LANES), lambda b, t: (b, t, gate_col // LANES)),
                  halo(0), halo(1),
                  pl.BlockSpec((4, 2 * D_GROUP), lambda b, t: (0, 0)),
                  pl.BlockSpec((1, LANES), lambda b, t: (0, 0)),
                  pl.BlockSpec((1, D_GROUP), lambda b, t: (0, 0))],
        out_specs=pl.BlockSpec((1, tb, D_GROUP), lambda b, t: (b, t, 0)),
        out_shape=jax.ShapeDtypeStruct((B, T, D_GROUP), BF16),
        scratch_shapes=[pltpu.VMEM((tb, D_GROUP), F32)] * 5
                       + [pltpu.VMEM((tb, LANES), F32)] * 2
                       + [pltpu.VMEM((tb // CHUNK * N_HEADS, HEAD_DIM, 2 * HEAD_DIM), F32),
                          pltpu.VMEM((N_HEADS, HEAD_DIM, 2 * HEAD_DIM), F32),
                          pltpu.VMEM((8, LANES), F32)],
        compiler_params=pltpu.CompilerParams(dimension_semantics=("arbitrary", "arbitrary"),
                                             vmem_limit_bytes=VMEM_LIMIT),
        name="mlstm",
    )(proj, proj, proj, proj, gproj, proj, proj, conv_w, gb, ng)


RET_CHUNK_GROUP = 1


def _ret_kernel(q_ref, k_ref, v_ref, g_ref, pos_ref, inv_ref, ng_ref, out_ref, qr_ref, o_ref, kv_ref, R_ref, *, tb):
    t = pl.program_id(1)

    @pl.when(t == 0)
    def _():
        R_ref[...] = jnp.zeros_like(R_ref)

    n_chunks = tb // CHUNK
    heads = range(N_HEADS)
    hs = [slice(h * HEAD_DIM, (h + 1) * HEAD_DIM) for h in heads]
    lg = [math.log1p(-2.0 ** (-RET_GAMMA_BASE - h)) for h in heads]
    causal = _tri(CHUNK)
    ri = lax.broadcasted_iota(jnp.int32, (CHUNK, CHUNK), 0)
    ci = lax.broadcasted_iota(jnp.int32, (CHUNK, CHUNK), 1)
    rel = (ri - ci).astype(F32)
    tcol = lax.broadcasted_iota(jnp.int32, (CHUNK, 1), 0).astype(F32)
    scale = HEAD_DIM ** -0.5
    lane = lax.broadcasted_iota(jnp.int32, (1, HEAD_DIM), 1)
    sign = jnp.where(lane < HEAD_DIM // 2, -1.0, 1.0)
    dmat = [jnp.where(causal, jnp.exp(rel * lg[h]), 0.0) for h in heads]

    for c0 in range(0, n_chunks, RET_CHUNK_GROUP):
        chunks = range(c0, c0 + RET_CHUNK_GROUP)
        rows = {c: slice(c * CHUNK, (c + 1) * CHUNK) for c in chunks}
        ang = {c: pos_ref[0, rows[c], :].astype(F32) * inv_ref[...] for c in chunks}
        cos2 = {c: jnp.cos(ang[c]) for c in chunks}
        sin2 = {c: jnp.sin(ang[c]) * sign for c in chunks}
        items = [(c, h) for c in chunks for h in heads]
        n_items = range(len(items))
        rot = lambda z, c: z * cos2[c] + pltpu.roll(z, HEAD_DIM // 2, 1) * sin2[c]
        qr = [rot(q_ref[0, rows[c], hs[h]], c) for c, h in items]
        kr = [rot(k_ref[0, rows[c], hs[h]], c) * scale for c, h in items]
        v = [v_ref[0, rows[c], hs[h]] for c, h in items]
        qk = [_dot_nt(qr[i], kr[i]) * dmat[h] for i, (c, h) in enumerate(items)]
        intra = [_dot(qk[i], v[i]) for i in n_items]
        kv = [_dot((kr[i] * jnp.exp((CHUNK - 1.0 - tcol) * lg[h])).T, v[i]) for i, (c, h) in enumerate(items)]
        for i, (c, h) in enumerate(items):
            qr_ref[rows[c], hs[h]] = qr[i]
            o_ref[rows[c], hs[h]] = intra[i]
            kv_ref[c * N_HEADS + h] = kv[i]

    for h in heads:
        R = R_ref[h]
        for c in range(n_chunks):
            inc = kv_ref[c * N_HEADS + h]
            kv_ref[c * N_HEADS + h] = R
            R = R * math.exp(CHUNK * lg[h]) + inc
        R_ref[h] = R
    for c in range(n_chunks):
        rows = slice(c * CHUNK, (c + 1) * CHUNK)
        inter = [_dot(qr_ref[rows, hs[h]], kv_ref[c * N_HEADS + h]) * jnp.exp((tcol + 1.0) * lg[h]) for h in heads]
        on = [_head_norm_rows(o_ref[rows, hs[h]] + inter[h], ng_ref[:, hs[h]], LN_EPS) for h in heads]
        for h in heads:
            out_ref[0, rows, hs[h]] = (on[h] * _silu(g_ref[0, rows, hs[h]])).astype(out_ref.dtype)


def _ret_call(proj, positions, norm_g, *, tb=256):
    B, T, _ = proj.shape
    half = HEAD_DIM // 2
    inv_freq = ROPE_BASE ** (-jnp.arange(half, dtype=F32) / half)
    inv2 = jnp.concatenate([inv_freq, inv_freq]).reshape(1, HEAD_DIM)
    colblk = lambda j: pl.BlockSpec((1, tb, D_GROUP), lambda b, t: (b, t, j))
    big = pltpu.VMEM((tb, D_GROUP), F32)
    return pl.pallas_call(
        functools.partial(_ret_kernel, tb=tb),
        grid=(B, T // tb),
        in_specs=[colblk(0), colblk(1), colblk(2), colblk(3),
                  pl.BlockSpec((1, tb, 1), lambda b, t: (b, t, 0)),
                  pl.BlockSpec((1, HEAD_DIM), lambda b, t: (0, 0)),
                  pl.BlockSpec((1, D_GROUP), lambda b, t: (0, 0))],
        out_specs=pl.BlockSpec((1, tb, D_GROUP), lambda b, t: (b, t, 0)),
        out_shape=jax.ShapeDtypeStruct((B, T, D_GROUP), BF16),
        scratch_shapes=[big, big, pltpu.VMEM((tb // CHUNK * N_HEADS, HEAD_DIM, HEAD_DIM), F32),
                        pltpu.VMEM((N_HEADS, HEAD_DIM, HEAD_DIM), F32)],
        compiler_params=pltpu.CompilerParams(dimension_semantics=("arbitrary", "arbitrary"),
                                             vmem_limit_bytes=VMEM_LIMIT),
        name="retention",
    )(proj, proj, proj, proj, positions.reshape(B, T, 1), inv2, norm_g.reshape(1, D_GROUP))


def _inv_unit_lower(nms):
    n = nms[0].shape[0]
    eye = (lax.broadcasted_iota(jnp.int32, (n, n), 0) == lax.broadcasted_iota(jnp.int32, (n, n), 1)).astype(F32)
    ps = [eye + nm for nm in nms]
    xs = [_dot(nm, nm) for nm in nms]
    for _ in range(int(math.log2(n)) - 2):
        px = [_dot(jnp.concatenate([p, x], axis=0), x) for p, x in zip(ps, xs)]
        ps = [p + y[:n] for p, y in zip(ps, px)]
        xs = [y[n:] for y in px]
    ps = [p + _dot(p, x) for p, x in zip(ps, xs)]
    resid = [eye - p + _dot(nm, p) for p, nm in zip(ps, nms)]
    return [p + _dot(p, r) for p, r in zip(ps, resid)]


def _blockdiag2(x, m0, m1):
    xb = x.astype(BF16)
    return jnp.concatenate([xb * m0.astype(BF16), xb * m1.astype(BF16)], axis=0)


def _inv_unit_lower_packed(nms, m0, m1):
    n = nms[0].shape[0]
    r = lax.broadcasted_iota(jnp.int32, (n, 2 * n), 0)
    c = lax.broadcasted_iota(jnp.int32, (n, 2 * n), 1)
    eye2 = (r == (c & (n - 1))).astype(F32)
    bd = lambda x: _blockdiag2(x, m0, m1)
    ps = [eye2 + nm for nm in nms]
    xs = [_dot(nm, bd(nm)) for nm in nms]
    for _ in range(int(math.log2(n)) - 2):
        px = [_dot(jnp.concatenate([p, x], axis=0), bd(x)) for p, x in zip(ps, xs)]
        ps = [p + y[:n] for p, y in zip(ps, px)]
        xs = [y[n:] for y in px]
    ps = [p + _dot(p, bd(x)) for p, x in zip(ps, xs)]
    resid = [eye2 - p + _dot(nm, bd(p)) for p, nm in zip(ps, nms)]
    return [p + _dot(p, bd(r_)) for p, r_ in zip(ps, resid)]


def _solve_unit_lower(nms, rhss):
    n = nms[0].shape[0]
    eye = (lax.broadcasted_iota(jnp.int32, (n, n), 0) == lax.broadcasted_iota(jnp.int32, (n, n), 1)).astype(F32)
    ps = [eye + nm for nm in nms]
    xs = [_dot(nm, nm) for nm in nms]
    for _ in range(int(math.log2(n)) - 2):
        px = [_dot(jnp.concatenate([p, x], axis=0), x) for p, x in zip(ps, xs)]
        ps = [p + y[:n] for p, y in zip(ps, px)]
        xs = [y[n:] for y in px]
    ps = [p + _dot(p, x) for p, x in zip(ps, xs)]
    x0 = [_dot(p, r) for p, r in zip(ps, rhss)]
    resid = [r - a + _dot(nm, a) for r, a, nm in zip(rhss, x0, nms)]
    return [a + _dot(p, r) for a, p, r in zip(x0, ps, resid)]


def _l2norm_rows(z):
    return z * lax.rsqrt(jnp.sum(z * z, axis=-1, keepdims=True) + 1e-6)


GDN_CHUNK_GROUP = 2


def _gdn_kernel(q_ref, k_ref, v_ref, z_ref, g_ref, qh_ref, kh_ref, vh_ref, cw_ref, an_ref, dt_ref, ng_ref,
                out_ref, qc_ref, kc_ref, vc_ref, u_ref, w_ref, qe_ref, o_ref, att_ref, kdT_ref, gl_ref, S_ref,
                *, tb):
    t = pl.program_id(1)
    first = t == 0

    @pl.when(first)
    def _():
        S_ref[...] = jnp.zeros_like(S_ref)

    srcs = ((q_ref, qh_ref, qc_ref), (k_ref, kh_ref, kc_ref), (v_ref, vh_ref, vc_ref))
    for c in range(tb // CHUNK):
        for cb in range(D_GROUP // 256):
            cs = cb * 256
            for i, (src, halo, dst) in enumerate(srcs):
                w = cw_ref[:, i * D_GROUP + cs:i * D_GROUP + cs + 256]
                dst[c * CHUNK:(c + 1) * CHUNK, cs:cs + 256] = _conv_silu_rows(
                    src, halo, w, first, c * CHUNK, CHUNK, cs, 256)

    causal = _tri(CHUNK)
    strict = _tri(CHUNK, strict=True)
    a_neg = an_ref[...]
    dtb = dt_ref[...]
    scale = HEAD_DIM ** -0.5
    n_chunks = tb // CHUNK
    heads = range(N_HEADS)
    hs = [slice(h * HEAD_DIM, (h + 1) * HEAD_DIM) for h in heads]

    for c0 in range(0, n_chunks, GDN_CHUNK_GROUP):
        chunks = range(c0, c0 + GDN_CHUNK_GROUP)
        rows = {c: slice(c * CHUNK, (c + 1) * CHUNK) for c in chunks}
        gz = {c: g_ref[0, rows[c], :] for c in chunks}
        beta = {c: _sigmoid(gz[c]) for c in chunks}
        gc = {c: _cumsum_rows(a_neg * _softplus(gz[c] + dtb)) for c in chunks}
        gcT = {c: gc[c].T for c in chunks}
        for c in chunks:
            gl_ref[c:c + 1, :] = gc[c][CHUNK - 1:CHUNK, :]
        items = [(c, h) for c in chunks for h in heads]
        n_items = range(len(items))
        gc_col = [gc[c][:, h:h + 1] for c, h in items]
        b_col = [beta[c][:, 8 + h:9 + h] for c, h in items]
        gamma = [jnp.where(causal, jnp.exp(gc_col[i] - gcT[c][h:h + 1, :]), 0.0) for i, (c, h) in enumerate(items)]
        q = [_l2norm_rows(qc_ref[rows[c], hs[h]]) * scale for c, h in items]
        k = [_l2norm_rows(kc_ref[rows[c], hs[h]]) for c, h in items]
        kb = [k[i] * b_col[i] for i in n_items]
        eg = [jnp.exp(gc_col[i]) for i in n_items]
        kq = [_dot_nt(jnp.concatenate([kb[i], q[i]], axis=0), k[i]) for i in n_items]
        inv = _inv_unit_lower([-jnp.where(strict, kq[i][:CHUNK] * gamma[i], 0.0) for i in n_items])
        uw = [_dot(inv[i], jnp.concatenate([vc_ref[rows[c], hs[h]] * b_col[i], kb[i] * eg[i]], axis=1))
              for i, (c, h) in enumerate(items)]
        for i, (c, h) in enumerate(items):
            u_ref[rows[c], hs[h]] = uw[i][:, :HEAD_DIM]
            w_ref[rows[c], hs[h]] = uw[i][:, HEAD_DIM:]
            qe_ref[rows[c], hs[h]] = q[i] * eg[i]
            att_ref[h, rows[c], :] = kq[i][CHUNK:] * gamma[i]
            g_last = gc[c][CHUNK - 1:CHUNK, h:h + 1]
            kdT_ref[c * N_HEADS + h] = (k[i] * jnp.exp(g_last - gc_col[i])).T

    for c in range(n_chunks):
        rows = slice(c * CHUNK, (c + 1) * CHUNK)
        S = [S_ref[h] for h in heads]
        ws = [_dot(jnp.concatenate([w_ref[rows, hs[h]], qe_ref[rows, hs[h]]], axis=0), S[h]) for h in heads]
        v_new = [u_ref[rows, hs[h]] - ws[h][:CHUNK] for h in heads]
        av = [_dot(att_ref[h, rows, :], v_new[h]) for h in heads]
        kv = [_dot(kdT_ref[c * N_HEADS + h], v_new[h]) for h in heads]
        for h in heads:
            S_ref[h] = S[h] * jnp.exp(gl_ref[c:c + 1, h:h + 1]) + kv[h]
            o_ref[rows, hs[h]] = ws[h][CHUNK:] + av[h]

    for c in range(n_chunks):
        rows = slice(c * CHUNK, (c + 1) * CHUNK)
        for h in heads:
            on = _head_norm_rows(o_ref[rows, hs[h]], ng_ref[:, hs[h]], 1e-6, center=False)
            out_ref[0, rows, hs[h]] = (on * _silu(z_ref[0, rows, hs[h]])).astype(out_ref.dtype)


def _gdn_call(proj, gproj, conv_w, a_log, dt_bias, norm_g, *, col0, gate_col, tb=256):
    B, T, _ = proj.shape
    j0 = col0 // D_GROUP
    an = jnp.zeros((1, LANES), F32).at[0, :N_HEADS].set(-jnp.exp(a_log.astype(F32)))
    dtb = jnp.zeros((1, LANES), F32).at[0, :N_HEADS].set(dt_bias)
    colblk = lambda j: pl.BlockSpec((1, tb, D_GROUP), lambda b, t: (b, t, j0 + j))
    halo = lambda j: pl.BlockSpec((1, 8, D_GROUP), lambda b, t: (b, jnp.maximum(t * (tb // 8) - 1, 0), j0 + j))
    return pl.pallas_call(
        functools.partial(_gdn_kernel, tb=tb),
        grid=(B, T // tb),
        in_specs=[colblk(0), colblk(1), colblk(2), colblk(3),
                  pl.BlockSpec((1, tb, LANES), lambda b, t: (b, t, gate_col // LANES)),
                  halo(0), halo(1), halo(2),
                  pl.BlockSpec((4, 3 * D_GROUP), lambda b, t: (0, 0)),
                  pl.BlockSpec((1, LANES), lambda b, t: (0, 0)),
                  pl.BlockSpec((1, LANES), lambda b, t: (0, 0)),
                  pl.BlockSpec((1, D_GROUP), lambda b, t: (0, 0))],
        out_specs=pl.BlockSpec((1, tb, D_GROUP), lambda b, t: (b, t, 0)),
        out_shape=jax.ShapeDtypeStruct((B, T, D_GROUP), BF16),
        scratch_shapes=[pltpu.VMEM((tb, D_GROUP), F32)] * 7
                       + [pltpu.VMEM((N_HEADS, tb, CHUNK), F32),
                          pltpu.VMEM((tb // CHUNK * N_HEADS, HEAD_DIM, CHUNK), F32),
                          pltpu.VMEM((max(tb // CHUNK, 8), LANES), F32),
                          pltpu.VMEM((N_HEADS, HEAD_DIM, HEAD_DIM), F32)],
        compiler_params=pltpu.CompilerParams(dimension_semantics=("arbitrary", "arbitrary"),
                                             vmem_limit_bytes=VMEM_LIMIT),
        name="gdn",
    )(proj, proj, proj, proj, gproj, proj, proj, proj, conv_w, an, dtb, norm_g.reshape(1, D_GROUP))


N_PAIRS = N_RWKV // 2
RWKV_PAIR_GROUP = 8
RWKV_LOW = 384


def _shift1_rows(src_ref, halo_ref, first, r0, nrows, c0, ncols):
    if r0 == 0:
        hl = jnp.where(first, 0.0, halo_ref[0, 7:8, c0:c0 + ncols])
        return jnp.concatenate([hl, src_ref[0, 0:nrows - 1, c0:c0 + ncols]], axis=0)
    return src_ref[0, r0 - 1:r0 - 1 + nrows, c0:c0 + ncols]


def _rwkv_kernel(r_ref, k_ref, v_ref, l0_ref, l1_ref, l2_ref,
                 rh_ref, kh_ref, vh_ref, l0h_ref, l1h_ref, l2h_ref,
                 mu_ref, mul_ref, w0_ref, w2_ref, a0_ref, a2_ref, g2_ref, kk_ref, ka_ref, rk_ref,
                 lng_ref, lnb_ref, out_ref,
                 gs_ref, bo_ref, y_ref, atrt_ref, avk_ref, yk_ref, inv_ref, arb_ref, btT_ref, kvT_ref, wlT_ref,
                 H_ref, *, tb):
    t = pl.program_id(1)
    first = t == 0

    @pl.when(first)
    def _():
        H_ref[...] = jnp.zeros_like(H_ref)

    ri = lax.broadcasted_iota(jnp.int32, (LANES, LANES), 0)
    ci = lax.broadcasted_iota(jnp.int32, (LANES, LANES), 1)
    same_head = ((ri // RWKV_HEAD) == (ci // RWKV_HEAD)).astype(F32)
    causal = _tri(CHUNK)
    strict = _tri(CHUNK, strict=True)
    lane1 = lax.broadcasted_iota(jnp.int32, (1, LANES), 1)
    m0 = (lane1 < RWKV_HEAD).astype(F32)
    m1 = 1.0 - m0
    t2 = lax.broadcasted_iota(jnp.int32, (CHUNK, LANES), 0)
    l2 = lax.broadcasted_iota(jnp.int32, (CHUNK, LANES), 1)
    lo = l2 < RWKV_HEAD
    s2 = l2 & (RWKV_HEAD - 1)
    causal2 = t2 >= s2
    strict2 = t2 > s2
    n_chunks = tb // CHUNK
    pairs = range(N_PAIRS)
    halves = [(p, hh) for p in pairs for hh in range(2)]
    ps = [slice(p * LANES, (p + 1) * LANES) for p in pairs]

    def lerp(src, halo, mu, r0, c0, ncols):
        cur = src[0, r0:r0 + CHUNK, c0:c0 + ncols]
        return cur + (_shift1_rows(src, halo, first, r0, CHUNK, c0, ncols) - cur) * mu

    for c in range(n_chunks):
        r0 = c * CHUNK
        rows = slice(r0, r0 + CHUNK)
        wl = lerp(l0_ref, l0h_ref, mul_ref[:, 0:LANES], r0, 0, LANES)
        g1 = lerp(l1_ref, l1h_ref, mul_ref[:, LANES:2 * LANES], r0, 0, LANES)
        g2 = lerp(l2_ref, l2h_ref, mul_ref[:, 2 * LANES:3 * LANES], r0, 0, LANES)
        wl_t = jnp.where(lo, jnp.tanh(wl), 0.0)
        al = jnp.where(lo, 0.0, wl)
        sg1 = _sigmoid(g1)
        sg2 = jnp.where(l2 < 32, _sigmoid(g2), 0.0)
        for g0 in range(0, N_PAIRS, RWKV_PAIR_GROUP):
            grp = range(g0, g0 + RWKV_PAIR_GROUP)
            lw = {p: -math.exp(-0.5) * _sigmoid(w0_ref[:, ps[p]] + _dot(wl_t, w2_ref[:, ps[p]])) for p in grp}
            a = {p: _sigmoid(a0_ref[:, ps[p]] + _dot(al, a2_ref[:, ps[p]])) for p in grp}
            g = {p: _dot(sg1, g2_ref[0:LANES, ps[p]]) + _dot(sg2, g2_ref[LANES:2 * LANES, ps[p]]) for p in grp}
            r = {p: lerp(r_ref, rh_ref, mu_ref[:, ps[p]], r0, p * LANES, LANES) for p in grp}
            k = {p: lerp(k_ref, kh_ref, mu_ref[:, D_GROUP + p * LANES:D_GROUP + (p + 1) * LANES], r0, p * LANES, LANES)
                 for p in grp}
            v = {p: lerp(v_ref, vh_ref, mu_ref[:, 2 * D_GROUP + p * LANES:2 * D_GROUP + (p + 1) * LANES], r0,
                         p * LANES, LANES) for p in grp}
            kk = {p: k[p] * kk_ref[:, ps[p]] for p in grp}
            nrm = {p: jnp.sqrt(_half_lane_sums(kk[p] * kk[p], m0, lo)) for p in grp}
            kk = {p: kk[p] / jnp.maximum(nrm[p], 1e-12) for p in grp}
            k2 = {p: k[p] * (1.0 + (a[p] - 1.0) * ka_ref[:, ps[p]]) for p in grp}
            rk = {p: _half_lane_sums(r[p] * k2[p] * rk_ref[:, ps[p]], m0, lo) for p in grp}
            for p in grp:
                gs_ref[p, rows, :] = g[p]
                bo_ref[p, rows, :] = rk[p] * v[p]
            cs = {p: _cumsum_rows(lw[p]) for p in grp}
            w_inv = {p: jnp.exp(-cs[p]) for p in grp}
            w_end = {p: jnp.exp(cs[p][CHUNK - 1:CHUNK, :]) for p in grp}
            rt = {p: r[p] * jnp.exp(cs[p]) for p in grp}
            at = {p: -kk[p] * jnp.exp(cs[p] - lw[p]) for p in grp}
            bt = {p: kk[p] * a[p] * w_inv[p] for p in grp}
            kt = {p: k2[p] * w_inv[p] for p in grp}
            atrt = {p: jnp.concatenate([at[p], rt[p]], axis=0).astype(BF16) for p in grp}
            pm0 = {p: _dot_nt(atrt[p] * m0.astype(BF16), jnp.concatenate([bt[p], kt[p]], axis=0)) for p in grp}
            pm1 = {p: _dot_nt(atrt[p] * m1.astype(BF16), jnp.concatenate([kt[p], bt[p]], axis=0)) for p in grp}
            n_ab = {p: jnp.where(strict2, jnp.where(lo, pm0[p][:CHUNK], pm1[p][:CHUNK]), 0.0) for p in grp}
            a_rb = {p: jnp.where(causal2, jnp.where(lo, pm0[p][CHUNK:], pm1[p][CHUNK:]), 0.0) for p in grp}
            akrk = {p: jnp.concatenate([jnp.where(strict2, jnp.where(lo, pm1[p][:CHUNK], pm0[p][:CHUNK]), 0.0),
                                        jnp.where(causal2, jnp.where(lo, pm1[p][CHUNK:], pm0[p][CHUNK:]), 0.0)], axis=0)
                    for p in grp}
            vk = {p: _dot(akrk[p], _blockdiag2(v[p], m1, m0)) for p in grp}
            inv = dict(zip(grp, _inv_unit_lower_packed([n_ab[p] for p in grp], m0, m1)))
            kv = {p: _dot((kt[p] * w_end[p]).T, v[p]) * same_head for p in grp}
            for p in grp:
                i = c * N_PAIRS + p
                atrt_ref[i] = atrt[p]
                avk_ref[i] = vk[p][:CHUNK]
                yk_ref[i] = vk[p][CHUNK:]
                btT_ref[i] = (bt[p] * w_end[p]).T.astype(BF16)
                kvT_ref[i] = kv[p]
                wlT_ref[i] = jnp.broadcast_to(w_end[p], (LANES, LANES)).T
                inv_ref[i] = inv[p].astype(BF16)
                arb_ref[i] = a_rb[p].astype(BF16)

    for c in range(n_chunks):
        rows = slice(c * CHUNK, (c + 1) * CHUNK)
        it = [c * N_PAIRS + p for p in pairs]
        H = [H_ref[p] for p in pairs]
        xy0 = [_dot(atrt_ref[it[p]], H[p]) for p in pairs]
        x = [xy0[p][:CHUNK] + avk_ref[it[p]] for p in pairs]
        u = [_dot(inv_ref[it[p]], _blockdiag2(x[p], m0, m1)) for p in pairs]
        yb = [_dot(arb_ref[it[p]], _blockdiag2(u[p], m0, m1)) for p in pairs]
        bu = [_dot(btT_ref[it[p]], u[p]) for p in pairs]
        for p in pairs:
            y_ref[p, rows, :] = xy0[p][CHUNK:] + yk_ref[it[p]] + yb[p]
            H_ref[p] = H[p] * wlT_ref[it[p]] + bu[p] * same_head + kvT_ref[it[p]]

    inv_n = 1.0 / RWKV_HEAD
    for c in range(n_chunks):
        rows = slice(c * CHUNK, (c + 1) * CHUNK)
        y = [y_ref[p, rows, :] for p in pairs]
        yc = [y[p] - _half_lane_sums(y[p], m0, lo) * inv_n for p in pairs]
        var = [_half_lane_sums(yc[p] * yc[p], m0, lo) * inv_n for p in pairs]
        for p in pairs:
            yn = yc[p] * lax.rsqrt(var[p] + RWKV_LN_EPS) * lng_ref[:, ps[p]] + lnb_ref[:, ps[p]]
            out_ref[0, rows, ps[p]] = ((yn + bo_ref[p, rows, :]) * gs_ref[p, rows, :]).astype(out_ref.dtype)


def _rwkv_call(proj, mu, w0, w2, a0, a2, g2, k_k, k_a, r_k, ln_g, ln_b, *, col0, tb=256):
    B, T, _ = proj.shape
    j0 = col0 // D_GROUP
    l0 = (col0 + 3 * D_GROUP) // LANES
    row = lambda a: a.reshape(1, -1).astype(F32)
    mul = jnp.zeros((1, RWKV_LOW), F32).at[0, :288].set(mu[3 * D_GROUP:])
    w2p = jnp.zeros((LANES, D_GROUP), F32).at[:64].set(w2)
    a2p = jnp.zeros((LANES, D_GROUP), F32).at[64:].set(a2)
    g2p = jnp.zeros((2 * LANES, D_GROUP), F32).at[:160].set(g2)
    colblk = lambda j: pl.BlockSpec((1, tb, D_GROUP), lambda b, t: (b, t, j0 + j))
    lowblk = lambda j: pl.BlockSpec((1, tb, LANES), lambda b, t: (b, t, l0 + j))
    hrow = lambda t: jnp.maximum(t * (tb // 8) - 1, 0)
    halo = lambda j: pl.BlockSpec((1, 8, D_GROUP), lambda b, t: (b, hrow(t), j0 + j))
    lowhalo = lambda j: pl.BlockSpec((1, 8, LANES), lambda b, t: (b, hrow(t), l0 + j))
    full = lambda a: pl.BlockSpec(a.shape, lambda b, t: (0,) * a.ndim)
    params = [row(mu[:3 * D_GROUP]), mul, row(w0), w2p, row(a0), a2p, g2p, row(k_k), row(k_a), row(r_k),
              row(ln_g), row(ln_b)]
    big = pltpu.VMEM((N_PAIRS, tb, LANES), F32)
    n_items = tb // CHUNK * N_PAIRS
    return pl.pallas_call(
        functools.partial(_rwkv_kernel, tb=tb),
        grid=(B, T // tb),
        in_specs=[colblk(0), colblk(1), colblk(2), lowblk(0), lowblk(1), lowblk(2),
                  halo(0), halo(1), halo(2), lowhalo(0), lowhalo(1), lowhalo(2)] + [full(a) for a in params],
        out_specs=pl.BlockSpec((1, tb, D_GROUP), lambda b, t: (b, t, 0)),
        out_shape=jax.ShapeDtypeStruct((B, T, D_GROUP), BF16),
        scratch_shapes=[big] * 3 + [
            pltpu.VMEM((n_items, 2 * CHUNK, LANES), BF16),
            pltpu.VMEM((n_items, CHUNK, LANES), F32),
            pltpu.VMEM((n_items, CHUNK, LANES), F32),
            pltpu.VMEM((n_items, CHUNK, LANES), BF16),
            pltpu.VMEM((n_items, CHUNK, LANES), BF16),
            pltpu.VMEM((n_items, LANES, CHUNK), BF16),
            pltpu.VMEM((n_items, LANES, LANES), F32),
            pltpu.VMEM((n_items, LANES, LANES), F32),
            pltpu.VMEM((N_PAIRS, LANES, LANES), F32)],
        compiler_params=pltpu.CompilerParams(dimension_semantics=("arbitrary", "arbitrary"),
                                             vmem_limit_bytes=VMEM_LIMIT),
        name="rwkv7",
    )(*([proj] * 12), *params)


def _ada_kernel(c_ref, w_ref, b_ref, out_ref):
    sc = _silu(c_ref[...]).astype(BF16)
    out_ref[0] = jnp.dot(sc, w_ref[0].astype(BF16), preferred_element_type=F32) + b_ref[0]


def _ada_call(c, ada_w, ada_b, *, tn=1536):
    B = c.shape[0]
    n_mod = ada_w.shape[0] * ada_w.shape[1]
    w = ada_w.reshape(n_mod, D_MODEL, 3 * D_MODEL)
    b = ada_b.reshape(n_mod, 1, 3 * D_MODEL)
    return pl.pallas_call(
        _ada_kernel,
        grid=(n_mod, 3 * D_MODEL // tn),
        in_specs=[pl.BlockSpec((B, D_MODEL), lambda i, j: (0, 0)),
                  pl.BlockSpec((1, D_MODEL, tn), lambda i, j: (i, 0, j)),
                  pl.BlockSpec((1, 1, tn), lambda i, j: (i, 0, j))],
        out_specs=pl.BlockSpec((1, B, tn), lambda i, j: (i, 0, j)),
        out_shape=jax.ShapeDtypeStruct((n_mod, B, 3 * D_MODEL), F32),
        compiler_params=pltpu.CompilerParams(dimension_semantics=("arbitrary", "arbitrary"),
                                             vmem_limit_bytes=VMEM_LIMIT),
        name="adaln",
    )(c, w, b)


def _mod_spec(i, part, nb):
    return pl.BlockSpec((1, 1, D_MODEL), lambda b, t: (i * nb + b, 0, part))


def _modulate_kernel(x_ref, shift_ref, scale_ref, h_ref):
    h_ref[0] = (x_ref[0] * (1.0 + scale_ref[0]) + shift_ref[0]).astype(h_ref.dtype)


def _modulate_call(x, mods3, i, *, tb=512):
    B, T, _ = x.shape
    blk = pl.BlockSpec((1, tb, D_MODEL), lambda b, t: (b, t, 0))
    return pl.pallas_call(
        _modulate_kernel,
        grid=(B, T // tb),
        in_specs=[blk, _mod_spec(i, 0, B), _mod_spec(i, 1, B)],
        out_specs=blk,
        out_shape=jax.ShapeDtypeStruct(x.shape, BF16),
        compiler_params=pltpu.CompilerParams(dimension_semantics=("arbitrary", "arbitrary"),
                                             vmem_limit_bytes=VMEM_LIMIT),
        name="modulate",
    )(x, mods3, mods3)


LN_ROWS = 16


def _proj_ln_kernel(*refs, n_lhs, nk, tm, with_next, x_scaled, scale_out):
    lhs = refs[:n_lhs]
    ws = refs[n_lhs:2 * n_lhs]
    x_ref, gate_ref, g_ref, b_ref = refs[2 * n_lhs:2 * n_lhs + 4]
    if with_next:
        shift_ref, scale_ref, xo_ref, h_ref, acc_ref = refs[2 * n_lhs + 4:]
    else:
        xo_ref, acc_ref = refs[2 * n_lhs + 4:]
    k = pl.program_id(2)

    def partial_product():
        part = jnp.dot(lhs[0][0], ws[0][...], preferred_element_type=F32)
        for j in range(1, n_lhs):
            part = part + jnp.dot(lhs[j][0], ws[j][...], preferred_element_type=F32)
        return part

    @pl.when(k == 0)
    def _():
        acc_ref[...] = partial_product()

    @pl.when(k > 0)
    def _():
        acc_ref[...] += partial_product()

    @pl.when(k == nk - 1)
    def _():
        gate1 = 1.0 + gate_ref[0]
        g, b = g_ref[...], b_ref[...]
        out_scale = ALPHA if scale_out else 1.0
        gx, bx = out_scale * g, out_scale * b
        if with_next:
            scale1 = 1.0 + scale_ref[0]
            gh, bh = g * scale1, b * scale1 + shift_ref[0]

        def rows_body(r, carry):
            rows = pl.ds(pl.multiple_of(r * LN_ROWS, LN_ROWS), LN_ROWS)
            xr = x_ref[0, rows, :]
            z = (xr if x_scaled else ALPHA * xr) + gate1 * acc_ref[rows, :]
            zc = z - jnp.mean(z, axis=-1, keepdims=True)
            var = jnp.mean(zc * zc, axis=-1, keepdims=True)
            t = zc * lax.rsqrt(var + LN_EPS)
            xo_ref[0, rows, :] = t * gx + bx
            if with_next:
                h_ref[0, rows, :] = (t * gh + bh).astype(h_ref.dtype)
            return carry

        lax.fori_loop(0, tm // LN_ROWS, rows_body, 0, unroll=16)


def _proj_ln_call(lhs, ws, x, mods3, i, g, b, *, with_next, x_scaled, tm, tk, vmem_limit=VMEM_LIMIT):
    B, T, D = x.shape
    n_lhs = len(lhs)
    nk = lhs[0].shape[2] // tk
    blk = pl.BlockSpec((1, tm, D), lambda b, t, k: (b, t, 0))
    row = pl.BlockSpec((1, D), lambda b, t, k: (0, 0))
    mod = lambda ii, part: pl.BlockSpec((1, 1, D), lambda b, t, k: (ii * B + b, 0, part))
    wspec = lambda idx, k0: pl.BlockSpec((None, tk, D), lambda b, t, k: (idx, k0 + k, 0))
    in_specs = ([pl.BlockSpec((1, tm, tk), lambda b, t, k: (b, t, k))] * n_lhs
                + [wspec(idx, k0) for _, idx, k0 in ws]
                + [blk, mod(i, 2), row, row])
    args = list(lhs) + [w for w, _, _ in ws] + [x, mods3, g.reshape(1, D), b.reshape(1, D)]
    out_specs = [blk]
    out_shape = [jax.ShapeDtypeStruct(x.shape, F32)]
    if with_next:
        in_specs += [mod(i + 1, 0), mod(i + 1, 1)]
        args += [mods3, mods3]
        out_specs.append(blk)
        out_shape.append(jax.ShapeDtypeStruct(x.shape, BF16))
    return pl.pallas_call(
        functools.partial(_proj_ln_kernel, n_lhs=n_lhs, nk=nk, tm=tm, with_next=with_next, x_scaled=x_scaled,
                          scale_out=with_next),
        grid=(B, T // tm, nk),
        in_specs=in_specs, out_specs=out_specs, out_shape=out_shape,
        scratch_shapes=[pltpu.VMEM((tm, D), F32)],
        compiler_params=pltpu.CompilerParams(dimension_semantics=("arbitrary", "arbitrary", "arbitrary"),
                                             vmem_limit_bytes=vmem_limit),
        name="proj_residual_ln",
    )(*args)


def _matmul_kernel(a_ref, b_ref, o_ref, *scratch, nk):
    if nk == 1:
        o_ref[...] = jnp.dot(a_ref[...], b_ref[...], preferred_element_type=F32).astype(o_ref.dtype)
        return
    (acc_ref,) = scratch
    k = pl.program_id(2)

    @pl.when(k == 0)
    def _():
        acc_ref[...] = jnp.zeros_like(acc_ref)

    acc_ref[...] += jnp.dot(a_ref[...], b_ref[...], preferred_element_type=F32)

    @pl.when(k == nk - 1)
    def _():
        o_ref[...] = acc_ref[...].astype(o_ref.dtype)


def _matmul_call(a, b, *, tm, tn, tk, out_dtype=F32):
    M, K = a.shape
    _, N = b.shape
    nk = K // tk
    return pl.pallas_call(
        functools.partial(_matmul_kernel, nk=nk),
        grid=(N // tn, M // tm, nk),
        in_specs=[pl.BlockSpec((tm, tk), lambda j, i, k: (i, k)),
                  pl.BlockSpec((tk, tn), lambda j, i, k: (k, j))],
        out_specs=pl.BlockSpec((tm, tn), lambda j, i, k: (i, j)),
        out_shape=jax.ShapeDtypeStruct((M, N), out_dtype),
        scratch_shapes=[] if nk == 1 else [pltpu.VMEM((tm, tn), F32)],
        compiler_params=pltpu.CompilerParams(dimension_semantics=("arbitrary", "arbitrary", "arbitrary"),
                                             vmem_limit_bytes=VMEM_LIMIT),
        name="matmul",
    )(a, b)


XPOSE_ROWS = 256


def _matmul_f32wt_kernel(a_ref, wt_ref, o_ref, wb_ref):
    @pl.when(pl.program_id(1) == 0)
    def _():
        for r in range(0, wt_ref.shape[0], XPOSE_ROWS):
            wb_ref[:, r:r + XPOSE_ROWS] = wt_ref[r:r + XPOSE_ROWS, :].T.astype(BF16)

    o_ref[...] = jnp.dot(a_ref[...], wb_ref[...], preferred_element_type=F32)


def _matmul_f32wt_call(a, wt, idx, ncols, *, tm=1024, tn=1024):
    M, K = a.shape
    tm = min(tm, M)
    return pl.pallas_call(
        _matmul_f32wt_kernel,
        grid=(ncols // tn, M // tm),
        in_specs=[pl.BlockSpec((tm, K), lambda j, i: (i, 0)),
                  pl.BlockSpec((None, tn, K), lambda j, i: (idx, j, 0))],
        out_specs=pl.BlockSpec((tm, tn), lambda j, i: (i, j)),
        out_shape=jax.ShapeDtypeStruct((M, ncols), F32),
        scratch_shapes=[pltpu.VMEM((K, tn), BF16)],
        compiler_params=pltpu.CompilerParams(dimension_semantics=("arbitrary", "arbitrary"),
                                             vmem_limit_bytes=VMEM_LIMIT),
        name="matmul_f32wt",
    )(a, wt)


def _regroup_rows_kernel(wt_ref, o_ref, *, segments):
    tc = wt_ref.shape[1]
    pieces, pos = [], 0
    for src, width, dst in segments:
        if dst > pos:
            pieces.append(jnp.zeros((dst - pos, tc), F32))
        pieces.append(wt_ref[src:src + width, :])
        pos = dst + width
    if pos < o_ref.shape[1]:
        pieces.append(jnp.zeros((o_ref.shape[1] - pos, tc), F32))
    stacked = jnp.concatenate(pieces, axis=0)
    for r in range(0, o_ref.shape[1], LANES):
        o_ref[:, r:r + LANES] = stacked[r:r + LANES, :].T.astype(o_ref.dtype)


def _regroup_rows_call(wt, idx, segments, out_cols, *, block_rows, block_index, tc):
    _, _, K = wt.shape
    return pl.pallas_call(
        functools.partial(_regroup_rows_kernel, segments=segments),
        grid=(K // tc,),
        in_specs=[pl.BlockSpec((None, block_rows, tc), lambda c: (idx, block_index, c))],
        out_specs=pl.BlockSpec((tc, out_cols), lambda c: (c, 0)),
        out_shape=jax.ShapeDtypeStruct((K, out_cols), BF16),
        compiler_params=pltpu.CompilerParams(dimension_semantics=("arbitrary",), vmem_limit_bytes=VMEM_LIMIT),
        name="regroup_rows",
    )(wt)


def _swiglu_up_kernel(h_ref, wg_ref, wu_ref, o_ref, wgb_ref, wub_ref):
    @pl.when(pl.program_id(1) == 0)
    def _():
        wgb_ref[...] = wg_ref[...].astype(BF16)
        wub_ref[...] = wu_ref[...].astype(BF16)

    h = h_ref[...]
    g = jnp.dot(h, wgb_ref[...], preferred_element_type=F32)
    u = jnp.dot(h, wub_ref[...], preferred_element_type=F32)
    o_ref[...] = (_silu(g) * u).astype(o_ref.dtype)


def _swiglu_up_call(h, wg, wu, layer, *, tm=1024, tn=512):
    M, K = h.shape
    _, _, N = wg.shape
    tm = min(tm, M)
    wspec = pl.BlockSpec((None, K, tn), lambda j, i: (layer, 0, j))
    return pl.pallas_call(
        _swiglu_up_kernel,
        grid=(N // tn, M // tm),
        in_specs=[pl.BlockSpec((tm, K), lambda j, i: (i, 0)), wspec, wspec],
        out_specs=pl.BlockSpec((tm, tn), lambda j, i: (i, j)),
        out_shape=jax.ShapeDtypeStruct((M, N), BF16),
        scratch_shapes=[pltpu.VMEM((K, tn), BF16), pltpu.VMEM((K, tn), BF16)],
        compiler_params=pltpu.CompilerParams(dimension_semantics=("arbitrary", "arbitrary"),
                                             vmem_limit_bytes=VMEM_LIMIT),
        name="swiglu_up",
    )(h, wg, wu)


A_MAIN = 4 * D_GROUP
B_COLS_PAD = 3 * D_GROUP + RWKV_LOW + LANES
MLSTM_GATE_COL = 3 * D_GROUP + RWKV_LOW
CD_MAIN = 8 * D_GROUP
GDN_COL0 = 4 * D_GROUP


N_GATES = 2 * N_HEADS
B_SRC = A_MAIN + N_GATES
B_LOW = 64 + 64 + 160
B_SEGMENTS = ((B_SRC, 3 * D_GROUP, 0), (B_SRC + 3 * D_GROUP, B_LOW, 3 * D_GROUP), (A_MAIN, N_GATES, MLSTM_GATE_COL))
CD_GATE_SEGMENTS = ((0, N_GATES, 0),)


def kernel(x, c, positions, ada_w, ada_b, ln_g, ln_b, ab_w_in, ab_w_out, mlstm_conv_w, mlstm_gate_b, mlstm_norm_g, rwkv_mu, rwkv_w0, rwkv_w2, rwkv_a0, rwkv_a2, rwkv_g2, rwkv_k_k, rwkv_k_a, rwkv_r_k, rwkv_ln_g, rwkv_ln_b, cd_w_in, cd_w_out, ret_norm_g, gdn_conv_w, gdn_a_log, gdn_dt_bias, gdn_norm_g, ffn_w_gate, ffn_w_up, ffn_w_down):
    B, T, D = x.shape
    M = B * T
    depth = ada_w.shape[0]
    mods = _ada_call(c, ada_w, ada_b)
    mods3 = mods.reshape(2 * depth * B, 1, 3 * D)
    h = _modulate_call(x, mods3, 0)
    w_down = ffn_w_down.astype(BF16)
    for layer in range(depth):
        j = layer // 2
        i_mix, i_ffn = 2 * layer, 2 * layer + 1
        if layer % 2 == 0:
            h2 = h.reshape(M, D)
            wt = jnp.swapaxes(ab_w_in, 1, 2)
            proj_a = _matmul_f32wt_call(h2, wt, j, A_MAIN).reshape(B, T, A_MAIN)
            w_b = _regroup_rows_call(wt, j, B_SEGMENTS, B_COLS_PAD, block_rows=wt.shape[1], block_index=0, tc=256)
            proj_b = _matmul_call(h2, w_b, tm=min(1024, M), tn=B_COLS_PAD // 2, tk=D)
            proj_b = proj_b.reshape(B, T, B_COLS_PAD)
            ya = _mlstm_call(proj_a, proj_b, mlstm_conv_w[j], mlstm_gate_b[j], mlstm_norm_g[j],
                             gate_col=MLSTM_GATE_COL)
            yb = _rwkv_call(proj_b, rwkv_mu[j], rwkv_w0[j], rwkv_w2[j], rwkv_a0[j], rwkv_a2[j], rwkv_g2[j],
                            rwkv_k_k[j], rwkv_k_a[j], rwkv_r_k[j], rwkv_ln_g[j], rwkv_ln_b[j], col0=0)
            w_out = ab_w_out
        else:
            h2 = h.reshape(M, D)
            wt = jnp.swapaxes(cd_w_in, 1, 2)
            proj = _matmul_f32wt_call(h2, wt, j, CD_MAIN).reshape(B, T, CD_MAIN)
            w_gates = _regroup_rows_call(wt, j, CD_GATE_SEGMENTS, LANES, block_rows=N_GATES,
                                         block_index=CD_MAIN // N_GATES, tc=D)
            proj_g = _matmul_call(h2, w_gates, tm=512, tn=LANES, tk=D).reshape(B, T, LANES)
            ya = _ret_call(proj, positions, ret_norm_g[j])
            yb = _gdn_call(proj, proj_g, gdn_conv_w[j], gdn_a_log[j], gdn_dt_bias[j], gdn_norm_g[j],
                           col0=GDN_COL0, gate_col=0)
            w_out = cd_w_out
        w_out = w_out.astype(BF16)
        x, h = _proj_ln_call([ya, yb], [(w_out, j, 0), (w_out, j, 1)], x, mods3, i_mix,
                             ln_g[layer, 0], ln_b[layer, 0], with_next=True, x_scaled=layer > 0, tm=512,
                             tk=D_GROUP)
        act = _swiglu_up_call(h.reshape(M, D), ffn_w_gate, ffn_w_up, layer)
        last = layer == depth - 1
        res = _proj_ln_call([act.reshape(B, T, D_FF)], [(w_down, layer, 0)], x, mods3, i_ffn,
                            ln_g[layer, 1], ln_b[layer, 1], with_next=not last, x_scaled=True, tm=512,
                            tk=D_FF // 2, vmem_limit=VMEM_LIMIT_DOWN)
        if last:
            (x,) = res
        else:
            x, h = res
    return x
```

```python
import functools
import math

import jax
import jax.numpy as jnp
from jax import lax
from jax.experimental import pallas as pl
from jax.experimental.pallas import tpu as pltpu

F32 = jnp.float32
BF16 = jnp.bfloat16

D_MODEL = 2048
D_GROUP = 1024
HEAD_DIM = 128
N_HEADS = 8
RWKV_HEAD = 64
N_RWKV = 16
CHUNK = 64
D_FF = 5632
DEPTH = 2
ALPHA = (2 * DEPTH) ** 0.25
LN_EPS = 1e-5
RWKV_LN_EPS = 64e-5
ROPE_BASE = 10000.0
RET_GAMMA_BASE = 5.0
LANES = 128
VMEM_LIMIT = 48 * 1024 * 1024
VMEM_LIMIT_DOWN = 58 * 1024 * 1024


def _dot(a, b):
    return jnp.dot(a.astype(BF16), b.astype(BF16), preferred_element_type=F32)


def _dot_nt(a, b):
    return lax.dot_general(a.astype(BF16), b.astype(BF16), (((1,), (1,)), ((), ())),
                           preferred_element_type=F32)


def _split3(x):
    hi = x.astype(BF16)
    r1 = x - hi.astype(F32)
    mid = r1.astype(BF16)
    lo = (r1 - mid.astype(F32)).astype(BF16)
    return hi, mid, lo


def _cumsum_rows(x):
    n = x.shape[1]
    out = jnp.dot(_tri(CHUNK).astype(BF16), jnp.concatenate(_split3(x), axis=1), preferred_element_type=F32)
    return out[:, :n] + out[:, n:2 * n] + out[:, 2 * n:]


def _half_lane_sums(x, m0, lo):
    s0 = jnp.sum(x * m0, axis=-1, keepdims=True)
    s1 = jnp.sum(x * (1.0 - m0), axis=-1, keepdims=True)
    return jnp.where(lo, s0, s1)


def _sigmoid(x):
    return 1.0 / (1.0 + jnp.exp(-x))


def _silu(x):
    return x * _sigmoid(x)


def _log_sigmoid(x):
    return jnp.minimum(x, 0.0) - jnp.log1p(jnp.exp(-jnp.abs(x)))


def _softplus(x):
    return jnp.maximum(x, 0.0) + jnp.log1p(jnp.exp(-jnp.abs(x)))


def _tri(n, strict=False):
    r = lax.broadcasted_iota(jnp.int32, (n, n), 0)
    c = lax.broadcasted_iota(jnp.int32, (n, n), 1)
    return (r > c) if strict else (r >= c)


def _conv_silu_rows(src_ref, halo_ref, w, first, r0, nrows, c0, ncols):
    cur = src_ref[0, r0:r0 + nrows, c0:c0 + ncols]
    acc = w[3:4] * cur
    if r0 == 0:
        hl = jnp.where(first, 0.0, halo_ref[0, :, c0:c0 + ncols])
        ext = jnp.concatenate([hl, cur[0:8]], axis=0)
        for j in range(3):
            head = ext[5 + j:13 + j]
            if nrows > 8:
                rest = src_ref[0, 5 + j:nrows - 3 + j, c0:c0 + ncols]
                sh = jnp.concatenate([head, rest], axis=0)
            else:
                sh = head
            acc = acc + w[j:j + 1] * sh
    else:
        for j in range(3):
            acc = acc + w[j:j + 1] * src_ref[0, r0 - 3 + j:r0 - 3 + j + nrows, c0:c0 + ncols]
    return _silu(acc)


def _head_norm_rows(h, g_row, eps, center=True):
    if center:
        h = h - jnp.mean(h, axis=-1, keepdims=True)
    return h * lax.rsqrt(jnp.mean(h * h, axis=-1, keepdims=True) + eps) * g_row


MLSTM_CHUNK_GROUP = 4


def _cummax_rows(x):
    row = lax.broadcasted_iota(jnp.int32, x.shape, 0)
    d = 1
    while d < x.shape[0]:
        x = jnp.where(row >= d, jnp.maximum(x, pltpu.roll(x, d, 0)), x)
        d *= 2
    return x


def _bcast_head_cols(x, sel, pieces):
    m = x.shape[0]
    parts, rest = [], x
    for _ in range(pieces):
        hi = rest.astype(BF16)
        parts.append(hi)
        rest = rest - hi.astype(F32)
    out = jnp.dot(jnp.concatenate(parts, axis=0), sel, preferred_element_type=F32)
    acc = out[:m]
    for i in range(1, pieces):
        acc = acc + out[i * m:(i + 1) * m]
    return acc


def _mlstm_kernel(q_ref, k_ref, v_ref, o_ref, g_ref, qh_ref, kh_ref, cw_ref, gb_ref, ng_ref,
                  out_ref, qc_ref, kc_ref, sv_ref, rs_ref, hh_ref, b0_ref, cm_ref, kvn_ref, CN_ref, m_ref, *, tb):
    t = pl.program_id(1)
    first = t == 0

    @pl.when(first)
    def _():
        CN_ref[...] = jnp.zeros_like(CN_ref)
        m_ref[...] = jnp.zeros_like(m_ref)

    for c in range(tb // CHUNK):
        for cb in range(D_GROUP // 256):
            cs = cb * 256
            qc_ref[c * CHUNK:(c + 1) * CHUNK, cs:cs + 256] = _conv_silu_rows(
                q_ref, qh_ref, cw_ref[:, cs:cs + 256], first, c * CHUNK, CHUNK, cs, 256)
            kc_ref[c * CHUNK:(c + 1) * CHUNK, cs:cs + 256] = _conv_silu_rows(
                k_ref, kh_ref, cw_ref[:, D_GROUP + cs:D_GROUP + cs + 256], first, c * CHUNK, CHUNK, cs, 256)

    causal = _tri(CHUNK)
    gb = gb_ref[...]
    scale = HEAD_DIM ** -0.5
    n_chunks = tb // CHUNK
    heads = range(N_HEADS)
    hs = [slice(h * HEAD_DIM, (h + 1) * HEAD_DIM) for h in heads]
    head_lane = lax.broadcasted_iota(jnp.int32, (1, LANES), 1) < N_HEADS
    sel = (lax.broadcasted_iota(jnp.int32, (LANES, N_HEADS * LANES), 0)
           == (lax.broadcasted_iota(jnp.int32, (LANES, N_HEADS * LANES), 1) >> 7)).astype(BF16)
    ones = jnp.ones((CHUNK, HEAD_DIM), F32)
    last = slice(CHUNK - 1, CHUNK)

    for c0 in range(0, n_chunks, MLSTM_CHUNK_GROUP):
        chunks = range(c0, c0 + MLSTM_CHUNK_GROUP)
        rows = {c: slice(c * CHUNK, (c + 1) * CHUNK) for c in chunks}
        z = {c: g_ref[0, rows[c], :] + gb for c in chunks}
        b0 = {c: pltpu.roll(_cumsum_rows(_log_sigmoid(z[c])), LANES - N_HEADS, 1) for c in chunks}
        cv = {c: jnp.where(head_lane, z[c] - b0[c], 0.0) for c in chunks}
        cm = {c: _cummax_rows(cv[c]) for c in chunks}
        cT = {c: cv[c].T for c in chunks}
        cmb = {c: _bcast_head_cols(cm[c], sel, 3) for c in chunks}
        e1b = {c: _bcast_head_cols(jnp.where(head_lane, jnp.exp(cv[c] - cm[c][last]), 0.0), sel, 2) for c in chunks}
        for c in chunks:
            b0_ref[rows[c], :] = b0[c]
            cm_ref[rows[c], :] = cm[c]
        items = [(c, h) for c in chunks for h in heads]
        n_items = range(len(items))
        q = [qc_ref[rows[c], hs[h]] for c, h in items]
        k = [kc_ref[rows[c], hs[h]] * scale for c, h in items]
        vo = [jnp.concatenate([v_ref[0, rows[c], hs[h]], ones], axis=1).astype(BF16) for c, h in items]
        qk = [_dot_nt(q[i], k[i]) for i in n_items]
        s = [qk[i] * jnp.where(causal, jnp.exp(jnp.minimum(cT[c][h:h + 1, :] - cmb[c][:, h * LANES:h * LANES + CHUNK],
                                                           0.0)), 0.0) for i, (c, h) in enumerate(items)]
        s_hi = [s[i].astype(BF16) for i in n_items]
        s_lo = [(s[i] - s_hi[i].astype(F32)).astype(BF16) for i in n_items]
        svr = [jnp.dot(jnp.concatenate([s_hi[i], s_lo[i]], axis=0), vo[i], preferred_element_type=F32)
               for i in n_items]
        kvn = [_dot((k[i] * e1b[c][:, hs[h]]).T, vo[i]) for i, (c, h) in enumerate(items)]
        for i, (c, h) in enumerate(items):
            sv_ref[rows[c], hs[h]] = svr[i][:CHUNK, :HEAD_DIM] + svr[i][CHUNK:, :HEAD_DIM]
            rs_ref[rows[c], hs[h]] = svr[i][:CHUNK, HEAD_DIM:] + svr[i][CHUNK:, HEAD_DIM:]
            kvn_ref[c * N_HEADS + h] = kvn[i]

    for c in range(n_chunks):
        rows = slice(c * CHUNK, (c + 1) * CHUNK)
        m = m_ref[0:1, :]
        b0 = b0_ref[rows, :]
        cm = cm_ref[rows, :]
        mx = jnp.maximum(cm, m)
        m_new = jnp.maximum(b0[last] + m, b0[last] + cm[last])
        m_ref[0:1, :] = m_new
        zero = lambda x: jnp.where(head_lane, x, 0.0)
        fib = _bcast_head_cols(zero(jnp.exp(cm - mx)), sel, 2)
        scb = _bcast_head_cols(zero(jnp.exp(m - mx)), sel, 2)
        emtb = _bcast_head_cols(zero(jnp.exp(jnp.minimum(-(b0 + mx), 80.0))), sel, 2)
        dfb = _bcast_head_cols(jnp.concatenate([zero(jnp.exp(b0[last] + m - m_new)),
                                                zero(jnp.exp(b0[last] + cm[last] - m_new)),
                                                jnp.zeros((6, LANES), F32)], axis=0), sel, 2)
        CN = [CN_ref[h] for h in heads]
        qcn = [_dot(qc_ref[rows, hs[h]], CN[h]) for h in heads]
        for h in heads:
            num = fib[:, hs[h]] * sv_ref[rows, hs[h]] + scb[:, hs[h]] * qcn[h][:, :HEAD_DIM]
            den = fib[:, hs[h]] * rs_ref[rows, hs[h]] + scb[:, hs[h]] * qcn[h][:, HEAD_DIM:]
            hh_ref[rows, hs[h]] = num / jnp.maximum(jnp.abs(den), emtb[:, hs[h]])
            dec = jnp.concatenate([dfb[0:1, hs[h]]] * 2, axis=1)
            fkv = jnp.concatenate([dfb[1:2, hs[h]]] * 2, axis=1)
            CN_ref[h] = CN[h] * dec + kvn_ref[c * N_HEADS + h] * fkv

    for c in range(n_chunks):
        rows = slice(c * CHUNK, (c + 1) * CHUNK)
        hn = [_head_norm_rows(hh_ref[rows, hs[h]], ng_ref[:, hs[h]], LN_EPS) for h in heads]
        for h in heads:
            out_ref[0, rows, hs[h]] = (hn[h] * _sigmoid(o_ref[0, rows, hs[h]])).astype(out_ref.dtype)


def _mlstm_call(proj, gproj, conv_w, gate_b, norm_g, *, gate_col, tb=256):
    B, T, _ = proj.shape
    nt = T // tb
    gb = jnp.zeros((1, LANES), F32).at[0, :2 * N_HEADS].set(gate_b)
    ng = norm_g.reshape(1, D_GROUP)
    colblk = lambda j: pl.BlockSpec((1, tb, D_GROUP), lambda b, t: (b, t, j))
    halo = lambda j: pl.BlockSpec((1, 8, D_GROUP), lambda b, t: (b, jnp.maximum(t * (tb // 8) - 1, 0), j))
    return pl.pallas_call(
        functools.partial(_mlstm_kernel, tb=tb),
        grid=(B, nt),
        in_specs=[colblk(0), colblk(1), colblk(2), colblk(3),
                  pl.BlockSpec((1, tb, LANES), lambda b, t: (b, t, gate_col // LANES)),
                  halo(0), halo(1),
                  pl.BlockSpec((4, 2 * D_GROUP), lambda b, t: (0, 0)),
                  pl.BlockSpec((1, LANES), lambda b, t: (0, 0)),
                  pl.BlockSpec((1, D_GROUP), lambda b, t: (0, 0))],
        out_specs=pl.BlockSpec((1, tb, D_GROUP), lambda b, t: (b, t, 0)),
        out_shape=jax.ShapeDtypeStruct((B, T, D_GROUP), BF16),
        scratch_shapes=[pltpu.VMEM((tb, D_GROUP), F32)] * 5
                       + [pltpu.VMEM((tb, LANES), F32)] * 2
                       + [pltpu.VMEM((tb // CHUNK * N_HEADS, HEAD_DIM, 2 * HEAD_DIM), F32),
                          pltpu.VMEM((N_HEADS, HEAD_DIM, 2 * HEAD_DIM), F32),
                          pltpu.VMEM((8, LANES), F32)],
        compiler_params=pltpu.CompilerParams(dimension_semantics=("arbitrary", "arbitrary"),
                                             vmem_limit_bytes=VMEM_LIMIT),
        name="mlstm",
    )(proj, proj, proj, proj, gproj, proj, proj, conv_w, gb, ng)


RET_CHUNK_GROUP = 1


def _ret_kernel(q_ref, k_ref, v_ref, g_ref, pos_ref, inv_ref, ng_ref, out_ref, qr_ref, o_ref, kv_ref, R_ref, *, tb):
    t = pl.program_id(1)

    @pl.when(t == 0)
    def _():
        R_ref[...] = jnp.zeros_like(R_ref)

    n_chunks = tb // CHUNK
    heads = range(N_HEADS)
    hs = [slice(h * HEAD_DIM, (h + 1) * HEAD_DIM) for h in heads]
    lg = [math.log1p(-2.0 ** (-RET_GAMMA_BASE - h)) for h in heads]
    causal = _tri(CHUNK)
    ri = lax.broadcasted_iota(jnp.int32, (CHUNK, CHUNK), 0)
    ci = lax.broadcasted_iota(jnp.int32, (CHUNK, CHUNK), 1)
    rel = (ri - ci).astype(F32)
    tcol = lax.broadcasted_iota(jnp.int32, (CHUNK, 1), 0).astype(F32)
    scale = HEAD_DIM ** -0.5
    lane = lax.broadcasted_iota(jnp.int32, (1, HEAD_DIM), 1)
    sign = jnp.where(lane < HEAD_DIM // 2, -1.0, 1.0)
    dmat = [jnp.where(causal, jnp.exp(rel * lg[h]), 0.0) for h in heads]

    for c0 in range(0, n_chunks, RET_CHUNK_GROUP):
        chunks = range(c0, c0 + RET_CHUNK_GROUP)
        rows = {c: slice(c * CHUNK, (c + 1) * CHUNK) for c in chunks}
        ang = {c: pos_ref[0, rows[c], :].astype(F32) * inv_ref[...] for c in chunks}
        cos2 = {c: jnp.cos(ang[c]) for c in chunks}
        sin2 = {c: jnp.sin(ang[c]) * sign for c in chunks}
        items = [(c, h) for c in chunks for h in heads]
        n_items = range(len(items))
        rot = lambda z, c: z * cos2[c] + pltpu.roll(z, HEAD_DIM // 2, 1) * sin2[c]
        qr = [rot(q_ref[0, rows[c], hs[h]], c) for c, h in items]
        kr = [rot(k_ref[0, rows[c], hs[h]], c) * scale for c, h in items]
        v = [v_ref[0, rows[c], hs[h]] for c, h in items]
        qk = [_dot_nt(qr[i], kr[i]) * dmat[h] for i, (c, h) in enumerate(items)]
        intra = [_dot(qk[i], v[i]) for i in n_items]
        kv = [_dot((kr[i] * jnp.exp((CHUNK - 1.0 - tcol) * lg[h])).T, v[i]) for i, (c, h) in enumerate(items)]
        for i, (c, h) in enumerate(items):
            qr_ref[rows[c], hs[h]] = qr[i]
            o_ref[rows[c], hs[h]] = intra[i]
            kv_ref[c * N_HEADS + h] = kv[i]

    for h in heads:
        R = R_ref[h]
        for c in range(n_chunks):
            inc = kv_ref[c * N_HEADS + h]
            kv_ref[c * N_HEADS + h] = R
            R = R * math.exp(CHUNK * lg[h]) + inc
        R_ref[h] = R
    for c in range(n_chunks):
        rows = slice(c * CHUNK, (c + 1) * CHUNK)
        inter = [_dot(qr_ref[rows, hs[h]], kv_ref[c * N_HEADS + h]) * jnp.exp((tcol + 1.0) * lg[h]) for h in heads]
        on = [_head_norm_rows(o_ref[rows, hs[h]] + inter[h], ng_ref[:, hs[h]], LN_EPS) for h in heads]
        for h in heads:
            out_ref[0, rows, hs[h]] = (on[h] * _silu(g_ref[0, rows, hs[h]])).astype(out_ref.dtype)


def _ret_call(proj, positions, norm_g, *, tb=256):
    B, T, _ = proj.shape
    half = HEAD_DIM // 2
    inv_freq = ROPE_BASE ** (-jnp.arange(half, dtype=F32) / half)
    inv2 = jnp.concatenate([inv_freq, inv_freq]).reshape(1, HEAD_DIM)
    colblk = lambda j: pl.BlockSpec((1, tb, D_GROUP), lambda b, t: (b, t, j))
    big = pltpu.VMEM((tb, D_GROUP), F32)
    return pl.pallas_call(
        functools.partial(_ret_kernel, tb=tb),
        grid=(B, T // tb),
        in_specs=[colblk(0), colblk(1), colblk(2), colblk(3),
                  pl.BlockSpec((1, tb, 1), lambda b, t: (b, t, 0)),
                  pl.BlockSpec((1, HEAD_DIM), lambda b, t: (0, 0)),
                  pl.BlockSpec((1, D_GROUP), lambda b, t: (0, 0))],
        out_specs=pl.BlockSpec((1, tb, D_GROUP), lambda b, t: (b, t, 0)),
        out_shape=jax.ShapeDtypeStruct((B, T, D_GROUP), BF16),
        scratch_shapes=[big, big, pltpu.VMEM((tb // CHUNK * N_HEADS, HEAD_DIM, HEAD_DIM), F32),
                        pltpu.VMEM((N_HEADS, HEAD_DIM, HEAD_DIM), F32)],
        compiler_params=pltpu.CompilerParams(dimension_semantics=("arbitrary", "arbitrary"),
                                             vmem_limit_bytes=VMEM_LIMIT),
        name="retention",
    )(proj, proj, proj, proj, positions.reshape(B, T, 1), inv2, norm_g.reshape(1, D_GROUP))


def _inv_unit_lower(nms):
    n = nms[0].shape[0]
    eye = (lax.broadcasted_iota(jnp.int32, (n, n), 0) == lax.broadcasted_iota(jnp.int32, (n, n), 1)).astype(F32)
    ps = [eye + nm for nm in nms]
    xs = [_dot(nm, nm) for nm in nms]
    for _ in range(int(math.log2(n)) - 2):
        px = [_dot(jnp.concatenate([p, x], axis=0), x) for p, x in zip(ps, xs)]
        ps = [p + y[:n] for p, y in zip(ps, px)]
        xs = [y[n:] for y in px]
    ps = [p + _dot(p, x) for p, x in zip(ps, xs)]
    resid = [eye - p + _dot(nm, p) for p, nm in zip(ps, nms)]
    return [p + _dot(p, r) for p, r in zip(ps, resid)]


def _blockdiag2(x, m0, m1):
    xb = x.astype(BF16)
    return jnp.concatenate([xb * m0.astype(BF16), xb * m1.astype(BF16)], axis=0)


def _inv_unit_lower_packed(nms, m0, m1):
    n = nms[0].shape[0]
    r = lax.broadcasted_iota(jnp.int32, (n, 2 * n), 0)
    c = lax.broadcasted_iota(jnp.int32, (n, 2 * n), 1)
    eye2 = (r == (c & (n - 1))).astype(F32)
    bd = lambda x: _blockdiag2(x, m0, m1)
    ps = [eye2 + nm for nm in nms]
    xs = [_dot(nm, bd(nm)) for nm in nms]
    for _ in range(int(math.log2(n)) - 2):
        px = [_dot(jnp.concatenate([p, x], axis=0), bd(x)) for p, x in zip(ps, xs)]
        ps = [p + y[:n] for p, y in zip(ps, px)]
        xs = [y[n:] for y in px]
    ps = [p + _dot(p, bd(x)) for p, x in zip(ps, xs)]
    resid = [eye2 - p + _dot(nm, bd(p)) for p, nm in zip(ps, nms)]
    return [p + _dot(p, bd(r_)) for p, r_ in zip(ps, resid)]


def _l2norm_rows(z):
    return z * lax.rsqrt(jnp.sum(z * z, axis=-1, keepdims=True) + 1e-6)


GDN_CHUNK_GROUP = 2


def _gdn_kernel(q_ref, k_ref, v_ref, z_ref, g_ref, qh_ref, kh_ref, vh_ref, cw_ref, an_ref, dt_ref, ng_ref,
                out_ref, qc_ref, kc_ref, vc_ref, u_ref, w_ref, qe_ref, o_ref, att_ref, kdT_ref, gl_ref, S_ref,
                *, tb):
    t = pl.program_id(1)
    first = t == 0

    @pl.when(first)
    def _():
        S_ref[...] = jnp.zeros_like(S_ref)

    srcs = ((q_ref, qh_ref, qc_ref), (k_ref, kh_ref, kc_ref), (v_ref, vh_ref, vc_ref))
    for c in range(tb // CHUNK):
        for cb in range(D_GROUP // 256):
            cs = cb * 256
            for i, (src, halo, dst) in enumerate(srcs):
                w = cw_ref[:, i * D_GROUP + cs:i * D_GROUP + cs + 256]
                dst[c * CHUNK:(c + 1) * CHUNK, cs:cs + 256] = _conv_silu_rows(
                    src, halo, w, first, c * CHUNK, CHUNK, cs, 256)

    causal = _tri(CHUNK)
    strict = _tri(CHUNK, strict=True)
    a_neg = an_ref[...]
    dtb = dt_ref[...]
    scale = HEAD_DIM ** -0.5
    n_chunks = tb // CHUNK
    heads = range(N_HEADS)
    hs = [slice(h * HEAD_DIM, (h + 1) * HEAD_DIM) for h in heads]

    for c0 in range(0, n_chunks, GDN_CHUNK_GROUP):
        chunks = range(c0, c0 + GDN_CHUNK_GROUP)
        rows = {c: slice(c * CHUNK, (c + 1) * CHUNK) for c in chunks}
        gz = {c: g_ref[0, rows[c], :] for c in chunks}
        beta = {c: _sigmoid(gz[c]) for c in chunks}
        gc = {c: _cumsum_rows(a_neg * _softplus(gz[c] + dtb)) for c in chunks}
        gcT = {c: gc[c].T for c in chunks}
        for c in chunks:
            gl_ref[c:c + 1, :] = gc[c][CHUNK - 1:CHUNK, :]
        items = [(c, h) for c in chunks for h in heads]
        n_items = range(len(items))
        gc_col = [gc[c][:, h:h + 1] for c, h in items]
        b_col = [beta[c][:, 8 + h:9 + h] for c, h in items]
        gamma = [jnp.where(causal, jnp.exp(gc_col[i] - gcT[c][h:h + 1, :]), 0.0) for i, (c, h) in enumerate(items)]
        q = [_l2norm_rows(qc_ref[rows[c], hs[h]]) * scale for c, h in items]
        k = [_l2norm_rows(kc_ref[rows[c], hs[h]]) for c, h in items]
        kb = [k[i] * b_col[i] for i in n_items]
        eg = [jnp.exp(gc_col[i]) for i in n_items]
        kq = [_dot_nt(jnp.concatenate([kb[i], q[i]], axis=0), k[i]) for i in n_items]
        inv = _inv_unit_lower([-jnp.where(strict, kq[i][:CHUNK] * gamma[i], 0.0) for i in n_items])
        uw = [_dot(inv[i], jnp.concatenate([vc_ref[rows[c], hs[h]] * b_col[i], kb[i] * eg[i]], axis=1))
              for i, (c, h) in enumerate(items)]
        for i, (c, h) in enumerate(items):
            u_ref[rows[c], hs[h]] = uw[i][:, :HEAD_DIM]
            w_ref[rows[c], hs[h]] = uw[i][:, HEAD_DIM:]
            qe_ref[rows[c], hs[h]] = q[i] * eg[i]
            att_ref[h, rows[c], :] = kq[i][CHUNK:] * gamma[i]
            g_last = gc[c][CHUNK - 1:CHUNK, h:h + 1]
            kdT_ref[c * N_HEADS + h] = (k[i] * jnp.exp(g_last - gc_col[i])).T

    for c in range(n_chunks):
        rows = slice(c * CHUNK, (c + 1) * CHUNK)
        S = [S_ref[h] for h in heads]
        ws = [_dot(jnp.concatenate([w_ref[rows, hs[h]], qe_ref[rows, hs[h]]], axis=0), S[h]) for h in heads]
        v_new = [u_ref[rows, hs[h]] - ws[h][:CHUNK] for h in heads]
        av = [_dot(att_ref[h, rows, :], v_new[h]) for h in heads]
        kv = [_dot(kdT_ref[c * N_HEADS + h], v_new[h]) for h in heads]
        for h in heads:
            S_ref[h] = S[h] * jnp.exp(gl_ref[c:c + 1, h:h + 1]) + kv[h]
            o_ref[rows, hs[h]] = ws[h][CHUNK:] + av[h]

    for c in range(n_chunks):
        rows = slice(c * CHUNK, (c + 1) * CHUNK)
        for h in heads:
            on = _head_norm_rows(o_ref[rows, hs[h]], ng_ref[:, hs[h]], 1e-6, center=False)
            out_ref[0, rows, hs[h]] = (on * _silu(z_ref[0, rows, hs[h]])).astype(out_ref.dtype)


def _gdn_call(proj, gproj, conv_w, a_log, dt_bias, norm_g, *, col0, gate_col, tb=256):
    B, T, _ = proj.shape
    j0 = col0 // D_GROUP
    an = jnp.zeros((1, LANES), F32).at[0, :N_HEADS].set(-jnp.exp(a_log.astype(F32)))
    dtb = jnp.zeros((1, LANES), F32).at[0, :N_HEADS].set(dt_bias)
    colblk = lambda j: pl.BlockSpec((1, tb, D_GROUP), lambda b, t: (b, t, j0 + j))
    halo = lambda j: pl.BlockSpec((1, 8, D_GROUP), lambda b, t: (b, jnp.maximum(t * (tb // 8) - 1, 0), j0 + j))
    return pl.pallas_call(
        functools.partial(_gdn_kernel, tb=tb),
        grid=(B, T // tb),
        in_specs=[colblk(0), colblk(1), colblk(2), colblk(3),
                  pl.BlockSpec((1, tb, LANES), lambda b, t: (b, t, gate_col // LANES)),
                  halo(0), halo(1), halo(2),
                  pl.BlockSpec((4, 3 * D_GROUP), lambda b, t: (0, 0)),
                  pl.BlockSpec((1, LANES), lambda b, t: (0, 0)),
                  pl.BlockSpec((1, LANES), lambda b, t: (0, 0)),
                  pl.BlockSpec((1, D_GROUP), lambda b, t: (0, 0))],
        out_specs=pl.BlockSpec((1, tb, D_GROUP), lambda b, t: (b, t, 0)),
        out_shape=jax.ShapeDtypeStruct((B, T, D_GROUP), BF16),
        scratch_shapes=[pltpu.VMEM((tb, D_GROUP), F32)] * 7
                       + [pltpu.VMEM((N_HEADS, tb, CHUNK), F32),
                          pltpu.VMEM((tb // CHUNK * N_HEADS, HEAD_DIM, CHUNK), F32),
                          pltpu.VMEM((max(tb // CHUNK, 8), LANES), F32),
                          pltpu.VMEM((N_HEADS, HEAD_DIM, HEAD_DIM), F32)],
        compiler_params=pltpu.CompilerParams(dimension_semantics=("arbitrary", "arbitrary"),
                                             vmem_limit_bytes=VMEM_LIMIT),
        name="gdn",
    )(proj, proj, proj, proj, gproj, proj, proj, proj, conv_w, an, dtb, norm_g.reshape(1, D_GROUP))


N_PAIRS = N_RWKV // 2
RWKV_PAIR_GROUP = 8
RWKV_LOW = 384


def _shift1_rows(src_ref, halo_ref, first, r0, nrows, c0, ncols):
    if r0 == 0:
        hl = jnp.where(first, 0.0, halo_ref[0, 7:8, c0:c0 + ncols])
        return jnp.concatenate([hl, src_ref[0, 0:nrows - 1, c0:c0 + ncols]], axis=0)
    return src_ref[0, r0 - 1:r0 - 1 + nrows, c0:c0 + ncols]


def _rwkv_kernel(r_ref, k_ref, v_ref, l0_ref, l1_ref, l2_ref,
                 rh_ref, kh_ref, vh_ref, l0h_ref, l1h_ref, l2h_ref,
                 mu_ref, mul_ref, w0_ref, w2_ref, a0_ref, a2_ref, g2_ref, kk_ref, ka_ref, rk_ref,
                 lng_ref, lnb_ref, out_ref,
                 gs_ref, bo_ref, y_ref, atrt_ref, avk_ref, yk_ref, inv_ref, arb_ref, btT_ref, kvT_ref, wlT_ref,
                 H_ref, *, tb):
    t = pl.program_id(1)
    first = t == 0

    @pl.when(first)
    def _():
        H_ref[...] = jnp.zeros_like(H_ref)

    ri = lax.broadcasted_iota(jnp.int32, (LANES, LANES), 0)
    ci = lax.broadcasted_iota(jnp.int32, (LANES, LANES), 1)
    same_head = ((ri // RWKV_HEAD) == (ci // RWKV_HEAD)).astype(F32)
    lane1 = lax.broadcasted_iota(jnp.int32, (1, LANES), 1)
    m0 = (lane1 < RWKV_HEAD).astype(F32)
    m1 = 1.0 - m0
    t2 = lax.broadcasted_iota(jnp.int32, (CHUNK, LANES), 0)
    l2 = lax.broadcasted_iota(jnp.int32, (CHUNK, LANES), 1)
    lo = l2 < RWKV_HEAD
    s2 = l2 & (RWKV_HEAD - 1)
    causal2 = t2 >= s2
    strict2 = t2 > s2
    n_chunks = tb // CHUNK
    pairs = range(N_PAIRS)
    ps = [slice(p * LANES, (p + 1) * LANES) for p in pairs]

    def lerp(src, halo, mu, r0, c0, ncols):
        cur = src[0, r0:r0 + CHUNK, c0:c0 + ncols]
        return cur + (_shift1_rows(src, halo, first, r0, CHUNK, c0, ncols) - cur) * mu

    for c in range(n_chunks):
        r0 = c * CHUNK
        rows = slice(r0, r0 + CHUNK)
        wl = lerp(l0_ref, l0h_ref, mul_ref[:, 0:LANES], r0, 0, LANES)
        g1 = lerp(l1_ref, l1h_ref, mul_ref[:, LANES:2 * LANES], r0, 0, LANES)
        g2 = lerp(l2_ref, l2h_ref, mul_ref[:, 2 * LANES:3 * LANES], r0, 0, LANES)
        wl_t = jnp.where(lo, jnp.tanh(wl), 0.0)
        al = jnp.where(lo, 0.0, wl)
        sg1 = _sigmoid(g1)
        sg2 = jnp.where(l2 < 32, _sigmoid(g2), 0.0)
        for g0 in range(0, N_PAIRS, RWKV_PAIR_GROUP):
            grp = range(g0, g0 + RWKV_PAIR_GROUP)
            lw = {p: -math.exp(-0.5) * _sigmoid(w0_ref[:, ps[p]] + _dot(wl_t, w2_ref[:, ps[p]])) for p in grp}
            a = {p: _sigmoid(a0_ref[:, ps[p]] + _dot(al, a2_ref[:, ps[p]])) for p in grp}
            g = {p: _dot(sg1, g2_ref[0:LANES, ps[p]]) + _dot(sg2, g2_ref[LANES:2 * LANES, ps[p]]) for p in grp}
            r = {p: lerp(r_ref, rh_ref, mu_ref[:, ps[p]], r0, p * LANES, LANES) for p in grp}
            k = {p: lerp(k_ref, kh_ref, mu_ref[:, D_GROUP + p * LANES:D_GROUP + (p + 1) * LANES], r0, p * LANES, LANES)
                 for p in grp}
            v = {p: lerp(v_ref, vh_ref, mu_ref[:, 2 * D_GROUP + p * LANES:2 * D_GROUP + (p + 1) * LANES], r0,
                         p * LANES, LANES) for p in grp}
            kk = {p: k[p] * kk_ref[:, ps[p]] for p in grp}
            nrm = {p: jnp.sqrt(_half_lane_sums(kk[p] * kk[p], m0, lo)) for p in grp}
            kk = {p: kk[p] / jnp.maximum(nrm[p], 1e-12) for p in grp}
            k2 = {p: k[p] * (1.0 + (a[p] - 1.0) * ka_ref[:, ps[p]]) for p in grp}
            rk = {p: _half_lane_sums(r[p] * k2[p] * rk_ref[:, ps[p]], m0, lo) for p in grp}
            for p in grp:
                gs_ref[p, rows, :] = g[p]
                bo_ref[p, rows, :] = rk[p] * v[p]
            cs = {p: _cumsum_rows(lw[p]) for p in grp}
            w_inv = {p: jnp.exp(-cs[p]) for p in grp}
            w_end = {p: jnp.exp(cs[p][CHUNK - 1:CHUNK, :]) for p in grp}
            rt = {p: r[p] * jnp.exp(cs[p]) for p in grp}
            at = {p: -kk[p] * jnp.exp(cs[p] - lw[p]) for p in grp}
            bt = {p: kk[p] * a[p] * w_inv[p] for p in grp}
            kt = {p: k2[p] * w_inv[p] for p in grp}
            atrt = {p: jnp.concatenate([at[p], rt[p]], axis=0).astype(BF16) for p in grp}
            pm0 = {p: _dot_nt(atrt[p] * m0.astype(BF16), jnp.concatenate([bt[p], kt[p]], axis=0)) for p in grp}
            pm1 = {p: _dot_nt(atrt[p] * m1.astype(BF16), jnp.concatenate([kt[p], bt[p]], axis=0)) for p in grp}
            n_ab = {p: jnp.where(strict2, jnp.where(lo, pm0[p][:CHUNK], pm1[p][:CHUNK]), 0.0) for p in grp}
            a_rb = {p: jnp.where(causal2, jnp.where(lo, pm0[p][CHUNK:], pm1[p][CHUNK:]), 0.0) for p in grp}
            akrk = {p: jnp.concatenate([jnp.where(strict2, jnp.where(lo, pm1[p][:CHUNK], pm0[p][:CHUNK]), 0.0),
                                        jnp.where(causal2, jnp.where(lo, pm1[p][CHUNK:], pm0[p][CHUNK:]), 0.0)], axis=0)
                    for p in grp}
            vk = {p: _dot(akrk[p], _blockdiag2(v[p], m1, m0)) for p in grp}
            inv = dict(zip(grp, _inv_unit_lower_packed([n_ab[p] for p in grp], m0, m1)))
            kv = {p: _dot((kt[p] * w_end[p]).T, v[p]) * same_head for p in grp}
            for p in grp:
                i = c * N_PAIRS + p
                atrt_ref[i] = atrt[p]
                avk_ref[i] = vk[p][:CHUNK]
                yk_ref[i] = vk[p][CHUNK:]
                btT_ref[i] = (bt[p] * w_end[p]).T.astype(BF16)
                kvT_ref[i] = kv[p]
                wlT_ref[i] = jnp.broadcast_to(w_end[p], (LANES, LANES)).T
                inv_ref[i] = inv[p].astype(BF16)
                arb_ref[i] = a_rb[p].astype(BF16)

    for c in range(n_chunks):
        rows = slice(c * CHUNK, (c + 1) * CHUNK)
        it = [c * N_PAIRS + p for p in pairs]
        H = [H_ref[p] for p in pairs]
        xy0 = [_dot(atrt_ref[it[p]], H[p]) for p in pairs]
        x = [xy0[p][:CHUNK] + avk_ref[it[p]] for p in pairs]
        u = [_dot(inv_ref[it[p]], _blockdiag2(x[p], m0, m1)) for p in pairs]
        yb = [_dot(arb_ref[it[p]], _blockdiag2(u[p], m0, m1)) for p in pairs]
        bu = [_dot(btT_ref[it[p]], u[p]) for p in pairs]
        for p in pairs:
            y_ref[p, rows, :] = xy0[p][CHUNK:] + yk_ref[it[p]] + yb[p]
            H_ref[p] = H[p] * wlT_ref[it[p]] + bu[p] * same_head + kvT_ref[it[p]]

    inv_n = 1.0 / RWKV_HEAD
    for c in range(n_chunks):
        rows = slice(c * CHUNK, (c + 1) * CHUNK)
        y = [y_ref[p, rows, :] for p in pairs]
        yc = [y[p] - _half_lane_sums(y[p], m0, lo) * inv_n for p in pairs]
        var = [_half_lane_sums(yc[p] * yc[p], m0, lo) * inv_n for p in pairs]
        for p in pairs:
            yn = yc[p] * lax.rsqrt(var[p] + RWKV_LN_EPS) * lng_ref[:, ps[p]] + lnb_ref[:, ps[p]]
            out_ref[0, rows, ps[p]] = ((yn + bo_ref[p, rows, :]) * gs_ref[p, rows, :]).astype(out_ref.dtype)


def _rwkv_call(proj, mu, w0, w2, a0, a2, g2, k_k, k_a, r_k, ln_g, ln_b, *, col0, tb=256):
    B, T, _ = proj.shape
    j0 = col0 // D_GROUP
    l0 = (col0 + 3 * D_GROUP) // LANES
    row = lambda a: a.reshape(1, -1).astype(F32)
    mul = jnp.zeros((1, RWKV_LOW), F32).at[0, :288].set(mu[3 * D_GROUP:])
    w2p = jnp.zeros((LANES, D_GROUP), F32).at[:64].set(w2)
    a2p = jnp.zeros((LANES, D_GROUP), F32).at[64:].set(a2)
    g2p = jnp.zeros((2 * LANES, D_GROUP), F32).at[:160].set(g2)
    colblk = lambda j: pl.BlockSpec((1, tb, D_GROUP), lambda b, t: (b, t, j0 + j))
    lowblk = lambda j: pl.BlockSpec((1, tb, LANES), lambda b, t: (b, t, l0 + j))
    hrow = lambda t: jnp.maximum(t * (tb // 8) - 1, 0)
    halo = lambda j: pl.BlockSpec((1, 8, D_GROUP), lambda b, t: (b, hrow(t), j0 + j))
    lowhalo = lambda j: pl.BlockSpec((1, 8, LANES), lambda b, t: (b, hrow(t), l0 + j))
    full = lambda a: pl.BlockSpec(a.shape, lambda b, t: (0,) * a.ndim)
    params = [row(mu[:3 * D_GROUP]), mul, row(w0), w2p, row(a0), a2p, g2p, row(k_k), row(k_a), row(r_k),
              row(ln_g), row(ln_b)]
    big = pltpu.VMEM((N_PAIRS, tb, LANES), F32)
    n_items = tb // CHUNK * N_PAIRS
    return pl.pallas_call(
        functools.partial(_rwkv_kernel, tb=tb),
        grid=(B, T // tb),
        in_specs=[colblk(0), colblk(1), colblk(2), lowblk(0), lowblk(1), lowblk(2),
                  halo(0), halo(1), halo(2), lowhalo(0), lowhalo(1), lowhalo(2)] + [full(a) for a in params],
        out_specs=pl.BlockSpec((1, tb, D_GROUP), lambda b, t: (b, t, 0)),
        out_shape=jax.ShapeDtypeStruct((B, T, D_GROUP), BF16),
        scratch_shapes=[big] * 3 + [
            pltpu.VMEM((n_items, 2 * CHUNK, LANES), BF16),
            pltpu.VMEM((n_items, CHUNK, LANES), F32),
            pltpu.VMEM((n_items, CHUNK, LANES), F32),
            pltpu.VMEM((n_items, CHUNK, LANES), BF16),
            pltpu.VMEM((n_items, CHUNK, LANES), BF16),
            pltpu.VMEM((n_items, LANES, CHUNK), BF16),
            pltpu.VMEM((n_items, LANES, LANES), F32),
            pltpu.VMEM((n_items, LANES, LANES), F32),
            pltpu.VMEM((N_PAIRS, LANES, LANES), F32)],
        compiler_params=pltpu.CompilerParams(dimension_semantics=("arbitrary", "arbitrary"),
                                             vmem_limit_bytes=VMEM_LIMIT),
        name="rwkv7",
    )(*([proj] * 12), *params)


def _ada_kernel(c_ref, w_ref, b_ref, out_ref):
    sc = _silu(c_ref[...]).astype(BF16)
    out_ref[0] = jnp.dot(sc, w_ref[0].astype(BF16), preferred_element_type=F32) + b_ref[0]


def _ada_call(c, ada_w, ada_b, *, tn=1536):
    B = c.shape[0]
    n_mod = ada_w.shape[0] * ada_w.shape[1]
    w = ada_w.reshape(n_mod, D_MODEL, 3 * D_MODEL)
    b = ada_b.reshape(n_mod, 1, 3 * D_MODEL)
    return pl.pallas_call(
        _ada_kernel,
        grid=(n_mod, 3 * D_MODEL // tn),
        in_specs=[pl.BlockSpec((B, D_MODEL), lambda i, j: (0, 0)),
                  pl.BlockSpec((1, D_MODEL, tn), lambda i, j: (i, 0, j)),
                  pl.BlockSpec((1, 1, tn), lambda i, j: (i, 0, j))],
        out_specs=pl.BlockSpec((1, B, tn), lambda i, j: (i, 0, j)),
        out_shape=jax.ShapeDtypeStruct((n_mod, B, 3 * D_MODEL), F32),
        compiler_params=pltpu.CompilerParams(dimension_semantics=("arbitrary", "arbitrary"),
                                             vmem_limit_bytes=VMEM_LIMIT),
        name="adaln",
    )(c, w, b)


def _mod_spec(i, part, nb):
    return pl.BlockSpec((1, 1, D_MODEL), lambda b, t: (i * nb + b, 0, part))


def _modulate_kernel(x_ref, shift_ref, scale_ref, h_ref):
    h_ref[0] = (x_ref[0] * (1.0 + scale_ref[0]) + shift_ref[0]).astype(h_ref.dtype)


def _modulate_call(x, mods3, i, *, tb=512):
    B, T, _ = x.shape
    blk = pl.BlockSpec((1, tb, D_MODEL), lambda b, t: (b, t, 0))
    return pl.pallas_call(
        _modulate_kernel,
        grid=(B, T // tb),
        in_specs=[blk, _mod_spec(i, 0, B), _mod_spec(i, 1, B)],
        out_specs=blk,
        out_shape=jax.ShapeDtypeStruct(x.shape, BF16),
        compiler_params=pltpu.CompilerParams(dimension_semantics=("arbitrary", "arbitrary"),
                                             vmem_limit_bytes=VMEM_LIMIT),
        name="modulate",
    )(x, mods3, mods3)


LN_ROWS = 16


def _proj_ln_kernel(*refs, n_lhs, nk, tm, with_next):
    lhs = refs[:n_lhs]
    ws = refs[n_lhs:2 * n_lhs]
    x_ref, gate_ref, g_ref, b_ref = refs[2 * n_lhs:2 * n_lhs + 4]
    if with_next:
        shift_ref, scale_ref, xo_ref, h_ref, acc_ref = refs[2 * n_lhs + 4:]
    else:
        xo_ref, acc_ref = refs[2 * n_lhs + 4:]
    k = pl.program_id(2)

    def partial_product():
        part = jnp.dot(lhs[0][0], ws[0][...], preferred_element_type=F32)
        for j in range(1, n_lhs):
            part = part + jnp.dot(lhs[j][0], ws[j][...], preferred_element_type=F32)
        return part

    @pl.when(k == 0)
    def _():
        acc_ref[...] = partial_product()

    @pl.when(k > 0)
    def _():
        acc_ref[...] += partial_product()

    @pl.when(k == nk - 1)
    def _():
        gate1 = 1.0 + gate_ref[0]
        g, b = g_ref[...], b_ref[...]
        if with_next:
            scale1, shift = 1.0 + scale_ref[0], shift_ref[0]

        def rows_body(r, carry):
            rows = pl.ds(pl.multiple_of(r * LN_ROWS, LN_ROWS), LN_ROWS)
            z = ALPHA * x_ref[0, rows, :] + gate1 * acc_ref[rows, :]
            zc = z - jnp.mean(z, axis=-1, keepdims=True)
            var = jnp.mean(zc * zc, axis=-1, keepdims=True)
            xn = zc * lax.rsqrt(var + LN_EPS) * g + b
            xo_ref[0, rows, :] = xn
            if with_next:
                h_ref[0, rows, :] = (xn * scale1 + shift).astype(h_ref.dtype)
            return carry

        lax.fori_loop(0, tm // LN_ROWS, rows_body, 0, unroll=8)


def _proj_ln_call(lhs, ws, x, mods3, i, g, b, *, with_next, tm, tk, vmem_limit=VMEM_LIMIT):
    B, T, D = x.shape
    n_lhs = len(lhs)
    nk = lhs[0].shape[2] // tk
    blk = pl.BlockSpec((1, tm, D), lambda b, t, k: (b, t, 0))
    row = pl.BlockSpec((1, D), lambda b, t, k: (0, 0))
    mod = lambda ii, part: pl.BlockSpec((1, 1, D), lambda b, t, k: (ii * B + b, 0, part))
    wspec = lambda idx, k0: pl.BlockSpec((None, tk, D), lambda b, t, k: (idx, k0 + k, 0))
    in_specs = ([pl.BlockSpec((1, tm, tk), lambda b, t, k: (b, t, k))] * n_lhs
                + [wspec(idx, k0) for _, idx, k0 in ws]
                + [blk, mod(i, 2), row, row])
    args = list(lhs) + [w for w, _, _ in ws] + [x, mods3, g.reshape(1, D), b.reshape(1, D)]
    out_specs = [blk]
    out_shape = [jax.ShapeDtypeStruct(x.shape, F32)]
    if with_next:
        in_specs += [mod(i + 1, 0), mod(i + 1, 1)]
        args += [mods3, mods3]
        out_specs.append(blk)
        out_shape.append(jax.ShapeDtypeStruct(x.shape, BF16))
    return pl.pallas_call(
        functools.partial(_proj_ln_kernel, n_lhs=n_lhs, nk=nk, tm=tm, with_next=with_next),
        grid=(B, T // tm, nk),
        in_specs=in_specs, out_specs=out_specs, out_shape=out_shape,
        scratch_shapes=[pltpu.VMEM((tm, D), F32)],
        compiler_params=pltpu.CompilerParams(dimension_semantics=("arbitrary", "arbitrary", "arbitrary"),
                                             vmem_limit_bytes=vmem_limit),
        name="proj_residual_ln",
    )(*args)


def _matmul_kernel(a_ref, b_ref, o_ref, *scratch, nk):
    if nk == 1:
        o_ref[...] = jnp.dot(a_ref[...], b_ref[...], preferred_element_type=F32).astype(o_ref.dtype)
        return
    (acc_ref,) = scratch
    k = pl.program_id(2)

    @pl.when(k == 0)
    def _():
        acc_ref[...] = jnp.zeros_like(acc_ref)

    acc_ref[...] += jnp.dot(a_ref[...], b_ref[...], preferred_element_type=F32)

    @pl.when(k == nk - 1)
    def _():
        o_ref[...] = acc_ref[...].astype(o_ref.dtype)


def _matmul_call(a, b, *, tm, tn, tk, out_dtype=F32):
    M, K = a.shape
    _, N = b.shape
    nk = K // tk
    return pl.pallas_call(
        functools.partial(_matmul_kernel, nk=nk),
        grid=(N // tn, M // tm, nk),
        in_specs=[pl.BlockSpec((tm, tk), lambda j, i, k: (i, k)),
                  pl.BlockSpec((tk, tn), lambda j, i, k: (k, j))],
        out_specs=pl.BlockSpec((tm, tn), lambda j, i, k: (i, j)),
        out_shape=jax.ShapeDtypeStruct((M, N), out_dtype),
        scratch_shapes=[] if nk == 1 else [pltpu.VMEM((tm, tn), F32)],
        compiler_params=pltpu.CompilerParams(dimension_semantics=("arbitrary", "arbitrary", "arbitrary"),
                                             vmem_limit_bytes=VMEM_LIMIT),
        name="matmul",
    )(a, b)


XPOSE_ROWS = 256


def _matmul_f32wt_kernel(a_ref, wt_ref, o_ref, wb_ref):
    @pl.when(pl.program_id(1) == 0)
    def _():
        for r in range(0, wt_ref.shape[0], XPOSE_ROWS):
            wb_ref[:, r:r + XPOSE_ROWS] = wt_ref[r:r + XPOSE_ROWS, :].T.astype(BF16)

    o_ref[...] = jnp.dot(a_ref[...], wb_ref[...], preferred_element_type=F32)


def _matmul_f32wt_call(a, wt, idx, ncols, *, tm=1024, tn=1024):
    M, K = a.shape
    tm = min(tm, M)
    return pl.pallas_call(
        _matmul_f32wt_kernel,
        grid=(ncols // tn, M // tm),
        in_specs=[pl.BlockSpec((tm, K), lambda j, i: (i, 0)),
                  pl.BlockSpec((None, tn, K), lambda j, i: (idx, j, 0))],
        out_specs=pl.BlockSpec((tm, tn), lambda j, i: (i, j)),
        out_shape=jax.ShapeDtypeStruct((M, ncols), F32),
        scratch_shapes=[pltpu.VMEM((K, tn), BF16)],
        compiler_params=pltpu.CompilerParams(dimension_semantics=("arbitrary", "arbitrary"),
                                             vmem_limit_bytes=VMEM_LIMIT),
        name="matmul_f32wt",
    )(a, wt)


def _regroup_rows_kernel(wt_ref, o_ref, *, segments):
    tc = wt_ref.shape[1]
    pieces, pos = [], 0
    for src, width, dst in segments:
        if dst > pos:
            pieces.append(jnp.zeros((dst - pos, tc), F32))
        pieces.append(wt_ref[src:src + width, :])
        pos = dst + width
    if pos < o_ref.shape[1]:
        pieces.append(jnp.zeros((o_ref.shape[1] - pos, tc), F32))
    stacked = jnp.concatenate(pieces, axis=0)
    for r in range(0, o_ref.shape[1], LANES):
        o_ref[:, r:r + LANES] = stacked[r:r + LANES, :].T.astype(o_ref.dtype)


def _regroup_rows_call(wt, idx, segments, out_cols, *, block_rows, block_index, tc):
    _, _, K = wt.shape
    return pl.pallas_call(
        functools.partial(_regroup_rows_kernel, segments=segments),
        grid=(K // tc,),
        in_specs=[pl.BlockSpec((None, block_rows, tc), lambda c: (idx, block_index, c))],
        out_specs=pl.BlockSpec((tc, out_cols), lambda c: (c, 0)),
        out_shape=jax.ShapeDtypeStruct((K, out_cols), BF16),
        compiler_params=pltpu.CompilerParams(dimension_semantics=("arbitrary",), vmem_limit_bytes=VMEM_LIMIT),
        name="regroup_rows",
    )(wt)


def _swiglu_up_kernel(h_ref, wg_ref, wu_ref, o_ref, wgb_ref, wub_ref):
    @pl.when(pl.program_id(1) == 0)
    def _():
        wgb_ref[...] = wg_ref[...].astype(BF16)
        wub_ref[...] = wu_ref[...].astype(BF16)

    h = h_ref[...]
    g = jnp.dot(h, wgb_ref[...], preferred_element_type=F32)
    u = jnp.dot(h, wub_ref[...], preferred_element_type=F32)
    o_ref[...] = (_silu(g) * u).astype(o_ref.dtype)


def _swiglu_up_call(h, wg, wu, layer, *, tm=1024, tn=512):
    M, K = h.shape
    _, _, N = wg.shape
    tm = min(tm, M)
    wspec = pl.BlockSpec((None, K, tn), lambda j, i: (layer, 0, j))
    return pl.pallas_call(
        _swiglu_up_kernel,
        grid=(N // tn, M // tm),
        in_specs=[pl.BlockSpec((tm, K), lambda j, i: (i, 0)), wspec, wspec],
        out_specs=pl.BlockSpec((tm, tn), lambda j, i: (i, j)),
        out_shape=jax.ShapeDtypeStruct((M, N), BF16),
        scratch_shapes=[pltpu.VMEM((K, tn), BF16), pltpu.VMEM((K, tn), BF16)],
        compiler_params=pltpu.CompilerParams(dimension_semantics=("arbitrary", "arbitrary"),
                                             vmem_limit_bytes=VMEM_LIMIT),
        name="swiglu_up",
    )(h, wg, wu)


A_MAIN = 4 * D_GROUP
B_COLS_PAD = 3 * D_GROUP + RWKV_LOW + LANES
MLSTM_GATE_COL = 3 * D_GROUP + RWKV_LOW
CD_MAIN = 8 * D_GROUP
GDN_COL0 = 4 * D_GROUP


N_GATES = 2 * N_HEADS
B_SRC = A_MAIN + N_GATES
B_LOW = 64 + 64 + 160
B_SEGMENTS = ((B_SRC, 3 * D_GROUP, 0), (B_SRC + 3 * D_GROUP, B_LOW, 3 * D_GROUP), (A_MAIN, N_GATES, MLSTM_GATE_COL))
CD_GATE_SEGMENTS = ((0, N_GATES, 0),)


def kernel(x, c, positions, ada_w, ada_b, ln_g, ln_b, ab_w_in, ab_w_out, mlstm_conv_w, mlstm_gate_b, mlstm_norm_g, rwkv_mu, rwkv_w0, rwkv_w2, rwkv_a0, rwkv_a2, rwkv_g2, rwkv_k_k, rwkv_k_a, rwkv_r_k, rwkv_ln_g, rwkv_ln_b, cd_w_in, cd_w_out, ret_norm_g, gdn_conv_w, gdn_a_log, gdn_dt_bias, gdn_norm_g, ffn_w_gate, ffn_w_up, ffn_w_down):
    B, T, D = x.shape
    M = B * T
    depth = ada_w.shape[0]
    mods = _ada_call(c, ada_w, ada_b)
    mods3 = mods.reshape(2 * depth * B, 1, 3 * D)
    h = _modulate_call(x, mods3, 0)
    w_down = ffn_w_down.astype(BF16)
    for layer in range(depth):
        j = layer // 2
        i_mix, i_ffn = 2 * layer, 2 * layer + 1
        if layer % 2 == 0:
            h2 = h.reshape(M, D)
            wt = jnp.swapaxes(ab_w_in, 1, 2)
            proj_a = _matmul_f32wt_call(h2, wt, j, A_MAIN).reshape(B, T, A_MAIN)
            w_b = _regroup_rows_call(wt, j, B_SEGMENTS, B_COLS_PAD, block_rows=wt.shape[1], block_index=0, tc=256)
            proj_b = _matmul_call(h2, w_b, tm=min(1024, M), tn=B_COLS_PAD // 2, tk=D)
            proj_b = proj_b.reshape(B, T, B_COLS_PAD)
            ya = _mlstm_call(proj_a, proj_b, mlstm_conv_w[j], mlstm_gate_b[j], mlstm_norm_g[j],
                             gate_col=MLSTM_GATE_COL)
            yb = _rwkv_call(proj_b, rwkv_mu[j], rwkv_w0[j], rwkv_w2[j], rwkv_a0[j], rwkv_a2[j], rwkv_g2[j],
                            rwkv_k_k[j], rwkv_k_a[j], rwkv_r_k[j], rwkv_ln_g[j], rwkv_ln_b[j], col0=0)
            w_out = ab_w_out
        else:
            h2 = h.reshape(M, D)
            wt = jnp.swapaxes(cd_w_in, 1, 2)
            proj = _matmul_f32wt_call(h2, wt, j, CD_MAIN).reshape(B, T, CD_MAIN)
            w_gates = _regroup_rows_call(wt, j, CD_GATE_SEGMENTS, LANES, block_rows=N_GATES,
                                         block_index=CD_MAIN // N_GATES, tc=D)
            proj_g = _matmul_call(h2, w_gates, tm=512, tn=LANES, tk=D).reshape(B, T, LANES)
            ya = _ret_call(proj, positions, ret_norm_g[j])
            yb = _gdn_call(proj, proj_g, gdn_conv_w[j], gdn_a_log[j], gdn_dt_bias[j], gdn_norm_g[j],
                           col0=GDN_COL0, gate_col=0)
            w_out = cd_w_out
        w_out = w_out.astype(BF16)
        x, h = _proj_ln_call([ya, yb], [(w_out, j, 0), (w_out, j, 1)], x, mods3, i_mix,
                             ln_g[layer, 0], ln_b[layer, 0], with_next=True, tm=512, tk=D_GROUP)
        act = _swiglu_up_call(h.reshape(M, D), ffn_w_gate, ffn_w_up, layer)
        last = layer == depth - 1
        res = _proj_ln_call([act.reshape(B, T, D_FF)], [(w_down, layer, 0)], x, mods3, i_ffn,
                            ln_g[layer, 1], ln_b[layer, 1], with_next=not last, tm=512, tk=D_FF // 2,
                            vmem_limit=VMEM_LIMIT_DOWN)
        if last:
            (x,) = res
        else:
            x, h = res
    return x
```

```python
import functools
import math

import jax
import jax.numpy as jnp
from jax import lax
from jax.experimental import pallas as pl
from jax.experimental.pallas import tpu as pltpu

F32 = jnp.float32
BF16 = jnp.bfloat16

D_MODEL = 2048
D_GROUP = 1024
HEAD_DIM = 128
N_HEADS = 8
RWKV_HEAD = 64
N_RWKV = 16
CHUNK = 64
D_FF = 5632
DEPTH = 2
ALPHA = (2 * DEPTH) ** 0.25
LN_EPS = 1e-5
RWKV_LN_EPS = 64e-5
ROPE_BASE = 10000.0
RET_GAMMA_BASE = 5.0
LANES = 128
VMEM_LIMIT = 48 * 1024 * 1024
VMEM_LIMIT_DOWN = 58 * 1024 * 1024


def _dot(a, b):
    return jnp.dot(a.astype(BF16), b.astype(BF16), preferred_element_type=F32)


def _dot_nt(a, b):
    return lax.dot_general(a.astype(BF16), b.astype(BF16), (((1,), (1,)), ((), ())),
                           preferred_element_type=F32)


def _split3(x):
    hi = x.astype(BF16)
    r1 = x - hi.astype(F32)
    mid = r1.astype(BF16)
    lo = (r1 - mid.astype(F32)).astype(BF16)
    return hi, mid, lo


def _cumsum_rows(x):
    n = x.shape[1]
    out = jnp.dot(_tri(CHUNK).astype(BF16), jnp.concatenate(_split3(x), axis=1), preferred_element_type=F32)
    return out[:, :n] + out[:, n:2 * n] + out[:, 2 * n:]


def _half_lane_sums(x, m0, lo):
    s0 = jnp.sum(x * m0, axis=-1, keepdims=True)
    s1 = jnp.sum(x * (1.0 - m0), axis=-1, keepdims=True)
    return jnp.where(lo, s0, s1)


def _sigmoid(x):
    return 1.0 / (1.0 + jnp.exp(-x))


def _silu(x):
    return x * _sigmoid(x)


def _log_sigmoid(x):
    return jnp.minimum(x, 0.0) - jnp.log1p(jnp.exp(-jnp.abs(x)))


def _softplus(x):
    return jnp.maximum(x, 0.0) + jnp.log1p(jnp.exp(-jnp.abs(x)))


def _tri(n, strict=False):
    r = lax.broadcasted_iota(jnp.int32, (n, n), 0)
    c = lax.broadcasted_iota(jnp.int32, (n, n), 1)
    return (r > c) if strict else (r >= c)


def _conv_silu_rows(src_ref, halo_ref, w, first, r0, nrows, c0, ncols):
    cur = src_ref[0, r0:r0 + nrows, c0:c0 + ncols]
    acc = w[3:4] * cur
    if r0 == 0:
        hl = jnp.where(first, 0.0, halo_ref[0, :, c0:c0 + ncols])
        ext = jnp.concatenate([hl, cur[0:8]], axis=0)
        for j in range(3):
            head = ext[5 + j:13 + j]
            if nrows > 8:
                rest = src_ref[0, 5 + j:nrows - 3 + j, c0:c0 + ncols]
                sh = jnp.concatenate([head, rest], axis=0)
            else:
                sh = head
            acc = acc + w[j:j + 1] * sh
    else:
        for j in range(3):
            acc = acc + w[j:j + 1] * src_ref[0, r0 - 3 + j:r0 - 3 + j + nrows, c0:c0 + ncols]
    return _silu(acc)


def _head_norm_rows(h, g_row, eps, center=True):
    if center:
        h = h - jnp.mean(h, axis=-1, keepdims=True)
    return h * lax.rsqrt(jnp.mean(h * h, axis=-1, keepdims=True) + eps) * g_row


MLSTM_CHUNK_GROUP = 4


def _cummax_rows(x):
    row = lax.broadcasted_iota(jnp.int32, x.shape, 0)
    d = 1
    while d < x.shape[0]:
        x = jnp.where(row >= d, jnp.maximum(x, pltpu.roll(x, d, 0)), x)
        d *= 2
    return x


def _bcast_head_cols(x, sel, pieces):
    m = x.shape[0]
    parts, rest = [], x
    for _ in range(pieces):
        hi = rest.astype(BF16)
        parts.append(hi)
        rest = rest - hi.astype(F32)
    out = jnp.dot(jnp.concatenate(parts, axis=0), sel, preferred_element_type=F32)
    acc = out[:m]
    for i in range(1, pieces):
        acc = acc + out[i * m:(i + 1) * m]
    return acc


def _mlstm_kernel(q_ref, k_ref, v_ref, o_ref, g_ref, qh_ref, kh_ref, cw_ref, gb_ref, ng_ref,
                  out_ref, qc_ref, kc_ref, sv_ref, rs_ref, hh_ref, b0_ref, cm_ref, kvn_ref, CN_ref, m_ref, *, tb):
    t = pl.program_id(1)
    first = t == 0

    @pl.when(first)
    def _():
        CN_ref[...] = jnp.zeros_like(CN_ref)
        m_ref[...] = jnp.zeros_like(m_ref)

    for c in range(tb // CHUNK):
        for cb in range(D_GROUP // 256):
            cs = cb * 256
            qc_ref[c * CHUNK:(c + 1) * CHUNK, cs:cs + 256] = _conv_silu_rows(
                q_ref, qh_ref, cw_ref[:, cs:cs + 256], first, c * CHUNK, CHUNK, cs, 256)
            kc_ref[c * CHUNK:(c + 1) * CHUNK, cs:cs + 256] = _conv_silu_rows(
                k_ref, kh_ref, cw_ref[:, D_GROUP + cs:D_GROUP + cs + 256], first, c * CHUNK, CHUNK, cs, 256)

    causal = _tri(CHUNK)
    gb = gb_ref[...]
    scale = HEAD_DIM ** -0.5
    n_chunks = tb // CHUNK
    heads = range(N_HEADS)
    hs = [slice(h * HEAD_DIM, (h + 1) * HEAD_DIM) for h in heads]
    head_lane = lax.broadcasted_iota(jnp.int32, (1, LANES), 1) < N_HEADS
    sel = (lax.broadcasted_iota(jnp.int32, (LANES, N_HEADS * LANES), 0)
           == (lax.broadcasted_iota(jnp.int32, (LANES, N_HEADS * LANES), 1) >> 7)).astype(BF16)
    ones = jnp.ones((CHUNK, HEAD_DIM), F32)
    last = slice(CHUNK - 1, CHUNK)

    for c0 in range(0, n_chunks, MLSTM_CHUNK_GROUP):
        chunks = range(c0, c0 + MLSTM_CHUNK_GROUP)
        rows = {c: slice(c * CHUNK, (c + 1) * CHUNK) for c in chunks}
        z = {c: g_ref[0, rows[c], :] + gb for c in chunks}
        b0 = {c: pltpu.roll(_cumsum_rows(_log_sigmoid(z[c])), LANES - N_HEADS, 1) for c in chunks}
        cv = {c: jnp.where(head_lane, z[c] - b0[c], 0.0) for c in chunks}
        cm = {c: _cummax_rows(cv[c]) for c in chunks}
        cT = {c: cv[c].T for c in chunks}
        cmb = {c: _bcast_head_cols(cm[c], sel, 3) for c in chunks}
        e1b = {c: _bcast_head_cols(jnp.where(head_lane, jnp.exp(cv[c] - cm[c][last]), 0.0), sel, 2) for c in chunks}
        for c in chunks:
            b0_ref[rows[c], :] = b0[c]
            cm_ref[rows[c], :] = cm[c]
        items = [(c, h) for c in chunks for h in heads]
        n_items = range(len(items))
        q = [qc_ref[rows[c], hs[h]] for c, h in items]
        k = [kc_ref[rows[c], hs[h]] * scale for c, h in items]
        vo = [jnp.concatenate([v_ref[0, rows[c], hs[h]], ones], axis=1).astype(BF16) for c, h in items]
        qk = [_dot_nt(q[i], k[i]) for i in n_items]
        s = [qk[i] * jnp.where(causal, jnp.exp(jnp.minimum(cT[c][h:h + 1, :] - cmb[c][:, h * LANES:h * LANES + CHUNK],
                                                           0.0)), 0.0) for i, (c, h) in enumerate(items)]
        s_hi = [s[i].astype(BF16) for i in n_items]
        s_lo = [(s[i] - s_hi[i].astype(F32)).astype(BF16) for i in n_items]
        svr = [jnp.dot(jnp.concatenate([s_hi[i], s_lo[i]], axis=0), vo[i], preferred_element_type=F32)
               for i in n_items]
        kvn = [_dot((k[i] * e1b[c][:, hs[h]]).T, vo[i]) for i, (c, h) in enumerate(items)]
        for i, (c, h) in enumerate(items):
            sv_ref[rows[c], hs[h]] = svr[i][:CHUNK, :HEAD_DIM] + svr[i][CHUNK:, :HEAD_DIM]
            rs_ref[rows[c], hs[h]] = svr[i][:CHUNK, HEAD_DIM:] + svr[i][CHUNK:, HEAD_DIM:]
            kvn_ref[c * N_HEADS + h] = kvn[i]

    for c in range(n_chunks):
        rows = slice(c * CHUNK, (c + 1) * CHUNK)
        m = m_ref[0:1, :]
        b0 = b0_ref[rows, :]
        cm = cm_ref[rows, :]
        mx = jnp.maximum(cm, m)
        m_new = jnp.maximum(b0[last] + m, b0[last] + cm[last])
        m_ref[0:1, :] = m_new
        zero = lambda x: jnp.where(head_lane, x, 0.0)
        fib = _bcast_head_cols(zero(jnp.exp(cm - mx)), sel, 2)
        scb = _bcast_head_cols(zero(jnp.exp(m - mx)), sel, 2)
        emtb = _bcast_head_cols(zero(jnp.exp(jnp.minimum(-(b0 + mx), 80.0))), sel, 2)
        dfb = _bcast_head_cols(jnp.concatenate([zero(jnp.exp(b0[last] + m - m_new)),
                                                zero(jnp.exp(b0[last] + cm[last] - m_new)),
                                                jnp.zeros((6, LANES), F32)], axis=0), sel, 2)
        CN = [CN_ref[h] for h in heads]
        qcn = [_dot(qc_ref[rows, hs[h]], CN[h]) for h in heads]
        for h in heads:
            num = fib[:, hs[h]] * sv_ref[rows, hs[h]] + scb[:, hs[h]] * qcn[h][:, :HEAD_DIM]
            den = fib[:, hs[h]] * rs_ref[rows, hs[h]] + scb[:, hs[h]] * qcn[h][:, HEAD_DIM:]
            hh_ref[rows, hs[h]] = num / jnp.maximum(jnp.abs(den), emtb[:, hs[h]])
            dec = jnp.concatenate([dfb[0:1, hs[h]]] * 2, axis=1)
            fkv = jnp.concatenate([dfb[1:2, hs[h]]] * 2, axis=1)
            CN_ref[h] = CN[h] * dec + kvn_ref[c * N_HEADS + h] * fkv

    for c in range(n_chunks):
        rows = slice(c * CHUNK, (c + 1) * CHUNK)
        hn = [_head_norm_rows(hh_ref[rows, hs[h]], ng_ref[:, hs[h]], LN_EPS) for h in heads]
        for h in heads:
            out_ref[0, rows, hs[h]] = (hn[h] * _sigmoid(o_ref[0, rows, hs[h]])).astype(out_ref.dtype)


def _mlstm_call(proj, gproj, conv_w, gate_b, norm_g, *, gate_col, tb=256):
    B, T, _ = proj.shape
    nt = T // tb
    gb = jnp.zeros((1, LANES), F32).at[0, :2 * N_HEADS].set(gate_b)
    ng = norm_g.reshape(1, D_GROUP)
    colblk = lambda j: pl.BlockSpec((1, tb, D_GROUP), lambda b, t: (b, t, j))
    halo = lambda j: pl.BlockSpec((1, 8, D_GROUP), lambda b, t: (b, jnp.maximum(t * (tb // 8) - 1, 0), j))
    return pl.pallas_call(
        functools.partial(_mlstm_kernel, tb=tb),
        grid=(B, nt),
        in_specs=[colblk(0), colblk(1), colblk(2), colblk(3),
                  pl.BlockSpec((1, tb, LANES), lambda b, t: (b, t, gate_col // LANES)),
                  halo(0), halo(1),
                  pl.BlockSpec((4, 2 * D_GROUP), lambda b, t: (0, 0)),
                  pl.BlockSpec((1, LANES), lambda b, t: (0, 0)),
                  pl.BlockSpec((1, D_GROUP), lambda b, t: (0, 0))],
        out_specs=pl.BlockSpec((1, tb, D_GROUP), lambda b, t: (b, t, 0)),
        out_shape=jax.ShapeDtypeStruct((B, T, D_GROUP), BF16),
        scratch_shapes=[pltpu.VMEM((tb, D_GROUP), F32)] * 5
                       + [pltpu.VMEM((tb, LANES), F32)] * 2
                       + [pltpu.VMEM((tb // CHUNK * N_HEADS, HEAD_DIM, 2 * HEAD_DIM), F32),
                          pltpu.VMEM((N_HEADS, HEAD_DIM, 2 * HEAD_DIM), F32),
                          pltpu.VMEM((8, LANES), F32)],
        compiler_params=pltpu.CompilerParams(dimension_semantics=("arbitrary", "arbitrary"),
                                             vmem_limit_bytes=VMEM_LIMIT),
        name="mlstm",
    )(proj, proj, proj, proj, gproj, proj, proj, conv_w, gb, ng)


RET_CHUNK_GROUP = 1


def _ret_kernel(q_ref, k_ref, v_ref, g_ref, pos_ref, inv_ref, ng_ref, out_ref, qr_ref, o_ref, kv_ref, R_ref, *, tb):
    t = pl.program_id(1)

    @pl.when(t == 0)
    def _():
        R_ref[...] = jnp.zeros_like(R_ref)

    n_chunks = tb // CHUNK
    heads = range(N_HEADS)
    hs = [slice(h * HEAD_DIM, (h + 1) * HEAD_DIM) for h in heads]
    lg = [math.log1p(-2.0 ** (-RET_GAMMA_BASE - h)) for h in heads]
    causal = _tri(CHUNK)
    ri = lax.broadcasted_iota(jnp.int32, (CHUNK, CHUNK), 0)
    ci = lax.broadcasted_iota(jnp.int32, (CHUNK, CHUNK), 1)
    rel = (ri - ci).astype(F32)
    tcol = lax.broadcasted_iota(jnp.int32, (CHUNK, 1), 0).astype(F32)
    scale = HEAD_DIM ** -0.5
    lane = lax.broadcasted_iota(jnp.int32, (1, HEAD_DIM), 1)
    sign = jnp.where(lane < HEAD_DIM // 2, -1.0, 1.0)
    dmat = [jnp.where(causal, jnp.exp(rel * lg[h]), 0.0) for h in heads]

    for c0 in range(0, n_chunks, RET_CHUNK_GROUP):
        chunks = range(c0, c0 + RET_CHUNK_GROUP)
        rows = {c: slice(c * CHUNK, (c + 1) * CHUNK) for c in chunks}
        ang = {c: pos_ref[0, rows[c], :].astype(F32) * inv_ref[...] for c in chunks}
        cos2 = {c: jnp.cos(ang[c]) for c in chunks}
        sin2 = {c: jnp.sin(ang[c]) * sign for c in chunks}
        items = [(c, h) for c in chunks for h in heads]
        n_items = range(len(items))
        rot = lambda z, c: z * cos2[c] + pltpu.roll(z, HEAD_DIM // 2, 1) * sin2[c]
        qr = [rot(q_ref[0, rows[c], hs[h]], c) for c, h in items]
        kr = [rot(k_ref[0, rows[c], hs[h]], c) * scale for c, h in items]
        v = [v_ref[0, rows[c], hs[h]] for c, h in items]
        qk = [_dot_nt(qr[i], kr[i]) * dmat[h] for i, (c, h) in enumerate(items)]
        intra = [_dot(qk[i], v[i]) for i in n_items]
        kv = [_dot((kr[i] * jnp.exp((CHUNK - 1.0 - tcol) * lg[h])).T, v[i]) for i, (c, h) in enumerate(items)]
        for i, (c, h) in enumerate(items):
            qr_ref[rows[c], hs[h]] = qr[i]
            o_ref[rows[c], hs[h]] = intra[i]
            kv_ref[c * N_HEADS + h] = kv[i]

    for h in heads:
        R = R_ref[h]
        for c in range(n_chunks):
            inc = kv_ref[c * N_HEADS + h]
            kv_ref[c * N_HEADS + h] = R
            R = R * math.exp(CHUNK * lg[h]) + inc
        R_ref[h] = R
    for c in range(n_chunks):
        rows = slice(c * CHUNK, (c + 1) * CHUNK)
        inter = [_dot(qr_ref[rows, hs[h]], kv_ref[c * N_HEADS + h]) * jnp.exp((tcol + 1.0) * lg[h]) for h in heads]
        on = [_head_norm_rows(o_ref[rows, hs[h]] + inter[h], ng_ref[:, hs[h]], LN_EPS) for h in heads]
        for h in heads:
            out_ref[0, rows, hs[h]] = (on[h] * _silu(g_ref[0, rows, hs[h]])).astype(out_ref.dtype)


def _ret_call(proj, positions, norm_g, *, tb=256):
    B, T, _ = proj.shape
    half = HEAD_DIM // 2
    inv_freq = ROPE_BASE ** (-jnp.arange(half, dtype=F32) / half)
    inv2 = jnp.concatenate([inv_freq, inv_freq]).reshape(1, HEAD_DIM)
    colblk = lambda j: pl.BlockSpec((1, tb, D_GROUP), lambda b, t: (b, t, j))
    big = pltpu.VMEM((tb, D_GROUP), F32)
    return pl.pallas_call(
        functools.partial(_ret_kernel, tb=tb),
        grid=(B, T // tb),
        in_specs=[colblk(0), colblk(1), colblk(2), colblk(3),
                  pl.BlockSpec((1, tb, 1), lambda b, t: (b, t, 0)),
                  pl.BlockSpec((1, HEAD_DIM), lambda b, t: (0, 0)),
                  pl.BlockSpec((1, D_GROUP), lambda b, t: (0, 0))],
        out_specs=pl.BlockSpec((1, tb, D_GROUP), lambda b, t: (b, t, 0)),
        out_shape=jax.ShapeDtypeStruct((B, T, D_GROUP), BF16),
        scratch_shapes=[big, big, pltpu.VMEM((tb // CHUNK * N_HEADS, HEAD_DIM, HEAD_DIM), F32),
                        pltpu.VMEM((N_HEADS, HEAD_DIM, HEAD_DIM), F32)],
        compiler_params=pltpu.CompilerParams(dimension_semantics=("arbitrary", "arbitrary"),
                                             vmem_limit_bytes=VMEM_LIMIT),
        name="retention",
    )(proj, proj, proj, proj, positions.reshape(B, T, 1), inv2, norm_g.reshape(1, D_GROUP))


def _inv_unit_lower(nms):
    n = nms[0].shape[0]
    eye = (lax.broadcasted_iota(jnp.int32, (n, n), 0) == lax.broadcasted_iota(jnp.int32, (n, n), 1)).astype(F32)
    ps = [eye + nm for nm in nms]
    xs = [_dot(nm, nm) for nm in nms]
    for _ in range(int(math.log2(n)) - 2):
        px = [_dot(jnp.concatenate([p, x], axis=0), x) for p, x in zip(ps, xs)]
        ps = [p + y[:n] for p, y in zip(ps, px)]
        xs = [y[n:] for y in px]
    ps = [p + _dot(p, x) for p, x in zip(ps, xs)]
    resid = [eye - p + _dot(nm, p) for p, nm in zip(ps, nms)]
    return [p + _dot(p, r) for p, r in zip(ps, resid)]


def _blockdiag2(x, m0, m1):
    xb = x.astype(BF16)
    return jnp.concatenate([xb * m0.astype(BF16), xb * m1.astype(BF16)], axis=0)


def _inv_unit_lower_packed(nms, m0, m1):
    n = nms[0].shape[0]
    r = lax.broadcasted_iota(jnp.int32, (n, 2 * n), 0)
    c = lax.broadcasted_iota(jnp.int32, (n, 2 * n), 1)
    eye2 = (r == (c & (n - 1))).astype(F32)
    bd = lambda x: _blockdiag2(x, m0, m1)
    ps = [eye2 + nm for nm in nms]
    xs = [_dot(nm, bd(nm)) for nm in nms]
    for _ in range(int(math.log2(n)) - 2):
        px = [_dot(jnp.concatenate([p, x], axis=0), bd(x)) for p, x in zip(ps, xs)]
        ps = [p + y[:n] for p, y in zip(ps, px)]
        xs = [y[n:] for y in px]
    ps = [p + _dot(p, bd(x)) for p, x in zip(ps, xs)]
    resid = [eye2 - p + _dot(nm, bd(p)) for p, nm in zip(ps, nms)]
    return [p + _dot(p, bd(r_)) for p, r_ in zip(ps, resid)]


def _l2norm_rows(z):
    return z * lax.rsqrt(jnp.sum(z * z, axis=-1, keepdims=True) + 1e-6)


GDN_CHUNK_GROUP = 2


def _gdn_kernel(q_ref, k_ref, v_ref, z_ref, g_ref, qh_ref, kh_ref, vh_ref, cw_ref, an_ref, dt_ref, ng_ref,
                out_ref, qc_ref, kc_ref, vc_ref, u_ref, w_ref, qe_ref, o_ref, att_ref, kdT_ref, gl_ref, S_ref,
                *, tb):
    t = pl.program_id(1)
    first = t == 0

    @pl.when(first)
    def _():
        S_ref[...] = jnp.zeros_like(S_ref)

    srcs = ((q_ref, qh_ref, qc_ref), (k_ref, kh_ref, kc_ref), (v_ref, vh_ref, vc_ref))
    for c in range(tb // CHUNK):
        for cb in range(D_GROUP // 256):
            cs = cb * 256
            for i, (src, halo, dst) in enumerate(srcs):
                w = cw_ref[:, i * D_GROUP + cs:i * D_GROUP + cs + 256]
                dst[c * CHUNK:(c + 1) * CHUNK, cs:cs + 256] = _conv_silu_rows(
                    src, halo, w, first, c * CHUNK, CHUNK, cs, 256)

    causal = _tri(CHUNK)
    strict = _tri(CHUNK, strict=True)
    a_neg = an_ref[...]
    dtb = dt_ref[...]
    scale = HEAD_DIM ** -0.5
    n_chunks = tb // CHUNK
    heads = range(N_HEADS)
    hs = [slice(h * HEAD_DIM, (h + 1) * HEAD_DIM) for h in heads]

    for c0 in range(0, n_chunks, GDN_CHUNK_GROUP):
        chunks = range(c0, c0 + GDN_CHUNK_GROUP)
        rows = {c: slice(c * CHUNK, (c + 1) * CHUNK) for c in chunks}
        gz = {c: g_ref[0, rows[c], :] for c in chunks}
        beta = {c: _sigmoid(gz[c]) for c in chunks}
        gc = {c: _cumsum_rows(a_neg * _softplus(gz[c] + dtb)) for c in chunks}
        gcT = {c: gc[c].T for c in chunks}
        for c in chunks:
            gl_ref[c:c + 1, :] = gc[c][CHUNK - 1:CHUNK, :]
        items = [(c, h) for c in chunks for h in heads]
        n_items = range(len(items))
        gc_col = [gc[c][:, h:h + 1] for c, h in items]
        b_col = [beta[c][:, 8 + h:9 + h] for c, h in items]
        gamma = [jnp.where(causal, jnp.exp(gc_col[i] - gcT[c][h:h + 1, :]), 0.0) for i, (c, h) in enumerate(items)]
        q = [_l2norm_rows(qc_ref[rows[c], hs[h]]) * scale for c, h in items]
        k = [_l2norm_rows(kc_ref[rows[c], hs[h]]) for c, h in items]
        kb = [k[i] * b_col[i] for i in n_items]
        eg = [jnp.exp(gc_col[i]) for i in n_items]
        kq = [_dot_nt(jnp.concatenate([kb[i], q[i]], axis=0), k[i]) for i in n_items]
        inv = _inv_unit_lower([-jnp.where(strict, kq[i][:CHUNK] * gamma[i], 0.0) for i in n_items])
        uw = [_dot(inv[i], jnp.concatenate([vc_ref[rows[c], hs[h]] * b_col[i], kb[i] * eg[i]], axis=1))
              for i, (c, h) in enumerate(items)]
        for i, (c, h) in enumerate(items):
            u_ref[rows[c], hs[h]] = uw[i][:, :HEAD_DIM]
            w_ref[rows[c], hs[h]] = uw[i][:, HEAD_DIM:]
            qe_ref[rows[c], hs[h]] = q[i] * eg[i]
            att_ref[h, rows[c], :] = kq[i][CHUNK:] * gamma[i]
            g_last = gc[c][CHUNK - 1:CHUNK, h:h + 1]
            kdT_ref[c * N_HEADS + h] = (k[i] * jnp.exp(g_last - gc_col[i])).T

    for c in range(n_chunks):
        rows = slice(c * CHUNK, (c + 1) * CHUNK)
        S = [S_ref[h] for h in heads]
        ws = [_dot(jnp.concatenate([w_ref[rows, hs[h]], qe_ref[rows, hs[h]]], axis=0), S[h]) for h in heads]
        v_new = [u_ref[rows, hs[h]] - ws[h][:CHUNK] for h in heads]
        av = [_dot(att_ref[h, rows, :], v_new[h]) for h in heads]
        kv = [_dot(kdT_ref[c * N_HEADS + h], v_new[h]) for h in heads]
        for h in heads:
            S_ref[h] = S[h] * jnp.exp(gl_ref[c:c + 1, h:h + 1]) + kv[h]
            o_ref[rows, hs[h]] = ws[h][CHUNK:] + av[h]

    for c in range(n_chunks):
        rows = slice(c * CHUNK, (c + 1) * CHUNK)
        for h in heads:
            on = _head_norm_rows(o_ref[rows, hs[h]], ng_ref[:, hs[h]], 1e-6, center=False)
            out_ref[0, rows, hs[h]] = (on * _silu(z_ref[0, rows, hs[h]])).astype(out_ref.dtype)


def _gdn_call(proj, gproj, conv_w, a_log, dt_bias, norm_g, *, col0, gate_col, tb=256):
    B, T, _ = proj.shape
    j0 = col0 // D_GROUP
    an = jnp.zeros((1, LANES), F32).at[0, :N_HEADS].set(-jnp.exp(a_log.astype(F32)))
    dtb = jnp.zeros((1, LANES), F32).at[0, :N_HEADS].set(dt_bias)
    colblk = lambda j: pl.BlockSpec((1, tb, D_GROUP), lambda b, t: (b, t, j0 + j))
    halo = lambda j: pl.BlockSpec((1, 8, D_GROUP), lambda b, t: (b, jnp.maximum(t * (tb // 8) - 1, 0), j0 + j))
    return pl.pallas_call(
        functools.partial(_gdn_kernel, tb=tb),
        grid=(B, T // tb),
        in_specs=[colblk(0), colblk(1), colblk(2), colblk(3),
                  pl.BlockSpec((1, tb, LANES), lambda b, t: (b, t, gate_col // LANES)),
                  halo(0), halo(1), halo(2),
                  pl.BlockSpec((4, 3 * D_GROUP), lambda b, t: (0, 0)),
                  pl.BlockSpec((1, LANES), lambda b, t: (0, 0)),
                  pl.BlockSpec((1, LANES), lambda b, t: (0, 0)),
                  pl.BlockSpec((1, D_GROUP), lambda b, t: (0, 0))],
        out_specs=pl.BlockSpec((1, tb, D_GROUP), lambda b, t: (b, t, 0)),
        out_shape=jax.ShapeDtypeStruct((B, T, D_GROUP), BF16),
        scratch_shapes=[pltpu.VMEM((tb, D_GROUP), F32)] * 7
                       + [pltpu.VMEM((N_HEADS, tb, CHUNK), F32),
                          pltpu.VMEM((tb // CHUNK * N_HEADS, HEAD_DIM, CHUNK), F32),
                          pltpu.VMEM((max(tb // CHUNK, 8), LANES), F32),
                          pltpu.VMEM((N_HEADS, HEAD_DIM, HEAD_DIM), F32)],
        compiler_params=pltpu.CompilerParams(dimension_semantics=("arbitrary", "arbitrary"),
                                             vmem_limit_bytes=VMEM_LIMIT),
        name="gdn",
    )(proj, proj, proj, proj, gproj, proj, proj, proj, conv_w, an, dtb, norm_g.reshape(1, D_GROUP))


N_PAIRS = N_RWKV // 2
RWKV_PAIR_GROUP = 8
RWKV_LOW = 384


def _shift1_rows(src_ref, halo_ref, first, r0, nrows, c0, ncols):
    if r0 == 0:
        hl = jnp.where(first, 0.0, halo_ref[0, 7:8, c0:c0 + ncols])
        return jnp.concatenate([hl, src_ref[0, 0:nrows - 1, c0:c0 + ncols]], axis=0)
    return src_ref[0, r0 - 1:r0 - 1 + nrows, c0:c0 + ncols]


def _rwkv_kernel(r_ref, k_ref, v_ref, l0_ref, l1_ref, l2_ref,
                 rh_ref, kh_ref, vh_ref, l0h_ref, l1h_ref, l2h_ref,
                 mu_ref, mul_ref, w0_ref, w2_ref, a0_ref, a2_ref, g2_ref, kk_ref, ka_ref, rk_ref,
                 lng_ref, lnb_ref, out_ref,
                 gs_ref, bo_ref, y_ref, atrt_ref, avk_ref, yk_ref, inv_ref, arb_ref, btT_ref, kvT_ref, wlT_ref,
                 H_ref, *, tb):
    t = pl.program_id(1)
    first = t == 0

    @pl.when(first)
    def _():
        H_ref[...] = jnp.zeros_like(H_ref)

    ri = lax.broadcasted_iota(jnp.int32, (LANES, LANES), 0)
    ci = lax.broadcasted_iota(jnp.int32, (LANES, LANES), 1)
    same_head = ((ri // RWKV_HEAD) == (ci // RWKV_HEAD)).astype(F32)
    lane1 = lax.broadcasted_iota(jnp.int32, (1, LANES), 1)
    m0 = (lane1 < RWKV_HEAD).astype(F32)
    m1 = 1.0 - m0
    t2 = lax.broadcasted_iota(jnp.int32, (CHUNK, LANES), 0)
    l2 = lax.broadcasted_iota(jnp.int32, (CHUNK, LANES), 1)
    lo = l2 < RWKV_HEAD
    s2 = l2 & (RWKV_HEAD - 1)
    causal2 = t2 >= s2
    strict2 = t2 > s2
    n_chunks = tb // CHUNK
    pairs = range(N_PAIRS)
    ps = [slice(p * LANES, (p + 1) * LANES) for p in pairs]

    def lerp(src, halo, mu, r0, c0, ncols):
        cur = src[0, r0:r0 + CHUNK, c0:c0 + ncols]
        return cur + (_shift1_rows(src, halo, first, r0, CHUNK, c0, ncols) - cur) * mu

    for c in range(n_chunks):
        r0 = c * CHUNK
        rows = slice(r0, r0 + CHUNK)
        wl = lerp(l0_ref, l0h_ref, mul_ref[:, 0:LANES], r0, 0, LANES)
        g1 = lerp(l1_ref, l1h_ref, mul_ref[:, LANES:2 * LANES], r0, 0, LANES)
        g2 = lerp(l2_ref, l2h_ref, mul_ref[:, 2 * LANES:3 * LANES], r0, 0, LANES)
        wl_t = jnp.where(lo, jnp.tanh(wl), 0.0)
        al = jnp.where(lo, 0.0, wl)
        sg1 = _sigmoid(g1)
        sg2 = jnp.where(l2 < 32, _sigmoid(g2), 0.0)
        for g0 in range(0, N_PAIRS, RWKV_PAIR_GROUP):
            grp = range(g0, g0 + RWKV_PAIR_GROUP)
            lw = {p: -math.exp(-0.5) * _sigmoid(w0_ref[:, ps[p]] + _dot(wl_t, w2_ref[:, ps[p]])) for p in grp}
            a = {p: _sigmoid(a0_ref[:, ps[p]] + _dot(al, a2_ref[:, ps[p]])) for p in grp}
            g = {p: _dot(sg1, g2_ref[0:LANES, ps[p]]) + _dot(sg2, g2_ref[LANES:2 * LANES, ps[p]]) for p in grp}
            r = {p: lerp(r_ref, rh_ref, mu_ref[:, ps[p]], r0, p * LANES, LANES) for p in grp}
            k = {p: lerp(k_ref, kh_ref, mu_ref[:, D_GROUP + p * LANES:D_GROUP + (p + 1) * LANES], r0, p * LANES, LANES)
                 for p in grp}
            v = {p: lerp(v_ref, vh_ref, mu_ref[:, 2 * D_GROUP + p * LANES:2 * D_GROUP + (p + 1) * LANES], r0,
                         p * LANES, LANES) for p in grp}
            kk = {p: k[p] * kk_ref[:, ps[p]] for p in grp}
            nrm = {p: jnp.sqrt(_half_lane_sums(kk[p] * kk[p], m0, lo)) for p in grp}
            kk = {p: kk[p] / jnp.maximum(nrm[p], 1e-12) for p in grp}
            k2 = {p: k[p] * (1.0 + (a[p] - 1.0) * ka_ref[:, ps[p]]) for p in grp}
            rk = {p: _half_lane_sums(r[p] * k2[p] * rk_ref[:, ps[p]], m0, lo) for p in grp}
            for p in grp:
                gs_ref[p, rows, :] = g[p]
                bo_ref[p, rows, :] = rk[p] * v[p]
            cs = {p: _cumsum_rows(lw[p]) for p in grp}
            w_inv = {p: jnp.exp(-cs[p]) for p in grp}
            w_end = {p: jnp.exp(cs[p][CHUNK - 1:CHUNK, :]) for p in grp}
            rt = {p: r[p] * jnp.exp(cs[p]) for p in grp}
            at = {p: -kk[p] * jnp.exp(cs[p] - lw[p]) for p in grp}
            bt = {p: kk[p] * a[p] * w_inv[p] for p in grp}
            kt = {p: k2[p] * w_inv[p] for p in grp}
            atrt = {p: jnp.concatenate([at[p], rt[p]], axis=0).astype(BF16) for p in grp}
            pm0 = {p: _dot_nt(atrt[p] * m0.astype(BF16), jnp.concatenate([bt[p], kt[p]], axis=0)) for p in grp}
            pm1 = {p: _dot_nt(atrt[p] * m1.astype(BF16), jnp.concatenate([kt[p], bt[p]], axis=0)) for p in grp}
            n_ab = {p: jnp.where(strict2, jnp.where(lo, pm0[p][:CHUNK], pm1[p][:CHUNK]), 0.0) for p in grp}
            a_rb = {p: jnp.where(causal2, jnp.where(lo, pm0[p][CHUNK:], pm1[p][CHUNK:]), 0.0) for p in grp}
            akrk = {p: jnp.concatenate([jnp.where(strict2, jnp.where(lo, pm1[p][:CHUNK], pm0[p][:CHUNK]), 0.0),
                                        jnp.where(causal2, jnp.where(lo, pm1[p][CHUNK:], pm0[p][CHUNK:]), 0.0)], axis=0)
                    for p in grp}
            vk = {p: _dot(akrk[p], _blockdiag2(v[p], m1, m0)) for p in grp}
            inv = dict(zip(grp, _inv_unit_lower_packed([n_ab[p] for p in grp], m0, m1)))
            kv = {p: _dot((kt[p] * w_end[p]).T, v[p]) * same_head for p in grp}
            for p in grp:
                i = c * N_PAIRS + p
                atrt_ref[i] = atrt[p]
                avk_ref[i] = vk[p][:CHUNK]
                yk_ref[i] = vk[p][CHUNK:]
                btT_ref[i] = (bt[p] * w_end[p]).T.astype(BF16)
                kvT_ref[i] = kv[p]
                wlT_ref[i] = jnp.broadcast_to(w_end[p], (LANES, LANES)).T
                inv_ref[i] = inv[p].astype(BF16)
                arb_ref[i] = a_rb[p].astype(BF16)

    for c in range(n_chunks):
        rows = slice(c * CHUNK, (c + 1) * CHUNK)
        it = [c * N_PAIRS + p for p in pairs]
        H = [H_ref[p] for p in pairs]
        xy0 = [_dot(atrt_ref[it[p]], H[p]) for p in pairs]
        x = [xy0[p][:CHUNK] + avk_ref[it[p]] for p in pairs]
        u = [_dot(inv_ref[it[p]], _blockdiag2(x[p], m0, m1)) for p in pairs]
        yb = [_dot(arb_ref[it[p]], _blockdiag2(u[p], m0, m1)) for p in pairs]
        bu = [_dot(btT_ref[it[p]], u[p]) for p in pairs]
        for p in pairs:
            y_ref[p, rows, :] = xy0[p][CHUNK:] + yk_ref[it[p]] + yb[p]
            H_ref[p] = H[p] * wlT_ref[it[p]] + bu[p] * same_head + kvT_ref[it[p]]

    inv_n = 1.0 / RWKV_HEAD
    for c in range(n_chunks):
        rows = slice(c * CHUNK, (c + 1) * CHUNK)
        y = [y_ref[p, rows, :] for p in pairs]
        yc = [y[p] - _half_lane_sums(y[p], m0, lo) * inv_n for p in pairs]
        var = [_half_lane_sums(yc[p] * yc[p], m0, lo) * inv_n for p in pairs]
        for p in pairs:
            yn = yc[p] * lax.rsqrt(var[p] + RWKV_LN_EPS) * lng_ref[:, ps[p]] + lnb_ref[:, ps[p]]
            out_ref[0, rows, ps[p]] = ((yn + bo_ref[p, rows, :]) * gs_ref[p, rows, :]).astype(out_ref.dtype)


def _rwkv_call(proj, mu, w0, w2, a0, a2, g2, k_k, k_a, r_k, ln_g, ln_b, *, col0, tb=256):
    B, T, _ = proj.shape
    j0 = col0 // D_GROUP
    l0 = (col0 + 3 * D_GROUP) // LANES
    row = lambda a: a.reshape(1, -1).astype(F32)
    mul = jnp.zeros((1, RWKV_LOW), F32).at[0, :288].set(mu[3 * D_GROUP:])
    w2p = jnp.zeros((LANES, D_GROUP), F32).at[:64].set(w2)
    a2p = jnp.zeros((LANES, D_GROUP), F32).at[64:].set(a2)
    g2p = jnp.zeros((2 * LANES, D_GROUP), F32).at[:160].set(g2)
    colblk = lambda j: pl.BlockSpec((1, tb, D_GROUP), lambda b, t: (b, t, j0 + j))
    lowblk = lambda j: pl.BlockSpec((1, tb, LANES), lambda b, t: (b, t, l0 + j))
    hrow = lambda t: jnp.maximum(t * (tb // 8) - 1, 0)
    halo = lambda j: pl.BlockSpec((1, 8, D_GROUP), lambda b, t: (b, hrow(t), j0 + j))
    lowhalo = lambda j: pl.BlockSpec((1, 8, LANES), lambda b, t: (b, hrow(t), l0 + j))
    full = lambda a: pl.BlockSpec(a.shape, lambda b, t: (0,) * a.ndim)
    params = [row(mu[:3 * D_GROUP]), mul, row(w0), w2p, row(a0), a2p, g2p, row(k_k), row(k_a), row(r_k),
              row(ln_g), row(ln_b)]
    big = pltpu.VMEM((N_PAIRS, tb, LANES), F32)
    n_items = tb // CHUNK * N_PAIRS
    return pl.pallas_call(
        functools.partial(_rwkv_kernel, tb=tb),
        grid=(B, T // tb),
        in_specs=[colblk(0), colblk(1), colblk(2), lowblk(0), lowblk(1), lowblk(2),
                  halo(0), halo(1), halo(2), lowhalo(0), lowhalo(1), lowhalo(2)] + [full(a) for a in params],
        out_specs=pl.BlockSpec((1, tb, D_GROUP), lambda b, t: (b, t, 0)),
        out_shape=jax.ShapeDtypeStruct((B, T, D_GROUP), BF16),
        scratch_shapes=[big] * 3 + [
            pltpu.VMEM((n_items, 2 * CHUNK, LANES), BF16),
            pltpu.VMEM((n_items, CHUNK, LANES), F32),
            pltpu.VMEM((n_items, CHUNK, LANES), F32),
            pltpu.VMEM((n_items, CHUNK, LANES), BF16),
            pltpu.VMEM((n_items, CHUNK, LANES), BF16),
            pltpu.VMEM((n_items, LANES, CHUNK), BF16),
            pltpu.VMEM((n_items, LANES, LANES), F32),
            pltpu.VMEM((n_items, LANES, LANES), F32),
            pltpu.VMEM((N_PAIRS, LANES, LANES), F32)],
        compiler_params=pltpu.CompilerParams(dimension_semantics=("arbitrary", "arbitrary"),
                                             vmem_limit_bytes=VMEM_LIMIT),
        name="rwkv7",
    )(*([proj] * 12), *params)


def _ada_kernel(c_ref, w_ref, b_ref, out_ref):
    sc = _silu(c_ref[...]).astype(BF16)
    out_ref[0] = jnp.dot(sc, w_ref[0].astype(BF16), preferred_element_type=F32) + b_ref[0]


def _ada_call(c, ada_w, ada_b, *, tn=1536):
    B = c.shape[0]
    n_mod = ada_w.shape[0] * ada_w.shape[1]
    w = ada_w.reshape(n_mod, D_MODEL, 3 * D_MODEL)
    b = ada_b.reshape(n_mod, 1, 3 * D_MODEL)
    return pl.pallas_call(
        _ada_kernel,
        grid=(n_mod, 3 * D_MODEL // tn),
        in_specs=[pl.BlockSpec((B, D_MODEL), lambda i, j: (0, 0)),
                  pl.BlockSpec((1, D_MODEL, tn), lambda i, j: (i, 0, j)),
                  pl.BlockSpec((1, 1, tn), lambda i, j: (i, 0, j))],
        out_specs=pl.BlockSpec((1, B, tn), lambda i, j: (i, 0, j)),
        out_shape=jax.ShapeDtypeStruct((n_mod, B, 3 * D_MODEL), F32),
        compiler_params=pltpu.CompilerParams(dimension_semantics=("arbitrary", "arbitrary"),
                                             vmem_limit_bytes=VMEM_LIMIT),
        name="adaln",
    )(c, w, b)


def _mod_spec(i, part, nb):
    return pl.BlockSpec((1, 1, D_MODEL), lambda b, t: (i * nb + b, 0, part))


def _modulate_kernel(x_ref, shift_ref, scale_ref, h_ref):
    h_ref[0] = (x_ref[0] * (1.0 + scale_ref[0]) + shift_ref[0]).astype(h_ref.dtype)


def _modulate_call(x, mods3, i, *, tb=512):
    B, T, _ = x.shape
    blk = pl.BlockSpec((1, tb, D_MODEL), lambda b, t: (b, t, 0))
    return pl.pallas_call(
        _modulate_kernel,
        grid=(B, T // tb),
        in_specs=[blk, _mod_spec(i, 0, B), _mod_spec(i, 1, B)],
        out_specs=blk,
        out_shape=jax.ShapeDtypeStruct(x.shape, BF16),
        compiler_params=pltpu.CompilerParams(dimension_semantics=("arbitrary", "arbitrary"),
                                             vmem_limit_bytes=VMEM_LIMIT),
        name="modulate",
    )(x, mods3, mods3)


LN_ROWS = 16


def _proj_ln_kernel(*refs, n_lhs, nk, tm, with_next):
    lhs = refs[:n_lhs]
    ws = refs[n_lhs:2 * n_lhs]
    x_ref, gate_ref, g_ref, b_ref = refs[2 * n_lhs:2 * n_lhs + 4]
    if with_next:
        shift_ref, scale_ref, xo_ref, h_ref, acc_ref = refs[2 * n_lhs + 4:]
    else:
        xo_ref, acc_ref = refs[2 * n_lhs + 4:]
    k = pl.program_id(2)

    def partial_product():
        part = jnp.dot(lhs[0][0], ws[0][...], preferred_element_type=F32)
        for j in range(1, n_lhs):
            part = part + jnp.dot(lhs[j][0], ws[j][...], preferred_element_type=F32)
        return part

    @pl.when(k == 0)
    def _():
        acc_ref[...] = partial_product()

    @pl.when(k > 0)
    def _():
        acc_ref[...] += partial_product()

    @pl.when(k == nk - 1)
    def _():
        gate1 = 1.0 + gate_ref[0]
        g, b = g_ref[...], b_ref[...]
        if with_next:
            scale1, shift = 1.0 + scale_ref[0], shift_ref[0]

        def rows_body(r, carry):
            rows = pl.ds(pl.multiple_of(r * LN_ROWS, LN_ROWS), LN_ROWS)
            z = ALPHA * x_ref[0, rows, :] + gate1 * acc_ref[rows, :]
            zc = z - jnp.mean(z, axis=-1, keepdims=True)
            var = jnp.mean(zc * zc, axis=-1, keepdims=True)
            xn = zc * lax.rsqrt(var + LN_EPS) * g + b
            xo_ref[0, rows, :] = xn
            if with_next:
                h_ref[0, rows, :] = (xn * scale1 + shift).astype(h_ref.dtype)
            return carry

        lax.fori_loop(0, tm // LN_ROWS, rows_body, 0, unroll=8)


def _proj_ln_call(lhs, ws, x, mods3, i, g, b, *, with_next, tm, tk, vmem_limit=VMEM_LIMIT):
    B, T, D = x.shape
    n_lhs = len(lhs)
    nk = lhs[0].shape[2] // tk
    blk = pl.BlockSpec((1, tm, D), lambda b, t, k: (b, t, 0))
    row = pl.BlockSpec((1, D), lambda b, t, k: (0, 0))
    mod = lambda ii, part: pl.BlockSpec((1, 1, D), lambda b, t, k: (ii * B + b, 0, part))
    wmode = dict(pipeline_mode=pl.Buffered(1)) if nk == 1 else {}
    wspec = lambda idx, k0: pl.BlockSpec((None, tk, D), lambda b, t, k: (idx, k0 + k, 0), **wmode)
    in_specs = ([pl.BlockSpec((1, tm, tk), lambda b, t, k: (b, t, k))] * n_lhs
                + [wspec(idx, k0) for _, idx, k0 in ws]
                + [blk, mod(i, 2), row, row])
    args = list(lhs) + [w for w, _, _ in ws] + [x, mods3, g.reshape(1, D), b.reshape(1, D)]
    out_specs = [blk]
    out_shape = [jax.ShapeDtypeStruct(x.shape, F32)]
    if with_next:
        in_specs += [mod(i + 1, 0), mod(i + 1, 1)]
        args += [mods3, mods3]
        out_specs.append(blk)
        out_shape.append(jax.ShapeDtypeStruct(x.shape, BF16))
    return pl.pallas_call(
        functools.partial(_proj_ln_kernel, n_lhs=n_lhs, nk=nk, tm=tm, with_next=with_next),
        grid=(B, T // tm, nk),
        in_specs=in_specs, out_specs=out_specs, out_shape=out_shape,
        scratch_shapes=[pltpu.VMEM((tm, D), F32)],
        compiler_params=pltpu.CompilerParams(dimension_semantics=("arbitrary", "arbitrary", "arbitrary"),
                                             vmem_limit_bytes=vmem_limit),
        name="proj_residual_ln",
    )(*args)


def _matmul_kernel(a_ref, b_ref, o_ref, *scratch, nk):
    if nk == 1:
        o_ref[...] = jnp.dot(a_ref[...], b_ref[...], preferred_element_type=F32).astype(o_ref.dtype)
        return
    (acc_ref,) = scratch
    k = pl.program_id(2)

    @pl.when(k == 0)
    def _():
        acc_ref[...] = jnp.zeros_like(acc_ref)

    acc_ref[...] += jnp.dot(a_ref[...], b_ref[...], preferred_element_type=F32)

    @pl.when(k == nk - 1)
    def _():
        o_ref[...] = acc_ref[...].astype(o_ref.dtype)


def _matmul_call(a, b, *, tm, tn, tk, out_dtype=F32):
    M, K = a.shape
    _, N = b.shape
    nk = K // tk
    return pl.pallas_call(
        functools.partial(_matmul_kernel, nk=nk),
        grid=(N // tn, M // tm, nk),
        in_specs=[pl.BlockSpec((tm, tk), lambda j, i, k: (i, k)),
                  pl.BlockSpec((tk, tn), lambda j, i, k: (k, j))],
        out_specs=pl.BlockSpec((tm, tn), lambda j, i, k: (i, j)),
        out_shape=jax.ShapeDtypeStruct((M, N), out_dtype),
        scratch_shapes=[] if nk == 1 else [pltpu.VMEM((tm, tn), F32)],
        compiler_params=pltpu.CompilerParams(dimension_semantics=("arbitrary", "arbitrary", "arbitrary"),
                                             vmem_limit_bytes=VMEM_LIMIT),
        name="matmul",
    )(a, b)


XPOSE_ROWS = 256


def _matmul_f32wt_kernel(a_ref, wt_ref, o_ref, wb_ref):
    @pl.when(pl.program_id(1) == 0)
    def _():
        for r in range(0, wt_ref.shape[0], XPOSE_ROWS):
            wb_ref[:, r:r + XPOSE_ROWS] = wt_ref[r:r + XPOSE_ROWS, :].T.astype(BF16)

    o_ref[...] = jnp.dot(a_ref[...], wb_ref[...], preferred_element_type=F32)


def _matmul_f32wt_call(a, wt, idx, ncols, *, tm=1024, tn=1024):
    M, K = a.shape
    tm = min(tm, M)
    return pl.pallas_call(
        _matmul_f32wt_kernel,
        grid=(ncols // tn, M // tm),
        in_specs=[pl.BlockSpec((tm, K), lambda j, i: (i, 0)),
                  pl.BlockSpec((None, tn, K), lambda j, i: (idx, j, 0))],
        out_specs=pl.BlockSpec((tm, tn), lambda j, i: (i, j)),
        out_shape=jax.ShapeDtypeStruct((M, ncols), F32),
        scratch_shapes=[pltpu.VMEM((K, tn), BF16)],
        compiler_params=pltpu.CompilerParams(dimension_semantics=("arbitrary", "arbitrary"),
                                             vmem_limit_bytes=VMEM_LIMIT),
        name="matmul_f32wt",
    )(a, wt)


def _regroup_rows_kernel(wt_ref, o_ref, *, segments):
    tc = wt_ref.shape[1]
    pieces, pos = [], 0
    for src, width, dst in segments:
        if dst > pos:
            pieces.append(jnp.zeros((dst - pos, tc), F32))
        pieces.append(wt_ref[src:src + width, :])
        pos = dst + width
    if pos < o_ref.shape[1]:
        pieces.append(jnp.zeros((o_ref.shape[1] - pos, tc), F32))
    stacked = jnp.concatenate(pieces, axis=0)
    for r in range(0, o_ref.shape[1], LANES):
        o_ref[:, r:r + LANES] = stacked[r:r + LANES, :].T.astype(o_ref.dtype)


def _regroup_rows_call(wt, idx, segments, out_cols, *, block_rows, block_index, tc):
    _, _, K = wt.shape
    return pl.pallas_call(
        functools.partial(_regroup_rows_kernel, segments=segments),
        grid=(K // tc,),
        in_specs=[pl.BlockSpec((None, block_rows, tc), lambda c: (idx, block_index, c))],
        out_specs=pl.BlockSpec((tc, out_cols), lambda c: (c, 0)),
        out_shape=jax.ShapeDtypeStruct((K, out_cols), BF16),
        compiler_params=pltpu.CompilerParams(dimension_semantics=("arbitrary",), vmem_limit_bytes=VMEM_LIMIT),
        name="regroup_rows",
    )(wt)


def _swiglu_up_kernel(h_ref, wg_ref, wu_ref, o_ref, wgb_ref, wub_ref):
    @pl.when(pl.program_id(1) == 0)
    def _():
        wgb_ref[...] = wg_ref[...].astype(BF16)
        wub_ref[...] = wu_ref[...].astype(BF16)

    h = h_ref[...]
    g = jnp.dot(h, wgb_ref[...], preferred_element_type=F32)
    u = jnp.dot(h, wub_ref[...], preferred_element_type=F32)
    o_ref[...] = (_silu(g) * u).astype(o_ref.dtype)


def _swiglu_up_call(h, wg, wu, layer, *, tm=1024, tn=512):
    M, K = h.shape
    _, _, N = wg.shape
    tm = min(tm, M)
    wspec = pl.BlockSpec((None, K, tn), lambda j, i: (layer, 0, j))
    return pl.pallas_call(
        _swiglu_up_kernel,
        grid=(N // tn, M // tm),
        in_specs=[pl.BlockSpec((tm, K), lambda j, i: (i, 0)), wspec, wspec],
        out_specs=pl.BlockSpec((tm, tn), lambda j, i: (i, j)),
        out_shape=jax.ShapeDtypeStruct((M, N), BF16),
        scratch_shapes=[pltpu.VMEM((K, tn), BF16), pltpu.VMEM((K, tn), BF16)],
        compiler_params=pltpu.CompilerParams(dimension_semantics=("arbitrary", "arbitrary"),
                                             vmem_limit_bytes=VMEM_LIMIT),
        name="swiglu_up",
    )(h, wg, wu)


A_MAIN = 4 * D_GROUP
B_COLS_PAD = 3 * D_GROUP + RWKV_LOW + LANES
MLSTM_GATE_COL = 3 * D_GROUP + RWKV_LOW
CD_MAIN = 8 * D_GROUP
GDN_COL0 = 4 * D_GROUP


N_GATES = 2 * N_HEADS
B_SRC = A_MAIN + N_GATES
B_LOW = 64 + 64 + 160
B_SEGMENTS = ((B_SRC, 3 * D_GROUP, 0), (B_SRC + 3 * D_GROUP, B_LOW, 3 * D_GROUP), (A_MAIN, N_GATES, MLSTM_GATE_COL))
CD_GATE_SEGMENTS = ((0, N_GATES, 0),)


def kernel(x, c, positions, ada_w, ada_b, ln_g, ln_b, ab_w_in, ab_w_out, mlstm_conv_w, mlstm_gate_b, mlstm_norm_g, rwkv_mu, rwkv_w0, rwkv_w2, rwkv_a0, rwkv_a2, rwkv_g2, rwkv_k_k, rwkv_k_a, rwkv_r_k, rwkv_ln_g, rwkv_ln_b, cd_w_in, cd_w_out, ret_norm_g, gdn_conv_w, gdn_a_log, gdn_dt_bias, gdn_norm_g, ffn_w_gate, ffn_w_up, ffn_w_down):
    B, T, D = x.shape
    M = B * T
    depth = ada_w.shape[0]
    mods = _ada_call(c, ada_w, ada_b)
    mods3 = mods.reshape(2 * depth * B, 1, 3 * D)
    h = _modulate_call(x, mods3, 0)
    w_down = ffn_w_down.astype(BF16)
    for layer in range(depth):
        j = layer // 2
        i_mix, i_ffn = 2 * layer, 2 * layer + 1
        if layer % 2 == 0:
            h2 = h.reshape(M, D)
            wt = jnp.swapaxes(ab_w_in, 1, 2)
            proj_a = _matmul_f32wt_call(h2, wt, j, A_MAIN).reshape(B, T, A_MAIN)
            w_b = _regroup_rows_call(wt, j, B_SEGMENTS, B_COLS_PAD, block_rows=wt.shape[1], block_index=0, tc=256)
            proj_b = _matmul_call(h2, w_b, tm=min(1024, M), tn=B_COLS_PAD // 2, tk=D)
            proj_b = proj_b.reshape(B, T, B_COLS_PAD)
            ya = _mlstm_call(proj_a, proj_b, mlstm_conv_w[j], mlstm_gate_b[j], mlstm_norm_g[j],
                             gate_col=MLSTM_GATE_COL)
            yb = _rwkv_call(proj_b, rwkv_mu[j], rwkv_w0[j], rwkv_w2[j], rwkv_a0[j], rwkv_a2[j], rwkv_g2[j],
                            rwkv_k_k[j], rwkv_k_a[j], rwkv_r_k[j], rwkv_ln_g[j], rwkv_ln_b[j], col0=0)
            w_out = ab_w_out
        else:
            h2 = h.reshape(M, D)
            wt = jnp.swapaxes(cd_w_in, 1, 2)
            proj = _matmul_f32wt_call(h2, wt, j, CD_MAIN).reshape(B, T, CD_MAIN)
            w_gates = _regroup_rows_call(wt, j, CD_GATE_SEGMENTS, LANES, block_rows=N_GATES,
                                         block_index=CD_MAIN // N_GATES, tc=D)
            proj_g = _matmul_call(h2, w_gates, tm=512, tn=LANES, tk=D).reshape(B, T, LANES)
            ya = _ret_call(proj, positions, ret_norm_g[j])
            yb = _gdn_call(proj, proj_g, gdn_conv_w[j], gdn_a_log[j], gdn_dt_bias[j], gdn_norm_g[j],
                           col0=GDN_COL0, gate_col=0)
            w_out = cd_w_out
        w_out = w_out.astype(BF16)
        x, h = _proj_ln_call([ya, yb], [(w_out, j, 0), (w_out, j, 1)], x, mods3, i_mix,
                             ln_g[layer, 0], ln_b[layer, 0], with_next=True, tm=512, tk=D_GROUP)
        act = _swiglu_up_call(h.reshape(M, D), ffn_w_gate, ffn_w_up, layer)
        last = layer == depth - 1
        res = _proj_ln_call([act.reshape(B, T, D_FF)], [(w_down, layer, 0)], x, mods3, i_ffn,
                            ln_g[layer, 1], ln_b[layer, 1], with_next=not last, tm=256, tk=D_FF)
        if last:
            (x,) = res
        else:
            x, h = res
    return x
```
